```python
import math
import jax, jax.numpy as jnp
from jax import lax
import numpy as np

D_MODEL = 1024
BATCH = 16
SEQ = 2048
DEPTH = 2

N_A_LAYERS = DEPTH // 2
N_B_LAYERS = DEPTH - N_A_LAYERS
S5_GROUP = 16
S5_GROUPS = D_MODEL // S5_GROUP
S5_STATE = 64
SB_HEAD_DIM = 64
SB_HEADS = D_MODEL // SB_HEAD_DIM
D_FF = 4 * D_MODEL
Q_BLOCK = 128
EPS = 1e-6
DT_MIN = 1e-3
DT_MAX = 1e-1

kernel_name = "yoco_s5_stickbreaking_hybrid"


def rms_norm(x, g):
    xf = x.astype(jnp.float32)
    y = xf * lax.rsqrt(jnp.mean(xf * xf, axis=-1, keepdims=True) + EPS)
    return (y * g.astype(jnp.float32)).astype(x.dtype)


def modulate(h, shift, scale):
    return h * (1 + scale[:, None, :]) + shift[:, None, :]


def ada_chunks(c, w, b, n):
    m = jnp.einsum('bd,de->be', jax.nn.silu(c), w) + b
    return jnp.split(m, n, axis=-1)


def _ssm_combine(left, right):
    a1r, a1i, b1r, b1i = left
    a2r, a2i, b2r, b2i = right
    ar = a2r * a1r - a2i * a1i
    ai = a2r * a1i + a2i * a1r
    br = a2r * b1r - a2i * b1i + b2r
    bi = a2r * b1i + a2i * b1r + b2i
    return ar, ai, br, bi


def s5_mixer(u, a_re, a_im, log_dt, b_re, b_im, c_re, c_im, d_skip):
    bsz, seq, _ = u.shape
    f32 = jnp.float32
    uf = u.astype(f32).reshape(bsz, seq, S5_GROUPS, S5_GROUP)
    lam_re = a_re.astype(f32)
    lam_im = a_im.astype(f32)
    dt = jnp.exp(log_dt.astype(f32))[:, None]
    mag = jnp.exp(lam_re * dt)
    ab_re = mag * jnp.cos(lam_im * dt)
    ab_im = mag * jnp.sin(lam_im * dt)
    den = lam_re * lam_re + lam_im * lam_im
    nr = ab_re - 1
    ni = ab_im
    f_re = (nr * lam_re + ni * lam_im) / den
    f_im = (ni * lam_re - nr * lam_im) / den
    br = b_re.astype(f32)
    bi = b_im.astype(f32)
    bb_re = f_re[..., None] * br - f_im[..., None] * bi
    bb_im = f_re[..., None] * bi + f_im[..., None] * br
    bu_re = jnp.einsum('bsgh,gph->bsgp', uf, bb_re)
    bu_im = jnp.einsum('bsgh,gph->bsgp', uf, bb_im)
    a_seq_re = jnp.broadcast_to(ab_re, (1, seq) + ab_re.shape)
    a_seq_im = jnp.broadcast_to(ab_im, (1, seq) + ab_im.shape)
    _, _, st_re, st_im = lax.associative_scan(
        _ssm_combine, (a_seq_re, a_seq_im, bu_re, bu_im), axis=1)
    y = (jnp.einsum('bsgp,ghp->bsgh', st_re, c_re.astype(f32))
         - jnp.einsum('bsgp,ghp->bsgh', st_im, c_im.astype(f32)))
    y = y.reshape(bsz, seq, D_MODEL) + d_skip.astype(f32) * u.astype(f32)
    return y.astype(u.dtype)


def stick_breaking_attention(q, k, v):
    seq = q.shape[2]
    scale = 1.0 / math.sqrt(SB_HEAD_DIM)
    outs = []
    for t0 in range(0, seq, Q_BLOCK):
        t1 = t0 + Q_BLOCK
        qb = q[:, :, t0:t1]
        kb = k[:, :, :t1]
        vb = v[:, :, :t1]
        z = jnp.einsum('bhtd,bhsd->bhts', qb, kb).astype(jnp.float32) * scale
        t_idx = jnp.arange(t0, t1)[:, None]
        s_idx = jnp.arange(t1)[None, :]
        strict = s_idx < t_idx
        log_fail = jnp.where(strict, jax.nn.log_sigmoid(-z), 0.0)
        rev = lax.cumsum(log_fail, axis=3, reverse=True)
        after = jnp.concatenate([rev[..., 1:], jnp.zeros_like(rev[..., :1])], axis=-1)
        w = jnp.where(strict, jnp.exp(jax.nn.log_sigmoid(z) + after), 0.0)
        outs.append(jnp.einsum('bhts,bhsd->bhtd', w.astype(v.dtype), vb))
    return jnp.concatenate(outs, axis=2)


def split_heads(t):
    bsz, seq, _ = t.shape
    return t.reshape(bsz, seq, SB_HEADS, SB_HEAD_DIM)


def _fwd_setup_inputs(seed: int = 0) -> dict:
    key = jax.random.key(seed)
    ks = jax.random.split(key, 32)
    f32 = jnp.float32
    D = D_MODEL
    G, P, H = S5_GROUPS, S5_STATE, S5_GROUP
    nrm = lambda k, shape, std: jax.random.normal(k, shape, f32) * std
    x = jax.random.normal(ks[0], (BATCH, SEQ, D), f32)
    c = jax.random.normal(ks[1], (BATCH, D), f32)
    ada_w = nrm(ks[2], (DEPTH, D, 6 * D), 0.5 * D ** -0.5)
    ada_b = nrm(ks[3], (DEPTH, 6 * D), 0.02)
    mix_norm_g = 1.0 + nrm(ks[4], (DEPTH, D), 0.02)
    mlp_norm_g = 1.0 + nrm(ks[5], (DEPTH, D), 0.02)
    mlp_w1 = nrm(ks[6], (DEPTH, D, D_FF), D ** -0.5)
    mlp_w2 = nrm(ks[7], (DEPTH, D_FF, D), D_FF ** -0.5)
    s5_a_re = -0.5 + nrm(ks[8], (N_A_LAYERS, G, P), 0.01)
    s5_a_im = (jnp.float32(math.pi) * jnp.arange(P, dtype=f32))[None, None, :] + nrm(ks[9], (N_A_LAYERS, G, P), 0.01)
    s5_log_dt = jax.random.uniform(ks[10], (N_A_LAYERS, G), f32, math.log(DT_MIN), math.log(DT_MAX))
    s5_b_re = nrm(ks[11], (N_A_LAYERS, G, P, H), (2 * H) ** -0.5)
    s5_b_im = nrm(ks[12], (N_A_LAYERS, G, P, H), (2 * H) ** -0.5)
    s5_c_re = nrm(ks[13], (N_A_LAYERS, G, H, P), P ** -0.5)
    s5_c_im = nrm(ks[14], (N_A_LAYERS, G, H, P), P ** -0.5)
    s5_d = nrm(ks[15], (N_A_LAYERS, D), 1.0)
    s5_w_glu = nrm(ks[16], (N_A_LAYERS, D, 2 * D), D ** -0.5)
    kv_ada_w = nrm(ks[17], (D, 2 * D), 0.5 * D ** -0.5)
    kv_ada_b = nrm(ks[18], (2 * D,), 0.02)
    kv_norm_g = 1.0 + nrm(ks[19], (D,), 0.02)
    w_kv = nrm(ks[20], (D, 2 * D), D ** -0.5)
    k_norm_g = 1.0 + nrm(ks[21], (SB_HEAD_DIM,), 0.02)
    sb_w_q = nrm(ks[22], (N_B_LAYERS, D, D), D ** -0.5)
    q_norm_g = 1.0 + nrm(ks[23], (N_B_LAYERS, SB_HEAD_DIM), 0.02)
    sb_w_o = nrm(ks[24], (N_B_LAYERS, D, D), D ** -0.5)
    return {"x": x, "c": c, "ada_w": ada_w, "ada_b": ada_b,
            "mix_norm_g": mix_norm_g, "mlp_norm_g": mlp_norm_g,
            "mlp_w1": mlp_w1, "mlp_w2": mlp_w2,
            "s5_a_re": s5_a_re, "s5_a_im": s5_a_im, "s5_log_dt": s5_log_dt,
            "s5_b_re": s5_b_re, "s5_b_im": s5_b_im, "s5_c_re": s5_c_re, "s5_c_im": s5_c_im,
            "s5_d": s5_d, "s5_w_glu": s5_w_glu,
            "kv_ada_w": kv_ada_w, "kv_ada_b": kv_ada_b, "kv_norm_g": kv_norm_g,
            "w_kv": w_kv, "k_norm_g": k_norm_g,
            "sb_w_q": sb_w_q, "q_norm_g": q_norm_g, "sb_w_o": sb_w_o}


def _fwd_reference(x, c, ada_w, ada_b, mix_norm_g, mlp_norm_g, mlp_w1, mlp_w2,
              s5_a_re, s5_a_im, s5_log_dt, s5_b_re, s5_b_im, s5_c_re, s5_c_im,
              s5_d, s5_w_glu, kv_ada_w, kv_ada_b, kv_norm_g, w_kv, k_norm_g,
              sb_w_q, q_norm_g, sb_w_o):
    bsz, seq, _ = x.shape
    k_sh = None
    v_sh = None
    for i in range(DEPTH):
        sh_a, sc_a, g_a, sh_m, sc_m, g_m = ada_chunks(c, ada_w[i], ada_b[i], 6)
        if i < N_A_LAYERS:
            j = i
            h = modulate(rms_norm(x, mix_norm_g[i]), sh_a, sc_a)
            y = s5_mixer(h, s5_a_re[j], s5_a_im[j], s5_log_dt[j], s5_b_re[j], s5_b_im[j],
                         s5_c_re[j], s5_c_im[j], s5_d[j])
            val, gate = jnp.split(jnp.einsum('bsd,de->bse', jax.nn.gelu(y), s5_w_glu[j]), 2, axis=-1)
            mix = val * jax.nn.sigmoid(gate)
        else:
            j = i - N_A_LAYERS
            if j == 0:
                kv_shift, kv_scale = ada_chunks(c, kv_ada_w, kv_ada_b, 2)
                hkv = modulate(rms_norm(x, kv_norm_g), kv_shift, kv_scale)
                k_flat, v_flat = jnp.split(jnp.einsum('bsd,de->bse', hkv, w_kv), 2, axis=-1)
                k_sh = rms_norm(split_heads(k_flat), k_norm_g).transpose(0, 2, 1, 3)
                v_sh = split_heads(v_flat).transpose(0, 2, 1, 3)
            h = modulate(rms_norm(x, mix_norm_g[i]), sh_a, sc_a)
            q = jnp.einsum('bsd,de->bse', h, sb_w_q[j])
            q = rms_norm(split_heads(q), q_norm_g[j]).transpose(0, 2, 1, 3)
            o = stick_breaking_attention(q, k_sh, v_sh)
            o = o.transpose(0, 2, 1, 3).reshape(bsz, seq, D_MODEL)
            mix = jnp.einsum('bsd,de->bse', o, sb_w_o[j])
        x = x + g_a[:, None, :] * mix
        h = modulate(rms_norm(x, mlp_norm_g[i]), sh_m, sc_m)
        ff = jnp.einsum('bsf,fd->bsd', jnp.square(jax.nn.relu(jnp.einsum('bsd,df->bsf', h, mlp_w1[i]))), mlp_w2[i])
        x = x + g_m[:, None, :] * ff
    return x


import jax as _jax
import jax.numpy as _jnp

TWIN_FORMAT = 'train_step'
FWD_PARAMS = ['x', 'c', 'ada_w', 'ada_b', 'mix_norm_g', 'mlp_norm_g', 'mlp_w1', 'mlp_w2', 's5_a_re', 's5_a_im', 's5_log_dt', 's5_b_re', 's5_b_im', 's5_c_re', 's5_c_im', 's5_d', 's5_w_glu', 'kv_ada_w', 'kv_ada_b', 'kv_norm_g', 'w_kv', 'k_norm_g', 'sb_w_q', 'q_norm_g', 'sb_w_o']
TWIN_WEIGHTS = ['ada_w', 'ada_b', 'mix_norm_g', 'mlp_norm_g', 'mlp_w1', 'mlp_w2', 's5_a_re', 's5_a_im', 's5_log_dt', 's5_b_re', 's5_b_im', 's5_c_re', 's5_c_im', 's5_d', 's5_w_glu', 'kv_ada_w', 'kv_ada_b', 'kv_norm_g', 'w_kv', 'k_norm_g', 'sb_w_q', 'q_norm_g', 'sb_w_o']
TWIN_DIFF_INPUT = 'x'
TWIN_INPUTS = ['x', 'c', 'ada_w', 'ada_b', 'mix_norm_g', 'mlp_norm_g', 'mlp_w1', 'mlp_w2', 's5_a_re', 's5_a_im', 's5_log_dt', 's5_b_re', 's5_b_im', 's5_c_re', 's5_c_im', 's5_d', 's5_w_glu', 'kv_ada_w', 'kv_ada_b', 'kv_norm_g', 'w_kv', 'k_norm_g', 'sb_w_q', 'q_norm_g', 'sb_w_o', 'loss_target', 'm_ada_w', 'm_ada_b', 'm_mix_norm_g', 'm_mlp_norm_g', 'm_mlp_w1', 'm_mlp_w2', 'm_s5_a_re', 'm_s5_a_im', 'm_s5_log_dt', 'm_s5_b_re', 'm_s5_b_im', 'm_s5_c_re', 'm_s5_c_im', 'm_s5_d', 'm_s5_w_glu', 'm_kv_ada_w', 'm_kv_ada_b', 'm_kv_norm_g', 'm_w_kv', 'm_k_norm_g', 'm_sb_w_q', 'm_q_norm_g', 'm_sb_w_o', 'v_ada_w', 'v_ada_b', 'v_mix_norm_g', 'v_mlp_norm_g', 'v_mlp_w1', 'v_mlp_w2', 'v_s5_a_re', 'v_s5_a_im', 'v_s5_log_dt', 'v_s5_b_re', 'v_s5_b_im', 'v_s5_c_re', 'v_s5_c_im', 'v_s5_d', 'v_s5_w_glu', 'v_kv_ada_w', 'v_kv_ada_b', 'v_kv_norm_g', 'v_w_kv', 'v_k_norm_g', 'v_sb_w_q', 'v_q_norm_g', 'v_sb_w_o']
TWIN_OUTPUTS = ['loss', 'grad_x', 'grad_ada_w', 'grad_ada_b', 'grad_mix_norm_g', 'grad_mlp_norm_g', 'grad_mlp_w1', 'grad_mlp_w2', 'grad_s5_a_re', 'grad_s5_a_im', 'grad_s5_log_dt', 'grad_s5_b_re', 'grad_s5_b_im', 'grad_s5_c_re', 'grad_s5_c_im', 'grad_s5_d', 'grad_s5_w_glu', 'grad_kv_ada_w', 'grad_kv_ada_b', 'grad_kv_norm_g', 'grad_w_kv', 'grad_k_norm_g', 'grad_sb_w_q', 'grad_q_norm_g', 'grad_sb_w_o', 'delta_ada_w', 'delta_ada_b', 'delta_mix_norm_g', 'delta_mlp_norm_g', 'delta_mlp_w1', 'delta_mlp_w2', 'delta_s5_a_re', 'delta_s5_a_im', 'delta_s5_log_dt', 'delta_s5_b_re', 'delta_s5_b_im', 'delta_s5_c_re', 'delta_s5_c_im', 'delta_s5_d', 'delta_s5_w_glu', 'delta_kv_ada_w', 'delta_kv_ada_b', 'delta_kv_norm_g', 'delta_w_kv', 'delta_k_norm_g', 'delta_sb_w_q', 'delta_q_norm_g', 'delta_sb_w_o', 'new_m_ada_w', 'new_m_ada_b', 'new_m_mix_norm_g', 'new_m_mlp_norm_g', 'new_m_mlp_w1', 'new_m_mlp_w2', 'new_m_s5_a_re', 'new_m_s5_a_im', 'new_m_s5_log_dt', 'new_m_s5_b_re', 'new_m_s5_b_im', 'new_m_s5_c_re', 'new_m_s5_c_im', 'new_m_s5_d', 'new_m_s5_w_glu', 'new_m_kv_ada_w', 'new_m_kv_ada_b', 'new_m_kv_norm_g', 'new_m_w_kv', 'new_m_k_norm_g', 'new_m_sb_w_q', 'new_m_q_norm_g', 'new_m_sb_w_o', 'new_v_ada_w', 'new_v_ada_b', 'new_v_mix_norm_g', 'new_v_mlp_norm_g', 'new_v_mlp_w1', 'new_v_mlp_w2', 'new_v_s5_a_re', 'new_v_s5_a_im', 'new_v_s5_log_dt', 'new_v_s5_b_re', 'new_v_s5_b_im', 'new_v_s5_c_re', 'new_v_s5_c_im', 'new_v_s5_d', 'new_v_s5_w_glu', 'new_v_kv_ada_w', 'new_v_kv_ada_b', 'new_v_kv_norm_g', 'new_v_w_kv', 'new_v_k_norm_g', 'new_v_sb_w_q', 'new_v_q_norm_g', 'new_v_sb_w_o']
TWIN_LEAF_KINDS = {'loss': 'loss', 'grad_x': 'grad_x', 'grad_ada_w': 'grad_w', 'grad_ada_b': 'grad_w', 'grad_mix_norm_g': 'grad_w', 'grad_mlp_norm_g': 'grad_w', 'grad_mlp_w1': 'grad_w', 'grad_mlp_w2': 'grad_w', 'grad_s5_a_re': 'grad_w', 'grad_s5_a_im': 'grad_w', 'grad_s5_log_dt': 'grad_w', 'grad_s5_b_re': 'grad_w', 'grad_s5_b_im': 'grad_w', 'grad_s5_c_re': 'grad_w', 'grad_s5_c_im': 'grad_w', 'grad_s5_d': 'grad_w', 'grad_s5_w_glu': 'grad_w', 'grad_kv_ada_w': 'grad_w', 'grad_kv_ada_b': 'grad_w', 'grad_kv_norm_g': 'grad_w', 'grad_w_kv': 'grad_w', 'grad_k_norm_g': 'grad_w', 'grad_sb_w_q': 'grad_w', 'grad_q_norm_g': 'grad_w', 'grad_sb_w_o': 'grad_w', 'delta_ada_w': 'delta_w', 'delta_ada_b': 'delta_w', 'delta_mix_norm_g': 'delta_w', 'delta_mlp_norm_g': 'delta_w', 'delta_mlp_w1': 'delta_w', 'delta_mlp_w2': 'delta_w', 'delta_s5_a_re': 'delta_w', 'delta_s5_a_im': 'delta_w', 'delta_s5_log_dt': 'delta_w', 'delta_s5_b_re': 'delta_w', 'delta_s5_b_im': 'delta_w', 'delta_s5_c_re': 'delta_w', 'delta_s5_c_im': 'delta_w', 'delta_s5_d': 'delta_w', 'delta_s5_w_glu': 'delta_w', 'delta_kv_ada_w': 'delta_w', 'delta_kv_ada_b': 'delta_w', 'delta_kv_norm_g': 'delta_w', 'delta_w_kv': 'delta_w', 'delta_k_norm_g': 'delta_w', 'delta_sb_w_q': 'delta_w', 'delta_q_norm_g': 'delta_w', 'delta_sb_w_o': 'delta_w', 'new_m_ada_w': 'new_m', 'new_m_ada_b': 'new_m', 'new_m_mix_norm_g': 'new_m', 'new_m_mlp_norm_g': 'new_m', 'new_m_mlp_w1': 'new_m', 'new_m_mlp_w2': 'new_m', 'new_m_s5_a_re': 'new_m', 'new_m_s5_a_im': 'new_m', 'new_m_s5_log_dt': 'new_m', 'new_m_s5_b_re': 'new_m', 'new_m_s5_b_im': 'new_m', 'new_m_s5_c_re': 'new_m', 'new_m_s5_c_im': 'new_m', 'new_m_s5_d': 'new_m', 'new_m_s5_w_glu': 'new_m', 'new_m_kv_ada_w': 'new_m', 'new_m_kv_ada_b': 'new_m', 'new_m_kv_norm_g': 'new_m', 'new_m_w_kv': 'new_m', 'new_m_k_norm_g': 'new_m', 'new_m_sb_w_q': 'new_m', 'new_m_q_norm_g': 'new_m', 'new_m_sb_w_o': 'new_m', 'new_v_ada_w': 'new_v', 'new_v_ada_b': 'new_v', 'new_v_mix_norm_g': 'new_v', 'new_v_mlp_norm_g': 'new_v', 'new_v_mlp_w1': 'new_v', 'new_v_mlp_w2': 'new_v', 'new_v_s5_a_re': 'new_v', 'new_v_s5_a_im': 'new_v', 'new_v_s5_log_dt': 'new_v', 'new_v_s5_b_re': 'new_v', 'new_v_s5_b_im': 'new_v', 'new_v_s5_c_re': 'new_v', 'new_v_s5_c_im': 'new_v', 'new_v_s5_d': 'new_v', 'new_v_s5_w_glu': 'new_v', 'new_v_kv_ada_w': 'new_v', 'new_v_kv_ada_b': 'new_v', 'new_v_kv_norm_g': 'new_v', 'new_v_w_kv': 'new_v', 'new_v_k_norm_g': 'new_v', 'new_v_sb_w_q': 'new_v', 'new_v_q_norm_g': 'new_v', 'new_v_sb_w_o': 'new_v'}


def _forward(args):
    return _fwd_reference(*[args[k] for k in FWD_PARAMS])


def _output_shape():
    out = _jax.eval_shape(lambda: _forward(_fwd_setup_inputs(0)))
    return out.shape, out.dtype

N_MICROBATCH = 1
ADAM_LR = 0.001
ADAM_B1 = 0.9
ADAM_B2 = 0.999
ADAM_EPS = 1e-08
ADAM_WD = 0.01
ADAM_STEP = 10
PER_EXAMPLE_BATCH_AXIS = {'x': 0, 'c': 0, 'loss_target': 0}
SHARED_INPUTS = []
_WEIGHT_DTYPES = {'ada_w': _jnp.float32, 'ada_b': _jnp.float32, 'mix_norm_g': _jnp.float32, 'mlp_norm_g': _jnp.float32, 'mlp_w1': _jnp.float32, 'mlp_w2': _jnp.float32, 's5_a_re': _jnp.float32, 's5_a_im': _jnp.float32, 's5_log_dt': _jnp.float32, 's5_b_re': _jnp.float32, 's5_b_im': _jnp.float32, 's5_c_re': _jnp.float32, 's5_c_im': _jnp.float32, 's5_d': _jnp.float32, 's5_w_glu': _jnp.float32, 'kv_ada_w': _jnp.float32, 'kv_ada_b': _jnp.float32, 'kv_norm_g': _jnp.float32, 'w_kv': _jnp.float32, 'k_norm_g': _jnp.float32, 'sb_w_q': _jnp.float32, 'q_norm_g': _jnp.float32, 'sb_w_o': _jnp.float32}
MOMENT_SCALE = {'ada_w': 3.593635e+00, 'ada_b': 7.465662e+00, 'mix_norm_g': 7.386112e-01, 'mlp_norm_g': 1.255814e+01, 'mlp_w1': 4.714002e-01, 'mlp_w2': 1.818513e+00, 's5_a_re': 2.267814e-02, 's5_a_im': 3.047989e-02, 's5_log_dt': 2.491780e+00, 's5_b_re': 2.104153e-02, 's5_b_im': 1.989458e-02, 's5_c_re': 3.115152e-02, 's5_c_im': 3.034652e-02, 's5_d': 7.773720e-01, 's5_w_glu': 3.626978e-01, 'kv_ada_w': 1.089051e+00, 'kv_ada_b': 1.947804e+00, 'kv_norm_g': 1.662929e+00, 'w_kv': 5.356601e-01, 'k_norm_g': 3.114319e+00, 'sb_w_q': 9.513783e-02, 'q_norm_g': 3.098308e+00, 'sb_w_o': 7.052357e-01}


def _to_microbatches(a, axis):
    t = _jnp.moveaxis(a, axis, 0)
    t = t.reshape((N_MICROBATCH, t.shape[0] // N_MICROBATCH) + t.shape[1:])
    return _jnp.moveaxis(t, 1, axis + 1)


def setup_inputs(seed: int = 0) -> dict:
    inp = _fwd_setup_inputs(seed)
    key = _jax.random.fold_in(_jax.random.key(seed), 7919)
    shape, _ = _output_shape()
    out = dict(inp)
    out["loss_target"] = _jax.random.normal(_jax.random.fold_in(key, 0), shape, _jnp.float32)
    for i, name in enumerate(TWIN_WEIGHTS):
        w = inp[name].astype(_jnp.float32)
        if MOMENT_SCALE is None:
            s = _jnp.sqrt(_jnp.mean(_jnp.square(w)) + 1e-30)
        else:
            s = MOMENT_SCALE[name]
        km, kv = _jax.random.split(_jax.random.fold_in(key, i + 1))
        out[name] = w
        out["m_" + name] = s * _jax.random.normal(km, w.shape, _jnp.float32)
        out["v_" + name] = (s * s) * _jax.random.uniform(kv, w.shape, _jnp.float32, 0.5, 1.5)
    if N_MICROBATCH > 1:
        for name, axis in PER_EXAMPLE_BATCH_AXIS.items():
            out[name] = _to_microbatches(out[name], axis)
    return {'x': out['x'], 'c': out['c'], 'ada_w': out['ada_w'], 'ada_b': out['ada_b'], 'mix_norm_g': out['mix_norm_g'], 'mlp_norm_g': out['mlp_norm_g'], 'mlp_w1': out['mlp_w1'], 'mlp_w2': out['mlp_w2'], 's5_a_re': out['s5_a_re'], 's5_a_im': out['s5_a_im'], 's5_log_dt': out['s5_log_dt'], 's5_b_re': out['s5_b_re'], 's5_b_im': out['s5_b_im'], 's5_c_re': out['s5_c_re'], 's5_c_im': out['s5_c_im'], 's5_d': out['s5_d'], 's5_w_glu': out['s5_w_glu'], 'kv_ada_w': out['kv_ada_w'], 'kv_ada_b': out['kv_ada_b'], 'kv_norm_g': out['kv_norm_g'], 'w_kv': out['w_kv'], 'k_norm_g': out['k_norm_g'], 'sb_w_q': out['sb_w_q'], 'q_norm_g': out['q_norm_g'], 'sb_w_o': out['sb_w_o'], 'loss_target': out['loss_target'], 'm_ada_w': out['m_ada_w'], 'm_ada_b': out['m_ada_b'], 'm_mix_norm_g': out['m_mix_norm_g'], 'm_mlp_norm_g': out['m_mlp_norm_g'], 'm_mlp_w1': out['m_mlp_w1'], 'm_mlp_w2': out['m_mlp_w2'], 'm_s5_a_re': out['m_s5_a_re'], 'm_s5_a_im': out['m_s5_a_im'], 'm_s5_log_dt': out['m_s5_log_dt'], 'm_s5_b_re': out['m_s5_b_re'], 'm_s5_b_im': out['m_s5_b_im'], 'm_s5_c_re': out['m_s5_c_re'], 'm_s5_c_im': out['m_s5_c_im'], 'm_s5_d': out['m_s5_d'], 'm_s5_w_glu': out['m_s5_w_glu'], 'm_kv_ada_w': out['m_kv_ada_w'], 'm_kv_ada_b': out['m_kv_ada_b'], 'm_kv_norm_g': out['m_kv_norm_g'], 'm_w_kv': out['m_w_kv'], 'm_k_norm_g': out['m_k_norm_g'], 'm_sb_w_q': out['m_sb_w_q'], 'm_q_norm_g': out['m_q_norm_g'], 'm_sb_w_o': out['m_sb_w_o'], 'v_ada_w': out['v_ada_w'], 'v_ada_b': out['v_ada_b'], 'v_mix_norm_g': out['v_mix_norm_g'], 'v_mlp_norm_g': out['v_mlp_norm_g'], 'v_mlp_w1': out['v_mlp_w1'], 'v_mlp_w2': out['v_mlp_w2'], 'v_s5_a_re': out['v_s5_a_re'], 'v_s5_a_im': out['v_s5_a_im'], 'v_s5_log_dt': out['v_s5_log_dt'], 'v_s5_b_re': out['v_s5_b_re'], 'v_s5_b_im': out['v_s5_b_im'], 'v_s5_c_re': out['v_s5_c_re'], 'v_s5_c_im': out['v_s5_c_im'], 'v_s5_d': out['v_s5_d'], 'v_s5_w_glu': out['v_s5_w_glu'], 'v_kv_ada_w': out['v_kv_ada_w'], 'v_kv_ada_b': out['v_kv_ada_b'], 'v_kv_norm_g': out['v_kv_norm_g'], 'v_w_kv': out['v_w_kv'], 'v_k_norm_g': out['v_k_norm_g'], 'v_sb_w_q': out['v_sb_w_q'], 'v_q_norm_g': out['v_q_norm_g'], 'v_sb_w_o': out['v_sb_w_o']}


def _loss(weights, diff, rest, loss_target):
    with _jax.named_scope("forward"):
        args = {**rest, TWIN_DIFF_INPUT: diff, **{k: w.astype(_WEIGHT_DTYPES[k]) for k, w in weights.items()}}
        y = _forward(args)
    with _jax.named_scope("loss_head"):
        err = _jnp.square(y.astype(_jnp.float32) - loss_target)
        return 0.5 * _jnp.sum(_jnp.mean(err, axis=-1)) if err.ndim else 0.5 * err


def _adamw(w, g, m, v):
    m = ADAM_B1 * m + (1.0 - ADAM_B1) * g
    v = ADAM_B2 * v + (1.0 - ADAM_B2) * _jnp.square(g)
    m_hat = m / (1.0 - ADAM_B1 ** ADAM_STEP)
    v_hat = v / (1.0 - ADAM_B2 ** ADAM_STEP)
    delta = -ADAM_LR * (m_hat / (_jnp.sqrt(v_hat) + ADAM_EPS) + ADAM_WD * w)
    return delta, m, v


def reference(x, c, ada_w, ada_b, mix_norm_g, mlp_norm_g, mlp_w1, mlp_w2, s5_a_re, s5_a_im, s5_log_dt, s5_b_re, s5_b_im, s5_c_re, s5_c_im, s5_d, s5_w_glu, kv_ada_w, kv_ada_b, kv_norm_g, w_kv, k_norm_g, sb_w_q, q_norm_g, sb_w_o, loss_target, m_ada_w, m_ada_b, m_mix_norm_g, m_mlp_norm_g, m_mlp_w1, m_mlp_w2, m_s5_a_re, m_s5_a_im, m_s5_log_dt, m_s5_b_re, m_s5_b_im, m_s5_c_re, m_s5_c_im, m_s5_d, m_s5_w_glu, m_kv_ada_w, m_kv_ada_b, m_kv_norm_g, m_w_kv, m_k_norm_g, m_sb_w_q, m_q_norm_g, m_sb_w_o, v_ada_w, v_ada_b, v_mix_norm_g, v_mlp_norm_g, v_mlp_w1, v_mlp_w2, v_s5_a_re, v_s5_a_im, v_s5_log_dt, v_s5_b_re, v_s5_b_im, v_s5_c_re, v_s5_c_im, v_s5_d, v_s5_w_glu, v_kv_ada_w, v_kv_ada_b, v_kv_norm_g, v_w_kv, v_k_norm_g, v_sb_w_q, v_q_norm_g, v_sb_w_o):
    given = dict(x=x, c=c, ada_w=ada_w, ada_b=ada_b, mix_norm_g=mix_norm_g, mlp_norm_g=mlp_norm_g, mlp_w1=mlp_w1, mlp_w2=mlp_w2, s5_a_re=s5_a_re, s5_a_im=s5_a_im, s5_log_dt=s5_log_dt, s5_b_re=s5_b_re, s5_b_im=s5_b_im, s5_c_re=s5_c_re, s5_c_im=s5_c_im, s5_d=s5_d, s5_w_glu=s5_w_glu, kv_ada_w=kv_ada_w, kv_ada_b=kv_ada_b, kv_norm_g=kv_norm_g, w_kv=w_kv, k_norm_g=k_norm_g, sb_w_q=sb_w_q, q_norm_g=q_norm_g, sb_w_o=sb_w_o, loss_target=loss_target, m_ada_w=m_ada_w, m_ada_b=m_ada_b, m_mix_norm_g=m_mix_norm_g, m_mlp_norm_g=m_mlp_norm_g, m_mlp_w1=m_mlp_w1, m_mlp_w2=m_mlp_w2, m_s5_a_re=m_s5_a_re, m_s5_a_im=m_s5_a_im, m_s5_log_dt=m_s5_log_dt, m_s5_b_re=m_s5_b_re, m_s5_b_im=m_s5_b_im, m_s5_c_re=m_s5_c_re, m_s5_c_im=m_s5_c_im, m_s5_d=m_s5_d, m_s5_w_glu=m_s5_w_glu, m_kv_ada_w=m_kv_ada_w, m_kv_ada_b=m_kv_ada_b, m_kv_norm_g=m_kv_norm_g, m_w_kv=m_w_kv, m_k_norm_g=m_k_norm_g, m_sb_w_q=m_sb_w_q, m_q_norm_g=m_q_norm_g, m_sb_w_o=m_sb_w_o, v_ada_w=v_ada_w, v_ada_b=v_ada_b, v_mix_norm_g=v_mix_norm_g, v_mlp_norm_g=v_mlp_norm_g, v_mlp_w1=v_mlp_w1, v_mlp_w2=v_mlp_w2, v_s5_a_re=v_s5_a_re, v_s5_a_im=v_s5_a_im, v_s5_log_dt=v_s5_log_dt, v_s5_b_re=v_s5_b_re, v_s5_b_im=v_s5_b_im, v_s5_c_re=v_s5_c_re, v_s5_c_im=v_s5_c_im, v_s5_d=v_s5_d, v_s5_w_glu=v_s5_w_glu, v_kv_ada_w=v_kv_ada_w, v_kv_ada_b=v_kv_ada_b, v_kv_norm_g=v_kv_norm_g, v_w_kv=v_w_kv, v_k_norm_g=v_k_norm_g, v_sb_w_q=v_sb_w_q, v_q_norm_g=v_q_norm_g, v_sb_w_o=v_sb_w_o)
    weights = {n: given[n] for n in TWIN_WEIGHTS}
    shared = {n: given[n] for n in SHARED_INPUTS}
    per_example = {n: given[n] for n in ['x', 'c']}
    grad_fn = _jax.value_and_grad(_loss, argnums=(0, 1))

    def one_microbatch(ex, loss_target):
        ex = dict(ex)
        diff = ex.pop(TWIN_DIFF_INPUT)
        return grad_fn(weights, diff, {**shared, **ex}, loss_target)

    if N_MICROBATCH == 1:
        loss, (grad_w, grad_x) = one_microbatch(per_example, given["loss_target"])
    else:
        def body(carry, xs):
            loss_sum, grad_sum = carry
            l_k, (gw_k, gx_k) = one_microbatch(xs[0], xs[1])
            with _jax.named_scope("update"):
                return (loss_sum + l_k, _jax.tree.map(_jnp.add, grad_sum, gw_k)), gx_k

        init = (_jnp.zeros((), _jnp.float32), _jax.tree.map(_jnp.zeros_like, weights))
        (loss, grad_w), grad_x = _jax.lax.scan(body, init, (per_example, given["loss_target"]))
    with _jax.named_scope("update"):
        delta_w, new_m, new_v = {}, {}, {}
        for n in TWIN_WEIGHTS:
            delta_w[n], new_m[n], new_v[n] = _adamw(weights[n], grad_w[n], given["m_" + n], given["v_" + n])
    return (loss, grad_x, *[grad_w[n] for n in TWIN_WEIGHTS], *[delta_w[n] for n in TWIN_WEIGHTS],
            *[new_m[n] for n in TWIN_WEIGHTS], *[new_v[n] for n in TWIN_WEIGHTS])
```

```python
import functools
import math

import jax
import jax.numpy as jnp
from jax import lax
from jax.experimental import pallas as pl
from jax.experimental.pallas import tpu as pltpu

F32 = jnp.float32
BF16 = jnp.bfloat16
EPS = 1e-6
HEAD_DIM = 64
S5_GROUP = 16
S5_STATE = 64
GROUPS_PER_STEP = 8
U_LANES = GROUPS_PER_STEP * S5_GROUP
ST_LANES = GROUPS_PER_STEP * S5_STATE
SCAN_LANES = 256
VMEM_LIMIT = 56 * 1024 * 1024
ADAM_LR, ADAM_B1, ADAM_B2, ADAM_EPS, ADAM_WD, ADAM_STEP = 0.001, 0.9, 0.999, 1e-08, 0.01, 10
MESH = pl.DeviceIdType.MESH


def _cp(sem):
    return pltpu.CompilerParams(dimension_semantics=sem, vmem_limit_bytes=VMEM_LIMIT)


def _mm(a, b, dims, *, name, out_dtypes=(F32,), epilogue=None, extras=(), tm=512, tn=1024, tk=1024):
    if dims == "nn":
        (M, K), (_, N) = a.shape, b.shape
    elif dims == "nt":
        (M, K), (N, _) = a.shape, b.shape
    else:
        (K, M), (_, N) = a.shape, b.shape
    tm, tn, tk = min(tm, M), min(tn, N), min(tk, K)
    assert M % tm == 0 and N % tn == 0 and K % tk == 0, (M, N, K, tm, tn, tk)
    nk = K // tk
    extras = [e(tm, tn) for e in extras]
    a_spec = pl.BlockSpec((tk, tm), lambda i, j, k: (k, i)) if dims == "tn" else pl.BlockSpec((tm, tk), lambda i, j, k: (i, k))
    b_spec = pl.BlockSpec((tn, tk), lambda i, j, k: (j, k)) if dims == "nt" else pl.BlockSpec((tk, tn), lambda i, j, k: (k, j))
    contract = {"nn": ((1,), (0,)), "nt": ((1,), (1,)), "tn": ((0,), (0,))}[dims]
    n_ex, n_out = len(extras), len(out_dtypes)

    def body(a_ref, b_ref, *rest):
        ex, outs, acc = rest[:n_ex], rest[n_ex:n_ex + n_out], rest[-1]
        k = pl.program_id(2)

        @pl.when(k == 0)
        def _():
            acc[...] = jnp.zeros_like(acc)

        acc[...] += lax.dot_general(a_ref[...].astype(BF16), b_ref[...].astype(BF16), (contract, ((), ())),
                                    preferred_element_type=F32)

        @pl.when(k == nk - 1)
        def _():
            r = acc[...]
            res = epilogue(r, *[e[...] for e in ex]) if epilogue is not None else (r,)
            for o, v in zip(outs, res):
                o[...] = v.astype(o.dtype)

    out = pl.pallas_call(
        body, name=name, grid=(M // tm, N // tn, nk),
        in_specs=[a_spec, b_spec] + [pl.BlockSpec(blk, im) for (_, blk, im) in extras],
        out_specs=[pl.BlockSpec((tm, tn), lambda i, j, k: (i, j)) for _ in out_dtypes],
        out_shape=[jax.ShapeDtypeStruct((M, N), d) for d in out_dtypes],
        scratch_shapes=[pltpu.VMEM((tm, tn), F32)],
        compiler_params=_cp(("parallel", "parallel", "arbitrary")),
    )(a, b, *[e[0] for e in extras])
    return out if n_out > 1 else out[0]


def _mn_extra(arr):
    return lambda tm, tn: (arr, (tm, tn), lambda i, j, k: (i, j))


def _vec_extra(vec, S):
    return lambda tm, tn: (vec, (None, 1, tn), lambda i, j, k: ((i * tm) // S, 0, j))


def _rowwise(fn, rows, vecs=(), consts=(), out_rows=(), out_sums=(), *, n_ex, name, tr=256):
    rows = [r if len(r) == 4 else (*r, 0) for r in rows]
    S = min(r[0].shape[0] for r in rows if r[3] == 0) // n_ex
    tr = math.gcd(tr, S)
    assert S % tr == 0
    nb = S // tr
    in_specs = []
    for (arr, w, cb, roff) in rows:
        assert roff % tr == 0
        in_specs.append(pl.BlockSpec((tr, w), functools.partial(lambda e, i, cb, ro: (e * nb + i + ro, cb), cb=cb, ro=roff // tr)))
    for v in vecs:
        in_specs.append(pl.BlockSpec((None, 1, v.shape[-1]), lambda e, i: (e, 0, 0)))
    for c in consts:
        in_specs.append(pl.BlockSpec((1, c.shape[-1]), lambda e, i: (0, 0)))
    n_in, n_or, n_os = len(in_specs), len(out_rows), len(out_sums)
    out_specs = [pl.BlockSpec((tr, w), lambda e, i: (e * nb + i, 0)) for (w, _) in out_rows]
    out_specs += [pl.BlockSpec((None, 1, w), lambda e, i: (e, 0, 0)) for w in out_sums]
    out_shape = [jax.ShapeDtypeStruct((n_ex * S, w), d) for (w, d) in out_rows]
    out_shape += [jax.ShapeDtypeStruct((n_ex, 1, w), F32) for w in out_sums]

    def body(*refs):
        ins, o_r, o_s = refs[:n_in], refs[n_in:n_in + n_or], refs[n_in + n_or:]
        ro, so = fn(*[r[...] for r in ins])
        for o, v in zip(o_r, ro):
            o[...] = v.astype(o.dtype)
        i = pl.program_id(1)
        for o, v in zip(o_s, so):
            @pl.when(i == 0)
            def _(o=o, v=v):
                o[...] = v

            @pl.when(i > 0)
            def _(o=o, v=v):
                o[...] += v

    outs = pl.pallas_call(
        body, name=name, grid=(n_ex, nb), in_specs=in_specs, out_specs=out_specs, out_shape=out_shape,
        compiler_params=_cp(("parallel", "arbitrary")),
    )(*[r[0] for r in rows], *vecs, *consts)
    return outs[:n_or], outs[n_or:]


def _csum(x):
    return jnp.sum(x, axis=0, keepdims=True)


def _norm_mod_fwd(x, g, sh, sc, *, n_ex, out_dtype, name):
    def fn(xt, sht, sct, gt):
        r = lax.rsqrt(jnp.mean(xt * xt, axis=-1, keepdims=True) + EPS)
        return [(xt * r * gt) * (1.0 + sct) + sht], []
    D = x.shape[1]
    return _rowwise(fn, [(x, D, 0)], [sh, sc], [g], [(D, out_dtype)], [], n_ex=n_ex, name=name)[0][0]


def _norm_mod_bwd(x, dh, dres, g, sc, *, n_ex, name):
    def fn(xt, dht, drt, sct, gt):
        dht = dht.astype(F32)
        r = lax.rsqrt(jnp.mean(xt * xt, axis=-1, keepdims=True) + EPS)
        n = xt * r
        y = n * gt
        dy = dht * (1.0 + sct)
        dn = dy * gt
        dx = r * (dn - n * jnp.mean(dn * n, axis=-1, keepdims=True))
        return [drt + dx], [_csum(dht), _csum(dht * y), _csum(dy * n)]
    D = x.shape[1]
    return _rowwise(fn, [(x, D, 0), (dh, D, 0), (dres, D, 0)], [sc], [g], [(D, F32)], [D, D, D], n_ex=n_ex, name=name)


def _sigmoid(x):
    return 1.0 / (1.0 + jnp.exp(-x))


def _gelu(y):
    return 0.5 * y * (1.0 + jnp.tanh(0.7978845608028654 * (y + 0.044715 * y * y * y)))


def _gelu_grad(y):
    t = jnp.tanh(0.7978845608028654 * (y + 0.044715 * y * y * y))
    return 0.5 * (1.0 + t) + 0.5 * y * (1.0 - t * t) * 0.7978845608028654 * (1.0 + 3 * 0.044715 * y * y)


def _adamw_fn(w, g, m, v):
    m2 = ADAM_B1 * m + (1.0 - ADAM_B1) * g
    v2 = ADAM_B2 * v + (1.0 - ADAM_B2) * (g * g)
    m_hat = m2 / (1.0 - ADAM_B1 ** ADAM_STEP)
    v_hat = v2 / (1.0 - ADAM_B2 ** ADAM_STEP)
    delta = -ADAM_LR * (m_hat / (jnp.sqrt(v_hat) + ADAM_EPS) + ADAM_WD * w)
    return delta, m2, v2


def _adamw2d(w, g, m, v, *, name, g_roff=0, g_cb=0):
    R, W = w.shape

    def fn(wt, gt, mt, vt):
        d, m2, v2 = _adamw_fn(wt, gt, mt, vt)
        return [d, m2, v2, gt], []
    return _rowwise(fn, [(w, W, 0), (g, W, g_cb, g_roff), (m, W, 0), (v, W, 0)], [], [],
                    [(W, F32)] * 4, [], n_ex=1, name=name, tr=256)[0]


def _scan_tiles(re_ref, im_ref, cf, lane0, n_chunks, reverse, extra=None):
    L = SCAN_LANES
    lanes = pl.ds(lane0, L)
    A = [cf[i, :, lanes] for i in range(8)]
    shifts = (7, 6, 4) if reverse else (1, 2, 4)
    edge = 0 if reverse else 7

    def body(c, carry):
        cr, ci = carry[0], carry[1]
        cc = (n_chunks - 1 - c) if reverse else c
        rows = pl.ds(pl.multiple_of(cc * 8, 8), 8)
        xr, xi = re_ref[rows, lanes], im_ref[rows, lanes]
        for idx, sft in enumerate(shifts):
            ar, ai = A[2 * idx], A[2 * idx + 1]
            rr, ri = pltpu.roll(xr, sft, 0), pltpu.roll(xi, sft, 0)
            xr, xi = xr + ar * rr - ai * ri, xi + ar * ri + ai * rr
        pr, pi = A[6], A[7]
        xr, xi = xr + pr * cr - pi * ci, xi + pr * ci + pi * cr
        re_ref[rows, lanes] = xr
        im_ref[rows, lanes] = xi
        out = (jnp.broadcast_to(xr[edge:edge + 1, :], (8, L)), jnp.broadcast_to(xi[edge:edge + 1, :], (8, L)))
        if extra is not None:
            out = out + extra(cc, xr, xi, carry[2:])
        return out

    z = jnp.zeros((8, L), F32)
    init = (z, z) if extra is None else (z, z, z, z)
    return lax.fori_loop(0, n_chunks, body, init)


def _s5_consts(ab_re, ab_im):
    ng = ab_re.shape[0] // GROUPS_PER_STEP
    ar, ai = ab_re.reshape(ng, 1, ST_LANES), ab_im.reshape(ng, 1, ST_LANES)

    def cmul(xr, xi, yr, yi):
        return xr * yr - xi * yi, xr * yi + xi * yr

    def build(ar, ai, reverse):
        pw = [(ar, ai)]
        for _ in range(7):
            pw.append(cmul(*pw[-1], ar, ai))
        row = jnp.arange(8).reshape(1, 8, 1)
        tiles = []
        for k in (1, 2, 4):
            keep = (row <= 7 - k) if reverse else (row >= k)
            tiles += [jnp.where(keep, pw[k - 1][0], 0.0), jnp.where(keep, pw[k - 1][1], 0.0)]
        order = [7 - r for r in range(8)] if reverse else list(range(8))
        tiles += [jnp.concatenate([pw[o][0] for o in order], axis=1), jnp.concatenate([pw[o][1] for o in order], axis=1)]
        return jnp.stack([jnp.broadcast_to(t, (ng, 8, ST_LANES)) for t in tiles], axis=1)

    return build(ar, ai, False), build(ar, -ai, True)


def _s5_blockdiag(bb_re, bb_im, c_re, c_im):
    G = bb_re.shape[0]
    ng = G // GROUPS_PER_STEP
    eye = jnp.eye(GROUPS_PER_STEP, dtype=F32)

    def wb(bb):
        return jnp.einsum("bgph,gk->bghkp", bb.reshape(ng, GROUPS_PER_STEP, S5_STATE, S5_GROUP), eye).reshape(ng, U_LANES, ST_LANES)

    def wc(cc):
        return jnp.einsum("bghp,gk->bkpgh", cc.reshape(ng, GROUPS_PER_STEP, S5_GROUP, S5_STATE), eye).reshape(ng, ST_LANES, U_LANES)

    Wb = jnp.concatenate([wb(bb_re), wb(bb_im)], axis=2).astype(BF16)
    Wc = jnp.concatenate([wc(c_re), -wc(c_im)], axis=1).astype(BF16)
    return Wb, Wc


def _s5_unblock(dWb, dWc):
    ng = dWb.shape[0]
    eye = jnp.eye(GROUPS_PER_STEP, dtype=F32)

    def ub(w):
        return jnp.einsum("bghkp,gk->bgph", w.reshape(ng, GROUPS_PER_STEP, S5_GROUP, GROUPS_PER_STEP, S5_STATE), eye).reshape(-1, S5_STATE, S5_GROUP)

    def uc(w):
        return jnp.einsum("bkpgh,gk->bghp", w.reshape(ng, GROUPS_PER_STEP, S5_STATE, GROUPS_PER_STEP, S5_GROUP), eye).reshape(-1, S5_GROUP, S5_STATE)

    return ub(dWb[:, :, :ST_LANES]), ub(dWb[:, :, ST_LANES:]), uc(dWc[:, :ST_LANES, :]), -uc(dWc[:, ST_LANES:, :])


def _s5_disc(a_re, a_im, log_dt, b_re, b_im):
    dt = jnp.exp(log_dt)[:, None]
    mag = jnp.exp(a_re * dt)
    ab_re = mag * jnp.cos(a_im * dt)
    ab_im = mag * jnp.sin(a_im * dt)
    den = a_re * a_re + a_im * a_im
    nr, ni = ab_re - 1, ab_im
    f_re = (nr * a_re + ni * a_im) / den
    f_im = (ni * a_re - nr * a_im) / den
    bb_re = f_re[..., None] * b_re - f_im[..., None] * b_im
    bb_im = f_re[..., None] * b_im + f_im[..., None] * b_re
    return ab_re, ab_im, bb_re, bb_im


ROW_CHUNK = 512


def _s5_fwd(u, Wb, Wc, cf, d, *, n_ex, name):
    T, D = u.shape
    S = T // n_ex
    ng = D // U_LANES
    rc = min(ROW_CHUNK, S)

    def body(u_ref, wb_ref, wc_ref, cf_ref, d_ref, y_ref, gy_ref, re_s, im_s):
        for r in range(S // rc):
            rows = pl.ds(r * rc, rc)
            bu = jnp.dot(u_ref[rows, :].astype(BF16), wb_ref[...], preferred_element_type=F32)
            re_s[rows, :] = bu[:, :ST_LANES]
            im_s[rows, :] = bu[:, ST_LANES:]
        for l0 in range(0, ST_LANES, SCAN_LANES):
            _scan_tiles(re_s, im_s, cf_ref, l0, S // 8, False)
        for r in range(S // rc):
            rows = pl.ds(r * rc, rc)
            st = jnp.concatenate([re_s[rows, :], im_s[rows, :]], axis=1).astype(BF16)
            y = jnp.dot(st, wc_ref[...], preferred_element_type=F32) + d_ref[...] * u_ref[rows, :]
            y_ref[rows, :] = y
            gy_ref[rows, :] = _gelu(y).astype(BF16)

    return pl.pallas_call(
        body, name=name, grid=(n_ex, ng),
        in_specs=[pl.BlockSpec((S, U_LANES), lambda e, g: (e, g)),
                  pl.BlockSpec((None, U_LANES, 2 * ST_LANES), lambda e, g: (g, 0, 0)),
                  pl.BlockSpec((None, 2 * ST_LANES, U_LANES), lambda e, g: (g, 0, 0)),
                  pl.BlockSpec((None, 8, 8, ST_LANES), lambda e, g: (g, 0, 0, 0)),
                  pl.BlockSpec((1, U_LANES), lambda e, g: (0, g))],
        out_specs=[pl.BlockSpec((S, U_LANES), lambda e, g: (e, g))] * 2,
        out_shape=[jax.ShapeDtypeStruct((T, D), F32), jax.ShapeDtypeStruct((T, D), BF16)],
        scratch_shapes=[pltpu.VMEM((S, ST_LANES), F32)] * 2,
        compiler_params=_cp(("parallel", "parallel")),
    )(u, Wb, Wc, cf, d)


def _s5_bwd(u, y, dgy, Wb, Wc, cf, cr, d, *, n_ex, name):
    T, D = u.shape
    S = T // n_ex
    ng = D // U_LANES
    rc = min(ROW_CHUNK, S)
    nch = S // 8

    def body(u_ref, y_ref, dgy_ref, wb_ref, wc_ref, cf_ref, cr_ref, d_ref,
             du_ref, dwb_ref, dwc_ref, dab_ref, dd_ref, re_s, im_s, gr_s, gi_s, dy_s):
        e = pl.program_id(1)

        @pl.when(e == 0)
        def _():
            dwb_ref[...] = jnp.zeros_like(dwb_ref)
            dwc_ref[...] = jnp.zeros_like(dwc_ref)
            dab_ref[...] = jnp.zeros_like(dab_ref)
            dd_ref[...] = jnp.zeros_like(dd_ref)

        dd = jnp.zeros((1, U_LANES), F32)
        for r in range(S // rc):
            rows = pl.ds(r * rc, rc)
            ut = u_ref[rows, :]
            bu = jnp.dot(ut.astype(BF16), wb_ref[...], preferred_element_type=F32)
            re_s[rows, :] = bu[:, :ST_LANES]
            im_s[rows, :] = bu[:, ST_LANES:]
            dy = dgy_ref[rows, :].astype(F32) * _gelu_grad(y_ref[rows, :])
            dy_s[rows, :] = dy
            dd = dd + _csum(dy * ut)
            go = lax.dot_general(dy.astype(BF16), wc_ref[...], (((1,), (1,)), ((), ())), preferred_element_type=F32)
            gr_s[rows, :] = go[:, :ST_LANES]
            gi_s[rows, :] = go[:, ST_LANES:]
        dd_ref[0:1, :] += dd
        row0 = lax.broadcasted_iota(jnp.int32, (8, SCAN_LANES), 0) == 0
        for l0 in range(0, ST_LANES, SCAN_LANES):
            lanes = pl.ds(l0, SCAN_LANES)
            _scan_tiles(re_s, im_s, cf_ref, l0, nch, False)

            def dab_part(cc, gr, gi, acc, lanes=lanes):
                rows = pl.ds(pl.multiple_of(cc * 8, 8), 8)
                prev = pl.ds(pl.multiple_of(jnp.maximum(cc - 1, 0) * 8, 8), 8)
                live = (cc > 0).astype(F32)
                sr = jnp.where(row0, pltpu.roll(re_s[prev, lanes], 1, 0) * live, pltpu.roll(re_s[rows, lanes], 1, 0))
                si = jnp.where(row0, pltpu.roll(im_s[prev, lanes], 1, 0) * live, pltpu.roll(im_s[rows, lanes], 1, 0))
                return (acc[0] + gr * sr + gi * si, acc[1] + gi * sr - gr * si)

            res = _scan_tiles(gr_s, gi_s, cr_ref, l0, nch, True, extra=dab_part)
            dab_ref[0:1, lanes] += _csum(res[2])
            dab_ref[1:2, lanes] += _csum(res[3])
        for r in range(S // rc):
            rows = pl.ds(r * rc, rc)
            st = jnp.concatenate([re_s[rows, :], im_s[rows, :]], axis=1).astype(BF16)
            g = jnp.concatenate([gr_s[rows, :], gi_s[rows, :]], axis=1).astype(BF16)
            dyb = dy_s[rows, :].astype(BF16)
            dwc_ref[...] += lax.dot_general(st, dyb, (((0,), (0,)), ((), ())), preferred_element_type=F32)
            dwb_ref[...] += lax.dot_general(u_ref[rows, :].astype(BF16), g, (((0,), (0,)), ((), ())), preferred_element_type=F32)
            du = lax.dot_general(g, wb_ref[...], (((1,), (1,)), ((), ())), preferred_element_type=F32)
            du_ref[rows, :] = du + d_ref[...] * dy_s[rows, :]

    return pl.pallas_call(
        body, name=name, grid=(ng, n_ex),
        in_specs=[pl.BlockSpec((S, U_LANES), lambda g, e: (e, g))] * 3 + [
            pl.BlockSpec((None, U_LANES, 2 * ST_LANES), lambda g, e: (g, 0, 0)),
            pl.BlockSpec((None, 2 * ST_LANES, U_LANES), lambda g, e: (g, 0, 0)),
            pl.BlockSpec((None, 8, 8, ST_LANES), lambda g, e: (g, 0, 0, 0)),
            pl.BlockSpec((None, 8, 8, ST_LANES), lambda g, e: (g, 0, 0, 0)),
            pl.BlockSpec((1, U_LANES), lambda g, e: (0, g))],
        out_specs=[pl.BlockSpec((S, U_LANES), lambda g, e: (e, g)),
                   pl.BlockSpec((None, U_LANES, 2 * ST_LANES), lambda g, e: (g, 0, 0)),
                   pl.BlockSpec((None, 2 * ST_LANES, U_LANES), lambda g, e: (g, 0, 0)),
                   pl.BlockSpec((None, 8, ST_LANES), lambda g, e: (g, 0, 0)),
                   pl.BlockSpec((None, 8, U_LANES), lambda g, e: (g, 0, 0))],
        out_shape=[jax.ShapeDtypeStruct((T, D), F32),
                   jax.ShapeDtypeStruct((ng, U_LANES, 2 * ST_LANES), F32),
                   jax.ShapeDtypeStruct((ng, 2 * ST_LANES, U_LANES), F32),
                   jax.ShapeDtypeStruct((ng, 8, ST_LANES), F32),
                   jax.ShapeDtypeStruct((ng, 8, U_LANES), F32)],
        scratch_shapes=[pltpu.VMEM((S, ST_LANES), F32)] * 4 + [pltpu.VMEM((S, U_LANES), F32)],
        compiler_params=_cp(("parallel", "arbitrary")),
    )(u, y, dgy, Wb, Wc, cf, cr, d)


TQ = 128
KB = 128


def _head_masks():
    lane = lax.broadcasted_iota(jnp.int32, (1, 2 * HEAD_DIM), 1)
    m0 = (lane < HEAD_DIM).astype(F32)
    return m0, 1.0 - m0


def _head_norm(x, g, m0, m1):
    sq = x * x
    r0 = lax.rsqrt(jnp.sum(sq * m0, axis=-1, keepdims=True) / HEAD_DIM + EPS)
    r1 = lax.rsqrt(jnp.sum(sq * m1, axis=-1, keepdims=True) / HEAD_DIM + EPS)
    r = m0 * r0 + m1 * r1
    return x * r, r


def _head_norm_bwd(dy, n, r, g, m0, m1):
    dn = dy * g
    p = dn * n
    mean = (m0 * jnp.sum(p * m0, axis=-1, keepdims=True) + m1 * jnp.sum(p * m1, axis=-1, keepdims=True)) / HEAD_DIM
    return r * (dn - n * mean), _csum(dy * n)


def _tri_consts():
    r = lax.broadcasted_iota(jnp.int32, (KB, KB), 0)
    c = lax.broadcasted_iota(jnp.int32, (KB, KB), 1)
    ones = jnp.ones((KB, KB), BF16)

    def with_total(m):
        return jnp.concatenate([m.astype(BF16), ones], axis=1)

    return with_total(r > c), with_total(r <= c), with_total(r < c), c < r


def _split_dot(x, u):
    hi = x.astype(BF16)
    lo = (x - hi.astype(F32)).astype(BF16)
    return jnp.dot(hi, u, preferred_element_type=F32) + jnp.dot(lo, u, preferred_element_type=F32)


def _sb_logits(qh, kj, strict):
    z = lax.dot_general(qh, kj, (((1,), (1,)), ((), ())), preferred_element_type=F32)
    lp = jnp.minimum(z, 0.0) - jnp.log(1.0 + jnp.exp(-jnp.abs(z)))
    lf = lp - z
    if strict is not None:
        lf = jnp.where(strict, lf, 0.0)
    return lp, lf


def _sb_block(qh, kj, u_after, carry, strict):
    lp, lf = _sb_logits(qh, kj, strict)
    res = _split_dot(lf, u_after)
    w = jnp.exp(lp + res[:, :KB] + carry)
    if strict is not None:
        w = jnp.where(strict, w, 0.0)
    return w, carry + res[:, KB:]


def _attn_fwd(q, kv, qg, kg, *, n_ex, name):
    T, D = q.shape
    S = T // n_ex
    nhp = D // (2 * HEAD_DIM)
    nq = S // TQ
    scale = 1.0 / math.sqrt(HEAD_DIM)

    def body(q_ref, k_ref, v_ref, qg_ref, kg_ref, o_ref, tot_ref, kn_s, qm_s, vm_s):
        m0, m1 = _head_masks()
        qn, _ = _head_norm(q_ref[...], None, m0, m1)
        qn = qn * (qg_ref[...] * scale)
        kn, _ = _head_norm(k_ref[...], None, m0, m1)
        kn_s[...] = (kn * kg_ref[...]).astype(BF16)
        v = v_ref[...]
        for h, m in enumerate((m0, m1)):
            qm_s[h] = (qn * m).astype(BF16)
            vm_s[h] = (v * m).astype(BF16)
        u_after, _, _, strict = _tri_consts()

        def qblock(i, _):
            rows = pl.ds(pl.multiple_of(i * TQ, TQ), TQ)
            acc = jnp.zeros((TQ, 2 * HEAD_DIM), F32)
            tot = jnp.zeros((TQ, 2 * HEAD_DIM), F32)
            for h, m in enumerate((m0, m1)):
                qh = qm_s[h, rows, :]
                w, carry = _sb_block(qh, kn_s[rows, :], u_after, jnp.zeros((TQ, KB), F32), strict)
                acc = acc + jnp.dot(w.astype(BF16), vm_s[h, rows, :], preferred_element_type=F32)

                def kblock(jj, st, qh=qh, h=h):
                    carry, acc = st
                    keys = pl.ds(pl.multiple_of((i - 1 - jj) * KB, KB), KB)
                    w, carry = _sb_block(qh, kn_s[keys, :], u_after, carry, None)
                    return carry, acc + jnp.dot(w.astype(BF16), vm_s[h, keys, :], preferred_element_type=F32)

                carry, acc = lax.fori_loop(0, i, kblock, (carry, acc))
                tot = tot + carry * m
            o_ref[rows, :] = acc
            tot_ref[rows, :] = tot
            return 0

        lax.fori_loop(0, nq, qblock, 0)

    blk = (S, 2 * HEAD_DIM)
    return pl.pallas_call(
        body, name=name, grid=(n_ex, nhp),
        in_specs=[pl.BlockSpec(blk, lambda e, h: (e, h)), pl.BlockSpec(blk, lambda e, h: (e, h)),
                  pl.BlockSpec(blk, lambda e, h: (e, h + nhp)),
                  pl.BlockSpec((1, 2 * HEAD_DIM), lambda e, h: (0, 0)), pl.BlockSpec((1, 2 * HEAD_DIM), lambda e, h: (0, 0))],
        out_specs=[pl.BlockSpec(blk, lambda e, h: (e, h))] * 2,
        out_shape=[jax.ShapeDtypeStruct((T, D), F32)] * 2,
        scratch_shapes=[pltpu.VMEM(blk, BF16), pltpu.VMEM((2,) + blk, BF16), pltpu.VMEM((2,) + blk, BF16)],
        compiler_params=_cp(("parallel", "parallel")),
    )(q, kv, kv, qg, kg)


def _attn_bwd(q, kv, tot, do, qg, kg, *, n_ex, name):
    T, D = q.shape
    S = T // n_ex
    nhp = D // (2 * HEAD_DIM)
    nq = S // TQ
    scale = 1.0 / math.sqrt(HEAD_DIM)

    def body(q_ref, k_ref, v_ref, tot_ref, do_ref, qg_ref, kg_ref, dq_ref, dk_ref, dv_ref, dqg_ref, dkg_ref,
             kn_s, km_s, qm_s, vb_s, dom_s, dqn_s, dkn_s, dv_s):
        m0, m1 = _head_masks()
        qn, qr = _head_norm(q_ref[...], None, m0, m1)
        kn, kr = _head_norm(k_ref[...], None, m0, m1)
        qs = qn * (qg_ref[...] * scale)
        kk = kn * kg_ref[...]
        kn_s[...] = kk.astype(BF16)
        vb_s[...] = v_ref[...].astype(BF16)
        do = do_ref[...]
        for h, m in enumerate((m0, m1)):
            qm_s[h] = (qs * m).astype(BF16)
            km_s[h] = (kk * m).astype(BF16)
            dom_s[h] = (do * m).astype(BF16)
        dkn_s[...] = jnp.zeros_like(dkn_s)
        dv_s[...] = jnp.zeros_like(dv_s)
        _, u_upto, u_before, strict = _tri_consts()

        def one(qh, doh, total, keys, h, pre_lf, pre_e, dq, mask):
            lp, lf = _sb_logits(qh, kn_s[keys, :], mask)
            r_lf = _split_dot(lf, u_upto)
            w = jnp.exp(lp + (total - (r_lf[:, :KB] + pre_lf)))
            if mask is not None:
                w = jnp.where(mask, w, 0.0)
            dw = lax.dot_general(doh, vb_s[keys, :], (((1,), (1,)), ((), ())), preferred_element_type=F32)
            ew = dw * w
            r_e = _split_dot(ew, u_before)
            sig = jnp.exp(lp)
            dz = ew * (1.0 - sig) - (r_e[:, :KB] + pre_e) * sig
            if mask is not None:
                dz = jnp.where(mask, dz, 0.0)
            dzb = dz.astype(BF16)
            dq = dq + jnp.dot(dzb, km_s[h, keys, :], preferred_element_type=F32)
            dkn_s[keys, :] += lax.dot_general(dzb, qh, (((0,), (0,)), ((), ())), preferred_element_type=F32)
            dv_s[keys, :] += lax.dot_general(w.astype(BF16), doh, (((0,), (0,)), ((), ())), preferred_element_type=F32)
            return pre_lf + r_lf[:, KB:], pre_e + r_e[:, KB:], dq

        def qblock(i, _):
            rows = pl.ds(pl.multiple_of(i * TQ, TQ), TQ)
            dq = jnp.zeros((TQ, 2 * HEAD_DIM), F32)
            tt = tot_ref[rows, :]
            for h, m in enumerate((m0, m1)):
                qh = qm_s[h, rows, :]
                doh = dom_s[h, rows, :]
                total = jnp.broadcast_to(jnp.sum(tt * m, axis=-1, keepdims=True) * (1.0 / HEAD_DIM), (TQ, KB))
                zero = jnp.zeros((TQ, KB), F32)

                def kblock(jj, st, qh=qh, doh=doh, total=total, h=h):
                    keys = pl.ds(pl.multiple_of(jj * KB, KB), KB)
                    return one(qh, doh, total, keys, h, st[0], st[1], st[2], None)

                p_lf, p_e, dq = lax.fori_loop(0, i, kblock, (zero, zero, dq))
                _, _, dq = one(qh, doh, total, rows, h, p_lf, p_e, dq, strict)
            dqn_s[rows, :] = dq
            return 0

        lax.fori_loop(0, nq, qblock, 0)
        dq, dqg = _head_norm_bwd(dqn_s[...] * scale, qn, qr, qg_ref[...], m0, m1)
        dk, dkg = _head_norm_bwd(dkn_s[...], kn, kr, kg_ref[...], m0, m1)
        dq_ref[...] = dq
        dk_ref[...] = dk
        dv_ref[...] = dv_s[...]
        dqg_ref[...] = dqg
        dkg_ref[...] = dkg

    blk = (S, 2 * HEAD_DIM)
    gblk = (None, None, 1, 2 * HEAD_DIM)
    dq, dk, dv, dqg, dkg = pl.pallas_call(
        body, name=name, grid=(n_ex, nhp),
        in_specs=[pl.BlockSpec(blk, lambda e, h: (e, h)), pl.BlockSpec(blk, lambda e, h: (e, h)),
                  pl.BlockSpec(blk, lambda e, h: (e, h + nhp)),
                  pl.BlockSpec(blk, lambda e, h: (e, h)), pl.BlockSpec(blk, lambda e, h: (e, h)),
                  pl.BlockSpec((1, 2 * HEAD_DIM), lambda e, h: (0, 0)), pl.BlockSpec((1, 2 * HEAD_DIM), lambda e, h: (0, 0))],
        out_specs=[pl.BlockSpec(blk, lambda e, h: (e, h))] * 3 + [pl.BlockSpec(gblk, lambda e, h: (e, h, 0, 0))] * 2,
        out_shape=[jax.ShapeDtypeStruct((T, D), F32)] * 3 + [jax.ShapeDtypeStruct((n_ex, nhp, 1, 2 * HEAD_DIM), F32)] * 2,
        scratch_shapes=[pltpu.VMEM(blk, BF16), pltpu.VMEM((2,) + blk, BF16), pltpu.VMEM((2,) + blk, BF16),
                        pltpu.VMEM(blk, BF16), pltpu.VMEM((2,) + blk, BF16),
                        pltpu.VMEM(blk, F32), pltpu.VMEM(blk, F32), pltpu.VMEM(blk, F32)],
        compiler_params=_cp(("parallel", "parallel")),
    )(q, kv, kv, tot, do, qg, kg)
    return dq, dk, dv, dqg, dkg


def _place():
    return lax.axis_index("x"), lax.axis_index("y"), lax.axis_index("c")


def _all_gather8(x_shard, *, name):
    m_per, n = x_shard.shape

    def body(x_ref, out_ref, send_sems, recv_sems, local_sem):
        x, y, c = _place()
        me, sibling = (x, y, c), (x, y, 1 - c)
        chips = [(1 - x, y), (x, 1 - y), (1 - x, 1 - y)]

        def rows(px, py, pc):
            return out_ref.at[pl.ds((4 * px + 2 * py + pc) * m_per, m_per), :]

        def copy(k, block, to, src=None):
            return pltpu.make_async_remote_copy(
                src_ref=rows(*block) if src is None else src, dst_ref=rows(*block),
                send_sem=send_sems.at[k], recv_sem=recv_sems.at[k], device_id=to, device_id_type=MESH)

        mine = pltpu.make_async_copy(x_ref, rows(*me), local_sem)
        mine.start()
        first = [copy(0, me, sibling, src=x_ref)]
        first += [copy(1 + j, me, (*chip, c), src=x_ref) for j, chip in enumerate(chips)]
        for cp in first:
            cp.start()
        passed = [copy(4 + j, (*chip, c), sibling) for j, chip in enumerate(chips)]
        for j, chip in enumerate(chips):
            copy(1 + j, (*chip, c), me).wait_recv()
            passed[j].start()
        copy(0, sibling, me).wait_recv()
        for j, chip in enumerate(chips):
            copy(4 + j, (*chip, 1 - c), me).wait_recv()
        for cp in first + passed:
            cp.wait_send()
        mine.wait()

    return pl.pallas_call(
        body, name=name, out_shape=jax.ShapeDtypeStruct((8 * m_per, n), x_shard.dtype),
        in_specs=[pl.BlockSpec(memory_space=pltpu.VMEM)], out_specs=pl.BlockSpec(memory_space=pltpu.VMEM),
        scratch_shapes=[pltpu.SemaphoreType.DMA((7,)), pltpu.SemaphoreType.DMA((7,)), pltpu.SemaphoreType.DMA],
        compiler_params=pltpu.CompilerParams(vmem_limit_bytes=VMEM_LIMIT),
    )(x_shard)


def _sum_blocks(x, n, *, name):
    R = x.shape[0] // n

    def body(x_ref, o_ref):
        acc = x_ref[pl.ds(0, R), :]
        for k in range(1, n):
            acc = acc + x_ref[pl.ds(k * R, R), :]
        o_ref[...] = acc

    return pl.pallas_call(body, name=name, out_shape=jax.ShapeDtypeStruct((R, x.shape[1]), x.dtype),
                          compiler_params=pltpu.CompilerParams(vmem_limit_bytes=VMEM_LIMIT))(x)


def _colsum(x, *, name):
    def body(x_ref, o_ref):
        o_ref[...] = jnp.sum(x_ref[...], axis=0, keepdims=True)
    return pl.pallas_call(body, name=name, out_shape=jax.ShapeDtypeStruct((1, x.shape[1]), x.dtype))(x)


ANY = pl.BlockSpec(memory_space=pl.ANY)


def _chip_exchange(src, *, scatter, name):
    blk = src.shape[1:]

    def body(src_ref, out_ref, send_sems, recv_sems, local_sem):
        x, y, c = _place()
        myj = 2 * x + y
        chips = [(1 - x, y), (x, 1 - y), (1 - x, 1 - y)]

        def slot(j):
            return out_ref.at[j] if scatter else out_ref.at[j, c]

        def piece(j):
            return src_ref.at[j] if scatter else src_ref.at[c]

        mine = pltpu.make_async_copy(piece(myj), slot(myj), local_sem)
        mine.start()
        sends = []
        for k, (cx, cy) in enumerate(chips):
            sends.append(pltpu.make_async_remote_copy(
                src_ref=piece(2 * cx + cy), dst_ref=slot(myj),
                send_sem=send_sems.at[k], recv_sem=recv_sems.at[k], device_id=(cx, cy, c), device_id_type=MESH))
            sends[-1].start()
        for k, (cx, cy) in enumerate(chips):
            pltpu.make_async_remote_copy(
                src_ref=slot(2 * cx + cy), dst_ref=slot(2 * cx + cy),
                send_sem=send_sems.at[k], recv_sem=recv_sems.at[k], device_id=(cx, cy, c), device_id_type=MESH).wait_recv()
        for cp in sends:
            cp.wait_send()
        mine.wait()

    return pl.pallas_call(
        body, name=name, out_shape=jax.ShapeDtypeStruct(((4,) + tuple(blk)) if scatter else ((4, 2) + tuple(blk)), src.dtype),
        in_specs=[ANY], out_specs=ANY,
        scratch_shapes=[pltpu.SemaphoreType.DMA((3,)), pltpu.SemaphoreType.DMA((3,)), pltpu.SemaphoreType.DMA],
    )(src)


def _sibling_fill(buf, *, axis, name):
    def half(ref, h):
        return ref.at[h] if axis == 0 else ref.at[:, h]

    def body(in_ref, out_ref, send_sem, recv_sem):
        x, y, c = _place()
        cp = pltpu.make_async_remote_copy(src_ref=half(out_ref, c), dst_ref=half(out_ref, c), send_sem=send_sem, recv_sem=recv_sem,
                                          device_id=(x, y, 1 - c), device_id_type=MESH)
        cp.start()
        pltpu.make_async_remote_copy(src_ref=half(out_ref, 1 - c), dst_ref=half(out_ref, 1 - c), send_sem=send_sem, recv_sem=recv_sem,
                                     device_id=(x, y, 1 - c), device_id_type=MESH).wait_recv()
        cp.wait_send()

    return pl.pallas_call(
        body, name=name, out_shape=jax.ShapeDtypeStruct(buf.shape, buf.dtype), in_specs=[ANY], out_specs=ANY,
        input_output_aliases={0: 0}, scratch_shapes=[pltpu.SemaphoreType.DMA, pltpu.SemaphoreType.DMA],
    )(buf)


def _sibling_swap_half(g, *, name):
    def body(g_ref, out_ref, send_sem, recv_sem):
        x, y, c = _place()
        cp = pltpu.make_async_remote_copy(src_ref=g_ref.at[:, 1 - c], dst_ref=out_ref, send_sem=send_sem, recv_sem=recv_sem,
                                          device_id=(x, y, 1 - c), device_id_type=MESH)
        cp.start()
        cp.wait()

    return pl.pallas_call(
        body, name=name, out_shape=jax.ShapeDtypeStruct((g.shape[0],) + g.shape[2:], g.dtype), in_specs=[ANY], out_specs=ANY,
        scratch_shapes=[pltpu.SemaphoreType.DMA, pltpu.SemaphoreType.DMA],
    )(g)


def _add_my_half(g, b, cidx, *, name, tr=256):
    n, _, R, C = g.shape
    tr = math.gcd(tr, R)

    def body(c_ref, g_ref, b_ref, o_ref):
        o_ref[...] = g_ref[...] + b_ref[...]

    return pl.pallas_call(
        body, name=name, out_shape=jax.ShapeDtypeStruct((n, R, C), g.dtype),
        grid_spec=pltpu.PrefetchScalarGridSpec(
            num_scalar_prefetch=1, grid=(n, R // tr),
            in_specs=[pl.BlockSpec((None, None, tr, C), lambda j, i, c: (j, c[0], i, 0)),
                      pl.BlockSpec((None, tr, C), lambda j, i, c: (j, i, 0))],
            out_specs=pl.BlockSpec((None, tr, C), lambda j, i, c: (j, i, 0))),
        compiler_params=_cp(("parallel", "parallel")),
    )(cidx, g, b)


def _sum4_into_half(q, cidx, *, name, tr=256):
    _, R, C = q.shape
    tr = math.gcd(tr, R)

    def body(c_ref, q_ref, o_ref):
        o_ref[...] = ((q_ref[0] + q_ref[1]) + q_ref[2]) + q_ref[3]

    return pl.pallas_call(
        body, name=name, out_shape=jax.ShapeDtypeStruct((2, R, C), q.dtype),
        grid_spec=pltpu.PrefetchScalarGridSpec(
            num_scalar_prefetch=1, grid=(R // tr,),
            in_specs=[pl.BlockSpec((4, tr, C), lambda i, c: (0, i, 0))],
            out_specs=pl.BlockSpec((None, tr, C), lambda i, c: (c[0], i, 0))),
        compiler_params=_cp(("parallel",)),
    )(cidx, q)


def _pack_rows(parts, width=1024):
    rows, spans, r0 = [], [], 0
    for p in parts:
        n = p.size
        nr = -(-n // width)
        flat = p.reshape(-1)
        if nr * width != n:
            flat = jnp.pad(flat, (0, nr * width - n))
        rows.append(flat.reshape(nr, width))
        spans.append((r0, nr, n, p.shape))
        r0 += nr
    pad = (-r0) % 8
    if pad:
        rows.append(jnp.zeros((pad, width), parts[0].dtype))
    return jnp.concatenate(rows, axis=0), spans


def _unpack_rows(buf, spans):
    return [buf[r0:r0 + nr].reshape(-1)[:n].reshape(shape) for (r0, nr, n, shape) in spans]


def kernel(x, c, ada_w, ada_b, mix_norm_g, mlp_norm_g, mlp_w1, mlp_w2, s5_a_re, s5_a_im, s5_log_dt, s5_b_re, s5_b_im, s5_c_re, s5_c_im, s5_d, s5_w_glu, kv_ada_w, kv_ada_b, kv_norm_g, w_kv, k_norm_g, sb_w_q, q_norm_g, sb_w_o, loss_target, m_ada_w, m_ada_b, m_mix_norm_g, m_mlp_norm_g, m_mlp_w1, m_mlp_w2, m_s5_a_re, m_s5_a_im, m_s5_log_dt, m_s5_b_re, m_s5_b_im, m_s5_c_re, m_s5_c_im, m_s5_d, m_s5_w_glu, m_kv_ada_w, m_kv_ada_b, m_kv_norm_g, m_w_kv, m_k_norm_g, m_sb_w_q, m_q_norm_g, m_sb_w_o, v_ada_w, v_ada_b, v_mix_norm_g, v_mlp_norm_g, v_mlp_w1, v_mlp_w2, v_s5_a_re, v_s5_a_im, v_s5_log_dt, v_s5_b_re, v_s5_b_im, v_s5_c_re, v_s5_c_im, v_s5_d, v_s5_w_glu, v_kv_ada_w, v_kv_ada_b, v_kv_norm_g, v_w_kv, v_k_norm_g, v_sb_w_q, v_q_norm_g, v_sb_w_o):
    E, S, D = x.shape
    T = E * S
    FF = 4 * D
    NB = 8 * E
    px, py, pc = _place()
    chip = 2 * px + py
    dev = 4 * px + 2 * py + pc
    cidx = jnp.reshape(pc, (1,)).astype(jnp.int32)
    x0 = x.reshape(T, D)
    tgt = loss_target.reshape(T, D)

    c_all = _all_gather8(c.reshape(-1, 128), name="ag_c").reshape(NB, D)
    sc_all = (c_all * _sigmoid(c_all)).astype(BF16)
    wa = ada_w.shape[2]
    wk = kv_ada_w.shape[1]
    m_sh = jnp.concatenate([_mm(sc_all, ada_w[0], "nn", name="ada0", tn=256),
                            _mm(sc_all, ada_w[1], "nn", name="ada1", tn=256),
                            _mm(sc_all, kv_ada_w, "nn", name="ada_kv", tn=256)], axis=1)
    m_all = _all_gather8(m_sh, name="ag_m").reshape(4, 2, NB, 2 * wa + wk)[:, 0]
    mods = []
    for l in range(2):
        full = jnp.transpose(m_all[:, :, l * wa:(l + 1) * wa], (1, 0, 2)).reshape(NB, 6 * D) + ada_b[l]
        mine = lax.dynamic_slice_in_dim(full, E * dev, E, axis=0)
        mods.append([mine[:, i * D:(i + 1) * D].reshape(E, 1, D) for i in range(6)])
    full = jnp.transpose(m_all[:, :, 2 * wa:], (1, 0, 2)).reshape(NB, 2 * D) + kv_ada_b
    mine = lax.dynamic_slice_in_dim(full, E * dev, E, axis=0)
    kv_sh, kv_sc = [mine[:, i * D:(i + 1) * D].reshape(E, 1, D) for i in range(2)]

    wrows = [mlp_w1[0], mlp_w1[1], mlp_w2[0], mlp_w2[1], jnp.concatenate([s5_w_glu[0], w_kv], axis=1), sb_w_q[0], sb_w_o[0]]
    wpack = jnp.concatenate(wrows, axis=0).astype(BF16)
    RW = wpack.shape[0]
    wfull = _chip_exchange(wpack.reshape(2, RW // 2, D), scatter=False, name="wgather_ici")
    wfull = _sibling_fill(wfull, axis=1, name="wgather_d2d").reshape(4, RW, D)

    def cols(r0, nr, c0, nc):
        return jnp.transpose(wfull[:, r0:r0 + nr, c0:c0 + nc], (1, 0, 2)).reshape(nr, 4 * nc)

    def rws(r0, nr):
        return wfull[:, r0:r0 + nr, :].reshape(4 * nr, D)

    W1 = [cols(0, D, 0, D), cols(D, D, 0, D)]
    W2 = [rws(2 * D, D), rws(3 * D, D)]
    Wglu = cols(4 * D, D, 0, D // 2)
    Wkv = cols(4 * D, D, D // 2, D // 2)
    Wq = rws(5 * D, D // 4)
    Wo = rws(5 * D + D // 4, D // 4)

    tm = min(512, S)

    def mlp_fwd(xa, l, mod):
        sh_m, sc_m, g_m = mod[3], mod[4], mod[5]
        h = _norm_mod_fwd(xa, mlp_norm_g[l:l + 1], sh_m, sc_m, n_ex=E, out_dtype=BF16, name=f"mlp_norm{l}")
        a, r = _mm(h, W1[l], "nn", name=f"mlp_up{l}", out_dtypes=(F32, BF16), tm=tm,
                   epilogue=lambda acc: (acc, jnp.square(jnp.maximum(acc, 0.0))))
        xb, ff = _mm(r, W2[l], "nn", name=f"mlp_down{l}", out_dtypes=(F32, F32), tm=tm,
                     extras=[_mn_extra(xa), _vec_extra(g_m, S)],
                     epilogue=lambda acc, xat, gt: (xat + gt * acc, acc))
        return xb, (h, a, r, ff)

    def mlp_bwd(dxb, xa, l, mod, saved):
        sc_m, g_m = mod[4], mod[5]
        h, a, r, ff = saved
        (dff,), (dgm,) = _rowwise(lambda d, f, g: ([g * d], [_csum(d * f)]), [(dxb, D, 0), (ff, D, 0)], [g_m], [],
                                  [(D, BF16)], [D], n_ex=E, name=f"mlp_gate_bwd{l}")
        da = _mm(dff, W2[l], "nt", name=f"mlp_down_dx{l}", out_dtypes=(BF16,), tm=tm, extras=[_mn_extra(a)],
                 epilogue=lambda acc, at: (acc * (2.0 * jnp.maximum(at, 0.0)),))
        dW2 = _mm(r, dff, "tn", name=f"mlp_down_dw{l}", tk=512)
        dh = _mm(da, W1[l], "nt", name=f"mlp_up_dx{l}", tm=tm)
        dW1 = _mm(h, da, "tn", name=f"mlp_up_dw{l}", tk=512)
        (dxa,), (dsh, dsc, dg) = _norm_mod_bwd(xa, dh, dxb, mlp_norm_g[l:l + 1], sc_m, n_ex=E, name=f"mlp_norm_bwd{l}")
        return dxa, dW1, dW2, (dsh, dsc, dgm), dg

    ab_re, ab_im, bb_re, bb_im = _s5_disc(s5_a_re[0], s5_a_im[0], s5_log_dt[0], s5_b_re[0], s5_b_im[0])
    cf, cr = _s5_consts(ab_re, ab_im)
    Wb, Wc = _s5_blockdiag(bb_re, bb_im, s5_c_re[0], s5_c_im[0])
    ng = D // U_LANES
    nd = s5_d.size // 128
    d_full = _all_gather8(jnp.pad(s5_d.reshape(nd, 128), ((0, 8 - nd), (0, 0))), name="ag_d")
    d_full = d_full.reshape(4, 2, 8, 128)[:, 0, :nd].reshape(1, D)

    mod0, mod1 = mods
    h0 = _norm_mod_fwd(x0, mix_norm_g[0:1], mod0[0], mod0[1], n_ex=E, out_dtype=F32, name="mix_norm0")
    y, gy = _s5_fwd(h0, Wb, Wc, cf, d_full, n_ex=E, name="s5_fwd")
    vg = _mm(gy, Wglu, "nn", name="glu_up", tm=tm)
    (x1,), _ = _rowwise(lambda v, g, xt, ga: ([xt + ga * (v * _sigmoid(g))], []),
                        [(vg, D, 0), (vg, D, 1), (x0, D, 0)], [mod0[2]], [], [(D, F32)], [], n_ex=E, name="glu_gate")
    x2, saved_mlp0 = mlp_fwd(x1, 0, mod0)

    hkv = _norm_mod_fwd(x2, kv_norm_g.reshape(1, D), kv_sh, kv_sc, n_ex=E, out_dtype=BF16, name="kv_norm")
    kvf = _mm(hkv, Wkv, "nn", name="kv_proj", tm=tm)
    h1 = _norm_mod_fwd(x2, mix_norm_g[1:2], mod1[0], mod1[1], n_ex=E, out_dtype=BF16, name="mix_norm1")
    qf = _mm(h1, Wq, "nn", name="q_proj", tm=tm)
    qg2 = jnp.tile(q_norm_g.reshape(1, HEAD_DIM), (1, 2))
    kg2 = jnp.tile(k_norm_g.reshape(1, HEAD_DIM), (1, 2))
    o, lf_tot = _attn_fwd(qf, kvf, qg2, kg2, n_ex=E, name="attn_fwd")
    x3, mix1 = _mm(o, Wo, "nn", name="o_proj", out_dtypes=(F32, F32), tm=tm,
                   extras=[_mn_extra(x2), _vec_extra(mod1[2], S)],
                   epilogue=lambda acc, xat, gt: (xat + gt * acc, acc))
    x4, saved_mlp1 = mlp_fwd(x3, 1, mod1)

    (dx4,), (lsum,) = _rowwise(lambda xt, tt: ([(xt - tt) * (1.0 / D)], [_csum(jnp.square(xt - tt)) * (0.5 / D)]),
                               [(x4, D, 0), (tgt, D, 0)], [], [], [(D, F32)], [D], n_ex=E, name="loss")
    loss = lax.psum(jnp.sum(lsum), ("x", "y", "c"))

    dx3, dW1_1, dW2_1, (dsh_m1, dsc_m1, dgm1), dg_mlp1 = mlp_bwd(dx4, x3, 1, mod1, saved_mlp1)
    (dmix1,), (dga1,) = _rowwise(lambda d, f, g: ([g * d], [_csum(d * f)]), [(dx3, D, 0), (mix1, D, 0)], [mod1[2]], [],
                                 [(D, BF16)], [D], n_ex=E, name="attn_gate_bwd")
    do = _mm(dmix1, Wo, "nt", name="o_proj_dx", tm=tm)
    dWo = _mm(o, dmix1, "tn", name="o_proj_dw", tk=512)
    dq, dk, dv, dqg, dkg = _attn_bwd(qf, kvf, lf_tot, do, qg2, kg2, n_ex=E, name="attn_bwd")
    dh1 = _mm(dq, Wq, "nt", name="q_proj_dx", tm=tm)
    dWq = _mm(h1, dq, "tn", name="q_proj_dw", tk=512)
    (dx2,), (dsh_a1, dsc_a1, dg_mix1) = _norm_mod_bwd(x2, dh1, dx3, mix_norm_g[1:2], mod1[1], n_ex=E, name="mix_norm_bwd1")
    dkv = jnp.concatenate([dk, dv], axis=1)
    dhkv = _mm(dkv, Wkv, "nt", name="kv_proj_dx", tm=tm)
    dWkv = _mm(hkv, dkv, "tn", name="kv_proj_dw", tk=512)
    (dx2,), (dkv_sh, dkv_sc, dg_kv) = _norm_mod_bwd(x2, dhkv, dx2, kv_norm_g.reshape(1, D), kv_sc, n_ex=E, name="kv_norm_bwd")

    dx1, dW1_0, dW2_0, (dsh_m0, dsc_m0, dgm0), dg_mlp0 = mlp_bwd(dx2, x1, 0, mod0, saved_mlp0)

    def glu_bwd(v, g, d, ga):
        sg = _sigmoid(g)
        dm = ga * d
        return [jnp.concatenate([dm * sg, dm * v * sg * (1.0 - sg)], axis=1)], [_csum(d * (v * sg))]
    (dvg,), (dga0,) = _rowwise(glu_bwd, [(vg, D, 0), (vg, D, 1), (dx1, D, 0)], [mod0[2]], [], [(2 * D, BF16)], [D],
                               n_ex=E, name="glu_gate_bwd")
    dgy = _mm(dvg, Wglu, "nt", name="glu_up_dx", tm=tm)
    dWglu = _mm(gy, dvg, "tn", name="glu_up_dw", tk=512)
    dh0, dWb, dWc, dab, dd = _s5_bwd(h0, y, dgy, Wb, Wc, cf, cr, d_full, n_ex=E, name="s5_bwd")
    (gx,), (dsh_a0, dsc_a0, dg_mix0) = _norm_mod_bwd(x0, dh0, dx1, mix_norm_g[0:1], mod0[1], n_ex=E, name="mix_norm_bwd0")
    grad_x = gx.reshape(E, S, D)

    dm_mine = jnp.concatenate([t.reshape(E, D) for t in
                               (dsh_a0, dsc_a0, dga0, dsh_m0, dsc_m0, dgm0, dsh_a1, dsc_a1, dga1, dsh_m1, dsc_m1, dgm1, dkv_sh, dkv_sc)], axis=1)
    dm_all = _all_gather8(dm_mine.reshape(8, -1), name="ag_dm").reshape(NB, 14 * D)
    sc_f32 = c_all * _sigmoid(c_all)
    g_ada_w = jnp.stack([_mm(sc_f32, lax.dynamic_slice_in_dim(dm_all, l * 6 * D + chip * wa, wa, axis=1), "tn", name=f"ada_dw{l}", tn=256)
                         for l in range(2)])
    g_kv_ada_w = _mm(sc_f32, lax.dynamic_slice_in_dim(dm_all, 12 * D + chip * wk, wk, axis=1), "tn", name="ada_kv_dw", tn=256)
    db_all = _colsum(dm_all, name="ada_db")
    g_ada_b = db_all[0, :12 * D].reshape(2, 6 * D)
    g_kv_ada_b = db_all[0, 12 * D:]

    dWb_re, dWb_im, dC_re, dC_im = _s5_unblock(dWb, dWc)
    small_parts = [dg_mix0.sum(0), dg_mix1.sum(0), dg_mlp0.sum(0), dg_mlp1.sum(0), dg_kv.sum(0),
                   dqg.sum((0, 1, 2)).reshape(2, HEAD_DIM).sum(0), dkg.sum((0, 1, 2)).reshape(2, HEAD_DIM).sum(0),
                   dd[:, 0, :], dab[:, 0, :], dab[:, 1, :], dWb_re, dWb_im, dC_re, dC_im]
    spack, spans = _pack_rows(small_parts)
    ssum = _sum_blocks(_all_gather8(spack, name="ag_small"), 8, name="sum_small")
    (g_mix0, g_mix1, g_mlp0, g_mlp1, g_kvn, g_qn, g_kn, g_d, g_abr, g_abi, g_bbr, g_bbi, g_cre, g_cim) = _unpack_rows(ssum, spans)
    _, disc_vjp = jax.vjp(_s5_disc, s5_a_re[0], s5_a_im[0], s5_log_dt[0], s5_b_re[0], s5_b_im[0])
    g_are, g_aim, g_ldt, g_bre, g_bim = disc_vjp((g_abr.reshape(ab_re.shape), g_abi.reshape(ab_im.shape), g_bbr, g_bbi))
    g_s5d = lax.dynamic_slice_in_dim(g_d.reshape(1, D), chip * s5_d.shape[1], s5_d.shape[1], axis=1)

    def csh(g, nc):
        return jnp.transpose(g.reshape(g.shape[0], 4, nc), (1, 0, 2))

    gparts = [csh(dW1_0, D), csh(dW1_1, D), dW2_0.reshape(4, D, D), dW2_1.reshape(4, D, D),
              jnp.concatenate([csh(dWglu, D // 2), csh(dWkv, D // 2)], axis=2), dWq.reshape(4, D // 4, D), dWo.reshape(4, D // 4, D)]
    gpack = jnp.concatenate(gparts, axis=1).reshape(4, 2, RW // 2, D)
    theirs = _sibling_swap_half(gpack, name="gscatter_d2d")
    chip_sum = _add_my_half(gpack, theirs, cidx, name="gscatter_add")
    from_chips = _chip_exchange(chip_sum, scatter=True, name="gscatter_ici")
    ghalf = _sum4_into_half(from_chips, cidx, name="gscatter_sum")
    gsh = _sibling_fill(ghalf, axis=0, name="gscatter_fill").reshape(RW, D)

    def upd_big(w, m, v, roff, cb, name):
        shape = w.shape
        W = shape[-1]
        d_, m_, v_, g_ = _adamw2d(w.reshape(-1, W), gsh, m.reshape(-1, W), v.reshape(-1, W), name=name, g_roff=roff, g_cb=cb)
        return [t.reshape(shape) for t in (g_, d_, m_, v_)]

    def upd_own(w, g, m, v, name):
        shape = w.shape
        W = shape[-1]
        d_, m_, v_, g_ = _adamw2d(w.reshape(-1, W), g.reshape(-1, W), m.reshape(-1, W), v.reshape(-1, W), name=name)
        return [t.reshape(shape) for t in (g_, d_, m_, v_)]

    res = {}
    res["ada_w"] = upd_own(ada_w, g_ada_w, m_ada_w, v_ada_w, "adam_ada_w")
    res["kv_ada_w"] = upd_own(kv_ada_w, g_kv_ada_w, m_kv_ada_w, v_kv_ada_w, "adam_kv_ada_w")
    res["mlp_w1"] = upd_big(mlp_w1, m_mlp_w1, v_mlp_w1, 0, 0, "adam_w1")
    res["mlp_w2"] = upd_big(mlp_w2, m_mlp_w2, v_mlp_w2, 2 * D, 0, "adam_w2")
    res["s5_w_glu"] = upd_big(s5_w_glu, m_s5_w_glu, v_s5_w_glu, 4 * D, 0, "adam_glu")
    res["w_kv"] = upd_big(w_kv, m_w_kv, v_w_kv, 4 * D, 1, "adam_wkv")
    res["sb_w_q"] = upd_big(sb_w_q, m_sb_w_q, v_sb_w_q, 5 * D, 0, "adam_wq")
    res["sb_w_o"] = upd_big(sb_w_o, m_sb_w_o, v_sb_w_o, 5 * D + D // 4, 0, "adam_wo")

    small = {
        "ada_b": (ada_b, g_ada_b, m_ada_b, v_ada_b),
        "mix_norm_g": (mix_norm_g, jnp.stack([g_mix0, g_mix1]), m_mix_norm_g, v_mix_norm_g),
        "mlp_norm_g": (mlp_norm_g, jnp.stack([g_mlp0, g_mlp1]), m_mlp_norm_g, v_mlp_norm_g),
        "s5_a_re": (s5_a_re, g_are[None], m_s5_a_re, v_s5_a_re),
        "s5_a_im": (s5_a_im, g_aim[None], m_s5_a_im, v_s5_a_im),
        "s5_log_dt": (s5_log_dt, g_ldt[None], m_s5_log_dt, v_s5_log_dt),
        "s5_b_re": (s5_b_re, g_bre[None], m_s5_b_re, v_s5_b_re),
        "s5_b_im": (s5_b_im, g_bim[None], m_s5_b_im, v_s5_b_im),
        "s5_c_re": (s5_c_re, g_cre[None], m_s5_c_re, v_s5_c_re),
        "s5_c_im": (s5_c_im, g_cim[None], m_s5_c_im, v_s5_c_im),
        "s5_d": (s5_d, g_s5d, m_s5_d, v_s5_d),
        "kv_ada_b": (kv_ada_b, g_kv_ada_b, m_kv_ada_b, v_kv_ada_b),
        "kv_norm_g": (kv_norm_g, g_kvn, m_kv_norm_g, v_kv_norm_g),
        "k_norm_g": (k_norm_g, g_kn, m_k_norm_g, v_k_norm_g),
        "q_norm_g": (q_norm_g, g_qn.reshape(q_norm_g.shape), m_q_norm_g, v_q_norm_g),
    }
    names = list(small)
    packs = [_pack_rows([small[n][i].reshape(small[n][0].shape) for n in names]) for i in range(4)]
    sp = packs[0][1]
    d_, m_, v_, g_ = _adamw2d(packs[0][0], packs[1][0], packs[2][0], packs[3][0], name="adam_small")
    for n, gg, dd_, mm_, vv_ in zip(names, _unpack_rows(g_, sp), _unpack_rows(d_, sp), _unpack_rows(m_, sp), _unpack_rows(v_, sp)):
        res[n] = [gg, dd_, mm_, vv_]

    order = ["ada_w", "ada_b", "mix_norm_g", "mlp_norm_g", "mlp_w1", "mlp_w2", "s5_a_re", "s5_a_im", "s5_log_dt", "s5_b_re", "s5_b_im",
             "s5_c_re", "s5_c_im", "s5_d", "s5_w_glu", "kv_ada_w", "kv_ada_b", "kv_norm_g", "w_kv", "k_norm_g", "sb_w_q", "q_norm_g", "sb_w_o"]
    return (loss, grad_x, *[res[n][0] for n in order], *[res[n][1] for n in order], *[res[n][2] for n in order], *[res[n][3] for n in order])
```

```python
import functools
import math

import jax
import jax.numpy as jnp
from jax import lax
from jax.experimental import pallas as pl
from jax.experimental.pallas import tpu as pltpu

F32 = jnp.float32
BF16 = jnp.bfloat16
EPS = 1e-6
HEAD_DIM = 64
S5_GROUP = 16
S5_STATE = 64
GROUPS_PER_STEP = 8
U_LANES = GROUPS_PER_STEP * S5_GROUP
ST_LANES = GROUPS_PER_STEP * S5_STATE
SCAN_LANES = 256
VMEM_LIMIT = 56 * 1024 * 1024
ADAM_LR, ADAM_B1, ADAM_B2, ADAM_EPS, ADAM_WD, ADAM_STEP = 0.001, 0.9, 0.999, 1e-08, 0.01, 10
MESH = pl.DeviceIdType.MESH


def _cp(sem):
    return pltpu.CompilerParams(dimension_semantics=sem, vmem_limit_bytes=VMEM_LIMIT)


def _mm(a, b, dims, *, name, out_dtypes=(F32,), epilogue=None, extras=(), tm=512, tn=1024, tk=1024):
    if dims == "nn":
        (M, K), (_, N) = a.shape, b.shape
    elif dims == "nt":
        (M, K), (N, _) = a.shape, b.shape
    else:
        (K, M), (_, N) = a.shape, b.shape
    tm, tn, tk = min(tm, M), min(tn, N), min(tk, K)
    assert M % tm == 0 and N % tn == 0 and K % tk == 0, (M, N, K, tm, tn, tk)
    nk = K // tk
    extras = [e(tm, tn) for e in extras]
    a_spec = pl.BlockSpec((tk, tm), lambda i, j, k: (k, i)) if dims == "tn" else pl.BlockSpec((tm, tk), lambda i, j, k: (i, k))
    b_spec = pl.BlockSpec((tn, tk), lambda i, j, k: (j, k)) if dims == "nt" else pl.BlockSpec((tk, tn), lambda i, j, k: (k, j))
    contract = {"nn": ((1,), (0,)), "nt": ((1,), (1,)), "tn": ((0,), (0,))}[dims]
    n_ex, n_out = len(extras), len(out_dtypes)

    def body(a_ref, b_ref, *rest):
        ex, outs, acc = rest[:n_ex], rest[n_ex:n_ex + n_out], rest[-1]
        k = pl.program_id(2)

        @pl.when(k == 0)
        def _():
            acc[...] = jnp.zeros_like(acc)

        acc[...] += lax.dot_general(a_ref[...].astype(BF16), b_ref[...].astype(BF16), (contract, ((), ())),
                                    preferred_element_type=F32)

        @pl.when(k == nk - 1)
        def _():
            r = acc[...]
            res = epilogue(r, *[e[...] for e in ex]) if epilogue is not None else (r,)
            for o, v in zip(outs, res):
                o[...] = v.astype(o.dtype)

    out = pl.pallas_call(
        body, name=name, grid=(M // tm, N // tn, nk),
        in_specs=[a_spec, b_spec] + [pl.BlockSpec(blk, im) for (_, blk, im) in extras],
        out_specs=[pl.BlockSpec((tm, tn), lambda i, j, k: (i, j)) for _ in out_dtypes],
        out_shape=[jax.ShapeDtypeStruct((M, N), d) for d in out_dtypes],
        scratch_shapes=[pltpu.VMEM((tm, tn), F32)],
        compiler_params=_cp(("parallel", "parallel", "arbitrary")),
    )(a, b, *[e[0] for e in extras])
    return out if n_out > 1 else out[0]


def _mn_extra(arr):
    return lambda tm, tn: (arr, (tm, tn), lambda i, j, k: (i, j))


def _vec_extra(vec, S):
    return lambda tm, tn: (vec, (None, 1, tn), lambda i, j, k: ((i * tm) // S, 0, j))


def _rowwise(fn, rows, vecs=(), consts=(), out_rows=(), out_sums=(), *, n_ex, name, tr=256):
    rows = [r if len(r) == 4 else (*r, 0) for r in rows]
    S = min(r[0].shape[0] for r in rows if r[3] == 0) // n_ex
    tr = math.gcd(tr, S)
    assert S % tr == 0
    nb = S // tr
    in_specs = []
    for (arr, w, cb, roff) in rows:
        assert roff % tr == 0
        in_specs.append(pl.BlockSpec((tr, w), functools.partial(lambda e, i, cb, ro: (e * nb + i + ro, cb), cb=cb, ro=roff // tr)))
    for v in vecs:
        in_specs.append(pl.BlockSpec((None, 1, v.shape[-1]), lambda e, i: (e, 0, 0)))
    for c in consts:
        in_specs.append(pl.BlockSpec((1, c.shape[-1]), lambda e, i: (0, 0)))
    n_in, n_or, n_os = len(in_specs), len(out_rows), len(out_sums)
    out_specs = [pl.BlockSpec((tr, w), lambda e, i: (e * nb + i, 0)) for (w, _) in out_rows]
    out_specs += [pl.BlockSpec((None, 1, w), lambda e, i: (e, 0, 0)) for w in out_sums]
    out_shape = [jax.ShapeDtypeStruct((n_ex * S, w), d) for (w, d) in out_rows]
    out_shape += [jax.ShapeDtypeStruct((n_ex, 1, w), F32) for w in out_sums]

    def body(*refs):
        ins, o_r, o_s = refs[:n_in], refs[n_in:n_in + n_or], refs[n_in + n_or:]
        ro, so = fn(*[r[...] for r in ins])
        for o, v in zip(o_r, ro):
            o[...] = v.astype(o.dtype)
        i = pl.program_id(1)
        for o, v in zip(o_s, so):
            @pl.when(i == 0)
            def _(o=o, v=v):
                o[...] = v

            @pl.when(i > 0)
            def _(o=o, v=v):
                o[...] += v

    outs = pl.pallas_call(
        body, name=name, grid=(n_ex, nb), in_specs=in_specs, out_specs=out_specs, out_shape=out_shape,
        compiler_params=_cp(("parallel", "arbitrary")),
    )(*[r[0] for r in rows], *vecs, *consts)
    return outs[:n_or], outs[n_or:]


def _csum(x):
    return jnp.sum(x, axis=0, keepdims=True)


def _norm_mod_fwd(x, g, sh, sc, *, n_ex, out_dtype, name):
    def fn(xt, sht, sct, gt):
        r = lax.rsqrt(jnp.mean(xt * xt, axis=-1, keepdims=True) + EPS)
        return [(xt * r * gt) * (1.0 + sct) + sht], []
    D = x.shape[1]
    return _rowwise(fn, [(x, D, 0)], [sh, sc], [g], [(D, out_dtype)], [], n_ex=n_ex, name=name)[0][0]


def _norm_mod_bwd(x, dh, dres, g, sc, *, n_ex, name):
    def fn(xt, dht, drt, sct, gt):
        dht = dht.astype(F32)
        r = lax.rsqrt(jnp.mean(xt * xt, axis=-1, keepdims=True) + EPS)
        n = xt * r
        y = n * gt
        dy = dht * (1.0 + sct)
        dn = dy * gt
        dx = r * (dn - n * jnp.mean(dn * n, axis=-1, keepdims=True))
        return [drt + dx], [_csum(dht), _csum(dht * y), _csum(dy * n)]
    D = x.shape[1]
    return _rowwise(fn, [(x, D, 0), (dh, D, 0), (dres, D, 0)], [sc], [g], [(D, F32)], [D, D, D], n_ex=n_ex, name=name)


def _sigmoid(x):
    return 1.0 / (1.0 + jnp.exp(-x))


def _gelu(y):
    return 0.5 * y * (1.0 + jnp.tanh(0.7978845608028654 * (y + 0.044715 * y * y * y)))


def _gelu_grad(y):
    t = jnp.tanh(0.7978845608028654 * (y + 0.044715 * y * y * y))
    return 0.5 * (1.0 + t) + 0.5 * y * (1.0 - t * t) * 0.7978845608028654 * (1.0 + 3 * 0.044715 * y * y)


def _adamw_fn(w, g, m, v):
    m2 = ADAM_B1 * m + (1.0 - ADAM_B1) * g
    v2 = ADAM_B2 * v + (1.0 - ADAM_B2) * (g * g)
    m_hat = m2 / (1.0 - ADAM_B1 ** ADAM_STEP)
    v_hat = v2 / (1.0 - ADAM_B2 ** ADAM_STEP)
    delta = -ADAM_LR * (m_hat / (jnp.sqrt(v_hat) + ADAM_EPS) + ADAM_WD * w)
    return delta, m2, v2


def _adamw2d(w, g, m, v, *, name, g_roff=0, g_cb=0):
    R, W = w.shape

    def fn(wt, gt, mt, vt):
        d, m2, v2 = _adamw_fn(wt, gt, mt, vt)
        return [d, m2, v2, gt], []
    return _rowwise(fn, [(w, W, 0), (g, W, g_cb, g_roff), (m, W, 0), (v, W, 0)], [], [],
                    [(W, F32)] * 4, [], n_ex=1, name=name, tr=256)[0]


def _scan_tiles(re_ref, im_ref, cf, lane0, n_chunks, reverse, extra=None):
    L = SCAN_LANES
    lanes = pl.ds(lane0, L)
    A = [cf[i, :, lanes] for i in range(8)]
    shifts = (7, 6, 4) if reverse else (1, 2, 4)
    edge = 0 if reverse else 7

    def body(c, carry):
        cr, ci = carry[0], carry[1]
        cc = (n_chunks - 1 - c) if reverse else c
        rows = pl.ds(pl.multiple_of(cc * 8, 8), 8)
        xr, xi = re_ref[rows, lanes], im_ref[rows, lanes]
        for idx, sft in enumerate(shifts):
            ar, ai = A[2 * idx], A[2 * idx + 1]
            rr, ri = pltpu.roll(xr, sft, 0), pltpu.roll(xi, sft, 0)
            xr, xi = xr + ar * rr - ai * ri, xi + ar * ri + ai * rr
        pr, pi = A[6], A[7]
        xr, xi = xr + pr * cr - pi * ci, xi + pr * ci + pi * cr
        re_ref[rows, lanes] = xr
        im_ref[rows, lanes] = xi
        out = (jnp.broadcast_to(xr[edge:edge + 1, :], (8, L)), jnp.broadcast_to(xi[edge:edge + 1, :], (8, L)))
        if extra is not None:
            out = out + extra(cc, xr, xi, carry[2:])
        return out

    z = jnp.zeros((8, L), F32)
    init = (z, z) if extra is None else (z, z, z, z)
    return lax.fori_loop(0, n_chunks, body, init)


def _s5_consts(ab_re, ab_im):
    ng = ab_re.shape[0] // GROUPS_PER_STEP
    ar, ai = ab_re.reshape(ng, 1, ST_LANES), ab_im.reshape(ng, 1, ST_LANES)

    def cmul(xr, xi, yr, yi):
        return xr * yr - xi * yi, xr * yi + xi * yr

    def build(ar, ai, reverse):
        pw = [(ar, ai)]
        for _ in range(7):
            pw.append(cmul(*pw[-1], ar, ai))
        row = jnp.arange(8).reshape(1, 8, 1)
        tiles = []
        for k in (1, 2, 4):
            keep = (row <= 7 - k) if reverse else (row >= k)
            tiles += [jnp.where(keep, pw[k - 1][0], 0.0), jnp.where(keep, pw[k - 1][1], 0.0)]
        order = [7 - r for r in range(8)] if reverse else list(range(8))
        tiles += [jnp.concatenate([pw[o][0] for o in order], axis=1), jnp.concatenate([pw[o][1] for o in order], axis=1)]
        return jnp.stack([jnp.broadcast_to(t, (ng, 8, ST_LANES)) for t in tiles], axis=1)

    return build(ar, ai, False), build(ar, -ai, True)


def _s5_blockdiag(bb_re, bb_im, c_re, c_im):
    G = bb_re.shape[0]
    ng = G // GROUPS_PER_STEP
    eye = jnp.eye(GROUPS_PER_STEP, dtype=F32)

    def wb(bb):
        return jnp.einsum("bgph,gk->bghkp", bb.reshape(ng, GROUPS_PER_STEP, S5_STATE, S5_GROUP), eye).reshape(ng, U_LANES, ST_LANES)

    def wc(cc):
        return jnp.einsum("bghp,gk->bkpgh", cc.reshape(ng, GROUPS_PER_STEP, S5_GROUP, S5_STATE), eye).reshape(ng, ST_LANES, U_LANES)

    Wb = jnp.concatenate([wb(bb_re), wb(bb_im)], axis=2).astype(BF16)
    Wc = jnp.concatenate([wc(c_re), -wc(c_im)], axis=1).astype(BF16)
    return Wb, Wc


def _s5_unblock(dWb, dWc):
    ng = dWb.shape[0]
    eye = jnp.eye(GROUPS_PER_STEP, dtype=F32)

    def ub(w):
        return jnp.einsum("bghkp,gk->bgph", w.reshape(ng, GROUPS_PER_STEP, S5_GROUP, GROUPS_PER_STEP, S5_STATE), eye).reshape(-1, S5_STATE, S5_GROUP)

    def uc(w):
        return jnp.einsum("bkpgh,gk->bghp", w.reshape(ng, GROUPS_PER_STEP, S5_STATE, GROUPS_PER_STEP, S5_GROUP), eye).reshape(-1, S5_GROUP, S5_STATE)

    return ub(dWb[:, :, :ST_LANES]), ub(dWb[:, :, ST_LANES:]), uc(dWc[:, :ST_LANES, :]), -uc(dWc[:, ST_LANES:, :])


def _s5_disc(a_re, a_im, log_dt, b_re, b_im):
    dt = jnp.exp(log_dt)[:, None]
    mag = jnp.exp(a_re * dt)
    ab_re = mag * jnp.cos(a_im * dt)
    ab_im = mag * jnp.sin(a_im * dt)
    den = a_re * a_re + a_im * a_im
    nr, ni = ab_re - 1, ab_im
    f_re = (nr * a_re + ni * a_im) / den
    f_im = (ni * a_re - nr * a_im) / den
    bb_re = f_re[..., None] * b_re - f_im[..., None] * b_im
    bb_im = f_re[..., None] * b_im + f_im[..., None] * b_re
    return ab_re, ab_im, bb_re, bb_im


ROW_CHUNK = 512


def _s5_fwd(u, Wb, Wc, cf, d, *, n_ex, name):
    T, D = u.shape
    S = T // n_ex
    ng = D // U_LANES
    rc = min(ROW_CHUNK, S)

    def body(u_ref, wb_ref, wc_ref, cf_ref, d_ref, y_ref, gy_ref, re_s, im_s):
        for r in range(S // rc):
            rows = pl.ds(r * rc, rc)
            bu = jnp.dot(u_ref[rows, :].astype(BF16), wb_ref[...], preferred_element_type=F32)
            re_s[rows, :] = bu[:, :ST_LANES]
            im_s[rows, :] = bu[:, ST_LANES:]
        for l0 in range(0, ST_LANES, SCAN_LANES):
            _scan_tiles(re_s, im_s, cf_ref, l0, S // 8, False)
        for r in range(S // rc):
            rows = pl.ds(r * rc, rc)
            st = jnp.concatenate([re_s[rows, :], im_s[rows, :]], axis=1).astype(BF16)
            y = jnp.dot(st, wc_ref[...], preferred_element_type=F32) + d_ref[...] * u_ref[rows, :]
            y_ref[rows, :] = y
            gy_ref[rows, :] = _gelu(y).astype(BF16)

    return pl.pallas_call(
        body, name=name, grid=(n_ex, ng),
        in_specs=[pl.BlockSpec((S, U_LANES), lambda e, g: (e, g)),
                  pl.BlockSpec((None, U_LANES, 2 * ST_LANES), lambda e, g: (g, 0, 0)),
                  pl.BlockSpec((None, 2 * ST_LANES, U_LANES), lambda e, g: (g, 0, 0)),
                  pl.BlockSpec((None, 8, 8, ST_LANES), lambda e, g: (g, 0, 0, 0)),
                  pl.BlockSpec((1, U_LANES), lambda e, g: (0, g))],
        out_specs=[pl.BlockSpec((S, U_LANES), lambda e, g: (e, g))] * 2,
        out_shape=[jax.ShapeDtypeStruct((T, D), F32), jax.ShapeDtypeStruct((T, D), BF16)],
        scratch_shapes=[pltpu.VMEM((S, ST_LANES), F32)] * 2,
        compiler_params=_cp(("parallel", "parallel")),
    )(u, Wb, Wc, cf, d)


def _s5_bwd(u, y, dgy, Wb, Wc, cf, cr, d, *, n_ex, name):
    T, D = u.shape
    S = T // n_ex
    ng = D // U_LANES
    rc = min(ROW_CHUNK, S)
    nch = S // 8

    def body(u_ref, y_ref, dgy_ref, wb_ref, wc_ref, cf_ref, cr_ref, d_ref,
             du_ref, dwb_ref, dwc_ref, dab_ref, dd_ref, re_s, im_s, gr_s, gi_s, dy_s):
        e = pl.program_id(1)

        @pl.when(e == 0)
        def _():
            dwb_ref[...] = jnp.zeros_like(dwb_ref)
            dwc_ref[...] = jnp.zeros_like(dwc_ref)
            dab_ref[...] = jnp.zeros_like(dab_ref)
            dd_ref[...] = jnp.zeros_like(dd_ref)

        dd = jnp.zeros((1, U_LANES), F32)
        for r in range(S // rc):
            rows = pl.ds(r * rc, rc)
            ut = u_ref[rows, :]
            bu = jnp.dot(ut.astype(BF16), wb_ref[...], preferred_element_type=F32)
            re_s[rows, :] = bu[:, :ST_LANES]
            im_s[rows, :] = bu[:, ST_LANES:]
            dy = dgy_ref[rows, :].astype(F32) * _gelu_grad(y_ref[rows, :])
            dy_s[rows, :] = dy
            dd = dd + _csum(dy * ut)
            go = lax.dot_general(dy.astype(BF16), wc_ref[...], (((1,), (1,)), ((), ())), preferred_element_type=F32)
            gr_s[rows, :] = go[:, :ST_LANES]
            gi_s[rows, :] = go[:, ST_LANES:]
        dd_ref[0:1, :] += dd
        row0 = lax.broadcasted_iota(jnp.int32, (8, SCAN_LANES), 0) == 0
        for l0 in range(0, ST_LANES, SCAN_LANES):
            lanes = pl.ds(l0, SCAN_LANES)
            _scan_tiles(re_s, im_s, cf_ref, l0, nch, False)

            def dab_part(cc, gr, gi, acc, lanes=lanes):
                rows = pl.ds(pl.multiple_of(cc * 8, 8), 8)
                prev = pl.ds(pl.multiple_of(jnp.maximum(cc - 1, 0) * 8, 8), 8)
                live = (cc > 0).astype(F32)
                sr = jnp.where(row0, pltpu.roll(re_s[prev, lanes], 1, 0) * live, pltpu.roll(re_s[rows, lanes], 1, 0))
                si = jnp.where(row0, pltpu.roll(im_s[prev, lanes], 1, 0) * live, pltpu.roll(im_s[rows, lanes], 1, 0))
                return (acc[0] + gr * sr + gi * si, acc[1] + gi * sr - gr * si)

            res = _scan_tiles(gr_s, gi_s, cr_ref, l0, nch, True, extra=dab_part)
            dab_ref[0:1, lanes] += _csum(res[2])
            dab_ref[1:2, lanes] += _csum(res[3])
        for r in range(S // rc):
            rows = pl.ds(r * rc, rc)
            st = jnp.concatenate([re_s[rows, :], im_s[rows, :]], axis=1).astype(BF16)
            g = jnp.concatenate([gr_s[rows, :], gi_s[rows, :]], axis=1).astype(BF16)
            dyb = dy_s[rows, :].astype(BF16)
            dwc_ref[...] += lax.dot_general(st, dyb, (((0,), (0,)), ((), ())), preferred_element_type=F32)
            dwb_ref[...] += lax.dot_general(u_ref[rows, :].astype(BF16), g, (((0,), (0,)), ((), ())), preferred_element_type=F32)
            du = lax.dot_general(g, wb_ref[...], (((1,), (1,)), ((), ())), preferred_element_type=F32)
            du_ref[rows, :] = du + d_ref[...] * dy_s[rows, :]

    return pl.pallas_call(
        body, name=name, grid=(ng, n_ex),
        in_specs=[pl.BlockSpec((S, U_LANES), lambda g, e: (e, g))] * 3 + [
            pl.BlockSpec((None, U_LANES, 2 * ST_LANES), lambda g, e: (g, 0, 0)),
            pl.BlockSpec((None, 2 * ST_LANES, U_LANES), lambda g, e: (g, 0, 0)),
            pl.BlockSpec((None, 8, 8, ST_LANES), lambda g, e: (g, 0, 0, 0)),
            pl.BlockSpec((None, 8, 8, ST_LANES), lambda g, e: (g, 0, 0, 0)),
            pl.BlockSpec((1, U_LANES), lambda g, e: (0, g))],
        out_specs=[pl.BlockSpec((S, U_LANES), lambda g, e: (e, g)),
                   pl.BlockSpec((None, U_LANES, 2 * ST_LANES), lambda g, e: (g, 0, 0)),
                   pl.BlockSpec((None, 2 * ST_LANES, U_LANES), lambda g, e: (g, 0, 0)),
                   pl.BlockSpec((None, 8, ST_LANES), lambda g, e: (g, 0, 0)),
                   pl.BlockSpec((None, 8, U_LANES), lambda g, e: (g, 0, 0))],
        out_shape=[jax.ShapeDtypeStruct((T, D), F32),
                   jax.ShapeDtypeStruct((ng, U_LANES, 2 * ST_LANES), F32),
                   jax.ShapeDtypeStruct((ng, 2 * ST_LANES, U_LANES), F32),
                   jax.ShapeDtypeStruct((ng, 8, ST_LANES), F32),
                   jax.ShapeDtypeStruct((ng, 8, U_LANES), F32)],
        scratch_shapes=[pltpu.VMEM((S, ST_LANES), F32)] * 4 + [pltpu.VMEM((S, U_LANES), F32)],
        compiler_params=_cp(("parallel", "arbitrary")),
    )(u, y, dgy, Wb, Wc, cf, cr, d)


TQ = 256
KW = 512
SUB = 128


def _head_masks():
    lane = lax.broadcasted_iota(jnp.int32, (1, 2 * HEAD_DIM), 1)
    m0 = (lane < HEAD_DIM).astype(F32)
    return m0, 1.0 - m0


def _head_norm(x, g, m0, m1):
    sq = x * x
    r0 = lax.rsqrt(jnp.sum(sq * m0, axis=-1, keepdims=True) / HEAD_DIM + EPS)
    r1 = lax.rsqrt(jnp.sum(sq * m1, axis=-1, keepdims=True) / HEAD_DIM + EPS)
    r = m0 * r0 + m1 * r1
    return x * r, r


def _head_norm_bwd(dy, n, r, g, m0, m1):
    dn = dy * g
    p = dn * n
    mean = (m0 * jnp.sum(p * m0, axis=-1, keepdims=True) + m1 * jnp.sum(p * m1, axis=-1, keepdims=True)) / HEAD_DIM
    return r * (dn - n * mean), _csum(dy * n)


def _pair_matrix(kind):
    r = lax.broadcasted_iota(jnp.int32, (2 * SUB, 2 * SUB), 0)
    c = lax.broadcasted_iota(jnp.int32, (2 * SUB, 2 * SUB), 1)
    same = (r < SUB) == (c < SUB)
    rel = {"after": r > c, "upto": r <= c, "before": r < c}[kind]
    return jnp.logical_and(same, rel).astype(BF16)


def _block_sums(x, mat, carry, reverse):
    hi = x.astype(BF16)
    lo = (x - hi.astype(F32)).astype(BF16)
    npair = KW // (2 * SUB)
    parts = [None] * (2 * npair)
    for p in (range(npair - 1, -1, -1) if reverse else range(npair)):
        sl = slice(2 * SUB * p, 2 * SUB * (p + 1))
        loc = jnp.dot(hi[:, sl], mat, preferred_element_type=F32) + jnp.dot(lo[:, sl], mat, preferred_element_type=F32)
        for b in ((1, 0) if reverse else (0, 1)):
            k = 2 * p + b
            parts[k] = loc[:, SUB * b:SUB * (b + 1)] + carry
            carry = carry + jnp.sum(x[:, SUB * k:SUB * (k + 1)], axis=-1, keepdims=True)
    return jnp.concatenate(parts, axis=1), carry


def _sb_logits(qh, kT, mask):
    z = jnp.dot(qh, kT, preferred_element_type=F32)
    lp = jnp.minimum(z, 0.0) - jnp.log(1.0 + jnp.exp(-jnp.abs(z)))
    lf = lp - z
    if mask is not None:
        lf = jnp.where(mask, lf, 0.0)
    return lp, lf


def _causal_mask(row0, col0):
    r = row0 + lax.broadcasted_iota(jnp.int32, (TQ, KW), 0)
    c = col0 + lax.broadcasted_iota(jnp.int32, (TQ, KW), 1)
    return c < r


def _transposed_windows(x, ref):
    for w in range(x.shape[0] // KW):
        ref[w] = x[w * KW:(w + 1) * KW, :].T.astype(BF16)


def _attn_fwd(q, kv, qg, kg, *, n_ex, name):
    T, D = q.shape
    S = T // n_ex
    nhp = D // (2 * HEAD_DIM)
    nq = S // TQ
    scale = 1.0 / math.sqrt(HEAD_DIM)

    def body(q_ref, k_ref, v_ref, qg_ref, kg_ref, o_ref, tot_ref, kT_s, qm_s, vm_s):
        m0, m1 = _head_masks()
        qn, _ = _head_norm(q_ref[...], None, m0, m1)
        qn = qn * (qg_ref[...] * scale)
        kn, _ = _head_norm(k_ref[...], None, m0, m1)
        _transposed_windows(kn * kg_ref[...], kT_s)
        v = v_ref[...]
        for h, m in enumerate((m0, m1)):
            qm_s[h] = (qn * m).astype(BF16)
            vm_s[h] = (v * m).astype(BF16)
        u_after = _pair_matrix("after")

        def window(h, rows, win, carry, acc, mask):
            lp, lf = _sb_logits(qm_s[h, rows, :], kT_s[win], mask)
            after, carry = _block_sums(lf, u_after, carry, True)
            w = jnp.exp(lp + after)
            if mask is not None:
                w = jnp.where(mask, w, 0.0)
            keys = pl.ds(pl.multiple_of(win * KW, KW), KW)
            return carry, acc + jnp.dot(w.astype(BF16), vm_s[h, keys, :], preferred_element_type=F32)

        def qtile(iq, _):
            rows = pl.ds(pl.multiple_of(iq * TQ, TQ), TQ)
            last = (iq * TQ) // KW
            mask = _causal_mask(iq * TQ, last * KW)
            st = ()
            for h in range(2):
                st += window(h, rows, last, jnp.zeros((TQ, 1), F32), jnp.zeros((TQ, 2 * HEAD_DIM), F32), mask)

            def full(jj, st):
                out = ()
                for h in range(2):
                    out += window(h, rows, last - 1 - jj, st[2 * h], st[2 * h + 1], None)
                return out

            st = lax.fori_loop(0, last, full, st)
            o_ref[rows, :] = st[1] + st[3]
            tot_ref[rows, :] = st[0] * m0 + st[2] * m1
            return 0

        lax.fori_loop(0, nq, qtile, 0)

    assert S % KW == 0 and KW % TQ == 0
    nwin = S // KW
    blk = (S, 2 * HEAD_DIM)
    return pl.pallas_call(
        body, name=name, grid=(n_ex, nhp),
        in_specs=[pl.BlockSpec(blk, lambda e, h: (e, h)), pl.BlockSpec(blk, lambda e, h: (e, h)),
                  pl.BlockSpec(blk, lambda e, h: (e, h + nhp)),
                  pl.BlockSpec((1, 2 * HEAD_DIM), lambda e, h: (0, 0)), pl.BlockSpec((1, 2 * HEAD_DIM), lambda e, h: (0, 0))],
        out_specs=[pl.BlockSpec(blk, lambda e, h: (e, h))] * 2,
        out_shape=[jax.ShapeDtypeStruct((T, D), F32)] * 2,
        scratch_shapes=[pltpu.VMEM((nwin, 2 * HEAD_DIM, KW), BF16), pltpu.VMEM((2,) + blk, BF16), pltpu.VMEM((2,) + blk, BF16)],
        compiler_params=_cp(("parallel", "parallel")),
    )(q, kv, kv, qg, kg)


def _attn_bwd(q, kv, tot, do, qg, kg, *, n_ex, name):
    T, D = q.shape
    S = T // n_ex
    nhp = D // (2 * HEAD_DIM)
    nq = S // TQ
    scale = 1.0 / math.sqrt(HEAD_DIM)

    def body(q_ref, k_ref, v_ref, tot_ref, do_ref, qg_ref, kg_ref, dq_ref, dk_ref, dv_ref, dqg_ref, dkg_ref,
             kT_s, vT_s, km_s, qm_s, dom_s, dqn_s, dkT_s, dvT_s):
        m0, m1 = _head_masks()
        qn, qr = _head_norm(q_ref[...], None, m0, m1)
        kn, kr = _head_norm(k_ref[...], None, m0, m1)
        qs = qn * (qg_ref[...] * scale)
        kk = kn * kg_ref[...]
        _transposed_windows(kk, kT_s)
        _transposed_windows(v_ref[...], vT_s)
        do = do_ref[...]
        for h, m in enumerate((m0, m1)):
            qm_s[h] = (qs * m).astype(BF16)
            km_s[h] = (kk * m).astype(BF16)
            dom_s[h] = (do * m).astype(BF16)
        dkT_s[...] = jnp.zeros_like(dkT_s)
        dvT_s[...] = jnp.zeros_like(dvT_s)
        u_upto, u_before = _pair_matrix("upto"), _pair_matrix("before")

        def window(h, inv, win, pre_lf, pre_e, dq, mask):
            qh, qhT, doh, dohT, total = inv
            lp, lf = _sb_logits(qh, kT_s[win], mask)
            upto, pre_lf = _block_sums(lf, u_upto, pre_lf, False)
            w = jnp.exp(lp + (total - upto))
            if mask is not None:
                w = jnp.where(mask, w, 0.0)
            ew = jnp.dot(doh, vT_s[win], preferred_element_type=F32) * w
            dlf, pre_e = _block_sums(ew, u_before, pre_e, False)
            sig = jnp.exp(lp)
            dz = ew * (1.0 - sig) - dlf * sig
            if mask is not None:
                dz = jnp.where(mask, dz, 0.0)
            dzb = dz.astype(BF16)
            keys = pl.ds(pl.multiple_of(win * KW, KW), KW)
            dq = dq + jnp.dot(dzb, km_s[h, keys, :], preferred_element_type=F32)
            dkT_s[win] += jnp.dot(qhT, dzb, preferred_element_type=F32)
            dvT_s[win] += jnp.dot(dohT, w.astype(BF16), preferred_element_type=F32)
            return pre_lf, pre_e, dq

        def qtile(iq, _):
            rows = pl.ds(pl.multiple_of(iq * TQ, TQ), TQ)
            last = (iq * TQ) // KW
            mask = _causal_mask(iq * TQ, last * KW)
            tt = tot_ref[rows, :]
            inv = []
            for h, m in enumerate((m0, m1)):
                qh, doh = qm_s[h, rows, :], dom_s[h, rows, :]
                total = jnp.sum(tt * m, axis=-1, keepdims=True) * (1.0 / HEAD_DIM)
                inv.append((qh, qh.astype(F32).T.astype(BF16), doh, doh.astype(F32).T.astype(BF16), total))

            def both(win, st, mask):
                out = ()
                for h in range(2):
                    out += window(h, inv[h], win, st[3 * h], st[3 * h + 1], st[3 * h + 2], mask)
                return out

            z1, zq = jnp.zeros((TQ, 1), F32), jnp.zeros((TQ, 2 * HEAD_DIM), F32)
            st = lax.fori_loop(0, last, lambda win, st: both(win, st, None), (z1, z1, zq, z1, z1, zq))
            st = both(last, st, mask)
            dqn_s[rows, :] = st[2] + st[5]
            return 0

        lax.fori_loop(0, nq, qtile, 0)
        dkn = jnp.concatenate([dkT_s[w].T for w in range(nwin)], axis=0)
        dq, dqg = _head_norm_bwd(dqn_s[...] * scale, qn, qr, qg_ref[...], m0, m1)
        dk, dkg = _head_norm_bwd(dkn, kn, kr, kg_ref[...], m0, m1)
        dq_ref[...] = dq
        dk_ref[...] = dk
        dv_ref[...] = jnp.concatenate([dvT_s[w].T for w in range(nwin)], axis=0)
        dqg_ref[...] = dqg
        dkg_ref[...] = dkg

    assert S % KW == 0 and KW % TQ == 0
    nwin = S // KW
    blk = (S, 2 * HEAD_DIM)
    tblk = (nwin, 2 * HEAD_DIM, KW)
    gblk = (None, None, 1, 2 * HEAD_DIM)
    dq, dk, dv, dqg, dkg = pl.pallas_call(
        body, name=name, grid=(n_ex, nhp),
        in_specs=[pl.BlockSpec(blk, lambda e, h: (e, h)), pl.BlockSpec(blk, lambda e, h: (e, h)),
                  pl.BlockSpec(blk, lambda e, h: (e, h + nhp)),
                  pl.BlockSpec(blk, lambda e, h: (e, h)), pl.BlockSpec(blk, lambda e, h: (e, h)),
                  pl.BlockSpec((1, 2 * HEAD_DIM), lambda e, h: (0, 0)), pl.BlockSpec((1, 2 * HEAD_DIM), lambda e, h: (0, 0))],
        out_specs=[pl.BlockSpec(blk, lambda e, h: (e, h))] * 3 + [pl.BlockSpec(gblk, lambda e, h: (e, h, 0, 0))] * 2,
        out_shape=[jax.ShapeDtypeStruct((T, D), F32)] * 3 + [jax.ShapeDtypeStruct((n_ex, nhp, 1, 2 * HEAD_DIM), F32)] * 2,
        scratch_shapes=[pltpu.VMEM(tblk, BF16), pltpu.VMEM(tblk, BF16),
                        pltpu.VMEM((2,) + blk, BF16), pltpu.VMEM((2,) + blk, BF16), pltpu.VMEM((2,) + blk, BF16),
                        pltpu.VMEM(blk, F32), pltpu.VMEM(tblk, F32), pltpu.VMEM(tblk, F32)],
        compiler_params=_cp(("parallel", "parallel")),
    )(q, kv, kv, tot, do, qg, kg)
    return dq, dk, dv, dqg, dkg


def _place():
    return lax.axis_index("x"), lax.axis_index("y"), lax.axis_index("c")


def _all_gather8(x_shard, *, name):
    m_per, n = x_shard.shape

    def body(x_ref, out_ref, send_sems, recv_sems, local_sem):
        x, y, c = _place()
        me, sibling = (x, y, c), (x, y, 1 - c)
        chips = [(1 - x, y), (x, 1 - y), (1 - x, 1 - y)]

        def rows(px, py, pc):
            return out_ref.at[pl.ds((4 * px + 2 * py + pc) * m_per, m_per), :]

        def copy(k, block, to, src=None):
            return pltpu.make_async_remote_copy(
                src_ref=rows(*block) if src is None else src, dst_ref=rows(*block),
                send_sem=send_sems.at[k], recv_sem=recv_sems.at[k], device_id=to, device_id_type=MESH)

        mine = pltpu.make_async_copy(x_ref, rows(*me), local_sem)
        mine.start()
        first = [copy(0, me, sibling, src=x_ref)]
        first += [copy(1 + j, me, (*chip, c), src=x_ref) for j, chip in enumerate(chips)]
        for cp in first:
            cp.start()
        passed = [copy(4 + j, (*chip, c), sibling) for j, chip in enumerate(chips)]
        for j, chip in enumerate(chips):
            copy(1 + j, (*chip, c), me).wait_recv()
            passed[j].start()
        copy(0, sibling, me).wait_recv()
        for j, chip in enumerate(chips):
            copy(4 + j, (*chip, 1 - c), me).wait_recv()
        for cp in first + passed:
            cp.wait_send()
        mine.wait()

    return pl.pallas_call(
        body, name=name, out_shape=jax.ShapeDtypeStruct((8 * m_per, n), x_shard.dtype),
        in_specs=[pl.BlockSpec(memory_space=pltpu.VMEM)], out_specs=pl.BlockSpec(memory_space=pltpu.VMEM),
        scratch_shapes=[pltpu.SemaphoreType.DMA((7,)), pltpu.SemaphoreType.DMA((7,)), pltpu.SemaphoreType.DMA],
        compiler_params=pltpu.CompilerParams(vmem_limit_bytes=VMEM_LIMIT),
    )(x_shard)


def _sum_blocks(x, n, *, name):
    R = x.shape[0] // n

    def body(x_ref, o_ref):
        acc = x_ref[pl.ds(0, R), :]
        for k in range(1, n):
            acc = acc + x_ref[pl.ds(k * R, R), :]
        o_ref[...] = acc

    return pl.pallas_call(body, name=name, out_shape=jax.ShapeDtypeStruct((R, x.shape[1]), x.dtype),
                          compiler_params=pltpu.CompilerParams(vmem_limit_bytes=VMEM_LIMIT))(x)


def _colsum(x, *, name):
    def body(x_ref, o_ref):
        o_ref[...] = jnp.sum(x_ref[...], axis=0, keepdims=True)
    return pl.pallas_call(body, name=name, out_shape=jax.ShapeDtypeStruct((1, x.shape[1]), x.dtype))(x)


ANY = pl.BlockSpec(memory_space=pl.ANY)


def _chip_exchange(src, *, scatter, name):
    blk = src.shape[1:]

    def body(src_ref, out_ref, send_sems, recv_sems, local_sem):
        x, y, c = _place()
        myj = 2 * x + y
        chips = [(1 - x, y), (x, 1 - y), (1 - x, 1 - y)]

        def slot(j):
            return out_ref.at[j] if scatter else out_ref.at[j, c]

        def piece(j):
            return src_ref.at[j] if scatter else src_ref.at[c]

        mine = pltpu.make_async_copy(piece(myj), slot(myj), local_sem)
        mine.start()
        sends = []
        for k, (cx, cy) in enumerate(chips):
            sends.append(pltpu.make_async_remote_copy(
                src_ref=piece(2 * cx + cy), dst_ref=slot(myj),
                send_sem=send_sems.at[k], recv_sem=recv_sems.at[k], device_id=(cx, cy, c), device_id_type=MESH))
            sends[-1].start()
        for k, (cx, cy) in enumerate(chips):
            pltpu.make_async_remote_copy(
                src_ref=slot(2 * cx + cy), dst_ref=slot(2 * cx + cy),
                send_sem=send_sems.at[k], recv_sem=recv_sems.at[k], device_id=(cx, cy, c), device_id_type=MESH).wait_recv()
        for cp in sends:
            cp.wait_send()
        mine.wait()

    return pl.pallas_call(
        body, name=name, out_shape=jax.ShapeDtypeStruct(((4,) + tuple(blk)) if scatter else ((4, 2) + tuple(blk)), src.dtype),
        in_specs=[ANY], out_specs=ANY,
        scratch_shapes=[pltpu.SemaphoreType.DMA((3,)), pltpu.SemaphoreType.DMA((3,)), pltpu.SemaphoreType.DMA],
    )(src)


def _sibling_fill(buf, *, axis, name):
    def half(ref, h):
        return ref.at[h] if axis == 0 else ref.at[:, h]

    def body(in_ref, out_ref, send_sem, recv_sem):
        x, y, c = _place()
        cp = pltpu.make_async_remote_copy(src_ref=half(out_ref, c), dst_ref=half(out_ref, c), send_sem=send_sem, recv_sem=recv_sem,
                                          device_id=(x, y, 1 - c), device_id_type=MESH)
        cp.start()
        pltpu.make_async_remote_copy(src_ref=half(out_ref, 1 - c), dst_ref=half(out_ref, 1 - c), send_sem=send_sem, recv_sem=recv_sem,
                                     device_id=(x, y, 1 - c), device_id_type=MESH).wait_recv()
        cp.wait_send()

    return pl.pallas_call(
        body, name=name, out_shape=jax.ShapeDtypeStruct(buf.shape, buf.dtype), in_specs=[ANY], out_specs=ANY,
        input_output_aliases={0: 0}, scratch_shapes=[pltpu.SemaphoreType.DMA, pltpu.SemaphoreType.DMA],
    )(buf)


def _sibling_swap_half(g, *, name):
    def body(g_ref, out_ref, send_sem, recv_sem):
        x, y, c = _place()
        cp = pltpu.make_async_remote_copy(src_ref=g_ref.at[:, 1 - c], dst_ref=out_ref, send_sem=send_sem, recv_sem=recv_sem,
                                          device_id=(x, y, 1 - c), device_id_type=MESH)
        cp.start()
        cp.wait()

    return pl.pallas_call(
        body, name=name, out_shape=jax.ShapeDtypeStruct((g.shape[0],) + g.shape[2:], g.dtype), in_specs=[ANY], out_specs=ANY,
        scratch_shapes=[pltpu.SemaphoreType.DMA, pltpu.SemaphoreType.DMA],
    )(g)


def _add_my_half(g, b, cidx, *, name, tr=256):
    n, _, R, C = g.shape
    tr = math.gcd(tr, R)

    def body(c_ref, g_ref, b_ref, o_ref):
        o_ref[...] = g_ref[...] + b_ref[...]

    return pl.pallas_call(
        body, name=name, out_shape=jax.ShapeDtypeStruct((n, R, C), g.dtype),
        grid_spec=pltpu.PrefetchScalarGridSpec(
            num_scalar_prefetch=1, grid=(n, R // tr),
            in_specs=[pl.BlockSpec((None, None, tr, C), lambda j, i, c: (j, c[0], i, 0)),
                      pl.BlockSpec((None, tr, C), lambda j, i, c: (j, i, 0))],
            out_specs=pl.BlockSpec((None, tr, C), lambda j, i, c: (j, i, 0))),
        compiler_params=_cp(("parallel", "parallel")),
    )(cidx, g, b)


def _sum4_into_half(q, cidx, *, name, tr=256):
    _, R, C = q.shape
    tr = math.gcd(tr, R)

    def body(c_ref, q_ref, o_ref):
        o_ref[...] = ((q_ref[0] + q_ref[1]) + q_ref[2]) + q_ref[3]

    return pl.pallas_call(
        body, name=name, out_shape=jax.ShapeDtypeStruct((2, R, C), q.dtype),
        grid_spec=pltpu.PrefetchScalarGridSpec(
            num_scalar_prefetch=1, grid=(R // tr,),
            in_specs=[pl.BlockSpec((4, tr, C), lambda i, c: (0, i, 0))],
            out_specs=pl.BlockSpec((None, tr, C), lambda i, c: (c[0], i, 0))),
        compiler_params=_cp(("parallel",)),
    )(cidx, q)


def _pack_rows(parts, width=1024):
    rows, spans, r0 = [], [], 0
    for p in parts:
        n = p.size
        nr = -(-n // width)
        flat = p.reshape(-1)
        if nr * width != n:
            flat = jnp.pad(flat, (0, nr * width - n))
        rows.append(flat.reshape(nr, width))
        spans.append((r0, nr, n, p.shape))
        r0 += nr
    pad = (-r0) % 8
    if pad:
        rows.append(jnp.zeros((pad, width), parts[0].dtype))
    return jnp.concatenate(rows, axis=0), spans


def _unpack_rows(buf, spans):
    return [buf[r0:r0 + nr].reshape(-1)[:n].reshape(shape) for (r0, nr, n, shape) in spans]


def kernel(x, c, ada_w, ada_b, mix_norm_g, mlp_norm_g, mlp_w1, mlp_w2, s5_a_re, s5_a_im, s5_log_dt, s5_b_re, s5_b_im, s5_c_re, s5_c_im, s5_d, s5_w_glu, kv_ada_w, kv_ada_b, kv_norm_g, w_kv, k_norm_g, sb_w_q, q_norm_g, sb_w_o, loss_target, m_ada_w, m_ada_b, m_mix_norm_g, m_mlp_norm_g, m_mlp_w1, m_mlp_w2, m_s5_a_re, m_s5_a_im, m_s5_log_dt, m_s5_b_re, m_s5_b_im, m_s5_c_re, m_s5_c_im, m_s5_d, m_s5_w_glu, m_kv_ada_w, m_kv_ada_b, m_kv_norm_g, m_w_kv, m_k_norm_g, m_sb_w_q, m_q_norm_g, m_sb_w_o, v_ada_w, v_ada_b, v_mix_norm_g, v_mlp_norm_g, v_mlp_w1, v_mlp_w2, v_s5_a_re, v_s5_a_im, v_s5_log_dt, v_s5_b_re, v_s5_b_im, v_s5_c_re, v_s5_c_im, v_s5_d, v_s5_w_glu, v_kv_ada_w, v_kv_ada_b, v_kv_norm_g, v_w_kv, v_k_norm_g, v_sb_w_q, v_q_norm_g, v_sb_w_o):
    E, S, D = x.shape
    T = E * S
    FF = 4 * D
    NB = 8 * E
    px, py, pc = _place()
    chip = 2 * px + py
    dev = 4 * px + 2 * py + pc
    cidx = jnp.reshape(pc, (1,)).astype(jnp.int32)
    x0 = x.reshape(T, D)
    tgt = loss_target.reshape(T, D)

    c_all = _all_gather8(c.reshape(-1, 128), name="ag_c").reshape(NB, D)
    sc_all = (c_all * _sigmoid(c_all)).astype(BF16)
    wa = ada_w.shape[2]
    wk = kv_ada_w.shape[1]
    m_sh = jnp.concatenate([_mm(sc_all, ada_w[0], "nn", name="ada0", tn=256),
                            _mm(sc_all, ada_w[1], "nn", name="ada1", tn=256),
                            _mm(sc_all, kv_ada_w, "nn", name="ada_kv", tn=256)], axis=1)
    m_all = _all_gather8(m_sh, name="ag_m").reshape(4, 2, NB, 2 * wa + wk)[:, 0]
    mods = []
    for l in range(2):
        full = jnp.transpose(m_all[:, :, l * wa:(l + 1) * wa], (1, 0, 2)).reshape(NB, 6 * D) + ada_b[l]
        mine = lax.dynamic_slice_in_dim(full, E * dev, E, axis=0)
        mods.append([mine[:, i * D:(i + 1) * D].reshape(E, 1, D) for i in range(6)])
    full = jnp.transpose(m_all[:, :, 2 * wa:], (1, 0, 2)).reshape(NB, 2 * D) + kv_ada_b
    mine = lax.dynamic_slice_in_dim(full, E * dev, E, axis=0)
    kv_sh, kv_sc = [mine[:, i * D:(i + 1) * D].reshape(E, 1, D) for i in range(2)]

    wrows = [mlp_w1[0], mlp_w1[1], mlp_w2[0], mlp_w2[1], jnp.concatenate([s5_w_glu[0], w_kv], axis=1), sb_w_q[0], sb_w_o[0]]
    wpack = jnp.concatenate(wrows, axis=0).astype(BF16)
    RW = wpack.shape[0]
    wfull = _chip_exchange(wpack.reshape(2, RW // 2, D), scatter=False, name="wgather_ici")
    wfull = _sibling_fill(wfull, axis=1, name="wgather_d2d").reshape(4, RW, D)

    def cols(r0, nr, c0, nc):
        return jnp.transpose(wfull[:, r0:r0 + nr, c0:c0 + nc], (1, 0, 2)).reshape(nr, 4 * nc)

    def rws(r0, nr):
        return wfull[:, r0:r0 + nr, :].reshape(4 * nr, D)

    W1 = [cols(0, D, 0, D), cols(D, D, 0, D)]
    W2 = [rws(2 * D, D), rws(3 * D, D)]
    Wglu = cols(4 * D, D, 0, D // 2)
    Wkv = cols(4 * D, D, D // 2, D // 2)
    Wq = rws(5 * D, D // 4)
    Wo = rws(5 * D + D // 4, D // 4)

    tm = min(512, S)

    def mlp_fwd(xa, l, mod):
        sh_m, sc_m, g_m = mod[3], mod[4], mod[5]
        h = _norm_mod_fwd(xa, mlp_norm_g[l:l + 1], sh_m, sc_m, n_ex=E, out_dtype=BF16, name=f"mlp_norm{l}")
        a, r = _mm(h, W1[l], "nn", name=f"mlp_up{l}", out_dtypes=(F32, BF16), tm=tm,
                   epilogue=lambda acc: (acc, jnp.square(jnp.maximum(acc, 0.0))))
        xb, ff = _mm(r, W2[l], "nn", name=f"mlp_down{l}", out_dtypes=(F32, F32), tm=tm,
                     extras=[_mn_extra(xa), _vec_extra(g_m, S)],
                     epilogue=lambda acc, xat, gt: (xat + gt * acc, acc))
        return xb, (h, a, r, ff)

    def mlp_bwd(dxb, xa, l, mod, saved):
        sc_m, g_m = mod[4], mod[5]
        h, a, r, ff = saved
        (dff,), (dgm,) = _rowwise(lambda d, f, g: ([g * d], [_csum(d * f)]), [(dxb, D, 0), (ff, D, 0)], [g_m], [],
                                  [(D, BF16)], [D], n_ex=E, name=f"mlp_gate_bwd{l}")
        da = _mm(dff, W2[l], "nt", name=f"mlp_down_dx{l}", out_dtypes=(BF16,), tm=tm, extras=[_mn_extra(a)],
                 epilogue=lambda acc, at: (acc * (2.0 * jnp.maximum(at, 0.0)),))
        dW2 = _mm(r, dff, "tn", name=f"mlp_down_dw{l}", tk=512)
        dh = _mm(da, W1[l], "nt", name=f"mlp_up_dx{l}", tm=tm)
        dW1 = _mm(h, da, "tn", name=f"mlp_up_dw{l}", tk=512)
        (dxa,), (dsh, dsc, dg) = _norm_mod_bwd(xa, dh, dxb, mlp_norm_g[l:l + 1], sc_m, n_ex=E, name=f"mlp_norm_bwd{l}")
        return dxa, dW1, dW2, (dsh, dsc, dgm), dg

    ab_re, ab_im, bb_re, bb_im = _s5_disc(s5_a_re[0], s5_a_im[0], s5_log_dt[0], s5_b_re[0], s5_b_im[0])
    cf, cr = _s5_consts(ab_re, ab_im)
    Wb, Wc = _s5_blockdiag(bb_re, bb_im, s5_c_re[0], s5_c_im[0])
    ng = D // U_LANES
    nd = s5_d.size // 128
    d_full = _all_gather8(jnp.pad(s5_d.reshape(nd, 128), ((0, 8 - nd), (0, 0))), name="ag_d")
    d_full = d_full.reshape(4, 2, 8, 128)[:, 0, :nd].reshape(1, D)

    mod0, mod1 = mods
    h0 = _norm_mod_fwd(x0, mix_norm_g[0:1], mod0[0], mod0[1], n_ex=E, out_dtype=F32, name="mix_norm0")
    y, gy = _s5_fwd(h0, Wb, Wc, cf, d_full, n_ex=E, name="s5_fwd")
    vg = _mm(gy, Wglu, "nn", name="glu_up", tm=tm)
    (x1,), _ = _rowwise(lambda v, g, xt, ga: ([xt + ga * (v * _sigmoid(g))], []),
                        [(vg, D, 0), (vg, D, 1), (x0, D, 0)], [mod0[2]], [], [(D, F32)], [], n_ex=E, name="glu_gate")
    x2, saved_mlp0 = mlp_fwd(x1, 0, mod0)

    hkv = _norm_mod_fwd(x2, kv_norm_g.reshape(1, D), kv_sh, kv_sc, n_ex=E, out_dtype=BF16, name="kv_norm")
    kvf = _mm(hkv, Wkv, "nn", name="kv_proj", tm=tm)
    h1 = _norm_mod_fwd(x2, mix_norm_g[1:2], mod1[0], mod1[1], n_ex=E, out_dtype=BF16, name="mix_norm1")
    qf = _mm(h1, Wq, "nn", name="q_proj", tm=tm)
    qg2 = jnp.tile(q_norm_g.reshape(1, HEAD_DIM), (1, 2))
    kg2 = jnp.tile(k_norm_g.reshape(1, HEAD_DIM), (1, 2))
    o, lf_tot = _attn_fwd(qf, kvf, qg2, kg2, n_ex=E, name="attn_fwd")
    x3, mix1 = _mm(o, Wo, "nn", name="o_proj", out_dtypes=(F32, F32), tm=tm,
                   extras=[_mn_extra(x2), _vec_extra(mod1[2], S)],
                   epilogue=lambda acc, xat, gt: (xat + gt * acc, acc))
    x4, saved_mlp1 = mlp_fwd(x3, 1, mod1)

    (dx4,), (lsum,) = _rowwise(lambda xt, tt: ([(xt - tt) * (1.0 / D)], [_csum(jnp.square(xt - tt)) * (0.5 / D)]),
                               [(x4, D, 0), (tgt, D, 0)], [], [], [(D, F32)], [D], n_ex=E, name="loss")
    loss = lax.psum(jnp.sum(lsum), ("x", "y", "c"))

    dx3, dW1_1, dW2_1, (dsh_m1, dsc_m1, dgm1), dg_mlp1 = mlp_bwd(dx4, x3, 1, mod1, saved_mlp1)
    (dmix1,), (dga1,) = _rowwise(lambda d, f, g: ([g * d], [_csum(d * f)]), [(dx3, D, 0), (mix1, D, 0)], [mod1[2]], [],
                                 [(D, BF16)], [D], n_ex=E, name="attn_gate_bwd")
    do = _mm(dmix1, Wo, "nt", name="o_proj_dx", tm=tm)
    dWo = _mm(o, dmix1, "tn", name="o_proj_dw", tk=512)
    dq, dk, dv, dqg, dkg = _attn_bwd(qf, kvf, lf_tot, do, qg2, kg2, n_ex=E, name="attn_bwd")
    dh1 = _mm(dq, Wq, "nt", name="q_proj_dx", tm=tm)
    dWq = _mm(h1, dq, "tn", name="q_proj_dw", tk=512)
    (dx2,), (dsh_a1, dsc_a1, dg_mix1) = _norm_mod_bwd(x2, dh1, dx3, mix_norm_g[1:2], mod1[1], n_ex=E, name="mix_norm_bwd1")
    dkv = jnp.concatenate([dk, dv], axis=1)
    dhkv = _mm(dkv, Wkv, "nt", name="kv_proj_dx", tm=tm)
    dWkv = _mm(hkv, dkv, "tn", name="kv_proj_dw", tk=512)
    (dx2,), (dkv_sh, dkv_sc, dg_kv) = _norm_mod_bwd(x2, dhkv, dx2, kv_norm_g.reshape(1, D), kv_sc, n_ex=E, name="kv_norm_bwd")

    dx1, dW1_0, dW2_0, (dsh_m0, dsc_m0, dgm0), dg_mlp0 = mlp_bwd(dx2, x1, 0, mod0, saved_mlp0)

    def glu_bwd(v, g, d, ga):
        sg = _sigmoid(g)
        dm = ga * d
        return [jnp.concatenate([dm * sg, dm * v * sg * (1.0 - sg)], axis=1)], [_csum(d * (v * sg))]
    (dvg,), (dga0,) = _rowwise(glu_bwd, [(vg, D, 0), (vg, D, 1), (dx1, D, 0)], [mod0[2]], [], [(2 * D, BF16)], [D],
                               n_ex=E, name="glu_gate_bwd")
    dgy = _mm(dvg, Wglu, "nt", name="glu_up_dx", tm=tm)
    dWglu = _mm(gy, dvg, "tn", name="glu_up_dw", tk=512)
    dh0, dWb, dWc, dab, dd = _s5_bwd(h0, y, dgy, Wb, Wc, cf, cr, d_full, n_ex=E, name="s5_bwd")
    (gx,), (dsh_a0, dsc_a0, dg_mix0) = _norm_mod_bwd(x0, dh0, dx1, mix_norm_g[0:1], mod0[1], n_ex=E, name="mix_norm_bwd0")
    grad_x = gx.reshape(E, S, D)

    dm_mine = jnp.concatenate([t.reshape(E, D) for t in
                               (dsh_a0, dsc_a0, dga0, dsh_m0, dsc_m0, dgm0, dsh_a1, dsc_a1, dga1, dsh_m1, dsc_m1, dgm1, dkv_sh, dkv_sc)], axis=1)
    dm_all = _all_gather8(dm_mine.reshape(8, -1), name="ag_dm").reshape(NB, 14 * D)
    sc_f32 = c_all * _sigmoid(c_all)
    g_ada_w = jnp.stack([_mm(sc_f32, lax.dynamic_slice_in_dim(dm_all, l * 6 * D + chip * wa, wa, axis=1), "tn", name=f"ada_dw{l}", tn=256)
                         for l in range(2)])
    g_kv_ada_w = _mm(sc_f32, lax.dynamic_slice_in_dim(dm_all, 12 * D + chip * wk, wk, axis=1), "tn", name="ada_kv_dw", tn=256)
    db_all = _colsum(dm_all, name="ada_db")
    g_ada_b = db_all[0, :12 * D].reshape(2, 6 * D)
    g_kv_ada_b = db_all[0, 12 * D:]

    dWb_re, dWb_im, dC_re, dC_im = _s5_unblock(dWb, dWc)
    small_parts = [dg_mix0.sum(0), dg_mix1.sum(0), dg_mlp0.sum(0), dg_mlp1.sum(0), dg_kv.sum(0),
                   dqg.sum((0, 1, 2)).reshape(2, HEAD_DIM).sum(0), dkg.sum((0, 1, 2)).reshape(2, HEAD_DIM).sum(0),
                   dd[:, 0, :], dab[:, 0, :], dab[:, 1, :], dWb_re, dWb_im, dC_re, dC_im]
    spack, spans = _pack_rows(small_parts)
    ssum = _sum_blocks(_all_gather8(spack, name="ag_small"), 8, name="sum_small")
    (g_mix0, g_mix1, g_mlp0, g_mlp1, g_kvn, g_qn, g_kn, g_d, g_abr, g_abi, g_bbr, g_bbi, g_cre, g_cim) = _unpack_rows(ssum, spans)
    _, disc_vjp = jax.vjp(_s5_disc, s5_a_re[0], s5_a_im[0], s5_log_dt[0], s5_b_re[0], s5_b_im[0])
    g_are, g_aim, g_ldt, g_bre, g_bim = disc_vjp((g_abr.reshape(ab_re.shape), g_abi.reshape(ab_im.shape), g_bbr, g_bbi))
    g_s5d = lax.dynamic_slice_in_dim(g_d.reshape(1, D), chip * s5_d.shape[1], s5_d.shape[1], axis=1)

    def csh(g, nc):
        return jnp.transpose(g.reshape(g.shape[0], 4, nc), (1, 0, 2))

    gparts = [csh(dW1_0, D), csh(dW1_1, D), dW2_0.reshape(4, D, D), dW2_1.reshape(4, D, D),
              jnp.concatenate([csh(dWglu, D // 2), csh(dWkv, D // 2)], axis=2), dWq.reshape(4, D // 4, D), dWo.reshape(4, D // 4, D)]
    gpack = jnp.concatenate(gparts, axis=1).reshape(4, 2, RW // 2, D)
    theirs = _sibling_swap_half(gpack, name="gscatter_d2d")
    chip_sum = _add_my_half(gpack, theirs, cidx, name="gscatter_add")
    from_chips = _chip_exchange(chip_sum, scatter=True, name="gscatter_ici")
    ghalf = _sum4_into_half(from_chips, cidx, name="gscatter_sum")
    gsh = _sibling_fill(ghalf, axis=0, name="gscatter_fill").reshape(RW, D)

    def upd_big(w, m, v, roff, cb, name):
        shape = w.shape
        W = shape[-1]
        d_, m_, v_, g_ = _adamw2d(w.reshape(-1, W), gsh, m.reshape(-1, W), v.reshape(-1, W), name=name, g_roff=roff, g_cb=cb)
        return [t.reshape(shape) for t in (g_, d_, m_, v_)]

    def upd_own(w, g, m, v, name):
        shape = w.shape
        W = shape[-1]
        d_, m_, v_, g_ = _adamw2d(w.reshape(-1, W), g.reshape(-1, W), m.reshape(-1, W), v.reshape(-1, W), name=name)
        return [t.reshape(shape) for t in (g_, d_, m_, v_)]

    res = {}
    res["ada_w"] = upd_own(ada_w, g_ada_w, m_ada_w, v_ada_w, "adam_ada_w")
    res["kv_ada_w"] = upd_own(kv_ada_w, g_kv_ada_w, m_kv_ada_w, v_kv_ada_w, "adam_kv_ada_w")
    res["mlp_w1"] = upd_big(mlp_w1, m_mlp_w1, v_mlp_w1, 0, 0, "adam_w1")
    res["mlp_w2"] = upd_big(mlp_w2, m_mlp_w2, v_mlp_w2, 2 * D, 0, "adam_w2")
    res["s5_w_glu"] = upd_big(s5_w_glu, m_s5_w_glu, v_s5_w_glu, 4 * D, 0, "adam_glu")
    res["w_kv"] = upd_big(w_kv, m_w_kv, v_w_kv, 4 * D, 1, "adam_wkv")
    res["sb_w_q"] = upd_big(sb_w_q, m_sb_w_q, v_sb_w_q, 5 * D, 0, "adam_wq")
    res["sb_w_o"] = upd_big(sb_w_o, m_sb_w_o, v_sb_w_o, 5 * D + D // 4, 0, "adam_wo")

    small = {
        "ada_b": (ada_b, g_ada_b, m_ada_b, v_ada_b),
        "mix_norm_g": (mix_norm_g, jnp.stack([g_mix0, g_mix1]), m_mix_norm_g, v_mix_norm_g),
        "mlp_norm_g": (mlp_norm_g, jnp.stack([g_mlp0, g_mlp1]), m_mlp_norm_g, v_mlp_norm_g),
        "s5_a_re": (s5_a_re, g_are[None], m_s5_a_re, v_s5_a_re),
        "s5_a_im": (s5_a_im, g_aim[None], m_s5_a_im, v_s5_a_im),
        "s5_log_dt": (s5_log_dt, g_ldt[None], m_s5_log_dt, v_s5_log_dt),
        "s5_b_re": (s5_b_re, g_bre[None], m_s5_b_re, v_s5_b_re),
        "s5_b_im": (s5_b_im, g_bim[None], m_s5_b_im, v_s5_b_im),
        "s5_c_re": (s5_c_re, g_cre[None], m_s5_c_re, v_s5_c_re),
        "s5_c_im": (s5_c_im, g_cim[None], m_s5_c_im, v_s5_c_im),
        "s5_d": (s5_d, g_s5d, m_s5_d, v_s5_d),
        "kv_ada_b": (kv_ada_b, g_kv_ada_b, m_kv_ada_b, v_kv_ada_b),
        "kv_norm_g": (kv_norm_g, g_kvn, m_kv_norm_g, v_kv_norm_g),
        "k_norm_g": (k_norm_g, g_kn, m_k_norm_g, v_k_norm_g),
        "q_norm_g": (q_norm_g, g_qn.reshape(q_norm_g.shape), m_q_norm_g, v_q_norm_g),
    }
    names = list(small)
    packs = [_pack_rows([small[n][i].reshape(small[n][0].shape) for n in names]) for i in range(4)]
    sp = packs[0][1]
    d_, m_, v_, g_ = _adamw2d(packs[0][0], packs[1][0], packs[2][0], packs[3][0], name="adam_small")
    for n, gg, dd_, mm_, vv_ in zip(names, _unpack_rows(g_, sp), _unpack_rows(d_, sp), _unpack_rows(m_, sp), _unpack_rows(v_, sp)):
        res[n] = [gg, dd_, mm_, vv_]

    order = ["ada_w", "ada_b", "mix_norm_g", "mlp_norm_g", "mlp_w1", "mlp_w2", "s5_a_re", "s5_a_im", "s5_log_dt", "s5_b_re", "s5_b_im",
             "s5_c_re", "s5_c_im", "s5_d", "s5_w_glu", "kv_ada_w", "kv_ada_b", "kv_norm_g", "w_kv", "k_norm_g", "sb_w_q", "q_norm_g", "sb_w_o"]
    return (loss, grad_x, *[res[n][0] for n in order], *[res[n][1] for n in order], *[res[n][2] for n in order], *[res[n][3] for n in order])
```

```python
import functools
import math

import jax
import jax.numpy as jnp
from jax import lax
from jax.experimental import pallas as pl
from jax.experimental.pallas import tpu as pltpu

F32 = jnp.float32
BF16 = jnp.bfloat16
EPS = 1e-6
HEAD_DIM = 64
S5_GROUP = 16
S5_STATE = 64
GROUPS_PER_STEP = 8
U_LANES = GROUPS_PER_STEP * S5_GROUP
ST_LANES = GROUPS_PER_STEP * S5_STATE
SCAN_LANES = 256
SCAN_UNROLL = 4
VMEM_LIMIT = 56 * 1024 * 1024
ADAM_LR, ADAM_B1, ADAM_B2, ADAM_EPS, ADAM_WD, ADAM_STEP = 0.001, 0.9, 0.999, 1e-08, 0.01, 10
MESH = pl.DeviceIdType.MESH


def _cp(sem):
    return pltpu.CompilerParams(dimension_semantics=sem, vmem_limit_bytes=VMEM_LIMIT)


def _mm(a, b, dims, *, name, out_dtypes=(F32,), epilogue=None, extras=(), tm=512, tn=1024, tk=1024):
    if dims == "nn":
        (M, K), (_, N) = a.shape, b.shape
    elif dims == "nt":
        (M, K), (N, _) = a.shape, b.shape
    else:
        (K, M), (_, N) = a.shape, b.shape
    tm, tn, tk = min(tm, M), min(tn, N), min(tk, K)
    assert M % tm == 0 and N % tn == 0 and K % tk == 0, (M, N, K, tm, tn, tk)
    nk = K // tk
    extras = [e(tm, tn) for e in extras]
    a_spec = pl.BlockSpec((tk, tm), lambda i, j, k: (k, i)) if dims == "tn" else pl.BlockSpec((tm, tk), lambda i, j, k: (i, k))
    b_spec = pl.BlockSpec((tn, tk), lambda i, j, k: (j, k)) if dims == "nt" else pl.BlockSpec((tk, tn), lambda i, j, k: (k, j))
    contract = {"nn": ((1,), (0,)), "nt": ((1,), (1,)), "tn": ((0,), (0,))}[dims]
    n_ex, n_out = len(extras), len(out_dtypes)

    def finish(r, ex, outs):
        res = epilogue(r, *[e[...] for e in ex]) if epilogue is not None else (r,)
        for o, v in zip(outs, res):
            o[...] = v.astype(o.dtype)

    def product(a_ref, b_ref):
        return lax.dot_general(a_ref[...].astype(BF16), b_ref[...].astype(BF16), (contract, ((), ())), preferred_element_type=F32)

    def body_one(a_ref, b_ref, *rest):
        finish(product(a_ref, b_ref), rest[:n_ex], rest[n_ex:])

    def body_acc(a_ref, b_ref, *rest):
        ex, outs, acc = rest[:n_ex], rest[n_ex:n_ex + n_out], rest[-1]
        k = pl.program_id(2)

        @pl.when(k == 0)
        def _():
            acc[...] = product(a_ref, b_ref)

        @pl.when(k > 0)
        def _():
            acc[...] += product(a_ref, b_ref)

        @pl.when(k == nk - 1)
        def _():
            finish(acc[...], ex, outs)

    out = pl.pallas_call(
        body_one if nk == 1 else body_acc, name=name, grid=(M // tm, N // tn, nk),
        in_specs=[a_spec, b_spec] + [pl.BlockSpec(blk, im) for (_, blk, im) in extras],
        out_specs=[pl.BlockSpec((tm, tn), lambda i, j, k: (i, j)) for _ in out_dtypes],
        out_shape=[jax.ShapeDtypeStruct((M, N), d) for d in out_dtypes],
        scratch_shapes=[] if nk == 1 else [pltpu.VMEM((tm, tn), F32)],
        compiler_params=_cp(("parallel", "parallel", "arbitrary")),
    )(a, b, *[e[0] for e in extras])
    return out if n_out > 1 else out[0]


def _mn_extra(arr):
    return lambda tm, tn: (arr, (tm, tn), lambda i, j, k: (i, j))


def _vec_extra(vec, S):
    return lambda tm, tn: (vec, (None, 1, tn), lambda i, j, k: ((i * tm) // S, 0, j))


def _rowwise(fn, rows, vecs=(), consts=(), out_rows=(), out_sums=(), *, n_ex, name, tr=256):
    rows = [r if len(r) == 4 else (*r, 0) for r in rows]
    S = min(r[0].shape[0] for r in rows if r[3] == 0) // n_ex
    tr = math.gcd(tr, S)
    assert S % tr == 0
    nb = S // tr
    in_specs = []
    for (arr, w, cb, roff) in rows:
        assert roff % tr == 0
        in_specs.append(pl.BlockSpec((tr, w), functools.partial(lambda e, i, cb, ro: (e * nb + i + ro, cb), cb=cb, ro=roff // tr)))
    for v in vecs:
        in_specs.append(pl.BlockSpec((None, 1, v.shape[-1]), lambda e, i: (e, 0, 0)))
    for c in consts:
        in_specs.append(pl.BlockSpec((1, c.shape[-1]), lambda e, i: (0, 0)))
    n_in, n_or, n_os = len(in_specs), len(out_rows), len(out_sums)
    out_specs = [pl.BlockSpec((tr, w), lambda e, i: (e * nb + i, 0)) for (w, _) in out_rows]
    out_specs += [pl.BlockSpec((None, 1, w), lambda e, i: (e, 0, 0)) for w in out_sums]
    out_shape = [jax.ShapeDtypeStruct((n_ex * S, w), d) for (w, d) in out_rows]
    out_shape += [jax.ShapeDtypeStruct((n_ex, 1, w), F32) for w in out_sums]

    def body(*refs):
        ins, o_r, o_s = refs[:n_in], refs[n_in:n_in + n_or], refs[n_in + n_or:]
        ro, so = fn(*[r[...] for r in ins])
        for o, v in zip(o_r, ro):
            o[...] = v.astype(o.dtype)
        i = pl.program_id(1)
        for o, v in zip(o_s, so):
            @pl.when(i == 0)
            def _(o=o, v=v):
                o[...] = v

            @pl.when(i > 0)
            def _(o=o, v=v):
                o[...] += v

    outs = pl.pallas_call(
        body, name=name, grid=(n_ex, nb), in_specs=in_specs, out_specs=out_specs, out_shape=out_shape,
        compiler_params=_cp(("parallel", "arbitrary")),
    )(*[r[0] for r in rows], *vecs, *consts)
    return outs[:n_or], outs[n_or:]


def _csum(x):
    return jnp.sum(x, axis=0, keepdims=True)


def _norm_mod_fwd(x, g, sh, sc, *, n_ex, out_dtype, name):
    def fn(xt, sht, sct, gt):
        r = lax.rsqrt(jnp.mean(xt * xt, axis=-1, keepdims=True) + EPS)
        return [(xt * r * gt) * (1.0 + sct) + sht], []
    D = x.shape[1]
    return _rowwise(fn, [(x, D, 0)], [sh, sc], [g], [(D, out_dtype)], [], n_ex=n_ex, name=name)[0][0]


def _norm_mod_bwd(x, dh, dres, g, sc, *, n_ex, name):
    def fn(xt, dht, drt, sct, gt):
        dht = dht.astype(F32)
        r = lax.rsqrt(jnp.mean(xt * xt, axis=-1, keepdims=True) + EPS)
        n = xt * r
        y = n * gt
        dy = dht * (1.0 + sct)
        dn = dy * gt
        dx = r * (dn - n * jnp.mean(dn * n, axis=-1, keepdims=True))
        return [drt + dx], [_csum(dht), _csum(dht * y), _csum(dy * n)]
    D = x.shape[1]
    return _rowwise(fn, [(x, D, 0), (dh, D, 0), (dres, D, 0)], [sc], [g], [(D, F32)], [D, D, D], n_ex=n_ex, name=name)


def _sigmoid(x):
    return 1.0 / (1.0 + jnp.exp(-x))


def _gelu(y):
    return 0.5 * y * (1.0 + jnp.tanh(0.7978845608028654 * (y + 0.044715 * y * y * y)))


def _gelu_grad(y):
    t = jnp.tanh(0.7978845608028654 * (y + 0.044715 * y * y * y))
    return 0.5 * (1.0 + t) + 0.5 * y * (1.0 - t * t) * 0.7978845608028654 * (1.0 + 3 * 0.044715 * y * y)


def _adamw_fn(w, g, m, v):
    m2 = ADAM_B1 * m + (1.0 - ADAM_B1) * g
    v2 = ADAM_B2 * v + (1.0 - ADAM_B2) * (g * g)
    m_hat = m2 / (1.0 - ADAM_B1 ** ADAM_STEP)
    v_hat = v2 / (1.0 - ADAM_B2 ** ADAM_STEP)
    delta = -ADAM_LR * (m_hat / (jnp.sqrt(v_hat) + ADAM_EPS) + ADAM_WD * w)
    return delta, m2, v2


def _adamw2d(w, g, m, v, *, name, g_roff=0, g_cb=0):
    R, W = w.shape

    def fn(wt, gt, mt, vt):
        d, m2, v2 = _adamw_fn(wt, gt, mt, vt)
        return [d, m2, v2, gt], []
    return _rowwise(fn, [(w, W, 0), (g, W, g_cb, g_roff), (m, W, 0), (v, W, 0)], [], [],
                    [(W, F32)] * 4, [], n_ex=1, name=name, tr=256)[0]


def _scan_tiles(re_ref, im_ref, cf, lane0, n_chunks, reverse, extra=None):
    L = SCAN_LANES
    lanes = pl.ds(lane0, L)
    A = [cf[i, :, lanes] for i in range(8)]
    shifts = (7, 6, 4) if reverse else (1, 2, 4)
    edge = 0 if reverse else 7

    def one_tile(cc, carry):
        cr, ci = carry[0], carry[1]
        rows = pl.ds(pl.multiple_of(cc * 8, 8), 8)
        xr, xi = re_ref[rows, lanes], im_ref[rows, lanes]
        for idx, sft in enumerate(shifts):
            ar, ai = A[2 * idx], A[2 * idx + 1]
            rr, ri = pltpu.roll(xr, sft, 0), pltpu.roll(xi, sft, 0)
            xr, xi = xr + ar * rr - ai * ri, xi + ar * ri + ai * rr
        pr, pi = A[6], A[7]
        xr, xi = xr + pr * cr - pi * ci, xi + pr * ci + pi * cr
        re_ref[rows, lanes] = xr
        im_ref[rows, lanes] = xi
        out = (jnp.broadcast_to(xr[edge:edge + 1, :], (8, L)), jnp.broadcast_to(xi[edge:edge + 1, :], (8, L)))
        if extra is not None:
            out = out + extra(cc, xr, xi, carry[2:])
        return out

    def body(c, carry):
        for u in range(SCAN_UNROLL):
            t = c * SCAN_UNROLL + u
            carry = one_tile((n_chunks - 1 - t) if reverse else t, carry)
        return carry

    assert n_chunks % SCAN_UNROLL == 0
    z = jnp.zeros((8, L), F32)
    init = (z, z) if extra is None else (z, z, z, z)
    return lax.fori_loop(0, n_chunks // SCAN_UNROLL, body, init)


def _s5_consts(ab_re, ab_im):
    ng = ab_re.shape[0] // GROUPS_PER_STEP
    ar, ai = ab_re.reshape(ng, 1, ST_LANES), ab_im.reshape(ng, 1, ST_LANES)

    def cmul(xr, xi, yr, yi):
        return xr * yr - xi * yi, xr * yi + xi * yr

    def build(ar, ai, reverse):
        pw = [(ar, ai)]
        for _ in range(7):
            pw.append(cmul(*pw[-1], ar, ai))
        row = jnp.arange(8).reshape(1, 8, 1)
        tiles = []
        for k in (1, 2, 4):
            keep = (row <= 7 - k) if reverse else (row >= k)
            tiles += [jnp.where(keep, pw[k - 1][0], 0.0), jnp.where(keep, pw[k - 1][1], 0.0)]
        order = [7 - r for r in range(8)] if reverse else list(range(8))
        tiles += [jnp.concatenate([pw[o][0] for o in order], axis=1), jnp.concatenate([pw[o][1] for o in order], axis=1)]
        return jnp.stack([jnp.broadcast_to(t, (ng, 8, ST_LANES)) for t in tiles], axis=1)

    return build(ar, ai, False), build(ar, -ai, True)


def _s5_blockdiag(bb_re, bb_im, c_re, c_im):
    G = bb_re.shape[0]
    ng = G // GROUPS_PER_STEP
    eye = jnp.eye(GROUPS_PER_STEP, dtype=F32)

    def wb(bb):
        return jnp.einsum("bgph,gk->bghkp", bb.reshape(ng, GROUPS_PER_STEP, S5_STATE, S5_GROUP), eye).reshape(ng, U_LANES, ST_LANES)

    def wc(cc):
        return jnp.einsum("bghp,gk->bkpgh", cc.reshape(ng, GROUPS_PER_STEP, S5_GROUP, S5_STATE), eye).reshape(ng, ST_LANES, U_LANES)

    Wb = jnp.concatenate([wb(bb_re), wb(bb_im)], axis=2).astype(BF16)
    Wc = jnp.concatenate([wc(c_re), -wc(c_im)], axis=1).astype(BF16)
    return Wb, Wc


def _s5_unblock(dWb, dWc):
    ng = dWb.shape[0]
    eye = jnp.eye(GROUPS_PER_STEP, dtype=F32)

    def ub(w):
        return jnp.einsum("bghkp,gk->bgph", w.reshape(ng, GROUPS_PER_STEP, S5_GROUP, GROUPS_PER_STEP, S5_STATE), eye).reshape(-1, S5_STATE, S5_GROUP)

    def uc(w):
        return jnp.einsum("bkpgh,gk->bghp", w.reshape(ng, GROUPS_PER_STEP, S5_STATE, GROUPS_PER_STEP, S5_GROUP), eye).reshape(-1, S5_GROUP, S5_STATE)

    return ub(dWb[:, :, :ST_LANES]), ub(dWb[:, :, ST_LANES:]), uc(dWc[:, :ST_LANES, :]), -uc(dWc[:, ST_LANES:, :])


def _s5_disc(a_re, a_im, log_dt, b_re, b_im):
    dt = jnp.exp(log_dt)[:, None]
    mag = jnp.exp(a_re * dt)
    ab_re = mag * jnp.cos(a_im * dt)
    ab_im = mag * jnp.sin(a_im * dt)
    den = a_re * a_re + a_im * a_im
    nr, ni = ab_re - 1, ab_im
    f_re = (nr * a_re + ni * a_im) / den
    f_im = (ni * a_re - nr * a_im) / den
    bb_re = f_re[..., None] * b_re - f_im[..., None] * b_im
    bb_im = f_re[..., None] * b_im + f_im[..., None] * b_re
    return ab_re, ab_im, bb_re, bb_im


ROW_CHUNK = 512


def _s5_fwd(u, Wb, Wc, cf, d, xsrc, *, n_ex, name):
    T, D = u.shape
    S = T // n_ex
    ng = D // U_LANES
    rc = min(ROW_CHUNK, S)

    def body(u_ref, wb_ref, wc_ref, cf_ref, d_ref, xsrc_ref, y_ref, gy_ref, xout_ref, re_s, im_s, *sems):
        step = pl.program_id(0) * ng + pl.program_id(1)
        exch = _ChipExchange(xsrc_ref, xout_ref, *sems, scatter=False)

        @pl.when(step == 0)
        def _():
            exch.start()

        for r in range(S // rc):
            rows = pl.ds(r * rc, rc)
            bu = jnp.dot(u_ref[rows, :].astype(BF16), wb_ref[...], preferred_element_type=F32)
            re_s[rows, :] = bu[:, :ST_LANES]
            im_s[rows, :] = bu[:, ST_LANES:]
        for l0 in range(0, ST_LANES, SCAN_LANES):
            _scan_tiles(re_s, im_s, cf_ref, l0, S // 8, False)
        for r in range(S // rc):
            rows = pl.ds(r * rc, rc)
            st = jnp.concatenate([re_s[rows, :], im_s[rows, :]], axis=1).astype(BF16)
            y = jnp.dot(st, wc_ref[...], preferred_element_type=F32) + d_ref[...] * u_ref[rows, :]
            y_ref[rows, :] = y
            gy_ref[rows, :] = _gelu(y).astype(BF16)

        @pl.when(step == n_ex * ng - 1)
        def _():
            exch.wait()

    return pl.pallas_call(
        body, name=name, grid=(n_ex, ng),
        in_specs=[pl.BlockSpec((S, U_LANES), lambda e, g: (e, g)),
                  pl.BlockSpec((None, U_LANES, 2 * ST_LANES), lambda e, g: (g, 0, 0)),
                  pl.BlockSpec((None, 2 * ST_LANES, U_LANES), lambda e, g: (g, 0, 0)),
                  pl.BlockSpec((None, 8, 8, ST_LANES), lambda e, g: (g, 0, 0, 0)),
                  pl.BlockSpec((1, U_LANES), lambda e, g: (0, g)), ANY],
        out_specs=[pl.BlockSpec((S, U_LANES), lambda e, g: (e, g))] * 2 + [ANY],
        out_shape=[jax.ShapeDtypeStruct((T, D), F32), jax.ShapeDtypeStruct((T, D), BF16), _ChipExchange.out_shape(xsrc, False)],
        scratch_shapes=[pltpu.VMEM((S, ST_LANES), F32)] * 2 + _ChipExchange.SCRATCH,
        compiler_params=_cp(("arbitrary", "arbitrary")),
    )(u, Wb, Wc, cf, d, xsrc)


def _s5_bwd(u, y, dgy, Wb, Wc, cf, cr, d, xsrc, *, n_ex, name):
    T, D = u.shape
    S = T // n_ex
    ng = D // U_LANES
    rc = min(ROW_CHUNK, S)
    nch = S // 8

    def body(u_ref, y_ref, dgy_ref, wb_ref, wc_ref, cf_ref, cr_ref, d_ref, xsrc_ref,
             du_ref, dwb_ref, dwc_ref, dab_ref, dd_ref, xout_ref, re_s, im_s, gr_s, gi_s, dy_s, *sems):
        e = pl.program_id(1)
        step = pl.program_id(0) * n_ex + e
        exch = _ChipExchange(xsrc_ref, xout_ref, *sems, scatter=True)

        @pl.when(step == 0)
        def _():
            exch.start()

        @pl.when(e == 0)
        def _():
            dwb_ref[...] = jnp.zeros_like(dwb_ref)
            dwc_ref[...] = jnp.zeros_like(dwc_ref)
            dab_ref[...] = jnp.zeros_like(dab_ref)
            dd_ref[...] = jnp.zeros_like(dd_ref)

        dd = jnp.zeros((1, U_LANES), F32)
        for r in range(S // rc):
            rows = pl.ds(r * rc, rc)
            ut = u_ref[rows, :]
            bu = jnp.dot(ut.astype(BF16), wb_ref[...], preferred_element_type=F32)
            re_s[rows, :] = bu[:, :ST_LANES]
            im_s[rows, :] = bu[:, ST_LANES:]
            dy = dgy_ref[rows, :].astype(F32) * _gelu_grad(y_ref[rows, :])
            dy_s[rows, :] = dy
            dd = dd + _csum(dy * ut)
            go = lax.dot_general(dy.astype(BF16), wc_ref[...], (((1,), (1,)), ((), ())), preferred_element_type=F32)
            gr_s[rows, :] = go[:, :ST_LANES]
            gi_s[rows, :] = go[:, ST_LANES:]
        dd_ref[0:1, :] += dd
        row0 = lax.broadcasted_iota(jnp.int32, (8, SCAN_LANES), 0) == 0
        for l0 in range(0, ST_LANES, SCAN_LANES):
            lanes = pl.ds(l0, SCAN_LANES)
            _scan_tiles(re_s, im_s, cf_ref, l0, nch, False)

            def dab_part(cc, gr, gi, acc, lanes=lanes):
                rows = pl.ds(pl.multiple_of(cc * 8, 8), 8)
                prev = pl.ds(pl.multiple_of(jnp.maximum(cc - 1, 0) * 8, 8), 8)
                live = (cc > 0).astype(F32)
                sr = jnp.where(row0, pltpu.roll(re_s[prev, lanes], 1, 0) * live, pltpu.roll(re_s[rows, lanes], 1, 0))
                si = jnp.where(row0, pltpu.roll(im_s[prev, lanes], 1, 0) * live, pltpu.roll(im_s[rows, lanes], 1, 0))
                return (acc[0] + gr * sr + gi * si, acc[1] + gi * sr - gr * si)

            res = _scan_tiles(gr_s, gi_s, cr_ref, l0, nch, True, extra=dab_part)
            dab_ref[0:1, lanes] += _csum(res[2])
            dab_ref[1:2, lanes] += _csum(res[3])
        for r in range(S // rc):
            rows = pl.ds(r * rc, rc)
            st = jnp.concatenate([re_s[rows, :], im_s[rows, :]], axis=1).astype(BF16)
            g = jnp.concatenate([gr_s[rows, :], gi_s[rows, :]], axis=1).astype(BF16)
            dyb = dy_s[rows, :].astype(BF16)
            dwc_ref[...] += lax.dot_general(st, dyb, (((0,), (0,)), ((), ())), preferred_element_type=F32)
            dwb_ref[...] += lax.dot_general(u_ref[rows, :].astype(BF16), g, (((0,), (0,)), ((), ())), preferred_element_type=F32)
            du = lax.dot_general(g, wb_ref[...], (((1,), (1,)), ((), ())), preferred_element_type=F32)
            du_ref[rows, :] = du + d_ref[...] * dy_s[rows, :]

        @pl.when(step == ng * n_ex - 1)
        def _():
            exch.wait()

    return pl.pallas_call(
        body, name=name, grid=(ng, n_ex),
        in_specs=[pl.BlockSpec((S, U_LANES), lambda g, e: (e, g))] * 3 + [
            pl.BlockSpec((None, U_LANES, 2 * ST_LANES), lambda g, e: (g, 0, 0)),
            pl.BlockSpec((None, 2 * ST_LANES, U_LANES), lambda g, e: (g, 0, 0)),
            pl.BlockSpec((None, 8, 8, ST_LANES), lambda g, e: (g, 0, 0, 0)),
            pl.BlockSpec((None, 8, 8, ST_LANES), lambda g, e: (g, 0, 0, 0)),
            pl.BlockSpec((1, U_LANES), lambda g, e: (0, g)), ANY],
        out_specs=[pl.BlockSpec((S, U_LANES), lambda g, e: (e, g)),
                   pl.BlockSpec((None, U_LANES, 2 * ST_LANES), lambda g, e: (g, 0, 0)),
                   pl.BlockSpec((None, 2 * ST_LANES, U_LANES), lambda g, e: (g, 0, 0)),
                   pl.BlockSpec((None, 8, ST_LANES), lambda g, e: (g, 0, 0)),
                   pl.BlockSpec((None, 8, U_LANES), lambda g, e: (g, 0, 0)), ANY],
        out_shape=[jax.ShapeDtypeStruct((T, D), F32),
                   jax.ShapeDtypeStruct((ng, U_LANES, 2 * ST_LANES), F32),
                   jax.ShapeDtypeStruct((ng, 2 * ST_LANES, U_LANES), F32),
                   jax.ShapeDtypeStruct((ng, 8, ST_LANES), F32),
                   jax.ShapeDtypeStruct((ng, 8, U_LANES), F32), _ChipExchange.out_shape(xsrc, True)],
        scratch_shapes=[pltpu.VMEM((S, ST_LANES), F32)] * 4 + [pltpu.VMEM((S, U_LANES), F32)] + _ChipExchange.SCRATCH,
        compiler_params=_cp(("arbitrary", "arbitrary")),
    )(u, y, dgy, Wb, Wc, cf, cr, d, xsrc)


TQ = 256
KW = 512
SUB = 128


def _head_masks():
    lane = lax.broadcasted_iota(jnp.int32, (1, 2 * HEAD_DIM), 1)
    m0 = (lane < HEAD_DIM).astype(F32)
    return m0, 1.0 - m0


def _head_norm(x, g, m0, m1):
    sq = x * x
    r0 = lax.rsqrt(jnp.sum(sq * m0, axis=-1, keepdims=True) / HEAD_DIM + EPS)
    r1 = lax.rsqrt(jnp.sum(sq * m1, axis=-1, keepdims=True) / HEAD_DIM + EPS)
    r = m0 * r0 + m1 * r1
    return x * r, r


def _head_norm_bwd(dy, n, r, g, m0, m1):
    dn = dy * g
    p = dn * n
    mean = (m0 * jnp.sum(p * m0, axis=-1, keepdims=True) + m1 * jnp.sum(p * m1, axis=-1, keepdims=True)) / HEAD_DIM
    return r * (dn - n * mean), _csum(dy * n)


def _pair_matrix(kind):
    r = lax.broadcasted_iota(jnp.int32, (2 * SUB, 2 * SUB), 0)
    c = lax.broadcasted_iota(jnp.int32, (2 * SUB, 2 * SUB), 1)
    same = (r < SUB) == (c < SUB)
    rel = {"after": r > c, "upto": r <= c, "before": r < c}[kind]
    return jnp.logical_and(same, rel).astype(BF16)


def _block_sums(x, mat, carry, reverse):
    hi = x.astype(BF16)
    lo = (x - hi.astype(F32)).astype(BF16)
    npair = KW // (2 * SUB)
    parts = [None] * (2 * npair)
    for p in (range(npair - 1, -1, -1) if reverse else range(npair)):
        sl = slice(2 * SUB * p, 2 * SUB * (p + 1))
        loc = jnp.dot(hi[:, sl], mat, preferred_element_type=F32) + jnp.dot(lo[:, sl], mat, preferred_element_type=F32)
        for b in ((1, 0) if reverse else (0, 1)):
            k = 2 * p + b
            parts[k] = loc[:, SUB * b:SUB * (b + 1)] + carry
            carry = carry + jnp.sum(x[:, SUB * k:SUB * (k + 1)], axis=-1, keepdims=True)
    return jnp.concatenate(parts, axis=1), carry


def _sb_logits(qh, kT, mask):
    z = jnp.dot(qh, kT, preferred_element_type=F32)
    lp = jnp.minimum(z, 0.0) - jnp.log(1.0 + jnp.exp(-jnp.abs(z)))
    lf = lp - z
    if mask is not None:
        lf = jnp.where(mask, lf, 0.0)
    return lp, lf


def _causal_mask(row0, col0):
    r = row0 + lax.broadcasted_iota(jnp.int32, (TQ, KW), 0)
    c = col0 + lax.broadcasted_iota(jnp.int32, (TQ, KW), 1)
    return c < r


def _transposed_windows(x, ref):
    for w in range(x.shape[0] // KW):
        ref[w] = x[w * KW:(w + 1) * KW, :].T.astype(BF16)


def _attn_fwd(q, kv, qg, kg, *, n_ex, name):
    T, D = q.shape
    S = T // n_ex
    nhp = D // (2 * HEAD_DIM)
    nq = S // TQ
    scale = 1.0 / math.sqrt(HEAD_DIM)

    def body(q_ref, k_ref, v_ref, qg_ref, kg_ref, o_ref, tot_ref, kT_s, qm_s, vm_s):
        m0, m1 = _head_masks()
        qn, _ = _head_norm(q_ref[...], None, m0, m1)
        qn = qn * (qg_ref[...] * scale)
        kn, _ = _head_norm(k_ref[...], None, m0, m1)
        _transposed_windows(kn * kg_ref[...], kT_s)
        v = v_ref[...]
        for h, m in enumerate((m0, m1)):
            qm_s[h] = (qn * m).astype(BF16)
            vm_s[h] = (v * m).astype(BF16)
        u_after = _pair_matrix("after")

        def window(h, rows, win, carry, acc, mask):
            lp, lf = _sb_logits(qm_s[h, rows, :], kT_s[win], mask)
            after, carry = _block_sums(lf, u_after, carry, True)
            w = jnp.exp(lp + after)
            if mask is not None:
                w = jnp.where(mask, w, 0.0)
            keys = pl.ds(pl.multiple_of(win * KW, KW), KW)
            return carry, acc + jnp.dot(w.astype(BF16), vm_s[h, keys, :], preferred_element_type=F32)

        def qtile(iq, _):
            rows = pl.ds(pl.multiple_of(iq * TQ, TQ), TQ)
            last = (iq * TQ) // KW
            mask = _causal_mask(iq * TQ, last * KW)
            st = ()
            for h in range(2):
                st += window(h, rows, last, jnp.zeros((TQ, 1), F32), jnp.zeros((TQ, 2 * HEAD_DIM), F32), mask)

            def full(jj, st):
                out = ()
                for h in range(2):
                    out += window(h, rows, last - 1 - jj, st[2 * h], st[2 * h + 1], None)
                return out

            st = lax.fori_loop(0, last, full, st)
            o_ref[rows, :] = st[1] + st[3]
            tot_ref[rows, :] = st[0] * m0 + st[2] * m1
            return 0

        lax.fori_loop(0, nq, qtile, 0)

    assert S % KW == 0 and KW % TQ == 0
    nwin = S // KW
    blk = (S, 2 * HEAD_DIM)
    return pl.pallas_call(
        body, name=name, grid=(n_ex, nhp),
        in_specs=[pl.BlockSpec(blk, lambda e, h: (e, h)), pl.BlockSpec(blk, lambda e, h: (e, h)),
                  pl.BlockSpec(blk, lambda e, h: (e, h + nhp)),
                  pl.BlockSpec((1, 2 * HEAD_DIM), lambda e, h: (0, 0)), pl.BlockSpec((1, 2 * HEAD_DIM), lambda e, h: (0, 0))],
        out_specs=[pl.BlockSpec(blk, lambda e, h: (e, h))] * 2,
        out_shape=[jax.ShapeDtypeStruct((T, D), F32)] * 2,
        scratch_shapes=[pltpu.VMEM((nwin, 2 * HEAD_DIM, KW), BF16), pltpu.VMEM((2,) + blk, BF16), pltpu.VMEM((2,) + blk, BF16)],
        compiler_params=_cp(("parallel", "parallel")),
    )(q, kv, kv, qg, kg)


def _attn_bwd(q, kv, tot, do, qg, kg, *, n_ex, name):
    T, D = q.shape
    S = T // n_ex
    nhp = D // (2 * HEAD_DIM)
    nq = S // TQ
    scale = 1.0 / math.sqrt(HEAD_DIM)

    def body(q_ref, k_ref, v_ref, tot_ref, do_ref, qg_ref, kg_ref, dq_ref, dk_ref, dv_ref, dqg_ref, dkg_ref,
             kT_s, vT_s, km_s, qm_s, dom_s, dqn_s, dkT_s, dvT_s):
        m0, m1 = _head_masks()
        qn, qr = _head_norm(q_ref[...], None, m0, m1)
        kn, kr = _head_norm(k_ref[...], None, m0, m1)
        qs = qn * (qg_ref[...] * scale)
        kk = kn * kg_ref[...]
        _transposed_windows(kk, kT_s)
        _transposed_windows(v_ref[...], vT_s)
        do = do_ref[...]
        for h, m in enumerate((m0, m1)):
            qm_s[h] = (qs * m).astype(BF16)
            km_s[h] = (kk * m).astype(BF16)
            dom_s[h] = (do * m).astype(BF16)
        dkT_s[...] = jnp.zeros_like(dkT_s)
        dvT_s[...] = jnp.zeros_like(dvT_s)
        u_upto, u_before = _pair_matrix("upto"), _pair_matrix("before")

        def window(h, inv, win, pre_lf, pre_e, dq, mask):
            qh, qhT, doh, dohT, total = inv
            lp, lf = _sb_logits(qh, kT_s[win], mask)
            upto, pre_lf = _block_sums(lf, u_upto, pre_lf, False)
            w = jnp.exp(lp + (total - upto))
            if mask is not None:
                w = jnp.where(mask, w, 0.0)
            ew = jnp.dot(doh, vT_s[win], preferred_element_type=F32) * w
            dlf, pre_e = _block_sums(ew, u_before, pre_e, False)
            sig = jnp.exp(lp)
            dz = ew * (1.0 - sig) - dlf * sig
            if mask is not None:
                dz = jnp.where(mask, dz, 0.0)
            dzb = dz.astype(BF16)
            keys = pl.ds(pl.multiple_of(win * KW, KW), KW)
            dq = dq + jnp.dot(dzb, km_s[h, keys, :], preferred_element_type=F32)
            dkT_s[win] += jnp.dot(qhT, dzb, preferred_element_type=F32)
            dvT_s[win] += jnp.dot(dohT, w.astype(BF16), preferred_element_type=F32)
            return pre_lf, pre_e, dq

        def qtile(iq, _):
            rows = pl.ds(pl.multiple_of(iq * TQ, TQ), TQ)
            last = (iq * TQ) // KW
            mask = _causal_mask(iq * TQ, last * KW)
            tt = tot_ref[rows, :]
            inv = []
            for h, m in enumerate((m0, m1)):
                qh, doh = qm_s[h, rows, :], dom_s[h, rows, :]
                total = jnp.sum(tt * m, axis=-1, keepdims=True) * (1.0 / HEAD_DIM)
                inv.append((qh, qh.astype(F32).T.astype(BF16), doh, doh.astype(F32).T.astype(BF16), total))

            def both(win, st, mask):
                out = ()
                for h in range(2):
                    out += window(h, inv[h], win, st[3 * h], st[3 * h + 1], st[3 * h + 2], mask)
                return out

            z1, zq = jnp.zeros((TQ, 1), F32), jnp.zeros((TQ, 2 * HEAD_DIM), F32)
            st = lax.fori_loop(0, last, lambda win, st: both(win, st, None), (z1, z1, zq, z1, z1, zq))
            st = both(last, st, mask)
            dqn_s[rows, :] = st[2] + st[5]
            return 0

        lax.fori_loop(0, nq, qtile, 0)
        dkn = jnp.concatenate([dkT_s[w].T for w in range(nwin)], axis=0)
        dq, dqg = _head_norm_bwd(dqn_s[...] * scale, qn, qr, qg_ref[...], m0, m1)
        dk, dkg = _head_norm_bwd(dkn, kn, kr, kg_ref[...], m0, m1)
        dq_ref[...] = dq
        dk_ref[...] = dk
        dv_ref[...] = jnp.concatenate([dvT_s[w].T for w in range(nwin)], axis=0)
        dqg_ref[...] = dqg
        dkg_ref[...] = dkg

    assert S % KW == 0 and KW % TQ == 0
    nwin = S // KW
    blk = (S, 2 * HEAD_DIM)
    tblk = (nwin, 2 * HEAD_DIM, KW)
    gblk = (None, None, 1, 2 * HEAD_DIM)
    dq, dk, dv, dqg, dkg = pl.pallas_call(
        body, name=name, grid=(n_ex, nhp),
        in_specs=[pl.BlockSpec(blk, lambda e, h: (e, h)), pl.BlockSpec(blk, lambda e, h: (e, h)),
                  pl.BlockSpec(blk, lambda e, h: (e, h + nhp)),
                  pl.BlockSpec(blk, lambda e, h: (e, h)), pl.BlockSpec(blk, lambda e, h: (e, h)),
                  pl.BlockSpec((1, 2 * HEAD_DIM), lambda e, h: (0, 0)), pl.BlockSpec((1, 2 * HEAD_DIM), lambda e, h: (0, 0))],
        out_specs=[pl.BlockSpec(blk, lambda e, h: (e, h))] * 3 + [pl.BlockSpec(gblk, lambda e, h: (e, h, 0, 0))] * 2,
        out_shape=[jax.ShapeDtypeStruct((T, D), F32)] * 3 + [jax.ShapeDtypeStruct((n_ex, nhp, 1, 2 * HEAD_DIM), F32)] * 2,
        scratch_shapes=[pltpu.VMEM(tblk, BF16), pltpu.VMEM(tblk, BF16),
                        pltpu.VMEM((2,) + blk, BF16), pltpu.VMEM((2,) + blk, BF16), pltpu.VMEM((2,) + blk, BF16),
                        pltpu.VMEM(blk, F32), pltpu.VMEM(tblk, F32), pltpu.VMEM(tblk, F32)],
        compiler_params=_cp(("parallel", "parallel")),
    )(q, kv, kv, tot, do, qg, kg)
    return dq, dk, dv, dqg, dkg


def _place():
    return lax.axis_index("x"), lax.axis_index("y"), lax.axis_index("c")


def _all_gather8(x_shard, *, name):
    m_per, n = x_shard.shape

    def body(x_ref, out_ref, send_sems, recv_sems, local_sem):
        x, y, c = _place()
        me, sibling = (x, y, c), (x, y, 1 - c)
        chips = [(1 - x, y), (x, 1 - y), (1 - x, 1 - y)]

        def rows(px, py, pc):
            return out_ref.at[pl.ds((4 * px + 2 * py + pc) * m_per, m_per), :]

        def copy(k, block, to, src=None):
            return pltpu.make_async_remote_copy(
                src_ref=rows(*block) if src is None else src, dst_ref=rows(*block),
                send_sem=send_sems.at[k], recv_sem=recv_sems.at[k], device_id=to, device_id_type=MESH)

        mine = pltpu.make_async_copy(x_ref, rows(*me), local_sem)
        mine.start()
        first = [copy(0, me, sibling, src=x_ref)]
        first += [copy(1 + j, me, (*chip, c), src=x_ref) for j, chip in enumerate(chips)]
        for cp in first:
            cp.start()
        passed = [copy(4 + j, (*chip, c), sibling) for j, chip in enumerate(chips)]
        for j, chip in enumerate(chips):
            copy(1 + j, (*chip, c), me).wait_recv()
            passed[j].start()
        copy(0, sibling, me).wait_recv()
        for j, chip in enumerate(chips):
            copy(4 + j, (*chip, 1 - c), me).wait_recv()
        for cp in first + passed:
            cp.wait_send()
        mine.wait()

    return pl.pallas_call(
        body, name=name, out_shape=jax.ShapeDtypeStruct((8 * m_per, n), x_shard.dtype),
        in_specs=[pl.BlockSpec(memory_space=pltpu.VMEM)], out_specs=pl.BlockSpec(memory_space=pltpu.VMEM),
        scratch_shapes=[pltpu.SemaphoreType.DMA((7,)), pltpu.SemaphoreType.DMA((7,)), pltpu.SemaphoreType.DMA],
        compiler_params=pltpu.CompilerParams(vmem_limit_bytes=VMEM_LIMIT),
    )(x_shard)


def _sum_blocks(x, n, *, name):
    R = x.shape[0] // n

    def body(x_ref, o_ref):
        acc = x_ref[pl.ds(0, R), :]
        for k in range(1, n):
            acc = acc + x_ref[pl.ds(k * R, R), :]
        o_ref[...] = acc

    return pl.pallas_call(body, name=name, out_shape=jax.ShapeDtypeStruct((R, x.shape[1]), x.dtype),
                          compiler_params=pltpu.CompilerParams(vmem_limit_bytes=VMEM_LIMIT))(x)


def _colsum(x, *, name):
    def body(x_ref, o_ref):
        o_ref[...] = jnp.sum(x_ref[...], axis=0, keepdims=True)
    return pl.pallas_call(body, name=name, out_shape=jax.ShapeDtypeStruct((1, x.shape[1]), x.dtype))(x)


ANY = pl.BlockSpec(memory_space=pl.ANY)


class _ChipExchange:
    SCRATCH = [pltpu.SemaphoreType.DMA((3,)), pltpu.SemaphoreType.DMA((3,)), pltpu.SemaphoreType.DMA]

    @staticmethod
    def out_shape(src, scatter):
        return jax.ShapeDtypeStruct(((4,) + tuple(src.shape[1:])) if scatter else ((4, 2) + tuple(src.shape[1:])), src.dtype)

    def __init__(self, src_ref, out_ref, send_sems, recv_sems, local_sem, scatter):
        x, y, c = _place()
        myj = 2 * x + y
        chips = [(1 - x, y), (x, 1 - y), (1 - x, 1 - y)]

        def slot(j):
            return out_ref.at[j] if scatter else out_ref.at[j, c]

        def piece(j):
            return src_ref.at[j] if scatter else src_ref.at[c]

        self.mine = pltpu.make_async_copy(piece(myj), slot(myj), local_sem)
        self.sends = [pltpu.make_async_remote_copy(
            src_ref=piece(2 * cx + cy), dst_ref=slot(myj), send_sem=send_sems.at[k], recv_sem=recv_sems.at[k],
            device_id=(cx, cy, c), device_id_type=MESH) for k, (cx, cy) in enumerate(chips)]
        self.recvs = [pltpu.make_async_remote_copy(
            src_ref=slot(2 * cx + cy), dst_ref=slot(2 * cx + cy), send_sem=send_sems.at[k], recv_sem=recv_sems.at[k],
            device_id=(cx, cy, c), device_id_type=MESH) for k, (cx, cy) in enumerate(chips)]

    def start(self):
        self.mine.start()
        for cp in self.sends:
            cp.start()

    def wait(self):
        for cp in self.recvs:
            cp.wait_recv()
        for cp in self.sends:
            cp.wait_send()
        self.mine.wait()


def _sibling_fill(buf, *, axis, name):
    def half(ref, h):
        return ref.at[h] if axis == 0 else ref.at[:, h]

    def body(in_ref, out_ref, send_sem, recv_sem):
        x, y, c = _place()
        cp = pltpu.make_async_remote_copy(src_ref=half(out_ref, c), dst_ref=half(out_ref, c), send_sem=send_sem, recv_sem=recv_sem,
                                          device_id=(x, y, 1 - c), device_id_type=MESH)
        cp.start()
        pltpu.make_async_remote_copy(src_ref=half(out_ref, 1 - c), dst_ref=half(out_ref, 1 - c), send_sem=send_sem, recv_sem=recv_sem,
                                     device_id=(x, y, 1 - c), device_id_type=MESH).wait_recv()
        cp.wait_send()

    return pl.pallas_call(
        body, name=name, out_shape=jax.ShapeDtypeStruct(buf.shape, buf.dtype), in_specs=[ANY], out_specs=ANY,
        input_output_aliases={0: 0}, scratch_shapes=[pltpu.SemaphoreType.DMA, pltpu.SemaphoreType.DMA],
    )(buf)


def _sibling_swap_half(g, *, name):
    def body(g_ref, out_ref, send_sem, recv_sem):
        x, y, c = _place()
        cp = pltpu.make_async_remote_copy(src_ref=g_ref.at[:, 1 - c], dst_ref=out_ref, send_sem=send_sem, recv_sem=recv_sem,
                                          device_id=(x, y, 1 - c), device_id_type=MESH)
        cp.start()
        cp.wait()

    return pl.pallas_call(
        body, name=name, out_shape=jax.ShapeDtypeStruct((g.shape[0],) + g.shape[2:], g.dtype), in_specs=[ANY], out_specs=ANY,
        scratch_shapes=[pltpu.SemaphoreType.DMA, pltpu.SemaphoreType.DMA],
    )(g)


def _add_my_half(g, b, cidx, *, name, tr=256):
    n, _, R, C = g.shape
    tr = math.gcd(tr, R)

    def body(c_ref, g_ref, b_ref, o_ref):
        o_ref[...] = (g_ref[...] + b_ref[...]).astype(o_ref.dtype)

    return pl.pallas_call(
        body, name=name, out_shape=jax.ShapeDtypeStruct((n, R, C), BF16),
        grid_spec=pltpu.PrefetchScalarGridSpec(
            num_scalar_prefetch=1, grid=(n, R // tr),
            in_specs=[pl.BlockSpec((None, None, tr, C), lambda j, i, c: (j, c[0], i, 0)),
                      pl.BlockSpec((None, tr, C), lambda j, i, c: (j, i, 0))],
            out_specs=pl.BlockSpec((None, tr, C), lambda j, i, c: (j, i, 0))),
        compiler_params=_cp(("parallel", "parallel")),
    )(cidx, g, b)


def _sum4_into_half(q, cidx, *, name, tr=256):
    _, R, C = q.shape
    tr = math.gcd(tr, R)

    def body(c_ref, q_ref, o_ref):
        o_ref[...] = ((q_ref[0].astype(F32) + q_ref[1].astype(F32)) + q_ref[2].astype(F32)) + q_ref[3].astype(F32)

    return pl.pallas_call(
        body, name=name, out_shape=jax.ShapeDtypeStruct((2, R, C), F32),
        grid_spec=pltpu.PrefetchScalarGridSpec(
            num_scalar_prefetch=1, grid=(R // tr,),
            in_specs=[pl.BlockSpec((4, tr, C), lambda i, c: (0, i, 0))],
            out_specs=pl.BlockSpec((None, tr, C), lambda i, c: (c[0], i, 0))),
        compiler_params=_cp(("parallel",)),
    )(cidx, q)


def _pack_rows(parts, width=1024):
    rows, spans, r0 = [], [], 0
    for p in parts:
        n = p.size
        nr = 8 * (-(-n // (8 * width)))
        flat = p.reshape(-1)
        if nr * width != n:
            flat = jnp.pad(flat, (0, nr * width - n))
        rows.append(flat.reshape(nr, width))
        spans.append((r0, nr, n, p.shape))
        r0 += nr
    return jnp.concatenate(rows, axis=0), spans


def _unpack_rows(buf, spans):
    return [buf[r0:r0 + nr].reshape(-1)[:n].reshape(shape) for (r0, nr, n, shape) in spans]


def kernel(x, c, ada_w, ada_b, mix_norm_g, mlp_norm_g, mlp_w1, mlp_w2, s5_a_re, s5_a_im, s5_log_dt, s5_b_re, s5_b_im, s5_c_re, s5_c_im, s5_d, s5_w_glu, kv_ada_w, kv_ada_b, kv_norm_g, w_kv, k_norm_g, sb_w_q, q_norm_g, sb_w_o, loss_target, m_ada_w, m_ada_b, m_mix_norm_g, m_mlp_norm_g, m_mlp_w1, m_mlp_w2, m_s5_a_re, m_s5_a_im, m_s5_log_dt, m_s5_b_re, m_s5_b_im, m_s5_c_re, m_s5_c_im, m_s5_d, m_s5_w_glu, m_kv_ada_w, m_kv_ada_b, m_kv_norm_g, m_w_kv, m_k_norm_g, m_sb_w_q, m_q_norm_g, m_sb_w_o, v_ada_w, v_ada_b, v_mix_norm_g, v_mlp_norm_g, v_mlp_w1, v_mlp_w2, v_s5_a_re, v_s5_a_im, v_s5_log_dt, v_s5_b_re, v_s5_b_im, v_s5_c_re, v_s5_c_im, v_s5_d, v_s5_w_glu, v_kv_ada_w, v_kv_ada_b, v_kv_norm_g, v_w_kv, v_k_norm_g, v_sb_w_q, v_q_norm_g, v_sb_w_o):
    E, S, D = x.shape
    T = E * S
    FF = 4 * D
    NB = 8 * E
    px, py, pc = _place()
    chip = 2 * px + py
    dev = 4 * px + 2 * py + pc
    cidx = jnp.reshape(pc, (1,)).astype(jnp.int32)
    x0 = x.reshape(T, D)
    tgt = loss_target.reshape(T, D)

    c_all = _all_gather8(c.reshape(-1, 128), name="ag_c").reshape(NB, D)
    sc_all = (c_all * _sigmoid(c_all)).astype(BF16)
    wa = ada_w.shape[2]
    wk = kv_ada_w.shape[1]
    m_sh = jnp.concatenate([_mm(sc_all, ada_w[0], "nn", name="ada0", tn=256),
                            _mm(sc_all, ada_w[1], "nn", name="ada1", tn=256),
                            _mm(sc_all, kv_ada_w, "nn", name="ada_kv", tn=256)], axis=1)
    m_all = _all_gather8(m_sh, name="ag_m").reshape(4, 2, NB, 2 * wa + wk)[:, 0]
    mods = []
    for l in range(2):
        full = jnp.transpose(m_all[:, :, l * wa:(l + 1) * wa], (1, 0, 2)).reshape(NB, 6 * D) + ada_b[l]
        mine = lax.dynamic_slice_in_dim(full, E * dev, E, axis=0)
        mods.append([mine[:, i * D:(i + 1) * D].reshape(E, 1, D) for i in range(6)])
    full = jnp.transpose(m_all[:, :, 2 * wa:], (1, 0, 2)).reshape(NB, 2 * D) + kv_ada_b
    mine = lax.dynamic_slice_in_dim(full, E * dev, E, axis=0)
    kv_sh, kv_sc = [mine[:, i * D:(i + 1) * D].reshape(E, 1, D) for i in range(2)]

    wrows = [mlp_w1[0], mlp_w1[1], mlp_w2[0], mlp_w2[1], jnp.concatenate([s5_w_glu[0], w_kv], axis=1), sb_w_q[0], sb_w_o[0]]
    wpack = jnp.concatenate(wrows, axis=0).astype(BF16)
    RW = wpack.shape[0]

    tm = min(1024, S)
    tkw = 2048

    def mlp_fwd(xa, l, mod):
        sh_m, sc_m, g_m = mod[3], mod[4], mod[5]
        h = _norm_mod_fwd(xa, mlp_norm_g[l:l + 1], sh_m, sc_m, n_ex=E, out_dtype=BF16, name=f"mlp_norm{l}")
        r = _mm(h, W1[l], "nn", name=f"mlp_up{l}", out_dtypes=(BF16,), tm=tm,
                epilogue=lambda acc: (jnp.square(jnp.maximum(acc, 0.0)),))
        xb, ff = _mm(r, W2[l], "nn", name=f"mlp_down{l}", out_dtypes=(F32, F32), tm=tm,
                     extras=[_mn_extra(xa), _vec_extra(g_m, S)],
                     epilogue=lambda acc, xat, gt: (xat + gt * acc, acc))
        return xb, (h, r, ff)

    def mlp_bwd(dxb, xa, l, mod, saved):
        sc_m, g_m = mod[4], mod[5]
        h, r, ff = saved
        (dff,), (dgm,) = _rowwise(lambda d, f, g: ([g * d], [_csum(d * f)]), [(dxb, D, 0), (ff, D, 0)], [g_m], [],
                                  [(D, BF16)], [D], n_ex=E, name=f"mlp_gate_bwd{l}")
        da = _mm(dff, W2[l], "nt", name=f"mlp_down_dx{l}", out_dtypes=(BF16,), tm=tm, extras=[_mn_extra(r)],
                 epilogue=lambda acc, rt: (acc * (2.0 * jnp.sqrt(rt.astype(F32))),))
        dW2 = _mm(r, dff, "tn", name=f"mlp_down_dw{l}", tk=tkw)
        dh = _mm(da, W1[l], "nt", name=f"mlp_up_dx{l}", tm=tm)
        dW1 = _mm(h, da, "tn", name=f"mlp_up_dw{l}", tk=tkw)
        (dxa,), (dsh, dsc, dg) = _norm_mod_bwd(xa, dh, dxb, mlp_norm_g[l:l + 1], sc_m, n_ex=E, name=f"mlp_norm_bwd{l}")
        return dxa, dW1, dW2, (dsh, dsc, dgm), dg

    ab_re, ab_im, bb_re, bb_im = _s5_disc(s5_a_re[0], s5_a_im[0], s5_log_dt[0], s5_b_re[0], s5_b_im[0])
    cf, cr = _s5_consts(ab_re, ab_im)
    Wb, Wc = _s5_blockdiag(bb_re, bb_im, s5_c_re[0], s5_c_im[0])
    ng = D // U_LANES
    nd = s5_d.size // 128
    d_full = _all_gather8(jnp.pad(s5_d.reshape(nd, 128), ((0, 8 - nd), (0, 0))), name="ag_d")
    d_full = d_full.reshape(4, 2, 8, 128)[:, 0, :nd].reshape(1, D)

    mod0, mod1 = mods
    h0 = _norm_mod_fwd(x0, mix_norm_g[0:1], mod0[0], mod0[1], n_ex=E, out_dtype=F32, name="mix_norm0")
    y, gy, wfull = _s5_fwd(h0, Wb, Wc, cf, d_full, wpack.reshape(2, RW // 2, D), n_ex=E, name="s5_fwd")
    wfull = _sibling_fill(wfull, axis=1, name="wgather_d2d").reshape(4, RW, D)

    def cols(r0, nr, c0, nc):
        return jnp.transpose(wfull[:, r0:r0 + nr, c0:c0 + nc], (1, 0, 2)).reshape(nr, 4 * nc)

    def rws(r0, nr):
        return wfull[:, r0:r0 + nr, :].reshape(4 * nr, D)

    W1 = [cols(0, D, 0, D), cols(D, D, 0, D)]
    W2 = [rws(2 * D, D), rws(3 * D, D)]
    Wglu = cols(4 * D, D, 0, D // 2)
    Wkv = cols(4 * D, D, D // 2, D // 2)
    Wq = rws(5 * D, D // 4)
    Wo = rws(5 * D + D // 4, D // 4)
    vg = _mm(gy, Wglu, "nn", name="glu_up", tm=tm)
    (x1,), _ = _rowwise(lambda v, g, xt, ga: ([xt + ga * (v * _sigmoid(g))], []),
                        [(vg, D, 0), (vg, D, 1), (x0, D, 0)], [mod0[2]], [], [(D, F32)], [], n_ex=E, name="glu_gate")
    x2, saved_mlp0 = mlp_fwd(x1, 0, mod0)

    hkv = _norm_mod_fwd(x2, kv_norm_g.reshape(1, D), kv_sh, kv_sc, n_ex=E, out_dtype=BF16, name="kv_norm")
    kvf = _mm(hkv, Wkv, "nn", name="kv_proj", tm=tm)
    h1 = _norm_mod_fwd(x2, mix_norm_g[1:2], mod1[0], mod1[1], n_ex=E, out_dtype=BF16, name="mix_norm1")
    qf = _mm(h1, Wq, "nn", name="q_proj", tm=tm)
    qg2 = jnp.tile(q_norm_g.reshape(1, HEAD_DIM), (1, 2))
    kg2 = jnp.tile(k_norm_g.reshape(1, HEAD_DIM), (1, 2))
    o, lf_tot = _attn_fwd(qf, kvf, qg2, kg2, n_ex=E, name="attn_fwd")
    x3, mix1 = _mm(o, Wo, "nn", name="o_proj", out_dtypes=(F32, F32), tm=tm,
                   extras=[_mn_extra(x2), _vec_extra(mod1[2], S)],
                   epilogue=lambda acc, xat, gt: (xat + gt * acc, acc))
    x4, saved_mlp1 = mlp_fwd(x3, 1, mod1)

    (dx4,), (lsum,) = _rowwise(lambda xt, tt: ([(xt - tt) * (1.0 / D)], [_csum(jnp.square(xt - tt)) * (0.5 / D)]),
                               [(x4, D, 0), (tgt, D, 0)], [], [], [(D, F32)], [D], n_ex=E, name="loss")
    loss = lax.psum(jnp.sum(lsum), ("x", "y", "c"))

    dx3, dW1_1, dW2_1, (dsh_m1, dsc_m1, dgm1), dg_mlp1 = mlp_bwd(dx4, x3, 1, mod1, saved_mlp1)
    (dmix1,), (dga1,) = _rowwise(lambda d, f, g: ([g * d], [_csum(d * f)]), [(dx3, D, 0), (mix1, D, 0)], [mod1[2]], [],
                                 [(D, BF16)], [D], n_ex=E, name="attn_gate_bwd")
    do = _mm(dmix1, Wo, "nt", name="o_proj_dx", tm=tm)
    dWo = _mm(o, dmix1, "tn", name="o_proj_dw", tk=tkw)
    dq, dk, dv, dqg, dkg = _attn_bwd(qf, kvf, lf_tot, do, qg2, kg2, n_ex=E, name="attn_bwd")
    dh1 = _mm(dq, Wq, "nt", name="q_proj_dx", tm=tm)
    dWq = _mm(h1, dq, "tn", name="q_proj_dw", tk=tkw)
    (dx2,), (dsh_a1, dsc_a1, dg_mix1) = _norm_mod_bwd(x2, dh1, dx3, mix_norm_g[1:2], mod1[1], n_ex=E, name="mix_norm_bwd1")
    dkv = jnp.concatenate([dk, dv], axis=1)
    dhkv = _mm(dkv, Wkv, "nt", name="kv_proj_dx", tm=tm)
    dWkv = _mm(hkv, dkv, "tn", name="kv_proj_dw", tk=tkw)
    (dx2,), (dkv_sh, dkv_sc, dg_kv) = _norm_mod_bwd(x2, dhkv, dx2, kv_norm_g.reshape(1, D), kv_sc, n_ex=E, name="kv_norm_bwd")

    dx1, dW1_0, dW2_0, (dsh_m0, dsc_m0, dgm0), dg_mlp0 = mlp_bwd(dx2, x1, 0, mod0, saved_mlp0)

    def glu_bwd(v, g, d, ga):
        sg = _sigmoid(g)
        dm = ga * d
        return [jnp.concatenate([dm * sg, dm * v * sg * (1.0 - sg)], axis=1)], [_csum(d * (v * sg))]
    (dvg,), (dga0,) = _rowwise(glu_bwd, [(vg, D, 0), (vg, D, 1), (dx1, D, 0)], [mod0[2]], [], [(2 * D, BF16)], [D],
                               n_ex=E, name="glu_gate_bwd")
    dgy = _mm(dvg, Wglu, "nt", name="glu_up_dx", tm=tm)
    dWglu = _mm(gy, dvg, "tn", name="glu_up_dw", tk=tkw)

    def csh(g, nc):
        return jnp.transpose(g.reshape(g.shape[0], 4, nc), (1, 0, 2))

    gparts = [csh(dW1_0, D), csh(dW1_1, D), dW2_0.reshape(4, D, D), dW2_1.reshape(4, D, D),
              jnp.concatenate([csh(dWglu, D // 2), csh(dWkv, D // 2)], axis=2), dWq.reshape(4, D // 4, D), dWo.reshape(4, D // 4, D)]
    gpack = jnp.concatenate(gparts, axis=1).reshape(4, 2, RW // 2, D)
    theirs = _sibling_swap_half(gpack, name="gscatter_d2d")
    chip_sum = _add_my_half(gpack, theirs, cidx, name="gscatter_add")
    dh0, dWb, dWc, dab, dd, from_chips = _s5_bwd(h0, y, dgy, Wb, Wc, cf, cr, d_full, chip_sum, n_ex=E, name="s5_bwd")
    ghalf = _sum4_into_half(from_chips, cidx, name="gscatter_sum")
    gsh = _sibling_fill(ghalf, axis=0, name="gscatter_fill").reshape(RW, D)
    (gx,), (dsh_a0, dsc_a0, dg_mix0) = _norm_mod_bwd(x0, dh0, dx1, mix_norm_g[0:1], mod0[1], n_ex=E, name="mix_norm_bwd0")
    grad_x = gx.reshape(E, S, D)

    dm_mine = jnp.concatenate([t.reshape(E, D) for t in
                               (dsh_a0, dsc_a0, dga0, dsh_m0, dsc_m0, dgm0, dsh_a1, dsc_a1, dga1, dsh_m1, dsc_m1, dgm1, dkv_sh, dkv_sc)], axis=1)
    dm_all = _all_gather8(dm_mine.reshape(8, -1), name="ag_dm").reshape(NB, 14 * D)
    sc_f32 = c_all * _sigmoid(c_all)
    g_ada_w = jnp.stack([_mm(sc_f32, lax.dynamic_slice_in_dim(dm_all, l * 6 * D + chip * wa, wa, axis=1), "tn", name=f"ada_dw{l}", tn=256)
                         for l in range(2)])
    g_kv_ada_w = _mm(sc_f32, lax.dynamic_slice_in_dim(dm_all, 12 * D + chip * wk, wk, axis=1), "tn", name="ada_kv_dw", tn=256)
    db_all = _colsum(dm_all, name="ada_db")
    g_ada_b = db_all[0, :12 * D].reshape(2, 6 * D)
    g_kv_ada_b = db_all[0, 12 * D:]

    dWb_re, dWb_im, dC_re, dC_im = _s5_unblock(dWb, dWc)
    small_parts = [dg_mix0.sum(0), dg_mix1.sum(0), dg_mlp0.sum(0), dg_mlp1.sum(0), dg_kv.sum(0),
                   dqg.sum((0, 1, 2)).reshape(2, HEAD_DIM).sum(0), dkg.sum((0, 1, 2)).reshape(2, HEAD_DIM).sum(0),
                   dd[:, 0, :], dab[:, 0, :], dab[:, 1, :], dWb_re, dWb_im, dC_re, dC_im]
    spack, spans = _pack_rows(small_parts)
    ssum = _sum_blocks(_all_gather8(spack, name="ag_small"), 8, name="sum_small")
    (g_mix0, g_mix1, g_mlp0, g_mlp1, g_kvn, g_qn, g_kn, g_d, g_abr, g_abi, g_bbr, g_bbi, g_cre, g_cim) = _unpack_rows(ssum, spans)
    _, disc_vjp = jax.vjp(_s5_disc, s5_a_re[0], s5_a_im[0], s5_log_dt[0], s5_b_re[0], s5_b_im[0])
    g_are, g_aim, g_ldt, g_bre, g_bim = disc_vjp((g_abr.reshape(ab_re.shape), g_abi.reshape(ab_im.shape), g_bbr, g_bbi))
    g_s5d = lax.dynamic_slice_in_dim(g_d.reshape(1, D), chip * s5_d.shape[1], s5_d.shape[1], axis=1)

    def upd_big(w, m, v, roff, cb, name):
        shape = w.shape
        W = shape[-1]
        d_, m_, v_, g_ = _adamw2d(w.reshape(-1, W), gsh, m.reshape(-1, W), v.reshape(-1, W), name=name, g_roff=roff, g_cb=cb)
        return [t.reshape(shape) for t in (g_, d_, m_, v_)]

    def upd_own(w, g, m, v, name):
        shape = w.shape
        W = shape[-1]
        d_, m_, v_, g_ = _adamw2d(w.reshape(-1, W), g.reshape(-1, W), m.reshape(-1, W), v.reshape(-1, W), name=name)
        return [t.reshape(shape) for t in (g_, d_, m_, v_)]

    res = {}
    res["ada_w"] = upd_own(ada_w, g_ada_w, m_ada_w, v_ada_w, "adam_ada_w")
    res["kv_ada_w"] = upd_own(kv_ada_w, g_kv_ada_w, m_kv_ada_w, v_kv_ada_w, "adam_kv_ada_w")
    res["mlp_w1"] = upd_big(mlp_w1, m_mlp_w1, v_mlp_w1, 0, 0, "adam_w1")
    res["mlp_w2"] = upd_big(mlp_w2, m_mlp_w2, v_mlp_w2, 2 * D, 0, "adam_w2")
    res["s5_w_glu"] = upd_big(s5_w_glu, m_s5_w_glu, v_s5_w_glu, 4 * D, 0, "adam_glu")
    res["w_kv"] = upd_big(w_kv, m_w_kv, v_w_kv, 4 * D, 1, "adam_wkv")
    res["sb_w_q"] = upd_big(sb_w_q, m_sb_w_q, v_sb_w_q, 5 * D, 0, "adam_wq")
    res["sb_w_o"] = upd_big(sb_w_o, m_sb_w_o, v_sb_w_o, 5 * D + D // 4, 0, "adam_wo")

    small = {
        "ada_b": (ada_b, g_ada_b, m_ada_b, v_ada_b),
        "mix_norm_g": (mix_norm_g, jnp.stack([g_mix0, g_mix1]), m_mix_norm_g, v_mix_norm_g),
        "mlp_norm_g": (mlp_norm_g, jnp.stack([g_mlp0, g_mlp1]), m_mlp_norm_g, v_mlp_norm_g),
        "s5_a_re": (s5_a_re, g_are[None], m_s5_a_re, v_s5_a_re),
        "s5_a_im": (s5_a_im, g_aim[None], m_s5_a_im, v_s5_a_im),
        "s5_log_dt": (s5_log_dt, g_ldt[None], m_s5_log_dt, v_s5_log_dt),
        "s5_b_re": (s5_b_re, g_bre[None], m_s5_b_re, v_s5_b_re),
        "s5_b_im": (s5_b_im, g_bim[None], m_s5_b_im, v_s5_b_im),
        "s5_c_re": (s5_c_re, g_cre[None], m_s5_c_re, v_s5_c_re),
        "s5_c_im": (s5_c_im, g_cim[None], m_s5_c_im, v_s5_c_im),
        "s5_d": (s5_d, g_s5d, m_s5_d, v_s5_d),
        "kv_ada_b": (kv_ada_b, g_kv_ada_b, m_kv_ada_b, v_kv_ada_b),
        "kv_norm_g": (kv_norm_g, g_kvn, m_kv_norm_g, v_kv_norm_g),
        "k_norm_g": (k_norm_g, g_kn, m_k_norm_g, v_k_norm_g),
        "q_norm_g": (q_norm_g, g_qn.reshape(q_norm_g.shape), m_q_norm_g, v_q_norm_g),
    }
    names = list(small)
    packs = [_pack_rows([small[n][i].reshape(small[n][0].shape) for n in names]) for i in range(4)]
    sp = packs[0][1]
    d_, m_, v_, g_ = _adamw2d(packs[0][0], packs[1][0], packs[2][0], packs[3][0], name="adam_small")
    for n, gg, dd_, mm_, vv_ in zip(names, _unpack_rows(g_, sp), _unpack_rows(d_, sp), _unpack_rows(m_, sp), _unpack_rows(v_, sp)):
        res[n] = [gg, dd_, mm_, vv_]

    order = ["ada_w", "ada_b", "mix_norm_g", "mlp_norm_g", "mlp_w1", "mlp_w2", "s5_a_re", "s5_a_im", "s5_log_dt", "s5_b_re", "s5_b_im",
             "s5_c_re", "s5_c_im", "s5_d", "s5_w_glu", "kv_ada_w", "kv_ada_b", "kv_norm_g", "w_kv", "k_norm_g", "sb_w_q", "q_norm_g", "sb_w_o"]
    return (loss, grad_x, *[res[n][0] for n in order], *[res[n][1] for n in order], *[res[n][2] for n in order], *[res[n][3] for n in order])
```

```python
import functools
import math

import jax
import jax.numpy as jnp
from jax import lax
from jax.experimental import pallas as pl
from jax.experimental.pallas import tpu as pltpu

F32 = jnp.float32
BF16 = jnp.bfloat16
EPS = 1e-6
HEAD_DIM = 64
S5_GROUP = 16
S5_STATE = 64
GROUPS_PER_STEP = 8
U_LANES = GROUPS_PER_STEP * S5_GROUP
ST_LANES = GROUPS_PER_STEP * S5_STATE
SCAN_LANES = 256
SCAN_UNROLL = 4
VMEM_LIMIT = 56 * 1024 * 1024
ADAM_LR, ADAM_B1, ADAM_B2, ADAM_EPS, ADAM_WD, ADAM_STEP = 0.001, 0.9, 0.999, 1e-08, 0.01, 10
MESH = pl.DeviceIdType.MESH


def _cp(sem):
    return pltpu.CompilerParams(dimension_semantics=sem, vmem_limit_bytes=VMEM_LIMIT)


def _mm(a, b, dims, *, name, out_dtypes=(F32,), epilogue=None, extras=(), tm=512, tn=1024, tk=1024):
    if dims == "nn":
        (M, K), (_, N) = a.shape, b.shape
    elif dims == "nt":
        (M, K), (N, _) = a.shape, b.shape
    else:
        (K, M), (_, N) = a.shape, b.shape
    tm, tn, tk = min(tm, M), min(tn, N), min(tk, K)
    assert M % tm == 0 and N % tn == 0 and K % tk == 0, (M, N, K, tm, tn, tk)
    nk = K // tk
    extras = [e(tm, tn) for e in extras]
    a_spec = pl.BlockSpec((tk, tm), lambda i, j, k: (k, i)) if dims == "tn" else pl.BlockSpec((tm, tk), lambda i, j, k: (i, k))
    b_spec = pl.BlockSpec((tn, tk), lambda i, j, k: (j, k)) if dims == "nt" else pl.BlockSpec((tk, tn), lambda i, j, k: (k, j))
    contract = {"nn": ((1,), (0,)), "nt": ((1,), (1,)), "tn": ((0,), (0,))}[dims]
    n_ex, n_out = len(extras), len(out_dtypes)

    def finish(r, ex, outs):
        res = epilogue(r, *[e[...] for e in ex]) if epilogue is not None else (r,)
        for o, v in zip(outs, res):
            o[...] = v.astype(o.dtype)

    def product(a_ref, b_ref):
        return lax.dot_general(a_ref[...].astype(BF16), b_ref[...].astype(BF16), (contract, ((), ())), preferred_element_type=F32)

    def body_one(a_ref, b_ref, *rest):
        finish(product(a_ref, b_ref), rest[:n_ex], rest[n_ex:])

    def body_acc(a_ref, b_ref, *rest):
        ex, outs, acc = rest[:n_ex], rest[n_ex:n_ex + n_out], rest[-1]
        k = pl.program_id(2)

        @pl.when(k == 0)
        def _():
            acc[...] = product(a_ref, b_ref)

        @pl.when(k > 0)
        def _():
            acc[...] += product(a_ref, b_ref)

        @pl.when(k == nk - 1)
        def _():
            finish(acc[...], ex, outs)

    out = pl.pallas_call(
        body_one if nk == 1 else body_acc, name=name, grid=(M // tm, N // tn, nk),
        in_specs=[a_spec, b_spec] + [pl.BlockSpec(blk, im) for (_, blk, im) in extras],
        out_specs=[pl.BlockSpec((tm, tn), lambda i, j, k: (i, j)) for _ in out_dtypes],
        out_shape=[jax.ShapeDtypeStruct((M, N), d) for d in out_dtypes],
        scratch_shapes=[] if nk == 1 else [pltpu.VMEM((tm, tn), F32)],
        compiler_params=_cp(("parallel", "parallel", "arbitrary")),
    )(a, b, *[e[0] for e in extras])
    return out if n_out > 1 else out[0]


def _mn_extra(arr):
    return lambda tm, tn: (arr, (tm, tn), lambda i, j, k: (i, j))


def _vec_extra(vec, S):
    return lambda tm, tn: (vec, (None, 1, tn), lambda i, j, k: ((i * tm) // S, 0, j))


def _rowwise(fn, rows, vecs=(), consts=(), out_rows=(), out_sums=(), *, n_ex, name, tr=256):
    rows = [r if len(r) == 4 else (*r, 0) for r in rows]
    S = min(r[0].shape[0] for r in rows if r[3] == 0) // n_ex
    tr = math.gcd(tr, S)
    assert S % tr == 0
    nb = S // tr
    in_specs = []
    for (arr, w, cb, roff) in rows:
        assert roff % tr == 0
        in_specs.append(pl.BlockSpec((tr, w), functools.partial(lambda e, i, cb, ro: (e * nb + i + ro, cb), cb=cb, ro=roff // tr)))
    for v in vecs:
        in_specs.append(pl.BlockSpec((None, 1, v.shape[-1]), lambda e, i: (e, 0, 0)))
    for c in consts:
        in_specs.append(pl.BlockSpec((1, c.shape[-1]), lambda e, i: (0, 0)))
    n_in, n_or, n_os = len(in_specs), len(out_rows), len(out_sums)
    out_specs = [pl.BlockSpec((tr, w), lambda e, i: (e * nb + i, 0)) for (w, _) in out_rows]
    out_specs += [pl.BlockSpec((None, 1, w), lambda e, i: (e, 0, 0)) for w in out_sums]
    out_shape = [jax.ShapeDtypeStruct((n_ex * S, w), d) for (w, d) in out_rows]
    out_shape += [jax.ShapeDtypeStruct((n_ex, 1, w), F32) for w in out_sums]

    def body(*refs):
        ins, o_r, o_s = refs[:n_in], refs[n_in:n_in + n_or], refs[n_in + n_or:]
        ro, so = fn(*[r[...] for r in ins])
        for o, v in zip(o_r, ro):
            o[...] = v.astype(o.dtype)
        i = pl.program_id(1)
        for o, v in zip(o_s, so):
            @pl.when(i == 0)
            def _(o=o, v=v):
                o[...] = v

            @pl.when(i > 0)
            def _(o=o, v=v):
                o[...] += v

    outs = pl.pallas_call(
        body, name=name, grid=(n_ex, nb), in_specs=in_specs, out_specs=out_specs, out_shape=out_shape,
        compiler_params=_cp(("parallel", "arbitrary")),
    )(*[r[0] for r in rows], *vecs, *consts)
    return outs[:n_or], outs[n_or:]


def _csum(x):
    return jnp.sum(x, axis=0, keepdims=True)


def _norm_mod_fwd(x, g, sh, sc, *, n_ex, out_dtype, name):
    def fn(xt, sht, sct, gt):
        r = lax.rsqrt(jnp.mean(xt * xt, axis=-1, keepdims=True) + EPS)
        return [(xt * r * gt) * (1.0 + sct) + sht], []
    D = x.shape[1]
    return _rowwise(fn, [(x, D, 0)], [sh, sc], [g], [(D, out_dtype)], [], n_ex=n_ex, name=name)[0][0]


def _norm_mod_bwd(x, dh, dres, g, sc, *, n_ex, name):
    def fn(xt, dht, drt, sct, gt):
        dht = dht.astype(F32)
        r = lax.rsqrt(jnp.mean(xt * xt, axis=-1, keepdims=True) + EPS)
        n = xt * r
        y = n * gt
        dy = dht * (1.0 + sct)
        dn = dy * gt
        dx = r * (dn - n * jnp.mean(dn * n, axis=-1, keepdims=True))
        return [drt + dx], [_csum(dht), _csum(dht * y), _csum(dy * n)]
    D = x.shape[1]
    return _rowwise(fn, [(x, D, 0), (dh, D, 0), (dres, D, 0)], [sc], [g], [(D, F32)], [D, D, D], n_ex=n_ex, name=name)


def _sigmoid(x):
    return 1.0 / (1.0 + jnp.exp(-x))


def _gelu(y):
    return 0.5 * y * (1.0 + jnp.tanh(0.7978845608028654 * (y + 0.044715 * y * y * y)))


def _gelu_grad(y):
    t = jnp.tanh(0.7978845608028654 * (y + 0.044715 * y * y * y))
    return 0.5 * (1.0 + t) + 0.5 * y * (1.0 - t * t) * 0.7978845608028654 * (1.0 + 3 * 0.044715 * y * y)


def _adamw_fn(w, g, m, v):
    m2 = ADAM_B1 * m + (1.0 - ADAM_B1) * g
    v2 = ADAM_B2 * v + (1.0 - ADAM_B2) * (g * g)
    m_hat = m2 / (1.0 - ADAM_B1 ** ADAM_STEP)
    v_hat = v2 / (1.0 - ADAM_B2 ** ADAM_STEP)
    delta = -ADAM_LR * (m_hat / (jnp.sqrt(v_hat) + ADAM_EPS) + ADAM_WD * w)
    return delta, m2, v2


def _adamw2d(w, g, m, v, *, name, g_roff=0, g_cb=0):
    R, W = w.shape

    def fn(wt, gt, mt, vt):
        d, m2, v2 = _adamw_fn(wt, gt, mt, vt)
        return [d, m2, v2, gt], []
    return _rowwise(fn, [(w, W, 0), (g, W, g_cb, g_roff), (m, W, 0), (v, W, 0)], [], [],
                    [(W, F32)] * 4, [], n_ex=1, name=name, tr=256)[0]


def _scan_tiles(re_ref, im_ref, cf, lane0, n_chunks, reverse, extra=None):
    L = SCAN_LANES
    lanes = pl.ds(lane0, L)
    A = [cf[i, :, lanes] for i in range(8)]
    shifts = (7, 6, 4) if reverse else (1, 2, 4)
    edge = 0 if reverse else 7

    def one_tile(cc, carry):
        cr, ci = carry[0], carry[1]
        rows = pl.ds(pl.multiple_of(cc * 8, 8), 8)
        xr, xi = re_ref[rows, lanes], im_ref[rows, lanes]
        for idx, sft in enumerate(shifts):
            ar, ai = A[2 * idx], A[2 * idx + 1]
            rr, ri = pltpu.roll(xr, sft, 0), pltpu.roll(xi, sft, 0)
            xr, xi = xr + ar * rr - ai * ri, xi + ar * ri + ai * rr
        pr, pi = A[6], A[7]
        xr, xi = xr + pr * cr - pi * ci, xi + pr * ci + pi * cr
        re_ref[rows, lanes] = xr
        im_ref[rows, lanes] = xi
        out = (jnp.broadcast_to(xr[edge:edge + 1, :], (8, L)), jnp.broadcast_to(xi[edge:edge + 1, :], (8, L)))
        if extra is not None:
            out = out + extra(cc, xr, xi, carry[2:])
        return out

    def body(c, carry):
        for u in range(SCAN_UNROLL):
            t = c * SCAN_UNROLL + u
            carry = one_tile((n_chunks - 1 - t) if reverse else t, carry)
        return carry

    assert n_chunks % SCAN_UNROLL == 0
    z = jnp.zeros((8, L), F32)
    init = (z, z) if extra is None else (z, z, z, z)
    return lax.fori_loop(0, n_chunks // SCAN_UNROLL, body, init)


def _s5_consts(ab_re, ab_im):
    ng = ab_re.shape[0] // GROUPS_PER_STEP
    ar, ai = ab_re.reshape(ng, 1, ST_LANES), ab_im.reshape(ng, 1, ST_LANES)

    def cmul(xr, xi, yr, yi):
        return xr * yr - xi * yi, xr * yi + xi * yr

    def build(ar, ai, reverse):
        pw = [(ar, ai)]
        for _ in range(7):
            pw.append(cmul(*pw[-1], ar, ai))
        row = jnp.arange(8).reshape(1, 8, 1)
        tiles = []
        for k in (1, 2, 4):
            keep = (row <= 7 - k) if reverse else (row >= k)
            tiles += [jnp.where(keep, pw[k - 1][0], 0.0), jnp.where(keep, pw[k - 1][1], 0.0)]
        order = [7 - r for r in range(8)] if reverse else list(range(8))
        tiles += [jnp.concatenate([pw[o][0] for o in order], axis=1), jnp.concatenate([pw[o][1] for o in order], axis=1)]
        return jnp.stack([jnp.broadcast_to(t, (ng, 8, ST_LANES)) for t in tiles], axis=1)

    return build(ar, ai, False), build(ar, -ai, True)


def _s5_blockdiag(bb_re, bb_im, c_re, c_im):
    G = bb_re.shape[0]
    ng = G // GROUPS_PER_STEP
    eye = jnp.eye(GROUPS_PER_STEP, dtype=F32)

    def wb(bb):
        return jnp.einsum("bgph,gk->bghkp", bb.reshape(ng, GROUPS_PER_STEP, S5_STATE, S5_GROUP), eye).reshape(ng, U_LANES, ST_LANES)

    def wc(cc):
        return jnp.einsum("bghp,gk->bkpgh", cc.reshape(ng, GROUPS_PER_STEP, S5_GROUP, S5_STATE), eye).reshape(ng, ST_LANES, U_LANES)

    Wb = jnp.concatenate([wb(bb_re), wb(bb_im)], axis=2).astype(BF16)
    Wc = jnp.concatenate([wc(c_re), -wc(c_im)], axis=1).astype(BF16)
    return Wb, Wc


def _s5_unblock(dWb, dWc):
    ng = dWb.shape[0]
    eye = jnp.eye(GROUPS_PER_STEP, dtype=F32)

    def ub(w):
        return jnp.einsum("bghkp,gk->bgph", w.reshape(ng, GROUPS_PER_STEP, S5_GROUP, GROUPS_PER_STEP, S5_STATE), eye).reshape(-1, S5_STATE, S5_GROUP)

    def uc(w):
        return jnp.einsum("bkpgh,gk->bghp", w.reshape(ng, GROUPS_PER_STEP, S5_STATE, GROUPS_PER_STEP, S5_GROUP), eye).reshape(-1, S5_GROUP, S5_STATE)

    return ub(dWb[:, :, :ST_LANES]), ub(dWb[:, :, ST_LANES:]), uc(dWc[:, :ST_LANES, :]), -uc(dWc[:, ST_LANES:, :])


def _s5_disc(a_re, a_im, log_dt, b_re, b_im):
    dt = jnp.exp(log_dt)[:, None]
    mag = jnp.exp(a_re * dt)
    ab_re = mag * jnp.cos(a_im * dt)
    ab_im = mag * jnp.sin(a_im * dt)
    den = a_re * a_re + a_im * a_im
    nr, ni = ab_re - 1, ab_im
    f_re = (nr * a_re + ni * a_im) / den
    f_im = (ni * a_re - nr * a_im) / den
    bb_re = f_re[..., None] * b_re - f_im[..., None] * b_im
    bb_im = f_re[..., None] * b_im + f_im[..., None] * b_re
    return ab_re, ab_im, bb_re, bb_im


ROW_CHUNK = 512


def _s5_fwd(u, Wb, Wc, cf, d, xsrc, *, n_ex, name):
    T, D = u.shape
    S = T // n_ex
    ng = D // U_LANES
    rc = min(ROW_CHUNK, S)

    def body(u_ref, wb_ref, wc_ref, cf_ref, d_ref, xsrc_ref, y_ref, gy_ref, xout_ref, re_s, im_s, *sems):
        step = pl.program_id(0) * ng + pl.program_id(1)
        exch = _ChipExchange(xsrc_ref, xout_ref, *sems, scatter=False)

        @pl.when(step == 0)
        def _():
            exch.start()

        for r in range(S // rc):
            rows = pl.ds(r * rc, rc)
            bu = jnp.dot(u_ref[rows, :].astype(BF16), wb_ref[...], preferred_element_type=F32)
            re_s[rows, :] = bu[:, :ST_LANES]
            im_s[rows, :] = bu[:, ST_LANES:]
        for l0 in range(0, ST_LANES, SCAN_LANES):
            _scan_tiles(re_s, im_s, cf_ref, l0, S // 8, False)
        for r in range(S // rc):
            rows = pl.ds(r * rc, rc)
            st = jnp.concatenate([re_s[rows, :], im_s[rows, :]], axis=1).astype(BF16)
            y = jnp.dot(st, wc_ref[...], preferred_element_type=F32) + d_ref[...] * u_ref[rows, :]
            y_ref[rows, :] = y
            gy_ref[rows, :] = _gelu(y).astype(BF16)

        @pl.when(step == n_ex * ng - 1)
        def _():
            exch.wait()

    return pl.pallas_call(
        body, name=name, grid=(n_ex, ng),
        in_specs=[pl.BlockSpec((S, U_LANES), lambda e, g: (e, g)),
                  pl.BlockSpec((None, U_LANES, 2 * ST_LANES), lambda e, g: (g, 0, 0)),
                  pl.BlockSpec((None, 2 * ST_LANES, U_LANES), lambda e, g: (g, 0, 0)),
                  pl.BlockSpec((None, 8, 8, ST_LANES), lambda e, g: (g, 0, 0, 0)),
                  pl.BlockSpec((1, U_LANES), lambda e, g: (0, g)), ANY],
        out_specs=[pl.BlockSpec((S, U_LANES), lambda e, g: (e, g))] * 2 + [ANY],
        out_shape=[jax.ShapeDtypeStruct((T, D), F32), jax.ShapeDtypeStruct((T, D), BF16), _ChipExchange.out_shape(xsrc, False)],
        scratch_shapes=[pltpu.VMEM((S, ST_LANES), F32)] * 2 + _ChipExchange.SCRATCH,
        compiler_params=_cp(("arbitrary", "arbitrary")),
    )(u, Wb, Wc, cf, d, xsrc)


def _s5_bwd(u, y, dgy, Wb, Wc, cf, cr, d, xsrc, *, n_ex, name):
    T, D = u.shape
    S = T // n_ex
    ng = D // U_LANES
    rc = min(ROW_CHUNK, S)
    nch = S // 8

    def body(u_ref, y_ref, dgy_ref, wb_ref, wc_ref, cf_ref, cr_ref, d_ref, xsrc_ref,
             du_ref, dwb_ref, dwc_ref, dab_ref, dd_ref, xout_ref, re_s, im_s, gr_s, gi_s, dy_s, *sems):
        e = pl.program_id(1)
        step = pl.program_id(0) * n_ex + e
        exch = _ChipExchange(xsrc_ref, xout_ref, *sems, scatter=True)

        @pl.when(step == 0)
        def _():
            exch.start()

        @pl.when(e == 0)
        def _():
            dwb_ref[...] = jnp.zeros_like(dwb_ref)
            dwc_ref[...] = jnp.zeros_like(dwc_ref)
            dab_ref[...] = jnp.zeros_like(dab_ref)
            dd_ref[...] = jnp.zeros_like(dd_ref)

        dd = jnp.zeros((1, U_LANES), F32)
        for r in range(S // rc):
            rows = pl.ds(r * rc, rc)
            ut = u_ref[rows, :]
            bu = jnp.dot(ut.astype(BF16), wb_ref[...], preferred_element_type=F32)
            re_s[rows, :] = bu[:, :ST_LANES]
            im_s[rows, :] = bu[:, ST_LANES:]
            dy = dgy_ref[rows, :].astype(F32) * _gelu_grad(y_ref[rows, :])
            dy_s[rows, :] = dy
            dd = dd + _csum(dy * ut)
            go = lax.dot_general(dy.astype(BF16), wc_ref[...], (((1,), (1,)), ((), ())), preferred_element_type=F32)
            gr_s[rows, :] = go[:, :ST_LANES]
            gi_s[rows, :] = go[:, ST_LANES:]
        dd_ref[0:1, :] += dd
        row0 = lax.broadcasted_iota(jnp.int32, (8, SCAN_LANES), 0) == 0
        for l0 in range(0, ST_LANES, SCAN_LANES):
            lanes = pl.ds(l0, SCAN_LANES)
            _scan_tiles(re_s, im_s, cf_ref, l0, nch, False)

            def dab_part(cc, gr, gi, acc, lanes=lanes):
                rows = pl.ds(pl.multiple_of(cc * 8, 8), 8)
                prev = pl.ds(pl.multiple_of(jnp.maximum(cc - 1, 0) * 8, 8), 8)
                live = (cc > 0).astype(F32)
                sr = jnp.where(row0, pltpu.roll(re_s[prev, lanes], 1, 0) * live, pltpu.roll(re_s[rows, lanes], 1, 0))
                si = jnp.where(row0, pltpu.roll(im_s[prev, lanes], 1, 0) * live, pltpu.roll(im_s[rows, lanes], 1, 0))
                return (acc[0] + gr * sr + gi * si, acc[1] + gi * sr - gr * si)

            res = _scan_tiles(gr_s, gi_s, cr_ref, l0, nch, True, extra=dab_part)
            dab_ref[0:1, lanes] += _csum(res[2])
            dab_ref[1:2, lanes] += _csum(res[3])
        for r in range(S // rc):
            rows = pl.ds(r * rc, rc)
            st = jnp.concatenate([re_s[rows, :], im_s[rows, :]], axis=1).astype(BF16)
            g = jnp.concatenate([gr_s[rows, :], gi_s[rows, :]], axis=1).astype(BF16)
            dyb = dy_s[rows, :].astype(BF16)
            dwc_ref[...] += lax.dot_general(st, dyb, (((0,), (0,)), ((), ())), preferred_element_type=F32)
            dwb_ref[...] += lax.dot_general(u_ref[rows, :].astype(BF16), g, (((0,), (0,)), ((), ())), preferred_element_type=F32)
            du = lax.dot_general(g, wb_ref[...], (((1,), (1,)), ((), ())), preferred_element_type=F32)
            du_ref[rows, :] = du + d_ref[...] * dy_s[rows, :]

        @pl.when(step == ng * n_ex - 1)
        def _():
            exch.wait()

    return pl.pallas_call(
        body, name=name, grid=(ng, n_ex),
        in_specs=[pl.BlockSpec((S, U_LANES), lambda g, e: (e, g))] * 3 + [
            pl.BlockSpec((None, U_LANES, 2 * ST_LANES), lambda g, e: (g, 0, 0)),
            pl.BlockSpec((None, 2 * ST_LANES, U_LANES), lambda g, e: (g, 0, 0)),
            pl.BlockSpec((None, 8, 8, ST_LANES), lambda g, e: (g, 0, 0, 0)),
            pl.BlockSpec((None, 8, 8, ST_LANES), lambda g, e: (g, 0, 0, 0)),
            pl.BlockSpec((1, U_LANES), lambda g, e: (0, g)), ANY],
        out_specs=[pl.BlockSpec((S, U_LANES), lambda g, e: (e, g)),
                   pl.BlockSpec((None, U_LANES, 2 * ST_LANES), lambda g, e: (g, 0, 0)),
                   pl.BlockSpec((None, 2 * ST_LANES, U_LANES), lambda g, e: (g, 0, 0)),
                   pl.BlockSpec((None, 8, ST_LANES), lambda g, e: (g, 0, 0)),
                   pl.BlockSpec((None, 8, U_LANES), lambda g, e: (g, 0, 0)), ANY],
        out_shape=[jax.ShapeDtypeStruct((T, D), F32),
                   jax.ShapeDtypeStruct((ng, U_LANES, 2 * ST_LANES), F32),
                   jax.ShapeDtypeStruct((ng, 2 * ST_LANES, U_LANES), F32),
                   jax.ShapeDtypeStruct((ng, 8, ST_LANES), F32),
                   jax.ShapeDtypeStruct((ng, 8, U_LANES), F32), _ChipExchange.out_shape(xsrc, True)],
        scratch_shapes=[pltpu.VMEM((S, ST_LANES), F32)] * 4 + [pltpu.VMEM((S, U_LANES), F32)] + _ChipExchange.SCRATCH,
        compiler_params=_cp(("arbitrary", "arbitrary")),
    )(u, y, dgy, Wb, Wc, cf, cr, d, xsrc)


TQ = 256
KW = 512
SUB = 128


def _head_masks():
    lane = lax.broadcasted_iota(jnp.int32, (1, 2 * HEAD_DIM), 1)
    m0 = (lane < HEAD_DIM).astype(F32)
    return m0, 1.0 - m0


def _head_norm(x, g, m0, m1):
    sq = x * x
    r0 = lax.rsqrt(jnp.sum(sq * m0, axis=-1, keepdims=True) / HEAD_DIM + EPS)
    r1 = lax.rsqrt(jnp.sum(sq * m1, axis=-1, keepdims=True) / HEAD_DIM + EPS)
    r = m0 * r0 + m1 * r1
    return x * r, r


def _head_norm_bwd(dy, n, r, g, m0, m1):
    dn = dy * g
    p = dn * n
    mean = (m0 * jnp.sum(p * m0, axis=-1, keepdims=True) + m1 * jnp.sum(p * m1, axis=-1, keepdims=True)) / HEAD_DIM
    return r * (dn - n * mean), _csum(dy * n)


def _pair_matrix(kind):
    r = lax.broadcasted_iota(jnp.int32, (2 * SUB, 2 * SUB), 0)
    c = lax.broadcasted_iota(jnp.int32, (2 * SUB, 2 * SUB), 1)
    same = (r < SUB) == (c < SUB)
    rel = {"after": r > c, "upto": r <= c, "before": r < c}[kind]
    return jnp.logical_and(same, rel).astype(BF16)


def _block_sums(x, mat, carry, reverse, terms=2):
    hi = x.astype(BF16)
    lo = (x - hi.astype(F32)).astype(BF16) if terms == 2 else None
    npair = KW // (2 * SUB)
    parts = [None] * (2 * npair)
    for p in (range(npair - 1, -1, -1) if reverse else range(npair)):
        sl = slice(2 * SUB * p, 2 * SUB * (p + 1))
        loc = jnp.dot(hi[:, sl], mat, preferred_element_type=F32)
        if terms == 2:
            loc = loc + jnp.dot(lo[:, sl], mat, preferred_element_type=F32)
        for b in ((1, 0) if reverse else (0, 1)):
            k = 2 * p + b
            parts[k] = loc[:, SUB * b:SUB * (b + 1)] + carry
            carry = carry + jnp.sum(x[:, SUB * k:SUB * (k + 1)], axis=-1, keepdims=True)
    return jnp.concatenate(parts, axis=1), carry


def _sb_logits(qh, kT, mask):
    z = jnp.dot(qh, kT, preferred_element_type=F32)
    lp = jnp.minimum(z, 0.0) - jnp.log(1.0 + jnp.exp(-jnp.abs(z)))
    lf = lp - z
    if mask is not None:
        lf = jnp.where(mask, lf, 0.0)
    return lp, lf


def _causal_mask(row0, col0):
    r = row0 + lax.broadcasted_iota(jnp.int32, (TQ, KW), 0)
    c = col0 + lax.broadcasted_iota(jnp.int32, (TQ, KW), 1)
    return c < r


def _transposed_windows(x, ref):
    for w in range(x.shape[0] // KW):
        ref[w] = x[w * KW:(w + 1) * KW, :].T.astype(BF16)


def _attn_fwd(q, kv, qg, kg, xsrc, *, n_ex, name):
    T, D = q.shape
    S = T // n_ex
    nhp = D // (2 * HEAD_DIM)
    nq = S // TQ
    scale = 1.0 / math.sqrt(HEAD_DIM)

    def body(q_ref, k_ref, v_ref, qg_ref, kg_ref, xsrc_ref, o_ref, tot_ref, xout_ref, kT_s, qm_s, vm_s, *sems):
        step = pl.program_id(0) * nhp + pl.program_id(1)
        exch = _ChipExchange(xsrc_ref, xout_ref, *sems, scatter=False)

        @pl.when(step == 0)
        def _():
            exch.start()

        m0, m1 = _head_masks()
        qn, _ = _head_norm(q_ref[...], None, m0, m1)
        qn = qn * (qg_ref[...] * scale)
        kn, _ = _head_norm(k_ref[...], None, m0, m1)
        _transposed_windows(kn * kg_ref[...], kT_s)
        v = v_ref[...]
        for h, m in enumerate((m0, m1)):
            qm_s[h] = (qn * m).astype(BF16)
            vm_s[h] = (v * m).astype(BF16)
        u_after = _pair_matrix("after")

        def window(rows, win, st, mask):
            keys = pl.ds(pl.multiple_of(win * KW, KW), KW)
            lg = [_sb_logits(qm_s[h, rows, :], kT_s[win], mask) for h in range(2)]
            sums = [_block_sums(lg[h][1], u_after, st[2 * h], True) for h in range(2)]
            out = ()
            for h in range(2):
                w = jnp.exp(lg[h][0] + sums[h][0])
                if mask is not None:
                    w = jnp.where(mask, w, 0.0)
                out += (sums[h][1], st[2 * h + 1] + jnp.dot(w.astype(BF16), vm_s[h, keys, :], preferred_element_type=F32))
            return out

        def qtile(iq, _):
            rows = pl.ds(pl.multiple_of(iq * TQ, TQ), TQ)
            last = (iq * TQ) // KW
            mask = _causal_mask(iq * TQ, last * KW)
            z1, zq = jnp.zeros((TQ, 1), F32), jnp.zeros((TQ, 2 * HEAD_DIM), F32)
            st = window(rows, last, (z1, zq, z1, zq), mask)
            st = lax.fori_loop(0, last, lambda jj, st: window(rows, last - 1 - jj, st, None), st)
            o_ref[rows, :] = st[1] + st[3]
            tot_ref[rows, :] = st[0] * m0 + st[2] * m1
            return 0

        lax.fori_loop(0, nq, qtile, 0)

        @pl.when(step == n_ex * nhp - 1)
        def _():
            exch.wait()

    assert S % KW == 0 and KW % TQ == 0
    nwin = S // KW
    blk = (S, 2 * HEAD_DIM)
    return pl.pallas_call(
        body, name=name, grid=(n_ex, nhp),
        in_specs=[pl.BlockSpec(blk, lambda e, h: (e, h)), pl.BlockSpec(blk, lambda e, h: (e, h)),
                  pl.BlockSpec(blk, lambda e, h: (e, h + nhp)),
                  pl.BlockSpec((1, 2 * HEAD_DIM), lambda e, h: (0, 0)), pl.BlockSpec((1, 2 * HEAD_DIM), lambda e, h: (0, 0)), ANY],
        out_specs=[pl.BlockSpec(blk, lambda e, h: (e, h))] * 2 + [ANY],
        out_shape=[jax.ShapeDtypeStruct((T, D), F32)] * 2 + [_ChipExchange.out_shape(xsrc, False)],
        scratch_shapes=[pltpu.VMEM((nwin, 2 * HEAD_DIM, KW), BF16), pltpu.VMEM((2,) + blk, BF16), pltpu.VMEM((2,) + blk, BF16)]
        + _ChipExchange.SCRATCH,
        compiler_params=_cp(("arbitrary", "arbitrary")),
    )(q, kv, kv, qg, kg, xsrc)


def _attn_bwd(q, kv, tot, do, qg, kg, *, n_ex, name):
    T, D = q.shape
    S = T // n_ex
    nhp = D // (2 * HEAD_DIM)
    nq = S // TQ
    scale = 1.0 / math.sqrt(HEAD_DIM)

    def body(q_ref, k_ref, v_ref, tot_ref, do_ref, qg_ref, kg_ref, dq_ref, dk_ref, dv_ref, dqg_ref, dkg_ref,
             kT_s, vT_s, km_s, qm_s, dom_s, dqn_s, dkT_s, dvT_s):
        m0, m1 = _head_masks()
        qn, qr = _head_norm(q_ref[...], None, m0, m1)
        kn, kr = _head_norm(k_ref[...], None, m0, m1)
        qs = qn * (qg_ref[...] * scale)
        kk = kn * kg_ref[...]
        _transposed_windows(kk, kT_s)
        _transposed_windows(v_ref[...], vT_s)
        do = do_ref[...]
        for h, m in enumerate((m0, m1)):
            qm_s[h] = (qs * m).astype(BF16)
            km_s[h] = (kk * m).astype(BF16)
            dom_s[h] = (do * m).astype(BF16)
        dkT_s[...] = jnp.zeros_like(dkT_s)
        dvT_s[...] = jnp.zeros_like(dvT_s)
        u_upto, u_before = _pair_matrix("upto"), _pair_matrix("before")

        def both(inv, win, st, mask):
            keys = pl.ds(pl.multiple_of(win * KW, KW), KW)
            lg = [_sb_logits(inv[h][0], kT_s[win], mask) for h in range(2)]
            dw = [jnp.dot(inv[h][2], vT_s[win], preferred_element_type=F32) for h in range(2)]
            s_lf = [_block_sums(lg[h][1], u_upto, st[3 * h], False) for h in range(2)]
            ws, ews = [], []
            for h in range(2):
                w = jnp.exp(lg[h][0] + (inv[h][4] - s_lf[h][0]))
                if mask is not None:
                    w = jnp.where(mask, w, 0.0)
                ws.append(w)
                ews.append(dw[h] * w)
            s_e = [_block_sums(ews[h], u_before, st[3 * h + 1], False, terms=1) for h in range(2)]
            out, dk, dv = (), None, None
            for h in range(2):
                sig = jnp.exp(lg[h][0])
                dz = ews[h] * (1.0 - sig) - s_e[h][0] * sig
                if mask is not None:
                    dz = jnp.where(mask, dz, 0.0)
                dzb = dz.astype(BF16)
                out += (s_lf[h][1], s_e[h][1], st[3 * h + 2] + jnp.dot(dzb, km_s[h, keys, :], preferred_element_type=F32))
                dkh = jnp.dot(inv[h][1], dzb, preferred_element_type=F32)
                dvh = jnp.dot(inv[h][3], ws[h].astype(BF16), preferred_element_type=F32)
                dk, dv = (dkh, dvh) if h == 0 else (dk + dkh, dv + dvh)
            dkT_s[win] += dk
            dvT_s[win] += dv
            return out

        def qtile(iq, _):
            rows = pl.ds(pl.multiple_of(iq * TQ, TQ), TQ)
            last = (iq * TQ) // KW
            mask = _causal_mask(iq * TQ, last * KW)
            tt = tot_ref[rows, :]
            inv = []
            for h, m in enumerate((m0, m1)):
                qh, doh = qm_s[h, rows, :], dom_s[h, rows, :]
                total = jnp.sum(tt * m, axis=-1, keepdims=True) * (1.0 / HEAD_DIM)
                inv.append((qh, qh.astype(F32).T.astype(BF16), doh, doh.astype(F32).T.astype(BF16), total))

            z1, zq = jnp.zeros((TQ, 1), F32), jnp.zeros((TQ, 2 * HEAD_DIM), F32)
            st = lax.fori_loop(0, last, lambda win, st: both(inv, win, st, None), (z1, z1, zq, z1, z1, zq))
            st = both(inv, last, st, mask)
            dqn_s[rows, :] = st[2] + st[5]
            return 0

        lax.fori_loop(0, nq, qtile, 0)
        dkn = jnp.concatenate([dkT_s[w].T for w in range(nwin)], axis=0)
        dq, dqg = _head_norm_bwd(dqn_s[...] * scale, qn, qr, qg_ref[...], m0, m1)
        dk, dkg = _head_norm_bwd(dkn, kn, kr, kg_ref[...], m0, m1)
        dq_ref[...] = dq
        dk_ref[...] = dk
        dv_ref[...] = jnp.concatenate([dvT_s[w].T for w in range(nwin)], axis=0)
        dqg_ref[...] = dqg
        dkg_ref[...] = dkg

    assert S % KW == 0 and KW % TQ == 0
    nwin = S // KW
    blk = (S, 2 * HEAD_DIM)
    tblk = (nwin, 2 * HEAD_DIM, KW)
    gblk = (None, None, 1, 2 * HEAD_DIM)
    dq, dk, dv, dqg, dkg = pl.pallas_call(
        body, name=name, grid=(n_ex, nhp),
        in_specs=[pl.BlockSpec(blk, lambda e, h: (e, h)), pl.BlockSpec(blk, lambda e, h: (e, h)),
                  pl.BlockSpec(blk, lambda e, h: (e, h + nhp)),
                  pl.BlockSpec(blk, lambda e, h: (e, h)), pl.BlockSpec(blk, lambda e, h: (e, h)),
                  pl.BlockSpec((1, 2 * HEAD_DIM), lambda e, h: (0, 0)), pl.BlockSpec((1, 2 * HEAD_DIM), lambda e, h: (0, 0))],
        out_specs=[pl.BlockSpec(blk, lambda e, h: (e, h))] * 3 + [pl.BlockSpec(gblk, lambda e, h: (e, h, 0, 0))] * 2,
        out_shape=[jax.ShapeDtypeStruct((T, D), F32)] * 3 + [jax.ShapeDtypeStruct((n_ex, nhp, 1, 2 * HEAD_DIM), F32)] * 2,
        scratch_shapes=[pltpu.VMEM(tblk, BF16), pltpu.VMEM(tblk, BF16),
                        pltpu.VMEM((2,) + blk, BF16), pltpu.VMEM((2,) + blk, BF16), pltpu.VMEM((2,) + blk, BF16),
                        pltpu.VMEM(blk, F32), pltpu.VMEM(tblk, F32), pltpu.VMEM(tblk, F32)],
        compiler_params=_cp(("parallel", "parallel")),
    )(q, kv, kv, tot, do, qg, kg)
    return dq, dk, dv, dqg, dkg


def _place():
    return lax.axis_index("x"), lax.axis_index("y"), lax.axis_index("c")


def _all_gather8(x_shard, *, name):
    m_per, n = x_shard.shape

    def body(x_ref, out_ref, send_sems, recv_sems, local_sem):
        x, y, c = _place()
        me, sibling = (x, y, c), (x, y, 1 - c)
        chips = [(1 - x, y), (x, 1 - y), (1 - x, 1 - y)]

        def rows(px, py, pc):
            return out_ref.at[pl.ds((4 * px + 2 * py + pc) * m_per, m_per), :]

        def copy(k, block, to, src=None):
            return pltpu.make_async_remote_copy(
                src_ref=rows(*block) if src is None else src, dst_ref=rows(*block),
                send_sem=send_sems.at[k], recv_sem=recv_sems.at[k], device_id=to, device_id_type=MESH)

        mine = pltpu.make_async_copy(x_ref, rows(*me), local_sem)
        mine.start()
        first = [copy(0, me, sibling, src=x_ref)]
        first += [copy(1 + j, me, (*chip, c), src=x_ref) for j, chip in enumerate(chips)]
        for cp in first:
            cp.start()
        passed = [copy(4 + j, (*chip, c), sibling) for j, chip in enumerate(chips)]
        for j, chip in enumerate(chips):
            copy(1 + j, (*chip, c), me).wait_recv()
            passed[j].start()
        copy(0, sibling, me).wait_recv()
        for j, chip in enumerate(chips):
            copy(4 + j, (*chip, 1 - c), me).wait_recv()
        for cp in first + passed:
            cp.wait_send()
        mine.wait()

    return pl.pallas_call(
        body, name=name, out_shape=jax.ShapeDtypeStruct((8 * m_per, n), x_shard.dtype),
        in_specs=[pl.BlockSpec(memory_space=pltpu.VMEM)], out_specs=pl.BlockSpec(memory_space=pltpu.VMEM),
        scratch_shapes=[pltpu.SemaphoreType.DMA((7,)), pltpu.SemaphoreType.DMA((7,)), pltpu.SemaphoreType.DMA],
        compiler_params=pltpu.CompilerParams(vmem_limit_bytes=VMEM_LIMIT),
    )(x_shard)


def _sum_blocks(x, n, *, name):
    R = x.shape[0] // n

    def body(x_ref, o_ref):
        acc = x_ref[pl.ds(0, R), :]
        for k in range(1, n):
            acc = acc + x_ref[pl.ds(k * R, R), :]
        o_ref[...] = acc

    return pl.pallas_call(body, name=name, out_shape=jax.ShapeDtypeStruct((R, x.shape[1]), x.dtype),
                          compiler_params=pltpu.CompilerParams(vmem_limit_bytes=VMEM_LIMIT))(x)


def _colsum(x, *, name):
    def body(x_ref, o_ref):
        o_ref[...] = jnp.sum(x_ref[...], axis=0, keepdims=True)
    return pl.pallas_call(body, name=name, out_shape=jax.ShapeDtypeStruct((1, x.shape[1]), x.dtype))(x)


ANY = pl.BlockSpec(memory_space=pl.ANY)


class _ChipExchange:
    SCRATCH = [pltpu.SemaphoreType.DMA((3,)), pltpu.SemaphoreType.DMA((3,)), pltpu.SemaphoreType.DMA]

    @staticmethod
    def out_shape(src, scatter):
        return jax.ShapeDtypeStruct(((4,) + tuple(src.shape[1:])) if scatter else ((4, 2) + tuple(src.shape[1:])), src.dtype)

    def __init__(self, src_ref, out_ref, send_sems, recv_sems, local_sem, scatter):
        x, y, c = _place()
        myj = 2 * x + y
        chips = [(1 - x, y), (x, 1 - y), (1 - x, 1 - y)]

        def slot(j):
            return out_ref.at[j] if scatter else out_ref.at[j, c]

        def piece(j):
            return src_ref.at[j] if scatter else src_ref.at[c]

        self.mine = pltpu.make_async_copy(piece(myj), slot(myj), local_sem)
        self.sends = [pltpu.make_async_remote_copy(
            src_ref=piece(2 * cx + cy), dst_ref=slot(myj), send_sem=send_sems.at[k], recv_sem=recv_sems.at[k],
            device_id=(cx, cy, c), device_id_type=MESH) for k, (cx, cy) in enumerate(chips)]
        self.recvs = [pltpu.make_async_remote_copy(
            src_ref=slot(2 * cx + cy), dst_ref=slot(2 * cx + cy), send_sem=send_sems.at[k], recv_sem=recv_sems.at[k],
            device_id=(cx, cy, c), device_id_type=MESH) for k, (cx, cy) in enumerate(chips)]

    def start(self):
        self.mine.start()
        for cp in self.sends:
            cp.start()

    def wait(self):
        for cp in self.recvs:
            cp.wait_recv()
        for cp in self.sends:
            cp.wait_send()
        self.mine.wait()


def _sibling_fill(buf, *, axis, name):
    def half(ref, h):
        return ref.at[h] if axis == 0 else ref.at[:, h]

    def body(in_ref, out_ref, send_sem, recv_sem):
        x, y, c = _place()
        cp = pltpu.make_async_remote_copy(src_ref=half(out_ref, c), dst_ref=half(out_ref, c), send_sem=send_sem, recv_sem=recv_sem,
                                          device_id=(x, y, 1 - c), device_id_type=MESH)
        cp.start()
        pltpu.make_async_remote_copy(src_ref=half(out_ref, 1 - c), dst_ref=half(out_ref, 1 - c), send_sem=send_sem, recv_sem=recv_sem,
                                     device_id=(x, y, 1 - c), device_id_type=MESH).wait_recv()
        cp.wait_send()

    return pl.pallas_call(
        body, name=name, out_shape=jax.ShapeDtypeStruct(buf.shape, buf.dtype), in_specs=[ANY], out_specs=ANY,
        input_output_aliases={0: 0}, scratch_shapes=[pltpu.SemaphoreType.DMA, pltpu.SemaphoreType.DMA],
    )(buf)


def _sibling_swap_half(g, *, name):
    def body(g_ref, out_ref, send_sem, recv_sem):
        x, y, c = _place()
        cp = pltpu.make_async_remote_copy(src_ref=g_ref.at[:, 1 - c], dst_ref=out_ref, send_sem=send_sem, recv_sem=recv_sem,
                                          device_id=(x, y, 1 - c), device_id_type=MESH)
        cp.start()
        cp.wait()

    return pl.pallas_call(
        body, name=name, out_shape=jax.ShapeDtypeStruct((g.shape[0],) + g.shape[2:], g.dtype), in_specs=[ANY], out_specs=ANY,
        scratch_shapes=[pltpu.SemaphoreType.DMA, pltpu.SemaphoreType.DMA],
    )(g)


def _add_my_half(g, b, cidx, *, name, tr=256):
    n, _, R, C = g.shape
    tr = math.gcd(tr, R)

    def body(c_ref, g_ref, b_ref, o_ref):
        o_ref[...] = (g_ref[...] + b_ref[...]).astype(o_ref.dtype)

    return pl.pallas_call(
        body, name=name, out_shape=jax.ShapeDtypeStruct((n, R, C), BF16),
        grid_spec=pltpu.PrefetchScalarGridSpec(
            num_scalar_prefetch=1, grid=(n, R // tr),
            in_specs=[pl.BlockSpec((None, None, tr, C), lambda j, i, c: (j, c[0], i, 0)),
                      pl.BlockSpec((None, tr, C), lambda j, i, c: (j, i, 0))],
            out_specs=pl.BlockSpec((None, tr, C), lambda j, i, c: (j, i, 0))),
        compiler_params=_cp(("parallel", "parallel")),
    )(cidx, g, b)


def _sum4_into_half(q, cidx, *, name, tr=256):
    _, R, C = q.shape
    tr = math.gcd(tr, R)

    def body(c_ref, q_ref, o_ref):
        o_ref[...] = ((q_ref[0].astype(F32) + q_ref[1].astype(F32)) + q_ref[2].astype(F32)) + q_ref[3].astype(F32)

    return pl.pallas_call(
        body, name=name, out_shape=jax.ShapeDtypeStruct((2, R, C), F32),
        grid_spec=pltpu.PrefetchScalarGridSpec(
            num_scalar_prefetch=1, grid=(R // tr,),
            in_specs=[pl.BlockSpec((4, tr, C), lambda i, c: (0, i, 0))],
            out_specs=pl.BlockSpec((None, tr, C), lambda i, c: (c[0], i, 0))),
        compiler_params=_cp(("parallel",)),
    )(cidx, q)


def _pack_rows(parts, width=1024):
    rows, spans, r0 = [], [], 0
    for p in parts:
        n = p.size
        nr = 8 * (-(-n // (8 * width)))
        flat = p.reshape(-1)
        if nr * width != n:
            flat = jnp.pad(flat, (0, nr * width - n))
        rows.append(flat.reshape(nr, width))
        spans.append((r0, nr, n, p.shape))
        r0 += nr
    return jnp.concatenate(rows, axis=0), spans


def _unpack_rows(buf, spans):
    return [buf[r0:r0 + nr].reshape(-1)[:n].reshape(shape) for (r0, nr, n, shape) in spans]


def kernel(x, c, ada_w, ada_b, mix_norm_g, mlp_norm_g, mlp_w1, mlp_w2, s5_a_re, s5_a_im, s5_log_dt, s5_b_re, s5_b_im, s5_c_re, s5_c_im, s5_d, s5_w_glu, kv_ada_w, kv_ada_b, kv_norm_g, w_kv, k_norm_g, sb_w_q, q_norm_g, sb_w_o, loss_target, m_ada_w, m_ada_b, m_mix_norm_g, m_mlp_norm_g, m_mlp_w1, m_mlp_w2, m_s5_a_re, m_s5_a_im, m_s5_log_dt, m_s5_b_re, m_s5_b_im, m_s5_c_re, m_s5_c_im, m_s5_d, m_s5_w_glu, m_kv_ada_w, m_kv_ada_b, m_kv_norm_g, m_w_kv, m_k_norm_g, m_sb_w_q, m_q_norm_g, m_sb_w_o, v_ada_w, v_ada_b, v_mix_norm_g, v_mlp_norm_g, v_mlp_w1, v_mlp_w2, v_s5_a_re, v_s5_a_im, v_s5_log_dt, v_s5_b_re, v_s5_b_im, v_s5_c_re, v_s5_c_im, v_s5_d, v_s5_w_glu, v_kv_ada_w, v_kv_ada_b, v_kv_norm_g, v_w_kv, v_k_norm_g, v_sb_w_q, v_q_norm_g, v_sb_w_o):
    E, S, D = x.shape
    T = E * S
    FF = 4 * D
    NB = 8 * E
    px, py, pc = _place()
    chip = 2 * px + py
    dev = 4 * px + 2 * py + pc
    cidx = jnp.reshape(pc, (1,)).astype(jnp.int32)
    x0 = x.reshape(T, D)
    tgt = loss_target.reshape(T, D)

    c_all = _all_gather8(c.reshape(-1, 128), name="ag_c").reshape(NB, D)
    sc_all = (c_all * _sigmoid(c_all)).astype(BF16)
    wa = ada_w.shape[2]
    wk = kv_ada_w.shape[1]
    m_sh = jnp.concatenate([_mm(sc_all, ada_w[0], "nn", name="ada0", tn=256),
                            _mm(sc_all, ada_w[1], "nn", name="ada1", tn=256),
                            _mm(sc_all, kv_ada_w, "nn", name="ada_kv", tn=256)], axis=1)
    m_all = _all_gather8(m_sh, name="ag_m").reshape(4, 2, NB, 2 * wa + wk)[:, 0]
    mods = []
    for l in range(2):
        full = jnp.transpose(m_all[:, :, l * wa:(l + 1) * wa], (1, 0, 2)).reshape(NB, 6 * D) + ada_b[l]
        mine = lax.dynamic_slice_in_dim(full, E * dev, E, axis=0)
        mods.append([mine[:, i * D:(i + 1) * D].reshape(E, 1, D) for i in range(6)])
    full = jnp.transpose(m_all[:, :, 2 * wa:], (1, 0, 2)).reshape(NB, 2 * D) + kv_ada_b
    mine = lax.dynamic_slice_in_dim(full, E * dev, E, axis=0)
    kv_sh, kv_sc = [mine[:, i * D:(i + 1) * D].reshape(E, 1, D) for i in range(2)]

    wpack_a = jnp.concatenate([mlp_w1[0], mlp_w2[0], jnp.concatenate([s5_w_glu[0], w_kv], axis=1), sb_w_q[0]], axis=0).astype(BF16)
    wpack_b = jnp.concatenate([mlp_w1[1], mlp_w2[1], sb_w_o[0]], axis=0).astype(BF16)
    RA, RB = wpack_a.shape[0], wpack_b.shape[0]
    RW = RA + RB

    tm = min(1024, S)
    tkw = 2048

    def mlp_fwd(xa, l, mod):
        sh_m, sc_m, g_m = mod[3], mod[4], mod[5]
        h = _norm_mod_fwd(xa, mlp_norm_g[l:l + 1], sh_m, sc_m, n_ex=E, out_dtype=BF16, name=f"mlp_norm{l}")
        r = _mm(h, W1[l], "nn", name=f"mlp_up{l}", out_dtypes=(BF16,), tm=tm,
                epilogue=lambda acc: (jnp.square(jnp.maximum(acc, 0.0)),))
        xb, ff = _mm(r, W2[l], "nn", name=f"mlp_down{l}", out_dtypes=(F32, F32), tm=tm,
                     extras=[_mn_extra(xa), _vec_extra(g_m, S)],
                     epilogue=lambda acc, xat, gt: (xat + gt * acc, acc))
        return xb, (h, r, ff)

    def mlp_bwd(dxb, xa, l, mod, saved):
        sc_m, g_m = mod[4], mod[5]
        h, r, ff = saved
        (dff,), (dgm,) = _rowwise(lambda d, f, g: ([g * d], [_csum(d * f)]), [(dxb, D, 0), (ff, D, 0)], [g_m], [],
                                  [(D, BF16)], [D], n_ex=E, name=f"mlp_gate_bwd{l}")
        da = _mm(dff, W2[l], "nt", name=f"mlp_down_dx{l}", out_dtypes=(BF16,), tm=tm, extras=[_mn_extra(r)],
                 epilogue=lambda acc, rt: (acc * (2.0 * jnp.sqrt(rt.astype(F32))),))
        dW2 = _mm(r, dff, "tn", name=f"mlp_down_dw{l}", tk=tkw)
        dh = _mm(da, W1[l], "nt", name=f"mlp_up_dx{l}", tm=tm)
        dW1 = _mm(h, da, "tn", name=f"mlp_up_dw{l}", tk=tkw)
        (dxa,), (dsh, dsc, dg) = _norm_mod_bwd(xa, dh, dxb, mlp_norm_g[l:l + 1], sc_m, n_ex=E, name=f"mlp_norm_bwd{l}")
        return dxa, dW1, dW2, (dsh, dsc, dgm), dg

    ab_re, ab_im, bb_re, bb_im = _s5_disc(s5_a_re[0], s5_a_im[0], s5_log_dt[0], s5_b_re[0], s5_b_im[0])
    cf, cr = _s5_consts(ab_re, ab_im)
    Wb, Wc = _s5_blockdiag(bb_re, bb_im, s5_c_re[0], s5_c_im[0])
    ng = D // U_LANES
    nd = s5_d.size // 128
    d_full = _all_gather8(jnp.pad(s5_d.reshape(nd, 128), ((0, 8 - nd), (0, 0))), name="ag_d")
    d_full = d_full.reshape(4, 2, 8, 128)[:, 0, :nd].reshape(1, D)

    mod0, mod1 = mods
    h0 = _norm_mod_fwd(x0, mix_norm_g[0:1], mod0[0], mod0[1], n_ex=E, out_dtype=F32, name="mix_norm0")
    y, gy, wfull_a = _s5_fwd(h0, Wb, Wc, cf, d_full, wpack_a.reshape(2, RA // 2, D), n_ex=E, name="s5_fwd")
    wfull_a = _sibling_fill(wfull_a, axis=1, name="wgather_a_d2d").reshape(4, RA, D)

    def cols(wfull, r0, nr, c0, nc):
        return jnp.transpose(wfull[:, r0:r0 + nr, c0:c0 + nc], (1, 0, 2)).reshape(nr, 4 * nc)

    def rws(wfull, r0, nr):
        return wfull[:, r0:r0 + nr, :].reshape(4 * nr, D)

    W1 = [cols(wfull_a, 0, D, 0, D), None]
    W2 = [rws(wfull_a, D, D), None]
    Wglu = cols(wfull_a, 2 * D, D, 0, D // 2)
    Wkv = cols(wfull_a, 2 * D, D, D // 2, D // 2)
    Wq = rws(wfull_a, 3 * D, D // 4)
    vg = _mm(gy, Wglu, "nn", name="glu_up", tm=tm)
    (x1,), _ = _rowwise(lambda v, g, xt, ga: ([xt + ga * (v * _sigmoid(g))], []),
                        [(vg, D, 0), (vg, D, 1), (x0, D, 0)], [mod0[2]], [], [(D, F32)], [], n_ex=E, name="glu_gate")
    x2, saved_mlp0 = mlp_fwd(x1, 0, mod0)

    hkv = _norm_mod_fwd(x2, kv_norm_g.reshape(1, D), kv_sh, kv_sc, n_ex=E, out_dtype=BF16, name="kv_norm")
    kvf = _mm(hkv, Wkv, "nn", name="kv_proj", tm=tm)
    h1 = _norm_mod_fwd(x2, mix_norm_g[1:2], mod1[0], mod1[1], n_ex=E, out_dtype=BF16, name="mix_norm1")
    qf = _mm(h1, Wq, "nn", name="q_proj", tm=tm)
    qg2 = jnp.tile(q_norm_g.reshape(1, HEAD_DIM), (1, 2))
    kg2 = jnp.tile(k_norm_g.reshape(1, HEAD_DIM), (1, 2))
    o, lf_tot, wfull_b = _attn_fwd(qf, kvf, qg2, kg2, wpack_b.reshape(2, RB // 2, D), n_ex=E, name="attn_fwd")
    wfull_b = _sibling_fill(wfull_b, axis=1, name="wgather_b_d2d").reshape(4, RB, D)
    W1[1] = cols(wfull_b, 0, D, 0, D)
    W2[1] = rws(wfull_b, D, D)
    Wo = rws(wfull_b, 2 * D, D // 4)
    x3, mix1 = _mm(o, Wo, "nn", name="o_proj", out_dtypes=(F32, F32), tm=tm,
                   extras=[_mn_extra(x2), _vec_extra(mod1[2], S)],
                   epilogue=lambda acc, xat, gt: (xat + gt * acc, acc))
    x4, saved_mlp1 = mlp_fwd(x3, 1, mod1)

    (dx4,), (lsum,) = _rowwise(lambda xt, tt: ([(xt - tt) * (1.0 / D)], [_csum(jnp.square(xt - tt)) * (0.5 / D)]),
                               [(x4, D, 0), (tgt, D, 0)], [], [], [(D, F32)], [D], n_ex=E, name="loss")
    loss = lax.psum(jnp.sum(lsum), ("x", "y", "c"))

    dx3, dW1_1, dW2_1, (dsh_m1, dsc_m1, dgm1), dg_mlp1 = mlp_bwd(dx4, x3, 1, mod1, saved_mlp1)
    (dmix1,), (dga1,) = _rowwise(lambda d, f, g: ([g * d], [_csum(d * f)]), [(dx3, D, 0), (mix1, D, 0)], [mod1[2]], [],
                                 [(D, BF16)], [D], n_ex=E, name="attn_gate_bwd")
    do = _mm(dmix1, Wo, "nt", name="o_proj_dx", tm=tm)
    dWo = _mm(o, dmix1, "tn", name="o_proj_dw", tk=tkw)
    dq, dk, dv, dqg, dkg = _attn_bwd(qf, kvf, lf_tot, do, qg2, kg2, n_ex=E, name="attn_bwd")
    dh1 = _mm(dq, Wq, "nt", name="q_proj_dx", tm=tm)
    dWq = _mm(h1, dq, "tn", name="q_proj_dw", tk=tkw)
    (dx2,), (dsh_a1, dsc_a1, dg_mix1) = _norm_mod_bwd(x2, dh1, dx3, mix_norm_g[1:2], mod1[1], n_ex=E, name="mix_norm_bwd1")
    dkv = jnp.concatenate([dk, dv], axis=1)
    dhkv = _mm(dkv, Wkv, "nt", name="kv_proj_dx", tm=tm)
    dWkv = _mm(hkv, dkv, "tn", name="kv_proj_dw", tk=tkw)
    (dx2,), (dkv_sh, dkv_sc, dg_kv) = _norm_mod_bwd(x2, dhkv, dx2, kv_norm_g.reshape(1, D), kv_sc, n_ex=E, name="kv_norm_bwd")

    dx1, dW1_0, dW2_0, (dsh_m0, dsc_m0, dgm0), dg_mlp0 = mlp_bwd(dx2, x1, 0, mod0, saved_mlp0)

    def glu_bwd(v, g, d, ga):
        sg = _sigmoid(g)
        dm = ga * d
        return [jnp.concatenate([dm * sg, dm * v * sg * (1.0 - sg)], axis=1)], [_csum(d * (v * sg))]
    (dvg,), (dga0,) = _rowwise(glu_bwd, [(vg, D, 0), (vg, D, 1), (dx1, D, 0)], [mod0[2]], [], [(2 * D, BF16)], [D],
                               n_ex=E, name="glu_gate_bwd")
    dgy = _mm(dvg, Wglu, "nt", name="glu_up_dx", tm=tm)
    dWglu = _mm(gy, dvg, "tn", name="glu_up_dw", tk=tkw)

    def csh(g, nc):
        return jnp.transpose(g.reshape(g.shape[0], 4, nc), (1, 0, 2))

    gparts = [csh(dW1_0, D), csh(dW1_1, D), dW2_0.reshape(4, D, D), dW2_1.reshape(4, D, D),
              jnp.concatenate([csh(dWglu, D // 2), csh(dWkv, D // 2)], axis=2), dWq.reshape(4, D // 4, D), dWo.reshape(4, D // 4, D)]
    gpack = jnp.concatenate(gparts, axis=1).reshape(4, 2, RW // 2, D)
    theirs = _sibling_swap_half(gpack, name="gscatter_d2d")
    chip_sum = _add_my_half(gpack, theirs, cidx, name="gscatter_add")
    dh0, dWb, dWc, dab, dd, from_chips = _s5_bwd(h0, y, dgy, Wb, Wc, cf, cr, d_full, chip_sum, n_ex=E, name="s5_bwd")
    ghalf = _sum4_into_half(from_chips, cidx, name="gscatter_sum")
    gsh = _sibling_fill(ghalf, axis=0, name="gscatter_fill").reshape(RW, D)
    (gx,), (dsh_a0, dsc_a0, dg_mix0) = _norm_mod_bwd(x0, dh0, dx1, mix_norm_g[0:1], mod0[1], n_ex=E, name="mix_norm_bwd0")
    grad_x = gx.reshape(E, S, D)

    dm_mine = jnp.concatenate([t.reshape(E, D) for t in
                               (dsh_a0, dsc_a0, dga0, dsh_m0, dsc_m0, dgm0, dsh_a1, dsc_a1, dga1, dsh_m1, dsc_m1, dgm1, dkv_sh, dkv_sc)], axis=1)
    dm_all = _all_gather8(dm_mine.reshape(8, -1), name="ag_dm").reshape(NB, 14 * D)
    sc_f32 = c_all * _sigmoid(c_all)
    g_ada_w = jnp.stack([_mm(sc_f32, lax.dynamic_slice_in_dim(dm_all, l * 6 * D + chip * wa, wa, axis=1), "tn", name=f"ada_dw{l}", tn=256)
                         for l in range(2)])
    g_kv_ada_w = _mm(sc_f32, lax.dynamic_slice_in_dim(dm_all, 12 * D + chip * wk, wk, axis=1), "tn", name="ada_kv_dw", tn=256)
    db_all = _colsum(dm_all, name="ada_db")
    g_ada_b = db_all[0, :12 * D].reshape(2, 6 * D)
    g_kv_ada_b = db_all[0, 12 * D:]

    dWb_re, dWb_im, dC_re, dC_im = _s5_unblock(dWb, dWc)
    small_parts = [dg_mix0.sum(0), dg_mix1.sum(0), dg_mlp0.sum(0), dg_mlp1.sum(0), dg_kv.sum(0),
                   dqg.sum((0, 1, 2)).reshape(2, HEAD_DIM).sum(0), dkg.sum((0, 1, 2)).reshape(2, HEAD_DIM).sum(0),
                   dd[:, 0, :], dab[:, 0, :], dab[:, 1, :], dWb_re, dWb_im, dC_re, dC_im]
    spack, spans = _pack_rows(small_parts)
    ssum = _sum_blocks(_all_gather8(spack, name="ag_small"), 8, name="sum_small")
    (g_mix0, g_mix1, g_mlp0, g_mlp1, g_kvn, g_qn, g_kn, g_d, g_abr, g_abi, g_bbr, g_bbi, g_cre, g_cim) = _unpack_rows(ssum, spans)
    _, disc_vjp = jax.vjp(_s5_disc, s5_a_re[0], s5_a_im[0], s5_log_dt[0], s5_b_re[0], s5_b_im[0])
    g_are, g_aim, g_ldt, g_bre, g_bim = disc_vjp((g_abr.reshape(ab_re.shape), g_abi.reshape(ab_im.shape), g_bbr, g_bbi))
    g_s5d = lax.dynamic_slice_in_dim(g_d.reshape(1, D), chip * s5_d.shape[1], s5_d.shape[1], axis=1)

    def upd_big(w, m, v, roff, cb, name):
        shape = w.shape
        W = shape[-1]
        d_, m_, v_, g_ = _adamw2d(w.reshape(-1, W), gsh, m.reshape(-1, W), v.reshape(-1, W), name=name, g_roff=roff, g_cb=cb)
        return [t.reshape(shape) for t in (g_, d_, m_, v_)]

    def upd_own(w, g, m, v, name):
        shape = w.shape
        W = shape[-1]
        d_, m_, v_, g_ = _adamw2d(w.reshape(-1, W), g.reshape(-1, W), m.reshape(-1, W), v.reshape(-1, W), name=name)
        return [t.reshape(shape) for t in (g_, d_, m_, v_)]

    res = {}
    res["ada_w"] = upd_own(ada_w, g_ada_w, m_ada_w, v_ada_w, "adam_ada_w")
    res["kv_ada_w"] = upd_own(kv_ada_w, g_kv_ada_w, m_kv_ada_w, v_kv_ada_w, "adam_kv_ada_w")
    res["mlp_w1"] = upd_big(mlp_w1, m_mlp_w1, v_mlp_w1, 0, 0, "adam_w1")
    res["mlp_w2"] = upd_big(mlp_w2, m_mlp_w2, v_mlp_w2, 2 * D, 0, "adam_w2")
    res["s5_w_glu"] = upd_big(s5_w_glu, m_s5_w_glu, v_s5_w_glu, 4 * D, 0, "adam_glu")
    res["w_kv"] = upd_big(w_kv, m_w_kv, v_w_kv, 4 * D, 1, "adam_wkv")
    res["sb_w_q"] = upd_big(sb_w_q, m_sb_w_q, v_sb_w_q, 5 * D, 0, "adam_wq")
    res["sb_w_o"] = upd_big(sb_w_o, m_sb_w_o, v_sb_w_o, 5 * D + D // 4, 0, "adam_wo")

    small = {
        "ada_b": (ada_b, g_ada_b, m_ada_b, v_ada_b),
        "mix_norm_g": (mix_norm_g, jnp.stack([g_mix0, g_mix1]), m_mix_norm_g, v_mix_norm_g),
        "mlp_norm_g": (mlp_norm_g, jnp.stack([g_mlp0, g_mlp1]), m_mlp_norm_g, v_mlp_norm_g),
        "s5_a_re": (s5_a_re, g_are[None], m_s5_a_re, v_s5_a_re),
        "s5_a_im": (s5_a_im, g_aim[None], m_s5_a_im, v_s5_a_im),
        "s5_log_dt": (s5_log_dt, g_ldt[None], m_s5_log_dt, v_s5_log_dt),
        "s5_b_re": (s5_b_re, g_bre[None], m_s5_b_re, v_s5_b_re),
        "s5_b_im": (s5_b_im, g_bim[None], m_s5_b_im, v_s5_b_im),
        "s5_c_re": (s5_c_re, g_cre[None], m_s5_c_re, v_s5_c_re),
        "s5_c_im": (s5_c_im, g_cim[None], m_s5_c_im, v_s5_c_im),
        "s5_d": (s5_d, g_s5d, m_s5_d, v_s5_d),
        "kv_ada_b": (kv_ada_b, g_kv_ada_b, m_kv_ada_b, v_kv_ada_b),
        "kv_norm_g": (kv_norm_g, g_kvn, m_kv_norm_g, v_kv_norm_g),
        "k_norm_g": (k_norm_g, g_kn, m_k_norm_g, v_k_norm_g),
        "q_norm_g": (q_norm_g, g_qn.reshape(q_norm_g.shape), m_q_norm_g, v_q_norm_g),
    }
    names = list(small)
    packs = [_pack_rows([small[n][i].reshape(small[n][0].shape) for n in names]) for i in range(4)]
    sp = packs[0][1]
    d_, m_, v_, g_ = _adamw2d(packs[0][0], packs[1][0], packs[2][0], packs[3][0], name="adam_small")
    for n, gg, dd_, mm_, vv_ in zip(names, _unpack_rows(g_, sp), _unpack_rows(d_, sp), _unpack_rows(m_, sp), _unpack_rows(v_, sp)):
        res[n] = [gg, dd_, mm_, vv_]

    order = ["ada_w", "ada_b", "mix_norm_g", "mlp_norm_g", "mlp_w1", "mlp_w2", "s5_a_re", "s5_a_im", "s5_log_dt", "s5_b_re", "s5_b_im",
             "s5_c_re", "s5_c_im", "s5_d", "s5_w_glu", "kv_ada_w", "kv_ada_b", "kv_norm_g", "w_kv", "k_norm_g", "sb_w_q", "q_norm_g", "sb_w_o"]
    return (loss, grad_x, *[res[n][0] for n in order], *[res[n][1] for n in order], *[res[n][2] for n in order], *[res[n][3] for n in order])
```

```python
import functools
import math

import jax
import jax.numpy as jnp
from jax import lax
from jax.experimental import pallas as pl
from jax.experimental.pallas import tpu as pltpu

F32 = jnp.float32
BF16 = jnp.bfloat16
EPS = 1e-6
HEAD_DIM = 64
S5_GROUP = 16
S5_STATE = 64
GROUPS_PER_STEP = 8
U_LANES = GROUPS_PER_STEP * S5_GROUP
ST_LANES = GROUPS_PER_STEP * S5_STATE
SCAN_LANES = 256
SCAN_UNROLL = 4
VMEM_LIMIT = 56 * 1024 * 1024
ADAM_LR, ADAM_B1, ADAM_B2, ADAM_EPS, ADAM_WD, ADAM_STEP = 0.001, 0.9, 0.999, 1e-08, 0.01, 10
MESH = pl.DeviceIdType.MESH


def _cp(sem):
    return pltpu.CompilerParams(dimension_semantics=sem, vmem_limit_bytes=VMEM_LIMIT)


class _Sharded:
    def __init__(self, buf, kind, roff, nr, c0, nc):
        self.buf, self.kind, self.roff, self.nr, self.c0, self.nc = buf, kind, roff, nr, c0, nc
        self.shape = (nr, 4 * nc) if kind == "cols" else (4 * nr, nc)

    def operand(self, dims, tn, tk):
        roff, nr, c0, nc = self.roff, self.nr, self.c0, self.nc
        if self.kind == "cols" and dims == "nn":
            tk = min(tk, nr)
            assert roff % tk == 0
            return nc, tk, (None, tk, nc), lambda i, j, k: (j, roff // tk + k, c0 // nc)
        if self.kind == "cols":
            tn = min(tn, nr)
            assert roff % tn == 0
            return tn, nc, (None, tn, nc), lambda i, j, k: (k, roff // tn + j, c0 // nc)
        if dims == "nn":
            tn = min(tn, nc)
            assert roff % nr == 0 and c0 % tn == 0
            return tn, nr, (None, nr, tn), lambda i, j, k: (k, roff // nr, c0 // tn + j)
        tk = min(tk, nc)
        assert roff % nr == 0 and c0 % tk == 0
        return nr, tk, (None, nr, tk), lambda i, j, k: (j, roff // nr, c0 // tk + k)

    def result(self, tm, tn):
        roff, nr, c0, nc = self.roff, self.nr, self.c0, self.nc
        if self.kind == "cols":
            tm = min(tm, nr)
            assert roff % tm == 0
            return tm, nc, (None, tm, nc), lambda i, j, k: (j, roff // tm + i, c0 // nc)
        tm, tn = min(tm, nr), min(tn, nc)
        assert roff % tm == 0 and c0 % tn == 0
        per = nr // tm
        return tm, tn, (None, tm, tn), lambda i, j, k: (i // per, roff // tm + i % per, c0 // tn + j)


def _mm(a, b, dims, *, name, out_dtypes=(F32,), epilogue=None, extras=(), tm=512, tn=1024, tk=1024, into=None):
    bshape = b.shape
    if dims == "nn":
        (M, K), (_, N) = a.shape, bshape
    elif dims == "nt":
        (M, K), (N, _) = a.shape, bshape
    else:
        (K, M), (_, N) = a.shape, bshape
    tm, tn, tk = min(tm, M), min(tn, N), min(tk, K)
    b_arr = b
    if into is not None:
        assert (M, N) == into.shape and len(out_dtypes) == 1 and not isinstance(b, _Sharded)
        tm, tn, o_blk, o_map = into.result(tm, tn)
        out_specs, out_shape = [pl.BlockSpec(o_blk, o_map)], [jax.ShapeDtypeStruct(into.buf.shape, into.buf.dtype)]
    if isinstance(b, _Sharded):
        tn, tk, b_blk, b_map = b.operand(dims, tn, tk)
        b_spec, b_arr = pl.BlockSpec(b_blk, b_map), b.buf
    else:
        b_spec = pl.BlockSpec((tn, tk), lambda i, j, k: (j, k)) if dims == "nt" else pl.BlockSpec((tk, tn), lambda i, j, k: (k, j))
    if into is None:
        out_specs = [pl.BlockSpec((tm, tn), lambda i, j, k: (i, j)) for _ in out_dtypes]
        out_shape = [jax.ShapeDtypeStruct((M, N), d) for d in out_dtypes]
    assert M % tm == 0 and N % tn == 0 and K % tk == 0, (M, N, K, tm, tn, tk)
    nk = K // tk
    extras = [e(tm, tn) for e in extras]
    a_spec = pl.BlockSpec((tk, tm), lambda i, j, k: (k, i)) if dims == "tn" else pl.BlockSpec((tm, tk), lambda i, j, k: (i, k))
    contract = {"nn": ((1,), (0,)), "nt": ((1,), (1,)), "tn": ((0,), (0,))}[dims]
    n_ex, n_out = len(extras), len(out_dtypes)
    chain = [into.buf] if into is not None and not isinstance(into.buf, jax.ShapeDtypeStruct) else []
    n_in = n_ex + len(chain)

    def finish(r, ex, outs):
        res = epilogue(r, *[e[...] for e in ex]) if epilogue is not None else (r,)
        for o, v in zip(outs, res):
            o[...] = v.astype(o.dtype)

    def product(a_ref, b_ref):
        return lax.dot_general(a_ref[...].astype(BF16), b_ref[...].astype(BF16), (contract, ((), ())), preferred_element_type=F32)

    def body_one(a_ref, b_ref, *rest):
        finish(product(a_ref, b_ref), rest[:n_ex], rest[n_in:])

    def body_acc(a_ref, b_ref, *rest):
        ex, outs, acc = rest[:n_ex], rest[n_in:n_in + n_out], rest[-1]
        k = pl.program_id(2)

        @pl.when(k == 0)
        def _():
            acc[...] = product(a_ref, b_ref)

        @pl.when(k > 0)
        def _():
            acc[...] += product(a_ref, b_ref)

        @pl.when(k == nk - 1)
        def _():
            finish(acc[...], ex, outs)

    out = pl.pallas_call(
        body_one if nk == 1 else body_acc, name=name, grid=(M // tm, N // tn, nk),
        in_specs=[a_spec, b_spec] + [pl.BlockSpec(blk, im) for (_, blk, im) in extras] + [ANY for _ in chain],
        out_specs=out_specs, out_shape=out_shape,
        input_output_aliases={2 + n_ex: 0} if chain else {},
        scratch_shapes=[] if nk == 1 else [pltpu.VMEM((tm, tn), F32)],
        compiler_params=_cp(("parallel", "parallel", "arbitrary")),
    )(a, b_arr, *[e[0] for e in extras], *chain)
    return out if n_out > 1 else out[0]


def _mn_extra(arr):
    return lambda tm, tn: (arr, (tm, tn), lambda i, j, k: (i, j))


def _vec_extra(vec, S):
    return lambda tm, tn: (vec, (None, 1, tn), lambda i, j, k: ((i * tm) // S, 0, j))


def _rowwise(fn, rows, vecs=(), consts=(), out_rows=(), out_sums=(), *, n_ex, name, tr=256):
    rows = [r if len(r) == 4 else (*r, 0) for r in rows]
    S = min(r[0].shape[0] for r in rows if r[3] == 0) // n_ex
    tr = math.gcd(tr, S)
    assert S % tr == 0
    nb = S // tr
    in_specs = []
    for (arr, w, cb, roff) in rows:
        assert roff % tr == 0
        in_specs.append(pl.BlockSpec((tr, w), functools.partial(lambda e, i, cb, ro: (e * nb + i + ro, cb), cb=cb, ro=roff // tr)))
    for v in vecs:
        in_specs.append(pl.BlockSpec((None, 1, v.shape[-1]), lambda e, i: (e, 0, 0)))
    for c in consts:
        in_specs.append(pl.BlockSpec((1, c.shape[-1]), lambda e, i: (0, 0)))
    n_in, n_or, n_os = len(in_specs), len(out_rows), len(out_sums)
    out_specs = [pl.BlockSpec((tr, w), lambda e, i: (e * nb + i, 0)) for (w, _) in out_rows]
    out_specs += [pl.BlockSpec((None, 1, w), lambda e, i: (e, 0, 0)) for w in out_sums]
    out_shape = [jax.ShapeDtypeStruct((n_ex * S, w), d) for (w, d) in out_rows]
    out_shape += [jax.ShapeDtypeStruct((n_ex, 1, w), F32) for w in out_sums]

    def body(*refs):
        ins, o_r, o_s = refs[:n_in], refs[n_in:n_in + n_or], refs[n_in + n_or:]
        ro, so = fn(*[r[...] for r in ins])
        for o, v in zip(o_r, ro):
            o[...] = v.astype(o.dtype)
        i = pl.program_id(1)
        for o, v in zip(o_s, so):
            @pl.when(i == 0)
            def _(o=o, v=v):
                o[...] = v

            @pl.when(i > 0)
            def _(o=o, v=v):
                o[...] += v

    outs = pl.pallas_call(
        body, name=name, grid=(n_ex, nb), in_specs=in_specs, out_specs=out_specs, out_shape=out_shape,
        compiler_params=_cp(("parallel", "arbitrary")),
    )(*[r[0] for r in rows], *vecs, *consts)
    return outs[:n_or], outs[n_or:]


def _csum(x):
    return jnp.sum(x, axis=0, keepdims=True)


def _norm_mod_fwd(x, g, sh, sc, *, n_ex, out_dtype, name):
    def fn(xt, sht, sct, gt):
        r = lax.rsqrt(jnp.mean(xt * xt, axis=-1, keepdims=True) + EPS)
        return [(xt * r * gt) * (1.0 + sct) + sht], []
    D = x.shape[1]
    return _rowwise(fn, [(x, D, 0)], [sh, sc], [g], [(D, out_dtype)], [], n_ex=n_ex, name=name)[0][0]


def _norm_mod_bwd(x, dh, dres, g, sc, *, n_ex, name):
    def fn(xt, dht, drt, sct, gt):
        dht = dht.astype(F32)
        r = lax.rsqrt(jnp.mean(xt * xt, axis=-1, keepdims=True) + EPS)
        n = xt * r
        y = n * gt
        dy = dht * (1.0 + sct)
        dn = dy * gt
        dx = r * (dn - n * jnp.mean(dn * n, axis=-1, keepdims=True))
        return [drt + dx], [_csum(dht), _csum(dht * y), _csum(dy * n)]
    D = x.shape[1]
    return _rowwise(fn, [(x, D, 0), (dh, D, 0), (dres, D, 0)], [sc], [g], [(D, F32)], [D, D, D], n_ex=n_ex, name=name)


def _sigmoid(x):
    return 1.0 / (1.0 + jnp.exp(-x))


def _gelu(y):
    return 0.5 * y * (1.0 + jnp.tanh(0.7978845608028654 * (y + 0.044715 * y * y * y)))


def _gelu_grad(y):
    t = jnp.tanh(0.7978845608028654 * (y + 0.044715 * y * y * y))
    return 0.5 * (1.0 + t) + 0.5 * y * (1.0 - t * t) * 0.7978845608028654 * (1.0 + 3 * 0.044715 * y * y)


def _adamw_fn(w, g, m, v):
    m2 = ADAM_B1 * m + (1.0 - ADAM_B1) * g
    v2 = ADAM_B2 * v + (1.0 - ADAM_B2) * (g * g)
    m_hat = m2 / (1.0 - ADAM_B1 ** ADAM_STEP)
    v_hat = v2 / (1.0 - ADAM_B2 ** ADAM_STEP)
    delta = -ADAM_LR * (m_hat / (jnp.sqrt(v_hat) + ADAM_EPS) + ADAM_WD * w)
    return delta, m2, v2


def _adamw2d(w, g, m, v, *, name, g_roff=0, g_cb=0):
    R, W = w.shape

    def fn(wt, gt, mt, vt):
        d, m2, v2 = _adamw_fn(wt, gt, mt, vt)
        return [d, m2, v2, gt], []
    return _rowwise(fn, [(w, W, 0), (g, W, g_cb, g_roff), (m, W, 0), (v, W, 0)], [], [],
                    [(W, F32)] * 4, [], n_ex=1, name=name, tr=256)[0]


def _scan_tiles(re_ref, im_ref, cf, lane0, n_chunks, reverse, extra=None):
    L = SCAN_LANES
    lanes = pl.ds(lane0, L)
    A = [cf[i, :, lanes] for i in range(8)]
    shifts = (7, 6, 4) if reverse else (1, 2, 4)
    edge = 0 if reverse else 7

    U = SCAN_UNROLL
    n_groups = n_chunks // U

    def body(c, carry):
        first = ((n_groups - 1 - c) if reverse else c) * U
        rows = pl.ds(pl.multiple_of(first * 8, 8 * U), 8 * U)
        big_r, big_i = re_ref[rows, lanes], im_ref[rows, lanes]
        tiles = []
        for u in range(U):
            xr, xi = big_r[8 * u:8 * u + 8, :], big_i[8 * u:8 * u + 8, :]
            for idx, sft in enumerate(shifts):
                ar, ai = A[2 * idx], A[2 * idx + 1]
                rr, ri = pltpu.roll(xr, sft, 0), pltpu.roll(xi, sft, 0)
                xr, xi = xr + ar * rr - ai * ri, xi + ar * ri + ai * rr
            tiles.append((xr, xi))
        pr, pi = A[6], A[7]
        for u in (range(U - 1, -1, -1) if reverse else range(U)):
            xr, xi = tiles[u]
            cr, ci = carry[0], carry[1]
            xr, xi = xr + pr * cr - pi * ci, xi + pr * ci + pi * cr
            tiles[u] = (xr, xi)
            nxt = (jnp.broadcast_to(xr[edge:edge + 1, :], (8, L)), jnp.broadcast_to(xi[edge:edge + 1, :], (8, L)))
            if extra is not None:
                nxt = nxt + extra(first + u, xr, xi, carry[2:])
            carry = nxt
        re_ref[rows, lanes] = jnp.concatenate([t[0] for t in tiles], axis=0)
        im_ref[rows, lanes] = jnp.concatenate([t[1] for t in tiles], axis=0)
        return carry

    assert n_chunks % U == 0
    z = jnp.zeros((8, L), F32)
    init = (z, z) if extra is None else (z, z, z, z)
    return lax.fori_loop(0, n_groups, body, init)


def _s5_consts(ab_re, ab_im):
    ng = ab_re.shape[0] // GROUPS_PER_STEP
    ar, ai = ab_re.reshape(ng, 1, ST_LANES), ab_im.reshape(ng, 1, ST_LANES)

    def cmul(xr, xi, yr, yi):
        return xr * yr - xi * yi, xr * yi + xi * yr

    def build(ar, ai, reverse):
        pw = [(ar, ai)]
        for _ in range(7):
            pw.append(cmul(*pw[-1], ar, ai))
        row = jnp.arange(8).reshape(1, 8, 1)
        tiles = []
        for k in (1, 2, 4):
            keep = (row <= 7 - k) if reverse else (row >= k)
            tiles += [jnp.where(keep, pw[k - 1][0], 0.0), jnp.where(keep, pw[k - 1][1], 0.0)]
        order = [7 - r for r in range(8)] if reverse else list(range(8))
        tiles += [jnp.concatenate([pw[o][0] for o in order], axis=1), jnp.concatenate([pw[o][1] for o in order], axis=1)]
        return jnp.stack([jnp.broadcast_to(t, (ng, 8, ST_LANES)) for t in tiles], axis=1)

    return build(ar, ai, False), build(ar, -ai, True)


def _s5_blockdiag(bb_re, bb_im, c_re, c_im):
    G = bb_re.shape[0]
    ng = G // GROUPS_PER_STEP
    eye = jnp.eye(GROUPS_PER_STEP, dtype=F32)

    def wb(bb):
        return jnp.einsum("bgph,gk->bghkp", bb.reshape(ng, GROUPS_PER_STEP, S5_STATE, S5_GROUP), eye).reshape(ng, U_LANES, ST_LANES)

    def wc(cc):
        return jnp.einsum("bghp,gk->bkpgh", cc.reshape(ng, GROUPS_PER_STEP, S5_GROUP, S5_STATE), eye).reshape(ng, ST_LANES, U_LANES)

    Wb = jnp.concatenate([wb(bb_re), wb(bb_im)], axis=2).astype(BF16)
    Wc = jnp.concatenate([wc(c_re), -wc(c_im)], axis=1).astype(BF16)
    return Wb, Wc


def _s5_unblock(dWb, dWc):
    ng = dWb.shape[0]
    eye = jnp.eye(GROUPS_PER_STEP, dtype=F32)

    def ub(w):
        return jnp.einsum("bghkp,gk->bgph", w.reshape(ng, GROUPS_PER_STEP, S5_GROUP, GROUPS_PER_STEP, S5_STATE), eye).reshape(-1, S5_STATE, S5_GROUP)

    def uc(w):
        return jnp.einsum("bkpgh,gk->bghp", w.reshape(ng, GROUPS_PER_STEP, S5_STATE, GROUPS_PER_STEP, S5_GROUP), eye).reshape(-1, S5_GROUP, S5_STATE)

    return ub(dWb[:, :, :ST_LANES]), ub(dWb[:, :, ST_LANES:]), uc(dWc[:, :ST_LANES, :]), -uc(dWc[:, ST_LANES:, :])


def _s5_disc(a_re, a_im, log_dt, b_re, b_im):
    dt = jnp.exp(log_dt)[:, None]
    mag = jnp.exp(a_re * dt)
    ab_re = mag * jnp.cos(a_im * dt)
    ab_im = mag * jnp.sin(a_im * dt)
    den = a_re * a_re + a_im * a_im
    nr, ni = ab_re - 1, ab_im
    f_re = (nr * a_re + ni * a_im) / den
    f_im = (ni * a_re - nr * a_im) / den
    bb_re = f_re[..., None] * b_re - f_im[..., None] * b_im
    bb_im = f_re[..., None] * b_im + f_im[..., None] * b_re
    return ab_re, ab_im, bb_re, bb_im


ROW_CHUNK = 512


def _s5_fwd(u, Wb, Wc, cf, d, xsrc, *, n_ex, name):
    T, D = u.shape
    S = T // n_ex
    ng = D // U_LANES
    rc = min(ROW_CHUNK, S)

    def body(u_ref, wb_ref, wc_ref, cf_ref, d_ref, xsrc_ref, y_ref, gy_ref, xout_ref, re_s, im_s, *sems):
        step = pl.program_id(0) * ng + pl.program_id(1)
        exch = _ChipExchange(xsrc_ref, xout_ref, *sems, scatter=False)

        @pl.when(step == 0)
        def _():
            exch.start()

        for r in range(S // rc):
            rows = pl.ds(r * rc, rc)
            bu = jnp.dot(u_ref[rows, :].astype(BF16), wb_ref[...], preferred_element_type=F32)
            re_s[rows, :] = bu[:, :ST_LANES]
            im_s[rows, :] = bu[:, ST_LANES:]
        for l0 in range(0, ST_LANES, SCAN_LANES):
            _scan_tiles(re_s, im_s, cf_ref, l0, S // 8, False)
        for r in range(S // rc):
            rows = pl.ds(r * rc, rc)
            st = jnp.concatenate([re_s[rows, :], im_s[rows, :]], axis=1).astype(BF16)
            y = jnp.dot(st, wc_ref[...], preferred_element_type=F32) + d_ref[...] * u_ref[rows, :]
            y_ref[rows, :] = y
            gy_ref[rows, :] = _gelu(y).astype(BF16)

        @pl.when(step == n_ex * ng - 1)
        def _():
            exch.wait()

    return pl.pallas_call(
        body, name=name, grid=(n_ex, ng),
        in_specs=[pl.BlockSpec((S, U_LANES), lambda e, g: (e, g)),
                  pl.BlockSpec((None, U_LANES, 2 * ST_LANES), lambda e, g: (g, 0, 0)),
                  pl.BlockSpec((None, 2 * ST_LANES, U_LANES), lambda e, g: (g, 0, 0)),
                  pl.BlockSpec((None, 8, 8, ST_LANES), lambda e, g: (g, 0, 0, 0)),
                  pl.BlockSpec((1, U_LANES), lambda e, g: (0, g)), ANY],
        out_specs=[pl.BlockSpec((S, U_LANES), lambda e, g: (e, g))] * 2 + [ANY],
        out_shape=[jax.ShapeDtypeStruct((T, D), F32), jax.ShapeDtypeStruct((T, D), BF16), _ChipExchange.out_shape(xsrc, False)],
        scratch_shapes=[pltpu.VMEM((S, ST_LANES), F32)] * 2 + _ChipExchange.SCRATCH,
        compiler_params=_cp(("arbitrary", "arbitrary")),
    )(u, Wb, Wc, cf, d, xsrc)


def _s5_bwd(u, y, dgy, Wb, Wc, cf, cr, d, xsrc, *, n_ex, name):
    T, D = u.shape
    S = T // n_ex
    ng = D // U_LANES
    rc = min(ROW_CHUNK, S)
    nch = S // 8

    def body(u_ref, y_ref, dgy_ref, wb_ref, wc_ref, cf_ref, cr_ref, d_ref, xsrc_ref,
             du_ref, dwb_ref, dwc_ref, dab_ref, dd_ref, xout_ref, re_s, im_s, gr_s, gi_s, dy_s, *sems):
        e = pl.program_id(1)
        step = pl.program_id(0) * n_ex + e
        exch = _ChipExchange(xsrc_ref, xout_ref, *sems, scatter=True)

        @pl.when(step == 0)
        def _():
            exch.start()

        @pl.when(e == 0)
        def _():
            dwb_ref[...] = jnp.zeros_like(dwb_ref)
            dwc_ref[...] = jnp.zeros_like(dwc_ref)
            dab_ref[...] = jnp.zeros_like(dab_ref)
            dd_ref[...] = jnp.zeros_like(dd_ref)

        dd = jnp.zeros((1, U_LANES), F32)
        for r in range(S // rc):
            rows = pl.ds(r * rc, rc)
            ut = u_ref[rows, :]
            bu = jnp.dot(ut.astype(BF16), wb_ref[...], preferred_element_type=F32)
            re_s[rows, :] = bu[:, :ST_LANES]
            im_s[rows, :] = bu[:, ST_LANES:]
            dy = dgy_ref[rows, :].astype(F32) * _gelu_grad(y_ref[rows, :])
            dy_s[rows, :] = dy
            dd = dd + _csum(dy * ut)
            go = lax.dot_general(dy.astype(BF16), wc_ref[...], (((1,), (1,)), ((), ())), preferred_element_type=F32)
            gr_s[rows, :] = go[:, :ST_LANES]
            gi_s[rows, :] = go[:, ST_LANES:]
        dd_ref[0:1, :] += dd
        row0 = lax.broadcasted_iota(jnp.int32, (8, SCAN_LANES), 0) == 0
        for l0 in range(0, ST_LANES, SCAN_LANES):
            lanes = pl.ds(l0, SCAN_LANES)
            _scan_tiles(re_s, im_s, cf_ref, l0, nch, False)

            def dab_part(cc, gr, gi, acc, lanes=lanes):
                rows = pl.ds(pl.multiple_of(cc * 8, 8), 8)
                prev = pl.ds(pl.multiple_of(jnp.maximum(cc - 1, 0) * 8, 8), 8)
                live = (cc > 0).astype(F32)
                sr = jnp.where(row0, pltpu.roll(re_s[prev, lanes], 1, 0) * live, pltpu.roll(re_s[rows, lanes], 1, 0))
                si = jnp.where(row0, pltpu.roll(im_s[prev, lanes], 1, 0) * live, pltpu.roll(im_s[rows, lanes], 1, 0))
                return (acc[0] + gr * sr + gi * si, acc[1] + gi * sr - gr * si)

            res = _scan_tiles(gr_s, gi_s, cr_ref, l0, nch, True, extra=dab_part)
            dab_ref[0:1, lanes] += _csum(res[2])
            dab_ref[1:2, lanes] += _csum(res[3])
        for r in range(S // rc):
            rows = pl.ds(r * rc, rc)
            st = jnp.concatenate([re_s[rows, :], im_s[rows, :]], axis=1).astype(BF16)
            g = jnp.concatenate([gr_s[rows, :], gi_s[rows, :]], axis=1).astype(BF16)
            dyb = dy_s[rows, :].astype(BF16)
            dwc_ref[...] += lax.dot_general(st, dyb, (((0,), (0,)), ((), ())), preferred_element_type=F32)
            dwb_ref[...] += lax.dot_general(u_ref[rows, :].astype(BF16), g, (((0,), (0,)), ((), ())), preferred_element_type=F32)
            du = lax.dot_general(g, wb_ref[...], (((1,), (1,)), ((), ())), preferred_element_type=F32)
            du_ref[rows, :] = du + d_ref[...] * dy_s[rows, :]

        @pl.when(step == ng * n_ex - 1)
        def _():
            exch.wait()

    return pl.pallas_call(
        body, name=name, grid=(ng, n_ex),
        in_specs=[pl.BlockSpec((S, U_LANES), lambda g, e: (e, g))] * 3 + [
            pl.BlockSpec((None, U_LANES, 2 * ST_LANES), lambda g, e: (g, 0, 0)),
            pl.BlockSpec((None, 2 * ST_LANES, U_LANES), lambda g, e: (g, 0, 0)),
            pl.BlockSpec((None, 8, 8, ST_LANES), lambda g, e: (g, 0, 0, 0)),
            pl.BlockSpec((None, 8, 8, ST_LANES), lambda g, e: (g, 0, 0, 0)),
            pl.BlockSpec((1, U_LANES), lambda g, e: (0, g)), ANY],
        out_specs=[pl.BlockSpec((S, U_LANES), lambda g, e: (e, g)),
                   pl.BlockSpec((None, U_LANES, 2 * ST_LANES), lambda g, e: (g, 0, 0)),
                   pl.BlockSpec((None, 2 * ST_LANES, U_LANES), lambda g, e: (g, 0, 0)),
                   pl.BlockSpec((None, 8, ST_LANES), lambda g, e: (g, 0, 0)),
                   pl.BlockSpec((None, 8, U_LANES), lambda g, e: (g, 0, 0)), ANY],
        out_shape=[jax.ShapeDtypeStruct((T, D), F32),
                   jax.ShapeDtypeStruct((ng, U_LANES, 2 * ST_LANES), F32),
                   jax.ShapeDtypeStruct((ng, 2 * ST_LANES, U_LANES), F32),
                   jax.ShapeDtypeStruct((ng, 8, ST_LANES), F32),
                   jax.ShapeDtypeStruct((ng, 8, U_LANES), F32), _ChipExchange.out_shape(xsrc, True)],
        scratch_shapes=[pltpu.VMEM((S, ST_LANES), F32)] * 4 + [pltpu.VMEM((S, U_LANES), F32)] + _ChipExchange.SCRATCH,
        compiler_params=_cp(("arbitrary", "arbitrary")),
    )(u, y, dgy, Wb, Wc, cf, cr, d, xsrc)


TQ = 256
KW = 512
SUB = 128


def _head_masks():
    lane = lax.broadcasted_iota(jnp.int32, (1, 2 * HEAD_DIM), 1)
    m0 = (lane < HEAD_DIM).astype(F32)
    return m0, 1.0 - m0


def _head_norm(x, g, m0, m1):
    sq = x * x
    r0 = lax.rsqrt(jnp.sum(sq * m0, axis=-1, keepdims=True) / HEAD_DIM + EPS)
    r1 = lax.rsqrt(jnp.sum(sq * m1, axis=-1, keepdims=True) / HEAD_DIM + EPS)
    r = m0 * r0 + m1 * r1
    return x * r, r


def _head_norm_bwd(dy, n, r, g, m0, m1):
    dn = dy * g
    p = dn * n
    mean = (m0 * jnp.sum(p * m0, axis=-1, keepdims=True) + m1 * jnp.sum(p * m1, axis=-1, keepdims=True)) / HEAD_DIM
    return r * (dn - n * mean), _csum(dy * n)


def _pair_matrix(kind):
    r = lax.broadcasted_iota(jnp.int32, (2 * SUB, 2 * SUB), 0)
    c = lax.broadcasted_iota(jnp.int32, (2 * SUB, 2 * SUB), 1)
    same = (r < SUB) == (c < SUB)
    rel = {"after": r > c, "upto": r <= c, "before": r < c}[kind]
    return jnp.logical_and(same, rel).astype(BF16)


def _block_sums(x, mat, carry, reverse, terms=2):
    hi = x.astype(BF16)
    lo = (x - hi.astype(F32)).astype(BF16) if terms == 2 else None
    npair = KW // (2 * SUB)
    parts = [None] * (2 * npair)
    for p in (range(npair - 1, -1, -1) if reverse else range(npair)):
        sl = slice(2 * SUB * p, 2 * SUB * (p + 1))
        loc = jnp.dot(hi[:, sl], mat, preferred_element_type=F32)
        if terms == 2:
            loc = loc + jnp.dot(lo[:, sl], mat, preferred_element_type=F32)
        for b in ((1, 0) if reverse else (0, 1)):
            k = 2 * p + b
            parts[k] = loc[:, SUB * b:SUB * (b + 1)] + carry
            carry = carry + jnp.sum(x[:, SUB * k:SUB * (k + 1)], axis=-1, keepdims=True)
    return jnp.concatenate(parts, axis=1), carry


def _sb_logits(qh, kT, mask):
    z = jnp.dot(qh, kT, preferred_element_type=F32)
    lp = jnp.minimum(z, 0.0) - jnp.log(1.0 + jnp.exp(-jnp.abs(z)))
    lf = lp - z
    if mask is not None:
        lf = jnp.where(mask, lf, 0.0)
    return lp, lf


def _causal_mask(row0, col0):
    r = row0 + lax.broadcasted_iota(jnp.int32, (TQ, KW), 0)
    c = col0 + lax.broadcasted_iota(jnp.int32, (TQ, KW), 1)
    return c < r


def _transposed_windows(x, ref):
    for w in range(x.shape[0] // KW):
        ref[w] = x[w * KW:(w + 1) * KW, :].T.astype(BF16)


def _attn_fwd(q, kv, qg, kg, xsrc, *, n_ex, name):
    T, D = q.shape
    S = T // n_ex
    nhp = D // (2 * HEAD_DIM)
    nq = S // TQ
    scale = 1.0 / math.sqrt(HEAD_DIM)

    def body(q_ref, k_ref, v_ref, qg_ref, kg_ref, xsrc_ref, o_ref, tot_ref, xout_ref, kT_s, qm_s, vm_s, *sems):
        step = pl.program_id(0) * nhp + pl.program_id(1)
        exch = _ChipExchange(xsrc_ref, xout_ref, *sems, scatter=False)

        @pl.when(step == 0)
        def _():
            exch.start()

        m0, m1 = _head_masks()
        qn, _ = _head_norm(q_ref[...], None, m0, m1)
        qn = qn * (qg_ref[...] * scale)
        kn, _ = _head_norm(k_ref[...], None, m0, m1)
        _transposed_windows(kn * kg_ref[...], kT_s)
        v = v_ref[...]
        for h, m in enumerate((m0, m1)):
            qm_s[h] = (qn * m).astype(BF16)
            vm_s[h] = (v * m).astype(BF16)
        u_after = _pair_matrix("after")

        def window(rows, win, st, mask):
            keys = pl.ds(pl.multiple_of(win * KW, KW), KW)
            lg = [_sb_logits(qm_s[h, rows, :], kT_s[win], mask) for h in range(2)]
            sums = [_block_sums(lg[h][1], u_after, st[2 * h], True) for h in range(2)]
            out = ()
            for h in range(2):
                w = jnp.exp(lg[h][0] + sums[h][0])
                if mask is not None:
                    w = jnp.where(mask, w, 0.0)
                out += (sums[h][1], st[2 * h + 1] + jnp.dot(w.astype(BF16), vm_s[h, keys, :], preferred_element_type=F32))
            return out

        def qtile(iq, _):
            rows = pl.ds(pl.multiple_of(iq * TQ, TQ), TQ)
            last = (iq * TQ) // KW
            mask = _causal_mask(iq * TQ, last * KW)
            z1, zq = jnp.zeros((TQ, 1), F32), jnp.zeros((TQ, 2 * HEAD_DIM), F32)
            st = window(rows, last, (z1, zq, z1, zq), mask)
            st = lax.fori_loop(0, last, lambda jj, st: window(rows, last - 1 - jj, st, None), st)
            o_ref[rows, :] = st[1] + st[3]
            tot_ref[rows, :] = st[0] * m0 + st[2] * m1
            return 0

        lax.fori_loop(0, nq, qtile, 0)

        @pl.when(step == n_ex * nhp - 1)
        def _():
            exch.wait()

    assert S % KW == 0 and KW % TQ == 0
    nwin = S // KW
    blk = (S, 2 * HEAD_DIM)
    return pl.pallas_call(
        body, name=name, grid=(n_ex, nhp),
        in_specs=[pl.BlockSpec(blk, lambda e, h: (e, h)), pl.BlockSpec(blk, lambda e, h: (e, h)),
                  pl.BlockSpec(blk, lambda e, h: (e, h + nhp)),
                  pl.BlockSpec((1, 2 * HEAD_DIM), lambda e, h: (0, 0)), pl.BlockSpec((1, 2 * HEAD_DIM), lambda e, h: (0, 0)), ANY],
        out_specs=[pl.BlockSpec(blk, lambda e, h: (e, h))] * 2 + [ANY],
        out_shape=[jax.ShapeDtypeStruct((T, D), F32)] * 2 + [_ChipExchange.out_shape(xsrc, False)],
        scratch_shapes=[pltpu.VMEM((nwin, 2 * HEAD_DIM, KW), BF16), pltpu.VMEM((2,) + blk, BF16), pltpu.VMEM((2,) + blk, BF16)]
        + _ChipExchange.SCRATCH,
        compiler_params=_cp(("arbitrary", "arbitrary")),
    )(q, kv, kv, qg, kg, xsrc)


def _attn_bwd(q, kv, tot, do, qg, kg, *, n_ex, name):
    T, D = q.shape
    S = T // n_ex
    nhp = D // (2 * HEAD_DIM)
    nq = S // TQ
    scale = 1.0 / math.sqrt(HEAD_DIM)

    def body(q_ref, k_ref, v_ref, tot_ref, do_ref, qg_ref, kg_ref, dq_ref, dk_ref, dv_ref, dqg_ref, dkg_ref,
             kT_s, vT_s, km_s, qm_s, dom_s, dqn_s, dkT_s, dvT_s):
        m0, m1 = _head_masks()
        qn, qr = _head_norm(q_ref[...], None, m0, m1)
        kn, kr = _head_norm(k_ref[...], None, m0, m1)
        qs = qn * (qg_ref[...] * scale)
        kk = kn * kg_ref[...]
        _transposed_windows(kk, kT_s)
        _transposed_windows(v_ref[...], vT_s)
        do = do_ref[...]
        for h, m in enumerate((m0, m1)):
            qm_s[h] = (qs * m).astype(BF16)
            km_s[h] = (kk * m).astype(BF16)
            dom_s[h] = (do * m).astype(BF16)
        dkT_s[...] = jnp.zeros_like(dkT_s)
        dvT_s[...] = jnp.zeros_like(dvT_s)
        u_upto, u_before = _pair_matrix("upto"), _pair_matrix("before")

        def both(inv, win, st, mask):
            keys = pl.ds(pl.multiple_of(win * KW, KW), KW)
            lg = [_sb_logits(inv[h][0], kT_s[win], mask) for h in range(2)]
            dw = [jnp.dot(inv[h][2], vT_s[win], preferred_element_type=F32) for h in range(2)]
            s_lf = [_block_sums(lg[h][1], u_upto, st[3 * h], False) for h in range(2)]
            ws, ews = [], []
            for h in range(2):
                w = jnp.exp(lg[h][0] + (inv[h][4] - s_lf[h][0]))
                if mask is not None:
                    w = jnp.where(mask, w, 0.0)
                ws.append(w)
                ews.append(dw[h] * w)
            s_e = [_block_sums(ews[h], u_before, st[3 * h + 1], False, terms=1) for h in range(2)]
            out, dk, dv = (), None, None
            for h in range(2):
                sig = jnp.exp(lg[h][0])
                dz = ews[h] * (1.0 - sig) - s_e[h][0] * sig
                if mask is not None:
                    dz = jnp.where(mask, dz, 0.0)
                dzb = dz.astype(BF16)
                out += (s_lf[h][1], s_e[h][1], st[3 * h + 2] + jnp.dot(dzb, km_s[h, keys, :], preferred_element_type=F32))
                dkh = jnp.dot(inv[h][1], dzb, preferred_element_type=F32)
                dvh = jnp.dot(inv[h][3], ws[h].astype(BF16), preferred_element_type=F32)
                dk, dv = (dkh, dvh) if h == 0 else (dk + dkh, dv + dvh)
            dkT_s[win] += dk
            dvT_s[win] += dv
            return out

        def qtile(iq, _):
            rows = pl.ds(pl.multiple_of(iq * TQ, TQ), TQ)
            last = (iq * TQ) // KW
            mask = _causal_mask(iq * TQ, last * KW)
            tt = tot_ref[rows, :]
            inv = []
            for h, m in enumerate((m0, m1)):
                qh, doh = qm_s[h, rows, :], dom_s[h, rows, :]
                total = jnp.sum(tt * m, axis=-1, keepdims=True) * (1.0 / HEAD_DIM)
                inv.append((qh, qh.astype(F32).T.astype(BF16), doh, doh.astype(F32).T.astype(BF16), total))

            z1, zq = jnp.zeros((TQ, 1), F32), jnp.zeros((TQ, 2 * HEAD_DIM), F32)
            st = lax.fori_loop(0, last, lambda win, st: both(inv, win, st, None), (z1, z1, zq, z1, z1, zq))
            st = both(inv, last, st, mask)
            dqn_s[rows, :] = st[2] + st[5]
            return 0

        lax.fori_loop(0, nq, qtile, 0)
        dkn = jnp.concatenate([dkT_s[w].T for w in range(nwin)], axis=0)
        dq, dqg = _head_norm_bwd(dqn_s[...] * scale, qn, qr, qg_ref[...], m0, m1)
        dk, dkg = _head_norm_bwd(dkn, kn, kr, kg_ref[...], m0, m1)
        dq_ref[...] = dq
        dk_ref[...] = dk
        dv_ref[...] = jnp.concatenate([dvT_s[w].T for w in range(nwin)], axis=0)
        dqg_ref[...] = dqg
        dkg_ref[...] = dkg

    assert S % KW == 0 and KW % TQ == 0
    nwin = S // KW
    blk = (S, 2 * HEAD_DIM)
    tblk = (nwin, 2 * HEAD_DIM, KW)
    gblk = (None, None, 1, 2 * HEAD_DIM)
    dq, dk, dv, dqg, dkg = pl.pallas_call(
        body, name=name, grid=(n_ex, nhp),
        in_specs=[pl.BlockSpec(blk, lambda e, h: (e, h)), pl.BlockSpec(blk, lambda e, h: (e, h)),
                  pl.BlockSpec(blk, lambda e, h: (e, h + nhp)),
                  pl.BlockSpec(blk, lambda e, h: (e, h)), pl.BlockSpec(blk, lambda e, h: (e, h)),
                  pl.BlockSpec((1, 2 * HEAD_DIM), lambda e, h: (0, 0)), pl.BlockSpec((1, 2 * HEAD_DIM), lambda e, h: (0, 0))],
        out_specs=[pl.BlockSpec(blk, lambda e, h: (e, h))] * 3 + [pl.BlockSpec(gblk, lambda e, h: (e, h, 0, 0))] * 2,
        out_shape=[jax.ShapeDtypeStruct((T, D), F32)] * 3 + [jax.ShapeDtypeStruct((n_ex, nhp, 1, 2 * HEAD_DIM), F32)] * 2,
        scratch_shapes=[pltpu.VMEM(tblk, BF16), pltpu.VMEM(tblk, BF16),
                        pltpu.VMEM((2,) + blk, BF16), pltpu.VMEM((2,) + blk, BF16), pltpu.VMEM((2,) + blk, BF16),
                        pltpu.VMEM(blk, F32), pltpu.VMEM(tblk, F32), pltpu.VMEM(tblk, F32)],
        compiler_params=_cp(("parallel", "parallel")),
    )(q, kv, kv, tot, do, qg, kg)
    return dq, dk, dv, dqg, dkg


def _place():
    return lax.axis_index("x"), lax.axis_index("y"), lax.axis_index("c")


def _all_gather8(x_shard, *, name):
    m_per, n = x_shard.shape

    def body(x_ref, out_ref, send_sems, recv_sems, local_sem):
        x, y, c = _place()
        me, sibling = (x, y, c), (x, y, 1 - c)
        chips = [(1 - x, y), (x, 1 - y), (1 - x, 1 - y)]

        def rows(px, py, pc):
            return out_ref.at[pl.ds((4 * px + 2 * py + pc) * m_per, m_per), :]

        def copy(k, block, to, src=None):
            return pltpu.make_async_remote_copy(
                src_ref=rows(*block) if src is None else src, dst_ref=rows(*block),
                send_sem=send_sems.at[k], recv_sem=recv_sems.at[k], device_id=to, device_id_type=MESH)

        mine = pltpu.make_async_copy(x_ref, rows(*me), local_sem)
        mine.start()
        first = [copy(0, me, sibling, src=x_ref)]
        first += [copy(1 + j, me, (*chip, c), src=x_ref) for j, chip in enumerate(chips)]
        for cp in first:
            cp.start()
        passed = [copy(4 + j, (*chip, c), sibling) for j, chip in enumerate(chips)]
        for j, chip in enumerate(chips):
            copy(1 + j, (*chip, c), me).wait_recv()
            passed[j].start()
        copy(0, sibling, me).wait_recv()
        for j, chip in enumerate(chips):
            copy(4 + j, (*chip, 1 - c), me).wait_recv()
        for cp in first + passed:
            cp.wait_send()
        mine.wait()

    return pl.pallas_call(
        body, name=name, out_shape=jax.ShapeDtypeStruct((8 * m_per, n), x_shard.dtype),
        in_specs=[pl.BlockSpec(memory_space=pltpu.VMEM)], out_specs=pl.BlockSpec(memory_space=pltpu.VMEM),
        scratch_shapes=[pltpu.SemaphoreType.DMA((7,)), pltpu.SemaphoreType.DMA((7,)), pltpu.SemaphoreType.DMA],
        compiler_params=pltpu.CompilerParams(vmem_limit_bytes=VMEM_LIMIT),
    )(x_shard)


def _sum_blocks(x, n, *, name):
    R = x.shape[0] // n

    def body(x_ref, o_ref):
        acc = x_ref[pl.ds(0, R), :]
        for k in range(1, n):
            acc = acc + x_ref[pl.ds(k * R, R), :]
        o_ref[...] = acc

    return pl.pallas_call(body, name=name, out_shape=jax.ShapeDtypeStruct((R, x.shape[1]), x.dtype),
                          compiler_params=pltpu.CompilerParams(vmem_limit_bytes=VMEM_LIMIT))(x)


def _colsum(x, *, name):
    def body(x_ref, o_ref):
        o_ref[...] = jnp.sum(x_ref[...], axis=0, keepdims=True)
    return pl.pallas_call(body, name=name, out_shape=jax.ShapeDtypeStruct((1, x.shape[1]), x.dtype))(x)


ANY = pl.BlockSpec(memory_space=pl.ANY)


class _ChipExchange:
    SCRATCH = [pltpu.SemaphoreType.DMA((3,)), pltpu.SemaphoreType.DMA((3,)), pltpu.SemaphoreType.DMA]

    @staticmethod
    def out_shape(src, scatter):
        return jax.ShapeDtypeStruct(((4,) + tuple(src.shape[1:])) if scatter else ((4, 2) + tuple(src.shape[1:])), src.dtype)

    def __init__(self, src_ref, out_ref, send_sems, recv_sems, local_sem, scatter):
        x, y, c = _place()
        myj = 2 * x + y
        chips = [(1 - x, y), (x, 1 - y), (1 - x, 1 - y)]

        def slot(j):
            return out_ref.at[j] if scatter else out_ref.at[j, c]

        def piece(j):
            return src_ref.at[j] if scatter else src_ref.at[c]

        self.mine = pltpu.make_async_copy(piece(myj), slot(myj), local_sem)
        self.sends = [pltpu.make_async_remote_copy(
            src_ref=piece(2 * cx + cy), dst_ref=slot(myj), send_sem=send_sems.at[k], recv_sem=recv_sems.at[k],
            device_id=(cx, cy, c), device_id_type=MESH) for k, (cx, cy) in enumerate(chips)]
        self.recvs = [pltpu.make_async_remote_copy(
            src_ref=slot(2 * cx + cy), dst_ref=slot(2 * cx + cy), send_sem=send_sems.at[k], recv_sem=recv_sems.at[k],
            device_id=(cx, cy, c), device_id_type=MESH) for k, (cx, cy) in enumerate(chips)]

    def start(self):
        self.mine.start()
        for cp in self.sends:
            cp.start()

    def wait(self):
        for cp in self.recvs:
            cp.wait_recv()
        for cp in self.sends:
            cp.wait_send()
        self.mine.wait()


def _sibling_fill(buf, *, axis, name):
    def half(ref, h):
        return ref.at[h] if axis == 0 else ref.at[:, h]

    def body(in_ref, out_ref, send_sem, recv_sem):
        x, y, c = _place()
        cp = pltpu.make_async_remote_copy(src_ref=half(out_ref, c), dst_ref=half(out_ref, c), send_sem=send_sem, recv_sem=recv_sem,
                                          device_id=(x, y, 1 - c), device_id_type=MESH)
        cp.start()
        pltpu.make_async_remote_copy(src_ref=half(out_ref, 1 - c), dst_ref=half(out_ref, 1 - c), send_sem=send_sem, recv_sem=recv_sem,
                                     device_id=(x, y, 1 - c), device_id_type=MESH).wait_recv()
        cp.wait_send()

    return pl.pallas_call(
        body, name=name, out_shape=jax.ShapeDtypeStruct(buf.shape, buf.dtype), in_specs=[ANY], out_specs=ANY,
        input_output_aliases={0: 0}, scratch_shapes=[pltpu.SemaphoreType.DMA, pltpu.SemaphoreType.DMA],
    )(buf)


def _sibling_swap_half(g, *, name):
    def body(g_ref, out_ref, send_sem, recv_sem):
        x, y, c = _place()
        cp = pltpu.make_async_remote_copy(src_ref=g_ref.at[:, 1 - c], dst_ref=out_ref, send_sem=send_sem, recv_sem=recv_sem,
                                          device_id=(x, y, 1 - c), device_id_type=MESH)
        cp.start()
        cp.wait()

    return pl.pallas_call(
        body, name=name, out_shape=jax.ShapeDtypeStruct((g.shape[0],) + g.shape[2:], g.dtype), in_specs=[ANY], out_specs=ANY,
        scratch_shapes=[pltpu.SemaphoreType.DMA, pltpu.SemaphoreType.DMA],
    )(g)


def _add_my_half(g, b, cidx, *, name, tr=256):
    n, _, R, C = g.shape
    tr = math.gcd(tr, R)

    def body(c_ref, g_ref, b_ref, o_ref):
        o_ref[...] = (g_ref[...] + b_ref[...]).astype(o_ref.dtype)

    return pl.pallas_call(
        body, name=name, out_shape=jax.ShapeDtypeStruct((n, R, C), BF16),
        grid_spec=pltpu.PrefetchScalarGridSpec(
            num_scalar_prefetch=1, grid=(n, R // tr),
            in_specs=[pl.BlockSpec((None, None, tr, C), lambda j, i, c: (j, c[0], i, 0)),
                      pl.BlockSpec((None, tr, C), lambda j, i, c: (j, i, 0))],
            out_specs=pl.BlockSpec((None, tr, C), lambda j, i, c: (j, i, 0))),
        compiler_params=_cp(("parallel", "parallel")),
    )(cidx, g, b)


def _sum4_into_half(q, cidx, *, name, tr=256):
    _, R, C = q.shape
    tr = math.gcd(tr, R)

    def body(c_ref, q_ref, o_ref):
        o_ref[...] = ((q_ref[0].astype(F32) + q_ref[1].astype(F32)) + q_ref[2].astype(F32)) + q_ref[3].astype(F32)

    return pl.pallas_call(
        body, name=name, out_shape=jax.ShapeDtypeStruct((2, R, C), F32),
        grid_spec=pltpu.PrefetchScalarGridSpec(
            num_scalar_prefetch=1, grid=(R // tr,),
            in_specs=[pl.BlockSpec((4, tr, C), lambda i, c: (0, i, 0))],
            out_specs=pl.BlockSpec((None, tr, C), lambda i, c: (c[0], i, 0))),
        compiler_params=_cp(("parallel",)),
    )(cidx, q)


def _pack_rows(parts, width=1024):
    rows, spans, r0 = [], [], 0
    for p in parts:
        n = p.size
        nr = 8 * (-(-n // (8 * width)))
        flat = p.reshape(-1)
        if nr * width != n:
            flat = jnp.pad(flat, (0, nr * width - n))
        rows.append(flat.reshape(nr, width))
        spans.append((r0, nr, n, p.shape))
        r0 += nr
    return jnp.concatenate(rows, axis=0), spans


def _unpack_rows(buf, spans):
    return [buf[r0:r0 + nr].reshape(-1)[:n].reshape(shape) for (r0, nr, n, shape) in spans]


def kernel(x, c, ada_w, ada_b, mix_norm_g, mlp_norm_g, mlp_w1, mlp_w2, s5_a_re, s5_a_im, s5_log_dt, s5_b_re, s5_b_im, s5_c_re, s5_c_im, s5_d, s5_w_glu, kv_ada_w, kv_ada_b, kv_norm_g, w_kv, k_norm_g, sb_w_q, q_norm_g, sb_w_o, loss_target, m_ada_w, m_ada_b, m_mix_norm_g, m_mlp_norm_g, m_mlp_w1, m_mlp_w2, m_s5_a_re, m_s5_a_im, m_s5_log_dt, m_s5_b_re, m_s5_b_im, m_s5_c_re, m_s5_c_im, m_s5_d, m_s5_w_glu, m_kv_ada_w, m_kv_ada_b, m_kv_norm_g, m_w_kv, m_k_norm_g, m_sb_w_q, m_q_norm_g, m_sb_w_o, v_ada_w, v_ada_b, v_mix_norm_g, v_mlp_norm_g, v_mlp_w1, v_mlp_w2, v_s5_a_re, v_s5_a_im, v_s5_log_dt, v_s5_b_re, v_s5_b_im, v_s5_c_re, v_s5_c_im, v_s5_d, v_s5_w_glu, v_kv_ada_w, v_kv_ada_b, v_kv_norm_g, v_w_kv, v_k_norm_g, v_sb_w_q, v_q_norm_g, v_sb_w_o):
    E, S, D = x.shape
    T = E * S
    FF = 4 * D
    NB = 8 * E
    px, py, pc = _place()
    chip = 2 * px + py
    dev = 4 * px + 2 * py + pc
    cidx = jnp.reshape(pc, (1,)).astype(jnp.int32)
    x0 = x.reshape(T, D)
    tgt = loss_target.reshape(T, D)

    c_all = _all_gather8(c.reshape(-1, 128), name="ag_c").reshape(NB, D)
    sc_all = (c_all * _sigmoid(c_all)).astype(BF16)
    wa = ada_w.shape[2]
    wk = kv_ada_w.shape[1]
    m_sh = jnp.concatenate([_mm(sc_all, ada_w[0], "nn", name="ada0", tn=256),
                            _mm(sc_all, ada_w[1], "nn", name="ada1", tn=256),
                            _mm(sc_all, kv_ada_w, "nn", name="ada_kv", tn=256)], axis=1)
    m_all = _all_gather8(m_sh, name="ag_m").reshape(4, 2, NB, 2 * wa + wk)[:, 0]
    mods = []
    for l in range(2):
        full = jnp.transpose(m_all[:, :, l * wa:(l + 1) * wa], (1, 0, 2)).reshape(NB, 6 * D) + ada_b[l]
        mine = lax.dynamic_slice_in_dim(full, E * dev, E, axis=0)
        mods.append([mine[:, i * D:(i + 1) * D].reshape(E, 1, D) for i in range(6)])
    full = jnp.transpose(m_all[:, :, 2 * wa:], (1, 0, 2)).reshape(NB, 2 * D) + kv_ada_b
    mine = lax.dynamic_slice_in_dim(full, E * dev, E, axis=0)
    kv_sh, kv_sc = [mine[:, i * D:(i + 1) * D].reshape(E, 1, D) for i in range(2)]

    wpack_a = jnp.concatenate([mlp_w1[0], mlp_w2[0], jnp.concatenate([s5_w_glu[0], w_kv], axis=1), sb_w_q[0]], axis=0).astype(BF16)
    wpack_b = jnp.concatenate([mlp_w1[1], mlp_w2[1], sb_w_o[0]], axis=0).astype(BF16)
    RA, RB = wpack_a.shape[0], wpack_b.shape[0]
    RW = RA + RB

    tm = min(1024, S)
    gbuf = [jax.ShapeDtypeStruct((4, RW, D), F32)]

    def grad_mm(act, dout, kind, roff, nr, c0, nc, name):
        gbuf[0] = _mm(act, dout, "tn", name=name, tk=2048, into=_Sharded(gbuf[0], kind, roff, nr, c0, nc))

    def mlp_fwd(xa, l, mod):
        sh_m, sc_m, g_m = mod[3], mod[4], mod[5]
        h = _norm_mod_fwd(xa, mlp_norm_g[l:l + 1], sh_m, sc_m, n_ex=E, out_dtype=BF16, name=f"mlp_norm{l}")
        r = _mm(h, W1[l], "nn", name=f"mlp_up{l}", out_dtypes=(BF16,), tm=tm,
                epilogue=lambda acc: (jnp.square(jnp.maximum(acc, 0.0)),))
        xb, ff = _mm(r, W2[l], "nn", name=f"mlp_down{l}", out_dtypes=(F32, F32), tm=tm,
                     extras=[_mn_extra(xa), _vec_extra(g_m, S)],
                     epilogue=lambda acc, xat, gt: (xat + gt * acc, acc))
        return xb, (h, r, ff)

    def mlp_bwd(dxb, xa, l, mod, saved):
        sc_m, g_m = mod[4], mod[5]
        h, r, ff = saved
        (dff,), (dgm,) = _rowwise(lambda d, f, g: ([g * d], [_csum(d * f)]), [(dxb, D, 0), (ff, D, 0)], [g_m], [],
                                  [(D, BF16)], [D], n_ex=E, name=f"mlp_gate_bwd{l}")
        da = _mm(dff, W2[l], "nt", name=f"mlp_down_dx{l}", out_dtypes=(BF16,), tm=tm, extras=[_mn_extra(r)],
                 epilogue=lambda acc, rt: (acc * (2.0 * jnp.sqrt(rt.astype(F32))),))
        grad_mm(r, dff, "rows", (2 + l) * D, D, 0, D, f"mlp_down_dw{l}")
        dh = _mm(da, W1[l], "nt", name=f"mlp_up_dx{l}", tm=tm)
        grad_mm(h, da, "cols", l * D, D, 0, D, f"mlp_up_dw{l}")
        (dxa,), (dsh, dsc, dg) = _norm_mod_bwd(xa, dh, dxb, mlp_norm_g[l:l + 1], sc_m, n_ex=E, name=f"mlp_norm_bwd{l}")
        return dxa, (dsh, dsc, dgm), dg

    ab_re, ab_im, bb_re, bb_im = _s5_disc(s5_a_re[0], s5_a_im[0], s5_log_dt[0], s5_b_re[0], s5_b_im[0])
    cf, cr = _s5_consts(ab_re, ab_im)
    Wb, Wc = _s5_blockdiag(bb_re, bb_im, s5_c_re[0], s5_c_im[0])
    ng = D // U_LANES
    nd = s5_d.size // 128
    d_full = _all_gather8(jnp.pad(s5_d.reshape(nd, 128), ((0, 8 - nd), (0, 0))), name="ag_d")
    d_full = d_full.reshape(4, 2, 8, 128)[:, 0, :nd].reshape(1, D)

    mod0, mod1 = mods
    h0 = _norm_mod_fwd(x0, mix_norm_g[0:1], mod0[0], mod0[1], n_ex=E, out_dtype=F32, name="mix_norm0")
    y, gy, wfull_a = _s5_fwd(h0, Wb, Wc, cf, d_full, wpack_a.reshape(2, RA // 2, D), n_ex=E, name="s5_fwd")
    wfull_a = _sibling_fill(wfull_a, axis=1, name="wgather_a_d2d").reshape(4, RA, D)

    W1 = [_Sharded(wfull_a, "cols", 0, D, 0, D), None]
    W2 = [_Sharded(wfull_a, "rows", D, D, 0, D), None]
    Wglu = _Sharded(wfull_a, "cols", 2 * D, D, 0, D // 2)
    Wkv = _Sharded(wfull_a, "cols", 2 * D, D, D // 2, D // 2)
    Wq = _Sharded(wfull_a, "rows", 3 * D, D // 4, 0, D)
    vg = _mm(gy, Wglu, "nn", name="glu_up", tm=tm)
    (x1,), _ = _rowwise(lambda v, g, xt, ga: ([xt + ga * (v * _sigmoid(g))], []),
                        [(vg, D, 0), (vg, D, 1), (x0, D, 0)], [mod0[2]], [], [(D, F32)], [], n_ex=E, name="glu_gate")
    x2, saved_mlp0 = mlp_fwd(x1, 0, mod0)

    hkv = _norm_mod_fwd(x2, kv_norm_g.reshape(1, D), kv_sh, kv_sc, n_ex=E, out_dtype=BF16, name="kv_norm")
    kvf = _mm(hkv, Wkv, "nn", name="kv_proj", tm=tm)
    h1 = _norm_mod_fwd(x2, mix_norm_g[1:2], mod1[0], mod1[1], n_ex=E, out_dtype=BF16, name="mix_norm1")
    qf = _mm(h1, Wq, "nn", name="q_proj", tm=tm)
    qg2 = jnp.tile(q_norm_g.reshape(1, HEAD_DIM), (1, 2))
    kg2 = jnp.tile(k_norm_g.reshape(1, HEAD_DIM), (1, 2))
    o, lf_tot, wfull_b = _attn_fwd(qf, kvf, qg2, kg2, wpack_b.reshape(2, RB // 2, D), n_ex=E, name="attn_fwd")
    wfull_b = _sibling_fill(wfull_b, axis=1, name="wgather_b_d2d").reshape(4, RB, D)
    W1[1] = _Sharded(wfull_b, "cols", 0, D, 0, D)
    W2[1] = _Sharded(wfull_b, "rows", D, D, 0, D)
    Wo = _Sharded(wfull_b, "rows", 2 * D, D // 4, 0, D)
    x3, mix1 = _mm(o, Wo, "nn", name="o_proj", out_dtypes=(F32, F32), tm=tm,
                   extras=[_mn_extra(x2), _vec_extra(mod1[2], S)],
                   epilogue=lambda acc, xat, gt: (xat + gt * acc, acc))
    x4, saved_mlp1 = mlp_fwd(x3, 1, mod1)

    (dx4,), (lsum,) = _rowwise(lambda xt, tt: ([(xt - tt) * (1.0 / D)], [_csum(jnp.square(xt - tt)) * (0.5 / D)]),
                               [(x4, D, 0), (tgt, D, 0)], [], [], [(D, F32)], [D], n_ex=E, name="loss")
    loss = lax.psum(jnp.sum(lsum), ("x", "y", "c"))

    dx3, (dsh_m1, dsc_m1, dgm1), dg_mlp1 = mlp_bwd(dx4, x3, 1, mod1, saved_mlp1)
    (dmix1,), (dga1,) = _rowwise(lambda d, f, g: ([g * d], [_csum(d * f)]), [(dx3, D, 0), (mix1, D, 0)], [mod1[2]], [],
                                 [(D, BF16)], [D], n_ex=E, name="attn_gate_bwd")
    do = _mm(dmix1, Wo, "nt", name="o_proj_dx", tm=tm)
    grad_mm(o, dmix1, "rows", 5 * D + D // 4, D // 4, 0, D, "o_proj_dw")
    dq, dk, dv, dqg, dkg = _attn_bwd(qf, kvf, lf_tot, do, qg2, kg2, n_ex=E, name="attn_bwd")
    dh1 = _mm(dq, Wq, "nt", name="q_proj_dx", tm=tm)
    grad_mm(h1, dq, "rows", 5 * D, D // 4, 0, D, "q_proj_dw")
    (dx2,), (dsh_a1, dsc_a1, dg_mix1) = _norm_mod_bwd(x2, dh1, dx3, mix_norm_g[1:2], mod1[1], n_ex=E, name="mix_norm_bwd1")
    dkv = jnp.concatenate([dk, dv], axis=1)
    dhkv = _mm(dkv, Wkv, "nt", name="kv_proj_dx", tm=tm)
    grad_mm(hkv, dkv, "cols", 4 * D, D, D // 2, D // 2, "kv_proj_dw")
    (dx2,), (dkv_sh, dkv_sc, dg_kv) = _norm_mod_bwd(x2, dhkv, dx2, kv_norm_g.reshape(1, D), kv_sc, n_ex=E, name="kv_norm_bwd")

    dx1, (dsh_m0, dsc_m0, dgm0), dg_mlp0 = mlp_bwd(dx2, x1, 0, mod0, saved_mlp0)

    def glu_bwd(v, g, d, ga):
        sg = _sigmoid(g)
        dm = ga * d
        return [jnp.concatenate([dm * sg, dm * v * sg * (1.0 - sg)], axis=1)], [_csum(d * (v * sg))]
    (dvg,), (dga0,) = _rowwise(glu_bwd, [(vg, D, 0), (vg, D, 1), (dx1, D, 0)], [mod0[2]], [], [(2 * D, BF16)], [D],
                               n_ex=E, name="glu_gate_bwd")
    dgy = _mm(dvg, Wglu, "nt", name="glu_up_dx", tm=tm)
    grad_mm(gy, dvg, "cols", 4 * D, D, 0, D // 2, "glu_up_dw")

    gpack = gbuf[0].reshape(4, 2, RW // 2, D)
    theirs = _sibling_swap_half(gpack, name="gscatter_d2d")
    chip_sum = _add_my_half(gpack, theirs, cidx, name="gscatter_add")
    dh0, dWb, dWc, dab, dd, from_chips = _s5_bwd(h0, y, dgy, Wb, Wc, cf, cr, d_full, chip_sum, n_ex=E, name="s5_bwd")
    ghalf = _sum4_into_half(from_chips, cidx, name="gscatter_sum")
    gsh = _sibling_fill(ghalf, axis=0, name="gscatter_fill").reshape(RW, D)
    (gx,), (dsh_a0, dsc_a0, dg_mix0) = _norm_mod_bwd(x0, dh0, dx1, mix_norm_g[0:1], mod0[1], n_ex=E, name="mix_norm_bwd0")
    grad_x = gx.reshape(E, S, D)

    dm_mine = jnp.concatenate([t.reshape(E, D) for t in
                               (dsh_a0, dsc_a0, dga0, dsh_m0, dsc_m0, dgm0, dsh_a1, dsc_a1, dga1, dsh_m1, dsc_m1, dgm1, dkv_sh, dkv_sc)], axis=1)
    dm_all = _all_gather8(dm_mine.reshape(8, -1), name="ag_dm").reshape(NB, 14 * D)
    sc_f32 = c_all * _sigmoid(c_all)
    g_ada_w = jnp.stack([_mm(sc_f32, lax.dynamic_slice_in_dim(dm_all, l * 6 * D + chip * wa, wa, axis=1), "tn", name=f"ada_dw{l}", tn=256)
                         for l in range(2)])
    g_kv_ada_w = _mm(sc_f32, lax.dynamic_slice_in_dim(dm_all, 12 * D + chip * wk, wk, axis=1), "tn", name="ada_kv_dw", tn=256)
    db_all = _colsum(dm_all, name="ada_db")
    g_ada_b = db_all[0, :12 * D].reshape(2, 6 * D)
    g_kv_ada_b = db_all[0, 12 * D:]

    dWb_re, dWb_im, dC_re, dC_im = _s5_unblock(dWb, dWc)
    small_parts = [dg_mix0.sum(0), dg_mix1.sum(0), dg_mlp0.sum(0), dg_mlp1.sum(0), dg_kv.sum(0),
                   dqg.sum((0, 1, 2)).reshape(2, HEAD_DIM).sum(0), dkg.sum((0, 1, 2)).reshape(2, HEAD_DIM).sum(0),
                   dd[:, 0, :], dab[:, 0, :], dab[:, 1, :], dWb_re, dWb_im, dC_re, dC_im]
    spack, spans = _pack_rows(small_parts)
    ssum = _sum_blocks(_all_gather8(spack, name="ag_small"), 8, name="sum_small")
    (g_mix0, g_mix1, g_mlp0, g_mlp1, g_kvn, g_qn, g_kn, g_d, g_abr, g_abi, g_bbr, g_bbi, g_cre, g_cim) = _unpack_rows(ssum, spans)
    _, disc_vjp = jax.vjp(_s5_disc, s5_a_re[0], s5_a_im[0], s5_log_dt[0], s5_b_re[0], s5_b_im[0])
    g_are, g_aim, g_ldt, g_bre, g_bim = disc_vjp((g_abr.reshape(ab_re.shape), g_abi.reshape(ab_im.shape), g_bbr, g_bbi))
    g_s5d = lax.dynamic_slice_in_dim(g_d.reshape(1, D), chip * s5_d.shape[1], s5_d.shape[1], axis=1)

    def upd_big(w, m, v, roff, cb, name):
        shape = w.shape
        W = shape[-1]
        d_, m_, v_, g_ = _adamw2d(w.reshape(-1, W), gsh, m.reshape(-1, W), v.reshape(-1, W), name=name, g_roff=roff, g_cb=cb)
        return [t.reshape(shape) for t in (g_, d_, m_, v_)]

    def upd_own(w, g, m, v, name):
        shape = w.shape
        W = shape[-1]
        d_, m_, v_, g_ = _adamw2d(w.reshape(-1, W), g.reshape(-1, W), m.reshape(-1, W), v.reshape(-1, W), name=name)
        return [t.reshape(shape) for t in (g_, d_, m_, v_)]

    res = {}
    res["ada_w"] = upd_own(ada_w, g_ada_w, m_ada_w, v_ada_w, "adam_ada_w")
    res["kv_ada_w"] = upd_own(kv_ada_w, g_kv_ada_w, m_kv_ada_w, v_kv_ada_w, "adam_kv_ada_w")
    res["mlp_w1"] = upd_big(mlp_w1, m_mlp_w1, v_mlp_w1, 0, 0, "adam_w1")
    res["mlp_w2"] = upd_big(mlp_w2, m_mlp_w2, v_mlp_w2, 2 * D, 0, "adam_w2")
    res["s5_w_glu"] = upd_big(s5_w_glu, m_s5_w_glu, v_s5_w_glu, 4 * D, 0, "adam_glu")
    res["w_kv"] = upd_big(w_kv, m_w_kv, v_w_kv, 4 * D, 1, "adam_wkv")
    res["sb_w_q"] = upd_big(sb_w_q, m_sb_w_q, v_sb_w_q, 5 * D, 0, "adam_wq")
    res["sb_w_o"] = upd_big(sb_w_o, m_sb_w_o, v_sb_w_o, 5 * D + D // 4, 0, "adam_wo")

    small = {
        "ada_b": (ada_b, g_ada_b, m_ada_b, v_ada_b),
        "mix_norm_g": (mix_norm_g, jnp.stack([g_mix0, g_mix1]), m_mix_norm_g, v_mix_norm_g),
        "mlp_norm_g": (mlp_norm_g, jnp.stack([g_mlp0, g_mlp1]), m_mlp_norm_g, v_mlp_norm_g),
        "s5_a_re": (s5_a_re, g_are[None], m_s5_a_re, v_s5_a_re),
        "s5_a_im": (s5_a_im, g_aim[None], m_s5_a_im, v_s5_a_im),
        "s5_log_dt": (s5_log_dt, g_ldt[None], m_s5_log_dt, v_s5_log_dt),
        "s5_b_re": (s5_b_re, g_bre[None], m_s5_b_re, v_s5_b_re),
        "s5_b_im": (s5_b_im, g_bim[None], m_s5_b_im, v_s5_b_im),
        "s5_c_re": (s5_c_re, g_cre[None], m_s5_c_re, v_s5_c_re),
        "s5_c_im": (s5_c_im, g_cim[None], m_s5_c_im, v_s5_c_im),
        "s5_d": (s5_d, g_s5d, m_s5_d, v_s5_d),
        "kv_ada_b": (kv_ada_b, g_kv_ada_b, m_kv_ada_b, v_kv_ada_b),
        "kv_norm_g": (kv_norm_g, g_kvn, m_kv_norm_g, v_kv_norm_g),
        "k_norm_g": (k_norm_g, g_kn, m_k_norm_g, v_k_norm_g),
        "q_norm_g": (q_norm_g, g_qn.reshape(q_norm_g.shape), m_q_norm_g, v_q_norm_g),
    }
    names = list(small)
    packs = [_pack_rows([small[n][i].reshape(small[n][0].shape) for n in names]) for i in range(4)]
    sp = packs[0][1]
    d_, m_, v_, g_ = _adamw2d(packs[0][0], packs[1][0], packs[2][0], packs[3][0], name="adam_small")
    for n, gg, dd_, mm_, vv_ in zip(names, _unpack_rows(g_, sp), _unpack_rows(d_, sp), _unpack_rows(m_, sp), _unpack_rows(v_, sp)):
        res[n] = [gg, dd_, mm_, vv_]

    order = ["ada_w", "ada_b", "mix_norm_g", "mlp_norm_g", "mlp_w1", "mlp_w2", "s5_a_re", "s5_a_im", "s5_log_dt", "s5_b_re", "s5_b_im",
             "s5_c_re", "s5_c_im", "s5_d", "s5_w_glu", "kv_ada_w", "kv_ada_b", "kv_norm_g", "w_kv", "k_norm_g", "sb_w_q", "q_norm_g", "sb_w_o"]
    return (loss, grad_x, *[res[n][0] for n in order], *[res[n][1] for n in order], *[res[n][2] for n in order], *[res[n][3] for n in order])
```

```python
import functools
import math

import jax
import jax.numpy as jnp
from jax import lax
from jax.experimental import pallas as pl
from jax.experimental.pallas import tpu as pltpu

F32 = jnp.float32
BF16 = jnp.bfloat16
EPS = 1e-6
HEAD_DIM = 64
S5_GROUP = 16
S5_STATE = 64
GROUPS_PER_STEP = 8
U_LANES = GROUPS_PER_STEP * S5_GROUP
ST_LANES = GROUPS_PER_STEP * S5_STATE
SCAN_LANES = 256
SCAN_UNROLL = 4
VMEM_LIMIT = 56 * 1024 * 1024
ADAM_LR, ADAM_B1, ADAM_B2, ADAM_EPS, ADAM_WD, ADAM_STEP = 0.001, 0.9, 0.999, 1e-08, 0.01, 10
MESH = pl.DeviceIdType.MESH


def _cp(sem):
    return pltpu.CompilerParams(dimension_semantics=sem, vmem_limit_bytes=VMEM_LIMIT)


class _Sharded:
    def __init__(self, buf, kind, roff, nr, c0, nc):
        self.buf, self.kind, self.roff, self.nr, self.c0, self.nc = buf, kind, roff, nr, c0, nc
        self.shape = (nr, 4 * nc) if kind == "cols" else (4 * nr, nc)

    def operand(self, dims, tn, tk):
        roff, nr, c0, nc = self.roff, self.nr, self.c0, self.nc
        if self.kind == "cols" and dims == "nn":
            tk = min(tk, nr)
            assert roff % tk == 0
            return nc, tk, (None, tk, nc), lambda i, j, k: (j, roff // tk + k, c0 // nc)
        if self.kind == "cols":
            tn = min(tn, nr)
            assert roff % tn == 0
            return tn, nc, (None, tn, nc), lambda i, j, k: (k, roff // tn + j, c0 // nc)
        if dims == "nn":
            tn = min(tn, nc)
            assert roff % nr == 0 and c0 % tn == 0
            return tn, nr, (None, nr, tn), lambda i, j, k: (k, roff // nr, c0 // tn + j)
        tk = min(tk, nc)
        assert roff % nr == 0 and c0 % tk == 0
        return nr, tk, (None, nr, tk), lambda i, j, k: (j, roff // nr, c0 // tk + k)

    def result(self, tm, tn):
        roff, nr, c0, nc = self.roff, self.nr, self.c0, self.nc
        if self.kind == "cols":
            tm = min(tm, nr)
            assert roff % tm == 0
            return tm, nc, (None, tm, nc), lambda i, j, k: (j, roff // tm + i, c0 // nc)
        tm, tn = min(tm, nr), min(tn, nc)
        assert roff % tm == 0 and c0 % tn == 0
        per = nr // tm
        return tm, tn, (None, tm, tn), lambda i, j, k: (i // per, roff // tm + i % per, c0 // tn + j)


def _mm(a, b, dims, *, name, out_dtypes=(F32,), epilogue=None, extras=(), tm=512, tn=1024, tk=1024, into=None):
    bshape = b.shape
    if dims == "nn":
        (M, K), (_, N) = a.shape, bshape
    elif dims == "nt":
        (M, K), (N, _) = a.shape, bshape
    else:
        (K, M), (_, N) = a.shape, bshape
    tm, tn, tk = min(tm, M), min(tn, N), min(tk, K)
    b_arr = b
    if into is not None:
        assert (M, N) == into.shape and len(out_dtypes) == 1 and not isinstance(b, _Sharded)
        tm, tn, o_blk, o_map = into.result(tm, tn)
        out_specs, out_shape = [pl.BlockSpec(o_blk, o_map)], [jax.ShapeDtypeStruct(into.buf.shape, into.buf.dtype)]
    if isinstance(b, _Sharded):
        tn, tk, b_blk, b_map = b.operand(dims, tn, tk)
        b_spec, b_arr = pl.BlockSpec(b_blk, b_map), b.buf
    else:
        b_spec = pl.BlockSpec((tn, tk), lambda i, j, k: (j, k)) if dims == "nt" else pl.BlockSpec((tk, tn), lambda i, j, k: (k, j))
    if into is None:
        out_specs = [pl.BlockSpec((tm, tn), lambda i, j, k: (i, j)) for _ in out_dtypes]
        out_shape = [jax.ShapeDtypeStruct((M, N), d) for d in out_dtypes]
    assert M % tm == 0 and N % tn == 0 and K % tk == 0, (M, N, K, tm, tn, tk)
    nk = K // tk
    extras = [e(tm, tn) for e in extras]
    a_spec = pl.BlockSpec((tk, tm), lambda i, j, k: (k, i)) if dims == "tn" else pl.BlockSpec((tm, tk), lambda i, j, k: (i, k))
    contract = {"nn": ((1,), (0,)), "nt": ((1,), (1,)), "tn": ((0,), (0,))}[dims]
    n_ex, n_out = len(extras), len(out_dtypes)
    chain = [into.buf] if into is not None and not isinstance(into.buf, jax.ShapeDtypeStruct) else []
    n_in = n_ex + len(chain)

    def finish(r, ex, outs):
        res = epilogue(r, *[e[...] for e in ex]) if epilogue is not None else (r,)
        for o, v in zip(outs, res):
            o[...] = v.astype(o.dtype)

    def product(a_ref, b_ref):
        return lax.dot_general(a_ref[...].astype(BF16), b_ref[...].astype(BF16), (contract, ((), ())), preferred_element_type=F32)

    def body_one(a_ref, b_ref, *rest):
        finish(product(a_ref, b_ref), rest[:n_ex], rest[n_in:])

    def body_acc(a_ref, b_ref, *rest):
        ex, outs, acc = rest[:n_ex], rest[n_in:n_in + n_out], rest[-1]
        k = pl.program_id(2)

        @pl.when(k == 0)
        def _():
            acc[...] = product(a_ref, b_ref)

        @pl.when(k > 0)
        def _():
            acc[...] += product(a_ref, b_ref)

        @pl.when(k == nk - 1)
        def _():
            finish(acc[...], ex, outs)

    out = pl.pallas_call(
        body_one if nk == 1 else body_acc, name=name, grid=(M // tm, N // tn, nk),
        in_specs=[a_spec, b_spec] + [pl.BlockSpec(blk, im) for (_, blk, im) in extras] + [ANY for _ in chain],
        out_specs=out_specs, out_shape=out_shape,
        input_output_aliases={2 + n_ex: 0} if chain else {},
        scratch_shapes=[] if nk == 1 else [pltpu.VMEM((tm, tn), F32)],
        compiler_params=_cp(("parallel", "parallel", "arbitrary")),
    )(a, b_arr, *[e[0] for e in extras], *chain)
    return out if n_out > 1 else out[0]


def _mn_extra(arr):
    return lambda tm, tn: (arr, (tm, tn), lambda i, j, k: (i, j))


def _vec_extra(vec, S):
    return lambda tm, tn: (vec, (None, 1, tn), lambda i, j, k: ((i * tm) // S, 0, j))


def _rowwise(fn, rows, vecs=(), consts=(), out_rows=(), out_sums=(), *, n_ex, name, tr=512):
    rows = [r if len(r) == 4 else (*r, 0) for r in rows]
    S = min(r[0].shape[0] for r in rows if r[3] == 0) // n_ex
    tr = math.gcd(tr, S)
    assert S % tr == 0
    nb = S // tr
    in_specs = []
    for (arr, w, cb, roff) in rows:
        assert roff % tr == 0
        in_specs.append(pl.BlockSpec((tr, w), functools.partial(lambda e, i, cb, ro: (e * nb + i + ro, cb), cb=cb, ro=roff // tr)))
    for v in vecs:
        in_specs.append(pl.BlockSpec((None, 1, v.shape[-1]), lambda e, i: (e, 0, 0)))
    for c in consts:
        in_specs.append(pl.BlockSpec((1, c.shape[-1]), lambda e, i: (0, 0)))
    n_in, n_or, n_os = len(in_specs), len(out_rows), len(out_sums)
    out_specs = [pl.BlockSpec((tr, w), lambda e, i: (e * nb + i, 0)) for (w, _) in out_rows]
    out_specs += [pl.BlockSpec((None, 1, w), lambda e, i: (e, 0, 0)) for w in out_sums]
    out_shape = [jax.ShapeDtypeStruct((n_ex * S, w), d) for (w, d) in out_rows]
    out_shape += [jax.ShapeDtypeStruct((n_ex, 1, w), F32) for w in out_sums]

    def body(*refs):
        ins, o_r, o_s = refs[:n_in], refs[n_in:n_in + n_or], refs[n_in + n_or:]
        ro, so = fn(*[r[...] for r in ins])
        for o, v in zip(o_r, ro):
            o[...] = v.astype(o.dtype)
        i = pl.program_id(1)
        for o, v in zip(o_s, so):
            @pl.when(i == 0)
            def _(o=o, v=v):
                o[...] = v

            @pl.when(i > 0)
            def _(o=o, v=v):
                o[...] += v

    outs = pl.pallas_call(
        body, name=name, grid=(n_ex, nb), in_specs=in_specs, out_specs=out_specs, out_shape=out_shape,
        compiler_params=_cp(("parallel", "arbitrary")),
    )(*[r[0] for r in rows], *vecs, *consts)
    return outs[:n_or], outs[n_or:]


def _csum(x):
    return jnp.sum(x, axis=0, keepdims=True)


def _norm_mod_fwd(x, g, sh, sc, *, n_ex, out_dtype, name):
    def fn(xt, sht, sct, gt):
        r = lax.rsqrt(jnp.mean(xt * xt, axis=-1, keepdims=True) + EPS)
        return [(xt * r * gt) * (1.0 + sct) + sht], []
    D = x.shape[1]
    return _rowwise(fn, [(x, D, 0)], [sh, sc], [g], [(D, out_dtype)], [], n_ex=n_ex, name=name)[0][0]


def _norm_mod_bwd(x, dh, dres, g, sc, *, n_ex, name):
    def fn(xt, dht, drt, sct, gt):
        dht = dht.astype(F32)
        r = lax.rsqrt(jnp.mean(xt * xt, axis=-1, keepdims=True) + EPS)
        n = xt * r
        y = n * gt
        dy = dht * (1.0 + sct)
        dn = dy * gt
        dx = r * (dn - n * jnp.mean(dn * n, axis=-1, keepdims=True))
        return [drt + dx], [_csum(dht), _csum(dht * y), _csum(dy * n)]
    D = x.shape[1]
    return _rowwise(fn, [(x, D, 0), (dh, D, 0), (dres, D, 0)], [sc], [g], [(D, F32)], [D, D, D], n_ex=n_ex, name=name)


def _sigmoid(x):
    return 1.0 / (1.0 + jnp.exp(-x))


def _gelu(y):
    return 0.5 * y * (1.0 + jnp.tanh(0.7978845608028654 * (y + 0.044715 * y * y * y)))


def _gelu_grad(y):
    t = jnp.tanh(0.7978845608028654 * (y + 0.044715 * y * y * y))
    return 0.5 * (1.0 + t) + 0.5 * y * (1.0 - t * t) * 0.7978845608028654 * (1.0 + 3 * 0.044715 * y * y)


def _adamw_fn(w, g, m, v):
    m2 = ADAM_B1 * m + (1.0 - ADAM_B1) * g
    v2 = ADAM_B2 * v + (1.0 - ADAM_B2) * (g * g)
    m_hat = m2 / (1.0 - ADAM_B1 ** ADAM_STEP)
    v_hat = v2 / (1.0 - ADAM_B2 ** ADAM_STEP)
    delta = -ADAM_LR * (m_hat / (jnp.sqrt(v_hat) + ADAM_EPS) + ADAM_WD * w)
    return delta, m2, v2


def _adamw2d(w, g, m, v, *, name, g_roff=0, g_cb=0):
    R, W = w.shape

    def fn(wt, gt, mt, vt):
        d, m2, v2 = _adamw_fn(wt, gt, mt, vt)
        return [d, m2, v2, gt], []
    return _rowwise(fn, [(w, W, 0), (g, W, g_cb, g_roff), (m, W, 0), (v, W, 0)], [], [],
                    [(W, F32)] * 4, [], n_ex=1, name=name, tr=256)[0]


def _scan_tiles(re_ref, im_ref, cf, lane0, n_chunks, reverse, extra=None):
    L = SCAN_LANES
    lanes = pl.ds(lane0, L)
    A = [cf[i, :, lanes] for i in range(8)]
    shifts = (7, 6, 4) if reverse else (1, 2, 4)
    edge = 0 if reverse else 7

    U = SCAN_UNROLL
    n_groups = n_chunks // U

    def body(c, carry):
        first = ((n_groups - 1 - c) if reverse else c) * U
        rows = pl.ds(pl.multiple_of(first * 8, 8 * U), 8 * U)
        big_r, big_i = re_ref[rows, lanes], im_ref[rows, lanes]
        tiles = []
        for u in range(U):
            xr, xi = big_r[8 * u:8 * u + 8, :], big_i[8 * u:8 * u + 8, :]
            for idx, sft in enumerate(shifts):
                ar, ai = A[2 * idx], A[2 * idx + 1]
                rr, ri = pltpu.roll(xr, sft, 0), pltpu.roll(xi, sft, 0)
                xr, xi = xr + ar * rr - ai * ri, xi + ar * ri + ai * rr
            tiles.append((xr, xi))
        pr, pi = A[6], A[7]
        cr, ci = carry[0], carry[1]
        for u in (range(U - 1, -1, -1) if reverse else range(U)):
            xr, xi = tiles[u]
            xr, xi = xr + pr * cr - pi * ci, xi + pr * ci + pi * cr
            tiles[u] = (xr, xi)
            cr, ci = jnp.broadcast_to(xr[edge:edge + 1, :], (8, L)), jnp.broadcast_to(xi[edge:edge + 1, :], (8, L))
        re_ref[rows, lanes] = jnp.concatenate([t[0] for t in tiles], axis=0)
        im_ref[rows, lanes] = jnp.concatenate([t[1] for t in tiles], axis=0)
        return (cr, ci) if extra is None else (cr, ci) + extra(first, tiles, carry[2:])

    assert n_chunks % U == 0
    z = jnp.zeros((8, L), F32)
    init = (z, z) if extra is None else (z, z, z, z)
    return lax.fori_loop(0, n_groups, body, init)


def _s5_consts(ab_re, ab_im):
    ng = ab_re.shape[0] // GROUPS_PER_STEP
    ar, ai = ab_re.reshape(ng, 1, ST_LANES), ab_im.reshape(ng, 1, ST_LANES)

    def cmul(xr, xi, yr, yi):
        return xr * yr - xi * yi, xr * yi + xi * yr

    def build(ar, ai, reverse):
        pw = [(ar, ai)]
        for _ in range(7):
            pw.append(cmul(*pw[-1], ar, ai))
        row = jnp.arange(8).reshape(1, 8, 1)
        tiles = []
        for k in (1, 2, 4):
            keep = (row <= 7 - k) if reverse else (row >= k)
            tiles += [jnp.where(keep, pw[k - 1][0], 0.0), jnp.where(keep, pw[k - 1][1], 0.0)]
        order = [7 - r for r in range(8)] if reverse else list(range(8))
        tiles += [jnp.concatenate([pw[o][0] for o in order], axis=1), jnp.concatenate([pw[o][1] for o in order], axis=1)]
        return jnp.stack([jnp.broadcast_to(t, (ng, 8, ST_LANES)) for t in tiles], axis=1)

    return build(ar, ai, False), build(ar, -ai, True)


def _s5_blockdiag(bb_re, bb_im, c_re, c_im):
    G = bb_re.shape[0]
    ng = G // GROUPS_PER_STEP
    eye = jnp.eye(GROUPS_PER_STEP, dtype=F32)

    def wb(bb):
        return jnp.einsum("bgph,gk->bghkp", bb.reshape(ng, GROUPS_PER_STEP, S5_STATE, S5_GROUP), eye).reshape(ng, U_LANES, ST_LANES)

    def wc(cc):
        return jnp.einsum("bghp,gk->bkpgh", cc.reshape(ng, GROUPS_PER_STEP, S5_GROUP, S5_STATE), eye).reshape(ng, ST_LANES, U_LANES)

    Wb = jnp.concatenate([wb(bb_re), wb(bb_im)], axis=2).astype(BF16)
    Wc = jnp.concatenate([wc(c_re), -wc(c_im)], axis=1).astype(BF16)
    return Wb, Wc


def _s5_unblock(dWb, dWc):
    ng = dWb.shape[0]
    eye = jnp.eye(GROUPS_PER_STEP, dtype=F32)

    def ub(w):
        return jnp.einsum("bghkp,gk->bgph", w.reshape(ng, GROUPS_PER_STEP, S5_GROUP, GROUPS_PER_STEP, S5_STATE), eye).reshape(-1, S5_STATE, S5_GROUP)

    def uc(w):
        return jnp.einsum("bkpgh,gk->bghp", w.reshape(ng, GROUPS_PER_STEP, S5_STATE, GROUPS_PER_STEP, S5_GROUP), eye).reshape(-1, S5_GROUP, S5_STATE)

    return ub(dWb[:, :, :ST_LANES]), ub(dWb[:, :, ST_LANES:]), uc(dWc[:, :ST_LANES, :]), -uc(dWc[:, ST_LANES:, :])


def _s5_disc(a_re, a_im, log_dt, b_re, b_im):
    dt = jnp.exp(log_dt)[:, None]
    mag = jnp.exp(a_re * dt)
    ab_re = mag * jnp.cos(a_im * dt)
    ab_im = mag * jnp.sin(a_im * dt)
    den = a_re * a_re + a_im * a_im
    nr, ni = ab_re - 1, ab_im
    f_re = (nr * a_re + ni * a_im) / den
    f_im = (ni * a_re - nr * a_im) / den
    bb_re = f_re[..., None] * b_re - f_im[..., None] * b_im
    bb_im = f_re[..., None] * b_im + f_im[..., None] * b_re
    return ab_re, ab_im, bb_re, bb_im


ROW_CHUNK = 512


def _s5_fwd(u, Wb, Wc, cf, d, xsrc, *, n_ex, name):
    T, D = u.shape
    S = T // n_ex
    ng = D // U_LANES
    rc = min(ROW_CHUNK, S)

    def body(u_ref, wb_ref, wc_ref, cf_ref, d_ref, xsrc_ref, y_ref, gy_ref, st_ref, xout_ref, re_s, im_s, *sems):
        step = pl.program_id(0) * ng + pl.program_id(1)
        exch = _ChipExchange(xsrc_ref, xout_ref, *sems, scatter=False)

        @pl.when(step == 0)
        def _():
            exch.start()

        for r in range(S // rc):
            rows = pl.ds(r * rc, rc)
            bu = jnp.dot(u_ref[rows, :].astype(BF16), wb_ref[...], preferred_element_type=F32)
            re_s[rows, :] = bu[:, :ST_LANES]
            im_s[rows, :] = bu[:, ST_LANES:]
        for l0 in range(0, ST_LANES, SCAN_LANES):
            _scan_tiles(re_s, im_s, cf_ref, l0, S // 8, False)
        for r in range(S // rc):
            rows = pl.ds(r * rc, rc)
            st = jnp.concatenate([re_s[rows, :], im_s[rows, :]], axis=1).astype(BF16)
            st_ref[rows, :] = st
            y = jnp.dot(st, wc_ref[...], preferred_element_type=F32) + d_ref[...] * u_ref[rows, :]
            y_ref[rows, :] = y
            gy_ref[rows, :] = _gelu(y).astype(BF16)

        @pl.when(step == n_ex * ng - 1)
        def _():
            exch.wait()

    return pl.pallas_call(
        body, name=name, grid=(n_ex, ng),
        in_specs=[pl.BlockSpec((S, U_LANES), lambda e, g: (e, g)),
                  pl.BlockSpec((None, U_LANES, 2 * ST_LANES), lambda e, g: (g, 0, 0)),
                  pl.BlockSpec((None, 2 * ST_LANES, U_LANES), lambda e, g: (g, 0, 0)),
                  pl.BlockSpec((None, 8, 8, ST_LANES), lambda e, g: (g, 0, 0, 0)),
                  pl.BlockSpec((1, U_LANES), lambda e, g: (0, g)), ANY],
        out_specs=[pl.BlockSpec((S, U_LANES), lambda e, g: (e, g))] * 2 + [pl.BlockSpec((S, 2 * ST_LANES), lambda e, g: (e, g)), ANY],
        out_shape=[jax.ShapeDtypeStruct((T, D), F32), jax.ShapeDtypeStruct((T, D), BF16),
                   jax.ShapeDtypeStruct((T, ng * 2 * ST_LANES), BF16), _ChipExchange.out_shape(xsrc, False)],
        scratch_shapes=[pltpu.VMEM((S, ST_LANES), F32)] * 2 + _ChipExchange.SCRATCH,
        compiler_params=_cp(("arbitrary", "arbitrary")),
    )(u, Wb, Wc, cf, d, xsrc)


def _s5_bwd(u, y, dgy, st, Wb, Wc, cr, d, xsrc, *, n_ex, name):
    T, D = u.shape
    S = T // n_ex
    ng = D // U_LANES
    rc = min(ROW_CHUNK, S)
    nch = S // 8
    grp = 8 * SCAN_UNROLL
    assert grp % 16 == 0

    def body(u_ref, y_ref, dgy_ref, st_ref, wb_ref, wc_ref, cr_ref, d_ref, xsrc_ref,
             du_ref, dwb_ref, dwc_ref, dab_ref, dd_ref, xout_ref, gr_s, gi_s, dy_s, *sems):
        e = pl.program_id(1)
        step = pl.program_id(0) * n_ex + e
        exch = _ChipExchange(xsrc_ref, xout_ref, *sems, scatter=True)

        @pl.when(step == 0)
        def _():
            exch.start()

        @pl.when(e == 0)
        def _():
            dwb_ref[...] = jnp.zeros_like(dwb_ref)
            dwc_ref[...] = jnp.zeros_like(dwc_ref)
            dab_ref[...] = jnp.zeros_like(dab_ref)
            dd_ref[...] = jnp.zeros_like(dd_ref)

        dd = jnp.zeros((1, U_LANES), F32)
        for r in range(S // rc):
            rows = pl.ds(r * rc, rc)
            ut = u_ref[rows, :]
            dy = dgy_ref[rows, :].astype(F32) * _gelu_grad(y_ref[rows, :])
            dy_s[rows, :] = dy
            dd = dd + _csum(dy * ut)
            go = lax.dot_general(dy.astype(BF16), wc_ref[...], (((1,), (1,)), ((), ())), preferred_element_type=F32)
            gr_s[rows, :] = go[:, :ST_LANES]
            gi_s[rows, :] = go[:, ST_LANES:]
        dd_ref[0:1, :] += dd
        row0 = lax.broadcasted_iota(jnp.int32, (8, SCAN_LANES), 0) == 0
        for l0 in range(0, ST_LANES, SCAN_LANES):
            lanes = pl.ds(l0, SCAN_LANES)

            def dab_group(first, tiles, acc, l0=l0):
                def states(r0, n, lane0):
                    return st_ref[pl.ds(pl.multiple_of(r0, 16), n), pl.ds(lane0, SCAN_LANES)].astype(F32)
                r0 = first * 8
                cur = states(r0, grp, l0), states(r0, grp, ST_LANES + l0)
                live = (first > 0).astype(F32)
                p0 = jnp.maximum(r0 - 16, 0)
                before = [states(p0, 16, l0)[8:16, :] * live, states(p0, 16, ST_LANES + l0)[8:16, :] * live]
                a_re, a_im = acc
                for t, (gr, gi) in enumerate(tiles):
                    here = [c[8 * t:8 * t + 8, :] for c in cur]
                    sr, si = [jnp.where(row0, pltpu.roll(b, 1, 0), pltpu.roll(h, 1, 0)) for b, h in zip(before, here)]
                    a_re, a_im = a_re + gr * sr + gi * si, a_im + gi * sr - gr * si
                    before = here
                return a_re, a_im

            res = _scan_tiles(gr_s, gi_s, cr_ref, l0, nch, True, extra=dab_group)
            dab_ref[0:1, lanes] += _csum(res[2])
            dab_ref[1:2, lanes] += _csum(res[3])
        for r in range(S // rc):
            rows = pl.ds(r * rc, rc)
            st = st_ref[rows, :]
            g = jnp.concatenate([gr_s[rows, :], gi_s[rows, :]], axis=1).astype(BF16)
            dyb = dy_s[rows, :].astype(BF16)
            dwc_ref[...] += lax.dot_general(st, dyb, (((0,), (0,)), ((), ())), preferred_element_type=F32)
            dwb_ref[...] += lax.dot_general(u_ref[rows, :].astype(BF16), g, (((0,), (0,)), ((), ())), preferred_element_type=F32)
            du = lax.dot_general(g, wb_ref[...], (((1,), (1,)), ((), ())), preferred_element_type=F32)
            du_ref[rows, :] = du + d_ref[...] * dy_s[rows, :]

        @pl.when(step == ng * n_ex - 1)
        def _():
            exch.wait()

    return pl.pallas_call(
        body, name=name, grid=(ng, n_ex),
        in_specs=[pl.BlockSpec((S, U_LANES), lambda g, e: (e, g))] * 3 + [
            pl.BlockSpec((S, 2 * ST_LANES), lambda g, e: (e, g)),
            pl.BlockSpec((None, U_LANES, 2 * ST_LANES), lambda g, e: (g, 0, 0)),
            pl.BlockSpec((None, 2 * ST_LANES, U_LANES), lambda g, e: (g, 0, 0)),
            pl.BlockSpec((None, 8, 8, ST_LANES), lambda g, e: (g, 0, 0, 0)),
            pl.BlockSpec((1, U_LANES), lambda g, e: (0, g)), ANY],
        out_specs=[pl.BlockSpec((S, U_LANES), lambda g, e: (e, g)),
                   pl.BlockSpec((None, U_LANES, 2 * ST_LANES), lambda g, e: (g, 0, 0)),
                   pl.BlockSpec((None, 2 * ST_LANES, U_LANES), lambda g, e: (g, 0, 0)),
                   pl.BlockSpec((None, 8, ST_LANES), lambda g, e: (g, 0, 0)),
                   pl.BlockSpec((None, 8, U_LANES), lambda g, e: (g, 0, 0)), ANY],
        out_shape=[jax.ShapeDtypeStruct((T, D), F32),
                   jax.ShapeDtypeStruct((ng, U_LANES, 2 * ST_LANES), F32),
                   jax.ShapeDtypeStruct((ng, 2 * ST_LANES, U_LANES), F32),
                   jax.ShapeDtypeStruct((ng, 8, ST_LANES), F32),
                   jax.ShapeDtypeStruct((ng, 8, U_LANES), F32), _ChipExchange.out_shape(xsrc, True)],
        scratch_shapes=[pltpu.VMEM((S, ST_LANES), F32)] * 2 + [pltpu.VMEM((S, U_LANES), F32)] + _ChipExchange.SCRATCH,
        compiler_params=_cp(("arbitrary", "arbitrary")),
    )(u, y, dgy, st, Wb, Wc, cr, d, xsrc)


TQ = 256
KW = 512
SUB = 128


def _head_masks():
    lane = lax.broadcasted_iota(jnp.int32, (1, 2 * HEAD_DIM), 1)
    m0 = (lane < HEAD_DIM).astype(F32)
    return m0, 1.0 - m0


def _head_norm(x, g, m0, m1):
    sq = x * x
    r0 = lax.rsqrt(jnp.sum(sq * m0, axis=-1, keepdims=True) / HEAD_DIM + EPS)
    r1 = lax.rsqrt(jnp.sum(sq * m1, axis=-1, keepdims=True) / HEAD_DIM + EPS)
    r = m0 * r0 + m1 * r1
    return x * r, r


def _head_norm_bwd(dy, n, r, g, m0, m1):
    dn = dy * g
    p = dn * n
    mean = (m0 * jnp.sum(p * m0, axis=-1, keepdims=True) + m1 * jnp.sum(p * m1, axis=-1, keepdims=True)) / HEAD_DIM
    return r * (dn - n * mean), _csum(dy * n)


def _pair_matrix(kind):
    r = lax.broadcasted_iota(jnp.int32, (2 * SUB, 2 * SUB), 0)
    c = lax.broadcasted_iota(jnp.int32, (2 * SUB, 2 * SUB), 1)
    same = (r < SUB) == (c < SUB)
    rel = {"after": r > c, "upto": r <= c, "before": r < c}[kind]
    return jnp.logical_and(same, rel).astype(BF16)


def _block_sums(x, mat, carry, reverse, terms=2):
    hi = x.astype(BF16)
    lo = (x - hi.astype(F32)).astype(BF16) if terms == 2 else None
    npair = KW // (2 * SUB)
    parts = [None] * (2 * npair)
    for p in (range(npair - 1, -1, -1) if reverse else range(npair)):
        sl = slice(2 * SUB * p, 2 * SUB * (p + 1))
        loc = jnp.dot(hi[:, sl], mat, preferred_element_type=F32)
        if terms == 2:
            loc = loc + jnp.dot(lo[:, sl], mat, preferred_element_type=F32)
        for b in ((1, 0) if reverse else (0, 1)):
            k = 2 * p + b
            parts[k] = loc[:, SUB * b:SUB * (b + 1)] + carry
            carry = carry + jnp.sum(x[:, SUB * k:SUB * (k + 1)], axis=-1, keepdims=True)
    return jnp.concatenate(parts, axis=1), carry


def _sb_logits(z, mask):
    lp = jnp.minimum(z, 0.0) - jnp.log(1.0 + jnp.exp(-jnp.abs(z)))
    lf = lp - z
    if mask is not None:
        lf = jnp.where(mask, lf, 0.0)
    return lp, lf


def _causal_mask(row0, col0):
    r = row0 + lax.broadcasted_iota(jnp.int32, (TQ, KW), 0)
    c = col0 + lax.broadcasted_iota(jnp.int32, (TQ, KW), 1)
    return c < r


def _transposed_windows(x, ref):
    for w in range(x.shape[0] // KW):
        ref[w] = x[w * KW:(w + 1) * KW, :].T.astype(BF16)


def _attn_fwd(q, kv, qg, kg, xsrc, *, n_ex, name):
    T, D = q.shape
    S = T // n_ex
    nhp = D // (2 * HEAD_DIM)
    nq = S // TQ
    scale = 1.0 / math.sqrt(HEAD_DIM)

    def body(q_ref, k_ref, v_ref, qg_ref, kg_ref, xsrc_ref, o_ref, tot_ref, xout_ref, kT_s, qm_s, vm_s, *sems):
        step = pl.program_id(0) * nhp + pl.program_id(1)
        exch = _ChipExchange(xsrc_ref, xout_ref, *sems, scatter=False)

        @pl.when(step == 0)
        def _():
            exch.start()

        m0, m1 = _head_masks()
        qn, _ = _head_norm(q_ref[...], None, m0, m1)
        qn = qn * (qg_ref[...] * scale)
        kn, _ = _head_norm(k_ref[...], None, m0, m1)
        _transposed_windows(kn * kg_ref[...], kT_s)
        v = v_ref[...]
        for h, m in enumerate((m0, m1)):
            qm_s[h] = (qn * m).astype(BF16)
            vm_s[h] = (v * m).astype(BF16)
        u_after = _pair_matrix("after")

        def scores(rows, win):
            return tuple(jnp.dot(qm_s[h, rows, :], kT_s[win], preferred_element_type=F32) for h in range(2))

        def window(rows, win, zs, st, mask):
            keys = pl.ds(pl.multiple_of(win * KW, KW), KW)
            lg = [_sb_logits(zs[h], mask) for h in range(2)]
            sums = [_block_sums(lg[h][1], u_after, st[2 * h], True) for h in range(2)]
            out = ()
            for h in range(2):
                w = jnp.exp(lg[h][0] + sums[h][0])
                if mask is not None:
                    w = jnp.where(mask, w, 0.0)
                out += (sums[h][1], st[2 * h + 1] + jnp.dot(w.astype(BF16), vm_s[h, keys, :], preferred_element_type=F32))
            return out

        def qtile(iq, _):
            rows = pl.ds(pl.multiple_of(iq * TQ, TQ), TQ)
            last = (iq * TQ) // KW
            mask = _causal_mask(iq * TQ, last * KW)
            z1, zq = jnp.zeros((TQ, 1), F32), jnp.zeros((TQ, 2 * HEAD_DIM), F32)
            st = window(rows, last, scores(rows, last), (z1, zq, z1, zq), mask)
            st = lax.fori_loop(0, last, lambda jj, st: window(rows, last - 1 - jj, scores(rows, last - 1 - jj), st, None), st)
            o_ref[rows, :] = st[1] + st[3]
            tot_ref[rows, :] = st[0] * m0 + st[2] * m1
            return 0

        lax.fori_loop(0, nq, qtile, 0)

        @pl.when(step == n_ex * nhp - 1)
        def _():
            exch.wait()

    assert S % KW == 0 and KW % TQ == 0
    nwin = S // KW
    blk = (S, 2 * HEAD_DIM)
    return pl.pallas_call(
        body, name=name, grid=(n_ex, nhp),
        in_specs=[pl.BlockSpec(blk, lambda e, h: (e, h)), pl.BlockSpec(blk, lambda e, h: (e, h)),
                  pl.BlockSpec(blk, lambda e, h: (e, h + nhp)),
                  pl.BlockSpec((1, 2 * HEAD_DIM), lambda e, h: (0, 0)), pl.BlockSpec((1, 2 * HEAD_DIM), lambda e, h: (0, 0)), ANY],
        out_specs=[pl.BlockSpec(blk, lambda e, h: (e, h))] * 2 + [ANY],
        out_shape=[jax.ShapeDtypeStruct((T, D), F32)] * 2 + [_ChipExchange.out_shape(xsrc, False)],
        scratch_shapes=[pltpu.VMEM((nwin, 2 * HEAD_DIM, KW), BF16), pltpu.VMEM((2,) + blk, BF16), pltpu.VMEM((2,) + blk, BF16)]
        + _ChipExchange.SCRATCH,
        compiler_params=_cp(("arbitrary", "arbitrary")),
    )(q, kv, kv, qg, kg, xsrc)


def _attn_bwd(q, kv, tot, do, qg, kg, *, n_ex, name):
    T, D = q.shape
    S = T // n_ex
    nhp = D // (2 * HEAD_DIM)
    nq = S // TQ
    scale = 1.0 / math.sqrt(HEAD_DIM)

    def body(q_ref, k_ref, v_ref, tot_ref, do_ref, qg_ref, kg_ref, dq_ref, dk_ref, dv_ref, dqg_ref, dkg_ref,
             kT_s, vT_s, km_s, qm_s, dom_s, dqn_s, dkT_s, dvT_s):
        m0, m1 = _head_masks()
        qn, qr = _head_norm(q_ref[...], None, m0, m1)
        kn, kr = _head_norm(k_ref[...], None, m0, m1)
        qs = qn * (qg_ref[...] * scale)
        kk = kn * kg_ref[...]
        _transposed_windows(kk, kT_s)
        _transposed_windows(v_ref[...], vT_s)
        do = do_ref[...]
        for h, m in enumerate((m0, m1)):
            qm_s[h] = (qs * m).astype(BF16)
            km_s[h] = (kk * m).astype(BF16)
            dom_s[h] = (do * m).astype(BF16)
        dkT_s[...] = jnp.zeros_like(dkT_s)
        dvT_s[...] = jnp.zeros_like(dvT_s)
        u_upto, u_before = _pair_matrix("upto"), _pair_matrix("before")

        def both(inv, win, st, mask):
            keys = pl.ds(pl.multiple_of(win * KW, KW), KW)
            lg = [_sb_logits(jnp.dot(inv[h][0], kT_s[win], preferred_element_type=F32), mask) for h in range(2)]
            dw = [jnp.dot(inv[h][2], vT_s[win], preferred_element_type=F32) for h in range(2)]
            s_lf = [_block_sums(lg[h][1], u_upto, st[3 * h], False) for h in range(2)]
            ws, ews = [], []
            for h in range(2):
                w = jnp.exp(lg[h][0] + (inv[h][4] - s_lf[h][0]))
                if mask is not None:
                    w = jnp.where(mask, w, 0.0)
                ws.append(w)
                ews.append(dw[h] * w)
            s_e = [_block_sums(ews[h], u_before, st[3 * h + 1], False, terms=1) for h in range(2)]
            out, dk, dv = (), None, None
            for h in range(2):
                sig = jnp.exp(lg[h][0])
                dz = ews[h] * (1.0 - sig) - s_e[h][0] * sig
                if mask is not None:
                    dz = jnp.where(mask, dz, 0.0)
                dzb = dz.astype(BF16)
                out += (s_lf[h][1], s_e[h][1], st[3 * h + 2] + jnp.dot(dzb, km_s[h, keys, :], preferred_element_type=F32))
                dkh = jnp.dot(inv[h][1], dzb, preferred_element_type=F32)
                dvh = jnp.dot(inv[h][3], ws[h].astype(BF16), preferred_element_type=F32)
                dk, dv = (dkh, dvh) if h == 0 else (dk + dkh, dv + dvh)
            dkT_s[win] += dk
            dvT_s[win] += dv
            return out

        def qtile(iq, _):
            rows = pl.ds(pl.multiple_of(iq * TQ, TQ), TQ)
            last = (iq * TQ) // KW
            mask = _causal_mask(iq * TQ, last * KW)
            tt = tot_ref[rows, :]
            inv = []
            for h, m in enumerate((m0, m1)):
                qh, doh = qm_s[h, rows, :], dom_s[h, rows, :]
                total = jnp.sum(tt * m, axis=-1, keepdims=True) * (1.0 / HEAD_DIM)
                inv.append((qh, qh.astype(F32).T.astype(BF16), doh, doh.astype(F32).T.astype(BF16), total))

            z1, zq = jnp.zeros((TQ, 1), F32), jnp.zeros((TQ, 2 * HEAD_DIM), F32)
            st = lax.fori_loop(0, last, lambda win, st: both(inv, win, st, None), (z1, z1, zq, z1, z1, zq))
            st = both(inv, last, st, mask)
            dqn_s[rows, :] = st[2] + st[5]
            return 0

        lax.fori_loop(0, nq, qtile, 0)
        dkn = jnp.concatenate([dkT_s[w].T for w in range(nwin)], axis=0)
        dq, dqg = _head_norm_bwd(dqn_s[...] * scale, qn, qr, qg_ref[...], m0, m1)
        dk, dkg = _head_norm_bwd(dkn, kn, kr, kg_ref[...], m0, m1)
        dq_ref[...] = dq
        dk_ref[...] = dk
        dv_ref[...] = jnp.concatenate([dvT_s[w].T for w in range(nwin)], axis=0)
        dqg_ref[...] = dqg
        dkg_ref[...] = dkg

    assert S % KW == 0 and KW % TQ == 0
    nwin = S // KW
    blk = (S, 2 * HEAD_DIM)
    tblk = (nwin, 2 * HEAD_DIM, KW)
    gblk = (None, None, 1, 2 * HEAD_DIM)
    dq, dk, dv, dqg, dkg = pl.pallas_call(
        body, name=name, grid=(n_ex, nhp),
        in_specs=[pl.BlockSpec(blk, lambda e, h: (e, h)), pl.BlockSpec(blk, lambda e, h: (e, h)),
                  pl.BlockSpec(blk, lambda e, h: (e, h + nhp)),
                  pl.BlockSpec(blk, lambda e, h: (e, h)), pl.BlockSpec(blk, lambda e, h: (e, h)),
                  pl.BlockSpec((1, 2 * HEAD_DIM), lambda e, h: (0, 0)), pl.BlockSpec((1, 2 * HEAD_DIM), lambda e, h: (0, 0))],
        out_specs=[pl.BlockSpec(blk, lambda e, h: (e, h))] * 3 + [pl.BlockSpec(gblk, lambda e, h: (e, h, 0, 0))] * 2,
        out_shape=[jax.ShapeDtypeStruct((T, D), F32)] * 3 + [jax.ShapeDtypeStruct((n_ex, nhp, 1, 2 * HEAD_DIM), F32)] * 2,
        scratch_shapes=[pltpu.VMEM(tblk, BF16), pltpu.VMEM(tblk, BF16),
                        pltpu.VMEM((2,) + blk, BF16), pltpu.VMEM((2,) + blk, BF16), pltpu.VMEM((2,) + blk, BF16),
                        pltpu.VMEM(blk, F32), pltpu.VMEM(tblk, F32), pltpu.VMEM(tblk, F32)],
        compiler_params=_cp(("parallel", "parallel")),
    )(q, kv, kv, tot, do, qg, kg)
    return dq, dk, dv, dqg, dkg


def _place():
    return lax.axis_index("x"), lax.axis_index("y"), lax.axis_index("c")


def _all_gather8(x_shard, *, name):
    m_per, n = x_shard.shape

    def body(x_ref, out_ref, send_sems, recv_sems, local_sem):
        x, y, c = _place()
        me, sibling = (x, y, c), (x, y, 1 - c)
        chips = [(1 - x, y), (x, 1 - y), (1 - x, 1 - y)]

        def rows(px, py, pc):
            return out_ref.at[pl.ds((4 * px + 2 * py + pc) * m_per, m_per), :]

        def copy(k, block, to, src=None):
            return pltpu.make_async_remote_copy(
                src_ref=rows(*block) if src is None else src, dst_ref=rows(*block),
                send_sem=send_sems.at[k], recv_sem=recv_sems.at[k], device_id=to, device_id_type=MESH)

        mine = pltpu.make_async_copy(x_ref, rows(*me), local_sem)
        mine.start()
        first = [copy(0, me, sibling, src=x_ref)]
        first += [copy(1 + j, me, (*chip, c), src=x_ref) for j, chip in enumerate(chips)]
        for cp in first:
            cp.start()
        passed = [copy(4 + j, (*chip, c), sibling) for j, chip in enumerate(chips)]
        for j, chip in enumerate(chips):
            copy(1 + j, (*chip, c), me).wait_recv()
            passed[j].start()
        copy(0, sibling, me).wait_recv()
        for j, chip in enumerate(chips):
            copy(4 + j, (*chip, 1 - c), me).wait_recv()
        for cp in first + passed:
            cp.wait_send()
        mine.wait()

    return pl.pallas_call(
        body, name=name, out_shape=jax.ShapeDtypeStruct((8 * m_per, n), x_shard.dtype),
        in_specs=[pl.BlockSpec(memory_space=pltpu.VMEM)], out_specs=pl.BlockSpec(memory_space=pltpu.VMEM),
        scratch_shapes=[pltpu.SemaphoreType.DMA((7,)), pltpu.SemaphoreType.DMA((7,)), pltpu.SemaphoreType.DMA],
        compiler_params=pltpu.CompilerParams(vmem_limit_bytes=VMEM_LIMIT),
    )(x_shard)


def _sibling_sum_half(x, *, name):
    R, C = x.shape
    half = R // 2

    def body(x_ref, o_ref, theirs, send_sem, recv_sem):
        px, py, pc = _place()
        cp = pltpu.make_async_remote_copy(src_ref=x_ref, dst_ref=theirs, send_sem=send_sem, recv_sem=recv_sem,
                                          device_id=(px, py, 1 - pc), device_id_type=MESH)
        cp.start()
        cp.wait()
        rows = pl.ds(pl.multiple_of(pc * half, 8), half)
        o_ref[...] = x_ref[rows, :] + theirs[rows, :]

    return pl.pallas_call(
        body, name=name, out_shape=jax.ShapeDtypeStruct((half, C), x.dtype),
        in_specs=[pl.BlockSpec(memory_space=pltpu.VMEM)], out_specs=pl.BlockSpec(memory_space=pltpu.VMEM),
        scratch_shapes=[pltpu.VMEM((R, C), x.dtype), pltpu.SemaphoreType.DMA, pltpu.SemaphoreType.DMA],
        compiler_params=pltpu.CompilerParams(vmem_limit_bytes=VMEM_LIMIT),
    )(x)


def _sum_blocks(x, n, *, name):
    R = x.shape[0] // n

    def body(x_ref, o_ref):
        acc = x_ref[pl.ds(0, R), :]
        for k in range(1, n):
            acc = acc + x_ref[pl.ds(k * R, R), :]
        o_ref[...] = acc

    return pl.pallas_call(body, name=name, out_shape=jax.ShapeDtypeStruct((R, x.shape[1]), x.dtype),
                          compiler_params=pltpu.CompilerParams(vmem_limit_bytes=VMEM_LIMIT))(x)


def _colsum(x, *, name):
    def body(x_ref, o_ref):
        o_ref[...] = jnp.sum(x_ref[...], axis=0, keepdims=True)
    return pl.pallas_call(body, name=name, out_shape=jax.ShapeDtypeStruct((1, x.shape[1]), x.dtype))(x)


ANY = pl.BlockSpec(memory_space=pl.ANY)


class _ChipExchange:
    SCRATCH = [pltpu.SemaphoreType.DMA((3,)), pltpu.SemaphoreType.DMA((3,)), pltpu.SemaphoreType.DMA]

    @staticmethod
    def out_shape(src, scatter):
        return jax.ShapeDtypeStruct(((4,) + tuple(src.shape[1:])) if scatter else ((4, 2) + tuple(src.shape[1:])), src.dtype)

    def __init__(self, src_ref, out_ref, send_sems, recv_sems, local_sem, scatter):
        x, y, c = _place()
        myj = 2 * x + y
        chips = [(1 - x, y), (x, 1 - y), (1 - x, 1 - y)]

        def slot(j):
            return out_ref.at[j] if scatter else out_ref.at[j, c]

        def piece(j):
            return src_ref.at[j] if scatter else src_ref.at[c]

        self.mine = pltpu.make_async_copy(piece(myj), slot(myj), local_sem)
        self.sends = [pltpu.make_async_remote_copy(
            src_ref=piece(2 * cx + cy), dst_ref=slot(myj), send_sem=send_sems.at[k], recv_sem=recv_sems.at[k],
            device_id=(cx, cy, c), device_id_type=MESH) for k, (cx, cy) in enumerate(chips)]
        self.recvs = [pltpu.make_async_remote_copy(
            src_ref=slot(2 * cx + cy), dst_ref=slot(2 * cx + cy), send_sem=send_sems.at[k], recv_sem=recv_sems.at[k],
            device_id=(cx, cy, c), device_id_type=MESH) for k, (cx, cy) in enumerate(chips)]

    def start(self):
        self.mine.start()
        for cp in self.sends:
            cp.start()

    def wait(self):
        for cp in self.recvs:
            cp.wait_recv()
        for cp in self.sends:
            cp.wait_send()
        self.mine.wait()


def _sibling_fill(buf, *, axis, name):
    def half(ref, h):
        return ref.at[h] if axis == 0 else ref.at[:, h]

    def body(in_ref, out_ref, send_sem, recv_sem):
        x, y, c = _place()
        cp = pltpu.make_async_remote_copy(src_ref=half(out_ref, c), dst_ref=half(out_ref, c), send_sem=send_sem, recv_sem=recv_sem,
                                          device_id=(x, y, 1 - c), device_id_type=MESH)
        cp.start()
        pltpu.make_async_remote_copy(src_ref=half(out_ref, 1 - c), dst_ref=half(out_ref, 1 - c), send_sem=send_sem, recv_sem=recv_sem,
                                     device_id=(x, y, 1 - c), device_id_type=MESH).wait_recv()
        cp.wait_send()

    return pl.pallas_call(
        body, name=name, out_shape=jax.ShapeDtypeStruct(buf.shape, buf.dtype), in_specs=[ANY], out_specs=ANY,
        input_output_aliases={0: 0}, scratch_shapes=[pltpu.SemaphoreType.DMA, pltpu.SemaphoreType.DMA],
    )(buf)


def _sibling_swap_half(g, *, name):
    def body(g_ref, out_ref, send_sem, recv_sem):
        x, y, c = _place()
        cp = pltpu.make_async_remote_copy(src_ref=g_ref.at[:, 1 - c], dst_ref=out_ref, send_sem=send_sem, recv_sem=recv_sem,
                                          device_id=(x, y, 1 - c), device_id_type=MESH)
        cp.start()
        cp.wait()

    return pl.pallas_call(
        body, name=name, out_shape=jax.ShapeDtypeStruct((g.shape[0],) + g.shape[2:], g.dtype), in_specs=[ANY], out_specs=ANY,
        scratch_shapes=[pltpu.SemaphoreType.DMA, pltpu.SemaphoreType.DMA],
    )(g)


def _add_my_half(g, b, cidx, *, name, tr=256):
    n, _, R, C = g.shape
    tr = math.gcd(tr, R)

    def body(c_ref, g_ref, b_ref, o_ref):
        o_ref[...] = (g_ref[...] + b_ref[...]).astype(o_ref.dtype)

    return pl.pallas_call(
        body, name=name, out_shape=jax.ShapeDtypeStruct((n, R, C), BF16),
        grid_spec=pltpu.PrefetchScalarGridSpec(
            num_scalar_prefetch=1, grid=(n, R // tr),
            in_specs=[pl.BlockSpec((None, None, tr, C), lambda j, i, c: (j, c[0], i, 0)),
                      pl.BlockSpec((None, tr, C), lambda j, i, c: (j, i, 0))],
            out_specs=pl.BlockSpec((None, tr, C), lambda j, i, c: (j, i, 0))),
        compiler_params=_cp(("parallel", "parallel")),
    )(cidx, g, b)


def _sum4_into_half(q, cidx, *, name, tr=256):
    _, R, C = q.shape
    tr = math.gcd(tr, R)

    def body(c_ref, q_ref, o_ref):
        o_ref[...] = ((q_ref[0].astype(F32) + q_ref[1].astype(F32)) + q_ref[2].astype(F32)) + q_ref[3].astype(F32)

    return pl.pallas_call(
        body, name=name, out_shape=jax.ShapeDtypeStruct((2, R, C), F32),
        grid_spec=pltpu.PrefetchScalarGridSpec(
            num_scalar_prefetch=1, grid=(R // tr,),
            in_specs=[pl.BlockSpec((4, tr, C), lambda i, c: (0, i, 0))],
            out_specs=pl.BlockSpec((None, tr, C), lambda i, c: (c[0], i, 0))),
        compiler_params=_cp(("parallel",)),
    )(cidx, q)


def _pack_rows(parts, width=1024):
    rows, spans, r0 = [], [], 0
    for p in parts:
        n = p.size
        nr = 8 * (-(-n // (8 * width)))
        flat = p.reshape(-1)
        if nr * width != n:
            flat = jnp.pad(flat, (0, nr * width - n))
        rows.append(flat.reshape(nr, width))
        spans.append((r0, nr, n, p.shape))
        r0 += nr
    return jnp.concatenate(rows, axis=0), spans


def _unpack_rows(buf, spans):
    return [buf[r0:r0 + nr].reshape(-1)[:n].reshape(shape) for (r0, nr, n, shape) in spans]


def kernel(x, c, ada_w, ada_b, mix_norm_g, mlp_norm_g, mlp_w1, mlp_w2, s5_a_re, s5_a_im, s5_log_dt, s5_b_re, s5_b_im, s5_c_re, s5_c_im, s5_d, s5_w_glu, kv_ada_w, kv_ada_b, kv_norm_g, w_kv, k_norm_g, sb_w_q, q_norm_g, sb_w_o, loss_target, m_ada_w, m_ada_b, m_mix_norm_g, m_mlp_norm_g, m_mlp_w1, m_mlp_w2, m_s5_a_re, m_s5_a_im, m_s5_log_dt, m_s5_b_re, m_s5_b_im, m_s5_c_re, m_s5_c_im, m_s5_d, m_s5_w_glu, m_kv_ada_w, m_kv_ada_b, m_kv_norm_g, m_w_kv, m_k_norm_g, m_sb_w_q, m_q_norm_g, m_sb_w_o, v_ada_w, v_ada_b, v_mix_norm_g, v_mlp_norm_g, v_mlp_w1, v_mlp_w2, v_s5_a_re, v_s5_a_im, v_s5_log_dt, v_s5_b_re, v_s5_b_im, v_s5_c_re, v_s5_c_im, v_s5_d, v_s5_w_glu, v_kv_ada_w, v_kv_ada_b, v_kv_norm_g, v_w_kv, v_k_norm_g, v_sb_w_q, v_q_norm_g, v_sb_w_o):
    E, S, D = x.shape
    T = E * S
    FF = 4 * D
    NB = 8 * E
    px, py, pc = _place()
    chip = 2 * px + py
    dev = 4 * px + 2 * py + pc
    cidx = jnp.reshape(pc, (1,)).astype(jnp.int32)
    x0 = x.reshape(T, D)
    tgt = loss_target.reshape(T, D)

    c_all = _all_gather8(c.reshape(-1, 128), name="ag_c").reshape(NB, D)
    sc_all = (c_all * _sigmoid(c_all)).astype(BF16)
    wa = ada_w.shape[2]
    wk = kv_ada_w.shape[1]
    m_sh = jnp.concatenate([_mm(sc_all, ada_w[0], "nn", name="ada0", tn=256),
                            _mm(sc_all, ada_w[1], "nn", name="ada1", tn=256),
                            _mm(sc_all, kv_ada_w, "nn", name="ada_kv", tn=256)], axis=1)
    m_all = _all_gather8(m_sh, name="ag_m").reshape(4, 2, NB, 2 * wa + wk)[:, 0]
    mods = []
    for l in range(2):
        full = jnp.transpose(m_all[:, :, l * wa:(l + 1) * wa], (1, 0, 2)).reshape(NB, 6 * D) + ada_b[l]
        mine = lax.dynamic_slice_in_dim(full, E * dev, E, axis=0)
        mods.append([mine[:, i * D:(i + 1) * D].reshape(E, 1, D) for i in range(6)])
    full = jnp.transpose(m_all[:, :, 2 * wa:], (1, 0, 2)).reshape(NB, 2 * D) + kv_ada_b
    mine = lax.dynamic_slice_in_dim(full, E * dev, E, axis=0)
    kv_sh, kv_sc = [mine[:, i * D:(i + 1) * D].reshape(E, 1, D) for i in range(2)]

    wpack_a = jnp.concatenate([mlp_w1[0], mlp_w2[0], jnp.concatenate([s5_w_glu[0], w_kv], axis=1), sb_w_q[0]], axis=0).astype(BF16)
    wpack_b = jnp.concatenate([mlp_w1[1], mlp_w2[1], sb_w_o[0]], axis=0).astype(BF16)
    RA, RB = wpack_a.shape[0], wpack_b.shape[0]
    RW = RA + RB

    tm = min(1024, S)
    gbuf = [jax.ShapeDtypeStruct((4, RW, D), F32)]

    def grad_mm(act, dout, kind, roff, nr, c0, nc, name):
        gbuf[0] = _mm(act, dout, "tn", name=name, tk=2048, into=_Sharded(gbuf[0], kind, roff, nr, c0, nc))

    def mlp_fwd(xa, l, mod):
        sh_m, sc_m, g_m = mod[3], mod[4], mod[5]
        h = _norm_mod_fwd(xa, mlp_norm_g[l:l + 1], sh_m, sc_m, n_ex=E, out_dtype=BF16, name=f"mlp_norm{l}")
        r = _mm(h, W1[l], "nn", name=f"mlp_up{l}", out_dtypes=(BF16,), tm=tm,
                epilogue=lambda acc: (jnp.square(jnp.maximum(acc, 0.0)),))
        xb, ff = _mm(r, W2[l], "nn", name=f"mlp_down{l}", out_dtypes=(F32, F32), tm=tm,
                     extras=[_mn_extra(xa), _vec_extra(g_m, S)],
                     epilogue=lambda acc, xat, gt: (xat + gt * acc, acc))
        return xb, (h, r, ff)

    def mlp_bwd(dxb, xa, l, mod, saved):
        sc_m, g_m = mod[4], mod[5]
        h, r, ff = saved
        (dff,), (dgm,) = _rowwise(lambda d, f, g: ([g * d], [_csum(d * f)]), [(dxb, D, 0), (ff, D, 0)], [g_m], [],
                                  [(D, BF16)], [D], n_ex=E, name=f"mlp_gate_bwd{l}")
        da = _mm(dff, W2[l], "nt", name=f"mlp_down_dx{l}", out_dtypes=(BF16,), tm=tm, extras=[_mn_extra(r)],
                 epilogue=lambda acc, rt: (acc * (2.0 * jnp.sqrt(rt.astype(F32))),))
        grad_mm(r, dff, "rows", (2 + l) * D, D, 0, D, f"mlp_down_dw{l}")
        dh = _mm(da, W1[l], "nt", name=f"mlp_up_dx{l}", tm=tm)
        grad_mm(h, da, "cols", l * D, D, 0, D, f"mlp_up_dw{l}")
        (dxa,), (dsh, dsc, dg) = _norm_mod_bwd(xa, dh, dxb, mlp_norm_g[l:l + 1], sc_m, n_ex=E, name=f"mlp_norm_bwd{l}")
        return dxa, (dsh, dsc, dgm), dg

    ab_re, ab_im, bb_re, bb_im = _s5_disc(s5_a_re[0], s5_a_im[0], s5_log_dt[0], s5_b_re[0], s5_b_im[0])
    cf, cr = _s5_consts(ab_re, ab_im)
    Wb, Wc = _s5_blockdiag(bb_re, bb_im, s5_c_re[0], s5_c_im[0])
    ng = D // U_LANES
    nd = s5_d.size // 128
    d_full = _all_gather8(jnp.pad(s5_d.reshape(nd, 128), ((0, 8 - nd), (0, 0))), name="ag_d")
    d_full = d_full.reshape(4, 2, 8, 128)[:, 0, :nd].reshape(1, D)

    mod0, mod1 = mods
    h0 = _norm_mod_fwd(x0, mix_norm_g[0:1], mod0[0], mod0[1], n_ex=E, out_dtype=F32, name="mix_norm0")
    y, gy, s5_states, wfull_a = _s5_fwd(h0, Wb, Wc, cf, d_full, wpack_a.reshape(2, RA // 2, D), n_ex=E, name="s5_fwd")
    wfull_a = _sibling_fill(wfull_a, axis=1, name="wgather_a_d2d").reshape(4, RA, D)

    W1 = [_Sharded(wfull_a, "cols", 0, D, 0, D), None]
    W2 = [_Sharded(wfull_a, "rows", D, D, 0, D), None]
    Wglu = _Sharded(wfull_a, "cols", 2 * D, D, 0, D // 2)
    Wkv = _Sharded(wfull_a, "cols", 2 * D, D, D // 2, D // 2)
    Wq = _Sharded(wfull_a, "rows", 3 * D, D // 4, 0, D)
    vg = _mm(gy, Wglu, "nn", name="glu_up", tm=tm)
    (x1,), _ = _rowwise(lambda v, g, xt, ga: ([xt + ga * (v * _sigmoid(g))], []),
                        [(vg, D, 0), (vg, D, 1), (x0, D, 0)], [mod0[2]], [], [(D, F32)], [], n_ex=E, name="glu_gate")
    x2, saved_mlp0 = mlp_fwd(x1, 0, mod0)

    hkv = _norm_mod_fwd(x2, kv_norm_g.reshape(1, D), kv_sh, kv_sc, n_ex=E, out_dtype=BF16, name="kv_norm")
    kvf = _mm(hkv, Wkv, "nn", name="kv_proj", tm=tm)
    h1 = _norm_mod_fwd(x2, mix_norm_g[1:2], mod1[0], mod1[1], n_ex=E, out_dtype=BF16, name="mix_norm1")
    qf = _mm(h1, Wq, "nn", name="q_proj", tm=tm)
    qg2 = jnp.tile(q_norm_g.reshape(1, HEAD_DIM), (1, 2))
    kg2 = jnp.tile(k_norm_g.reshape(1, HEAD_DIM), (1, 2))
    o, lf_tot, wfull_b = _attn_fwd(qf, kvf, qg2, kg2, wpack_b.reshape(2, RB // 2, D), n_ex=E, name="attn_fwd")
    wfull_b = _sibling_fill(wfull_b, axis=1, name="wgather_b_d2d").reshape(4, RB, D)
    W1[1] = _Sharded(wfull_b, "cols", 0, D, 0, D)
    W2[1] = _Sharded(wfull_b, "rows", D, D, 0, D)
    Wo = _Sharded(wfull_b, "rows", 2 * D, D // 4, 0, D)
    x3, mix1 = _mm(o, Wo, "nn", name="o_proj", out_dtypes=(F32, F32), tm=tm,
                   extras=[_mn_extra(x2), _vec_extra(mod1[2], S)],
                   epilogue=lambda acc, xat, gt: (xat + gt * acc, acc))
    x4, saved_mlp1 = mlp_fwd(x3, 1, mod1)

    (dx4,), (lsum,) = _rowwise(lambda xt, tt: ([(xt - tt) * (1.0 / D)], [_csum(jnp.square(xt - tt)) * (0.5 / D)]),
                               [(x4, D, 0), (tgt, D, 0)], [], [], [(D, F32)], [D], n_ex=E, name="loss")
    loss = lax.psum(jnp.sum(lsum), ("x", "y", "c"))

    dx3, (dsh_m1, dsc_m1, dgm1), dg_mlp1 = mlp_bwd(dx4, x3, 1, mod1, saved_mlp1)
    (dmix1,), (dga1,) = _rowwise(lambda d, f, g: ([g * d], [_csum(d * f)]), [(dx3, D, 0), (mix1, D, 0)], [mod1[2]], [],
                                 [(D, BF16)], [D], n_ex=E, name="attn_gate_bwd")
    do = _mm(dmix1, Wo, "nt", name="o_proj_dx", tm=tm)
    grad_mm(o, dmix1, "rows", 5 * D + D // 4, D // 4, 0, D, "o_proj_dw")
    dq, dk, dv, dqg, dkg = _attn_bwd(qf, kvf, lf_tot, do, qg2, kg2, n_ex=E, name="attn_bwd")
    dh1 = _mm(dq, Wq, "nt", name="q_proj_dx", tm=tm)
    grad_mm(h1, dq, "rows", 5 * D, D // 4, 0, D, "q_proj_dw")
    (dx2,), (dsh_a1, dsc_a1, dg_mix1) = _norm_mod_bwd(x2, dh1, dx3, mix_norm_g[1:2], mod1[1], n_ex=E, name="mix_norm_bwd1")
    dkv = jnp.concatenate([dk, dv], axis=1)
    dhkv = _mm(dkv, Wkv, "nt", name="kv_proj_dx", tm=tm)
    grad_mm(hkv, dkv, "cols", 4 * D, D, D // 2, D // 2, "kv_proj_dw")
    (dx2,), (dkv_sh, dkv_sc, dg_kv) = _norm_mod_bwd(x2, dhkv, dx2, kv_norm_g.reshape(1, D), kv_sc, n_ex=E, name="kv_norm_bwd")

    dx1, (dsh_m0, dsc_m0, dgm0), dg_mlp0 = mlp_bwd(dx2, x1, 0, mod0, saved_mlp0)

    def glu_bwd(v, g, d, ga):
        sg = _sigmoid(g)
        dm = ga * d
        return [jnp.concatenate([dm * sg, dm * v * sg * (1.0 - sg)], axis=1)], [_csum(d * (v * sg))]
    (dvg,), (dga0,) = _rowwise(glu_bwd, [(vg, D, 0), (vg, D, 1), (dx1, D, 0)], [mod0[2]], [], [(2 * D, BF16)], [D],
                               n_ex=E, name="glu_gate_bwd")
    dgy = _mm(dvg, Wglu, "nt", name="glu_up_dx", tm=tm)
    grad_mm(gy, dvg, "cols", 4 * D, D, 0, D // 2, "glu_up_dw")

    gpack = gbuf[0].reshape(4, 2, RW // 2, D)
    theirs = _sibling_swap_half(gpack, name="gscatter_d2d")
    chip_sum = _add_my_half(gpack, theirs, cidx, name="gscatter_add")
    dh0, dWb, dWc, dab, dd, from_chips = _s5_bwd(h0, y, dgy, s5_states, Wb, Wc, cr, d_full, chip_sum, n_ex=E, name="s5_bwd")
    ghalf = _sum4_into_half(from_chips, cidx, name="gscatter_sum")
    gsh = _sibling_fill(ghalf, axis=0, name="gscatter_fill").reshape(RW, D)
    (gx,), (dsh_a0, dsc_a0, dg_mix0) = _norm_mod_bwd(x0, dh0, dx1, mix_norm_g[0:1], mod0[1], n_ex=E, name="mix_norm_bwd0")
    grad_x = gx.reshape(E, S, D)

    dm_mine = jnp.concatenate([t.reshape(E, D) for t in
                               (dsh_a0, dsc_a0, dga0, dsh_m0, dsc_m0, dgm0, dsh_a1, dsc_a1, dga1, dsh_m1, dsc_m1, dgm1, dkv_sh, dkv_sc)], axis=1)
    dm_all = _all_gather8(dm_mine.reshape(8, -1), name="ag_dm").reshape(NB, 14 * D)
    sc_f32 = c_all * _sigmoid(c_all)
    g_ada_w = jnp.stack([_mm(sc_f32, lax.dynamic_slice_in_dim(dm_all, l * 6 * D + chip * wa, wa, axis=1), "tn", name=f"ada_dw{l}", tn=256)
                         for l in range(2)])
    g_kv_ada_w = _mm(sc_f32, lax.dynamic_slice_in_dim(dm_all, 12 * D + chip * wk, wk, axis=1), "tn", name="ada_kv_dw", tn=256)
    db_all = _colsum(dm_all, name="ada_db")
    g_ada_b = db_all[0, :12 * D].reshape(2, 6 * D)
    g_kv_ada_b = db_all[0, 12 * D:]

    dWb_re, dWb_im, dC_re, dC_im = _s5_unblock(dWb, dWc)
    small_parts = [dg_mix0.sum(0), dg_mix1.sum(0), dg_mlp0.sum(0), dg_mlp1.sum(0), dg_kv.sum(0),
                   dqg.sum((0, 1, 2)).reshape(2, HEAD_DIM).sum(0), dkg.sum((0, 1, 2)).reshape(2, HEAD_DIM).sum(0),
                   dd[:, 0, :], dab[:, 0, :], dab[:, 1, :], dWb_re, dWb_im, dC_re, dC_im]
    spack, spans = _pack_rows(small_parts)
    chip_half = _sibling_sum_half(spack, name="small_d2d")
    ssum = _sum_blocks(_all_gather8(chip_half, name="ag_small"), 4, name="sum_small")
    (g_mix0, g_mix1, g_mlp0, g_mlp1, g_kvn, g_qn, g_kn, g_d, g_abr, g_abi, g_bbr, g_bbi, g_cre, g_cim) = _unpack_rows(ssum, spans)
    _, disc_vjp = jax.vjp(_s5_disc, s5_a_re[0], s5_a_im[0], s5_log_dt[0], s5_b_re[0], s5_b_im[0])
    g_are, g_aim, g_ldt, g_bre, g_bim = disc_vjp((g_abr.reshape(ab_re.shape), g_abi.reshape(ab_im.shape), g_bbr, g_bbi))
    g_s5d = lax.dynamic_slice_in_dim(g_d.reshape(1, D), chip * s5_d.shape[1], s5_d.shape[1], axis=1)

    def upd_big(w, m, v, roff, cb, name):
        shape = w.shape
        W = shape[-1]
        d_, m_, v_, g_ = _adamw2d(w.reshape(-1, W), gsh, m.reshape(-1, W), v.reshape(-1, W), name=name, g_roff=roff, g_cb=cb)
        return [t.reshape(shape) for t in (g_, d_, m_, v_)]

    def upd_own(w, g, m, v, name):
        shape = w.shape
        W = shape[-1]
        d_, m_, v_, g_ = _adamw2d(w.reshape(-1, W), g.reshape(-1, W), m.reshape(-1, W), v.reshape(-1, W), name=name)
        return [t.reshape(shape) for t in (g_, d_, m_, v_)]

    res = {}
    res["ada_w"] = upd_own(ada_w, g_ada_w, m_ada_w, v_ada_w, "adam_ada_w")
    res["kv_ada_w"] = upd_own(kv_ada_w, g_kv_ada_w, m_kv_ada_w, v_kv_ada_w, "adam_kv_ada_w")
    res["mlp_w1"] = upd_big(mlp_w1, m_mlp_w1, v_mlp_w1, 0, 0, "adam_w1")
    res["mlp_w2"] = upd_big(mlp_w2, m_mlp_w2, v_mlp_w2, 2 * D, 0, "adam_w2")
    res["s5_w_glu"] = upd_big(s5_w_glu, m_s5_w_glu, v_s5_w_glu, 4 * D, 0, "adam_glu")
    res["w_kv"] = upd_big(w_kv, m_w_kv, v_w_kv, 4 * D, 1, "adam_wkv")
    res["sb_w_q"] = upd_big(sb_w_q, m_sb_w_q, v_sb_w_q, 5 * D, 0, "adam_wq")
    res["sb_w_o"] = upd_big(sb_w_o, m_sb_w_o, v_sb_w_o, 5 * D + D // 4, 0, "adam_wo")

    small = {
        "ada_b": (ada_b, g_ada_b, m_ada_b, v_ada_b),
        "mix_norm_g": (mix_norm_g, jnp.stack([g_mix0, g_mix1]), m_mix_norm_g, v_mix_norm_g),
        "mlp_norm_g": (mlp_norm_g, jnp.stack([g_mlp0, g_mlp1]), m_mlp_norm_g, v_mlp_norm_g),
        "s5_a_re": (s5_a_re, g_are[None], m_s5_a_re, v_s5_a_re),
        "s5_a_im": (s5_a_im, g_aim[None], m_s5_a_im, v_s5_a_im),
        "s5_log_dt": (s5_log_dt, g_ldt[None], m_s5_log_dt, v_s5_log_dt),
        "s5_b_re": (s5_b_re, g_bre[None], m_s5_b_re, v_s5_b_re),
        "s5_b_im": (s5_b_im, g_bim[None], m_s5_b_im, v_s5_b_im),
        "s5_c_re": (s5_c_re, g_cre[None], m_s5_c_re, v_s5_c_re),
        "s5_c_im": (s5_c_im, g_cim[None], m_s5_c_im, v_s5_c_im),
        "s5_d": (s5_d, g_s5d, m_s5_d, v_s5_d),
        "kv_ada_b": (kv_ada_b, g_kv_ada_b, m_kv_ada_b, v_kv_ada_b),
        "kv_norm_g": (kv_norm_g, g_kvn, m_kv_norm_g, v_kv_norm_g),
        "k_norm_g": (k_norm_g, g_kn, m_k_norm_g, v_k_norm_g),
        "q_norm_g": (q_norm_g, g_qn.reshape(q_norm_g.shape), m_q_norm_g, v_q_norm_g),
    }
    names = list(small)
    packs = [_pack_rows([small[n][i].reshape(small[n][0].shape) for n in names]) for i in range(4)]
    sp = packs[0][1]
    d_, m_, v_, g_ = _adamw2d(packs[0][0], packs[1][0], packs[2][0], packs[3][0], name="adam_small")
    for n, gg, dd_, mm_, vv_ in zip(names, _unpack_rows(g_, sp), _unpack_rows(d_, sp), _unpack_rows(m_, sp), _unpack_rows(v_, sp)):
        res[n] = [gg, dd_, mm_, vv_]

    order = ["ada_w", "ada_b", "mix_norm_g", "mlp_norm_g", "mlp_w1", "mlp_w2", "s5_a_re", "s5_a_im", "s5_log_dt", "s5_b_re", "s5_b_im",
             "s5_c_re", "s5_c_im", "s5_d", "s5_w_glu", "kv_ada_w", "kv_ada_b", "kv_norm_g", "w_kv", "k_norm_g", "sb_w_q", "q_norm_g", "sb_w_o"]
    return (loss, grad_x, *[res[n][0] for n in order], *[res[n][1] for n in order], *[res[n][2] for n in order], *[res[n][3] for n in order])
```

```python
import functools
import math

import jax
import jax.numpy as jnp
from jax import lax
from jax.experimental import pallas as pl
from jax.experimental.pallas import tpu as pltpu

F32 = jnp.float32
BF16 = jnp.bfloat16
EPS = 1e-6
HEAD_DIM = 64
S5_GROUP = 16
S5_STATE = 64
GROUPS_PER_STEP = 8
U_LANES = GROUPS_PER_STEP * S5_GROUP
ST_LANES = GROUPS_PER_STEP * S5_STATE
SCAN_LANES = 256
SCAN_UNROLL = 4
VMEM_LIMIT = 56 * 1024 * 1024
ADAM_LR, ADAM_B1, ADAM_B2, ADAM_EPS, ADAM_WD, ADAM_STEP = 0.001, 0.9, 0.999, 1e-08, 0.01, 10
MESH = pl.DeviceIdType.MESH


def _cp(sem):
    return pltpu.CompilerParams(dimension_semantics=sem, vmem_limit_bytes=VMEM_LIMIT)


class _Sharded:
    def __init__(self, buf, kind, roff, nr, c0, nc):
        self.buf, self.kind, self.roff, self.nr, self.c0, self.nc = buf, kind, roff, nr, c0, nc
        self.shape = (nr, 4 * nc) if kind == "cols" else (4 * nr, nc)

    def operand(self, dims, tn, tk):
        roff, nr, c0, nc = self.roff, self.nr, self.c0, self.nc
        if self.kind == "cols" and dims == "nn":
            tk = min(tk, nr)
            assert roff % tk == 0
            return nc, tk, (None, tk, nc), lambda i, j, k: (j, roff // tk + k, c0 // nc)
        if self.kind == "cols":
            tn = min(tn, nr)
            assert roff % tn == 0
            return tn, nc, (None, tn, nc), lambda i, j, k: (k, roff // tn + j, c0 // nc)
        if dims == "nn":
            tn = min(tn, nc)
            assert roff % nr == 0 and c0 % tn == 0
            return tn, nr, (None, nr, tn), lambda i, j, k: (k, roff // nr, c0 // tn + j)
        tk = min(tk, nc)
        assert roff % nr == 0 and c0 % tk == 0
        return nr, tk, (None, nr, tk), lambda i, j, k: (j, roff // nr, c0 // tk + k)

    def result(self, tm, tn):
        roff, nr, c0, nc = self.roff, self.nr, self.c0, self.nc
        if self.kind == "cols":
            tm = min(tm, nr)
            assert roff % tm == 0
            return tm, nc, (None, tm, nc), lambda i, j, k: (j, roff // tm + i, c0 // nc)
        tm, tn = min(tm, nr), min(tn, nc)
        assert roff % tm == 0 and c0 % tn == 0
        per = nr // tm
        return tm, tn, (None, tm, tn), lambda i, j, k: (i // per, roff // tm + i % per, c0 // tn + j)


def _mm(a, b, dims, *, name, out_dtypes=(F32,), epilogue=None, extras=(), tm=512, tn=1024, tk=1024, into=None):
    bshape = b.shape
    if dims == "nn":
        (M, K), (_, N) = a.shape, bshape
    elif dims == "nt":
        (M, K), (N, _) = a.shape, bshape
    else:
        (K, M), (_, N) = a.shape, bshape
    tm, tn, tk = min(tm, M), min(tn, N), min(tk, K)
    b_arr = b
    if into is not None:
        assert (M, N) == into.shape and len(out_dtypes) == 1 and not isinstance(b, _Sharded)
        tm, tn, o_blk, o_map = into.result(tm, tn)
        out_specs, out_shape = [pl.BlockSpec(o_blk, o_map)], [jax.ShapeDtypeStruct(into.buf.shape, into.buf.dtype)]
    if isinstance(b, _Sharded):
        tn, tk, b_blk, b_map = b.operand(dims, tn, tk)
        b_spec, b_arr = pl.BlockSpec(b_blk, b_map), b.buf
    else:
        b_spec = pl.BlockSpec((tn, tk), lambda i, j, k: (j, k)) if dims == "nt" else pl.BlockSpec((tk, tn), lambda i, j, k: (k, j))
    if into is None:
        out_specs = [pl.BlockSpec((tm, tn), lambda i, j, k: (i, j)) for _ in out_dtypes]
        out_shape = [jax.ShapeDtypeStruct((M, N), d) for d in out_dtypes]
    assert M % tm == 0 and N % tn == 0 and K % tk == 0, (M, N, K, tm, tn, tk)
    nk = K // tk
    extras = [e(tm, tn) for e in extras]
    a_spec = pl.BlockSpec((tk, tm), lambda i, j, k: (k, i)) if dims == "tn" else pl.BlockSpec((tm, tk), lambda i, j, k: (i, k))
    contract = {"nn": ((1,), (0,)), "nt": ((1,), (1,)), "tn": ((0,), (0,))}[dims]
    n_ex, n_out = len(extras), len(out_dtypes)
    chain = [into.buf] if into is not None and not isinstance(into.buf, jax.ShapeDtypeStruct) else []
    n_in = n_ex + len(chain)

    def finish(r, ex, outs):
        res = epilogue(r, *[e[...] for e in ex]) if epilogue is not None else (r,)
        for o, v in zip(outs, res):
            o[...] = v.astype(o.dtype)

    def product(a_ref, b_ref):
        return lax.dot_general(a_ref[...].astype(BF16), b_ref[...].astype(BF16), (contract, ((), ())), preferred_element_type=F32)

    def body_one(a_ref, b_ref, *rest):
        finish(product(a_ref, b_ref), rest[:n_ex], rest[n_in:])

    def body_acc(a_ref, b_ref, *rest):
        ex, outs, acc = rest[:n_ex], rest[n_in:n_in + n_out], rest[-1]
        k = pl.program_id(2)

        @pl.when(k == 0)
        def _():
            acc[...] = product(a_ref, b_ref)

        @pl.when(k > 0)
        def _():
            acc[...] += product(a_ref, b_ref)

        @pl.when(k == nk - 1)
        def _():
            finish(acc[...], ex, outs)

    out = pl.pallas_call(
        body_one if nk == 1 else body_acc, name=name, grid=(M // tm, N // tn, nk),
        in_specs=[a_spec, b_spec] + [pl.BlockSpec(blk, im) for (_, blk, im) in extras] + [ANY for _ in chain],
        out_specs=out_specs, out_shape=out_shape,
        input_output_aliases={2 + n_ex: 0} if chain else {},
        scratch_shapes=[] if nk == 1 else [pltpu.VMEM((tm, tn), F32)],
        compiler_params=_cp(("parallel", "parallel", "arbitrary")),
    )(a, b_arr, *[e[0] for e in extras], *chain)
    return out if n_out > 1 else out[0]


def _mn_extra(arr):
    return lambda tm, tn: (arr, (tm, tn), lambda i, j, k: (i, j))


def _vec_extra(vec, S):
    return lambda tm, tn: (vec, (None, 1, tn), lambda i, j, k: ((i * tm) // S, 0, j))


def _rowwise(fn, rows, vecs=(), consts=(), out_rows=(), out_sums=(), *, n_ex, name, tr=512):
    rows = [r if len(r) == 4 else (*r, 0) for r in rows]
    S = min(r[0].shape[0] for r in rows if r[3] == 0) // n_ex
    tr = math.gcd(tr, S)
    assert S % tr == 0
    nb = S // tr
    in_specs = []
    for (arr, w, cb, roff) in rows:
        assert roff % tr == 0
        in_specs.append(pl.BlockSpec((tr, w), functools.partial(lambda e, i, cb, ro: (e * nb + i + ro, cb), cb=cb, ro=roff // tr)))
    for v in vecs:
        in_specs.append(pl.BlockSpec((None, 1, v.shape[-1]), lambda e, i: (e, 0, 0)))
    for c in consts:
        in_specs.append(pl.BlockSpec((1, c.shape[-1]), lambda e, i: (0, 0)))
    n_in, n_or, n_os = len(in_specs), len(out_rows), len(out_sums)
    out_specs = [pl.BlockSpec((tr, w), lambda e, i: (e * nb + i, 0)) for (w, _) in out_rows]
    out_specs += [pl.BlockSpec((None, 1, w), lambda e, i: (e, 0, 0)) for w in out_sums]
    out_shape = [jax.ShapeDtypeStruct((n_ex * S, w), d) for (w, d) in out_rows]
    out_shape += [jax.ShapeDtypeStruct((n_ex, 1, w), F32) for w in out_sums]

    def body(*refs):
        ins, o_r, o_s = refs[:n_in], refs[n_in:n_in + n_or], refs[n_in + n_or:]
        ro, so = fn(*[r[...] for r in ins])
        for o, v in zip(o_r, ro):
            o[...] = v.astype(o.dtype)
        i = pl.program_id(1)
        for o, v in zip(o_s, so):
            @pl.when(i == 0)
            def _(o=o, v=v):
                o[...] = v

            @pl.when(i > 0)
            def _(o=o, v=v):
                o[...] += v

    outs = pl.pallas_call(
        body, name=name, grid=(n_ex, nb), in_specs=in_specs, out_specs=out_specs, out_shape=out_shape,
        compiler_params=_cp(("parallel", "arbitrary")),
    )(*[r[0] for r in rows], *vecs, *consts)
    return outs[:n_or], outs[n_or:]


def _csum(x):
    return jnp.sum(x, axis=0, keepdims=True)


def _norm_mod_fwd(x, g, sh, sc, *, n_ex, out_dtype, name):
    def fn(xt, sht, sct, gt):
        r = lax.rsqrt(jnp.mean(xt * xt, axis=-1, keepdims=True) + EPS)
        return [(xt * r * gt) * (1.0 + sct) + sht], []
    D = x.shape[1]
    return _rowwise(fn, [(x, D, 0)], [sh, sc], [g], [(D, out_dtype)], [], n_ex=n_ex, name=name)[0][0]


def _norm_mod_bwd(x, dh, dres, g, sc, *, n_ex, name):
    def fn(xt, dht, drt, sct, gt):
        dht = dht.astype(F32)
        r = lax.rsqrt(jnp.mean(xt * xt, axis=-1, keepdims=True) + EPS)
        n = xt * r
        y = n * gt
        dy = dht * (1.0 + sct)
        dn = dy * gt
        dx = r * (dn - n * jnp.mean(dn * n, axis=-1, keepdims=True))
        return [drt + dx], [_csum(dht), _csum(dht * y), _csum(dy * n)]
    D = x.shape[1]
    return _rowwise(fn, [(x, D, 0), (dh, D, 0), (dres, D, 0)], [sc], [g], [(D, F32)], [D, D, D], n_ex=n_ex, name=name)


def _sigmoid(x):
    return 1.0 / (1.0 + jnp.exp(-x))


def _gelu(y):
    return 0.5 * y * (1.0 + jnp.tanh(0.7978845608028654 * (y + 0.044715 * y * y * y)))


def _gelu_grad(y):
    t = jnp.tanh(0.7978845608028654 * (y + 0.044715 * y * y * y))
    return 0.5 * (1.0 + t) + 0.5 * y * (1.0 - t * t) * 0.7978845608028654 * (1.0 + 3 * 0.044715 * y * y)


def _adamw_fn(w, g, m, v):
    m2 = ADAM_B1 * m + (1.0 - ADAM_B1) * g
    v2 = ADAM_B2 * v + (1.0 - ADAM_B2) * (g * g)
    m_hat = m2 / (1.0 - ADAM_B1 ** ADAM_STEP)
    v_hat = v2 / (1.0 - ADAM_B2 ** ADAM_STEP)
    delta = -ADAM_LR * (m_hat / (jnp.sqrt(v_hat) + ADAM_EPS) + ADAM_WD * w)
    return delta, m2, v2


def _adamw2d(w, g, m, v, *, name, g_roff=0, g_cb=0):
    R, W = w.shape

    def fn(wt, gt, mt, vt):
        d, m2, v2 = _adamw_fn(wt, gt, mt, vt)
        return [d, m2, v2, gt], []
    return _rowwise(fn, [(w, W, 0), (g, W, g_cb, g_roff), (m, W, 0), (v, W, 0)], [], [],
                    [(W, F32)] * 4, [], n_ex=1, name=name, tr=256)[0]


def _scan_tiles(re_ref, im_ref, cf, lane0, n_chunks, reverse, extra=None):
    L = SCAN_LANES
    lanes = pl.ds(lane0, L)
    A = [cf[i, :, lanes] for i in range(8)]
    shifts = (7, 6, 4) if reverse else (1, 2, 4)
    edge = 0 if reverse else 7

    U = SCAN_UNROLL
    n_groups = n_chunks // U

    def body(c, carry):
        first = ((n_groups - 1 - c) if reverse else c) * U
        rows = pl.ds(pl.multiple_of(first * 8, 8 * U), 8 * U)
        big_r, big_i = re_ref[rows, lanes], im_ref[rows, lanes]
        tiles = []
        for u in range(U):
            xr, xi = big_r[8 * u:8 * u + 8, :], big_i[8 * u:8 * u + 8, :]
            for idx, sft in enumerate(shifts):
                ar, ai = A[2 * idx], A[2 * idx + 1]
                rr, ri = pltpu.roll(xr, sft, 0), pltpu.roll(xi, sft, 0)
                xr, xi = xr + ar * rr - ai * ri, xi + ar * ri + ai * rr
            tiles.append((xr, xi))
        pr, pi = A[6], A[7]
        cr, ci = carry[0], carry[1]
        for u in (range(U - 1, -1, -1) if reverse else range(U)):
            xr, xi = tiles[u]
            xr, xi = xr + pr * cr - pi * ci, xi + pr * ci + pi * cr
            tiles[u] = (xr, xi)
            cr, ci = jnp.broadcast_to(xr[edge:edge + 1, :], (8, L)), jnp.broadcast_to(xi[edge:edge + 1, :], (8, L))
        re_ref[rows, lanes] = jnp.concatenate([t[0] for t in tiles], axis=0)
        im_ref[rows, lanes] = jnp.concatenate([t[1] for t in tiles], axis=0)
        return (cr, ci) if extra is None else (cr, ci) + extra(first, tiles, carry[2:])

    assert n_chunks % U == 0
    z = jnp.zeros((8, L), F32)
    init = (z, z) if extra is None else (z, z, z, z)
    return lax.fori_loop(0, n_groups, body, init)


def _s5_consts(ab_re, ab_im):
    ng = ab_re.shape[0] // GROUPS_PER_STEP
    ar, ai = ab_re.reshape(ng, 1, ST_LANES), ab_im.reshape(ng, 1, ST_LANES)

    def cmul(xr, xi, yr, yi):
        return xr * yr - xi * yi, xr * yi + xi * yr

    def build(ar, ai, reverse):
        pw = [(ar, ai)]
        for _ in range(7):
            pw.append(cmul(*pw[-1], ar, ai))
        row = jnp.arange(8).reshape(1, 8, 1)
        tiles = []
        for k in (1, 2, 4):
            keep = (row <= 7 - k) if reverse else (row >= k)
            tiles += [jnp.where(keep, pw[k - 1][0], 0.0), jnp.where(keep, pw[k - 1][1], 0.0)]
        order = [7 - r for r in range(8)] if reverse else list(range(8))
        tiles += [jnp.concatenate([pw[o][0] for o in order], axis=1), jnp.concatenate([pw[o][1] for o in order], axis=1)]
        return jnp.stack([jnp.broadcast_to(t, (ng, 8, ST_LANES)) for t in tiles], axis=1)

    return build(ar, ai, False), build(ar, -ai, True)


def _s5_blockdiag(bb_re, bb_im, c_re, c_im):
    G = bb_re.shape[0]
    ng = G // GROUPS_PER_STEP
    eye = jnp.eye(GROUPS_PER_STEP, dtype=F32)

    def wb(bb):
        return jnp.einsum("bgph,gk->bghkp", bb.reshape(ng, GROUPS_PER_STEP, S5_STATE, S5_GROUP), eye).reshape(ng, U_LANES, ST_LANES)

    def wc(cc):
        return jnp.einsum("bghp,gk->bkpgh", cc.reshape(ng, GROUPS_PER_STEP, S5_GROUP, S5_STATE), eye).reshape(ng, ST_LANES, U_LANES)

    Wb = jnp.concatenate([wb(bb_re), wb(bb_im)], axis=2).astype(BF16)
    Wc = jnp.concatenate([wc(c_re), -wc(c_im)], axis=1).astype(BF16)
    return Wb, Wc


def _s5_unblock(dWb, dWc):
    ng = dWb.shape[0]
    eye = jnp.eye(GROUPS_PER_STEP, dtype=F32)

    def ub(w):
        return jnp.einsum("bghkp,gk->bgph", w.reshape(ng, GROUPS_PER_STEP, S5_GROUP, GROUPS_PER_STEP, S5_STATE), eye).reshape(-1, S5_STATE, S5_GROUP)

    def uc(w):
        return jnp.einsum("bkpgh,gk->bghp", w.reshape(ng, GROUPS_PER_STEP, S5_STATE, GROUPS_PER_STEP, S5_GROUP), eye).reshape(-1, S5_GROUP, S5_STATE)

    return ub(dWb[:, :, :ST_LANES]), ub(dWb[:, :, ST_LANES:]), uc(dWc[:, :ST_LANES, :]), -uc(dWc[:, ST_LANES:, :])


def _s5_disc(a_re, a_im, log_dt, b_re, b_im):
    dt = jnp.exp(log_dt)[:, None]
    mag = jnp.exp(a_re * dt)
    ab_re = mag * jnp.cos(a_im * dt)
    ab_im = mag * jnp.sin(a_im * dt)
    den = a_re * a_re + a_im * a_im
    nr, ni = ab_re - 1, ab_im
    f_re = (nr * a_re + ni * a_im) / den
    f_im = (ni * a_re - nr * a_im) / den
    bb_re = f_re[..., None] * b_re - f_im[..., None] * b_im
    bb_im = f_re[..., None] * b_im + f_im[..., None] * b_re
    return ab_re, ab_im, bb_re, bb_im


ROW_CHUNK = 512


def _s5_fwd(u, Wb, Wc, cf, d, xsrc, *, n_ex, name):
    T, D = u.shape
    S = T // n_ex
    ng = D // U_LANES
    rc = min(ROW_CHUNK, S)

    def body(u_ref, wb_ref, wc_ref, cf_ref, d_ref, xsrc_ref, y_ref, gy_ref, st_ref, xout_ref, re_s, im_s, *sems):
        step = pl.program_id(0) * ng + pl.program_id(1)
        exch = _ChipExchange(xsrc_ref, xout_ref, *sems, scatter=False)

        @pl.when(step == 0)
        def _():
            exch.start()

        for r in range(S // rc):
            rows = pl.ds(r * rc, rc)
            bu = jnp.dot(u_ref[rows, :].astype(BF16), wb_ref[...], preferred_element_type=F32)
            re_s[rows, :] = bu[:, :ST_LANES]
            im_s[rows, :] = bu[:, ST_LANES:]
        for l0 in range(0, ST_LANES, SCAN_LANES):
            _scan_tiles(re_s, im_s, cf_ref, l0, S // 8, False)
        for r in range(S // rc):
            rows = pl.ds(r * rc, rc)
            st = jnp.concatenate([re_s[rows, :], im_s[rows, :]], axis=1).astype(BF16)
            st_ref[rows, :] = st
            y = jnp.dot(st, wc_ref[...], preferred_element_type=F32) + d_ref[...] * u_ref[rows, :]
            y_ref[rows, :] = y
            gy_ref[rows, :] = _gelu(y).astype(BF16)

        @pl.when(step == n_ex * ng - 1)
        def _():
            exch.wait()

    return pl.pallas_call(
        body, name=name, grid=(n_ex, ng),
        in_specs=[pl.BlockSpec((S, U_LANES), lambda e, g: (e, g)),
                  pl.BlockSpec((None, U_LANES, 2 * ST_LANES), lambda e, g: (g, 0, 0)),
                  pl.BlockSpec((None, 2 * ST_LANES, U_LANES), lambda e, g: (g, 0, 0)),
                  pl.BlockSpec((None, 8, 8, ST_LANES), lambda e, g: (g, 0, 0, 0)),
                  pl.BlockSpec((1, U_LANES), lambda e, g: (0, g)), ANY],
        out_specs=[pl.BlockSpec((S, U_LANES), lambda e, g: (e, g))] * 2 + [pl.BlockSpec((S, 2 * ST_LANES), lambda e, g: (e, g)), ANY],
        out_shape=[jax.ShapeDtypeStruct((T, D), F32), jax.ShapeDtypeStruct((T, D), BF16),
                   jax.ShapeDtypeStruct((T, ng * 2 * ST_LANES), BF16), _ChipExchange.out_shape(xsrc, False)],
        scratch_shapes=[pltpu.VMEM((S, ST_LANES), F32)] * 2 + _ChipExchange.SCRATCH,
        compiler_params=_cp(("arbitrary", "arbitrary")),
    )(u, Wb, Wc, cf, d, xsrc)


def _s5_bwd(u, y, dgy, st, Wb, Wc, cr, d, xsrc, *, n_ex, name):
    T, D = u.shape
    S = T // n_ex
    ng = D // U_LANES
    rc = min(ROW_CHUNK, S)
    nch = S // 8
    grp = 8 * SCAN_UNROLL
    assert grp % 16 == 0

    def body(u_ref, y_ref, dgy_ref, st_ref, wb_ref, wc_ref, cr_ref, d_ref, xsrc_ref,
             du_ref, dwb_ref, dwc_ref, dab_ref, dd_ref, xout_ref, gr_s, gi_s, dy_s, *sems):
        e = pl.program_id(1)
        step = pl.program_id(0) * n_ex + e
        exch = _ChipExchange(xsrc_ref, xout_ref, *sems, scatter=True)

        @pl.when(step == 0)
        def _():
            exch.start()

        @pl.when(e == 0)
        def _():
            dwb_ref[...] = jnp.zeros_like(dwb_ref)
            dwc_ref[...] = jnp.zeros_like(dwc_ref)
            dab_ref[...] = jnp.zeros_like(dab_ref)
            dd_ref[...] = jnp.zeros_like(dd_ref)

        dd = jnp.zeros((1, U_LANES), F32)
        for r in range(S // rc):
            rows = pl.ds(r * rc, rc)
            ut = u_ref[rows, :]
            dy = dgy_ref[rows, :].astype(F32) * _gelu_grad(y_ref[rows, :])
            dy_s[rows, :] = dy
            dd = dd + _csum(dy * ut)
            go = lax.dot_general(dy.astype(BF16), wc_ref[...], (((1,), (1,)), ((), ())), preferred_element_type=F32)
            gr_s[rows, :] = go[:, :ST_LANES]
            gi_s[rows, :] = go[:, ST_LANES:]
        dd_ref[0:1, :] += dd
        row0 = lax.broadcasted_iota(jnp.int32, (8, SCAN_LANES), 0) == 0
        for l0 in range(0, ST_LANES, SCAN_LANES):
            lanes = pl.ds(l0, SCAN_LANES)

            def dab_group(first, tiles, acc, l0=l0):
                def states(r0, n, lane0):
                    return st_ref[pl.ds(pl.multiple_of(r0, 16), n), pl.ds(lane0, SCAN_LANES)].astype(F32)
                r0 = first * 8
                cur = states(r0, grp, l0), states(r0, grp, ST_LANES + l0)
                live = (first > 0).astype(F32)
                p0 = jnp.maximum(r0 - 16, 0)
                before = [states(p0, 16, l0)[8:16, :] * live, states(p0, 16, ST_LANES + l0)[8:16, :] * live]
                a_re, a_im = acc
                for t, (gr, gi) in enumerate(tiles):
                    here = [c[8 * t:8 * t + 8, :] for c in cur]
                    sr, si = [jnp.where(row0, pltpu.roll(b, 1, 0), pltpu.roll(h, 1, 0)) for b, h in zip(before, here)]
                    a_re, a_im = a_re + gr * sr + gi * si, a_im + gi * sr - gr * si
                    before = here
                return a_re, a_im

            res = _scan_tiles(gr_s, gi_s, cr_ref, l0, nch, True, extra=dab_group)
            dab_ref[0:1, lanes] += _csum(res[2])
            dab_ref[1:2, lanes] += _csum(res[3])
        for r in range(S // rc):
            rows = pl.ds(r * rc, rc)
            st = st_ref[rows, :]
            g = jnp.concatenate([gr_s[rows, :], gi_s[rows, :]], axis=1).astype(BF16)
            dyb = dy_s[rows, :].astype(BF16)
            dwc_ref[...] += lax.dot_general(st, dyb, (((0,), (0,)), ((), ())), preferred_element_type=F32)
            dwb_ref[...] += lax.dot_general(u_ref[rows, :].astype(BF16), g, (((0,), (0,)), ((), ())), preferred_element_type=F32)
            du = lax.dot_general(g, wb_ref[...], (((1,), (1,)), ((), ())), preferred_element_type=F32)
            du_ref[rows, :] = du + d_ref[...] * dy_s[rows, :]

        @pl.when(step == ng * n_ex - 1)
        def _():
            exch.wait()

    return pl.pallas_call(
        body, name=name, grid=(ng, n_ex),
        in_specs=[pl.BlockSpec((S, U_LANES), lambda g, e: (e, g))] * 3 + [
            pl.BlockSpec((S, 2 * ST_LANES), lambda g, e: (e, g)),
            pl.BlockSpec((None, U_LANES, 2 * ST_LANES), lambda g, e: (g, 0, 0)),
            pl.BlockSpec((None, 2 * ST_LANES, U_LANES), lambda g, e: (g, 0, 0)),
            pl.BlockSpec((None, 8, 8, ST_LANES), lambda g, e: (g, 0, 0, 0)),
            pl.BlockSpec((1, U_LANES), lambda g, e: (0, g)), ANY],
        out_specs=[pl.BlockSpec((S, U_LANES), lambda g, e: (e, g)),
                   pl.BlockSpec((None, U_LANES, 2 * ST_LANES), lambda g, e: (g, 0, 0)),
                   pl.BlockSpec((None, 2 * ST_LANES, U_LANES), lambda g, e: (g, 0, 0)),
                   pl.BlockSpec((None, 8, ST_LANES), lambda g, e: (g, 0, 0)),
                   pl.BlockSpec((None, 8, U_LANES), lambda g, e: (g, 0, 0)), ANY],
        out_shape=[jax.ShapeDtypeStruct((T, D), F32),
                   jax.ShapeDtypeStruct((ng, U_LANES, 2 * ST_LANES), F32),
                   jax.ShapeDtypeStruct((ng, 2 * ST_LANES, U_LANES), F32),
                   jax.ShapeDtypeStruct((ng, 8, ST_LANES), F32),
                   jax.ShapeDtypeStruct((ng, 8, U_LANES), F32), _ChipExchange.out_shape(xsrc, True)],
        scratch_shapes=[pltpu.VMEM((S, ST_LANES), F32)] * 2 + [pltpu.VMEM((S, U_LANES), F32)] + _ChipExchange.SCRATCH,
        compiler_params=_cp(("arbitrary", "arbitrary")),
    )(u, y, dgy, st, Wb, Wc, cr, d, xsrc)


TQ = 256
KW = 512
SUB = 128


def _head_masks():
    lane = lax.broadcasted_iota(jnp.int32, (1, 2 * HEAD_DIM), 1)
    m0 = (lane < HEAD_DIM).astype(F32)
    return m0, 1.0 - m0


def _head_norm(x, g, m0, m1):
    sq = x * x
    r0 = lax.rsqrt(jnp.sum(sq * m0, axis=-1, keepdims=True) / HEAD_DIM + EPS)
    r1 = lax.rsqrt(jnp.sum(sq * m1, axis=-1, keepdims=True) / HEAD_DIM + EPS)
    r = m0 * r0 + m1 * r1
    return x * r, r


def _head_norm_bwd(dy, n, r, g, m0, m1):
    dn = dy * g
    p = dn * n
    mean = (m0 * jnp.sum(p * m0, axis=-1, keepdims=True) + m1 * jnp.sum(p * m1, axis=-1, keepdims=True)) / HEAD_DIM
    return r * (dn - n * mean), _csum(dy * n)


def _pair_matrix(kind):
    r = lax.broadcasted_iota(jnp.int32, (2 * SUB, 2 * SUB), 0)
    c = lax.broadcasted_iota(jnp.int32, (2 * SUB, 2 * SUB), 1)
    same = (r < SUB) == (c < SUB)
    rel = {"after": r > c, "upto": r <= c, "before": r < c}[kind]
    return jnp.logical_and(same, rel).astype(BF16)


def _block_sums(x, mat, carry, reverse, terms=2):
    hi = x.astype(BF16)
    lo = (x - hi.astype(F32)).astype(BF16) if terms == 2 else None
    npair = x.shape[1] // (2 * SUB)
    parts = [None] * (2 * npair)
    for p in (range(npair - 1, -1, -1) if reverse else range(npair)):
        sl = slice(2 * SUB * p, 2 * SUB * (p + 1))
        loc = jnp.dot(hi[:, sl], mat, preferred_element_type=F32)
        if terms == 2:
            loc = loc + jnp.dot(lo[:, sl], mat, preferred_element_type=F32)
        for b in ((1, 0) if reverse else (0, 1)):
            k = 2 * p + b
            parts[k] = loc[:, SUB * b:SUB * (b + 1)] + carry
            carry = carry + jnp.sum(x[:, SUB * k:SUB * (k + 1)], axis=-1, keepdims=True)
    return jnp.concatenate(parts, axis=1), carry


def _sb_logits(z, mask):
    lp = jnp.minimum(z, 0.0) - jnp.log(1.0 + jnp.exp(-jnp.abs(z)))
    lf = lp - z
    if mask is not None:
        lf = jnp.where(mask, lf, 0.0)
    return lp, lf


def _causal_mask(row0, col0, kw):
    r = row0 + lax.broadcasted_iota(jnp.int32, (TQ, kw), 0)
    c = col0 + lax.broadcasted_iota(jnp.int32, (TQ, kw), 1)
    return c < r


def _transposed_windows(x, ref):
    for w in range(x.shape[0] // KW):
        ref[w] = x[w * KW:(w + 1) * KW, :].T.astype(BF16)


def _attn_fwd(q, kv, qg, kg, xsrc, *, n_ex, name):
    T, D = q.shape
    S = T // n_ex
    nhp = D // (2 * HEAD_DIM)
    nq = S // TQ
    scale = 1.0 / math.sqrt(HEAD_DIM)

    def body(q_ref, k_ref, v_ref, qg_ref, kg_ref, xsrc_ref, o_ref, tot_ref, xout_ref, kT_s, qm_s, vm_s, *sems):
        step = pl.program_id(0) * nhp + pl.program_id(1)
        exch = _ChipExchange(xsrc_ref, xout_ref, *sems, scatter=False)

        @pl.when(step == 0)
        def _():
            exch.start()

        m0, m1 = _head_masks()
        qn, _ = _head_norm(q_ref[...], None, m0, m1)
        qn = qn * (qg_ref[...] * scale)
        kn, _ = _head_norm(k_ref[...], None, m0, m1)
        _transposed_windows(kn * kg_ref[...], kT_s)
        v = v_ref[...]
        for h, m in enumerate((m0, m1)):
            qm_s[h] = (qn * m).astype(BF16)
            vm_s[h] = (v * m).astype(BF16)
        u_after = _pair_matrix("after")

        def window(rows, win, st, mask, kw):
            keys = pl.ds(pl.multiple_of(win * KW, KW), kw)
            zs = [jnp.dot(qm_s[h, rows, :], kT_s[win, :, :kw], preferred_element_type=F32) for h in range(2)]
            lg = [_sb_logits(zs[h], mask) for h in range(2)]
            sums = [_block_sums(lg[h][1], u_after, st[2 * h], True) for h in range(2)]
            out = ()
            for h in range(2):
                w = jnp.exp(lg[h][0] + sums[h][0])
                if mask is not None:
                    w = jnp.where(mask, w, 0.0)
                out += (sums[h][1], st[2 * h + 1] + jnp.dot(w.astype(BF16), vm_s[h, keys, :], preferred_element_type=F32))
            return out

        def qtile(iq, last, kw):
            rows = pl.ds(pl.multiple_of(iq * TQ, TQ), TQ)
            mask = _causal_mask(iq * TQ, last * KW, kw)
            z1, zq = jnp.zeros((TQ, 1), F32), jnp.zeros((TQ, 2 * HEAD_DIM), F32)
            st = window(rows, last, (z1, zq, z1, zq), mask, kw)
            st = lax.fori_loop(0, last, lambda jj, st: window(rows, last - 1 - jj, st, None, KW), st)
            o_ref[rows, :] = st[1] + st[3]
            tot_ref[rows, :] = st[0] * m0 + st[2] * m1

        def qtiles_of_window(a, _):
            for sub in range(KW // TQ):
                qtile(a * (KW // TQ) + sub, a, (sub + 1) * TQ)
            return 0

        lax.fori_loop(0, S // KW, qtiles_of_window, 0)

        @pl.when(step == n_ex * nhp - 1)
        def _():
            exch.wait()

    assert S % KW == 0 and KW % TQ == 0
    nwin = S // KW
    blk = (S, 2 * HEAD_DIM)
    return pl.pallas_call(
        body, name=name, grid=(n_ex, nhp),
        in_specs=[pl.BlockSpec(blk, lambda e, h: (e, h)), pl.BlockSpec(blk, lambda e, h: (e, h)),
                  pl.BlockSpec(blk, lambda e, h: (e, h + nhp)),
                  pl.BlockSpec((1, 2 * HEAD_DIM), lambda e, h: (0, 0)), pl.BlockSpec((1, 2 * HEAD_DIM), lambda e, h: (0, 0)), ANY],
        out_specs=[pl.BlockSpec(blk, lambda e, h: (e, h))] * 2 + [ANY],
        out_shape=[jax.ShapeDtypeStruct((T, D), F32)] * 2 + [_ChipExchange.out_shape(xsrc, False)],
        scratch_shapes=[pltpu.VMEM((nwin, 2 * HEAD_DIM, KW), BF16), pltpu.VMEM((2,) + blk, BF16), pltpu.VMEM((2,) + blk, BF16)]
        + _ChipExchange.SCRATCH,
        compiler_params=_cp(("arbitrary", "arbitrary")),
    )(q, kv, kv, qg, kg, xsrc)


def _attn_bwd(q, kv, tot, do, qg, kg, *, n_ex, name):
    T, D = q.shape
    S = T // n_ex
    nhp = D // (2 * HEAD_DIM)
    nq = S // TQ
    scale = 1.0 / math.sqrt(HEAD_DIM)

    def body(q_ref, k_ref, v_ref, tot_ref, do_ref, qg_ref, kg_ref, dq_ref, dk_ref, dv_ref, dqg_ref, dkg_ref,
             kT_s, vT_s, km_s, qm_s, dom_s, dqn_s, dkT_s, dvT_s):
        m0, m1 = _head_masks()
        qn, qr = _head_norm(q_ref[...], None, m0, m1)
        kn, kr = _head_norm(k_ref[...], None, m0, m1)
        qs = qn * (qg_ref[...] * scale)
        kk = kn * kg_ref[...]
        _transposed_windows(kk, kT_s)
        _transposed_windows(v_ref[...], vT_s)
        do = do_ref[...]
        for h, m in enumerate((m0, m1)):
            qm_s[h] = (qs * m).astype(BF16)
            km_s[h] = (kk * m).astype(BF16)
            dom_s[h] = (do * m).astype(BF16)
        dkT_s[...] = jnp.zeros_like(dkT_s)
        dvT_s[...] = jnp.zeros_like(dvT_s)
        u_upto, u_before = _pair_matrix("upto"), _pair_matrix("before")

        def both(inv, win, st, mask, kw):
            keys = pl.ds(pl.multiple_of(win * KW, KW), kw)
            lg = [_sb_logits(jnp.dot(inv[h][0], kT_s[win, :, :kw], preferred_element_type=F32), mask) for h in range(2)]
            dw = [jnp.dot(inv[h][2], vT_s[win, :, :kw], preferred_element_type=F32) for h in range(2)]
            s_lf = [_block_sums(lg[h][1], u_upto, st[3 * h], False) for h in range(2)]
            ws, ews = [], []
            for h in range(2):
                w = jnp.exp(lg[h][0] + (inv[h][4] - s_lf[h][0]))
                if mask is not None:
                    w = jnp.where(mask, w, 0.0)
                ws.append(w)
                ews.append(dw[h] * w)
            s_e = [_block_sums(ews[h], u_before, st[3 * h + 1], False, terms=1) for h in range(2)]
            out, dk, dv = (), None, None
            for h in range(2):
                sig = jnp.exp(lg[h][0])
                dz = ews[h] * (1.0 - sig) - s_e[h][0] * sig
                if mask is not None:
                    dz = jnp.where(mask, dz, 0.0)
                dzb = dz.astype(BF16)
                out += (s_lf[h][1], s_e[h][1], st[3 * h + 2] + jnp.dot(dzb, km_s[h, keys, :], preferred_element_type=F32))
                dkh = jnp.dot(inv[h][1], dzb, preferred_element_type=F32)
                dvh = jnp.dot(inv[h][3], ws[h].astype(BF16), preferred_element_type=F32)
                dk, dv = (dkh, dvh) if h == 0 else (dk + dkh, dv + dvh)
            dkT_s[win, :, :kw] += dk
            dvT_s[win, :, :kw] += dv
            return out

        def qtile(iq, last, kw):
            rows = pl.ds(pl.multiple_of(iq * TQ, TQ), TQ)
            mask = _causal_mask(iq * TQ, last * KW, kw)
            tt = tot_ref[rows, :]
            inv = []
            for h, m in enumerate((m0, m1)):
                qh, doh = qm_s[h, rows, :], dom_s[h, rows, :]
                total = jnp.sum(tt * m, axis=-1, keepdims=True) * (1.0 / HEAD_DIM)
                inv.append((qh, qh.astype(F32).T.astype(BF16), doh, doh.astype(F32).T.astype(BF16), total))

            z1, zq = jnp.zeros((TQ, 1), F32), jnp.zeros((TQ, 2 * HEAD_DIM), F32)
            st = lax.fori_loop(0, last, lambda win, st: both(inv, win, st, None, KW), (z1, z1, zq, z1, z1, zq))
            st = both(inv, last, st, mask, kw)
            dqn_s[rows, :] = st[2] + st[5]

        def qtiles_of_window(a, _):
            for sub in range(KW // TQ):
                qtile(a * (KW // TQ) + sub, a, (sub + 1) * TQ)
            return 0

        lax.fori_loop(0, S // KW, qtiles_of_window, 0)
        dkn = jnp.concatenate([dkT_s[w].T for w in range(nwin)], axis=0)
        dq, dqg = _head_norm_bwd(dqn_s[...] * scale, qn, qr, qg_ref[...], m0, m1)
        dk, dkg = _head_norm_bwd(dkn, kn, kr, kg_ref[...], m0, m1)
        dq_ref[...] = dq
        dk_ref[...] = dk
        dv_ref[...] = jnp.concatenate([dvT_s[w].T for w in range(nwin)], axis=0)
        dqg_ref[...] = dqg
        dkg_ref[...] = dkg

    assert S % KW == 0 and KW % TQ == 0
    nwin = S // KW
    blk = (S, 2 * HEAD_DIM)
    tblk = (nwin, 2 * HEAD_DIM, KW)
    gblk = (None, None, 1, 2 * HEAD_DIM)
    dq, dk, dv, dqg, dkg = pl.pallas_call(
        body, name=name, grid=(n_ex, nhp),
        in_specs=[pl.BlockSpec(blk, lambda e, h: (e, h)), pl.BlockSpec(blk, lambda e, h: (e, h)),
                  pl.BlockSpec(blk, lambda e, h: (e, h + nhp)),
                  pl.BlockSpec(blk, lambda e, h: (e, h)), pl.BlockSpec(blk, lambda e, h: (e, h)),
                  pl.BlockSpec((1, 2 * HEAD_DIM), lambda e, h: (0, 0)), pl.BlockSpec((1, 2 * HEAD_DIM), lambda e, h: (0, 0))],
        out_specs=[pl.BlockSpec(blk, lambda e, h: (e, h))] * 3 + [pl.BlockSpec(gblk, lambda e, h: (e, h, 0, 0))] * 2,
        out_shape=[jax.ShapeDtypeStruct((T, D), F32)] * 3 + [jax.ShapeDtypeStruct((n_ex, nhp, 1, 2 * HEAD_DIM), F32)] * 2,
        scratch_shapes=[pltpu.VMEM(tblk, BF16), pltpu.VMEM(tblk, BF16),
                        pltpu.VMEM((2,) + blk, BF16), pltpu.VMEM((2,) + blk, BF16), pltpu.VMEM((2,) + blk, BF16),
                        pltpu.VMEM(blk, F32), pltpu.VMEM(tblk, F32), pltpu.VMEM(tblk, F32)],
        compiler_params=_cp(("parallel", "parallel")),
    )(q, kv, kv, tot, do, qg, kg)
    return dq, dk, dv, dqg, dkg


def _place():
    return lax.axis_index("x"), lax.axis_index("y"), lax.axis_index("c")


def _all_gather8(x_shard, *, name):
    m_per, n = x_shard.shape

    def body(x_ref, out_ref, send_sems, recv_sems, local_sem):
        x, y, c = _place()
        me, sibling = (x, y, c), (x, y, 1 - c)
        chips = [(1 - x, y), (x, 1 - y), (1 - x, 1 - y)]

        def rows(px, py, pc):
            return out_ref.at[pl.ds((4 * px + 2 * py + pc) * m_per, m_per), :]

        def copy(k, block, to, src=None):
            return pltpu.make_async_remote_copy(
                src_ref=rows(*block) if src is None else src, dst_ref=rows(*block),
                send_sem=send_sems.at[k], recv_sem=recv_sems.at[k], device_id=to, device_id_type=MESH)

        mine = pltpu.make_async_copy(x_ref, rows(*me), local_sem)
        mine.start()
        first = [copy(0, me, sibling, src=x_ref)]
        first += [copy(1 + j, me, (*chip, c), src=x_ref) for j, chip in enumerate(chips)]
        for cp in first:
            cp.start()
        passed = [copy(4 + j, (*chip, c), sibling) for j, chip in enumerate(chips)]
        for j, chip in enumerate(chips):
            copy(1 + j, (*chip, c), me).wait_recv()
            passed[j].start()
        copy(0, sibling, me).wait_recv()
        for j, chip in enumerate(chips):
            copy(4 + j, (*chip, 1 - c), me).wait_recv()
        for cp in first + passed:
            cp.wait_send()
        mine.wait()

    return pl.pallas_call(
        body, name=name, out_shape=jax.ShapeDtypeStruct((8 * m_per, n), x_shard.dtype),
        in_specs=[pl.BlockSpec(memory_space=pltpu.VMEM)], out_specs=pl.BlockSpec(memory_space=pltpu.VMEM),
        scratch_shapes=[pltpu.SemaphoreType.DMA((7,)), pltpu.SemaphoreType.DMA((7,)), pltpu.SemaphoreType.DMA],
        compiler_params=pltpu.CompilerParams(vmem_limit_bytes=VMEM_LIMIT),
    )(x_shard)


def _sibling_sum_half(x, *, name):
    R, C = x.shape
    half = R // 2

    def body(x_ref, o_ref, theirs, send_sem, recv_sem):
        px, py, pc = _place()
        cp = pltpu.make_async_remote_copy(src_ref=x_ref, dst_ref=theirs, send_sem=send_sem, recv_sem=recv_sem,
                                          device_id=(px, py, 1 - pc), device_id_type=MESH)
        cp.start()
        cp.wait()
        rows = pl.ds(pl.multiple_of(pc * half, 8), half)
        o_ref[...] = x_ref[rows, :] + theirs[rows, :]

    return pl.pallas_call(
        body, name=name, out_shape=jax.ShapeDtypeStruct((half, C), x.dtype),
        in_specs=[pl.BlockSpec(memory_space=pltpu.VMEM)], out_specs=pl.BlockSpec(memory_space=pltpu.VMEM),
        scratch_shapes=[pltpu.VMEM((R, C), x.dtype), pltpu.SemaphoreType.DMA, pltpu.SemaphoreType.DMA],
        compiler_params=pltpu.CompilerParams(vmem_limit_bytes=VMEM_LIMIT),
    )(x)


def _sum_blocks(x, n, *, name):
    R = x.shape[0] // n

    def body(x_ref, o_ref):
        acc = x_ref[pl.ds(0, R), :]
        for k in range(1, n):
            acc = acc + x_ref[pl.ds(k * R, R), :]
        o_ref[...] = acc

    return pl.pallas_call(body, name=name, out_shape=jax.ShapeDtypeStruct((R, x.shape[1]), x.dtype),
                          compiler_params=pltpu.CompilerParams(vmem_limit_bytes=VMEM_LIMIT))(x)


def _colsum(x, *, name):
    def body(x_ref, o_ref):
        o_ref[...] = jnp.sum(x_ref[...], axis=0, keepdims=True)
    return pl.pallas_call(body, name=name, out_shape=jax.ShapeDtypeStruct((1, x.shape[1]), x.dtype))(x)


ANY = pl.BlockSpec(memory_space=pl.ANY)


class _ChipExchange:
    SCRATCH = [pltpu.SemaphoreType.DMA((3,)), pltpu.SemaphoreType.DMA((3,)), pltpu.SemaphoreType.DMA]

    @staticmethod
    def out_shape(src, scatter):
        return jax.ShapeDtypeStruct(((4,) + tuple(src.shape[1:])) if scatter else ((4, 2) + tuple(src.shape[1:])), src.dtype)

    def __init__(self, src_ref, out_ref, send_sems, recv_sems, local_sem, scatter):
        x, y, c = _place()
        myj = 2 * x + y
        chips = [(1 - x, y), (x, 1 - y), (1 - x, 1 - y)]

        def slot(j):
            return out_ref.at[j] if scatter else out_ref.at[j, c]

        def piece(j):
            return src_ref.at[j] if scatter else src_ref.at[c]

        self.mine = pltpu.make_async_copy(piece(myj), slot(myj), local_sem)
        self.sends = [pltpu.make_async_remote_copy(
            src_ref=piece(2 * cx + cy), dst_ref=slot(myj), send_sem=send_sems.at[k], recv_sem=recv_sems.at[k],
            device_id=(cx, cy, c), device_id_type=MESH) for k, (cx, cy) in enumerate(chips)]
        self.recvs = [pltpu.make_async_remote_copy(
            src_ref=slot(2 * cx + cy), dst_ref=slot(2 * cx + cy), send_sem=send_sems.at[k], recv_sem=recv_sems.at[k],
            device_id=(cx, cy, c), device_id_type=MESH) for k, (cx, cy) in enumerate(chips)]

    def start(self):
        self.mine.start()
        for cp in self.sends:
            cp.start()

    def wait(self):
        for cp in self.recvs:
            cp.wait_recv()
        for cp in self.sends:
            cp.wait_send()
        self.mine.wait()


def _sibling_fill(buf, *, axis, name):
    def half(ref, h):
        return ref.at[h] if axis == 0 else ref.at[:, h]

    def body(in_ref, out_ref, send_sem, recv_sem):
        x, y, c = _place()
        cp = pltpu.make_async_remote_copy(src_ref=half(out_ref, c), dst_ref=half(out_ref, c), send_sem=send_sem, recv_sem=recv_sem,
                                          device_id=(x, y, 1 - c), device_id_type=MESH)
        cp.start()
        pltpu.make_async_remote_copy(src_ref=half(out_ref, 1 - c), dst_ref=half(out_ref, 1 - c), send_sem=send_sem, recv_sem=recv_sem,
                                     device_id=(x, y, 1 - c), device_id_type=MESH).wait_recv()
        cp.wait_send()

    return pl.pallas_call(
        body, name=name, out_shape=jax.ShapeDtypeStruct(buf.shape, buf.dtype), in_specs=[ANY], out_specs=ANY,
        input_output_aliases={0: 0}, scratch_shapes=[pltpu.SemaphoreType.DMA, pltpu.SemaphoreType.DMA],
    )(buf)


def _sibling_swap_half(g, *, name):
    def body(g_ref, out_ref, send_sem, recv_sem):
        x, y, c = _place()
        cp = pltpu.make_async_remote_copy(src_ref=g_ref.at[:, 1 - c], dst_ref=out_ref, send_sem=send_sem, recv_sem=recv_sem,
                                          device_id=(x, y, 1 - c), device_id_type=MESH)
        cp.start()
        cp.wait()

    return pl.pallas_call(
        body, name=name, out_shape=jax.ShapeDtypeStruct((g.shape[0],) + g.shape[2:], g.dtype), in_specs=[ANY], out_specs=ANY,
        scratch_shapes=[pltpu.SemaphoreType.DMA, pltpu.SemaphoreType.DMA],
    )(g)


def _add_my_half(g, b, cidx, *, name, tr=256):
    n, _, R, C = g.shape
    tr = math.gcd(tr, R)

    def body(c_ref, g_ref, b_ref, o_ref):
        o_ref[...] = (g_ref[...] + b_ref[...]).astype(o_ref.dtype)

    return pl.pallas_call(
        body, name=name, out_shape=jax.ShapeDtypeStruct((n, R, C), BF16),
        grid_spec=pltpu.PrefetchScalarGridSpec(
            num_scalar_prefetch=1, grid=(n, R // tr),
            in_specs=[pl.BlockSpec((None, None, tr, C), lambda j, i, c: (j, c[0], i, 0)),
                      pl.BlockSpec((None, tr, C), lambda j, i, c: (j, i, 0))],
            out_specs=pl.BlockSpec((None, tr, C), lambda j, i, c: (j, i, 0))),
        compiler_params=_cp(("parallel", "parallel")),
    )(cidx, g, b)


def _sum4_into_half(q, cidx, *, name, tr=256):
    _, R, C = q.shape
    tr = math.gcd(tr, R)

    def body(c_ref, q_ref, o_ref):
        o_ref[...] = ((q_ref[0].astype(F32) + q_ref[1].astype(F32)) + q_ref[2].astype(F32)) + q_ref[3].astype(F32)

    return pl.pallas_call(
        body, name=name, out_shape=jax.ShapeDtypeStruct((2, R, C), F32),
        grid_spec=pltpu.PrefetchScalarGridSpec(
            num_scalar_prefetch=1, grid=(R // tr,),
            in_specs=[pl.BlockSpec((4, tr, C), lambda i, c: (0, i, 0))],
            out_specs=pl.BlockSpec((None, tr, C), lambda i, c: (c[0], i, 0))),
        compiler_params=_cp(("parallel",)),
    )(cidx, q)


def _pack_rows(parts, width=1024):
    rows, spans, r0 = [], [], 0
    for p in parts:
        n = p.size
        nr = 8 * (-(-n // (8 * width)))
        flat = p.reshape(-1)
        if nr * width != n:
            flat = jnp.pad(flat, (0, nr * width - n))
        rows.append(flat.reshape(nr, width))
        spans.append((r0, nr, n, p.shape))
        r0 += nr
    return jnp.concatenate(rows, axis=0), spans


def _unpack_rows(buf, spans):
    return [buf[r0:r0 + nr].reshape(-1)[:n].reshape(shape) for (r0, nr, n, shape) in spans]


def kernel(x, c, ada_w, ada_b, mix_norm_g, mlp_norm_g, mlp_w1, mlp_w2, s5_a_re, s5_a_im, s5_log_dt, s5_b_re, s5_b_im, s5_c_re, s5_c_im, s5_d, s5_w_glu, kv_ada_w, kv_ada_b, kv_norm_g, w_kv, k_norm_g, sb_w_q, q_norm_g, sb_w_o, loss_target, m_ada_w, m_ada_b, m_mix_norm_g, m_mlp_norm_g, m_mlp_w1, m_mlp_w2, m_s5_a_re, m_s5_a_im, m_s5_log_dt, m_s5_b_re, m_s5_b_im, m_s5_c_re, m_s5_c_im, m_s5_d, m_s5_w_glu, m_kv_ada_w, m_kv_ada_b, m_kv_norm_g, m_w_kv, m_k_norm_g, m_sb_w_q, m_q_norm_g, m_sb_w_o, v_ada_w, v_ada_b, v_mix_norm_g, v_mlp_norm_g, v_mlp_w1, v_mlp_w2, v_s5_a_re, v_s5_a_im, v_s5_log_dt, v_s5_b_re, v_s5_b_im, v_s5_c_re, v_s5_c_im, v_s5_d, v_s5_w_glu, v_kv_ada_w, v_kv_ada_b, v_kv_norm_g, v_w_kv, v_k_norm_g, v_sb_w_q, v_q_norm_g, v_sb_w_o):
    E, S, D = x.shape
    T = E * S
    FF = 4 * D
    NB = 8 * E
    px, py, pc = _place()
    chip = 2 * px + py
    dev = 4 * px + 2 * py + pc
    cidx = jnp.reshape(pc, (1,)).astype(jnp.int32)
    x0 = x.reshape(T, D)
    tgt = loss_target.reshape(T, D)

    c_all = _all_gather8(c.reshape(-1, 128), name="ag_c").reshape(NB, D)
    sc_all = (c_all * _sigmoid(c_all)).astype(BF16)
    wa = ada_w.shape[2]
    wk = kv_ada_w.shape[1]
    m_sh = jnp.concatenate([_mm(sc_all, ada_w[0], "nn", name="ada0", tn=256),
                            _mm(sc_all, ada_w[1], "nn", name="ada1", tn=256),
                            _mm(sc_all, kv_ada_w, "nn", name="ada_kv", tn=256)], axis=1)
    m_all = _all_gather8(m_sh, name="ag_m").reshape(4, 2, NB, 2 * wa + wk)[:, 0]
    mods = []
    for l in range(2):
        full = jnp.transpose(m_all[:, :, l * wa:(l + 1) * wa], (1, 0, 2)).reshape(NB, 6 * D) + ada_b[l]
        mine = lax.dynamic_slice_in_dim(full, E * dev, E, axis=0)
        mods.append([mine[:, i * D:(i + 1) * D].reshape(E, 1, D) for i in range(6)])
    full = jnp.transpose(m_all[:, :, 2 * wa:], (1, 0, 2)).reshape(NB, 2 * D) + kv_ada_b
    mine = lax.dynamic_slice_in_dim(full, E * dev, E, axis=0)
    kv_sh, kv_sc = [mine[:, i * D:(i + 1) * D].reshape(E, 1, D) for i in range(2)]

    wpack_a = jnp.concatenate([mlp_w1[0], mlp_w2[0], jnp.concatenate([s5_w_glu[0], w_kv], axis=1), sb_w_q[0]], axis=0).astype(BF16)
    wpack_b = jnp.concatenate([mlp_w1[1], mlp_w2[1], sb_w_o[0]], axis=0).astype(BF16)
    RA, RB = wpack_a.shape[0], wpack_b.shape[0]
    RW = RA + RB

    tm = min(1024, S)
    gbuf = [jax.ShapeDtypeStruct((4, RW, D), F32)]

    def grad_mm(act, dout, kind, roff, nr, c0, nc, name):
        gbuf[0] = _mm(act, dout, "tn", name=name, tk=2048, into=_Sharded(gbuf[0], kind, roff, nr, c0, nc))

    def mlp_fwd(xa, l, mod):
        sh_m, sc_m, g_m = mod[3], mod[4], mod[5]
        h = _norm_mod_fwd(xa, mlp_norm_g[l:l + 1], sh_m, sc_m, n_ex=E, out_dtype=BF16, name=f"mlp_norm{l}")
        r = _mm(h, W1[l], "nn", name=f"mlp_up{l}", out_dtypes=(BF16,), tm=tm,
                epilogue=lambda acc: (jnp.square(jnp.maximum(acc, 0.0)),))
        xb, ff = _mm(r, W2[l], "nn", name=f"mlp_down{l}", out_dtypes=(F32, F32), tm=tm,
                     extras=[_mn_extra(xa), _vec_extra(g_m, S)],
                     epilogue=lambda acc, xat, gt: (xat + gt * acc, acc))
        return xb, (h, r, ff)

    def mlp_bwd(dxb, xa, l, mod, saved):
        sc_m, g_m = mod[4], mod[5]
        h, r, ff = saved
        (dff,), (dgm,) = _rowwise(lambda d, f, g: ([g * d], [_csum(d * f)]), [(dxb, D, 0), (ff, D, 0)], [g_m], [],
                                  [(D, BF16)], [D], n_ex=E, name=f"mlp_gate_bwd{l}")
        da = _mm(dff, W2[l], "nt", name=f"mlp_down_dx{l}", out_dtypes=(BF16,), tm=tm, extras=[_mn_extra(r)],
                 epilogue=lambda acc, rt: (acc * (2.0 * jnp.sqrt(rt.astype(F32))),))
        grad_mm(r, dff, "rows", (2 + l) * D, D, 0, D, f"mlp_down_dw{l}")
        dh = _mm(da, W1[l], "nt", name=f"mlp_up_dx{l}", tm=tm)
        grad_mm(h, da, "cols", l * D, D, 0, D, f"mlp_up_dw{l}")
        (dxa,), (dsh, dsc, dg) = _norm_mod_bwd(xa, dh, dxb, mlp_norm_g[l:l + 1], sc_m, n_ex=E, name=f"mlp_norm_bwd{l}")
        return dxa, (dsh, dsc, dgm), dg

    ab_re, ab_im, bb_re, bb_im = _s5_disc(s5_a_re[0], s5_a_im[0], s5_log_dt[0], s5_b_re[0], s5_b_im[0])
    cf, cr = _s5_consts(ab_re, ab_im)
    Wb, Wc = _s5_blockdiag(bb_re, bb_im, s5_c_re[0], s5_c_im[0])
    ng = D // U_LANES
    nd = s5_d.size // 128
    d_full = _all_gather8(jnp.pad(s5_d.reshape(nd, 128), ((0, 8 - nd), (0, 0))), name="ag_d")
    d_full = d_full.reshape(4, 2, 8, 128)[:, 0, :nd].reshape(1, D)

    mod0, mod1 = mods
    h0 = _norm_mod_fwd(x0, mix_norm_g[0:1], mod0[0], mod0[1], n_ex=E, out_dtype=F32, name="mix_norm0")
    y, gy, s5_states, wfull_a = _s5_fwd(h0, Wb, Wc, cf, d_full, wpack_a.reshape(2, RA // 2, D), n_ex=E, name="s5_fwd")
    wfull_a = _sibling_fill(wfull_a, axis=1, name="wgather_a_d2d").reshape(4, RA, D)

    W1 = [_Sharded(wfull_a, "cols", 0, D, 0, D), None]
    W2 = [_Sharded(wfull_a, "rows", D, D, 0, D), None]
    Wglu = _Sharded(wfull_a, "cols", 2 * D, D, 0, D // 2)
    Wkv = _Sharded(wfull_a, "cols", 2 * D, D, D // 2, D // 2)
    Wq = _Sharded(wfull_a, "rows", 3 * D, D // 4, 0, D)
    vg = _mm(gy, Wglu, "nn", name="glu_up", tm=tm)
    (x1,), _ = _rowwise(lambda v, g, xt, ga: ([xt + ga * (v * _sigmoid(g))], []),
                        [(vg, D, 0), (vg, D, 1), (x0, D, 0)], [mod0[2]], [], [(D, F32)], [], n_ex=E, name="glu_gate")
    x2, saved_mlp0 = mlp_fwd(x1, 0, mod0)

    hkv = _norm_mod_fwd(x2, kv_norm_g.reshape(1, D), kv_sh, kv_sc, n_ex=E, out_dtype=BF16, name="kv_norm")
    kvf = _mm(hkv, Wkv, "nn", name="kv_proj", tm=tm)
    h1 = _norm_mod_fwd(x2, mix_norm_g[1:2], mod1[0], mod1[1], n_ex=E, out_dtype=BF16, name="mix_norm1")
    qf = _mm(h1, Wq, "nn", name="q_proj", tm=tm)
    qg2 = jnp.tile(q_norm_g.reshape(1, HEAD_DIM), (1, 2))
    kg2 = jnp.tile(k_norm_g.reshape(1, HEAD_DIM), (1, 2))
    o, lf_tot, wfull_b = _attn_fwd(qf, kvf, qg2, kg2, wpack_b.reshape(2, RB // 2, D), n_ex=E, name="attn_fwd")
    wfull_b = _sibling_fill(wfull_b, axis=1, name="wgather_b_d2d").reshape(4, RB, D)
    W1[1] = _Sharded(wfull_b, "cols", 0, D, 0, D)
    W2[1] = _Sharded(wfull_b, "rows", D, D, 0, D)
    Wo = _Sharded(wfull_b, "rows", 2 * D, D // 4, 0, D)
    x3, mix1 = _mm(o, Wo, "nn", name="o_proj", out_dtypes=(F32, F32), tm=tm,
                   extras=[_mn_extra(x2), _vec_extra(mod1[2], S)],
                   epilogue=lambda acc, xat, gt: (xat + gt * acc, acc))
    x4, saved_mlp1 = mlp_fwd(x3, 1, mod1)

    (dx4,), (lsum,) = _rowwise(lambda xt, tt: ([(xt - tt) * (1.0 / D)], [_csum(jnp.square(xt - tt)) * (0.5 / D)]),
                               [(x4, D, 0), (tgt, D, 0)], [], [], [(D, F32)], [D], n_ex=E, name="loss")
    loss = lax.psum(jnp.sum(lsum), ("x", "y", "c"))

    dx3, (dsh_m1, dsc_m1, dgm1), dg_mlp1 = mlp_bwd(dx4, x3, 1, mod1, saved_mlp1)
    (dmix1,), (dga1,) = _rowwise(lambda d, f, g: ([g * d], [_csum(d * f)]), [(dx3, D, 0), (mix1, D, 0)], [mod1[2]], [],
                                 [(D, BF16)], [D], n_ex=E, name="attn_gate_bwd")
    do = _mm(dmix1, Wo, "nt", name="o_proj_dx", tm=tm)
    grad_mm(o, dmix1, "rows", 5 * D + D // 4, D // 4, 0, D, "o_proj_dw")
    dq, dk, dv, dqg, dkg = _attn_bwd(qf, kvf, lf_tot, do, qg2, kg2, n_ex=E, name="attn_bwd")
    dh1 = _mm(dq, Wq, "nt", name="q_proj_dx", tm=tm)
    grad_mm(h1, dq, "rows", 5 * D, D // 4, 0, D, "q_proj_dw")
    (dx2,), (dsh_a1, dsc_a1, dg_mix1) = _norm_mod_bwd(x2, dh1, dx3, mix_norm_g[1:2], mod1[1], n_ex=E, name="mix_norm_bwd1")
    dkv = jnp.concatenate([dk, dv], axis=1)
    dhkv = _mm(dkv, Wkv, "nt", name="kv_proj_dx", tm=tm)
    grad_mm(hkv, dkv, "cols", 4 * D, D, D // 2, D // 2, "kv_proj_dw")
    (dx2,), (dkv_sh, dkv_sc, dg_kv) = _norm_mod_bwd(x2, dhkv, dx2, kv_norm_g.reshape(1, D), kv_sc, n_ex=E, name="kv_norm_bwd")

    dx1, (dsh_m0, dsc_m0, dgm0), dg_mlp0 = mlp_bwd(dx2, x1, 0, mod0, saved_mlp0)

    def glu_bwd(v, g, d, ga):
        sg = _sigmoid(g)
        dm = ga * d
        return [jnp.concatenate([dm * sg, dm * v * sg * (1.0 - sg)], axis=1)], [_csum(d * (v * sg))]
    (dvg,), (dga0,) = _rowwise(glu_bwd, [(vg, D, 0), (vg, D, 1), (dx1, D, 0)], [mod0[2]], [], [(2 * D, BF16)], [D],
                               n_ex=E, name="glu_gate_bwd")
    dgy = _mm(dvg, Wglu, "nt", name="glu_up_dx", tm=tm)
    grad_mm(gy, dvg, "cols", 4 * D, D, 0, D // 2, "glu_up_dw")

    gpack = gbuf[0].reshape(4, 2, RW // 2, D)
    theirs = _sibling_swap_half(gpack, name="gscatter_d2d")
    chip_sum = _add_my_half(gpack, theirs, cidx, name="gscatter_add")
    dh0, dWb, dWc, dab, dd, from_chips = _s5_bwd(h0, y, dgy, s5_states, Wb, Wc, cr, d_full, chip_sum, n_ex=E, name="s5_bwd")
    ghalf = _sum4_into_half(from_chips, cidx, name="gscatter_sum")
    gsh = _sibling_fill(ghalf, axis=0, name="gscatter_fill").reshape(RW, D)
    (gx,), (dsh_a0, dsc_a0, dg_mix0) = _norm_mod_bwd(x0, dh0, dx1, mix_norm_g[0:1], mod0[1], n_ex=E, name="mix_norm_bwd0")
    grad_x = gx.reshape(E, S, D)

    dm_mine = jnp.concatenate([t.reshape(E, D) for t in
                               (dsh_a0, dsc_a0, dga0, dsh_m0, dsc_m0, dgm0, dsh_a1, dsc_a1, dga1, dsh_m1, dsc_m1, dgm1, dkv_sh, dkv_sc)], axis=1)
    dm_all = _all_gather8(dm_mine.reshape(8, -1), name="ag_dm").reshape(NB, 14 * D)
    sc_f32 = c_all * _sigmoid(c_all)
    g_ada_w = jnp.stack([_mm(sc_f32, lax.dynamic_slice_in_dim(dm_all, l * 6 * D + chip * wa, wa, axis=1), "tn", name=f"ada_dw{l}", tn=256)
                         for l in range(2)])
    g_kv_ada_w = _mm(sc_f32, lax.dynamic_slice_in_dim(dm_all, 12 * D + chip * wk, wk, axis=1), "tn", name="ada_kv_dw", tn=256)
    db_all = _colsum(dm_all, name="ada_db")
    g_ada_b = db_all[0, :12 * D].reshape(2, 6 * D)
    g_kv_ada_b = db_all[0, 12 * D:]

    dWb_re, dWb_im, dC_re, dC_im = _s5_unblock(dWb, dWc)
    small_parts = [dg_mix0.sum(0), dg_mix1.sum(0), dg_mlp0.sum(0), dg_mlp1.sum(0), dg_kv.sum(0),
                   dqg.sum((0, 1, 2)).reshape(2, HEAD_DIM).sum(0), dkg.sum((0, 1, 2)).reshape(2, HEAD_DIM).sum(0),
                   dd[:, 0, :], dab[:, 0, :], dab[:, 1, :], dWb_re, dWb_im, dC_re, dC_im]
    spack, spans = _pack_rows(small_parts)
    chip_half = _sibling_sum_half(spack, name="small_d2d")
    ssum = _sum_blocks(_all_gather8(chip_half, name="ag_small"), 4, name="sum_small")
    (g_mix0, g_mix1, g_mlp0, g_mlp1, g_kvn, g_qn, g_kn, g_d, g_abr, g_abi, g_bbr, g_bbi, g_cre, g_cim) = _unpack_rows(ssum, spans)
    _, disc_vjp = jax.vjp(_s5_disc, s5_a_re[0], s5_a_im[0], s5_log_dt[0], s5_b_re[0], s5_b_im[0])
    g_are, g_aim, g_ldt, g_bre, g_bim = disc_vjp((g_abr.reshape(ab_re.shape), g_abi.reshape(ab_im.shape), g_bbr, g_bbi))
    g_s5d = lax.dynamic_slice_in_dim(g_d.reshape(1, D), chip * s5_d.shape[1], s5_d.shape[1], axis=1)

    def upd_big(w, m, v, roff, cb, name):
        shape = w.shape
        W = shape[-1]
        d_, m_, v_, g_ = _adamw2d(w.reshape(-1, W), gsh, m.reshape(-1, W), v.reshape(-1, W), name=name, g_roff=roff, g_cb=cb)
        return [t.reshape(shape) for t in (g_, d_, m_, v_)]

    def upd_own(w, g, m, v, name):
        shape = w.shape
        W = shape[-1]
        d_, m_, v_, g_ = _adamw2d(w.reshape(-1, W), g.reshape(-1, W), m.reshape(-1, W), v.reshape(-1, W), name=name)
        return [t.reshape(shape) for t in (g_, d_, m_, v_)]

    res = {}
    res["ada_w"] = upd_own(ada_w, g_ada_w, m_ada_w, v_ada_w, "adam_ada_w")
    res["kv_ada_w"] = upd_own(kv_ada_w, g_kv_ada_w, m_kv_ada_w, v_kv_ada_w, "adam_kv_ada_w")
    res["mlp_w1"] = upd_big(mlp_w1, m_mlp_w1, v_mlp_w1, 0, 0, "adam_w1")
    res["mlp_w2"] = upd_big(mlp_w2, m_mlp_w2, v_mlp_w2, 2 * D, 0, "adam_w2")
    res["s5_w_glu"] = upd_big(s5_w_glu, m_s5_w_glu, v_s5_w_glu, 4 * D, 0, "adam_glu")
    res["w_kv"] = upd_big(w_kv, m_w_kv, v_w_kv, 4 * D, 1, "adam_wkv")
    res["sb_w_q"] = upd_big(sb_w_q, m_sb_w_q, v_sb_w_q, 5 * D, 0, "adam_wq")
    res["sb_w_o"] = upd_big(sb_w_o, m_sb_w_o, v_sb_w_o, 5 * D + D // 4, 0, "adam_wo")

    small = {
        "ada_b": (ada_b, g_ada_b, m_ada_b, v_ada_b),
        "mix_norm_g": (mix_norm_g, jnp.stack([g_mix0, g_mix1]), m_mix_norm_g, v_mix_norm_g),
        "mlp_norm_g": (mlp_norm_g, jnp.stack([g_mlp0, g_mlp1]), m_mlp_norm_g, v_mlp_norm_g),
        "s5_a_re": (s5_a_re, g_are[None], m_s5_a_re, v_s5_a_re),
        "s5_a_im": (s5_a_im, g_aim[None], m_s5_a_im, v_s5_a_im),
        "s5_log_dt": (s5_log_dt, g_ldt[None], m_s5_log_dt, v_s5_log_dt),
        "s5_b_re": (s5_b_re, g_bre[None], m_s5_b_re, v_s5_b_re),
        "s5_b_im": (s5_b_im, g_bim[None], m_s5_b_im, v_s5_b_im),
        "s5_c_re": (s5_c_re, g_cre[None], m_s5_c_re, v_s5_c_re),
        "s5_c_im": (s5_c_im, g_cim[None], m_s5_c_im, v_s5_c_im),
        "s5_d": (s5_d, g_s5d, m_s5_d, v_s5_d),
        "kv_ada_b": (kv_ada_b, g_kv_ada_b, m_kv_ada_b, v_kv_ada_b),
        "kv_norm_g": (kv_norm_g, g_kvn, m_kv_norm_g, v_kv_norm_g),
        "k_norm_g": (k_norm_g, g_kn, m_k_norm_g, v_k_norm_g),
        "q_norm_g": (q_norm_g, g_qn.reshape(q_norm_g.shape), m_q_norm_g, v_q_norm_g),
    }
    names = list(small)
    packs = [_pack_rows([small[n][i].reshape(small[n][0].shape) for n in names]) for i in range(4)]
    sp = packs[0][1]
    d_, m_, v_, g_ = _adamw2d(packs[0][0], packs[1][0], packs[2][0], packs[3][0], name="adam_small")
    for n, gg, dd_, mm_, vv_ in zip(names, _unpack_rows(g_, sp), _unpack_rows(d_, sp), _unpack_rows(m_, sp), _unpack_rows(v_, sp)):
        res[n] = [gg, dd_, mm_, vv_]

    order = ["ada_w", "ada_b", "mix_norm_g", "mlp_norm_g", "mlp_w1", "mlp_w2", "s5_a_re", "s5_a_im", "s5_log_dt", "s5_b_re", "s5_b_im",
             "s5_c_re", "s5_c_im", "s5_d", "s5_w_glu", "kv_ada_w", "kv_ada_b", "kv_norm_g", "w_kv", "k_norm_g", "sb_w_q", "q_norm_g", "sb_w_o"]
    return (loss, grad_x, *[res[n][0] for n in order], *[res[n][1] for n in order], *[res[n][2] for n in order], *[res[n][3] for n in order])
```

```python
import functools
import math

import jax
import jax.numpy as jnp
from jax import lax
from jax.experimental import pallas as pl
from jax.experimental.pallas import tpu as pltpu

F32 = jnp.float32
BF16 = jnp.bfloat16
EPS = 1e-6
HEAD_DIM = 64
S5_GROUP = 16
S5_STATE = 64
GROUPS_PER_STEP = 8
U_LANES = GROUPS_PER_STEP * S5_GROUP
ST_LANES = GROUPS_PER_STEP * S5_STATE
SCAN_LANES = 256
SCAN_UNROLL = 4
VMEM_LIMIT = 56 * 1024 * 1024
ADAM_LR, ADAM_B1, ADAM_B2, ADAM_EPS, ADAM_WD, ADAM_STEP = 0.001, 0.9, 0.999, 1e-08, 0.01, 10
MESH = pl.DeviceIdType.MESH


def _cp(sem):
    return pltpu.CompilerParams(dimension_semantics=sem, vmem_limit_bytes=VMEM_LIMIT)


class _Sharded:
    def __init__(self, buf, kind, roff, nr, c0, nc):
        self.buf, self.kind, self.roff, self.nr, self.c0, self.nc = buf, kind, roff, nr, c0, nc
        self.shape = (nr, 4 * nc) if kind == "cols" else (4 * nr, nc)

    def operand(self, dims, tn, tk):
        roff, nr, c0, nc = self.roff, self.nr, self.c0, self.nc
        if self.kind == "cols" and dims == "nn":
            tk = min(tk, nr)
            assert roff % tk == 0
            return nc, tk, (None, tk, nc), lambda i, j, k: (j, roff // tk + k, c0 // nc)
        if self.kind == "cols":
            tn = min(tn, nr)
            assert roff % tn == 0
            return tn, nc, (None, tn, nc), lambda i, j, k: (k, roff // tn + j, c0 // nc)
        if dims == "nn":
            tn = min(tn, nc)
            assert roff % nr == 0 and c0 % tn == 0
            return tn, nr, (None, nr, tn), lambda i, j, k: (k, roff // nr, c0 // tn + j)
        tk = min(tk, nc)
        assert roff % nr == 0 and c0 % tk == 0
        return nr, tk, (None, nr, tk), lambda i, j, k: (j, roff // nr, c0 // tk + k)

    def result(self, tm, tn):
        roff, nr, c0, nc = self.roff, self.nr, self.c0, self.nc
        if self.kind == "cols":
            tm = min(tm, nr)
            assert roff % tm == 0
            return tm, nc, (None, tm, nc), lambda i, j, k: (j, roff // tm + i, c0 // nc)
        tm, tn = min(tm, nr), min(tn, nc)
        assert roff % tm == 0 and c0 % tn == 0
        per = nr // tm
        return tm, tn, (None, tm, tn), lambda i, j, k: (i // per, roff // tm + i % per, c0 // tn + j)


def _mm(a, b, dims, *, name, out_dtypes=(F32,), epilogue=None, extras=(), tm=512, tn=1024, tk=1024, into=None):
    bshape = b.shape
    if dims == "nn":
        (M, K), (_, N) = a.shape, bshape
    elif dims == "nt":
        (M, K), (N, _) = a.shape, bshape
    else:
        (K, M), (_, N) = a.shape, bshape
    tm, tn, tk = min(tm, M), min(tn, N), min(tk, K)
    b_arr = b
    if into is not None:
        assert (M, N) == into.shape and len(out_dtypes) == 1 and not isinstance(b, _Sharded)
        tm, tn, o_blk, o_map = into.result(tm, tn)
        out_specs, out_shape = [pl.BlockSpec(o_blk, o_map)], [jax.ShapeDtypeStruct(into.buf.shape, into.buf.dtype)]
    if isinstance(b, _Sharded):
        tn, tk, b_blk, b_map = b.operand(dims, tn, tk)
        b_spec, b_arr = pl.BlockSpec(b_blk, b_map), b.buf
    else:
        b_spec = pl.BlockSpec((tn, tk), lambda i, j, k: (j, k)) if dims == "nt" else pl.BlockSpec((tk, tn), lambda i, j, k: (k, j))
    if into is None:
        out_specs = [pl.BlockSpec((tm, tn), lambda i, j, k: (i, j)) for _ in out_dtypes]
        out_shape = [jax.ShapeDtypeStruct((M, N), d) for d in out_dtypes]
    assert M % tm == 0 and N % tn == 0 and K % tk == 0, (M, N, K, tm, tn, tk)
    nk = K // tk
    extras = [e(tm, tn) for e in extras]
    a_spec = pl.BlockSpec((tk, tm), lambda i, j, k: (k, i)) if dims == "tn" else pl.BlockSpec((tm, tk), lambda i, j, k: (i, k))
    contract = {"nn": ((1,), (0,)), "nt": ((1,), (1,)), "tn": ((0,), (0,))}[dims]
    n_ex, n_out = len(extras), len(out_dtypes)
    chain = [into.buf] if into is not None and not isinstance(into.buf, jax.ShapeDtypeStruct) else []
    n_in = n_ex + len(chain)

    def finish(r, ex, outs):
        res = epilogue(r, *[e[...] for e in ex]) if epilogue is not None else (r,)
        for o, v in zip(outs, res):
            o[...] = v.astype(o.dtype)

    def product(a_ref, b_ref):
        return lax.dot_general(a_ref[...].astype(BF16), b_ref[...].astype(BF16), (contract, ((), ())), preferred_element_type=F32)

    def body_one(a_ref, b_ref, *rest):
        finish(product(a_ref, b_ref), rest[:n_ex], rest[n_in:])

    def body_acc(a_ref, b_ref, *rest):
        ex, outs, acc = rest[:n_ex], rest[n_in:n_in + n_out], rest[-1]
        k = pl.program_id(2)

        @pl.when(k == 0)
        def _():
            acc[...] = product(a_ref, b_ref)

        @pl.when(jnp.logical_and(k > 0, k < nk - 1))
        def _():
            acc[...] += product(a_ref, b_ref)

        @pl.when(k == nk - 1)
        def _():
            finish(acc[...] + product(a_ref, b_ref), ex, outs)

    out = pl.pallas_call(
        body_one if nk == 1 else body_acc, name=name, grid=(M // tm, N // tn, nk),
        in_specs=[a_spec, b_spec] + [pl.BlockSpec(blk, im) for (_, blk, im) in extras] + [ANY for _ in chain],
        out_specs=out_specs, out_shape=out_shape,
        input_output_aliases={2 + n_ex: 0} if chain else {},
        scratch_shapes=[] if nk == 1 else [pltpu.VMEM((tm, tn), F32)],
        compiler_params=_cp(("parallel", "parallel", "arbitrary")),
    )(a, b_arr, *[e[0] for e in extras], *chain)
    return out if n_out > 1 else out[0]


def _mn_extra(arr):
    return lambda tm, tn: (arr, (tm, tn), lambda i, j, k: (i, j))


def _vec_extra(vec, S):
    return lambda tm, tn: (vec, (None, 1, tn), lambda i, j, k: ((i * tm) // S, 0, j))


def _rowwise(fn, rows, vecs=(), consts=(), out_rows=(), out_sums=(), *, n_ex, name, tr=512):
    rows = [r if len(r) == 4 else (*r, 0) for r in rows]
    S = min(r[0].shape[0] for r in rows if r[3] == 0) // n_ex
    tr = math.gcd(tr, S)
    assert S % tr == 0
    nb = S // tr
    in_specs = []
    for (arr, w, cb, roff) in rows:
        assert roff % tr == 0
        in_specs.append(pl.BlockSpec((tr, w), functools.partial(lambda e, i, cb, ro: (e * nb + i + ro, cb), cb=cb, ro=roff // tr)))
    for v in vecs:
        in_specs.append(pl.BlockSpec((None, 1, v.shape[-1]), lambda e, i: (e, 0, 0)))
    for c in consts:
        in_specs.append(pl.BlockSpec((1, c.shape[-1]), lambda e, i: (0, 0)))
    n_in, n_or, n_os = len(in_specs), len(out_rows), len(out_sums)
    flipped = [len(o) == 3 and o[2] for o in out_rows]
    out_specs = [pl.BlockSpec((o[0], tr), lambda e, i: (0, e * nb + i)) if f else pl.BlockSpec((tr, o[0]), lambda e, i: (e * nb + i, 0))
                 for o, f in zip(out_rows, flipped)]
    out_specs += [pl.BlockSpec((None, 1, w), lambda e, i: (e, 0, 0)) for w in out_sums]
    out_shape = [jax.ShapeDtypeStruct((o[0], n_ex * S) if f else (n_ex * S, o[0]), o[1]) for o, f in zip(out_rows, flipped)]
    out_shape += [jax.ShapeDtypeStruct((n_ex, 1, w), F32) for w in out_sums]

    def body(*refs):
        ins, o_r, o_s = refs[:n_in], refs[n_in:n_in + n_or], refs[n_in + n_or:]
        ro, so = fn(*[r[...] for r in ins])
        for o, v, f in zip(o_r, ro, flipped):
            o[...] = (v.T if f else v).astype(o.dtype)
        i = pl.program_id(1)
        for o, v in zip(o_s, so):
            @pl.when(i == 0)
            def _(o=o, v=v):
                o[...] = v

            @pl.when(i > 0)
            def _(o=o, v=v):
                o[...] += v

    outs = pl.pallas_call(
        body, name=name, grid=(n_ex, nb), in_specs=in_specs, out_specs=out_specs, out_shape=out_shape,
        compiler_params=_cp(("parallel", "arbitrary")),
    )(*[r[0] for r in rows], *vecs, *consts)
    return outs[:n_or], outs[n_or:]


def _csum(x):
    return jnp.sum(x, axis=0, keepdims=True)


def _norm_mod_fwd(x, g, sh, sc, *, n_ex, out_dtype, name, with_transpose=False):
    def fn(xt, sht, sct, gt):
        r = lax.rsqrt(jnp.mean(xt * xt, axis=-1, keepdims=True) + EPS)
        h = (xt * r * gt) * (1.0 + sct) + sht
        return [h, h] if with_transpose else [h], []
    D = x.shape[1]
    outs = [(D, out_dtype), (D, out_dtype, True)] if with_transpose else [(D, out_dtype)]
    res = _rowwise(fn, [(x, D, 0)], [sh, sc], [g], outs, [], n_ex=n_ex, name=name)[0]
    return res if with_transpose else res[0]


def _norm_mod_bwd(x, dh, dres, g, sc, *, n_ex, name):
    def fn(xt, dht, drt, sct, gt):
        dht = dht.astype(F32)
        r = lax.rsqrt(jnp.mean(xt * xt, axis=-1, keepdims=True) + EPS)
        n = xt * r
        y = n * gt
        dy = dht * (1.0 + sct)
        dn = dy * gt
        dx = r * (dn - n * jnp.mean(dn * n, axis=-1, keepdims=True))
        return [drt + dx], [_csum(dht), _csum(dht * y), _csum(dy * n)]
    D = x.shape[1]
    return _rowwise(fn, [(x, D, 0), (dh, D, 0), (dres, D, 0)], [sc], [g], [(D, F32)], [D, D, D], n_ex=n_ex, name=name)


def _sigmoid(x):
    return 1.0 / (1.0 + jnp.exp(-x))


def _gelu(y):
    return 0.5 * y * (1.0 + jnp.tanh(0.7978845608028654 * (y + 0.044715 * y * y * y)))


def _gelu_grad(y):
    t = jnp.tanh(0.7978845608028654 * (y + 0.044715 * y * y * y))
    return 0.5 * (1.0 + t) + 0.5 * y * (1.0 - t * t) * 0.7978845608028654 * (1.0 + 3 * 0.044715 * y * y)


def _adamw_fn(w, g, m, v):
    m2 = ADAM_B1 * m + (1.0 - ADAM_B1) * g
    v2 = ADAM_B2 * v + (1.0 - ADAM_B2) * (g * g)
    m_hat = m2 / (1.0 - ADAM_B1 ** ADAM_STEP)
    v_hat = v2 / (1.0 - ADAM_B2 ** ADAM_STEP)
    delta = -ADAM_LR * (m_hat / (jnp.sqrt(v_hat) + ADAM_EPS) + ADAM_WD * w)
    return delta, m2, v2


def _adamw2d(w, g, m, v, *, name, g_roff=0, g_cb=0):
    R, W = w.shape

    def fn(wt, gt, mt, vt):
        d, m2, v2 = _adamw_fn(wt, gt, mt, vt)
        return [d, m2, v2, gt], []
    return _rowwise(fn, [(w, W, 0), (g, W, g_cb, g_roff), (m, W, 0), (v, W, 0)], [], [],
                    [(W, F32)] * 4, [], n_ex=1, name=name, tr=256)[0]


def _scan_tiles(re_ref, im_ref, cf, lane0, n_chunks, reverse, extra=None):
    L = SCAN_LANES
    lanes = pl.ds(lane0, L)
    A = [cf[i, :, lanes] for i in range(8)]
    shifts = (7, 6, 4) if reverse else (1, 2, 4)
    edge = 0 if reverse else 7

    U = SCAN_UNROLL
    n_groups = n_chunks // U

    def body(c, carry):
        first = ((n_groups - 1 - c) if reverse else c) * U
        rows = pl.ds(pl.multiple_of(first * 8, 8 * U), 8 * U)
        big_r, big_i = re_ref[rows, lanes], im_ref[rows, lanes]
        tiles = []
        for u in range(U):
            xr, xi = big_r[8 * u:8 * u + 8, :], big_i[8 * u:8 * u + 8, :]
            for idx, sft in enumerate(shifts):
                ar, ai = A[2 * idx], A[2 * idx + 1]
                rr, ri = pltpu.roll(xr, sft, 0), pltpu.roll(xi, sft, 0)
                xr, xi = xr + ar * rr - ai * ri, xi + ar * ri + ai * rr
            tiles.append((xr, xi))
        pr, pi = A[6], A[7]
        cr, ci = carry[0], carry[1]
        for u in (range(U - 1, -1, -1) if reverse else range(U)):
            xr, xi = tiles[u]
            xr, xi = xr + pr * cr - pi * ci, xi + pr * ci + pi * cr
            tiles[u] = (xr, xi)
            cr, ci = jnp.broadcast_to(xr[edge:edge + 1, :], (8, L)), jnp.broadcast_to(xi[edge:edge + 1, :], (8, L))
        re_ref[rows, lanes] = jnp.concatenate([t[0] for t in tiles], axis=0)
        im_ref[rows, lanes] = jnp.concatenate([t[1] for t in tiles], axis=0)
        return (cr, ci) if extra is None else (cr, ci) + extra(first, tiles, carry[2:])

    assert n_chunks % U == 0
    z = jnp.zeros((8, L), F32)
    init = (z, z) if extra is None else (z, z, z, z)
    return lax.fori_loop(0, n_groups, body, init)


def _s5_consts(ab_re, ab_im):
    ng = ab_re.shape[0] // GROUPS_PER_STEP
    ar, ai = ab_re.reshape(ng, 1, ST_LANES), ab_im.reshape(ng, 1, ST_LANES)

    def cmul(xr, xi, yr, yi):
        return xr * yr - xi * yi, xr * yi + xi * yr

    def build(ar, ai, reverse):
        pw = [(ar, ai)]
        for _ in range(7):
            pw.append(cmul(*pw[-1], ar, ai))
        row = jnp.arange(8).reshape(1, 8, 1)
        tiles = []
        for k in (1, 2, 4):
            keep = (row <= 7 - k) if reverse else (row >= k)
            tiles += [jnp.where(keep, pw[k - 1][0], 0.0), jnp.where(keep, pw[k - 1][1], 0.0)]
        order = [7 - r for r in range(8)] if reverse else list(range(8))
        tiles += [jnp.concatenate([pw[o][0] for o in order], axis=1), jnp.concatenate([pw[o][1] for o in order], axis=1)]
        return jnp.stack([jnp.broadcast_to(t, (ng, 8, ST_LANES)) for t in tiles], axis=1)

    return build(ar, ai, False), build(ar, -ai, True)


def _s5_blockdiag(bb_re, bb_im, c_re, c_im):
    G = bb_re.shape[0]
    ng = G // GROUPS_PER_STEP
    eye = jnp.eye(GROUPS_PER_STEP, dtype=F32)

    def wb(bb):
        return jnp.einsum("bgph,gk->bghkp", bb.reshape(ng, GROUPS_PER_STEP, S5_STATE, S5_GROUP), eye).reshape(ng, U_LANES, ST_LANES)

    def wc(cc):
        return jnp.einsum("bghp,gk->bkpgh", cc.reshape(ng, GROUPS_PER_STEP, S5_GROUP, S5_STATE), eye).reshape(ng, ST_LANES, U_LANES)

    Wb = jnp.concatenate([wb(bb_re), wb(bb_im)], axis=2).astype(BF16)
    Wc = jnp.concatenate([wc(c_re), -wc(c_im)], axis=1).astype(BF16)
    return Wb, Wc


def _s5_unblock(dWb, dWc):
    ng = dWb.shape[0]
    eye = jnp.eye(GROUPS_PER_STEP, dtype=F32)

    def ub(w):
        return jnp.einsum("bghkp,gk->bgph", w.reshape(ng, GROUPS_PER_STEP, S5_GROUP, GROUPS_PER_STEP, S5_STATE), eye).reshape(-1, S5_STATE, S5_GROUP)

    def uc(w):
        return jnp.einsum("bkpgh,gk->bghp", w.reshape(ng, GROUPS_PER_STEP, S5_STATE, GROUPS_PER_STEP, S5_GROUP), eye).reshape(-1, S5_GROUP, S5_STATE)

    return ub(dWb[:, :, :ST_LANES]), ub(dWb[:, :, ST_LANES:]), uc(dWc[:, :ST_LANES, :]), -uc(dWc[:, ST_LANES:, :])


def _s5_disc(a_re, a_im, log_dt, b_re, b_im):
    dt = jnp.exp(log_dt)[:, None]
    mag = jnp.exp(a_re * dt)
    ab_re = mag * jnp.cos(a_im * dt)
    ab_im = mag * jnp.sin(a_im * dt)
    den = a_re * a_re + a_im * a_im
    nr, ni = ab_re - 1, ab_im
    f_re = (nr * a_re + ni * a_im) / den
    f_im = (ni * a_re - nr * a_im) / den
    bb_re = f_re[..., None] * b_re - f_im[..., None] * b_im
    bb_im = f_re[..., None] * b_im + f_im[..., None] * b_re
    return ab_re, ab_im, bb_re, bb_im


ROW_CHUNK = 512


def _s5_fwd(u, Wb, Wc, cf, d, xsrc, *, n_ex, name):
    T, D = u.shape
    S = T // n_ex
    ng = D // U_LANES
    rc = min(ROW_CHUNK, S)

    def body(u_ref, wb_ref, wc_ref, cf_ref, d_ref, xsrc_ref, y_ref, gy_ref, gyt_ref, st_ref, xout_ref, re_s, im_s, *sems):
        step = pl.program_id(0) * ng + pl.program_id(1)
        exch = _ChipExchange(xsrc_ref, xout_ref, *sems, scatter=False)

        @pl.when(step == 0)
        def _():
            exch.start()

        for r in range(S // rc):
            rows = pl.ds(r * rc, rc)
            bu = jnp.dot(u_ref[rows, :].astype(BF16), wb_ref[...], preferred_element_type=F32)
            re_s[rows, :] = bu[:, :ST_LANES]
            im_s[rows, :] = bu[:, ST_LANES:]
        for l0 in range(0, ST_LANES, SCAN_LANES):
            _scan_tiles(re_s, im_s, cf_ref, l0, S // 8, False)
        for r in range(S // rc):
            rows = pl.ds(r * rc, rc)
            st = jnp.concatenate([re_s[rows, :], im_s[rows, :]], axis=1).astype(BF16)
            st_ref[rows, :] = st
            y = jnp.dot(st, wc_ref[...], preferred_element_type=F32) + d_ref[...] * u_ref[rows, :]
            y_ref[rows, :] = y
            gy = _gelu(y)
            gy_ref[rows, :] = gy.astype(BF16)
            gyt_ref[:, rows] = gy.T.astype(BF16)

        @pl.when(step == n_ex * ng - 1)
        def _():
            exch.wait()

    return pl.pallas_call(
        body, name=name, grid=(n_ex, ng),
        in_specs=[pl.BlockSpec((S, U_LANES), lambda e, g: (e, g)),
                  pl.BlockSpec((None, U_LANES, 2 * ST_LANES), lambda e, g: (g, 0, 0)),
                  pl.BlockSpec((None, 2 * ST_LANES, U_LANES), lambda e, g: (g, 0, 0)),
                  pl.BlockSpec((None, 8, 8, ST_LANES), lambda e, g: (g, 0, 0, 0)),
                  pl.BlockSpec((1, U_LANES), lambda e, g: (0, g)), ANY],
        out_specs=[pl.BlockSpec((S, U_LANES), lambda e, g: (e, g))] * 2 + [pl.BlockSpec((U_LANES, S), lambda e, g: (g, e)),
                   pl.BlockSpec((S, 2 * ST_LANES), lambda e, g: (e, g)), ANY],
        out_shape=[jax.ShapeDtypeStruct((T, D), F32), jax.ShapeDtypeStruct((T, D), BF16), jax.ShapeDtypeStruct((D, T), BF16),
                   jax.ShapeDtypeStruct((T, ng * 2 * ST_LANES), BF16), _ChipExchange.out_shape(xsrc, False)],
        scratch_shapes=[pltpu.VMEM((S, ST_LANES), F32)] * 2 + _ChipExchange.SCRATCH,
        compiler_params=_cp(("arbitrary", "arbitrary")),
    )(u, Wb, Wc, cf, d, xsrc)


def _s5_bwd(u, y, dgy, st, Wb, Wc, cr, d, xsrc, *, n_ex, name):
    T, D = u.shape
    S = T // n_ex
    ng = D // U_LANES
    rc = min(ROW_CHUNK, S)
    nch = S // 8
    grp = 8 * SCAN_UNROLL
    assert grp % 16 == 0

    def body(u_ref, y_ref, dgy_ref, st_ref, wb_ref, wc_ref, cr_ref, d_ref, xsrc_ref,
             du_ref, dwb_ref, dwc_ref, dab_ref, dd_ref, xout_ref, gr_s, gi_s, dy_s, *sems):
        e = pl.program_id(1)
        step = pl.program_id(0) * n_ex + e
        exch = _ChipExchange(xsrc_ref, xout_ref, *sems, scatter=True)

        @pl.when(step == 0)
        def _():
            exch.start()

        @pl.when(e == 0)
        def _():
            dwb_ref[...] = jnp.zeros_like(dwb_ref)
            dwc_ref[...] = jnp.zeros_like(dwc_ref)
            dab_ref[...] = jnp.zeros_like(dab_ref)
            dd_ref[...] = jnp.zeros_like(dd_ref)

        dd = jnp.zeros((1, U_LANES), F32)
        for r in range(S // rc):
            rows = pl.ds(r * rc, rc)
            ut = u_ref[rows, :]
            dy = dgy_ref[rows, :].astype(F32) * _gelu_grad(y_ref[rows, :])
            dy_s[rows, :] = dy
            dd = dd + _csum(dy * ut)
            go = lax.dot_general(dy.astype(BF16), wc_ref[...], (((1,), (1,)), ((), ())), preferred_element_type=F32)
            gr_s[rows, :] = go[:, :ST_LANES]
            gi_s[rows, :] = go[:, ST_LANES:]
        dd_ref[0:1, :] += dd
        row0 = lax.broadcasted_iota(jnp.int32, (8, SCAN_LANES), 0) == 0
        for l0 in range(0, ST_LANES, SCAN_LANES):
            lanes = pl.ds(l0, SCAN_LANES)

            def dab_group(first, tiles, acc, l0=l0):
                def states(r0, n, lane0):
                    return st_ref[pl.ds(pl.multiple_of(r0, 16), n), pl.ds(lane0, SCAN_LANES)].astype(F32)
                r0 = first * 8
                cur = states(r0, grp, l0), states(r0, grp, ST_LANES + l0)
                live = (first > 0).astype(F32)
                p0 = jnp.maximum(r0 - 16, 0)
                before = [states(p0, 16, l0)[8:16, :] * live, states(p0, 16, ST_LANES + l0)[8:16, :] * live]
                a_re, a_im = acc
                for t, (gr, gi) in enumerate(tiles):
                    here = [c[8 * t:8 * t + 8, :] for c in cur]
                    sr, si = [jnp.where(row0, pltpu.roll(b, 1, 0), pltpu.roll(h, 1, 0)) for b, h in zip(before, here)]
                    a_re, a_im = a_re + gr * sr + gi * si, a_im + gi * sr - gr * si
                    before = here
                return a_re, a_im

            res = _scan_tiles(gr_s, gi_s, cr_ref, l0, nch, True, extra=dab_group)
            dab_ref[0:1, lanes] += _csum(res[2])
            dab_ref[1:2, lanes] += _csum(res[3])
        for r in range(S // rc):
            rows = pl.ds(r * rc, rc)
            st = st_ref[rows, :]
            g = jnp.concatenate([gr_s[rows, :], gi_s[rows, :]], axis=1).astype(BF16)
            dyb = dy_s[rows, :].astype(BF16)
            dwc_ref[...] += lax.dot_general(st, dyb, (((0,), (0,)), ((), ())), preferred_element_type=F32)
            dwb_ref[...] += lax.dot_general(u_ref[rows, :].astype(BF16), g, (((0,), (0,)), ((), ())), preferred_element_type=F32)
            du = lax.dot_general(g, wb_ref[...], (((1,), (1,)), ((), ())), preferred_element_type=F32)
            du_ref[rows, :] = du + d_ref[...] * dy_s[rows, :]

        @pl.when(step == ng * n_ex - 1)
        def _():
            exch.wait()

    return pl.pallas_call(
        body, name=name, grid=(ng, n_ex),
        in_specs=[pl.BlockSpec((S, U_LANES), lambda g, e: (e, g))] * 3 + [
            pl.BlockSpec((S, 2 * ST_LANES), lambda g, e: (e, g)),
            pl.BlockSpec((None, U_LANES, 2 * ST_LANES), lambda g, e: (g, 0, 0)),
            pl.BlockSpec((None, 2 * ST_LANES, U_LANES), lambda g, e: (g, 0, 0)),
            pl.BlockSpec((None, 8, 8, ST_LANES), lambda g, e: (g, 0, 0, 0)),
            pl.BlockSpec((1, U_LANES), lambda g, e: (0, g)), ANY],
        out_specs=[pl.BlockSpec((S, U_LANES), lambda g, e: (e, g)),
                   pl.BlockSpec((None, U_LANES, 2 * ST_LANES), lambda g, e: (g, 0, 0)),
                   pl.BlockSpec((None, 2 * ST_LANES, U_LANES), lambda g, e: (g, 0, 0)),
                   pl.BlockSpec((None, 8, ST_LANES), lambda g, e: (g, 0, 0)),
                   pl.BlockSpec((None, 8, U_LANES), lambda g, e: (g, 0, 0)), ANY],
        out_shape=[jax.ShapeDtypeStruct((T, D), F32),
                   jax.ShapeDtypeStruct((ng, U_LANES, 2 * ST_LANES), F32),
                   jax.ShapeDtypeStruct((ng, 2 * ST_LANES, U_LANES), F32),
                   jax.ShapeDtypeStruct((ng, 8, ST_LANES), F32),
                   jax.ShapeDtypeStruct((ng, 8, U_LANES), F32), _ChipExchange.out_shape(xsrc, True)],
        scratch_shapes=[pltpu.VMEM((S, ST_LANES), F32)] * 2 + [pltpu.VMEM((S, U_LANES), F32)] + _ChipExchange.SCRATCH,
        compiler_params=_cp(("arbitrary", "arbitrary")),
    )(u, y, dgy, st, Wb, Wc, cr, d, xsrc)


TQ = 256
KW = 512
SUB = 128


def _head_masks():
    lane = lax.broadcasted_iota(jnp.int32, (1, 2 * HEAD_DIM), 1)
    m0 = (lane < HEAD_DIM).astype(F32)
    return m0, 1.0 - m0


def _head_norm(x, g, m0, m1):
    sq = x * x
    r0 = lax.rsqrt(jnp.sum(sq * m0, axis=-1, keepdims=True) / HEAD_DIM + EPS)
    r1 = lax.rsqrt(jnp.sum(sq * m1, axis=-1, keepdims=True) / HEAD_DIM + EPS)
    r = m0 * r0 + m1 * r1
    return x * r, r


def _head_norm_bwd(dy, n, r, g, m0, m1):
    dn = dy * g
    p = dn * n
    mean = (m0 * jnp.sum(p * m0, axis=-1, keepdims=True) + m1 * jnp.sum(p * m1, axis=-1, keepdims=True)) / HEAD_DIM
    return r * (dn - n * mean), _csum(dy * n)


def _pair_matrix(kind):
    r = lax.broadcasted_iota(jnp.int32, (2 * SUB, 2 * SUB), 0)
    c = lax.broadcasted_iota(jnp.int32, (2 * SUB, 2 * SUB), 1)
    same = (r < SUB) == (c < SUB)
    rel = {"after": r > c, "upto": r <= c, "before": r < c}[kind]
    return jnp.logical_and(same, rel).astype(BF16)


def _block_sums(x, mat, carry, reverse, terms=2):
    hi = x.astype(BF16)
    lo = (x - hi.astype(F32)).astype(BF16) if terms == 2 else None
    npair = x.shape[1] // (2 * SUB)
    parts = [None] * (2 * npair)
    for p in (range(npair - 1, -1, -1) if reverse else range(npair)):
        sl = slice(2 * SUB * p, 2 * SUB * (p + 1))
        loc = jnp.dot(hi[:, sl], mat, preferred_element_type=F32)
        if terms == 2:
            loc = loc + jnp.dot(lo[:, sl], mat, preferred_element_type=F32)
        for b in ((1, 0) if reverse else (0, 1)):
            k = 2 * p + b
            parts[k] = loc[:, SUB * b:SUB * (b + 1)] + carry
            carry = carry + jnp.sum(x[:, SUB * k:SUB * (k + 1)], axis=-1, keepdims=True)
    return jnp.concatenate(parts, axis=1), carry


def _sb_logits(z, mask):
    lp = jnp.minimum(z, 0.0) - jnp.log(1.0 + jnp.exp(-jnp.abs(z)))
    lf = lp - z
    if mask is not None:
        lf = jnp.where(mask, lf, 0.0)
    return lp, lf


def _causal_mask(row0, col0, kw):
    r = row0 + lax.broadcasted_iota(jnp.int32, (TQ, kw), 0)
    c = col0 + lax.broadcasted_iota(jnp.int32, (TQ, kw), 1)
    return c < r


def _transposed_windows(x, ref):
    for w in range(x.shape[0] // KW):
        ref[w] = x[w * KW:(w + 1) * KW, :].T.astype(BF16)


def _attn_fwd(q, kv, qg, kg, xsrc, *, n_ex, name):
    T, D = q.shape
    S = T // n_ex
    nhp = D // (2 * HEAD_DIM)
    nq = S // TQ
    scale = 1.0 / math.sqrt(HEAD_DIM)

    def body(q_ref, k_ref, v_ref, qg_ref, kg_ref, xsrc_ref, o_ref, tot_ref, ot_ref, xout_ref, kT_s, qm_s, vm_s, *sems):
        step = pl.program_id(0) * nhp + pl.program_id(1)
        exch = _ChipExchange(xsrc_ref, xout_ref, *sems, scatter=False)

        @pl.when(step == 0)
        def _():
            exch.start()

        m0, m1 = _head_masks()
        qn, _ = _head_norm(q_ref[...], None, m0, m1)
        qn = qn * (qg_ref[...] * scale)
        kn, _ = _head_norm(k_ref[...], None, m0, m1)
        _transposed_windows(kn * kg_ref[...], kT_s)
        v = v_ref[...]
        for h, m in enumerate((m0, m1)):
            qm_s[h] = (qn * m).astype(BF16)
            vm_s[h] = (v * m).astype(BF16)
        u_after = _pair_matrix("after")

        def window(rows, win, st, mask, kw):
            keys = pl.ds(pl.multiple_of(win * KW, KW), kw)
            zs = [jnp.dot(qm_s[h, rows, :], kT_s[win, :, :kw], preferred_element_type=F32) for h in range(2)]
            lg = [_sb_logits(zs[h], mask) for h in range(2)]
            sums = [_block_sums(lg[h][1], u_after, st[2 * h], True) for h in range(2)]
            out = ()
            for h in range(2):
                w = jnp.exp(lg[h][0] + sums[h][0])
                if mask is not None:
                    w = jnp.where(mask, w, 0.0)
                out += (sums[h][1], st[2 * h + 1] + jnp.dot(w.astype(BF16), vm_s[h, keys, :], preferred_element_type=F32))
            return out

        def qtile(iq, last, kw):
            rows = pl.ds(pl.multiple_of(iq * TQ, TQ), TQ)
            mask = _causal_mask(iq * TQ, last * KW, kw)
            z1, zq = jnp.zeros((TQ, 1), F32), jnp.zeros((TQ, 2 * HEAD_DIM), F32)
            st = window(rows, last, (z1, zq, z1, zq), mask, kw)
            st = lax.fori_loop(0, last, lambda jj, st: window(rows, last - 1 - jj, st, None, KW), st)
            o_ref[rows, :] = st[1] + st[3]
            tot_ref[rows, :] = st[0] * m0 + st[2] * m1

        def qtiles_of_window(a, _):
            for sub in range(KW // TQ):
                qtile(a * (KW // TQ) + sub, a, (sub + 1) * TQ)
            return 0

        lax.fori_loop(0, S // KW, qtiles_of_window, 0)
        ot_ref[...] = o_ref[...].T.astype(BF16)

        @pl.when(step == n_ex * nhp - 1)
        def _():
            exch.wait()

    assert S % KW == 0 and KW % TQ == 0
    nwin = S // KW
    blk = (S, 2 * HEAD_DIM)
    return pl.pallas_call(
        body, name=name, grid=(n_ex, nhp),
        in_specs=[pl.BlockSpec(blk, lambda e, h: (e, h)), pl.BlockSpec(blk, lambda e, h: (e, h)),
                  pl.BlockSpec(blk, lambda e, h: (e, h + nhp)),
                  pl.BlockSpec((1, 2 * HEAD_DIM), lambda e, h: (0, 0)), pl.BlockSpec((1, 2 * HEAD_DIM), lambda e, h: (0, 0)), ANY],
        out_specs=[pl.BlockSpec(blk, lambda e, h: (e, h))] * 2 + [pl.BlockSpec((2 * HEAD_DIM, S), lambda e, h: (h, e)), ANY],
        out_shape=[jax.ShapeDtypeStruct((T, D), F32)] * 2 + [jax.ShapeDtypeStruct((D, T), BF16), _ChipExchange.out_shape(xsrc, False)],
        scratch_shapes=[pltpu.VMEM((nwin, 2 * HEAD_DIM, KW), BF16), pltpu.VMEM((2,) + blk, BF16), pltpu.VMEM((2,) + blk, BF16)]
        + _ChipExchange.SCRATCH,
        compiler_params=_cp(("arbitrary", "arbitrary")),
    )(q, kv, kv, qg, kg, xsrc)


def _attn_bwd(q, kv, tot, do, qg, kg, *, n_ex, name):
    T, D = q.shape
    S = T // n_ex
    nhp = D // (2 * HEAD_DIM)
    nq = S // TQ
    scale = 1.0 / math.sqrt(HEAD_DIM)

    def body(q_ref, k_ref, v_ref, tot_ref, do_ref, qg_ref, kg_ref, dq_ref, dk_ref, dv_ref, dqg_ref, dkg_ref,
             kT_s, vT_s, km_s, qm_s, dom_s, dqn_s, dkT_s, dvT_s):
        m0, m1 = _head_masks()
        qn, qr = _head_norm(q_ref[...], None, m0, m1)
        kn, kr = _head_norm(k_ref[...], None, m0, m1)
        qs = qn * (qg_ref[...] * scale)
        kk = kn * kg_ref[...]
        _transposed_windows(kk, kT_s)
        _transposed_windows(v_ref[...], vT_s)
        do = do_ref[...]
        for h, m in enumerate((m0, m1)):
            qm_s[h] = (qs * m).astype(BF16)
            km_s[h] = (kk * m).astype(BF16)
            dom_s[h] = (do * m).astype(BF16)
        dkT_s[...] = jnp.zeros_like(dkT_s)
        dvT_s[...] = jnp.zeros_like(dvT_s)
        u_upto, u_before = _pair_matrix("upto"), _pair_matrix("before")

        def both(inv, win, st, mask, kw):
            keys = pl.ds(pl.multiple_of(win * KW, KW), kw)
            lg = [_sb_logits(jnp.dot(inv[h][0], kT_s[win, :, :kw], preferred_element_type=F32), mask) for h in range(2)]
            dw = [jnp.dot(inv[h][2], vT_s[win, :, :kw], preferred_element_type=F32) for h in range(2)]
            s_lf = [_block_sums(lg[h][1], u_upto, st[3 * h], False) for h in range(2)]
            ws, ews = [], []
            for h in range(2):
                w = jnp.exp(lg[h][0] + (inv[h][4] - s_lf[h][0]))
                if mask is not None:
                    w = jnp.where(mask, w, 0.0)
                ws.append(w)
                ews.append(dw[h] * w)
            s_e = [_block_sums(ews[h], u_before, st[3 * h + 1], False, terms=1) for h in range(2)]
            out, dk, dv = (), None, None
            for h in range(2):
                sig = jnp.exp(lg[h][0])
                dz = ews[h] * (1.0 - sig) - s_e[h][0] * sig
                if mask is not None:
                    dz = jnp.where(mask, dz, 0.0)
                dzb = dz.astype(BF16)
                out += (s_lf[h][1], s_e[h][1], st[3 * h + 2] + jnp.dot(dzb, km_s[h, keys, :], preferred_element_type=F32))
                dkh = jnp.dot(inv[h][1], dzb, preferred_element_type=F32)
                dvh = jnp.dot(inv[h][3], ws[h].astype(BF16), preferred_element_type=F32)
                dk, dv = (dkh, dvh) if h == 0 else (dk + dkh, dv + dvh)
            dkT_s[win, :, :kw] += dk
            dvT_s[win, :, :kw] += dv
            return out

        def qtile(iq, last, kw):
            rows = pl.ds(pl.multiple_of(iq * TQ, TQ), TQ)
            mask = _causal_mask(iq * TQ, last * KW, kw)
            tt = tot_ref[rows, :]
            inv = []
            for h, m in enumerate((m0, m1)):
                qh, doh = qm_s[h, rows, :], dom_s[h, rows, :]
                total = jnp.sum(tt * m, axis=-1, keepdims=True) * (1.0 / HEAD_DIM)
                inv.append((qh, qh.astype(F32).T.astype(BF16), doh, doh.astype(F32).T.astype(BF16), total))

            z1, zq = jnp.zeros((TQ, 1), F32), jnp.zeros((TQ, 2 * HEAD_DIM), F32)
            st = lax.fori_loop(0, last, lambda win, st: both(inv, win, st, None, KW), (z1, z1, zq, z1, z1, zq))
            st = both(inv, last, st, mask, kw)
            dqn_s[rows, :] = st[2] + st[5]

        def qtiles_of_window(a, _):
            for sub in range(KW // TQ):
                qtile(a * (KW // TQ) + sub, a, (sub + 1) * TQ)
            return 0

        lax.fori_loop(0, S // KW, qtiles_of_window, 0)
        dkn = jnp.concatenate([dkT_s[w].T for w in range(nwin)], axis=0)
        dq, dqg = _head_norm_bwd(dqn_s[...] * scale, qn, qr, qg_ref[...], m0, m1)
        dk, dkg = _head_norm_bwd(dkn, kn, kr, kg_ref[...], m0, m1)
        dq_ref[...] = dq
        dk_ref[...] = dk
        dv_ref[...] = jnp.concatenate([dvT_s[w].T for w in range(nwin)], axis=0)
        dqg_ref[...] = dqg
        dkg_ref[...] = dkg

    assert S % KW == 0 and KW % TQ == 0
    nwin = S // KW
    blk = (S, 2 * HEAD_DIM)
    tblk = (nwin, 2 * HEAD_DIM, KW)
    gblk = (None, None, 1, 2 * HEAD_DIM)
    dq, dk, dv, dqg, dkg = pl.pallas_call(
        body, name=name, grid=(n_ex, nhp),
        in_specs=[pl.BlockSpec(blk, lambda e, h: (e, h)), pl.BlockSpec(blk, lambda e, h: (e, h)),
                  pl.BlockSpec(blk, lambda e, h: (e, h + nhp)),
                  pl.BlockSpec(blk, lambda e, h: (e, h)), pl.BlockSpec(blk, lambda e, h: (e, h)),
                  pl.BlockSpec((1, 2 * HEAD_DIM), lambda e, h: (0, 0)), pl.BlockSpec((1, 2 * HEAD_DIM), lambda e, h: (0, 0))],
        out_specs=[pl.BlockSpec(blk, lambda e, h: (e, h))] * 3 + [pl.BlockSpec(gblk, lambda e, h: (e, h, 0, 0))] * 2,
        out_shape=[jax.ShapeDtypeStruct((T, D), F32)] * 3 + [jax.ShapeDtypeStruct((n_ex, nhp, 1, 2 * HEAD_DIM), F32)] * 2,
        scratch_shapes=[pltpu.VMEM(tblk, BF16), pltpu.VMEM(tblk, BF16),
                        pltpu.VMEM((2,) + blk, BF16), pltpu.VMEM((2,) + blk, BF16), pltpu.VMEM((2,) + blk, BF16),
                        pltpu.VMEM(blk, F32), pltpu.VMEM(tblk, F32), pltpu.VMEM(tblk, F32)],
        compiler_params=_cp(("parallel", "parallel")),
    )(q, kv, kv, tot, do, qg, kg)
    return dq, dk, dv, dqg, dkg


def _place():
    return lax.axis_index("x"), lax.axis_index("y"), lax.axis_index("c")


def _all_gather8(x_shard, *, name):
    m_per, n = x_shard.shape

    def body(x_ref, out_ref, send_sems, recv_sems, local_sem):
        x, y, c = _place()
        me, sibling = (x, y, c), (x, y, 1 - c)
        chips = [(1 - x, y), (x, 1 - y), (1 - x, 1 - y)]

        def rows(px, py, pc):
            return out_ref.at[pl.ds((4 * px + 2 * py + pc) * m_per, m_per), :]

        def copy(k, block, to, src=None):
            return pltpu.make_async_remote_copy(
                src_ref=rows(*block) if src is None else src, dst_ref=rows(*block),
                send_sem=send_sems.at[k], recv_sem=recv_sems.at[k], device_id=to, device_id_type=MESH)

        mine = pltpu.make_async_copy(x_ref, rows(*me), local_sem)
        mine.start()
        first = [copy(0, me, sibling, src=x_ref)]
        first += [copy(1 + j, me, (*chip, c), src=x_ref) for j, chip in enumerate(chips)]
        for cp in first:
            cp.start()
        passed = [copy(4 + j, (*chip, c), sibling) for j, chip in enumerate(chips)]
        for j, chip in enumerate(chips):
            copy(1 + j, (*chip, c), me).wait_recv()
            passed[j].start()
        copy(0, sibling, me).wait_recv()
        for j, chip in enumerate(chips):
            copy(4 + j, (*chip, 1 - c), me).wait_recv()
        for cp in first + passed:
            cp.wait_send()
        mine.wait()

    return pl.pallas_call(
        body, name=name, out_shape=jax.ShapeDtypeStruct((8 * m_per, n), x_shard.dtype),
        in_specs=[pl.BlockSpec(memory_space=pltpu.VMEM)], out_specs=pl.BlockSpec(memory_space=pltpu.VMEM),
        scratch_shapes=[pltpu.SemaphoreType.DMA((7,)), pltpu.SemaphoreType.DMA((7,)), pltpu.SemaphoreType.DMA],
        compiler_params=pltpu.CompilerParams(vmem_limit_bytes=VMEM_LIMIT),
    )(x_shard)


def _sibling_sum_half(x, *, name):
    R, C = x.shape
    half = R // 2

    def body(x_ref, o_ref, theirs, send_sem, recv_sem):
        px, py, pc = _place()
        cp = pltpu.make_async_remote_copy(src_ref=x_ref, dst_ref=theirs, send_sem=send_sem, recv_sem=recv_sem,
                                          device_id=(px, py, 1 - pc), device_id_type=MESH)
        cp.start()
        cp.wait()
        rows = pl.ds(pl.multiple_of(pc * half, 8), half)
        o_ref[...] = x_ref[rows, :] + theirs[rows, :]

    return pl.pallas_call(
        body, name=name, out_shape=jax.ShapeDtypeStruct((half, C), x.dtype),
        in_specs=[pl.BlockSpec(memory_space=pltpu.VMEM)], out_specs=pl.BlockSpec(memory_space=pltpu.VMEM),
        scratch_shapes=[pltpu.VMEM((R, C), x.dtype), pltpu.SemaphoreType.DMA, pltpu.SemaphoreType.DMA],
        compiler_params=pltpu.CompilerParams(vmem_limit_bytes=VMEM_LIMIT),
    )(x)


def _sum_blocks(x, n, *, name):
    R = x.shape[0] // n

    def body(x_ref, o_ref):
        acc = x_ref[pl.ds(0, R), :]
        for k in range(1, n):
            acc = acc + x_ref[pl.ds(k * R, R), :]
        o_ref[...] = acc

    return pl.pallas_call(body, name=name, out_shape=jax.ShapeDtypeStruct((R, x.shape[1]), x.dtype),
                          compiler_params=pltpu.CompilerParams(vmem_limit_bytes=VMEM_LIMIT))(x)


def _colsum(x, *, name):
    def body(x_ref, o_ref):
        o_ref[...] = jnp.sum(x_ref[...], axis=0, keepdims=True)
    return pl.pallas_call(body, name=name, out_shape=jax.ShapeDtypeStruct((1, x.shape[1]), x.dtype))(x)


ANY = pl.BlockSpec(memory_space=pl.ANY)


class _ChipExchange:
    SCRATCH = [pltpu.SemaphoreType.DMA((3,)), pltpu.SemaphoreType.DMA((3,)), pltpu.SemaphoreType.DMA]

    @staticmethod
    def out_shape(src, scatter):
        return jax.ShapeDtypeStruct(((4,) + tuple(src.shape[1:])) if scatter else ((4, 2) + tuple(src.shape[1:])), src.dtype)

    def __init__(self, src_ref, out_ref, send_sems, recv_sems, local_sem, scatter):
        x, y, c = _place()
        myj = 2 * x + y
        chips = [(1 - x, y), (x, 1 - y), (1 - x, 1 - y)]

        def slot(j):
            return out_ref.at[j] if scatter else out_ref.at[j, c]

        def piece(j):
            return src_ref.at[j] if scatter else src_ref.at[c]

        self.mine = pltpu.make_async_copy(piece(myj), slot(myj), local_sem)
        self.sends = [pltpu.make_async_remote_copy(
            src_ref=piece(2 * cx + cy), dst_ref=slot(myj), send_sem=send_sems.at[k], recv_sem=recv_sems.at[k],
            device_id=(cx, cy, c), device_id_type=MESH) for k, (cx, cy) in enumerate(chips)]
        self.recvs = [pltpu.make_async_remote_copy(
            src_ref=slot(2 * cx + cy), dst_ref=slot(2 * cx + cy), send_sem=send_sems.at[k], recv_sem=recv_sems.at[k],
            device_id=(cx, cy, c), device_id_type=MESH) for k, (cx, cy) in enumerate(chips)]

    def start(self):
        self.mine.start()
        for cp in self.sends:
            cp.start()

    def wait(self):
        for cp in self.recvs:
            cp.wait_recv()
        for cp in self.sends:
            cp.wait_send()
        self.mine.wait()


def _sibling_fill(buf, *, axis, name):
    def half(ref, h):
        return ref.at[h] if axis == 0 else ref.at[:, h]

    def body(in_ref, out_ref, send_sem, recv_sem):
        x, y, c = _place()
        cp = pltpu.make_async_remote_copy(src_ref=half(out_ref, c), dst_ref=half(out_ref, c), send_sem=send_sem, recv_sem=recv_sem,
                                          device_id=(x, y, 1 - c), device_id_type=MESH)
        cp.start()
        pltpu.make_async_remote_copy(src_ref=half(out_ref, 1 - c), dst_ref=half(out_ref, 1 - c), send_sem=send_sem, recv_sem=recv_sem,
                                     device_id=(x, y, 1 - c), device_id_type=MESH).wait_recv()
        cp.wait_send()

    return pl.pallas_call(
        body, name=name, out_shape=jax.ShapeDtypeStruct(buf.shape, buf.dtype), in_specs=[ANY], out_specs=ANY,
        input_output_aliases={0: 0}, scratch_shapes=[pltpu.SemaphoreType.DMA, pltpu.SemaphoreType.DMA],
    )(buf)


def _sibling_swap_half(g, *, name):
    def body(g_ref, out_ref, send_sem, recv_sem):
        x, y, c = _place()
        cp = pltpu.make_async_remote_copy(src_ref=g_ref.at[:, 1 - c], dst_ref=out_ref, send_sem=send_sem, recv_sem=recv_sem,
                                          device_id=(x, y, 1 - c), device_id_type=MESH)
        cp.start()
        cp.wait()

    return pl.pallas_call(
        body, name=name, out_shape=jax.ShapeDtypeStruct((g.shape[0],) + g.shape[2:], g.dtype), in_specs=[ANY], out_specs=ANY,
        scratch_shapes=[pltpu.SemaphoreType.DMA, pltpu.SemaphoreType.DMA],
    )(g)


def _add_my_half(g, b, cidx, *, name, tr=256):
    n, _, R, C = g.shape
    tr = math.gcd(tr, R)

    def body(c_ref, g_ref, b_ref, o_ref):
        o_ref[...] = (g_ref[...] + b_ref[...]).astype(o_ref.dtype)

    return pl.pallas_call(
        body, name=name, out_shape=jax.ShapeDtypeStruct((n, R, C), BF16),
        grid_spec=pltpu.PrefetchScalarGridSpec(
            num_scalar_prefetch=1, grid=(n, R // tr),
            in_specs=[pl.BlockSpec((None, None, tr, C), lambda j, i, c: (j, c[0], i, 0)),
                      pl.BlockSpec((None, tr, C), lambda j, i, c: (j, i, 0))],
            out_specs=pl.BlockSpec((None, tr, C), lambda j, i, c: (j, i, 0))),
        compiler_params=_cp(("parallel", "parallel")),
    )(cidx, g, b)


def _sum4_into_half(q, cidx, *, name, tr=256):
    _, R, C = q.shape
    tr = math.gcd(tr, R)

    def body(c_ref, q_ref, o_ref):
        o_ref[...] = ((q_ref[0].astype(F32) + q_ref[1].astype(F32)) + q_ref[2].astype(F32)) + q_ref[3].astype(F32)

    return pl.pallas_call(
        body, name=name, out_shape=jax.ShapeDtypeStruct((2, R, C), F32),
        grid_spec=pltpu.PrefetchScalarGridSpec(
            num_scalar_prefetch=1, grid=(R // tr,),
            in_specs=[pl.BlockSpec((4, tr, C), lambda i, c: (0, i, 0))],
            out_specs=pl.BlockSpec((None, tr, C), lambda i, c: (c[0], i, 0))),
        compiler_params=_cp(("parallel",)),
    )(cidx, q)


def _pack_rows(parts, width=1024):
    rows, spans, r0 = [], [], 0
    for p in parts:
        n = p.size
        nr = 8 * (-(-n // (8 * width)))
        flat = p.reshape(-1)
        if nr * width != n:
            flat = jnp.pad(flat, (0, nr * width - n))
        rows.append(flat.reshape(nr, width))
        spans.append((r0, nr, n, p.shape))
        r0 += nr
    return jnp.concatenate(rows, axis=0), spans


def _unpack_rows(buf, spans):
    return [buf[r0:r0 + nr].reshape(-1)[:n].reshape(shape) for (r0, nr, n, shape) in spans]


def kernel(x, c, ada_w, ada_b, mix_norm_g, mlp_norm_g, mlp_w1, mlp_w2, s5_a_re, s5_a_im, s5_log_dt, s5_b_re, s5_b_im, s5_c_re, s5_c_im, s5_d, s5_w_glu, kv_ada_w, kv_ada_b, kv_norm_g, w_kv, k_norm_g, sb_w_q, q_norm_g, sb_w_o, loss_target, m_ada_w, m_ada_b, m_mix_norm_g, m_mlp_norm_g, m_mlp_w1, m_mlp_w2, m_s5_a_re, m_s5_a_im, m_s5_log_dt, m_s5_b_re, m_s5_b_im, m_s5_c_re, m_s5_c_im, m_s5_d, m_s5_w_glu, m_kv_ada_w, m_kv_ada_b, m_kv_norm_g, m_w_kv, m_k_norm_g, m_sb_w_q, m_q_norm_g, m_sb_w_o, v_ada_w, v_ada_b, v_mix_norm_g, v_mlp_norm_g, v_mlp_w1, v_mlp_w2, v_s5_a_re, v_s5_a_im, v_s5_log_dt, v_s5_b_re, v_s5_b_im, v_s5_c_re, v_s5_c_im, v_s5_d, v_s5_w_glu, v_kv_ada_w, v_kv_ada_b, v_kv_norm_g, v_w_kv, v_k_norm_g, v_sb_w_q, v_q_norm_g, v_sb_w_o):
    E, S, D = x.shape
    T = E * S
    FF = 4 * D
    NB = 8 * E
    px, py, pc = _place()
    chip = 2 * px + py
    dev = 4 * px + 2 * py + pc
    cidx = jnp.reshape(pc, (1,)).astype(jnp.int32)
    x0 = x.reshape(T, D)
    tgt = loss_target.reshape(T, D)

    c_all = _all_gather8(c.reshape(-1, 128), name="ag_c").reshape(NB, D)
    sc_all = (c_all * _sigmoid(c_all)).astype(BF16)
    wa = ada_w.shape[2]
    wk = kv_ada_w.shape[1]
    m_sh = jnp.concatenate([_mm(sc_all, ada_w[0], "nn", name="ada0", tn=256),
                            _mm(sc_all, ada_w[1], "nn", name="ada1", tn=256),
                            _mm(sc_all, kv_ada_w, "nn", name="ada_kv", tn=256)], axis=1)
    m_all = _all_gather8(m_sh, name="ag_m").reshape(4, 2, NB, 2 * wa + wk)[:, 0]
    mods = []
    for l in range(2):
        full = jnp.transpose(m_all[:, :, l * wa:(l + 1) * wa], (1, 0, 2)).reshape(NB, 6 * D) + ada_b[l]
        mine = lax.dynamic_slice_in_dim(full, E * dev, E, axis=0)
        mods.append([mine[:, i * D:(i + 1) * D].reshape(E, 1, D) for i in range(6)])
    full = jnp.transpose(m_all[:, :, 2 * wa:], (1, 0, 2)).reshape(NB, 2 * D) + kv_ada_b
    mine = lax.dynamic_slice_in_dim(full, E * dev, E, axis=0)
    kv_sh, kv_sc = [mine[:, i * D:(i + 1) * D].reshape(E, 1, D) for i in range(2)]

    wpack_a = jnp.concatenate([mlp_w1[0], mlp_w2[0], jnp.concatenate([s5_w_glu[0], w_kv], axis=1), sb_w_q[0]], axis=0).astype(BF16)
    wpack_b = jnp.concatenate([mlp_w1[1], mlp_w2[1], sb_w_o[0]], axis=0).astype(BF16)
    RA, RB = wpack_a.shape[0], wpack_b.shape[0]
    RW = RA + RB

    tm = min(1024, S)
    gbuf = [jax.ShapeDtypeStruct((4, RW, D), F32)]

    def grad_mm(act, dout, kind, roff, nr, c0, nc, name, transposed=False):
        gbuf[0] = _mm(act, dout, "nn" if transposed else "tn", name=name, tk=2048, into=_Sharded(gbuf[0], kind, roff, nr, c0, nc))

    def mlp_fwd(xa, l, mod):
        sh_m, sc_m, g_m = mod[3], mod[4], mod[5]
        h, h_t = _norm_mod_fwd(xa, mlp_norm_g[l:l + 1], sh_m, sc_m, n_ex=E, out_dtype=BF16, name=f"mlp_norm{l}", with_transpose=True)

        def relu_sq(acc):
            ra = jnp.maximum(acc, 0.0)
            return ra * ra, ra
        r, ra = _mm(h, W1[l], "nn", name=f"mlp_up{l}", out_dtypes=(BF16, BF16), tm=tm, epilogue=relu_sq)
        xb, ff = _mm(r, W2[l], "nn", name=f"mlp_down{l}", out_dtypes=(F32, F32), tm=tm,
                     extras=[_mn_extra(xa), _vec_extra(g_m, S)],
                     epilogue=lambda acc, xat, gt: (xat + gt * acc, acc))
        return xb, (h_t, r, ra, ff)

    def mlp_bwd(dxb, xa, l, mod, saved):
        sc_m, g_m = mod[4], mod[5]
        h_t, r, ra, ff = saved
        (dff,), (dgm,) = _rowwise(lambda d, f, g: ([g * d], [_csum(d * f)]), [(dxb, D, 0), (ff, D, 0)], [g_m], [],
                                  [(D, BF16)], [D], n_ex=E, name=f"mlp_gate_bwd{l}")
        da = _mm(dff, W2[l], "nt", name=f"mlp_down_dx{l}", out_dtypes=(BF16,), tm=tm, extras=[_mn_extra(ra)],
                 epilogue=lambda acc, rat: (acc * (2.0 * rat.astype(F32)),))
        grad_mm(r, dff, "rows", (2 + l) * D, D, 0, D, f"mlp_down_dw{l}")
        dh = _mm(da, W1[l], "nt", name=f"mlp_up_dx{l}", tm=tm)
        grad_mm(h_t, da, "cols", l * D, D, 0, D, f"mlp_up_dw{l}", transposed=True)
        (dxa,), (dsh, dsc, dg) = _norm_mod_bwd(xa, dh, dxb, mlp_norm_g[l:l + 1], sc_m, n_ex=E, name=f"mlp_norm_bwd{l}")
        return dxa, (dsh, dsc, dgm), dg

    ab_re, ab_im, bb_re, bb_im = _s5_disc(s5_a_re[0], s5_a_im[0], s5_log_dt[0], s5_b_re[0], s5_b_im[0])
    cf, cr = _s5_consts(ab_re, ab_im)
    Wb, Wc = _s5_blockdiag(bb_re, bb_im, s5_c_re[0], s5_c_im[0])
    ng = D // U_LANES
    nd = s5_d.size // 128
    d_full = _all_gather8(jnp.pad(s5_d.reshape(nd, 128), ((0, 8 - nd), (0, 0))), name="ag_d")
    d_full = d_full.reshape(4, 2, 8, 128)[:, 0, :nd].reshape(1, D)

    mod0, mod1 = mods
    h0 = _norm_mod_fwd(x0, mix_norm_g[0:1], mod0[0], mod0[1], n_ex=E, out_dtype=F32, name="mix_norm0")
    y, gy, gy_t, s5_states, wfull_a = _s5_fwd(h0, Wb, Wc, cf, d_full, wpack_a.reshape(2, RA // 2, D), n_ex=E, name="s5_fwd")
    wfull_a = _sibling_fill(wfull_a, axis=1, name="wgather_a_d2d").reshape(4, RA, D)

    W1 = [_Sharded(wfull_a, "cols", 0, D, 0, D), None]
    W2 = [_Sharded(wfull_a, "rows", D, D, 0, D), None]
    Wglu = _Sharded(wfull_a, "cols", 2 * D, D, 0, D // 2)
    Wkv = _Sharded(wfull_a, "cols", 2 * D, D, D // 2, D // 2)
    Wq = _Sharded(wfull_a, "rows", 3 * D, D // 4, 0, D)
    vg = _mm(gy, Wglu, "nn", name="glu_up", tm=tm)
    (x1,), _ = _rowwise(lambda v, g, xt, ga: ([xt + ga * (v * _sigmoid(g))], []),
                        [(vg, D, 0), (vg, D, 1), (x0, D, 0)], [mod0[2]], [], [(D, F32)], [], n_ex=E, name="glu_gate")
    x2, saved_mlp0 = mlp_fwd(x1, 0, mod0)

    hkv, hkv_t = _norm_mod_fwd(x2, kv_norm_g.reshape(1, D), kv_sh, kv_sc, n_ex=E, out_dtype=BF16, name="kv_norm", with_transpose=True)
    kvf = _mm(hkv, Wkv, "nn", name="kv_proj", tm=tm)
    h1, h1_t = _norm_mod_fwd(x2, mix_norm_g[1:2], mod1[0], mod1[1], n_ex=E, out_dtype=BF16, name="mix_norm1", with_transpose=True)
    qf = _mm(h1, Wq, "nn", name="q_proj", tm=tm)
    qg2 = jnp.tile(q_norm_g.reshape(1, HEAD_DIM), (1, 2))
    kg2 = jnp.tile(k_norm_g.reshape(1, HEAD_DIM), (1, 2))
    o, lf_tot, o_t, wfull_b = _attn_fwd(qf, kvf, qg2, kg2, wpack_b.reshape(2, RB // 2, D), n_ex=E, name="attn_fwd")
    wfull_b = _sibling_fill(wfull_b, axis=1, name="wgather_b_d2d").reshape(4, RB, D)
    W1[1] = _Sharded(wfull_b, "cols", 0, D, 0, D)
    W2[1] = _Sharded(wfull_b, "rows", D, D, 0, D)
    Wo = _Sharded(wfull_b, "rows", 2 * D, D // 4, 0, D)
    x3, mix1 = _mm(o, Wo, "nn", name="o_proj", out_dtypes=(F32, F32), tm=tm,
                   extras=[_mn_extra(x2), _vec_extra(mod1[2], S)],
                   epilogue=lambda acc, xat, gt: (xat + gt * acc, acc))
    x4, saved_mlp1 = mlp_fwd(x3, 1, mod1)

    (dx4,), (lsum,) = _rowwise(lambda xt, tt: ([(xt - tt) * (1.0 / D)], [_csum(jnp.square(xt - tt)) * (0.5 / D)]),
                               [(x4, D, 0), (tgt, D, 0)], [], [], [(D, F32)], [D], n_ex=E, name="loss")
    loss = lax.psum(jnp.sum(lsum), ("x", "y", "c"))

    dx3, (dsh_m1, dsc_m1, dgm1), dg_mlp1 = mlp_bwd(dx4, x3, 1, mod1, saved_mlp1)
    (dmix1,), (dga1,) = _rowwise(lambda d, f, g: ([g * d], [_csum(d * f)]), [(dx3, D, 0), (mix1, D, 0)], [mod1[2]], [],
                                 [(D, BF16)], [D], n_ex=E, name="attn_gate_bwd")
    do = _mm(dmix1, Wo, "nt", name="o_proj_dx", tm=tm)
    grad_mm(o_t, dmix1, "rows", 5 * D + D // 4, D // 4, 0, D, "o_proj_dw", transposed=True)
    dq, dk, dv, dqg, dkg = _attn_bwd(qf, kvf, lf_tot, do, qg2, kg2, n_ex=E, name="attn_bwd")
    dh1 = _mm(dq, Wq, "nt", name="q_proj_dx", tm=tm)
    grad_mm(h1_t, dq, "rows", 5 * D, D // 4, 0, D, "q_proj_dw", transposed=True)
    (dx2,), (dsh_a1, dsc_a1, dg_mix1) = _norm_mod_bwd(x2, dh1, dx3, mix_norm_g[1:2], mod1[1], n_ex=E, name="mix_norm_bwd1")
    dkv = jnp.concatenate([dk, dv], axis=1)
    dhkv = _mm(dkv, Wkv, "nt", name="kv_proj_dx", tm=tm)
    grad_mm(hkv_t, dkv, "cols", 4 * D, D, D // 2, D // 2, "kv_proj_dw", transposed=True)
    (dx2,), (dkv_sh, dkv_sc, dg_kv) = _norm_mod_bwd(x2, dhkv, dx2, kv_norm_g.reshape(1, D), kv_sc, n_ex=E, name="kv_norm_bwd")

    dx1, (dsh_m0, dsc_m0, dgm0), dg_mlp0 = mlp_bwd(dx2, x1, 0, mod0, saved_mlp0)

    def glu_bwd(v, g, d, ga):
        sg = _sigmoid(g)
        dm = ga * d
        return [jnp.concatenate([dm * sg, dm * v * sg * (1.0 - sg)], axis=1)], [_csum(d * (v * sg))]
    (dvg,), (dga0,) = _rowwise(glu_bwd, [(vg, D, 0), (vg, D, 1), (dx1, D, 0)], [mod0[2]], [], [(2 * D, BF16)], [D],
                               n_ex=E, name="glu_gate_bwd")
    dgy = _mm(dvg, Wglu, "nt", name="glu_up_dx", tm=tm)
    grad_mm(gy_t, dvg, "cols", 4 * D, D, 0, D // 2, "glu_up_dw", transposed=True)

    gpack = gbuf[0].reshape(4, 2, RW // 2, D)
    theirs = _sibling_swap_half(gpack, name="gscatter_d2d")
    chip_sum = _add_my_half(gpack, theirs, cidx, name="gscatter_add")
    dh0, dWb, dWc, dab, dd, from_chips = _s5_bwd(h0, y, dgy, s5_states, Wb, Wc, cr, d_full, chip_sum, n_ex=E, name="s5_bwd")
    ghalf = _sum4_into_half(from_chips, cidx, name="gscatter_sum")
    gsh = _sibling_fill(ghalf, axis=0, name="gscatter_fill").reshape(RW, D)
    (gx,), (dsh_a0, dsc_a0, dg_mix0) = _norm_mod_bwd(x0, dh0, dx1, mix_norm_g[0:1], mod0[1], n_ex=E, name="mix_norm_bwd0")
    grad_x = gx.reshape(E, S, D)

    dm_mine = jnp.concatenate([t.reshape(E, D) for t in
                               (dsh_a0, dsc_a0, dga0, dsh_m0, dsc_m0, dgm0, dsh_a1, dsc_a1, dga1, dsh_m1, dsc_m1, dgm1, dkv_sh, dkv_sc)], axis=1)
    dm_all = _all_gather8(dm_mine.reshape(8, -1), name="ag_dm").reshape(NB, 14 * D)
    sc_f32 = c_all * _sigmoid(c_all)
    g_ada_w = jnp.stack([_mm(sc_f32, lax.dynamic_slice_in_dim(dm_all, l * 6 * D + chip * wa, wa, axis=1), "tn", name=f"ada_dw{l}", tn=256)
                         for l in range(2)])
    g_kv_ada_w = _mm(sc_f32, lax.dynamic_slice_in_dim(dm_all, 12 * D + chip * wk, wk, axis=1), "tn", name="ada_kv_dw", tn=256)
    db_all = _colsum(dm_all, name="ada_db")
    g_ada_b = db_all[0, :12 * D].reshape(2, 6 * D)
    g_kv_ada_b = db_all[0, 12 * D:]

    dWb_re, dWb_im, dC_re, dC_im = _s5_unblock(dWb, dWc)
    small_parts = [dg_mix0.sum(0), dg_mix1.sum(0), dg_mlp0.sum(0), dg_mlp1.sum(0), dg_kv.sum(0),
                   dqg.sum((0, 1, 2)).reshape(2, HEAD_DIM).sum(0), dkg.sum((0, 1, 2)).reshape(2, HEAD_DIM).sum(0),
                   dd[:, 0, :], dab[:, 0, :], dab[:, 1, :], dWb_re, dWb_im, dC_re, dC_im]
    spack, spans = _pack_rows(small_parts)
    chip_half = _sibling_sum_half(spack, name="small_d2d")
    ssum = _sum_blocks(_all_gather8(chip_half, name="ag_small"), 4, name="sum_small")
    (g_mix0, g_mix1, g_mlp0, g_mlp1, g_kvn, g_qn, g_kn, g_d, g_abr, g_abi, g_bbr, g_bbi, g_cre, g_cim) = _unpack_rows(ssum, spans)
    _, disc_vjp = jax.vjp(_s5_disc, s5_a_re[0], s5_a_im[0], s5_log_dt[0], s5_b_re[0], s5_b_im[0])
    g_are, g_aim, g_ldt, g_bre, g_bim = disc_vjp((g_abr.reshape(ab_re.shape), g_abi.reshape(ab_im.shape), g_bbr, g_bbi))
    g_s5d = lax.dynamic_slice_in_dim(g_d.reshape(1, D), chip * s5_d.shape[1], s5_d.shape[1], axis=1)

    def upd_big(w, m, v, roff, cb, name):
        shape = w.shape
        W = shape[-1]
        d_, m_, v_, g_ = _adamw2d(w.reshape(-1, W), gsh, m.reshape(-1, W), v.reshape(-1, W), name=name, g_roff=roff, g_cb=cb)
        return [t.reshape(shape) for t in (g_, d_, m_, v_)]

    def upd_own(w, g, m, v, name):
        shape = w.shape
        W = shape[-1]
        d_, m_, v_, g_ = _adamw2d(w.reshape(-1, W), g.reshape(-1, W), m.reshape(-1, W), v.reshape(-1, W), name=name)
        return [t.reshape(shape) for t in (g_, d_, m_, v_)]

    res = {}
    res["ada_w"] = upd_own(ada_w, g_ada_w, m_ada_w, v_ada_w, "adam_ada_w")
    res["kv_ada_w"] = upd_own(kv_ada_w, g_kv_ada_w, m_kv_ada_w, v_kv_ada_w, "adam_kv_ada_w")
    res["mlp_w1"] = upd_big(mlp_w1, m_mlp_w1, v_mlp_w1, 0, 0, "adam_w1")
    res["mlp_w2"] = upd_big(mlp_w2, m_mlp_w2, v_mlp_w2, 2 * D, 0, "adam_w2")
    res["s5_w_glu"] = upd_big(s5_w_glu, m_s5_w_glu, v_s5_w_glu, 4 * D, 0, "adam_glu")
    res["w_kv"] = upd_big(w_kv, m_w_kv, v_w_kv, 4 * D, 1, "adam_wkv")
    res["sb_w_q"] = upd_big(sb_w_q, m_sb_w_q, v_sb_w_q, 5 * D, 0, "adam_wq")
    res["sb_w_o"] = upd_big(sb_w_o, m_sb_w_o, v_sb_w_o, 5 * D + D // 4, 0, "adam_wo")

    small = {
        "ada_b": (ada_b, g_ada_b, m_ada_b, v_ada_b),
        "mix_norm_g": (mix_norm_g, jnp.stack([g_mix0, g_mix1]), m_mix_norm_g, v_mix_norm_g),
        "mlp_norm_g": (mlp_norm_g, jnp.stack([g_mlp0, g_mlp1]), m_mlp_norm_g, v_mlp_norm_g),
        "s5_a_re": (s5_a_re, g_are[None], m_s5_a_re, v_s5_a_re),
        "s5_a_im": (s5_a_im, g_aim[None], m_s5_a_im, v_s5_a_im),
        "s5_log_dt": (s5_log_dt, g_ldt[None], m_s5_log_dt, v_s5_log_dt),
        "s5_b_re": (s5_b_re, g_bre[None], m_s5_b_re, v_s5_b_re),
        "s5_b_im": (s5_b_im, g_bim[None], m_s5_b_im, v_s5_b_im),
        "s5_c_re": (s5_c_re, g_cre[None], m_s5_c_re, v_s5_c_re),
        "s5_c_im": (s5_c_im, g_cim[None], m_s5_c_im, v_s5_c_im),
        "s5_d": (s5_d, g_s5d, m_s5_d, v_s5_d),
        "kv_ada_b": (kv_ada_b, g_kv_ada_b, m_kv_ada_b, v_kv_ada_b),
        "kv_norm_g": (kv_norm_g, g_kvn, m_kv_norm_g, v_kv_norm_g),
        "k_norm_g": (k_norm_g, g_kn, m_k_norm_g, v_k_norm_g),
        "q_norm_g": (q_norm_g, g_qn.reshape(q_norm_g.shape), m_q_norm_g, v_q_norm_g),
    }
    names = list(small)
    packs = [_pack_rows([small[n][i].reshape(small[n][0].shape) for n in names]) for i in range(4)]
    sp = packs[0][1]
    d_, m_, v_, g_ = _adamw2d(packs[0][0], packs[1][0], packs[2][0], packs[3][0], name="adam_small")
    for n, gg, dd_, mm_, vv_ in zip(names, _unpack_rows(g_, sp), _unpack_rows(d_, sp), _unpack_rows(m_, sp), _unpack_rows(v_, sp)):
        res[n] = [gg, dd_, mm_, vv_]

    order = ["ada_w", "ada_b", "mix_norm_g", "mlp_norm_g", "mlp_w1", "mlp_w2", "s5_a_re", "s5_a_im", "s5_log_dt", "s5_b_re", "s5_b_im",
             "s5_c_re", "s5_c_im", "s5_d", "s5_w_glu", "kv_ada_w", "kv_ada_b", "kv_norm_g", "w_kv", "k_norm_g", "sb_w_q", "q_norm_g", "sb_w_o"]
    return (loss, grad_x, *[res[n][0] for n in order], *[res[n][1] for n in order], *[res[n][2] for n in order], *[res[n][3] for n in order])
```

```python
import functools
import math

import jax
import jax.numpy as jnp
from jax import lax
from jax.experimental import pallas as pl
from jax.experimental.pallas import tpu as pltpu

F32 = jnp.float32
BF16 = jnp.bfloat16
EPS = 1e-6
HEAD_DIM = 64
S5_GROUP = 16
S5_STATE = 64
GROUPS_PER_STEP = 8
U_LANES = GROUPS_PER_STEP * S5_GROUP
ST_LANES = GROUPS_PER_STEP * S5_STATE
SCAN_LANES = 256
SCAN_UNROLL = 4
VMEM_LIMIT = 56 * 1024 * 1024
ADAM_LR, ADAM_B1, ADAM_B2, ADAM_EPS, ADAM_WD, ADAM_STEP = 0.001, 0.9, 0.999, 1e-08, 0.01, 10
MESH = pl.DeviceIdType.MESH


def _cp(sem):
    return pltpu.CompilerParams(dimension_semantics=sem, vmem_limit_bytes=VMEM_LIMIT)


class _Sharded:
    def __init__(self, buf, kind, roff, nr, c0, nc):
        self.buf, self.kind, self.roff, self.nr, self.c0, self.nc = buf, kind, roff, nr, c0, nc
        self.shape = (nr, 4 * nc) if kind == "cols" else (4 * nr, nc)

    def operand(self, dims, tn, tk):
        roff, nr, c0, nc = self.roff, self.nr, self.c0, self.nc
        if self.kind == "cols" and dims == "nn":
            tk = min(tk, nr)
            assert roff % tk == 0
            return nc, tk, (None, tk, nc), lambda i, j, k: (j, roff // tk + k, c0 // nc)
        if self.kind == "cols":
            tn = min(tn, nr)
            assert roff % tn == 0
            return tn, nc, (None, tn, nc), lambda i, j, k: (k, roff // tn + j, c0 // nc)
        if dims == "nn":
            tn = min(tn, nc)
            assert roff % nr == 0 and c0 % tn == 0
            return tn, nr, (None, nr, tn), lambda i, j, k: (k, roff // nr, c0 // tn + j)
        tk = min(tk, nc)
        assert roff % nr == 0 and c0 % tk == 0
        return nr, tk, (None, nr, tk), lambda i, j, k: (j, roff // nr, c0 // tk + k)

    def result(self, tm, tn):
        roff, nr, c0, nc = self.roff, self.nr, self.c0, self.nc
        if self.kind == "cols":
            tm = min(tm, nr)
            assert roff % tm == 0
            return tm, nc, (None, tm, nc), lambda i, j, k: (j, roff // tm + i, c0 // nc)
        tm, tn = min(tm, nr), min(tn, nc)
        assert roff % tm == 0 and c0 % tn == 0
        per = nr // tm
        return tm, tn, (None, tm, tn), lambda i, j, k: (i // per, roff // tm + i % per, c0 // tn + j)


def _mm(a, b, dims, *, name, out_dtypes=(F32,), epilogue=None, extras=(), tm=512, tn=1024, tk=1024, into=None):
    bshape = b.shape
    if dims == "nn":
        (M, K), (_, N) = a.shape, bshape
    elif dims == "nt":
        (M, K), (N, _) = a.shape, bshape
    else:
        (K, M), (_, N) = a.shape, bshape
    tm, tn, tk = min(tm, M), min(tn, N), min(tk, K)
    b_arr = b
    if into is not None:
        assert (M, N) == into.shape and len(out_dtypes) == 1 and not isinstance(b, _Sharded)
        tm, tn, o_blk, o_map = into.result(tm, tn)
        out_specs, out_shape = [pl.BlockSpec(o_blk, o_map)], [jax.ShapeDtypeStruct(into.buf.shape, into.buf.dtype)]
    if isinstance(b, _Sharded):
        tn, tk, b_blk, b_map = b.operand(dims, tn, tk)
        b_spec, b_arr = pl.BlockSpec(b_blk, b_map), b.buf
    else:
        b_spec = pl.BlockSpec((tn, tk), lambda i, j, k: (j, k)) if dims == "nt" else pl.BlockSpec((tk, tn), lambda i, j, k: (k, j))
    if into is None:
        out_specs = [pl.BlockSpec((tm, tn), lambda i, j, k: (i, j)) for _ in out_dtypes]
        out_shape = [jax.ShapeDtypeStruct((M, N), d) for d in out_dtypes]
    assert M % tm == 0 and N % tn == 0 and K % tk == 0, (M, N, K, tm, tn, tk)
    nk = K // tk
    extras = [e(tm, tn) for e in extras]
    a_spec = pl.BlockSpec((tk, tm), lambda i, j, k: (k, i)) if dims == "tn" else pl.BlockSpec((tm, tk), lambda i, j, k: (i, k))
    contract = {"nn": ((1,), (0,)), "nt": ((1,), (1,)), "tn": ((0,), (0,))}[dims]
    n_ex, n_out = len(extras), len(out_dtypes)
    chain = [into.buf] if into is not None and not isinstance(into.buf, jax.ShapeDtypeStruct) else []
    n_in = n_ex + len(chain)

    def finish(r, ex, outs):
        res = epilogue(r, *[e[...] for e in ex]) if epilogue is not None else (r,)
        for o, v in zip(outs, res):
            o[...] = v.astype(o.dtype)

    def product(a_ref, b_ref):
        return lax.dot_general(a_ref[...].astype(BF16), b_ref[...].astype(BF16), (contract, ((), ())), preferred_element_type=F32)

    def body_one(a_ref, b_ref, *rest):
        finish(product(a_ref, b_ref), rest[:n_ex], rest[n_in:])

    def body_acc(a_ref, b_ref, *rest):
        ex, outs, acc = rest[:n_ex], rest[n_in:n_in + n_out], rest[-1]
        k = pl.program_id(2)

        @pl.when(k == 0)
        def _():
            acc[...] = product(a_ref, b_ref)

        @pl.when(jnp.logical_and(k > 0, k < nk - 1))
        def _():
            acc[...] += product(a_ref, b_ref)

        @pl.when(k == nk - 1)
        def _():
            finish(acc[...] + product(a_ref, b_ref), ex, outs)

    out = pl.pallas_call(
        body_one if nk == 1 else body_acc, name=name, grid=(M // tm, N // tn, nk),
        in_specs=[a_spec, b_spec] + [pl.BlockSpec(blk, im) for (_, blk, im) in extras] + [ANY for _ in chain],
        out_specs=out_specs, out_shape=out_shape,
        input_output_aliases={2 + n_ex: 0} if chain else {},
        scratch_shapes=[] if nk == 1 else [pltpu.VMEM((tm, tn), F32)],
        compiler_params=_cp(("parallel", "parallel", "arbitrary")),
    )(a, b_arr, *[e[0] for e in extras], *chain)
    return out if n_out > 1 else out[0]


def _mn_extra(arr):
    return lambda tm, tn: (arr, (tm, tn), lambda i, j, k: (i, j))


def _vec_extra(vec, S):
    return lambda tm, tn: (vec, (None, 1, tn), lambda i, j, k: ((i * tm) // S, 0, j))


def _rowwise(fn, rows, vecs=(), consts=(), out_rows=(), out_sums=(), *, n_ex, name, tr=512):
    rows = [r if len(r) == 4 else (*r, 0) for r in rows]
    S = min(r[0].shape[0] for r in rows if r[3] == 0) // n_ex
    tr = math.gcd(tr, S)
    assert S % tr == 0
    nb = S // tr
    in_specs = []
    for (arr, w, cb, roff) in rows:
        assert roff % tr == 0
        in_specs.append(pl.BlockSpec((tr, w), functools.partial(lambda e, i, cb, ro: (e * nb + i + ro, cb), cb=cb, ro=roff // tr)))
    for v in vecs:
        in_specs.append(pl.BlockSpec((None, 1, v.shape[-1]), lambda e, i: (e, 0, 0)))
    for c in consts:
        in_specs.append(pl.BlockSpec((1, c.shape[-1]), lambda e, i: (0, 0)))
    n_in, n_or, n_os = len(in_specs), len(out_rows), len(out_sums)
    flipped = [len(o) == 3 and o[2] for o in out_rows]
    out_specs = [pl.BlockSpec((o[0], tr), lambda e, i: (0, e * nb + i)) if f else pl.BlockSpec((tr, o[0]), lambda e, i: (e * nb + i, 0))
                 for o, f in zip(out_rows, flipped)]
    out_specs += [pl.BlockSpec((None, 1, w), lambda e, i: (e, 0, 0)) for w in out_sums]
    out_shape = [jax.ShapeDtypeStruct((o[0], n_ex * S) if f else (n_ex * S, o[0]), o[1]) for o, f in zip(out_rows, flipped)]
    out_shape += [jax.ShapeDtypeStruct((n_ex, 1, w), F32) for w in out_sums]

    def body(*refs):
        ins, o_r, o_s = refs[:n_in], refs[n_in:n_in + n_or], refs[n_in + n_or:]
        ro, so = fn(*[r[...] for r in ins])
        for o, v, f in zip(o_r, ro, flipped):
            o[...] = (v.T if f else v).astype(o.dtype)
        i = pl.program_id(1)
        for o, v in zip(o_s, so):
            @pl.when(i == 0)
            def _(o=o, v=v):
                o[...] = v

            @pl.when(i > 0)
            def _(o=o, v=v):
                o[...] += v

    outs = pl.pallas_call(
        body, name=name, grid=(n_ex, nb), in_specs=in_specs, out_specs=out_specs, out_shape=out_shape,
        compiler_params=_cp(("parallel", "arbitrary")),
    )(*[r[0] for r in rows], *vecs, *consts)
    return outs[:n_or], outs[n_or:]


def _csum(x):
    return jnp.sum(x, axis=0, keepdims=True)


def _norm_mod_fwd(x, g, sh, sc, *, n_ex, out_dtype, name, with_transpose=False):
    def fn(xt, sht, sct, gt):
        r = lax.rsqrt(jnp.mean(xt * xt, axis=-1, keepdims=True) + EPS)
        h = (xt * r * gt) * (1.0 + sct) + sht
        return [h, h] if with_transpose else [h], []
    D = x.shape[1]
    outs = [(D, out_dtype), (D, out_dtype, True)] if with_transpose else [(D, out_dtype)]
    res = _rowwise(fn, [(x, D, 0)], [sh, sc], [g], outs, [], n_ex=n_ex, name=name)[0]
    return res if with_transpose else res[0]


def _norm_mod_bwd(x, dh, dres, g, sc, *, n_ex, name):
    def fn(xt, dht, drt, sct, gt):
        dht = dht.astype(F32)
        r = lax.rsqrt(jnp.mean(xt * xt, axis=-1, keepdims=True) + EPS)
        n = xt * r
        y = n * gt
        dy = dht * (1.0 + sct)
        dn = dy * gt
        dx = r * (dn - n * jnp.mean(dn * n, axis=-1, keepdims=True))
        return [drt + dx], [_csum(dht), _csum(dht * y), _csum(dy * n)]
    D = x.shape[1]
    return _rowwise(fn, [(x, D, 0), (dh, D, 0), (dres, D, 0)], [sc], [g], [(D, F32)], [D, D, D], n_ex=n_ex, name=name)


def _sigmoid(x):
    return 1.0 / (1.0 + jnp.exp(-x))


def _gelu(y):
    return 0.5 * y * (1.0 + jnp.tanh(0.7978845608028654 * (y + 0.044715 * y * y * y)))


def _gelu_grad(y):
    t = jnp.tanh(0.7978845608028654 * (y + 0.044715 * y * y * y))
    return 0.5 * (1.0 + t) + 0.5 * y * (1.0 - t * t) * 0.7978845608028654 * (1.0 + 3 * 0.044715 * y * y)


def _adamw_fn(w, g, m, v):
    m2 = ADAM_B1 * m + (1.0 - ADAM_B1) * g
    v2 = ADAM_B2 * v + (1.0 - ADAM_B2) * (g * g)
    m_hat = m2 / (1.0 - ADAM_B1 ** ADAM_STEP)
    v_hat = v2 / (1.0 - ADAM_B2 ** ADAM_STEP)
    delta = -ADAM_LR * (m_hat / (jnp.sqrt(v_hat) + ADAM_EPS) + ADAM_WD * w)
    return delta, m2, v2


def _adamw2d(w, g, m, v, *, name, g_roff=0, g_cb=0):
    R, W = w.shape

    def fn(wt, gt, mt, vt):
        d, m2, v2 = _adamw_fn(wt, gt, mt, vt)
        return [d, m2, v2, gt], []
    return _rowwise(fn, [(w, W, 0), (g, W, g_cb, g_roff), (m, W, 0), (v, W, 0)], [], [],
                    [(W, F32)] * 4, [], n_ex=1, name=name, tr=256)[0]


def _scan_tiles(re_ref, im_ref, cf, lane0, n_chunks, reverse, extra=None):
    L = SCAN_LANES
    lanes = pl.ds(lane0, L)
    A = [cf[i, :, lanes] for i in range(8)]
    shifts = (7, 6, 4) if reverse else (1, 2, 4)
    edge = 0 if reverse else 7

    U = SCAN_UNROLL
    n_groups = n_chunks // U

    def body(c, carry):
        first = ((n_groups - 1 - c) if reverse else c) * U
        rows = pl.ds(pl.multiple_of(first * 8, 8 * U), 8 * U)
        big_r, big_i = re_ref[rows, lanes], im_ref[rows, lanes]
        tiles = []
        for u in range(U):
            xr, xi = big_r[8 * u:8 * u + 8, :], big_i[8 * u:8 * u + 8, :]
            for idx, sft in enumerate(shifts):
                ar, ai = A[2 * idx], A[2 * idx + 1]
                rr, ri = pltpu.roll(xr, sft, 0), pltpu.roll(xi, sft, 0)
                xr, xi = xr + ar * rr - ai * ri, xi + ar * ri + ai * rr
            tiles.append((xr, xi))
        pr, pi = A[6], A[7]
        cr, ci = carry[0], carry[1]
        for u in (range(U - 1, -1, -1) if reverse else range(U)):
            xr, xi = tiles[u]
            xr, xi = xr + pr * cr - pi * ci, xi + pr * ci + pi * cr
            tiles[u] = (xr, xi)
            cr, ci = jnp.broadcast_to(xr[edge:edge + 1, :], (8, L)), jnp.broadcast_to(xi[edge:edge + 1, :], (8, L))
        re_ref[rows, lanes] = jnp.concatenate([t[0] for t in tiles], axis=0)
        im_ref[rows, lanes] = jnp.concatenate([t[1] for t in tiles], axis=0)
        return (cr, ci) if extra is None else (cr, ci) + extra(first, tiles, carry[2:])

    assert n_chunks % U == 0
    z = jnp.zeros((8, L), F32)
    init = (z, z) if extra is None else (z, z, z, z)
    return lax.fori_loop(0, n_groups, body, init)


def _s5_consts(ab_re, ab_im):
    ng = ab_re.shape[0] // GROUPS_PER_STEP
    ar, ai = ab_re.reshape(ng, 1, ST_LANES), ab_im.reshape(ng, 1, ST_LANES)

    def cmul(xr, xi, yr, yi):
        return xr * yr - xi * yi, xr * yi + xi * yr

    def build(ar, ai, reverse):
        pw = [(ar, ai)]
        for _ in range(7):
            pw.append(cmul(*pw[-1], ar, ai))
        row = jnp.arange(8).reshape(1, 8, 1)
        tiles = []
        for k in (1, 2, 4):
            keep = (row <= 7 - k) if reverse else (row >= k)
            tiles += [jnp.where(keep, pw[k - 1][0], 0.0), jnp.where(keep, pw[k - 1][1], 0.0)]
        order = [7 - r for r in range(8)] if reverse else list(range(8))
        tiles += [jnp.concatenate([pw[o][0] for o in order], axis=1), jnp.concatenate([pw[o][1] for o in order], axis=1)]
        return jnp.stack([jnp.broadcast_to(t, (ng, 8, ST_LANES)) for t in tiles], axis=1)

    return build(ar, ai, False), build(ar, -ai, True)


def _s5_blockdiag(bb_re, bb_im, c_re, c_im):
    G = bb_re.shape[0]
    ng = G // GROUPS_PER_STEP
    eye = jnp.eye(GROUPS_PER_STEP, dtype=F32)

    def wb(bb):
        return jnp.einsum("bgph,gk->bghkp", bb.reshape(ng, GROUPS_PER_STEP, S5_STATE, S5_GROUP), eye).reshape(ng, U_LANES, ST_LANES)

    def wc(cc):
        return jnp.einsum("bghp,gk->bkpgh", cc.reshape(ng, GROUPS_PER_STEP, S5_GROUP, S5_STATE), eye).reshape(ng, ST_LANES, U_LANES)

    Wb = jnp.concatenate([wb(bb_re), wb(bb_im)], axis=2).astype(BF16)
    Wc = jnp.concatenate([wc(c_re), -wc(c_im)], axis=1).astype(BF16)
    return Wb, Wc


def _s5_unblock(dWb, dWc):
    ng = dWb.shape[0]
    eye = jnp.eye(GROUPS_PER_STEP, dtype=F32)

    def ub(w):
        return jnp.einsum("bghkp,gk->bgph", w.reshape(ng, GROUPS_PER_STEP, S5_GROUP, GROUPS_PER_STEP, S5_STATE), eye).reshape(-1, S5_STATE, S5_GROUP)

    def uc(w):
        return jnp.einsum("bkpgh,gk->bghp", w.reshape(ng, GROUPS_PER_STEP, S5_STATE, GROUPS_PER_STEP, S5_GROUP), eye).reshape(-1, S5_GROUP, S5_STATE)

    return ub(dWb[:, :, :ST_LANES]), ub(dWb[:, :, ST_LANES:]), uc(dWc[:, :ST_LANES, :]), -uc(dWc[:, ST_LANES:, :])


def _s5_disc(a_re, a_im, log_dt, b_re, b_im):
    dt = jnp.exp(log_dt)[:, None]
    mag = jnp.exp(a_re * dt)
    ab_re = mag * jnp.cos(a_im * dt)
    ab_im = mag * jnp.sin(a_im * dt)
    den = a_re * a_re + a_im * a_im
    nr, ni = ab_re - 1, ab_im
    f_re = (nr * a_re + ni * a_im) / den
    f_im = (ni * a_re - nr * a_im) / den
    bb_re = f_re[..., None] * b_re - f_im[..., None] * b_im
    bb_im = f_re[..., None] * b_im + f_im[..., None] * b_re
    return ab_re, ab_im, bb_re, bb_im


ROW_CHUNK = 512


def _s5_fwd(u, Wb, Wc, cf, d, xsrc, *, n_ex, name):
    T, D = u.shape
    S = T // n_ex
    ng = D // U_LANES
    rc = min(ROW_CHUNK, S)

    def body(u_ref, wb_ref, wc_ref, cf_ref, d_ref, xsrc_ref, y_ref, gy_ref, gyt_ref, st_ref, xout_ref, re_s, im_s, *sems):
        step = pl.program_id(0) * ng + pl.program_id(1)
        exch = _ChipExchange(xsrc_ref, xout_ref, *sems, scatter=False)

        @pl.when(step == 0)
        def _():
            exch.start()

        for r in range(S // rc):
            rows = pl.ds(r * rc, rc)
            bu = jnp.dot(u_ref[rows, :].astype(BF16), wb_ref[...], preferred_element_type=F32)
            re_s[rows, :] = bu[:, :ST_LANES]
            im_s[rows, :] = bu[:, ST_LANES:]
        for l0 in range(0, ST_LANES, SCAN_LANES):
            _scan_tiles(re_s, im_s, cf_ref, l0, S // 8, False)
        for r in range(S // rc):
            rows = pl.ds(r * rc, rc)
            st = jnp.concatenate([re_s[rows, :], im_s[rows, :]], axis=1).astype(BF16)
            st_ref[rows, :] = st
            y = jnp.dot(st, wc_ref[...], preferred_element_type=F32) + d_ref[...] * u_ref[rows, :]
            y_ref[rows, :] = y
            gy = _gelu(y)
            gy_ref[rows, :] = gy.astype(BF16)
            gyt_ref[:, rows] = gy.T.astype(BF16)

        @pl.when(step == n_ex * ng - 1)
        def _():
            exch.wait()

    return pl.pallas_call(
        body, name=name, grid=(n_ex, ng),
        in_specs=[pl.BlockSpec((S, U_LANES), lambda e, g: (e, g)),
                  pl.BlockSpec((None, U_LANES, 2 * ST_LANES), lambda e, g: (g, 0, 0)),
                  pl.BlockSpec((None, 2 * ST_LANES, U_LANES), lambda e, g: (g, 0, 0)),
                  pl.BlockSpec((None, 8, 8, ST_LANES), lambda e, g: (g, 0, 0, 0)),
                  pl.BlockSpec((1, U_LANES), lambda e, g: (0, g)), ANY],
        out_specs=[pl.BlockSpec((S, U_LANES), lambda e, g: (e, g))] * 2 + [pl.BlockSpec((U_LANES, S), lambda e, g: (g, e)),
                   pl.BlockSpec((S, 2 * ST_LANES), lambda e, g: (e, g)), ANY],
        out_shape=[jax.ShapeDtypeStruct((T, D), F32), jax.ShapeDtypeStruct((T, D), BF16), jax.ShapeDtypeStruct((D, T), BF16),
                   jax.ShapeDtypeStruct((T, ng * 2 * ST_LANES), BF16), _ChipExchange.out_shape(xsrc, False)],
        scratch_shapes=[pltpu.VMEM((S, ST_LANES), F32)] * 2 + _ChipExchange.SCRATCH,
        compiler_params=_cp(("arbitrary", "arbitrary")),
    )(u, Wb, Wc, cf, d, xsrc)


def _s5_bwd(u, y, dgy, st, Wb, Wc, cr, d, xsrc, *, n_ex, name):
    T, D = u.shape
    S = T // n_ex
    ng = D // U_LANES
    rc = min(ROW_CHUNK, S)
    nch = S // 8
    grp = 8 * SCAN_UNROLL
    assert grp % 16 == 0

    def body(u_ref, y_ref, dgy_ref, st_ref, wb_ref, wc_ref, cr_ref, d_ref, xsrc_ref,
             du_ref, dwb_ref, dwc_ref, dab_ref, dd_ref, xout_ref, gr_s, gi_s, dy_s, *sems):
        e = pl.program_id(1)
        step = pl.program_id(0) * n_ex + e
        exch = _ChipExchange(xsrc_ref, xout_ref, *sems, scatter=True)

        @pl.when(step == 0)
        def _():
            exch.start()

        @pl.when(e == 0)
        def _():
            dwb_ref[...] = jnp.zeros_like(dwb_ref)
            dwc_ref[...] = jnp.zeros_like(dwc_ref)
            dab_ref[...] = jnp.zeros_like(dab_ref)
            dd_ref[...] = jnp.zeros_like(dd_ref)

        dd = jnp.zeros((1, U_LANES), F32)
        for r in range(S // rc):
            rows = pl.ds(r * rc, rc)
            ut = u_ref[rows, :]
            dy = dgy_ref[rows, :].astype(F32) * _gelu_grad(y_ref[rows, :])
            dy_s[rows, :] = dy
            dd = dd + _csum(dy * ut)
            go = lax.dot_general(dy.astype(BF16), wc_ref[...], (((1,), (1,)), ((), ())), preferred_element_type=F32)
            gr_s[rows, :] = go[:, :ST_LANES]
            gi_s[rows, :] = go[:, ST_LANES:]
        dd_ref[0:1, :] += dd
        row0 = lax.broadcasted_iota(jnp.int32, (8, SCAN_LANES), 0) == 0
        for l0 in range(0, ST_LANES, SCAN_LANES):
            lanes = pl.ds(l0, SCAN_LANES)

            def dab_group(first, tiles, acc, l0=l0):
                def states(r0, n, lane0):
                    return st_ref[pl.ds(pl.multiple_of(r0, 16), n), pl.ds(lane0, SCAN_LANES)].astype(F32)
                r0 = first * 8
                cur = states(r0, grp, l0), states(r0, grp, ST_LANES + l0)
                live = (first > 0).astype(F32)
                p0 = jnp.maximum(r0 - 16, 0)
                before = [states(p0, 16, l0)[8:16, :] * live, states(p0, 16, ST_LANES + l0)[8:16, :] * live]
                a_re, a_im = acc
                for t, (gr, gi) in enumerate(tiles):
                    here = [c[8 * t:8 * t + 8, :] for c in cur]
                    sr, si = [jnp.where(row0, pltpu.roll(b, 1, 0), pltpu.roll(h, 1, 0)) for b, h in zip(before, here)]
                    a_re, a_im = a_re + gr * sr + gi * si, a_im + gi * sr - gr * si
                    before = here
                return a_re, a_im

            res = _scan_tiles(gr_s, gi_s, cr_ref, l0, nch, True, extra=dab_group)
            dab_ref[0:1, lanes] += _csum(res[2])
            dab_ref[1:2, lanes] += _csum(res[3])
        for r in range(S // rc):
            rows = pl.ds(r * rc, rc)
            st = st_ref[rows, :]
            g = jnp.concatenate([gr_s[rows, :], gi_s[rows, :]], axis=1).astype(BF16)
            dyb = dy_s[rows, :].astype(BF16)
            dwc_ref[...] += lax.dot_general(st, dyb, (((0,), (0,)), ((), ())), preferred_element_type=F32)
            dwb_ref[...] += lax.dot_general(u_ref[rows, :].astype(BF16), g, (((0,), (0,)), ((), ())), preferred_element_type=F32)
            du = lax.dot_general(g, wb_ref[...], (((1,), (1,)), ((), ())), preferred_element_type=F32)
            du_ref[rows, :] = du + d_ref[...] * dy_s[rows, :]

        @pl.when(step == ng * n_ex - 1)
        def _():
            exch.wait()

    return pl.pallas_call(
        body, name=name, grid=(ng, n_ex),
        in_specs=[pl.BlockSpec((S, U_LANES), lambda g, e: (e, g))] * 3 + [
            pl.BlockSpec((S, 2 * ST_LANES), lambda g, e: (e, g)),
            pl.BlockSpec((None, U_LANES, 2 * ST_LANES), lambda g, e: (g, 0, 0)),
            pl.BlockSpec((None, 2 * ST_LANES, U_LANES), lambda g, e: (g, 0, 0)),
            pl.BlockSpec((None, 8, 8, ST_LANES), lambda g, e: (g, 0, 0, 0)),
            pl.BlockSpec((1, U_LANES), lambda g, e: (0, g)), ANY],
        out_specs=[pl.BlockSpec((S, U_LANES), lambda g, e: (e, g)),
                   pl.BlockSpec((None, U_LANES, 2 * ST_LANES), lambda g, e: (g, 0, 0)),
                   pl.BlockSpec((None, 2 * ST_LANES, U_LANES), lambda g, e: (g, 0, 0)),
                   pl.BlockSpec((None, 8, ST_LANES), lambda g, e: (g, 0, 0)),
                   pl.BlockSpec((None, 8, U_LANES), lambda g, e: (g, 0, 0)), ANY],
        out_shape=[jax.ShapeDtypeStruct((T, D), F32),
                   jax.ShapeDtypeStruct((ng, U_LANES, 2 * ST_LANES), F32),
                   jax.ShapeDtypeStruct((ng, 2 * ST_LANES, U_LANES), F32),
                   jax.ShapeDtypeStruct((ng, 8, ST_LANES), F32),
                   jax.ShapeDtypeStruct((ng, 8, U_LANES), F32), _ChipExchange.out_shape(xsrc, True)],
        scratch_shapes=[pltpu.VMEM((S, ST_LANES), F32)] * 2 + [pltpu.VMEM((S, U_LANES), F32)] + _ChipExchange.SCRATCH,
        compiler_params=_cp(("arbitrary", "arbitrary")),
    )(u, y, dgy, st, Wb, Wc, cr, d, xsrc)


TQ = 256
KW = 512
SUB = 128


def _head_masks():
    lane = lax.broadcasted_iota(jnp.int32, (1, 2 * HEAD_DIM), 1)
    m0 = (lane < HEAD_DIM).astype(F32)
    return m0, 1.0 - m0


def _head_norm(x, g, m0, m1):
    sq = x * x
    r0 = lax.rsqrt(jnp.sum(sq * m0, axis=-1, keepdims=True) / HEAD_DIM + EPS)
    r1 = lax.rsqrt(jnp.sum(sq * m1, axis=-1, keepdims=True) / HEAD_DIM + EPS)
    r = m0 * r0 + m1 * r1
    return x * r, r


def _head_norm_bwd(dy, n, r, g, m0, m1):
    dn = dy * g
    p = dn * n
    mean = (m0 * jnp.sum(p * m0, axis=-1, keepdims=True) + m1 * jnp.sum(p * m1, axis=-1, keepdims=True)) / HEAD_DIM
    return r * (dn - n * mean), _csum(dy * n)


def _pair_matrix(kind):
    r = lax.broadcasted_iota(jnp.int32, (2 * SUB, 2 * SUB), 0)
    c = lax.broadcasted_iota(jnp.int32, (2 * SUB, 2 * SUB), 1)
    same = (r < SUB) == (c < SUB)
    rel = {"after": r > c, "upto": r <= c, "before": r < c}[kind]
    return jnp.logical_and(same, rel).astype(BF16)


def _block_sums(x, mat, carry, reverse, terms=2):
    hi = x.astype(BF16)
    lo = (x - hi.astype(F32)).astype(BF16) if terms == 2 else None
    npair = x.shape[1] // (2 * SUB)
    parts = [None] * (2 * npair)
    for p in (range(npair - 1, -1, -1) if reverse else range(npair)):
        sl = slice(2 * SUB * p, 2 * SUB * (p + 1))
        loc = jnp.dot(hi[:, sl], mat, preferred_element_type=F32)
        if terms == 2:
            loc = loc + jnp.dot(lo[:, sl], mat, preferred_element_type=F32)
        for b in ((1, 0) if reverse else (0, 1)):
            k = 2 * p + b
            parts[k] = loc[:, SUB * b:SUB * (b + 1)] + carry
            carry = carry + jnp.sum(x[:, SUB * k:SUB * (k + 1)], axis=-1, keepdims=True)
    return jnp.concatenate(parts, axis=1), carry


def _sb_logits(z, mask):
    lp = jnp.minimum(z, 0.0) - jnp.log(1.0 + jnp.exp(-jnp.abs(z)))
    lf = lp - z
    if mask is not None:
        lf = jnp.where(mask, lf, 0.0)
    return lp, lf


def _causal_mask(row0, col0, kw):
    r = row0 + lax.broadcasted_iota(jnp.int32, (TQ, kw), 0)
    c = col0 + lax.broadcasted_iota(jnp.int32, (TQ, kw), 1)
    return c < r


def _transposed_windows(x, ref):
    for w in range(x.shape[0] // KW):
        ref[w] = x[w * KW:(w + 1) * KW, :].T.astype(BF16)


def _attn_fwd(q, kv, qg, kg, xsrc, *, n_ex, name):
    T, D = q.shape
    S = T // n_ex
    nhp = D // (2 * HEAD_DIM)
    nq = S // TQ
    scale = 1.0 / math.sqrt(HEAD_DIM)

    def body(q_ref, k_ref, v_ref, qg_ref, kg_ref, xsrc_ref, o_ref, tot_ref, ot_ref, xout_ref, kT_s, qm_s, vm_s, *sems):
        step = pl.program_id(0) * nhp + pl.program_id(1)
        exch = _ChipExchange(xsrc_ref, xout_ref, *sems, scatter=False)

        @pl.when(step == 0)
        def _():
            exch.start()

        m0, m1 = _head_masks()
        qn, _ = _head_norm(q_ref[...], None, m0, m1)
        qn = qn * (qg_ref[...] * scale)
        kn, _ = _head_norm(k_ref[...], None, m0, m1)
        _transposed_windows(kn * kg_ref[...], kT_s)
        v = v_ref[...]
        for h, m in enumerate((m0, m1)):
            qm_s[h] = (qn * m).astype(BF16)
            vm_s[h] = (v * m).astype(BF16)
        u_after = _pair_matrix("after")

        def window(rows, win, st, mask, kw):
            keys = pl.ds(pl.multiple_of(win * KW, KW), kw)
            zs = [jnp.dot(qm_s[h, rows, :], kT_s[win, :, :kw], preferred_element_type=F32) for h in range(2)]
            lg = [_sb_logits(zs[h], mask) for h in range(2)]
            sums = [_block_sums(lg[h][1], u_after, st[2 * h], True) for h in range(2)]
            out = ()
            for h in range(2):
                w = jnp.exp(lg[h][0] + sums[h][0])
                if mask is not None:
                    w = jnp.where(mask, w, 0.0)
                out += (sums[h][1], st[2 * h + 1] + jnp.dot(w.astype(BF16), vm_s[h, keys, :], preferred_element_type=F32))
            return out

        def qtile(iq, last, kw):
            rows = pl.ds(pl.multiple_of(iq * TQ, TQ), TQ)
            mask = _causal_mask(iq * TQ, last * KW, kw)
            z1, zq = jnp.zeros((TQ, 1), F32), jnp.zeros((TQ, 2 * HEAD_DIM), F32)
            st = window(rows, last, (z1, zq, z1, zq), mask, kw)
            st = lax.fori_loop(0, last, lambda jj, st: window(rows, last - 1 - jj, st, None, KW), st)
            o_ref[rows, :] = st[1] + st[3]
            tot_ref[rows, :] = st[0] * m0 + st[2] * m1

        def qtiles_of_window(a, _):
            for sub in range(KW // TQ):
                qtile(a * (KW // TQ) + sub, a, (sub + 1) * TQ)
            return 0

        lax.fori_loop(0, S // KW, qtiles_of_window, 0)
        ot_ref[...] = o_ref[...].T.astype(BF16)

        @pl.when(step == n_ex * nhp - 1)
        def _():
            exch.wait()

    assert S % KW == 0 and KW % TQ == 0
    nwin = S // KW
    blk = (S, 2 * HEAD_DIM)
    return pl.pallas_call(
        body, name=name, grid=(n_ex, nhp),
        in_specs=[pl.BlockSpec(blk, lambda e, h: (e, h)), pl.BlockSpec(blk, lambda e, h: (e, h)),
                  pl.BlockSpec(blk, lambda e, h: (e, h + nhp)),
                  pl.BlockSpec((1, 2 * HEAD_DIM), lambda e, h: (0, 0)), pl.BlockSpec((1, 2 * HEAD_DIM), lambda e, h: (0, 0)), ANY],
        out_specs=[pl.BlockSpec(blk, lambda e, h: (e, h))] * 2 + [pl.BlockSpec((2 * HEAD_DIM, S), lambda e, h: (h, e)), ANY],
        out_shape=[jax.ShapeDtypeStruct((T, D), F32)] * 2 + [jax.ShapeDtypeStruct((D, T), BF16), _ChipExchange.out_shape(xsrc, False)],
        scratch_shapes=[pltpu.VMEM((nwin, 2 * HEAD_DIM, KW), BF16), pltpu.VMEM((2,) + blk, BF16), pltpu.VMEM((2,) + blk, BF16)]
        + _ChipExchange.SCRATCH,
        compiler_params=_cp(("arbitrary", "arbitrary")),
    )(q, kv, kv, qg, kg, xsrc)


def _attn_bwd(q, kv, tot, do, qg, kg, *, n_ex, name):
    T, D = q.shape
    S = T // n_ex
    nhp = D // (2 * HEAD_DIM)
    nq = S // TQ
    scale = 1.0 / math.sqrt(HEAD_DIM)

    def body(q_ref, k_ref, v_ref, tot_ref, do_ref, qg_ref, kg_ref, dq_ref, dk_ref, dv_ref, dqg_ref, dkg_ref,
             kT_s, vT_s, km_s, qm_s, dom_s, dqn_s, dkT_s, dvT_s):
        m0, m1 = _head_masks()
        qn, qr = _head_norm(q_ref[...], None, m0, m1)
        kn, kr = _head_norm(k_ref[...], None, m0, m1)
        qs = qn * (qg_ref[...] * scale)
        kk = kn * kg_ref[...]
        _transposed_windows(kk, kT_s)
        _transposed_windows(v_ref[...], vT_s)
        do = do_ref[...]
        for h, m in enumerate((m0, m1)):
            qm_s[h] = (qs * m).astype(BF16)
            km_s[h] = (kk * m).astype(BF16)
            dom_s[h] = (do * m).astype(BF16)
        dkT_s[...] = jnp.zeros_like(dkT_s)
        dvT_s[...] = jnp.zeros_like(dvT_s)
        u_upto, u_before = _pair_matrix("upto"), _pair_matrix("before")

        def both(inv, win, st, mask, kw):
            keys = pl.ds(pl.multiple_of(win * KW, KW), kw)
            lg = [_sb_logits(jnp.dot(inv[h][0], kT_s[win, :, :kw], preferred_element_type=F32), mask) for h in range(2)]
            s_lf = [_block_sums(lg[h][1], u_upto, st[3 * h], False) for h in range(2)]
            ws, ews = [], []
            for h in range(2):
                w = jnp.exp(lg[h][0] - s_lf[h][0])
                if mask is not None:
                    w = jnp.where(mask, w, 0.0)
                ws.append(w)
                ews.append(jnp.dot(inv[h][2], vT_s[win, :, :kw], preferred_element_type=F32) * w)
            s_e = [_block_sums(ews[h], u_before, st[3 * h + 1], False, terms=1) for h in range(2)]
            out, dk, dv = (), None, None
            for h in range(2):
                sig = jnp.exp(lg[h][0])
                dz = ews[h] - sig * (ews[h] + s_e[h][0])
                if mask is not None:
                    dz = jnp.where(mask, dz, 0.0)
                dzb = dz.astype(BF16)
                out += (s_lf[h][1], s_e[h][1], st[3 * h + 2] + jnp.dot(dzb, km_s[h, keys, :], preferred_element_type=F32))
                dkh = jnp.dot(inv[h][1], dzb, preferred_element_type=F32)
                dvh = jnp.dot(inv[h][3], ws[h].astype(BF16), preferred_element_type=F32)
                dk, dv = (dkh, dvh) if h == 0 else (dk + dkh, dv + dvh)
            dkT_s[win, :, :kw] += dk
            dvT_s[win, :, :kw] += dv
            return out

        def qtile(iq, last, kw):
            rows = pl.ds(pl.multiple_of(iq * TQ, TQ), TQ)
            mask = _causal_mask(iq * TQ, last * KW, kw)
            tt = tot_ref[rows, :]
            inv, neg_total = [], []
            for h, m in enumerate((m0, m1)):
                qh, doh = qm_s[h, rows, :], dom_s[h, rows, :]
                neg_total.append(jnp.sum(tt * m, axis=-1, keepdims=True) * (-1.0 / HEAD_DIM))
                inv.append((qh, qh.astype(F32).T.astype(BF16), doh, doh.astype(F32).T.astype(BF16)))

            z1, zq = jnp.zeros((TQ, 1), F32), jnp.zeros((TQ, 2 * HEAD_DIM), F32)
            st = lax.fori_loop(0, last, lambda win, st: both(inv, win, st, None, KW), (neg_total[0], z1, zq, neg_total[1], z1, zq))
            st = both(inv, last, st, mask, kw)
            dqn_s[rows, :] = st[2] + st[5]

        def qtiles_of_window(a, _):
            for sub in range(KW // TQ):
                qtile(a * (KW // TQ) + sub, a, (sub + 1) * TQ)
            return 0

        lax.fori_loop(0, S // KW, qtiles_of_window, 0)
        dkn = jnp.concatenate([dkT_s[w].T for w in range(nwin)], axis=0)
        dq, dqg = _head_norm_bwd(dqn_s[...] * scale, qn, qr, qg_ref[...], m0, m1)
        dk, dkg = _head_norm_bwd(dkn, kn, kr, kg_ref[...], m0, m1)
        dq_ref[...] = dq
        dk_ref[...] = dk
        dv_ref[...] = jnp.concatenate([dvT_s[w].T for w in range(nwin)], axis=0)
        dqg_ref[...] = dqg
        dkg_ref[...] = dkg

    assert S % KW == 0 and KW % TQ == 0
    nwin = S // KW
    blk = (S, 2 * HEAD_DIM)
    tblk = (nwin, 2 * HEAD_DIM, KW)
    gblk = (None, None, 1, 2 * HEAD_DIM)
    dq, dk, dv, dqg, dkg = pl.pallas_call(
        body, name=name, grid=(n_ex, nhp),
        in_specs=[pl.BlockSpec(blk, lambda e, h: (e, h)), pl.BlockSpec(blk, lambda e, h: (e, h)),
                  pl.BlockSpec(blk, lambda e, h: (e, h + nhp)),
                  pl.BlockSpec(blk, lambda e, h: (e, h)), pl.BlockSpec(blk, lambda e, h: (e, h)),
                  pl.BlockSpec((1, 2 * HEAD_DIM), lambda e, h: (0, 0)), pl.BlockSpec((1, 2 * HEAD_DIM), lambda e, h: (0, 0))],
        out_specs=[pl.BlockSpec(blk, lambda e, h: (e, h))] * 3 + [pl.BlockSpec(gblk, lambda e, h: (e, h, 0, 0))] * 2,
        out_shape=[jax.ShapeDtypeStruct((T, D), F32)] * 3 + [jax.ShapeDtypeStruct((n_ex, nhp, 1, 2 * HEAD_DIM), F32)] * 2,
        scratch_shapes=[pltpu.VMEM(tblk, BF16), pltpu.VMEM(tblk, BF16),
                        pltpu.VMEM((2,) + blk, BF16), pltpu.VMEM((2,) + blk, BF16), pltpu.VMEM((2,) + blk, BF16),
                        pltpu.VMEM(blk, F32), pltpu.VMEM(tblk, F32), pltpu.VMEM(tblk, F32)],
        compiler_params=_cp(("parallel", "parallel")),
    )(q, kv, kv, tot, do, qg, kg)
    return dq, dk, dv, dqg, dkg


def _place():
    return lax.axis_index("x"), lax.axis_index("y"), lax.axis_index("c")


def _all_gather8(x_shard, *, name):
    m_per, n = x_shard.shape

    def body(x_ref, out_ref, send_sems, recv_sems, local_sem):
        x, y, c = _place()
        me, sibling = (x, y, c), (x, y, 1 - c)
        chips = [(1 - x, y), (x, 1 - y), (1 - x, 1 - y)]

        def rows(px, py, pc):
            return out_ref.at[pl.ds((4 * px + 2 * py + pc) * m_per, m_per), :]

        def copy(k, block, to, src=None):
            return pltpu.make_async_remote_copy(
                src_ref=rows(*block) if src is None else src, dst_ref=rows(*block),
                send_sem=send_sems.at[k], recv_sem=recv_sems.at[k], device_id=to, device_id_type=MESH)

        mine = pltpu.make_async_copy(x_ref, rows(*me), local_sem)
        mine.start()
        first = [copy(0, me, sibling, src=x_ref)]
        first += [copy(1 + j, me, (*chip, c), src=x_ref) for j, chip in enumerate(chips)]
        for cp in first:
            cp.start()
        passed = [copy(4 + j, (*chip, c), sibling) for j, chip in enumerate(chips)]
        for j, chip in enumerate(chips):
            copy(1 + j, (*chip, c), me).wait_recv()
            passed[j].start()
        copy(0, sibling, me).wait_recv()
        for j, chip in enumerate(chips):
            copy(4 + j, (*chip, 1 - c), me).wait_recv()
        for cp in first + passed:
            cp.wait_send()
        mine.wait()

    return pl.pallas_call(
        body, name=name, out_shape=jax.ShapeDtypeStruct((8 * m_per, n), x_shard.dtype),
        in_specs=[pl.BlockSpec(memory_space=pltpu.VMEM)], out_specs=pl.BlockSpec(memory_space=pltpu.VMEM),
        scratch_shapes=[pltpu.SemaphoreType.DMA((7,)), pltpu.SemaphoreType.DMA((7,)), pltpu.SemaphoreType.DMA],
        compiler_params=pltpu.CompilerParams(vmem_limit_bytes=VMEM_LIMIT),
    )(x_shard)


def _sibling_sum_half(x, *, name):
    R, C = x.shape
    half = R // 2

    def body(x_ref, o_ref, theirs, send_sem, recv_sem):
        px, py, pc = _place()
        cp = pltpu.make_async_remote_copy(src_ref=x_ref, dst_ref=theirs, send_sem=send_sem, recv_sem=recv_sem,
                                          device_id=(px, py, 1 - pc), device_id_type=MESH)
        cp.start()
        cp.wait()
        rows = pl.ds(pl.multiple_of(pc * half, 8), half)
        o_ref[...] = x_ref[rows, :] + theirs[rows, :]

    return pl.pallas_call(
        body, name=name, out_shape=jax.ShapeDtypeStruct((half, C), x.dtype),
        in_specs=[pl.BlockSpec(memory_space=pltpu.VMEM)], out_specs=pl.BlockSpec(memory_space=pltpu.VMEM),
        scratch_shapes=[pltpu.VMEM((R, C), x.dtype), pltpu.SemaphoreType.DMA, pltpu.SemaphoreType.DMA],
        compiler_params=pltpu.CompilerParams(vmem_limit_bytes=VMEM_LIMIT),
    )(x)


def _sum_blocks(x, n, *, name):
    R = x.shape[0] // n

    def body(x_ref, o_ref):
        acc = x_ref[pl.ds(0, R), :]
        for k in range(1, n):
            acc = acc + x_ref[pl.ds(k * R, R), :]
        o_ref[...] = acc

    return pl.pallas_call(body, name=name, out_shape=jax.ShapeDtypeStruct((R, x.shape[1]), x.dtype),
                          compiler_params=pltpu.CompilerParams(vmem_limit_bytes=VMEM_LIMIT))(x)


def _colsum(x, *, name):
    def body(x_ref, o_ref):
        o_ref[...] = jnp.sum(x_ref[...], axis=0, keepdims=True)
    return pl.pallas_call(body, name=name, out_shape=jax.ShapeDtypeStruct((1, x.shape[1]), x.dtype))(x)


ANY = pl.BlockSpec(memory_space=pl.ANY)


class _ChipExchange:
    SCRATCH = [pltpu.SemaphoreType.DMA((3,)), pltpu.SemaphoreType.DMA((3,)), pltpu.SemaphoreType.DMA]

    @staticmethod
    def out_shape(src, scatter):
        return jax.ShapeDtypeStruct(((4,) + tuple(src.shape[1:])) if scatter else ((4, 2) + tuple(src.shape[1:])), src.dtype)

    def __init__(self, src_ref, out_ref, send_sems, recv_sems, local_sem, scatter):
        x, y, c = _place()
        myj = 2 * x + y
        chips = [(1 - x, y), (x, 1 - y), (1 - x, 1 - y)]

        def slot(j):
            return out_ref.at[j] if scatter else out_ref.at[j, c]

        def piece(j):
            return src_ref.at[j] if scatter else src_ref.at[c]

        self.mine = pltpu.make_async_copy(piece(myj), slot(myj), local_sem)
        self.sends = [pltpu.make_async_remote_copy(
            src_ref=piece(2 * cx + cy), dst_ref=slot(myj), send_sem=send_sems.at[k], recv_sem=recv_sems.at[k],
            device_id=(cx, cy, c), device_id_type=MESH) for k, (cx, cy) in enumerate(chips)]
        self.recvs = [pltpu.make_async_remote_copy(
            src_ref=slot(2 * cx + cy), dst_ref=slot(2 * cx + cy), send_sem=send_sems.at[k], recv_sem=recv_sems.at[k],
            device_id=(cx, cy, c), device_id_type=MESH) for k, (cx, cy) in enumerate(chips)]

    def start(self):
        self.mine.start()
        for cp in self.sends:
            cp.start()

    def wait(self):
        for cp in self.recvs:
            cp.wait_recv()
        for cp in self.sends:
            cp.wait_send()
        self.mine.wait()


def _sibling_fill(buf, *, axis, name):
    def half(ref, h):
        return ref.at[h] if axis == 0 else ref.at[:, h]

    def body(in_ref, out_ref, send_sem, recv_sem):
        x, y, c = _place()
        cp = pltpu.make_async_remote_copy(src_ref=half(out_ref, c), dst_ref=half(out_ref, c), send_sem=send_sem, recv_sem=recv_sem,
                                          device_id=(x, y, 1 - c), device_id_type=MESH)
        cp.start()
        pltpu.make_async_remote_copy(src_ref=half(out_ref, 1 - c), dst_ref=half(out_ref, 1 - c), send_sem=send_sem, recv_sem=recv_sem,
                                     device_id=(x, y, 1 - c), device_id_type=MESH).wait_recv()
        cp.wait_send()

    return pl.pallas_call(
        body, name=name, out_shape=jax.ShapeDtypeStruct(buf.shape, buf.dtype), in_specs=[ANY], out_specs=ANY,
        input_output_aliases={0: 0}, scratch_shapes=[pltpu.SemaphoreType.DMA, pltpu.SemaphoreType.DMA],
    )(buf)


def _sibling_swap_half(g, *, name):
    def body(g_ref, out_ref, send_sem, recv_sem):
        x, y, c = _place()
        cp = pltpu.make_async_remote_copy(src_ref=g_ref.at[:, 1 - c], dst_ref=out_ref, send_sem=send_sem, recv_sem=recv_sem,
                                          device_id=(x, y, 1 - c), device_id_type=MESH)
        cp.start()
        cp.wait()

    return pl.pallas_call(
        body, name=name, out_shape=jax.ShapeDtypeStruct((g.shape[0],) + g.shape[2:], g.dtype), in_specs=[ANY], out_specs=ANY,
        scratch_shapes=[pltpu.SemaphoreType.DMA, pltpu.SemaphoreType.DMA],
    )(g)


def _add_my_half(g, b, cidx, *, name, tr=256):
    n, _, R, C = g.shape
    tr = math.gcd(tr, R)

    def body(c_ref, g_ref, b_ref, o_ref):
        o_ref[...] = (g_ref[...] + b_ref[...]).astype(o_ref.dtype)

    return pl.pallas_call(
        body, name=name, out_shape=jax.ShapeDtypeStruct((n, R, C), BF16),
        grid_spec=pltpu.PrefetchScalarGridSpec(
            num_scalar_prefetch=1, grid=(n, R // tr),
            in_specs=[pl.BlockSpec((None, None, tr, C), lambda j, i, c: (j, c[0], i, 0)),
                      pl.BlockSpec((None, tr, C), lambda j, i, c: (j, i, 0))],
            out_specs=pl.BlockSpec((None, tr, C), lambda j, i, c: (j, i, 0))),
        compiler_params=_cp(("parallel", "parallel")),
    )(cidx, g, b)


def _sum4_into_half(q, cidx, *, name, tr=256):
    _, R, C = q.shape
    tr = math.gcd(tr, R)

    def body(c_ref, q_ref, o_ref):
        o_ref[...] = ((q_ref[0].astype(F32) + q_ref[1].astype(F32)) + q_ref[2].astype(F32)) + q_ref[3].astype(F32)

    return pl.pallas_call(
        body, name=name, out_shape=jax.ShapeDtypeStruct((2, R, C), F32),
        grid_spec=pltpu.PrefetchScalarGridSpec(
            num_scalar_prefetch=1, grid=(R // tr,),
            in_specs=[pl.BlockSpec((4, tr, C), lambda i, c: (0, i, 0))],
            out_specs=pl.BlockSpec((None, tr, C), lambda i, c: (c[0], i, 0))),
        compiler_params=_cp(("parallel",)),
    )(cidx, q)


def _pack_rows(parts, width=1024):
    rows, spans, r0 = [], [], 0
    for p in parts:
        n = p.size
        nr = 8 * (-(-n // (8 * width)))
        flat = p.reshape(-1)
        if nr * width != n:
            flat = jnp.pad(flat, (0, nr * width - n))
        rows.append(flat.reshape(nr, width))
        spans.append((r0, nr, n, p.shape))
        r0 += nr
    return jnp.concatenate(rows, axis=0), spans


def _unpack_rows(buf, spans):
    return [buf[r0:r0 + nr].reshape(-1)[:n].reshape(shape) for (r0, nr, n, shape) in spans]


def kernel(x, c, ada_w, ada_b, mix_norm_g, mlp_norm_g, mlp_w1, mlp_w2, s5_a_re, s5_a_im, s5_log_dt, s5_b_re, s5_b_im, s5_c_re, s5_c_im, s5_d, s5_w_glu, kv_ada_w, kv_ada_b, kv_norm_g, w_kv, k_norm_g, sb_w_q, q_norm_g, sb_w_o, loss_target, m_ada_w, m_ada_b, m_mix_norm_g, m_mlp_norm_g, m_mlp_w1, m_mlp_w2, m_s5_a_re, m_s5_a_im, m_s5_log_dt, m_s5_b_re, m_s5_b_im, m_s5_c_re, m_s5_c_im, m_s5_d, m_s5_w_glu, m_kv_ada_w, m_kv_ada_b, m_kv_norm_g, m_w_kv, m_k_norm_g, m_sb_w_q, m_q_norm_g, m_sb_w_o, v_ada_w, v_ada_b, v_mix_norm_g, v_mlp_norm_g, v_mlp_w1, v_mlp_w2, v_s5_a_re, v_s5_a_im, v_s5_log_dt, v_s5_b_re, v_s5_b_im, v_s5_c_re, v_s5_c_im, v_s5_d, v_s5_w_glu, v_kv_ada_w, v_kv_ada_b, v_kv_norm_g, v_w_kv, v_k_norm_g, v_sb_w_q, v_q_norm_g, v_sb_w_o):
    E, S, D = x.shape
    T = E * S
    FF = 4 * D
    NB = 8 * E
    px, py, pc = _place()
    chip = 2 * px + py
    dev = 4 * px + 2 * py + pc
    cidx = jnp.reshape(pc, (1,)).astype(jnp.int32)
    x0 = x.reshape(T, D)
    tgt = loss_target.reshape(T, D)

    c_all = _all_gather8(c.reshape(-1, 128), name="ag_c").reshape(NB, D)
    sc_all = (c_all * _sigmoid(c_all)).astype(BF16)
    wa = ada_w.shape[2]
    wk = kv_ada_w.shape[1]
    m_sh = jnp.concatenate([_mm(sc_all, ada_w[0], "nn", name="ada0", tn=256),
                            _mm(sc_all, ada_w[1], "nn", name="ada1", tn=256),
                            _mm(sc_all, kv_ada_w, "nn", name="ada_kv", tn=256)], axis=1)
    m_all = _all_gather8(m_sh, name="ag_m").reshape(4, 2, NB, 2 * wa + wk)[:, 0]
    mods = []
    for l in range(2):
        full = jnp.transpose(m_all[:, :, l * wa:(l + 1) * wa], (1, 0, 2)).reshape(NB, 6 * D) + ada_b[l]
        mine = lax.dynamic_slice_in_dim(full, E * dev, E, axis=0)
        mods.append([mine[:, i * D:(i + 1) * D].reshape(E, 1, D) for i in range(6)])
    full = jnp.transpose(m_all[:, :, 2 * wa:], (1, 0, 2)).reshape(NB, 2 * D) + kv_ada_b
    mine = lax.dynamic_slice_in_dim(full, E * dev, E, axis=0)
    kv_sh, kv_sc = [mine[:, i * D:(i + 1) * D].reshape(E, 1, D) for i in range(2)]

    wpack_a = jnp.concatenate([mlp_w1[0], mlp_w2[0], jnp.concatenate([s5_w_glu[0], w_kv], axis=1), sb_w_q[0]], axis=0).astype(BF16)
    wpack_b = jnp.concatenate([mlp_w1[1], mlp_w2[1], sb_w_o[0]], axis=0).astype(BF16)
    RA, RB = wpack_a.shape[0], wpack_b.shape[0]
    RW = RA + RB

    tm = min(1024, S)
    gbuf = [jax.ShapeDtypeStruct((4, RW, D), F32)]

    def grad_mm(act, dout, kind, roff, nr, c0, nc, name, transposed=False):
        gbuf[0] = _mm(act, dout, "nn" if transposed else "tn", name=name, tm=1024, tk=2048,
                      into=_Sharded(gbuf[0], kind, roff, nr, c0, nc))

    def mlp_fwd(xa, l, mod):
        sh_m, sc_m, g_m = mod[3], mod[4], mod[5]
        h, h_t = _norm_mod_fwd(xa, mlp_norm_g[l:l + 1], sh_m, sc_m, n_ex=E, out_dtype=BF16, name=f"mlp_norm{l}", with_transpose=True)

        def relu_sq(acc):
            ra = jnp.maximum(acc, 0.0)
            return ra * ra, ra
        r, ra = _mm(h, W1[l], "nn", name=f"mlp_up{l}", out_dtypes=(BF16, BF16), tm=tm, epilogue=relu_sq)
        xb, ff = _mm(r, W2[l], "nn", name=f"mlp_down{l}", out_dtypes=(F32, F32), tm=tm,
                     extras=[_mn_extra(xa), _vec_extra(g_m, S)],
                     epilogue=lambda acc, xat, gt: (xat + gt * acc, acc))
        return xb, (h_t, r, ra, ff)

    def mlp_bwd(dxb, xa, l, mod, saved):
        sc_m, g_m = mod[4], mod[5]
        h_t, r, ra, ff = saved
        (dff,), (dgm,) = _rowwise(lambda d, f, g: ([g * d], [_csum(d * f)]), [(dxb, D, 0), (ff, D, 0)], [g_m], [],
                                  [(D, BF16)], [D], n_ex=E, name=f"mlp_gate_bwd{l}")
        da = _mm(dff, W2[l], "nt", name=f"mlp_down_dx{l}", out_dtypes=(BF16,), tm=tm, extras=[_mn_extra(ra)],
                 epilogue=lambda acc, rat: (acc * (2.0 * rat.astype(F32)),))
        grad_mm(r, dff, "rows", (2 + l) * D, D, 0, D, f"mlp_down_dw{l}")
        dh = _mm(da, W1[l], "nt", name=f"mlp_up_dx{l}", tm=tm)
        grad_mm(h_t, da, "cols", l * D, D, 0, D, f"mlp_up_dw{l}", transposed=True)
        (dxa,), (dsh, dsc, dg) = _norm_mod_bwd(xa, dh, dxb, mlp_norm_g[l:l + 1], sc_m, n_ex=E, name=f"mlp_norm_bwd{l}")
        return dxa, (dsh, dsc, dgm), dg

    ab_re, ab_im, bb_re, bb_im = _s5_disc(s5_a_re[0], s5_a_im[0], s5_log_dt[0], s5_b_re[0], s5_b_im[0])
    cf, cr = _s5_consts(ab_re, ab_im)
    Wb, Wc = _s5_blockdiag(bb_re, bb_im, s5_c_re[0], s5_c_im[0])
    ng = D // U_LANES
    nd = s5_d.size // 128
    d_full = _all_gather8(jnp.pad(s5_d.reshape(nd, 128), ((0, 8 - nd), (0, 0))), name="ag_d")
    d_full = d_full.reshape(4, 2, 8, 128)[:, 0, :nd].reshape(1, D)

    mod0, mod1 = mods
    h0 = _norm_mod_fwd(x0, mix_norm_g[0:1], mod0[0], mod0[1], n_ex=E, out_dtype=F32, name="mix_norm0")
    y, gy, gy_t, s5_states, wfull_a = _s5_fwd(h0, Wb, Wc, cf, d_full, wpack_a.reshape(2, RA // 2, D), n_ex=E, name="s5_fwd")
    wfull_a = _sibling_fill(wfull_a, axis=1, name="wgather_a_d2d").reshape(4, RA, D)

    W1 = [_Sharded(wfull_a, "cols", 0, D, 0, D), None]
    W2 = [_Sharded(wfull_a, "rows", D, D, 0, D), None]
    Wglu = _Sharded(wfull_a, "cols", 2 * D, D, 0, D // 2)
    Wkv = _Sharded(wfull_a, "cols", 2 * D, D, D // 2, D // 2)
    Wq = _Sharded(wfull_a, "rows", 3 * D, D // 4, 0, D)
    vg = _mm(gy, Wglu, "nn", name="glu_up", tm=tm)
    (x1,), _ = _rowwise(lambda v, g, xt, ga: ([xt + ga * (v * _sigmoid(g))], []),
                        [(vg, D, 0), (vg, D, 1), (x0, D, 0)], [mod0[2]], [], [(D, F32)], [], n_ex=E, name="glu_gate")
    x2, saved_mlp0 = mlp_fwd(x1, 0, mod0)

    hkv, hkv_t = _norm_mod_fwd(x2, kv_norm_g.reshape(1, D), kv_sh, kv_sc, n_ex=E, out_dtype=BF16, name="kv_norm", with_transpose=True)
    kvf = _mm(hkv, Wkv, "nn", name="kv_proj", tm=tm)
    h1, h1_t = _norm_mod_fwd(x2, mix_norm_g[1:2], mod1[0], mod1[1], n_ex=E, out_dtype=BF16, name="mix_norm1", with_transpose=True)
    qf = _mm(h1, Wq, "nn", name="q_proj", tm=tm)
    qg2 = jnp.tile(q_norm_g.reshape(1, HEAD_DIM), (1, 2))
    kg2 = jnp.tile(k_norm_g.reshape(1, HEAD_DIM), (1, 2))
    o, lf_tot, o_t, wfull_b = _attn_fwd(qf, kvf, qg2, kg2, wpack_b.reshape(2, RB // 2, D), n_ex=E, name="attn_fwd")
    wfull_b = _sibling_fill(wfull_b, axis=1, name="wgather_b_d2d").reshape(4, RB, D)
    W1[1] = _Sharded(wfull_b, "cols", 0, D, 0, D)
    W2[1] = _Sharded(wfull_b, "rows", D, D, 0, D)
    Wo = _Sharded(wfull_b, "rows", 2 * D, D // 4, 0, D)
    x3, mix1 = _mm(o, Wo, "nn", name="o_proj", out_dtypes=(F32, F32), tm=tm,
                   extras=[_mn_extra(x2), _vec_extra(mod1[2], S)],
                   epilogue=lambda acc, xat, gt: (xat + gt * acc, acc))
    x4, saved_mlp1 = mlp_fwd(x3, 1, mod1)

    (dx4,), (lsum,) = _rowwise(lambda xt, tt: ([(xt - tt) * (1.0 / D)], [_csum(jnp.square(xt - tt)) * (0.5 / D)]),
                               [(x4, D, 0), (tgt, D, 0)], [], [], [(D, F32)], [D], n_ex=E, name="loss")
    loss = lax.psum(jnp.sum(lsum), ("x", "y", "c"))

    dx3, (dsh_m1, dsc_m1, dgm1), dg_mlp1 = mlp_bwd(dx4, x3, 1, mod1, saved_mlp1)
    (dmix1,), (dga1,) = _rowwise(lambda d, f, g: ([g * d], [_csum(d * f)]), [(dx3, D, 0), (mix1, D, 0)], [mod1[2]], [],
                                 [(D, BF16)], [D], n_ex=E, name="attn_gate_bwd")
    do = _mm(dmix1, Wo, "nt", name="o_proj_dx", tm=tm)
    grad_mm(o_t, dmix1, "rows", 5 * D + D // 4, D // 4, 0, D, "o_proj_dw", transposed=True)
    dq, dk, dv, dqg, dkg = _attn_bwd(qf, kvf, lf_tot, do, qg2, kg2, n_ex=E, name="attn_bwd")
    dh1 = _mm(dq, Wq, "nt", name="q_proj_dx", tm=tm)
    grad_mm(h1_t, dq, "rows", 5 * D, D // 4, 0, D, "q_proj_dw", transposed=True)
    (dx2,), (dsh_a1, dsc_a1, dg_mix1) = _norm_mod_bwd(x2, dh1, dx3, mix_norm_g[1:2], mod1[1], n_ex=E, name="mix_norm_bwd1")
    dkv = jnp.concatenate([dk, dv], axis=1)
    dhkv = _mm(dkv, Wkv, "nt", name="kv_proj_dx", tm=tm)
    grad_mm(hkv_t, dkv, "cols", 4 * D, D, D // 2, D // 2, "kv_proj_dw", transposed=True)
    (dx2,), (dkv_sh, dkv_sc, dg_kv) = _norm_mod_bwd(x2, dhkv, dx2, kv_norm_g.reshape(1, D), kv_sc, n_ex=E, name="kv_norm_bwd")

    dx1, (dsh_m0, dsc_m0, dgm0), dg_mlp0 = mlp_bwd(dx2, x1, 0, mod0, saved_mlp0)

    def glu_bwd(v, g, d, ga):
        sg = _sigmoid(g)
        dm = ga * d
        return [jnp.concatenate([dm * sg, dm * v * sg * (1.0 - sg)], axis=1)], [_csum(d * (v * sg))]
    (dvg,), (dga0,) = _rowwise(glu_bwd, [(vg, D, 0), (vg, D, 1), (dx1, D, 0)], [mod0[2]], [], [(2 * D, BF16)], [D],
                               n_ex=E, name="glu_gate_bwd")
    dgy = _mm(dvg, Wglu, "nt", name="glu_up_dx", tm=tm)
    grad_mm(gy_t, dvg, "cols", 4 * D, D, 0, D // 2, "glu_up_dw", transposed=True)

    gpack = gbuf[0].reshape(4, 2, RW // 2, D)
    theirs = _sibling_swap_half(gpack, name="gscatter_d2d")
    chip_sum = _add_my_half(gpack, theirs, cidx, name="gscatter_add")
    dh0, dWb, dWc, dab, dd, from_chips = _s5_bwd(h0, y, dgy, s5_states, Wb, Wc, cr, d_full, chip_sum, n_ex=E, name="s5_bwd")
    ghalf = _sum4_into_half(from_chips, cidx, name="gscatter_sum")
    gsh = _sibling_fill(ghalf, axis=0, name="gscatter_fill").reshape(RW, D)
    (gx,), (dsh_a0, dsc_a0, dg_mix0) = _norm_mod_bwd(x0, dh0, dx1, mix_norm_g[0:1], mod0[1], n_ex=E, name="mix_norm_bwd0")
    grad_x = gx.reshape(E, S, D)

    dm_mine = jnp.concatenate([t.reshape(E, D) for t in
                               (dsh_a0, dsc_a0, dga0, dsh_m0, dsc_m0, dgm0, dsh_a1, dsc_a1, dga1, dsh_m1, dsc_m1, dgm1, dkv_sh, dkv_sc)], axis=1)
    dm_all = _all_gather8(dm_mine.reshape(8, -1), name="ag_dm").reshape(NB, 14 * D)
    sc_f32 = c_all * _sigmoid(c_all)
    g_ada_w = jnp.stack([_mm(sc_f32, lax.dynamic_slice_in_dim(dm_all, l * 6 * D + chip * wa, wa, axis=1), "tn", name=f"ada_dw{l}", tn=256)
                         for l in range(2)])
    g_kv_ada_w = _mm(sc_f32, lax.dynamic_slice_in_dim(dm_all, 12 * D + chip * wk, wk, axis=1), "tn", name="ada_kv_dw", tn=256)
    db_all = _colsum(dm_all, name="ada_db")
    g_ada_b = db_all[0, :12 * D].reshape(2, 6 * D)
    g_kv_ada_b = db_all[0, 12 * D:]

    dWb_re, dWb_im, dC_re, dC_im = _s5_unblock(dWb, dWc)
    small_parts = [dg_mix0.sum(0), dg_mix1.sum(0), dg_mlp0.sum(0), dg_mlp1.sum(0), dg_kv.sum(0),
                   dqg.sum((0, 1, 2)).reshape(2, HEAD_DIM).sum(0), dkg.sum((0, 1, 2)).reshape(2, HEAD_DIM).sum(0),
                   dd[:, 0, :], dab[:, 0, :], dab[:, 1, :], dWb_re, dWb_im, dC_re, dC_im]
    spack, spans = _pack_rows(small_parts)
    chip_half = _sibling_sum_half(spack, name="small_d2d")
    ssum = _sum_blocks(_all_gather8(chip_half, name="ag_small"), 4, name="sum_small")
    (g_mix0, g_mix1, g_mlp0, g_mlp1, g_kvn, g_qn, g_kn, g_d, g_abr, g_abi, g_bbr, g_bbi, g_cre, g_cim) = _unpack_rows(ssum, spans)
    _, disc_vjp = jax.vjp(_s5_disc, s5_a_re[0], s5_a_im[0], s5_log_dt[0], s5_b_re[0], s5_b_im[0])
    g_are, g_aim, g_ldt, g_bre, g_bim = disc_vjp((g_abr.reshape(ab_re.shape), g_abi.reshape(ab_im.shape), g_bbr, g_bbi))
    g_s5d = lax.dynamic_slice_in_dim(g_d.reshape(1, D), chip * s5_d.shape[1], s5_d.shape[1], axis=1)

    def upd_big(w, m, v, roff, cb, name):
        shape = w.shape
        W = shape[-1]
        d_, m_, v_, g_ = _adamw2d(w.reshape(-1, W), gsh, m.reshape(-1, W), v.reshape(-1, W), name=name, g_roff=roff, g_cb=cb)
        return [t.reshape(shape) for t in (g_, d_, m_, v_)]

    def upd_own(w, g, m, v, name):
        shape = w.shape
        W = shape[-1]
        d_, m_, v_, g_ = _adamw2d(w.reshape(-1, W), g.reshape(-1, W), m.reshape(-1, W), v.reshape(-1, W), name=name)
        return [t.reshape(shape) for t in (g_, d_, m_, v_)]

    res = {}
    res["ada_w"] = upd_own(ada_w, g_ada_w, m_ada_w, v_ada_w, "adam_ada_w")
    res["kv_ada_w"] = upd_own(kv_ada_w, g_kv_ada_w, m_kv_ada_w, v_kv_ada_w, "adam_kv_ada_w")
    res["mlp_w1"] = upd_big(mlp_w1, m_mlp_w1, v_mlp_w1, 0, 0, "adam_w1")
    res["mlp_w2"] = upd_big(mlp_w2, m_mlp_w2, v_mlp_w2, 2 * D, 0, "adam_w2")
    res["s5_w_glu"] = upd_big(s5_w_glu, m_s5_w_glu, v_s5_w_glu, 4 * D, 0, "adam_glu")
    res["w_kv"] = upd_big(w_kv, m_w_kv, v_w_kv, 4 * D, 1, "adam_wkv")
    res["sb_w_q"] = upd_big(sb_w_q, m_sb_w_q, v_sb_w_q, 5 * D, 0, "adam_wq")
    res["sb_w_o"] = upd_big(sb_w_o, m_sb_w_o, v_sb_w_o, 5 * D + D // 4, 0, "adam_wo")

    small = {
        "ada_b": (ada_b, g_ada_b, m_ada_b, v_ada_b),
        "mix_norm_g": (mix_norm_g, jnp.stack([g_mix0, g_mix1]), m_mix_norm_g, v_mix_norm_g),
        "mlp_norm_g": (mlp_norm_g, jnp.stack([g_mlp0, g_mlp1]), m_mlp_norm_g, v_mlp_norm_g),
        "s5_a_re": (s5_a_re, g_are[None], m_s5_a_re, v_s5_a_re),
        "s5_a_im": (s5_a_im, g_aim[None], m_s5_a_im, v_s5_a_im),
        "s5_log_dt": (s5_log_dt, g_ldt[None], m_s5_log_dt, v_s5_log_dt),
        "s5_b_re": (s5_b_re, g_bre[None], m_s5_b_re, v_s5_b_re),
        "s5_b_im": (s5_b_im, g_bim[None], m_s5_b_im, v_s5_b_im),
        "s5_c_re": (s5_c_re, g_cre[None], m_s5_c_re, v_s5_c_re),
        "s5_c_im": (s5_c_im, g_cim[None], m_s5_c_im, v_s5_c_im),
        "s5_d": (s5_d, g_s5d, m_s5_d, v_s5_d),
        "kv_ada_b": (kv_ada_b, g_kv_ada_b, m_kv_ada_b, v_kv_ada_b),
        "kv_norm_g": (kv_norm_g, g_kvn, m_kv_norm_g, v_kv_norm_g),
        "k_norm_g": (k_norm_g, g_kn, m_k_norm_g, v_k_norm_g),
        "q_norm_g": (q_norm_g, g_qn.reshape(q_norm_g.shape), m_q_norm_g, v_q_norm_g),
    }
    names = list(small)
    packs = [_pack_rows([small[n][i].reshape(small[n][0].shape) for n in names]) for i in range(4)]
    sp = packs[0][1]
    d_, m_, v_, g_ = _adamw2d(packs[0][0], packs[1][0], packs[2][0], packs[3][0], name="adam_small")
    for n, gg, dd_, mm_, vv_ in zip(names, _unpack_rows(g_, sp), _unpack_rows(d_, sp), _unpack_rows(m_, sp), _unpack_rows(v_, sp)):
        res[n] = [gg, dd_, mm_, vv_]

    order = ["ada_w", "ada_b", "mix_norm_g", "mlp_norm_g", "mlp_w1", "mlp_w2", "s5_a_re", "s5_a_im", "s5_log_dt", "s5_b_re", "s5_b_im",
             "s5_c_re", "s5_c_im", "s5_d", "s5_w_glu", "kv_ada_w", "kv_ada_b", "kv_norm_g", "w_kv", "k_norm_g", "sb_w_q", "q_norm_g", "sb_w_o"]
    return (loss, grad_x, *[res[n][0] for n in order], *[res[n][1] for n in order], *[res[n][2] for n in order], *[res[n][3] for n in order])
```

```python
import functools
import math

import jax
import jax.numpy as jnp
from jax import lax
from jax.experimental import pallas as pl
from jax.experimental.pallas import tpu as pltpu

F32 = jnp.float32
BF16 = jnp.bfloat16
EPS = 1e-6
HEAD_DIM = 64
S5_GROUP = 16
S5_STATE = 64
GROUPS_PER_STEP = 8
U_LANES = GROUPS_PER_STEP * S5_GROUP
ST_LANES = GROUPS_PER_STEP * S5_STATE
SCAN_LANES = 256
SCAN_UNROLL = 4
VMEM_LIMIT = 56 * 1024 * 1024
ADAM_LR, ADAM_B1, ADAM_B2, ADAM_EPS, ADAM_WD, ADAM_STEP = 0.001, 0.9, 0.999, 1e-08, 0.01, 10
MESH = pl.DeviceIdType.MESH


def _cp(sem):
    return pltpu.CompilerParams(dimension_semantics=sem, vmem_limit_bytes=VMEM_LIMIT)


class _Sharded:
    def __init__(self, buf, kind, roff, nr, c0, nc):
        self.buf, self.kind, self.roff, self.nr, self.c0, self.nc = buf, kind, roff, nr, c0, nc
        self.shape = (nr, 4 * nc) if kind == "cols" else (4 * nr, nc)

    def operand(self, dims, tn, tk):
        roff, nr, c0, nc = self.roff, self.nr, self.c0, self.nc
        if self.kind == "cols" and dims == "nn":
            tk = min(tk, nr)
            assert roff % tk == 0
            return nc, tk, (None, tk, nc), lambda i, j, k: (j, roff // tk + k, c0 // nc)
        if self.kind == "cols":
            tn = min(tn, nr)
            assert roff % tn == 0
            return tn, nc, (None, tn, nc), lambda i, j, k: (k, roff // tn + j, c0 // nc)
        if dims == "nn":
            tn = min(tn, nc)
            assert roff % nr == 0 and c0 % tn == 0
            return tn, nr, (None, nr, tn), lambda i, j, k: (k, roff // nr, c0 // tn + j)
        tk = min(tk, nc)
        assert roff % nr == 0 and c0 % tk == 0
        return nr, tk, (None, nr, tk), lambda i, j, k: (j, roff // nr, c0 // tk + k)

    def result(self, tm, tn):
        roff, nr, c0, nc = self.roff, self.nr, self.c0, self.nc
        if self.kind == "cols":
            tm = min(tm, nr)
            assert roff % tm == 0
            return tm, nc, (None, tm, nc), lambda i, j, k: (j, roff // tm + i, c0 // nc)
        tm, tn = min(tm, nr), min(tn, nc)
        assert roff % tm == 0 and c0 % tn == 0
        per = nr // tm
        return tm, tn, (None, tm, tn), lambda i, j, k: (i // per, roff // tm + i % per, c0 // tn + j)


class _Layer:
    def __init__(self, buf, layer):
        self.buf, self.layer, self.shape = buf, layer, tuple(buf.shape[1:])

    def operand(self, dims, tn, tk):
        assert dims == "nn"
        layer = self.layer
        return tn, tk, (None, tk, tn), lambda i, j, k: (layer, k, j)

    def result(self, tm, tn):
        layer = self.layer
        return tm, tn, (None, tm, tn), lambda i, j, k: (layer, i, j)


def _mm(a, b, dims, *, name, out_dtypes=(F32,), epilogue=None, extras=(), tm=512, tn=1024, tk=1024, into=None):
    bshape = b.shape
    if dims == "nn":
        (M, K), (_, N) = a.shape, bshape
    elif dims == "nt":
        (M, K), (N, _) = a.shape, bshape
    else:
        (K, M), (_, N) = a.shape, bshape
    tm, tn, tk = min(tm, M), min(tn, N), min(tk, K)
    b_arr = b
    if into is not None:
        assert (M, N) == into.shape and len(out_dtypes) == 1 and not isinstance(b, _Sharded)
        tm, tn, o_blk, o_map = into.result(tm, tn)
        out_specs, out_shape = [pl.BlockSpec(o_blk, o_map)], [jax.ShapeDtypeStruct(into.buf.shape, into.buf.dtype)]
    if isinstance(b, (_Sharded, _Layer)):
        tn, tk, b_blk, b_map = b.operand(dims, tn, tk)
        b_spec, b_arr = pl.BlockSpec(b_blk, b_map), b.buf
    else:
        b_spec = pl.BlockSpec((tn, tk), lambda i, j, k: (j, k)) if dims == "nt" else pl.BlockSpec((tk, tn), lambda i, j, k: (k, j))
    if into is None:
        out_specs = [pl.BlockSpec((tm, tn), lambda i, j, k: (i, j)) for _ in out_dtypes]
        out_shape = [jax.ShapeDtypeStruct((M, N), d) for d in out_dtypes]
    assert M % tm == 0 and N % tn == 0 and K % tk == 0, (M, N, K, tm, tn, tk)
    nk = K // tk
    extras = [e(tm, tn) for e in extras]
    a_spec = pl.BlockSpec((tk, tm), lambda i, j, k: (k, i)) if dims == "tn" else pl.BlockSpec((tm, tk), lambda i, j, k: (i, k))
    contract = {"nn": ((1,), (0,)), "nt": ((1,), (1,)), "tn": ((0,), (0,))}[dims]
    n_ex, n_out = len(extras), len(out_dtypes)
    chain = [into.buf] if into is not None and not isinstance(into.buf, jax.ShapeDtypeStruct) else []
    n_in = n_ex + len(chain)

    def finish(r, ex, outs):
        res = epilogue(r, *[e[...] for e in ex]) if epilogue is not None else (r,)
        for o, v in zip(outs, res):
            o[...] = v.astype(o.dtype)

    def product(a_ref, b_ref):
        return lax.dot_general(a_ref[...].astype(BF16), b_ref[...].astype(BF16), (contract, ((), ())), preferred_element_type=F32)

    def body_one(a_ref, b_ref, *rest):
        finish(product(a_ref, b_ref), rest[:n_ex], rest[n_in:])

    def body_acc(a_ref, b_ref, *rest):
        ex, outs, acc = rest[:n_ex], rest[n_in:n_in + n_out], rest[-1]
        k = pl.program_id(2)

        @pl.when(k == 0)
        def _():
            acc[...] = product(a_ref, b_ref)

        @pl.when(jnp.logical_and(k > 0, k < nk - 1))
        def _():
            acc[...] += product(a_ref, b_ref)

        @pl.when(k == nk - 1)
        def _():
            finish(acc[...] + product(a_ref, b_ref), ex, outs)

    out = pl.pallas_call(
        body_one if nk == 1 else body_acc, name=name, grid=(M // tm, N // tn, nk),
        in_specs=[a_spec, b_spec] + [pl.BlockSpec(blk, im) for (_, blk, im) in extras] + [ANY for _ in chain],
        out_specs=out_specs, out_shape=out_shape,
        input_output_aliases={2 + n_ex: 0} if chain else {},
        scratch_shapes=[] if nk == 1 else [pltpu.VMEM((tm, tn), F32)],
        compiler_params=_cp(("parallel", "parallel", "arbitrary")),
    )(a, b_arr, *[e[0] for e in extras], *chain)
    return out if n_out > 1 else out[0]


def _mn_extra(arr):
    return lambda tm, tn: (arr, (tm, tn), lambda i, j, k: (i, j))


def _vec_extra(vec, S):
    return lambda tm, tn: (vec, (None, 1, tn), lambda i, j, k: ((i * tm) // S, 0, j))


def _rowwise(fn, rows, vecs=(), consts=(), out_rows=(), out_sums=(), *, n_ex, name, tr=512):
    rows = [r if len(r) == 4 else (*r, 0) for r in rows]
    S = min(r[0].shape[0] for r in rows if r[3] == 0) // n_ex
    tr = math.gcd(tr, S)
    assert S % tr == 0
    nb = S // tr
    in_specs = []
    for (arr, w, cb, roff) in rows:
        assert roff % tr == 0
        in_specs.append(pl.BlockSpec((tr, w), functools.partial(lambda e, i, cb, ro: (e * nb + i + ro, cb), cb=cb, ro=roff // tr)))
    for v in vecs:
        in_specs.append(pl.BlockSpec((None, 1, v.shape[-1]), lambda e, i: (e, 0, 0)))
    for c in consts:
        in_specs.append(pl.BlockSpec((1, c.shape[-1]), lambda e, i: (0, 0)))
    n_in, n_or, n_os = len(in_specs), len(out_rows), len(out_sums)
    flipped = [len(o) == 3 and o[2] for o in out_rows]
    out_specs = [pl.BlockSpec((o[0], tr), lambda e, i: (0, e * nb + i)) if f else pl.BlockSpec((tr, o[0]), lambda e, i: (e * nb + i, 0))
                 for o, f in zip(out_rows, flipped)]
    out_specs += [pl.BlockSpec((None, 1, w), lambda e, i: (e, 0, 0)) for w in out_sums]
    out_shape = [jax.ShapeDtypeStruct((o[0], n_ex * S) if f else (n_ex * S, o[0]), o[1]) for o, f in zip(out_rows, flipped)]
    out_shape += [jax.ShapeDtypeStruct((n_ex, 1, w), F32) for w in out_sums]

    def body(*refs):
        ins, o_r, o_s = refs[:n_in], refs[n_in:n_in + n_or], refs[n_in + n_or:]
        ro, so = fn(*[r[...] for r in ins])
        for o, v, f in zip(o_r, ro, flipped):
            o[...] = (v.T if f else v).astype(o.dtype)
        i = pl.program_id(1)
        for o, v in zip(o_s, so):
            @pl.when(i == 0)
            def _(o=o, v=v):
                o[...] = v

            @pl.when(i > 0)
            def _(o=o, v=v):
                o[...] += v

    outs = pl.pallas_call(
        body, name=name, grid=(n_ex, nb), in_specs=in_specs, out_specs=out_specs, out_shape=out_shape,
        compiler_params=_cp(("parallel", "arbitrary")),
    )(*[r[0] for r in rows], *vecs, *consts)
    return outs[:n_or], outs[n_or:]


def _csum(x):
    return jnp.sum(x, axis=0, keepdims=True)


def _norm_mod_fwd(x, g, sh, sc, *, n_ex, out_dtype, name, with_transpose=False):
    def fn(xt, sht, sct, gt):
        r = lax.rsqrt(jnp.mean(xt * xt, axis=-1, keepdims=True) + EPS)
        h = (xt * r * gt) * (1.0 + sct) + sht
        return [h, h] if with_transpose else [h], []
    D = x.shape[1]
    outs = [(D, out_dtype), (D, out_dtype, True)] if with_transpose else [(D, out_dtype)]
    res = _rowwise(fn, [(x, D, 0)], [sh, sc], [g], outs, [], n_ex=n_ex, name=name)[0]
    return res if with_transpose else res[0]


def _norm_mod_bwd(x, dh, dres, g, sc, *, n_ex, name):
    def fn(xt, dht, drt, sct, gt):
        dht = dht.astype(F32)
        r = lax.rsqrt(jnp.mean(xt * xt, axis=-1, keepdims=True) + EPS)
        n = xt * r
        y = n * gt
        dy = dht * (1.0 + sct)
        dn = dy * gt
        dx = r * (dn - n * jnp.mean(dn * n, axis=-1, keepdims=True))
        return [drt + dx], [_csum(dht), _csum(dht * y), _csum(dy * n)]
    D = x.shape[1]
    return _rowwise(fn, [(x, D, 0), (dh, D, 0), (dres, D, 0)], [sc], [g], [(D, F32)], [D, D, D], n_ex=n_ex, name=name)


def _sigmoid(x):
    return 1.0 / (1.0 + jnp.exp(-x))


def _gelu(y):
    return 0.5 * y * (1.0 + jnp.tanh(0.7978845608028654 * (y + 0.044715 * y * y * y)))


def _gelu_grad(y):
    t = jnp.tanh(0.7978845608028654 * (y + 0.044715 * y * y * y))
    return 0.5 * (1.0 + t) + 0.5 * y * (1.0 - t * t) * 0.7978845608028654 * (1.0 + 3 * 0.044715 * y * y)


def _adamw_fn(w, g, m, v):
    m2 = ADAM_B1 * m + (1.0 - ADAM_B1) * g
    v2 = ADAM_B2 * v + (1.0 - ADAM_B2) * (g * g)
    m_hat = m2 / (1.0 - ADAM_B1 ** ADAM_STEP)
    v_hat = v2 / (1.0 - ADAM_B2 ** ADAM_STEP)
    delta = -ADAM_LR * (m_hat / (jnp.sqrt(v_hat) + ADAM_EPS) + ADAM_WD * w)
    return delta, m2, v2


def _adamw2d(w, g, m, v, *, name, g_roff=0, g_cb=0):
    R, W = w.shape

    def fn(wt, gt, mt, vt):
        d, m2, v2 = _adamw_fn(wt, gt, mt, vt)
        return [d, m2, v2, gt], []
    return _rowwise(fn, [(w, W, 0), (g, W, g_cb, g_roff), (m, W, 0), (v, W, 0)], [], [],
                    [(W, F32)] * 4, [], n_ex=1, name=name, tr=256)[0]


def _scan_tiles(re_ref, im_ref, cf, lane0, n_chunks, reverse, extra=None):
    L = SCAN_LANES
    lanes = pl.ds(lane0, L)
    A = [cf[i, :, lanes] for i in range(8)]
    shifts = (7, 6, 4) if reverse else (1, 2, 4)
    edge = 0 if reverse else 7

    U = SCAN_UNROLL
    n_groups = n_chunks // U

    def body(c, carry):
        first = ((n_groups - 1 - c) if reverse else c) * U
        rows = pl.ds(pl.multiple_of(first * 8, 8 * U), 8 * U)
        big_r, big_i = re_ref[rows, lanes], im_ref[rows, lanes]
        tiles = []
        for u in range(U):
            xr, xi = big_r[8 * u:8 * u + 8, :], big_i[8 * u:8 * u + 8, :]
            for idx, sft in enumerate(shifts):
                ar, ai = A[2 * idx], A[2 * idx + 1]
                rr, ri = pltpu.roll(xr, sft, 0), pltpu.roll(xi, sft, 0)
                xr, xi = xr + ar * rr - ai * ri, xi + ar * ri + ai * rr
            tiles.append((xr, xi))
        pr, pi = A[6], A[7]
        cr, ci = carry[0], carry[1]
        for u in (range(U - 1, -1, -1) if reverse else range(U)):
            xr, xi = tiles[u]
            xr, xi = xr + pr * cr - pi * ci, xi + pr * ci + pi * cr
            tiles[u] = (xr, xi)
            cr, ci = jnp.broadcast_to(xr[edge:edge + 1, :], (8, L)), jnp.broadcast_to(xi[edge:edge + 1, :], (8, L))
        re_ref[rows, lanes] = jnp.concatenate([t[0] for t in tiles], axis=0)
        im_ref[rows, lanes] = jnp.concatenate([t[1] for t in tiles], axis=0)
        return (cr, ci) if extra is None else (cr, ci) + extra(first, tiles, carry[2:])

    assert n_chunks % U == 0
    z = jnp.zeros((8, L), F32)
    init = (z, z) if extra is None else (z, z, z, z)
    return lax.fori_loop(0, n_groups, body, init)


def _s5_consts(ab_re, ab_im):
    ng = ab_re.shape[0] // GROUPS_PER_STEP
    ar, ai = ab_re.reshape(ng, 1, ST_LANES), ab_im.reshape(ng, 1, ST_LANES)

    def cmul(xr, xi, yr, yi):
        return xr * yr - xi * yi, xr * yi + xi * yr

    def build(ar, ai, reverse):
        pw = [(ar, ai)]
        for _ in range(7):
            pw.append(cmul(*pw[-1], ar, ai))
        row = jnp.arange(8).reshape(1, 8, 1)
        tiles = []
        for k in (1, 2, 4):
            keep = (row <= 7 - k) if reverse else (row >= k)
            tiles += [jnp.where(keep, pw[k - 1][0], 0.0), jnp.where(keep, pw[k - 1][1], 0.0)]
        order = [7 - r for r in range(8)] if reverse else list(range(8))
        tiles += [jnp.concatenate([pw[o][0] for o in order], axis=1), jnp.concatenate([pw[o][1] for o in order], axis=1)]
        return jnp.stack([jnp.broadcast_to(t, (ng, 8, ST_LANES)) for t in tiles], axis=1)

    return build(ar, ai, False), build(ar, -ai, True)


def _s5_blockdiag(bb_re, bb_im, c_re, c_im):
    G = bb_re.shape[0]
    ng = G // GROUPS_PER_STEP
    eye = jnp.eye(GROUPS_PER_STEP, dtype=F32)

    def wb(bb):
        return jnp.einsum("bgph,gk->bghkp", bb.reshape(ng, GROUPS_PER_STEP, S5_STATE, S5_GROUP), eye).reshape(ng, U_LANES, ST_LANES)

    def wc(cc):
        return jnp.einsum("bghp,gk->bkpgh", cc.reshape(ng, GROUPS_PER_STEP, S5_GROUP, S5_STATE), eye).reshape(ng, ST_LANES, U_LANES)

    Wb = jnp.concatenate([wb(bb_re), wb(bb_im)], axis=2).astype(BF16)
    Wc = jnp.concatenate([wc(c_re), -wc(c_im)], axis=1).astype(BF16)
    return Wb, Wc


def _s5_unblock(dWb, dWc):
    ng = dWb.shape[0]
    eye = jnp.eye(GROUPS_PER_STEP, dtype=F32)

    def ub(w):
        return jnp.einsum("bghkp,gk->bgph", w.reshape(ng, GROUPS_PER_STEP, S5_GROUP, GROUPS_PER_STEP, S5_STATE), eye).reshape(-1, S5_STATE, S5_GROUP)

    def uc(w):
        return jnp.einsum("bkpgh,gk->bghp", w.reshape(ng, GROUPS_PER_STEP, S5_STATE, GROUPS_PER_STEP, S5_GROUP), eye).reshape(-1, S5_GROUP, S5_STATE)

    return ub(dWb[:, :, :ST_LANES]), ub(dWb[:, :, ST_LANES:]), uc(dWc[:, :ST_LANES, :]), -uc(dWc[:, ST_LANES:, :])


def _s5_disc(a_re, a_im, log_dt, b_re, b_im):
    dt = jnp.exp(log_dt)[:, None]
    mag = jnp.exp(a_re * dt)
    ab_re = mag * jnp.cos(a_im * dt)
    ab_im = mag * jnp.sin(a_im * dt)
    den = a_re * a_re + a_im * a_im
    nr, ni = ab_re - 1, ab_im
    f_re = (nr * a_re + ni * a_im) / den
    f_im = (ni * a_re - nr * a_im) / den
    bb_re = f_re[..., None] * b_re - f_im[..., None] * b_im
    bb_im = f_re[..., None] * b_im + f_im[..., None] * b_re
    return ab_re, ab_im, bb_re, bb_im


ROW_CHUNK = 512


def _s5_fwd(u, Wb, Wc, cf, d, xsrc, *, n_ex, name):
    T, D = u.shape
    S = T // n_ex
    ng = D // U_LANES
    rc = min(ROW_CHUNK, S)

    def body(u_ref, wb_ref, wc_ref, cf_ref, d_ref, xsrc_ref, y_ref, gy_ref, gyt_ref, st_ref, xout_ref, re_s, im_s, *sems):
        step = pl.program_id(0) * ng + pl.program_id(1)
        exch = _ChipExchange(xsrc_ref, xout_ref, *sems, scatter=False)

        @pl.when(step == 0)
        def _():
            exch.start()

        for r in range(S // rc):
            rows = pl.ds(r * rc, rc)
            bu = jnp.dot(u_ref[rows, :].astype(BF16), wb_ref[...], preferred_element_type=F32)
            re_s[rows, :] = bu[:, :ST_LANES]
            im_s[rows, :] = bu[:, ST_LANES:]
        for l0 in range(0, ST_LANES, SCAN_LANES):
            _scan_tiles(re_s, im_s, cf_ref, l0, S // 8, False)
        for r in range(S // rc):
            rows = pl.ds(r * rc, rc)
            st = jnp.concatenate([re_s[rows, :], im_s[rows, :]], axis=1).astype(BF16)
            st_ref[rows, :] = st
            y = jnp.dot(st, wc_ref[...], preferred_element_type=F32) + d_ref[...] * u_ref[rows, :]
            y_ref[rows, :] = y
            gy = _gelu(y)
            gy_ref[rows, :] = gy.astype(BF16)
            gyt_ref[:, rows] = gy.T.astype(BF16)

        @pl.when(step == n_ex * ng - 1)
        def _():
            exch.wait()

    return pl.pallas_call(
        body, name=name, grid=(n_ex, ng),
        in_specs=[pl.BlockSpec((S, U_LANES), lambda e, g: (e, g)),
                  pl.BlockSpec((None, U_LANES, 2 * ST_LANES), lambda e, g: (g, 0, 0)),
                  pl.BlockSpec((None, 2 * ST_LANES, U_LANES), lambda e, g: (g, 0, 0)),
                  pl.BlockSpec((None, 8, 8, ST_LANES), lambda e, g: (g, 0, 0, 0)),
                  pl.BlockSpec((1, U_LANES), lambda e, g: (0, g)), ANY],
        out_specs=[pl.BlockSpec((S, U_LANES), lambda e, g: (e, g))] * 2 + [pl.BlockSpec((U_LANES, S), lambda e, g: (g, e)),
                   pl.BlockSpec((S, 2 * ST_LANES), lambda e, g: (e, g)), ANY],
        out_shape=[jax.ShapeDtypeStruct((T, D), F32), jax.ShapeDtypeStruct((T, D), BF16), jax.ShapeDtypeStruct((D, T), BF16),
                   jax.ShapeDtypeStruct((T, ng * 2 * ST_LANES), BF16), _ChipExchange.out_shape(xsrc, False)],
        scratch_shapes=[pltpu.VMEM((S, ST_LANES), F32)] * 2 + _ChipExchange.SCRATCH,
        compiler_params=_cp(("arbitrary", "arbitrary")),
    )(u, Wb, Wc, cf, d, xsrc)


def _s5_bwd(u, y, dgy, st, Wb, Wc, cr, d, xsrc, *, n_ex, name):
    T, D = u.shape
    S = T // n_ex
    ng = D // U_LANES
    rc = min(ROW_CHUNK, S)
    nch = S // 8
    grp = 8 * SCAN_UNROLL
    assert grp % 16 == 0

    def body(u_ref, y_ref, dgy_ref, st_ref, wb_ref, wc_ref, cr_ref, d_ref, xsrc_ref,
             du_ref, dwb_ref, dwc_ref, dab_ref, dd_ref, xout_ref, gr_s, gi_s, dy_s, *sems):
        e = pl.program_id(1)
        step = pl.program_id(0) * n_ex + e
        exch = _ChipExchange(xsrc_ref, xout_ref, *sems, scatter=True)

        @pl.when(step == 0)
        def _():
            exch.start()

        @pl.when(e == 0)
        def _():
            dwb_ref[...] = jnp.zeros_like(dwb_ref)
            dwc_ref[...] = jnp.zeros_like(dwc_ref)
            dab_ref[...] = jnp.zeros_like(dab_ref)
            dd_ref[...] = jnp.zeros_like(dd_ref)

        dd = jnp.zeros((1, U_LANES), F32)
        for r in range(S // rc):
            rows = pl.ds(r * rc, rc)
            ut = u_ref[rows, :]
            dy = dgy_ref[rows, :].astype(F32) * _gelu_grad(y_ref[rows, :])
            dy_s[rows, :] = dy
            dd = dd + _csum(dy * ut)
            go = lax.dot_general(dy.astype(BF16), wc_ref[...], (((1,), (1,)), ((), ())), preferred_element_type=F32)
            gr_s[rows, :] = go[:, :ST_LANES]
            gi_s[rows, :] = go[:, ST_LANES:]
        dd_ref[0:1, :] += dd
        row0 = lax.broadcasted_iota(jnp.int32, (8, SCAN_LANES), 0) == 0
        for l0 in range(0, ST_LANES, SCAN_LANES):
            lanes = pl.ds(l0, SCAN_LANES)

            def dab_group(first, tiles, acc, l0=l0):
                def states(r0, n, lane0):
                    return st_ref[pl.ds(pl.multiple_of(r0, 16), n), pl.ds(lane0, SCAN_LANES)].astype(F32)
                r0 = first * 8
                cur = states(r0, grp, l0), states(r0, grp, ST_LANES + l0)
                live = (first > 0).astype(F32)
                p0 = jnp.maximum(r0 - 16, 0)
                before = [states(p0, 16, l0)[8:16, :] * live, states(p0, 16, ST_LANES + l0)[8:16, :] * live]
                a_re, a_im = acc
                for t, (gr, gi) in enumerate(tiles):
                    here = [c[8 * t:8 * t + 8, :] for c in cur]
                    sr, si = [jnp.where(row0, pltpu.roll(b, 1, 0), pltpu.roll(h, 1, 0)) for b, h in zip(before, here)]
                    a_re, a_im = a_re + gr * sr + gi * si, a_im + gi * sr - gr * si
                    before = here
                return a_re, a_im

            res = _scan_tiles(gr_s, gi_s, cr_ref, l0, nch, True, extra=dab_group)
            dab_ref[0:1, lanes] += _csum(res[2])
            dab_ref[1:2, lanes] += _csum(res[3])
        for r in range(S // rc):
            rows = pl.ds(r * rc, rc)
            st = st_ref[rows, :]
            g = jnp.concatenate([gr_s[rows, :], gi_s[rows, :]], axis=1).astype(BF16)
            dyb = dy_s[rows, :].astype(BF16)
            dwc_ref[...] += lax.dot_general(st, dyb, (((0,), (0,)), ((), ())), preferred_element_type=F32)
            dwb_ref[...] += lax.dot_general(u_ref[rows, :].astype(BF16), g, (((0,), (0,)), ((), ())), preferred_element_type=F32)
            du = lax.dot_general(g, wb_ref[...], (((1,), (1,)), ((), ())), preferred_element_type=F32)
            du_ref[rows, :] = du + d_ref[...] * dy_s[rows, :]

        @pl.when(step == ng * n_ex - 1)
        def _():
            exch.wait()

    return pl.pallas_call(
        body, name=name, grid=(ng, n_ex),
        in_specs=[pl.BlockSpec((S, U_LANES), lambda g, e: (e, g))] * 3 + [
            pl.BlockSpec((S, 2 * ST_LANES), lambda g, e: (e, g)),
            pl.BlockSpec((None, U_LANES, 2 * ST_LANES), lambda g, e: (g, 0, 0)),
            pl.BlockSpec((None, 2 * ST_LANES, U_LANES), lambda g, e: (g, 0, 0)),
            pl.BlockSpec((None, 8, 8, ST_LANES), lambda g, e: (g, 0, 0, 0)),
            pl.BlockSpec((1, U_LANES), lambda g, e: (0, g)), ANY],
        out_specs=[pl.BlockSpec((S, U_LANES), lambda g, e: (e, g)),
                   pl.BlockSpec((None, U_LANES, 2 * ST_LANES), lambda g, e: (g, 0, 0)),
                   pl.BlockSpec((None, 2 * ST_LANES, U_LANES), lambda g, e: (g, 0, 0)),
                   pl.BlockSpec((None, 8, ST_LANES), lambda g, e: (g, 0, 0)),
                   pl.BlockSpec((None, 8, U_LANES), lambda g, e: (g, 0, 0)), ANY],
        out_shape=[jax.ShapeDtypeStruct((T, D), F32),
                   jax.ShapeDtypeStruct((ng, U_LANES, 2 * ST_LANES), F32),
                   jax.ShapeDtypeStruct((ng, 2 * ST_LANES, U_LANES), F32),
                   jax.ShapeDtypeStruct((ng, 8, ST_LANES), F32),
                   jax.ShapeDtypeStruct((ng, 8, U_LANES), F32), _ChipExchange.out_shape(xsrc, True)],
        scratch_shapes=[pltpu.VMEM((S, ST_LANES), F32)] * 2 + [pltpu.VMEM((S, U_LANES), F32)] + _ChipExchange.SCRATCH,
        compiler_params=_cp(("arbitrary", "arbitrary")),
    )(u, y, dgy, st, Wb, Wc, cr, d, xsrc)


TQ = 256
KW = 512
SUB = 128


def _head_masks():
    lane = lax.broadcasted_iota(jnp.int32, (1, 2 * HEAD_DIM), 1)
    m0 = (lane < HEAD_DIM).astype(F32)
    return m0, 1.0 - m0


def _head_norm(x, g, m0, m1):
    sq = x * x
    r0 = lax.rsqrt(jnp.sum(sq * m0, axis=-1, keepdims=True) / HEAD_DIM + EPS)
    r1 = lax.rsqrt(jnp.sum(sq * m1, axis=-1, keepdims=True) / HEAD_DIM + EPS)
    r = m0 * r0 + m1 * r1
    return x * r, r


def _head_norm_bwd(dy, n, r, g, m0, m1):
    dn = dy * g
    p = dn * n
    mean = (m0 * jnp.sum(p * m0, axis=-1, keepdims=True) + m1 * jnp.sum(p * m1, axis=-1, keepdims=True)) / HEAD_DIM
    return r * (dn - n * mean), _csum(dy * n)


def _pair_matrix(kind):
    r = lax.broadcasted_iota(jnp.int32, (2 * SUB, 2 * SUB), 0)
    c = lax.broadcasted_iota(jnp.int32, (2 * SUB, 2 * SUB), 1)
    same = (r < SUB) == (c < SUB)
    rel = {"after": r > c, "upto": r <= c, "before": r < c}[kind]
    return jnp.logical_and(same, rel).astype(BF16)


def _block_sums(x, mat, carry, reverse, terms=2):
    hi = x.astype(BF16)
    lo = (x - hi.astype(F32)).astype(BF16) if terms == 2 else None
    npair = x.shape[1] // (2 * SUB)
    parts = [None] * (2 * npair)
    for p in (range(npair - 1, -1, -1) if reverse else range(npair)):
        sl = slice(2 * SUB * p, 2 * SUB * (p + 1))
        loc = jnp.dot(hi[:, sl], mat, preferred_element_type=F32)
        if terms == 2:
            loc = loc + jnp.dot(lo[:, sl], mat, preferred_element_type=F32)
        for b in ((1, 0) if reverse else (0, 1)):
            k = 2 * p + b
            parts[k] = loc[:, SUB * b:SUB * (b + 1)] + carry
            carry = carry + jnp.sum(x[:, SUB * k:SUB * (k + 1)], axis=-1, keepdims=True)
    return jnp.concatenate(parts, axis=1), carry


def _sb_logits(z, mask):
    lp = jnp.minimum(z, 0.0) - jnp.log(1.0 + jnp.exp(-jnp.abs(z)))
    lf = lp - z
    if mask is not None:
        lf = jnp.where(mask, lf, 0.0)
    return lp, lf


def _causal_mask(row0, col0, kw):
    r = row0 + lax.broadcasted_iota(jnp.int32, (TQ, kw), 0)
    c = col0 + lax.broadcasted_iota(jnp.int32, (TQ, kw), 1)
    return c < r


def _transposed_windows(x, ref):
    for w in range(x.shape[0] // KW):
        ref[w] = x[w * KW:(w + 1) * KW, :].T.astype(BF16)


def _attn_fwd(q, kv, qg, kg, xsrc, *, n_ex, name):
    T, D = q.shape
    S = T // n_ex
    nhp = D // (2 * HEAD_DIM)
    nq = S // TQ
    scale = 1.0 / math.sqrt(HEAD_DIM)

    def body(q_ref, k_ref, v_ref, qg_ref, kg_ref, xsrc_ref, o_ref, tot_ref, ot_ref, xout_ref, kT_s, qm_s, vm_s, *sems):
        step = pl.program_id(0) * nhp + pl.program_id(1)
        exch = _ChipExchange(xsrc_ref, xout_ref, *sems, scatter=False)

        @pl.when(step == 0)
        def _():
            exch.start()

        m0, m1 = _head_masks()
        qn, _ = _head_norm(q_ref[...], None, m0, m1)
        qn = qn * (qg_ref[...] * scale)
        kn, _ = _head_norm(k_ref[...], None, m0, m1)
        _transposed_windows(kn * kg_ref[...], kT_s)
        v = v_ref[...]
        for h, m in enumerate((m0, m1)):
            qm_s[h] = (qn * m).astype(BF16)
            vm_s[h] = (v * m).astype(BF16)
        u_after = _pair_matrix("after")

        def window(rows, win, st, mask, kw):
            keys = pl.ds(pl.multiple_of(win * KW, KW), kw)
            zs = [jnp.dot(qm_s[h, rows, :], kT_s[win, :, :kw], preferred_element_type=F32) for h in range(2)]
            lg = [_sb_logits(zs[h], mask) for h in range(2)]
            sums = [_block_sums(lg[h][1], u_after, st[2 * h], True) for h in range(2)]
            out = ()
            for h in range(2):
                w = jnp.exp(lg[h][0] + sums[h][0])
                if mask is not None:
                    w = jnp.where(mask, w, 0.0)
                out += (sums[h][1], st[2 * h + 1] + jnp.dot(w.astype(BF16), vm_s[h, keys, :], preferred_element_type=F32))
            return out

        def qtile(iq, last, kw):
            rows = pl.ds(pl.multiple_of(iq * TQ, TQ), TQ)
            mask = _causal_mask(iq * TQ, last * KW, kw)
            z1, zq = jnp.zeros((TQ, 1), F32), jnp.zeros((TQ, 2 * HEAD_DIM), F32)
            st = window(rows, last, (z1, zq, z1, zq), mask, kw)
            st = lax.fori_loop(0, last, lambda jj, st: window(rows, last - 1 - jj, st, None, KW), st)
            o_ref[rows, :] = st[1] + st[3]
            tot_ref[rows, :] = st[0] * m0 + st[2] * m1

        def qtiles_of_window(a, _):
            for sub in range(KW // TQ):
                qtile(a * (KW // TQ) + sub, a, (sub + 1) * TQ)
            return 0

        lax.fori_loop(0, S // KW, qtiles_of_window, 0)
        ot_ref[...] = o_ref[...].T.astype(BF16)

        @pl.when(step == n_ex * nhp - 1)
        def _():
            exch.wait()

    assert S % KW == 0 and KW % TQ == 0
    nwin = S // KW
    blk = (S, 2 * HEAD_DIM)
    return pl.pallas_call(
        body, name=name, grid=(n_ex, nhp),
        in_specs=[pl.BlockSpec(blk, lambda e, h: (e, h)), pl.BlockSpec(blk, lambda e, h: (e, h)),
                  pl.BlockSpec(blk, lambda e, h: (e, h + nhp)),
                  pl.BlockSpec((1, 2 * HEAD_DIM), lambda e, h: (0, 0)), pl.BlockSpec((1, 2 * HEAD_DIM), lambda e, h: (0, 0)), ANY],
        out_specs=[pl.BlockSpec(blk, lambda e, h: (e, h))] * 2 + [pl.BlockSpec((2 * HEAD_DIM, S), lambda e, h: (h, e)), ANY],
        out_shape=[jax.ShapeDtypeStruct((T, D), F32)] * 2 + [jax.ShapeDtypeStruct((D, T), BF16), _ChipExchange.out_shape(xsrc, False)],
        scratch_shapes=[pltpu.VMEM((nwin, 2 * HEAD_DIM, KW), BF16), pltpu.VMEM((2,) + blk, BF16), pltpu.VMEM((2,) + blk, BF16)]
        + _ChipExchange.SCRATCH,
        compiler_params=_cp(("arbitrary", "arbitrary")),
    )(q, kv, kv, qg, kg, xsrc)


def _attn_bwd(q, kv, tot, do, qg, kg, *, n_ex, name):
    T, D = q.shape
    S = T // n_ex
    nhp = D // (2 * HEAD_DIM)
    nq = S // TQ
    scale = 1.0 / math.sqrt(HEAD_DIM)

    def body(q_ref, k_ref, v_ref, tot_ref, do_ref, qg_ref, kg_ref, dq_ref, dk_ref, dv_ref, dqg_ref, dkg_ref,
             kT_s, vT_s, km_s, qm_s, dom_s, dqn_s, dkT_s, dvT_s):
        m0, m1 = _head_masks()
        qn, qr = _head_norm(q_ref[...], None, m0, m1)
        kn, kr = _head_norm(k_ref[...], None, m0, m1)
        qs = qn * (qg_ref[...] * scale)
        kk = kn * kg_ref[...]
        _transposed_windows(kk, kT_s)
        _transposed_windows(v_ref[...], vT_s)
        do = do_ref[...]
        for h, m in enumerate((m0, m1)):
            qm_s[h] = (qs * m).astype(BF16)
            km_s[h] = (kk * m).astype(BF16)
            dom_s[h] = (do * m).astype(BF16)
        dkT_s[...] = jnp.zeros_like(dkT_s)
        dvT_s[...] = jnp.zeros_like(dvT_s)
        u_upto, u_before = _pair_matrix("upto"), _pair_matrix("before")

        def both(inv, win, st, mask, kw):
            keys = pl.ds(pl.multiple_of(win * KW, KW), kw)
            lg = [_sb_logits(jnp.dot(inv[h][0], kT_s[win, :, :kw], preferred_element_type=F32), mask) for h in range(2)]
            s_lf = [_block_sums(lg[h][1], u_upto, st[3 * h], False) for h in range(2)]
            ws, ews = [], []
            for h in range(2):
                w = jnp.exp(lg[h][0] - s_lf[h][0])
                if mask is not None:
                    w = jnp.where(mask, w, 0.0)
                ws.append(w)
                ews.append(jnp.dot(inv[h][2], vT_s[win, :, :kw], preferred_element_type=F32) * w)
            s_e = [_block_sums(ews[h], u_before, st[3 * h + 1], False, terms=1) for h in range(2)]
            out, dk, dv = (), None, None
            for h in range(2):
                sig = jnp.exp(lg[h][0])
                dz = ews[h] - sig * (ews[h] + s_e[h][0])
                if mask is not None:
                    dz = jnp.where(mask, dz, 0.0)
                dzb = dz.astype(BF16)
                out += (s_lf[h][1], s_e[h][1], st[3 * h + 2] + jnp.dot(dzb, km_s[h, keys, :], preferred_element_type=F32))
                dkh = jnp.dot(inv[h][1], dzb, preferred_element_type=F32)
                dvh = jnp.dot(inv[h][3], ws[h].astype(BF16), preferred_element_type=F32)
                dk, dv = (dkh, dvh) if h == 0 else (dk + dkh, dv + dvh)
            dkT_s[win, :, :kw] += dk
            dvT_s[win, :, :kw] += dv
            return out

        def qtile(iq, last, kw):
            rows = pl.ds(pl.multiple_of(iq * TQ, TQ), TQ)
            mask = _causal_mask(iq * TQ, last * KW, kw)
            tt = tot_ref[rows, :]
            inv, neg_total = [], []
            for h, m in enumerate((m0, m1)):
                qh, doh = qm_s[h, rows, :], dom_s[h, rows, :]
                neg_total.append(jnp.sum(tt * m, axis=-1, keepdims=True) * (-1.0 / HEAD_DIM))
                inv.append((qh, qh.astype(F32).T.astype(BF16), doh, doh.astype(F32).T.astype(BF16)))

            z1, zq = jnp.zeros((TQ, 1), F32), jnp.zeros((TQ, 2 * HEAD_DIM), F32)
            st = lax.fori_loop(0, last, lambda win, st: both(inv, win, st, None, KW), (neg_total[0], z1, zq, neg_total[1], z1, zq))
            st = both(inv, last, st, mask, kw)
            dqn_s[rows, :] = st[2] + st[5]

        def qtiles_of_window(a, _):
            for sub in range(KW // TQ):
                qtile(a * (KW // TQ) + sub, a, (sub + 1) * TQ)
            return 0

        lax.fori_loop(0, S // KW, qtiles_of_window, 0)
        dkn = jnp.concatenate([dkT_s[w].T for w in range(nwin)], axis=0)
        dq, dqg = _head_norm_bwd(dqn_s[...] * scale, qn, qr, qg_ref[...], m0, m1)
        dk, dkg = _head_norm_bwd(dkn, kn, kr, kg_ref[...], m0, m1)
        dq_ref[...] = dq
        dk_ref[...] = dk
        dv_ref[...] = jnp.concatenate([dvT_s[w].T for w in range(nwin)], axis=0)
        dqg_ref[...] = dqg
        dkg_ref[...] = dkg

    assert S % KW == 0 and KW % TQ == 0
    nwin = S // KW
    blk = (S, 2 * HEAD_DIM)
    tblk = (nwin, 2 * HEAD_DIM, KW)
    gblk = (None, None, 1, 2 * HEAD_DIM)
    dq, dk, dv, dqg, dkg = pl.pallas_call(
        body, name=name, grid=(n_ex, nhp),
        in_specs=[pl.BlockSpec(blk, lambda e, h: (e, h)), pl.BlockSpec(blk, lambda e, h: (e, h)),
                  pl.BlockSpec(blk, lambda e, h: (e, h + nhp)),
                  pl.BlockSpec(blk, lambda e, h: (e, h)), pl.BlockSpec(blk, lambda e, h: (e, h)),
                  pl.BlockSpec((1, 2 * HEAD_DIM), lambda e, h: (0, 0)), pl.BlockSpec((1, 2 * HEAD_DIM), lambda e, h: (0, 0))],
        out_specs=[pl.BlockSpec(blk, lambda e, h: (e, h))] * 3 + [pl.BlockSpec(gblk, lambda e, h: (e, h, 0, 0))] * 2,
        out_shape=[jax.ShapeDtypeStruct((T, D), F32)] * 3 + [jax.ShapeDtypeStruct((n_ex, nhp, 1, 2 * HEAD_DIM), F32)] * 2,
        scratch_shapes=[pltpu.VMEM(tblk, BF16), pltpu.VMEM(tblk, BF16),
                        pltpu.VMEM((2,) + blk, BF16), pltpu.VMEM((2,) + blk, BF16), pltpu.VMEM((2,) + blk, BF16),
                        pltpu.VMEM(blk, F32), pltpu.VMEM(tblk, F32), pltpu.VMEM(tblk, F32)],
        compiler_params=_cp(("parallel", "parallel")),
    )(q, kv, kv, tot, do, qg, kg)
    return dq, dk, dv, dqg, dkg


def _place():
    return lax.axis_index("x"), lax.axis_index("y"), lax.axis_index("c")


def _all_gather8(x_shard, *, name):
    m_per, n = x_shard.shape

    def body(x_ref, out_ref, send_sems, recv_sems, local_sem):
        x, y, c = _place()
        me, sibling = (x, y, c), (x, y, 1 - c)
        chips = [(1 - x, y), (x, 1 - y), (1 - x, 1 - y)]

        def rows(px, py, pc):
            return out_ref.at[pl.ds((4 * px + 2 * py + pc) * m_per, m_per), :]

        def copy(k, block, to, src=None):
            return pltpu.make_async_remote_copy(
                src_ref=rows(*block) if src is None else src, dst_ref=rows(*block),
                send_sem=send_sems.at[k], recv_sem=recv_sems.at[k], device_id=to, device_id_type=MESH)

        mine = pltpu.make_async_copy(x_ref, rows(*me), local_sem)
        mine.start()
        first = [copy(0, me, sibling, src=x_ref)]
        first += [copy(1 + j, me, (*chip, c), src=x_ref) for j, chip in enumerate(chips)]
        for cp in first:
            cp.start()
        passed = [copy(4 + j, (*chip, c), sibling) for j, chip in enumerate(chips)]
        for j, chip in enumerate(chips):
            copy(1 + j, (*chip, c), me).wait_recv()
            passed[j].start()
        copy(0, sibling, me).wait_recv()
        for j, chip in enumerate(chips):
            copy(4 + j, (*chip, 1 - c), me).wait_recv()
        for cp in first + passed:
            cp.wait_send()
        mine.wait()

    return pl.pallas_call(
        body, name=name, out_shape=jax.ShapeDtypeStruct((8 * m_per, n), x_shard.dtype),
        in_specs=[pl.BlockSpec(memory_space=pltpu.VMEM)], out_specs=pl.BlockSpec(memory_space=pltpu.VMEM),
        scratch_shapes=[pltpu.SemaphoreType.DMA((7,)), pltpu.SemaphoreType.DMA((7,)), pltpu.SemaphoreType.DMA],
        compiler_params=pltpu.CompilerParams(vmem_limit_bytes=VMEM_LIMIT),
    )(x_shard)


def _sibling_sum_half(x, *, name):
    R, C = x.shape
    half = R // 2

    def body(x_ref, o_ref, theirs, send_sem, recv_sem):
        px, py, pc = _place()
        cp = pltpu.make_async_remote_copy(src_ref=x_ref, dst_ref=theirs, send_sem=send_sem, recv_sem=recv_sem,
                                          device_id=(px, py, 1 - pc), device_id_type=MESH)
        cp.start()
        cp.wait()
        rows = pl.ds(pl.multiple_of(pc * half, 8), half)
        o_ref[...] = x_ref[rows, :] + theirs[rows, :]

    return pl.pallas_call(
        body, name=name, out_shape=jax.ShapeDtypeStruct((half, C), x.dtype),
        in_specs=[pl.BlockSpec(memory_space=pltpu.VMEM)], out_specs=pl.BlockSpec(memory_space=pltpu.VMEM),
        scratch_shapes=[pltpu.VMEM((R, C), x.dtype), pltpu.SemaphoreType.DMA, pltpu.SemaphoreType.DMA],
        compiler_params=pltpu.CompilerParams(vmem_limit_bytes=VMEM_LIMIT),
    )(x)


def _sum_blocks(x, n, *, name):
    R = x.shape[0] // n

    def body(x_ref, o_ref):
        acc = x_ref[pl.ds(0, R), :]
        for k in range(1, n):
            acc = acc + x_ref[pl.ds(k * R, R), :]
        o_ref[...] = acc

    return pl.pallas_call(body, name=name, out_shape=jax.ShapeDtypeStruct((R, x.shape[1]), x.dtype),
                          compiler_params=pltpu.CompilerParams(vmem_limit_bytes=VMEM_LIMIT))(x)


def _colsum(x, *, name):
    def body(x_ref, o_ref):
        o_ref[...] = jnp.sum(x_ref[...], axis=0, keepdims=True)
    return pl.pallas_call(body, name=name, out_shape=jax.ShapeDtypeStruct((1, x.shape[1]), x.dtype))(x)


ANY = pl.BlockSpec(memory_space=pl.ANY)


class _ChipExchange:
    SCRATCH = [pltpu.SemaphoreType.DMA((3,)), pltpu.SemaphoreType.DMA((3,)), pltpu.SemaphoreType.DMA]

    @staticmethod
    def out_shape(src, scatter):
        return jax.ShapeDtypeStruct(((4,) + tuple(src.shape[1:])) if scatter else ((4, 2) + tuple(src.shape[1:])), src.dtype)

    def __init__(self, src_ref, out_ref, send_sems, recv_sems, local_sem, scatter):
        x, y, c = _place()
        myj = 2 * x + y
        chips = [(1 - x, y), (x, 1 - y), (1 - x, 1 - y)]

        def slot(j):
            return out_ref.at[j] if scatter else out_ref.at[j, c]

        def piece(j):
            return src_ref.at[j] if scatter else src_ref.at[c]

        self.mine = pltpu.make_async_copy(piece(myj), slot(myj), local_sem)
        self.sends = [pltpu.make_async_remote_copy(
            src_ref=piece(2 * cx + cy), dst_ref=slot(myj), send_sem=send_sems.at[k], recv_sem=recv_sems.at[k],
            device_id=(cx, cy, c), device_id_type=MESH) for k, (cx, cy) in enumerate(chips)]
        self.recvs = [pltpu.make_async_remote_copy(
            src_ref=slot(2 * cx + cy), dst_ref=slot(2 * cx + cy), send_sem=send_sems.at[k], recv_sem=recv_sems.at[k],
            device_id=(cx, cy, c), device_id_type=MESH) for k, (cx, cy) in enumerate(chips)]

    def start(self):
        self.mine.start()
        for cp in self.sends:
            cp.start()

    def wait(self):
        for cp in self.recvs:
            cp.wait_recv()
        for cp in self.sends:
            cp.wait_send()
        self.mine.wait()


def _sibling_fill(buf, *, axis, name):
    def half(ref, h):
        return ref.at[h] if axis == 0 else ref.at[:, h]

    def body(in_ref, out_ref, send_sem, recv_sem):
        x, y, c = _place()
        cp = pltpu.make_async_remote_copy(src_ref=half(out_ref, c), dst_ref=half(out_ref, c), send_sem=send_sem, recv_sem=recv_sem,
                                          device_id=(x, y, 1 - c), device_id_type=MESH)
        cp.start()
        pltpu.make_async_remote_copy(src_ref=half(out_ref, 1 - c), dst_ref=half(out_ref, 1 - c), send_sem=send_sem, recv_sem=recv_sem,
                                     device_id=(x, y, 1 - c), device_id_type=MESH).wait_recv()
        cp.wait_send()

    return pl.pallas_call(
        body, name=name, out_shape=jax.ShapeDtypeStruct(buf.shape, buf.dtype), in_specs=[ANY], out_specs=ANY,
        input_output_aliases={0: 0}, scratch_shapes=[pltpu.SemaphoreType.DMA, pltpu.SemaphoreType.DMA],
    )(buf)


def _sibling_swap_half(g, *, name):
    def body(g_ref, out_ref, send_sem, recv_sem):
        x, y, c = _place()
        cp = pltpu.make_async_remote_copy(src_ref=g_ref.at[:, 1 - c], dst_ref=out_ref, send_sem=send_sem, recv_sem=recv_sem,
                                          device_id=(x, y, 1 - c), device_id_type=MESH)
        cp.start()
        cp.wait()

    return pl.pallas_call(
        body, name=name, out_shape=jax.ShapeDtypeStruct((g.shape[0],) + g.shape[2:], g.dtype), in_specs=[ANY], out_specs=ANY,
        scratch_shapes=[pltpu.SemaphoreType.DMA, pltpu.SemaphoreType.DMA],
    )(g)


def _add_my_half(g, b, cidx, *, name, tr=256):
    n, _, R, C = g.shape
    tr = math.gcd(tr, R)

    def body(c_ref, g_ref, b_ref, o_ref):
        o_ref[...] = (g_ref[...] + b_ref[...]).astype(o_ref.dtype)

    return pl.pallas_call(
        body, name=name, out_shape=jax.ShapeDtypeStruct((n, R, C), BF16),
        grid_spec=pltpu.PrefetchScalarGridSpec(
            num_scalar_prefetch=1, grid=(n, R // tr),
            in_specs=[pl.BlockSpec((None, None, tr, C), lambda j, i, c: (j, c[0], i, 0)),
                      pl.BlockSpec((None, tr, C), lambda j, i, c: (j, i, 0))],
            out_specs=pl.BlockSpec((None, tr, C), lambda j, i, c: (j, i, 0))),
        compiler_params=_cp(("parallel", "parallel")),
    )(cidx, g, b)


def _sum4_into_half(q, cidx, *, name, tr=256):
    _, R, C = q.shape
    tr = math.gcd(tr, R)

    def body(c_ref, q_ref, o_ref):
        o_ref[...] = ((q_ref[0].astype(F32) + q_ref[1].astype(F32)) + q_ref[2].astype(F32)) + q_ref[3].astype(F32)

    return pl.pallas_call(
        body, name=name, out_shape=jax.ShapeDtypeStruct((2, R, C), F32),
        grid_spec=pltpu.PrefetchScalarGridSpec(
            num_scalar_prefetch=1, grid=(R // tr,),
            in_specs=[pl.BlockSpec((4, tr, C), lambda i, c: (0, i, 0))],
            out_specs=pl.BlockSpec((None, tr, C), lambda i, c: (c[0], i, 0))),
        compiler_params=_cp(("parallel",)),
    )(cidx, q)


def _pack_rows(parts, width=1024):
    rows, spans, r0 = [], [], 0
    for p in parts:
        n = p.size
        nr = 8 * (-(-n // (8 * width)))
        flat = p.reshape(-1)
        if nr * width != n:
            flat = jnp.pad(flat, (0, nr * width - n))
        rows.append(flat.reshape(nr, width))
        spans.append((r0, nr, n, p.shape))
        r0 += nr
    return jnp.concatenate(rows, axis=0), spans


def _unpack_rows(buf, spans):
    return [buf[r0:r0 + nr].reshape(-1)[:n].reshape(shape) for (r0, nr, n, shape) in spans]


def kernel(x, c, ada_w, ada_b, mix_norm_g, mlp_norm_g, mlp_w1, mlp_w2, s5_a_re, s5_a_im, s5_log_dt, s5_b_re, s5_b_im, s5_c_re, s5_c_im, s5_d, s5_w_glu, kv_ada_w, kv_ada_b, kv_norm_g, w_kv, k_norm_g, sb_w_q, q_norm_g, sb_w_o, loss_target, m_ada_w, m_ada_b, m_mix_norm_g, m_mlp_norm_g, m_mlp_w1, m_mlp_w2, m_s5_a_re, m_s5_a_im, m_s5_log_dt, m_s5_b_re, m_s5_b_im, m_s5_c_re, m_s5_c_im, m_s5_d, m_s5_w_glu, m_kv_ada_w, m_kv_ada_b, m_kv_norm_g, m_w_kv, m_k_norm_g, m_sb_w_q, m_q_norm_g, m_sb_w_o, v_ada_w, v_ada_b, v_mix_norm_g, v_mlp_norm_g, v_mlp_w1, v_mlp_w2, v_s5_a_re, v_s5_a_im, v_s5_log_dt, v_s5_b_re, v_s5_b_im, v_s5_c_re, v_s5_c_im, v_s5_d, v_s5_w_glu, v_kv_ada_w, v_kv_ada_b, v_kv_norm_g, v_w_kv, v_k_norm_g, v_sb_w_q, v_q_norm_g, v_sb_w_o):
    E, S, D = x.shape
    T = E * S
    FF = 4 * D
    NB = 8 * E
    px, py, pc = _place()
    chip = 2 * px + py
    dev = 4 * px + 2 * py + pc
    cidx = jnp.reshape(pc, (1,)).astype(jnp.int32)
    x0 = x.reshape(T, D)
    tgt = loss_target.reshape(T, D)

    c_all = _all_gather8(c.reshape(-1, 128), name="ag_c").reshape(NB, D)
    sc_all = (c_all * _sigmoid(c_all)).astype(BF16)
    wa = ada_w.shape[2]
    wk = kv_ada_w.shape[1]
    m_sh = jnp.concatenate([_mm(sc_all, _Layer(ada_w, 0), "nn", name="ada0", tn=256),
                            _mm(sc_all, _Layer(ada_w, 1), "nn", name="ada1", tn=256),
                            _mm(sc_all, kv_ada_w, "nn", name="ada_kv", tn=256)], axis=1)
    m_all = _all_gather8(m_sh, name="ag_m").reshape(4, 2, NB, 2 * wa + wk)[:, 0]
    mods = []
    for l in range(2):
        full = jnp.transpose(m_all[:, :, l * wa:(l + 1) * wa], (1, 0, 2)).reshape(NB, 6 * D) + ada_b[l]
        mine = lax.dynamic_slice_in_dim(full, E * dev, E, axis=0)
        mods.append([mine[:, i * D:(i + 1) * D].reshape(E, 1, D) for i in range(6)])
    full = jnp.transpose(m_all[:, :, 2 * wa:], (1, 0, 2)).reshape(NB, 2 * D) + kv_ada_b
    mine = lax.dynamic_slice_in_dim(full, E * dev, E, axis=0)
    kv_sh, kv_sc = [mine[:, i * D:(i + 1) * D].reshape(E, 1, D) for i in range(2)]

    wpack_a = jnp.concatenate([mlp_w1[0], mlp_w2[0], jnp.concatenate([s5_w_glu[0], w_kv], axis=1), sb_w_q[0]], axis=0).astype(BF16)
    wpack_b = jnp.concatenate([mlp_w1[1], mlp_w2[1], sb_w_o[0]], axis=0).astype(BF16)
    RA, RB = wpack_a.shape[0], wpack_b.shape[0]
    RW = RA + RB

    tm = min(2048, S)
    tm_res = min(1024, S)
    gbuf = [jax.ShapeDtypeStruct((4, RW, D), F32)]

    def grad_mm(act, dout, kind, roff, nr, c0, nc, name, transposed=False):
        gbuf[0] = _mm(act, dout, "nn" if transposed else "tn", name=name, tm=1024, tk=2048,
                      into=_Sharded(gbuf[0], kind, roff, nr, c0, nc))

    def mlp_fwd(xa, l, mod):
        sh_m, sc_m, g_m = mod[3], mod[4], mod[5]
        h, h_t = _norm_mod_fwd(xa, mlp_norm_g[l:l + 1], sh_m, sc_m, n_ex=E, out_dtype=BF16, name=f"mlp_norm{l}", with_transpose=True)

        def relu_sq(acc):
            ra = jnp.maximum(acc, 0.0)
            return ra * ra, ra
        r, ra = _mm(h, W1[l], "nn", name=f"mlp_up{l}", out_dtypes=(BF16, BF16), tm=tm, epilogue=relu_sq)
        xb, ff = _mm(r, W2[l], "nn", name=f"mlp_down{l}", out_dtypes=(F32, F32), tm=tm_res,
                     extras=[_mn_extra(xa), _vec_extra(g_m, S)],
                     epilogue=lambda acc, xat, gt: (xat + gt * acc, acc))
        return xb, (h_t, r, ra, ff)

    def mlp_bwd(dxb, xa, l, mod, saved):
        sc_m, g_m = mod[4], mod[5]
        h_t, r, ra, ff = saved
        (dff,), (dgm,) = _rowwise(lambda d, f, g: ([g * d], [_csum(d * f)]), [(dxb, D, 0), (ff, D, 0)], [g_m], [],
                                  [(D, BF16)], [D], n_ex=E, name=f"mlp_gate_bwd{l}")
        da = _mm(dff, W2[l], "nt", name=f"mlp_down_dx{l}", out_dtypes=(BF16,), tm=tm, extras=[_mn_extra(ra)],
                 epilogue=lambda acc, rat: (acc * (2.0 * rat.astype(F32)),))
        grad_mm(r, dff, "rows", (2 + l) * D, D, 0, D, f"mlp_down_dw{l}")
        dh = _mm(da, W1[l], "nt", name=f"mlp_up_dx{l}", tm=tm)
        grad_mm(h_t, da, "cols", l * D, D, 0, D, f"mlp_up_dw{l}", transposed=True)
        (dxa,), (dsh, dsc, dg) = _norm_mod_bwd(xa, dh, dxb, mlp_norm_g[l:l + 1], sc_m, n_ex=E, name=f"mlp_norm_bwd{l}")
        return dxa, (dsh, dsc, dgm), dg

    ab_re, ab_im, bb_re, bb_im = _s5_disc(s5_a_re[0], s5_a_im[0], s5_log_dt[0], s5_b_re[0], s5_b_im[0])
    cf, cr = _s5_consts(ab_re, ab_im)
    Wb, Wc = _s5_blockdiag(bb_re, bb_im, s5_c_re[0], s5_c_im[0])
    ng = D // U_LANES
    nd = s5_d.size // 128
    d_full = _all_gather8(jnp.pad(s5_d.reshape(nd, 128), ((0, 8 - nd), (0, 0))), name="ag_d")
    d_full = d_full.reshape(4, 2, 8, 128)[:, 0, :nd].reshape(1, D)

    mod0, mod1 = mods
    h0 = _norm_mod_fwd(x0, mix_norm_g[0:1], mod0[0], mod0[1], n_ex=E, out_dtype=F32, name="mix_norm0")
    y, gy, gy_t, s5_states, wfull_a = _s5_fwd(h0, Wb, Wc, cf, d_full, wpack_a.reshape(2, RA // 2, D), n_ex=E, name="s5_fwd")
    wfull_a = _sibling_fill(wfull_a, axis=1, name="wgather_a_d2d").reshape(4, RA, D)

    W1 = [_Sharded(wfull_a, "cols", 0, D, 0, D), None]
    W2 = [_Sharded(wfull_a, "rows", D, D, 0, D), None]
    Wglu = _Sharded(wfull_a, "cols", 2 * D, D, 0, D // 2)
    Wkv = _Sharded(wfull_a, "cols", 2 * D, D, D // 2, D // 2)
    Wq = _Sharded(wfull_a, "rows", 3 * D, D // 4, 0, D)
    vg = _mm(gy, Wglu, "nn", name="glu_up", tm=tm)
    (x1,), _ = _rowwise(lambda v, g, xt, ga: ([xt + ga * (v * _sigmoid(g))], []),
                        [(vg, D, 0), (vg, D, 1), (x0, D, 0)], [mod0[2]], [], [(D, F32)], [], n_ex=E, name="glu_gate")
    x2, saved_mlp0 = mlp_fwd(x1, 0, mod0)

    hkv, hkv_t = _norm_mod_fwd(x2, kv_norm_g.reshape(1, D), kv_sh, kv_sc, n_ex=E, out_dtype=BF16, name="kv_norm", with_transpose=True)
    kvf = _mm(hkv, Wkv, "nn", name="kv_proj", tm=tm)
    h1, h1_t = _norm_mod_fwd(x2, mix_norm_g[1:2], mod1[0], mod1[1], n_ex=E, out_dtype=BF16, name="mix_norm1", with_transpose=True)
    qf = _mm(h1, Wq, "nn", name="q_proj", tm=tm)
    qg2 = jnp.tile(q_norm_g.reshape(1, HEAD_DIM), (1, 2))
    kg2 = jnp.tile(k_norm_g.reshape(1, HEAD_DIM), (1, 2))
    o, lf_tot, o_t, wfull_b = _attn_fwd(qf, kvf, qg2, kg2, wpack_b.reshape(2, RB // 2, D), n_ex=E, name="attn_fwd")
    wfull_b = _sibling_fill(wfull_b, axis=1, name="wgather_b_d2d").reshape(4, RB, D)
    W1[1] = _Sharded(wfull_b, "cols", 0, D, 0, D)
    W2[1] = _Sharded(wfull_b, "rows", D, D, 0, D)
    Wo = _Sharded(wfull_b, "rows", 2 * D, D // 4, 0, D)
    x3, mix1 = _mm(o, Wo, "nn", name="o_proj", out_dtypes=(F32, F32), tm=tm_res,
                   extras=[_mn_extra(x2), _vec_extra(mod1[2], S)],
                   epilogue=lambda acc, xat, gt: (xat + gt * acc, acc))
    x4, saved_mlp1 = mlp_fwd(x3, 1, mod1)

    (dx4,), (lsum,) = _rowwise(lambda xt, tt: ([(xt - tt) * (1.0 / D)], [_csum(jnp.square(xt - tt)) * (0.5 / D)]),
                               [(x4, D, 0), (tgt, D, 0)], [], [], [(D, F32)], [D], n_ex=E, name="loss")
    loss = lax.psum(jnp.sum(lsum), ("x", "y", "c"))

    dx3, (dsh_m1, dsc_m1, dgm1), dg_mlp1 = mlp_bwd(dx4, x3, 1, mod1, saved_mlp1)
    (dmix1,), (dga1,) = _rowwise(lambda d, f, g: ([g * d], [_csum(d * f)]), [(dx3, D, 0), (mix1, D, 0)], [mod1[2]], [],
                                 [(D, BF16)], [D], n_ex=E, name="attn_gate_bwd")
    do = _mm(dmix1, Wo, "nt", name="o_proj_dx", tm=tm)
    grad_mm(o_t, dmix1, "rows", 5 * D + D // 4, D // 4, 0, D, "o_proj_dw", transposed=True)
    dq, dk, dv, dqg, dkg = _attn_bwd(qf, kvf, lf_tot, do, qg2, kg2, n_ex=E, name="attn_bwd")
    dh1 = _mm(dq, Wq, "nt", name="q_proj_dx", tm=tm)
    grad_mm(h1_t, dq, "rows", 5 * D, D // 4, 0, D, "q_proj_dw", transposed=True)
    (dx2,), (dsh_a1, dsc_a1, dg_mix1) = _norm_mod_bwd(x2, dh1, dx3, mix_norm_g[1:2], mod1[1], n_ex=E, name="mix_norm_bwd1")
    dkv = jnp.concatenate([dk, dv], axis=1)
    dhkv = _mm(dkv, Wkv, "nt", name="kv_proj_dx", tm=tm)
    grad_mm(hkv_t, dkv, "cols", 4 * D, D, D // 2, D // 2, "kv_proj_dw", transposed=True)
    (dx2,), (dkv_sh, dkv_sc, dg_kv) = _norm_mod_bwd(x2, dhkv, dx2, kv_norm_g.reshape(1, D), kv_sc, n_ex=E, name="kv_norm_bwd")

    dx1, (dsh_m0, dsc_m0, dgm0), dg_mlp0 = mlp_bwd(dx2, x1, 0, mod0, saved_mlp0)

    def glu_bwd(v, g, d, ga):
        sg = _sigmoid(g)
        dm = ga * d
        return [jnp.concatenate([dm * sg, dm * v * sg * (1.0 - sg)], axis=1)], [_csum(d * (v * sg))]
    (dvg,), (dga0,) = _rowwise(glu_bwd, [(vg, D, 0), (vg, D, 1), (dx1, D, 0)], [mod0[2]], [], [(2 * D, BF16)], [D],
                               n_ex=E, name="glu_gate_bwd")
    dgy = _mm(dvg, Wglu, "nt", name="glu_up_dx", tm=tm)
    grad_mm(gy_t, dvg, "cols", 4 * D, D, 0, D // 2, "glu_up_dw", transposed=True)

    gpack = gbuf[0].reshape(4, 2, RW // 2, D)
    theirs = _sibling_swap_half(gpack, name="gscatter_d2d")
    chip_sum = _add_my_half(gpack, theirs, cidx, name="gscatter_add")
    dh0, dWb, dWc, dab, dd, from_chips = _s5_bwd(h0, y, dgy, s5_states, Wb, Wc, cr, d_full, chip_sum, n_ex=E, name="s5_bwd")
    ghalf = _sum4_into_half(from_chips, cidx, name="gscatter_sum")
    gsh = _sibling_fill(ghalf, axis=0, name="gscatter_fill").reshape(RW, D)
    (gx,), (dsh_a0, dsc_a0, dg_mix0) = _norm_mod_bwd(x0, dh0, dx1, mix_norm_g[0:1], mod0[1], n_ex=E, name="mix_norm_bwd0")
    grad_x = gx.reshape(E, S, D)

    dm_mine = jnp.concatenate([t.reshape(E, D) for t in
                               (dsh_a0, dsc_a0, dga0, dsh_m0, dsc_m0, dgm0, dsh_a1, dsc_a1, dga1, dsh_m1, dsc_m1, dgm1, dkv_sh, dkv_sc)], axis=1)
    dm_all = _all_gather8(dm_mine.reshape(8, -1), name="ag_dm").reshape(NB, 14 * D)
    sc_f32 = c_all * _sigmoid(c_all)
    g_ada_w = jax.ShapeDtypeStruct(ada_w.shape, F32)
    for l in range(2):
        g_ada_w = _mm(sc_f32, lax.dynamic_slice_in_dim(dm_all, l * 6 * D + chip * wa, wa, axis=1), "tn", name=f"ada_dw{l}", tn=256,
                      into=_Layer(g_ada_w, l))
    g_kv_ada_w = _mm(sc_f32, lax.dynamic_slice_in_dim(dm_all, 12 * D + chip * wk, wk, axis=1), "tn", name="ada_kv_dw", tn=256)
    db_all = _colsum(dm_all, name="ada_db")
    g_ada_b = db_all[0, :12 * D].reshape(2, 6 * D)
    g_kv_ada_b = db_all[0, 12 * D:]

    dWb_re, dWb_im, dC_re, dC_im = _s5_unblock(dWb, dWc)
    small_parts = [dg_mix0.sum(0), dg_mix1.sum(0), dg_mlp0.sum(0), dg_mlp1.sum(0), dg_kv.sum(0),
                   dqg.sum((0, 1, 2)).reshape(2, HEAD_DIM).sum(0), dkg.sum((0, 1, 2)).reshape(2, HEAD_DIM).sum(0),
                   dd[:, 0, :], dab[:, 0, :], dab[:, 1, :], dWb_re, dWb_im, dC_re, dC_im]
    spack, spans = _pack_rows(small_parts)
    chip_half = _sibling_sum_half(spack, name="small_d2d")
    ssum = _sum_blocks(_all_gather8(chip_half, name="ag_small"), 4, name="sum_small")
    (g_mix0, g_mix1, g_mlp0, g_mlp1, g_kvn, g_qn, g_kn, g_d, g_abr, g_abi, g_bbr, g_bbi, g_cre, g_cim) = _unpack_rows(ssum, spans)
    _, disc_vjp = jax.vjp(_s5_disc, s5_a_re[0], s5_a_im[0], s5_log_dt[0], s5_b_re[0], s5_b_im[0])
    g_are, g_aim, g_ldt, g_bre, g_bim = disc_vjp((g_abr.reshape(ab_re.shape), g_abi.reshape(ab_im.shape), g_bbr, g_bbi))
    g_s5d = lax.dynamic_slice_in_dim(g_d.reshape(1, D), chip * s5_d.shape[1], s5_d.shape[1], axis=1)

    def upd_big(w, m, v, roff, cb, name):
        shape = w.shape
        W = shape[-1]
        d_, m_, v_, g_ = _adamw2d(w.reshape(-1, W), gsh, m.reshape(-1, W), v.reshape(-1, W), name=name, g_roff=roff, g_cb=cb)
        return [t.reshape(shape) for t in (g_, d_, m_, v_)]

    def upd_own(w, g, m, v, name):
        shape = w.shape
        W = shape[-1]
        d_, m_, v_, g_ = _adamw2d(w.reshape(-1, W), g.reshape(-1, W), m.reshape(-1, W), v.reshape(-1, W), name=name)
        return [t.reshape(shape) for t in (g_, d_, m_, v_)]

    res = {}
    res["ada_w"] = upd_own(ada_w, g_ada_w, m_ada_w, v_ada_w, "adam_ada_w")
    res["kv_ada_w"] = upd_own(kv_ada_w, g_kv_ada_w, m_kv_ada_w, v_kv_ada_w, "adam_kv_ada_w")
    res["mlp_w1"] = upd_big(mlp_w1, m_mlp_w1, v_mlp_w1, 0, 0, "adam_w1")
    res["mlp_w2"] = upd_big(mlp_w2, m_mlp_w2, v_mlp_w2, 2 * D, 0, "adam_w2")
    res["s5_w_glu"] = upd_big(s5_w_glu, m_s5_w_glu, v_s5_w_glu, 4 * D, 0, "adam_glu")
    res["w_kv"] = upd_big(w_kv, m_w_kv, v_w_kv, 4 * D, 1, "adam_wkv")
    res["sb_w_q"] = upd_big(sb_w_q, m_sb_w_q, v_sb_w_q, 5 * D, 0, "adam_wq")
    res["sb_w_o"] = upd_big(sb_w_o, m_sb_w_o, v_sb_w_o, 5 * D + D // 4, 0, "adam_wo")

    small = {
        "ada_b": (ada_b, g_ada_b, m_ada_b, v_ada_b),
        "mix_norm_g": (mix_norm_g, jnp.stack([g_mix0, g_mix1]), m_mix_norm_g, v_mix_norm_g),
        "mlp_norm_g": (mlp_norm_g, jnp.stack([g_mlp0, g_mlp1]), m_mlp_norm_g, v_mlp_norm_g),
        "s5_a_re": (s5_a_re, g_are[None], m_s5_a_re, v_s5_a_re),
        "s5_a_im": (s5_a_im, g_aim[None], m_s5_a_im, v_s5_a_im),
        "s5_log_dt": (s5_log_dt, g_ldt[None], m_s5_log_dt, v_s5_log_dt),
        "s5_b_re": (s5_b_re, g_bre[None], m_s5_b_re, v_s5_b_re),
        "s5_b_im": (s5_b_im, g_bim[None], m_s5_b_im, v_s5_b_im),
        "s5_c_re": (s5_c_re, g_cre[None], m_s5_c_re, v_s5_c_re),
        "s5_c_im": (s5_c_im, g_cim[None], m_s5_c_im, v_s5_c_im),
        "s5_d": (s5_d, g_s5d, m_s5_d, v_s5_d),
        "kv_ada_b": (kv_ada_b, g_kv_ada_b, m_kv_ada_b, v_kv_ada_b),
        "kv_norm_g": (kv_norm_g, g_kvn, m_kv_norm_g, v_kv_norm_g),
        "k_norm_g": (k_norm_g, g_kn, m_k_norm_g, v_k_norm_g),
        "q_norm_g": (q_norm_g, g_qn.reshape(q_norm_g.shape), m_q_norm_g, v_q_norm_g),
    }
    names = list(small)
    packs = [_pack_rows([small[n][i].reshape(small[n][0].shape) for n in names]) for i in range(4)]
    sp = packs[0][1]
    d_, m_, v_, g_ = _adamw2d(packs[0][0], packs[1][0], packs[2][0], packs[3][0], name="adam_small")
    for n, gg, dd_, mm_, vv_ in zip(names, _unpack_rows(g_, sp), _unpack_rows(d_, sp), _unpack_rows(m_, sp), _unpack_rows(v_, sp)):
        res[n] = [gg, dd_, mm_, vv_]

    order = ["ada_w", "ada_b", "mix_norm_g", "mlp_norm_g", "mlp_w1", "mlp_w2", "s5_a_re", "s5_a_im", "s5_log_dt", "s5_b_re", "s5_b_im",
             "s5_c_re", "s5_c_im", "s5_d", "s5_w_glu", "kv_ada_w", "kv_ada_b", "kv_norm_g", "w_kv", "k_norm_g", "sb_w_q", "q_norm_g", "sb_w_o"]
    return (loss, grad_x, *[res[n][0] for n in order], *[res[n][1] for n in order], *[res[n][2] for n in order], *[res[n][3] for n in order])
```

```python
import functools
import math

import jax
import jax.numpy as jnp
from jax import lax
from jax.experimental import pallas as pl
from jax.experimental.pallas import tpu as pltpu

F32 = jnp.float32
BF16 = jnp.bfloat16
EPS = 1e-6
HEAD_DIM = 64
S5_GROUP = 16
S5_STATE = 64
GROUPS_PER_STEP = 8
U_LANES = GROUPS_PER_STEP * S5_GROUP
ST_LANES = GROUPS_PER_STEP * S5_STATE
SCAN_LANES = 256
SCAN_UNROLL = 4
VMEM_LIMIT = 56 * 1024 * 1024
ADAM_LR, ADAM_B1, ADAM_B2, ADAM_EPS, ADAM_WD, ADAM_STEP = 0.001, 0.9, 0.999, 1e-08, 0.01, 10
MESH = pl.DeviceIdType.MESH


def _cp(sem):
    return pltpu.CompilerParams(dimension_semantics=sem, vmem_limit_bytes=VMEM_LIMIT)


class _Sharded:
    def __init__(self, buf, kind, roff, nr, c0, nc):
        self.buf, self.kind, self.roff, self.nr, self.c0, self.nc = buf, kind, roff, nr, c0, nc
        self.shape = (nr, 4 * nc) if kind == "cols" else (4 * nr, nc)

    def operand(self, dims, tn, tk):
        roff, nr, c0, nc = self.roff, self.nr, self.c0, self.nc
        if self.kind == "cols" and dims == "nn":
            tk = min(tk, nr)
            assert roff % tk == 0
            return nc, tk, (None, tk, nc), lambda i, j, k: (j, roff // tk + k, c0 // nc)
        if self.kind == "cols":
            tn = min(tn, nr)
            assert roff % tn == 0
            return tn, nc, (None, tn, nc), lambda i, j, k: (k, roff // tn + j, c0 // nc)
        if dims == "nn":
            tn = min(tn, nc)
            assert roff % nr == 0 and c0 % tn == 0
            return tn, nr, (None, nr, tn), lambda i, j, k: (k, roff // nr, c0 // tn + j)
        tk = min(tk, nc)
        assert roff % nr == 0 and c0 % tk == 0
        return nr, tk, (None, nr, tk), lambda i, j, k: (j, roff // nr, c0 // tk + k)

    def result(self, tm, tn):
        roff, nr, c0, nc = self.roff, self.nr, self.c0, self.nc
        if self.kind == "cols":
            tm = min(tm, nr)
            assert roff % tm == 0
            return tm, nc, (None, tm, nc), lambda i, j, k: (j, roff // tm + i, c0 // nc)
        tm, tn = min(tm, nr), min(tn, nc)
        assert roff % tm == 0 and c0 % tn == 0
        per = nr // tm
        return tm, tn, (None, tm, tn), lambda i, j, k: (i // per, roff // tm + i % per, c0 // tn + j)


class _Layer:
    def __init__(self, buf, layer):
        self.buf, self.layer, self.shape = buf, layer, tuple(buf.shape[1:])

    def operand(self, dims, tn, tk):
        assert dims == "nn"
        layer = self.layer
        return tn, tk, (None, tk, tn), lambda i, j, k: (layer, k, j)

    def result(self, tm, tn):
        layer = self.layer
        return tm, tn, (None, tm, tn), lambda i, j, k: (layer, i, j)


def _mm(a, b, dims, *, name, out_dtypes=(F32,), epilogue=None, extras=(), tm=512, tn=1024, tk=1024, into=None):
    bshape = b.shape
    if dims == "nn":
        (M, K), (_, N) = a.shape, bshape
    elif dims == "nt":
        (M, K), (N, _) = a.shape, bshape
    else:
        (K, M), (_, N) = a.shape, bshape
    tm, tn, tk = min(tm, M), min(tn, N), min(tk, K)
    b_arr = b
    if into is not None:
        assert (M, N) == into.shape and len(out_dtypes) == 1 and not isinstance(b, _Sharded)
        tm, tn, o_blk, o_map = into.result(tm, tn)
        out_specs, out_shape = [pl.BlockSpec(o_blk, o_map)], [jax.ShapeDtypeStruct(into.buf.shape, into.buf.dtype)]
    if isinstance(b, (_Sharded, _Layer)):
        tn, tk, b_blk, b_map = b.operand(dims, tn, tk)
        b_spec, b_arr = pl.BlockSpec(b_blk, b_map), b.buf
    else:
        b_spec = pl.BlockSpec((tn, tk), lambda i, j, k: (j, k)) if dims == "nt" else pl.BlockSpec((tk, tn), lambda i, j, k: (k, j))
    if into is None:
        out_specs = [pl.BlockSpec((tm, tn), lambda i, j, k: (i, j)) for _ in out_dtypes]
        out_shape = [jax.ShapeDtypeStruct((M, N), d) for d in out_dtypes]
    assert M % tm == 0 and N % tn == 0 and K % tk == 0, (M, N, K, tm, tn, tk)
    nk = K // tk
    extras = [e(tm, tn) for e in extras]
    a_spec = pl.BlockSpec((tk, tm), lambda i, j, k: (k, i)) if dims == "tn" else pl.BlockSpec((tm, tk), lambda i, j, k: (i, k))
    contract = {"nn": ((1,), (0,)), "nt": ((1,), (1,)), "tn": ((0,), (0,))}[dims]
    n_ex, n_out = len(extras), len(out_dtypes)
    chain = [into.buf] if into is not None and not isinstance(into.buf, jax.ShapeDtypeStruct) else []
    n_in = n_ex + len(chain)

    def finish(r, ex, outs):
        res = epilogue(r, *[e[...] for e in ex]) if epilogue is not None else (r,)
        for o, v in zip(outs, res):
            o[...] = v.astype(o.dtype)

    def product(a_ref, b_ref):
        return lax.dot_general(a_ref[...].astype(BF16), b_ref[...].astype(BF16), (contract, ((), ())), preferred_element_type=F32)

    def body_one(a_ref, b_ref, *rest):
        finish(product(a_ref, b_ref), rest[:n_ex], rest[n_in:])

    def body_acc(a_ref, b_ref, *rest):
        ex, outs, acc = rest[:n_ex], rest[n_in:n_in + n_out], rest[-1]
        k = pl.program_id(2)

        @pl.when(k == 0)
        def _():
            acc[...] = product(a_ref, b_ref)

        @pl.when(jnp.logical_and(k > 0, k < nk - 1))
        def _():
            acc[...] += product(a_ref, b_ref)

        @pl.when(k == nk - 1)
        def _():
            finish(acc[...] + product(a_ref, b_ref), ex, outs)

    out = pl.pallas_call(
        body_one if nk == 1 else body_acc, name=name, grid=(M // tm, N // tn, nk),
        in_specs=[a_spec, b_spec] + [pl.BlockSpec(blk, im) for (_, blk, im) in extras] + [ANY for _ in chain],
        out_specs=out_specs, out_shape=out_shape,
        input_output_aliases={2 + n_ex: 0} if chain else {},
        scratch_shapes=[] if nk == 1 else [pltpu.VMEM((tm, tn), F32)],
        compiler_params=_cp(("parallel", "parallel", "arbitrary")),
    )(a, b_arr, *[e[0] for e in extras], *chain)
    return out if n_out > 1 else out[0]


def _mn_extra(arr):
    return lambda tm, tn: (arr, (tm, tn), lambda i, j, k: (i, j))


def _vec_extra(vec, S):
    return lambda tm, tn: (vec, (None, 1, tn), lambda i, j, k: ((i * tm) // S, 0, j))


def _rowwise(fn, rows, vecs=(), consts=(), out_rows=(), out_sums=(), *, n_ex, name, tr=512):
    rows = [r if len(r) == 4 else (*r, 0) for r in rows]
    S = min(r[0].shape[0] for r in rows if r[3] == 0) // n_ex
    tr = math.gcd(tr, S)
    assert S % tr == 0
    nb = S // tr
    in_specs = []
    for (arr, w, cb, roff) in rows:
        assert roff % tr == 0
        in_specs.append(pl.BlockSpec((tr, w), functools.partial(lambda e, i, cb, ro: (e * nb + i + ro, cb), cb=cb, ro=roff // tr)))
    for v in vecs:
        in_specs.append(pl.BlockSpec((None, 1, v.shape[-1]), lambda e, i: (e, 0, 0)))
    for c in consts:
        in_specs.append(pl.BlockSpec((1, c.shape[-1]), lambda e, i: (0, 0)))
    n_in, n_or, n_os = len(in_specs), len(out_rows), len(out_sums)
    flipped = [len(o) == 3 and o[2] for o in out_rows]
    out_specs = [pl.BlockSpec((o[0], tr), lambda e, i: (0, e * nb + i)) if f else pl.BlockSpec((tr, o[0]), lambda e, i: (e * nb + i, 0))
                 for o, f in zip(out_rows, flipped)]
    out_specs += [pl.BlockSpec((None, 1, w), lambda e, i: (e, 0, 0)) for w in out_sums]
    out_shape = [jax.ShapeDtypeStruct((o[0], n_ex * S) if f else (n_ex * S, o[0]), o[1]) for o, f in zip(out_rows, flipped)]
    out_shape += [jax.ShapeDtypeStruct((n_ex, 1, w), F32) for w in out_sums]

    def body(*refs):
        ins, o_r, o_s = refs[:n_in], refs[n_in:n_in + n_or], refs[n_in + n_or:]
        ro, so = fn(*[r[...] for r in ins])
        for o, v, f in zip(o_r, ro, flipped):
            o[...] = (v.T if f else v).astype(o.dtype)
        i = pl.program_id(1)
        for o, v in zip(o_s, so):
            @pl.when(i == 0)
            def _(o=o, v=v):
                o[...] = v

            @pl.when(i > 0)
            def _(o=o, v=v):
                o[...] += v

    outs = pl.pallas_call(
        body, name=name, grid=(n_ex, nb), in_specs=in_specs, out_specs=out_specs, out_shape=out_shape,
        compiler_params=_cp(("parallel", "arbitrary")),
    )(*[r[0] for r in rows], *vecs, *consts)
    return outs[:n_or], outs[n_or:]


def _csum(x):
    return jnp.sum(x, axis=0, keepdims=True)


def _norm_mod_fwd(x, g, sh, sc, *, n_ex, out_dtype, name, with_transpose=False):
    def fn(xt, sht, sct, gt):
        r = lax.rsqrt(jnp.mean(xt * xt, axis=-1, keepdims=True) + EPS)
        h = (xt * r * gt) * (1.0 + sct) + sht
        return [h, h] if with_transpose else [h], []
    D = x.shape[1]
    outs = [(D, out_dtype), (D, out_dtype, True)] if with_transpose else [(D, out_dtype)]
    res = _rowwise(fn, [(x, D, 0)], [sh, sc], [g], outs, [], n_ex=n_ex, name=name)[0]
    return res if with_transpose else res[0]


def _norm_mod_bwd(x, dh, dres, g, sc, *, n_ex, name):
    def fn(xt, dht, drt, sct, gt):
        dht = dht.astype(F32)
        r = lax.rsqrt(jnp.mean(xt * xt, axis=-1, keepdims=True) + EPS)
        n = xt * r
        y = n * gt
        dy = dht * (1.0 + sct)
        dn = dy * gt
        dx = r * (dn - n * jnp.mean(dn * n, axis=-1, keepdims=True))
        return [drt + dx], [_csum(dht), _csum(dht * y), _csum(dy * n)]
    D = x.shape[1]
    return _rowwise(fn, [(x, D, 0), (dh, D, 0), (dres, D, 0)], [sc], [g], [(D, F32)], [D, D, D], n_ex=n_ex, name=name)


def _sigmoid(x):
    return 1.0 / (1.0 + jnp.exp(-x))


def _gelu(y):
    return 0.5 * y * (1.0 + jnp.tanh(0.7978845608028654 * (y + 0.044715 * y * y * y)))


def _gelu_grad(y):
    t = jnp.tanh(0.7978845608028654 * (y + 0.044715 * y * y * y))
    return 0.5 * (1.0 + t) + 0.5 * y * (1.0 - t * t) * 0.7978845608028654 * (1.0 + 3 * 0.044715 * y * y)


def _adamw_fn(w, g, m, v):
    m2 = ADAM_B1 * m + (1.0 - ADAM_B1) * g
    v2 = ADAM_B2 * v + (1.0 - ADAM_B2) * (g * g)
    m_hat = m2 / (1.0 - ADAM_B1 ** ADAM_STEP)
    v_hat = v2 / (1.0 - ADAM_B2 ** ADAM_STEP)
    delta = -ADAM_LR * (m_hat / (jnp.sqrt(v_hat) + ADAM_EPS) + ADAM_WD * w)
    return delta, m2, v2


def _adamw2d(w, g, m, v, *, name, g_roff=0, g_cb=0):
    R, W = w.shape

    def fn(wt, gt, mt, vt):
        d, m2, v2 = _adamw_fn(wt, gt, mt, vt)
        return [d, m2, v2, gt], []
    return _rowwise(fn, [(w, W, 0), (g, W, g_cb, g_roff), (m, W, 0), (v, W, 0)], [], [],
                    [(W, F32)] * 4, [], n_ex=1, name=name, tr=256)[0]


def _scan_tiles(re_ref, im_ref, cf, lane0, n_chunks, reverse, extra=None):
    L = SCAN_LANES
    lanes = pl.ds(lane0, L)
    A = [cf[i, :, lanes] for i in range(8)]
    shifts = (7, 6, 4) if reverse else (1, 2, 4)
    edge = 0 if reverse else 7

    U = SCAN_UNROLL
    n_groups = n_chunks // U

    def body(c, carry):
        first = ((n_groups - 1 - c) if reverse else c) * U
        rows = pl.ds(pl.multiple_of(first * 8, 8 * U), 8 * U)
        big_r, big_i = re_ref[rows, lanes], im_ref[rows, lanes]
        tiles = []
        for u in range(U):
            xr, xi = big_r[8 * u:8 * u + 8, :], big_i[8 * u:8 * u + 8, :]
            for idx, sft in enumerate(shifts):
                ar, ai = A[2 * idx], A[2 * idx + 1]
                rr, ri = pltpu.roll(xr, sft, 0), pltpu.roll(xi, sft, 0)
                xr, xi = xr + ar * rr - ai * ri, xi + ar * ri + ai * rr
            tiles.append((xr, xi))
        pr, pi = A[6], A[7]
        cr, ci = carry[0], carry[1]
        for u in (range(U - 1, -1, -1) if reverse else range(U)):
            xr, xi = tiles[u]
            xr, xi = xr + pr * cr - pi * ci, xi + pr * ci + pi * cr
            tiles[u] = (xr, xi)
            cr, ci = jnp.broadcast_to(xr[edge:edge + 1, :], (8, L)), jnp.broadcast_to(xi[edge:edge + 1, :], (8, L))
        re_ref[rows, lanes] = jnp.concatenate([t[0] for t in tiles], axis=0)
        im_ref[rows, lanes] = jnp.concatenate([t[1] for t in tiles], axis=0)
        return (cr, ci) if extra is None else (cr, ci) + extra(first, tiles, carry[2:])

    assert n_chunks % U == 0
    z = jnp.zeros((8, L), F32)
    init = (z, z) if extra is None else (z, z, z, z)
    return lax.fori_loop(0, n_groups, body, init)


def _s5_consts(ab_re, ab_im):
    ng = ab_re.shape[0] // GROUPS_PER_STEP
    ar, ai = ab_re.reshape(ng, 1, ST_LANES), ab_im.reshape(ng, 1, ST_LANES)

    def cmul(xr, xi, yr, yi):
        return xr * yr - xi * yi, xr * yi + xi * yr

    def build(ar, ai, reverse):
        pw = [(ar, ai)]
        for _ in range(7):
            pw.append(cmul(*pw[-1], ar, ai))
        row = jnp.arange(8).reshape(1, 8, 1)
        tiles = []
        for k in (1, 2, 4):
            keep = (row <= 7 - k) if reverse else (row >= k)
            tiles += [jnp.where(keep, pw[k - 1][0], 0.0), jnp.where(keep, pw[k - 1][1], 0.0)]
        order = [7 - r for r in range(8)] if reverse else list(range(8))
        tiles += [jnp.concatenate([pw[o][0] for o in order], axis=1), jnp.concatenate([pw[o][1] for o in order], axis=1)]
        return jnp.stack([jnp.broadcast_to(t, (ng, 8, ST_LANES)) for t in tiles], axis=1)

    return build(ar, ai, False), build(ar, -ai, True)


def _s5_blockdiag(bb_re, bb_im, c_re, c_im):
    G = bb_re.shape[0]
    ng = G // GROUPS_PER_STEP
    eye = jnp.eye(GROUPS_PER_STEP, dtype=F32)

    def wb(bb):
        return jnp.einsum("bgph,gk->bghkp", bb.reshape(ng, GROUPS_PER_STEP, S5_STATE, S5_GROUP), eye).reshape(ng, U_LANES, ST_LANES)

    def wc(cc):
        return jnp.einsum("bghp,gk->bkpgh", cc.reshape(ng, GROUPS_PER_STEP, S5_GROUP, S5_STATE), eye).reshape(ng, ST_LANES, U_LANES)

    Wb = jnp.concatenate([wb(bb_re), wb(bb_im)], axis=2).astype(BF16)
    Wc = jnp.concatenate([wc(c_re), -wc(c_im)], axis=1).astype(BF16)
    return Wb, Wc


def _s5_unblock(dWb, dWc):
    ng = dWb.shape[0]
    eye = jnp.eye(GROUPS_PER_STEP, dtype=F32)

    def ub(w):
        return jnp.einsum("bghkp,gk->bgph", w.reshape(ng, GROUPS_PER_STEP, S5_GROUP, GROUPS_PER_STEP, S5_STATE), eye).reshape(-1, S5_STATE, S5_GROUP)

    def uc(w):
        return jnp.einsum("bkpgh,gk->bghp", w.reshape(ng, GROUPS_PER_STEP, S5_STATE, GROUPS_PER_STEP, S5_GROUP), eye).reshape(-1, S5_GROUP, S5_STATE)

    return ub(dWb[:, :, :ST_LANES]), ub(dWb[:, :, ST_LANES:]), uc(dWc[:, :ST_LANES, :]), -uc(dWc[:, ST_LANES:, :])


def _s5_disc(a_re, a_im, log_dt, b_re, b_im):
    dt = jnp.exp(log_dt)[:, None]
    mag = jnp.exp(a_re * dt)
    ab_re = mag * jnp.cos(a_im * dt)
    ab_im = mag * jnp.sin(a_im * dt)
    den = a_re * a_re + a_im * a_im
    nr, ni = ab_re - 1, ab_im
    f_re = (nr * a_re + ni * a_im) / den
    f_im = (ni * a_re - nr * a_im) / den
    bb_re = f_re[..., None] * b_re - f_im[..., None] * b_im
    bb_im = f_re[..., None] * b_im + f_im[..., None] * b_re
    return ab_re, ab_im, bb_re, bb_im


ROW_CHUNK = 512


def _s5_fwd(u, Wb, Wc, cf, d, xsrc, *, n_ex, name):
    T, D = u.shape
    S = T // n_ex
    ng = D // U_LANES
    rc = min(ROW_CHUNK, S)

    def body(u_ref, wb_ref, wc_ref, cf_ref, d_ref, xsrc_ref, y_ref, gy_ref, gyt_ref, st_ref, xout_ref, re_s, im_s, *sems):
        step = pl.program_id(0) * ng + pl.program_id(1)
        exch = _ChipExchange(xsrc_ref, xout_ref, *sems, scatter=False)

        @pl.when(step == 0)
        def _():
            exch.start()

        for r in range(S // rc):
            rows = pl.ds(r * rc, rc)
            bu = jnp.dot(u_ref[rows, :].astype(BF16), wb_ref[...], preferred_element_type=F32)
            re_s[rows, :] = bu[:, :ST_LANES]
            im_s[rows, :] = bu[:, ST_LANES:]
        for l0 in range(0, ST_LANES, SCAN_LANES):
            _scan_tiles(re_s, im_s, cf_ref, l0, S // 8, False)
        for r in range(S // rc):
            rows = pl.ds(r * rc, rc)
            st = jnp.concatenate([re_s[rows, :], im_s[rows, :]], axis=1).astype(BF16)
            st_ref[rows, :] = st
            y = jnp.dot(st, wc_ref[...], preferred_element_type=F32) + d_ref[...] * u_ref[rows, :]
            y_ref[rows, :] = y
            gy = _gelu(y)
            gy_ref[rows, :] = gy.astype(BF16)
            gyt_ref[:, rows] = gy.T.astype(BF16)

        @pl.when(step == n_ex * ng - 1)
        def _():
            exch.wait()

    return pl.pallas_call(
        body, name=name, grid=(n_ex, ng),
        in_specs=[pl.BlockSpec((S, U_LANES), lambda e, g: (e, g)),
                  pl.BlockSpec((None, U_LANES, 2 * ST_LANES), lambda e, g: (g, 0, 0)),
                  pl.BlockSpec((None, 2 * ST_LANES, U_LANES), lambda e, g: (g, 0, 0)),
                  pl.BlockSpec((None, 8, 8, ST_LANES), lambda e, g: (g, 0, 0, 0)),
                  pl.BlockSpec((1, U_LANES), lambda e, g: (0, g)), ANY],
        out_specs=[pl.BlockSpec((S, U_LANES), lambda e, g: (e, g))] * 2 + [pl.BlockSpec((U_LANES, S), lambda e, g: (g, e)),
                   pl.BlockSpec((S, 2 * ST_LANES), lambda e, g: (e, g)), ANY],
        out_shape=[jax.ShapeDtypeStruct((T, D), F32), jax.ShapeDtypeStruct((T, D), BF16), jax.ShapeDtypeStruct((D, T), BF16),
                   jax.ShapeDtypeStruct((T, ng * 2 * ST_LANES), BF16), _ChipExchange.out_shape(xsrc, False)],
        scratch_shapes=[pltpu.VMEM((S, ST_LANES), F32)] * 2 + _ChipExchange.SCRATCH,
        compiler_params=_cp(("arbitrary", "arbitrary")),
    )(u, Wb, Wc, cf, d, xsrc)


def _s5_bwd(u, y, dgy, st, Wb, Wc, cr, d, xsrc, *, n_ex, name):
    T, D = u.shape
    S = T // n_ex
    ng = D // U_LANES
    rc = min(ROW_CHUNK, S)
    nch = S // 8
    grp = 8 * SCAN_UNROLL
    assert grp % 16 == 0

    def body(u_ref, y_ref, dgy_ref, st_ref, wb_ref, wc_ref, cr_ref, d_ref, xsrc_ref,
             du_ref, dwb_ref, dwc_ref, dab_ref, dd_ref, xout_ref, gr_s, gi_s, dy_s, *sems):
        e = pl.program_id(1)
        step = pl.program_id(0) * n_ex + e
        exch = _ChipExchange(xsrc_ref, xout_ref, *sems, scatter=True)

        @pl.when(step == 0)
        def _():
            exch.start()

        @pl.when(e == 0)
        def _():
            dwb_ref[...] = jnp.zeros_like(dwb_ref)
            dwc_ref[...] = jnp.zeros_like(dwc_ref)
            dab_ref[...] = jnp.zeros_like(dab_ref)
            dd_ref[...] = jnp.zeros_like(dd_ref)

        dd = jnp.zeros((1, U_LANES), F32)
        for r in range(S // rc):
            rows = pl.ds(r * rc, rc)
            ut = u_ref[rows, :]
            dy = dgy_ref[rows, :].astype(F32) * _gelu_grad(y_ref[rows, :])
            dy_s[rows, :] = dy
            dd = dd + _csum(dy * ut)
            go = lax.dot_general(dy.astype(BF16), wc_ref[...], (((1,), (1,)), ((), ())), preferred_element_type=F32)
            gr_s[rows, :] = go[:, :ST_LANES]
            gi_s[rows, :] = go[:, ST_LANES:]
        dd_ref[0:1, :] += dd
        row0 = lax.broadcasted_iota(jnp.int32, (8, SCAN_LANES), 0) == 0
        for l0 in range(0, ST_LANES, SCAN_LANES):
            lanes = pl.ds(l0, SCAN_LANES)

            def dab_group(first, tiles, acc, l0=l0):
                def states(r0, n, lane0):
                    return st_ref[pl.ds(pl.multiple_of(r0, 16), n), pl.ds(lane0, SCAN_LANES)].astype(F32)
                r0 = first * 8
                cur = states(r0, grp, l0), states(r0, grp, ST_LANES + l0)
                live = (first > 0).astype(F32)
                p0 = jnp.maximum(r0 - 16, 0)
                before = [states(p0, 16, l0)[8:16, :] * live, states(p0, 16, ST_LANES + l0)[8:16, :] * live]
                a_re, a_im = acc
                for t, (gr, gi) in enumerate(tiles):
                    here = [c[8 * t:8 * t + 8, :] for c in cur]
                    sr, si = [jnp.where(row0, pltpu.roll(b, 1, 0), pltpu.roll(h, 1, 0)) for b, h in zip(before, here)]
                    a_re, a_im = a_re + gr * sr + gi * si, a_im + gi * sr - gr * si
                    before = here
                return a_re, a_im

            res = _scan_tiles(gr_s, gi_s, cr_ref, l0, nch, True, extra=dab_group)
            dab_ref[0:1, lanes] += _csum(res[2])
            dab_ref[1:2, lanes] += _csum(res[3])
        for r in range(S // rc):
            rows = pl.ds(r * rc, rc)
            st = st_ref[rows, :]
            g = jnp.concatenate([gr_s[rows, :], gi_s[rows, :]], axis=1).astype(BF16)
            dyb = dy_s[rows, :].astype(BF16)
            dwc_ref[...] += lax.dot_general(st, dyb, (((0,), (0,)), ((), ())), preferred_element_type=F32)
            dwb_ref[...] += lax.dot_general(u_ref[rows, :].astype(BF16), g, (((0,), (0,)), ((), ())), preferred_element_type=F32)
            du = lax.dot_general(g, wb_ref[...], (((1,), (1,)), ((), ())), preferred_element_type=F32)
            du_ref[rows, :] = du + d_ref[...] * dy_s[rows, :]

        @pl.when(step == ng * n_ex - 1)
        def _():
            exch.wait()

    return pl.pallas_call(
        body, name=name, grid=(ng, n_ex),
        in_specs=[pl.BlockSpec((S, U_LANES), lambda g, e: (e, g))] * 3 + [
            pl.BlockSpec((S, 2 * ST_LANES), lambda g, e: (e, g)),
            pl.BlockSpec((None, U_LANES, 2 * ST_LANES), lambda g, e: (g, 0, 0)),
            pl.BlockSpec((None, 2 * ST_LANES, U_LANES), lambda g, e: (g, 0, 0)),
            pl.BlockSpec((None, 8, 8, ST_LANES), lambda g, e: (g, 0, 0, 0)),
            pl.BlockSpec((1, U_LANES), lambda g, e: (0, g)), ANY],
        out_specs=[pl.BlockSpec((S, U_LANES), lambda g, e: (e, g)),
                   pl.BlockSpec((None, U_LANES, 2 * ST_LANES), lambda g, e: (g, 0, 0)),
                   pl.BlockSpec((None, 2 * ST_LANES, U_LANES), lambda g, e: (g, 0, 0)),
                   pl.BlockSpec((None, 8, ST_LANES), lambda g, e: (g, 0, 0)),
                   pl.BlockSpec((None, 8, U_LANES), lambda g, e: (g, 0, 0)), ANY],
        out_shape=[jax.ShapeDtypeStruct((T, D), F32),
                   jax.ShapeDtypeStruct((ng, U_LANES, 2 * ST_LANES), F32),
                   jax.ShapeDtypeStruct((ng, 2 * ST_LANES, U_LANES), F32),
                   jax.ShapeDtypeStruct((ng, 8, ST_LANES), F32),
                   jax.ShapeDtypeStruct((ng, 8, U_LANES), F32), _ChipExchange.out_shape(xsrc, True)],
        scratch_shapes=[pltpu.VMEM((S, ST_LANES), F32)] * 2 + [pltpu.VMEM((S, U_LANES), F32)] + _ChipExchange.SCRATCH,
        compiler_params=_cp(("arbitrary", "arbitrary")),
    )(u, y, dgy, st, Wb, Wc, cr, d, xsrc)


TQ = 256
KW = 512
SUB = 128


def _head_masks():
    lane = lax.broadcasted_iota(jnp.int32, (1, 2 * HEAD_DIM), 1)
    m0 = (lane < HEAD_DIM).astype(F32)
    return m0, 1.0 - m0


def _head_norm(x, g, m0, m1):
    sq = x * x
    r0 = lax.rsqrt(jnp.sum(sq * m0, axis=-1, keepdims=True) / HEAD_DIM + EPS)
    r1 = lax.rsqrt(jnp.sum(sq * m1, axis=-1, keepdims=True) / HEAD_DIM + EPS)
    r = m0 * r0 + m1 * r1
    return x * r, r


def _head_norm_bwd(dy, n, r, g, m0, m1):
    dn = dy * g
    p = dn * n
    mean = (m0 * jnp.sum(p * m0, axis=-1, keepdims=True) + m1 * jnp.sum(p * m1, axis=-1, keepdims=True)) / HEAD_DIM
    return r * (dn - n * mean), _csum(dy * n)


def _pair_matrix(kind):
    r = lax.broadcasted_iota(jnp.int32, (2 * SUB, 2 * SUB), 0)
    c = lax.broadcasted_iota(jnp.int32, (2 * SUB, 2 * SUB), 1)
    same = (r < SUB) == (c < SUB)
    rel = {"after": r > c, "upto": r <= c, "before": r < c}[kind]
    return jnp.logical_and(same, rel).astype(BF16)


def _block_sums(x, mat, carry, reverse, terms=2):
    hi = x.astype(BF16)
    lo = (x - hi.astype(F32)).astype(BF16) if terms == 2 else None
    npair = x.shape[1] // (2 * SUB)
    parts = [None] * (2 * npair)
    for p in (range(npair - 1, -1, -1) if reverse else range(npair)):
        sl = slice(2 * SUB * p, 2 * SUB * (p + 1))
        loc = jnp.dot(hi[:, sl], mat, preferred_element_type=F32)
        if terms == 2:
            loc = loc + jnp.dot(lo[:, sl], mat, preferred_element_type=F32)
        for b in ((1, 0) if reverse else (0, 1)):
            k = 2 * p + b
            parts[k] = loc[:, SUB * b:SUB * (b + 1)] + carry
            carry = carry + jnp.sum(x[:, SUB * k:SUB * (k + 1)], axis=-1, keepdims=True)
    return jnp.concatenate(parts, axis=1), carry


def _sb_logits(z, mask):
    lp = jnp.minimum(z, 0.0) - jnp.log(1.0 + jnp.exp(-jnp.abs(z)))
    lf = lp - z
    if mask is not None:
        lf = jnp.where(mask, lf, 0.0)
    return lp, lf


def _causal_mask(row0, col0, kw):
    r = row0 + lax.broadcasted_iota(jnp.int32, (TQ, kw), 0)
    c = col0 + lax.broadcasted_iota(jnp.int32, (TQ, kw), 1)
    return c < r


def _transposed_windows(x, ref):
    for w in range(x.shape[0] // KW):
        ref[w] = x[w * KW:(w + 1) * KW, :].T.astype(BF16)


def _attn_fwd(q, kv, qg, kg, xsrc, *, n_ex, name):
    T, D = q.shape
    S = T // n_ex
    nhp = D // (2 * HEAD_DIM)
    nq = S // TQ
    scale = 1.0 / math.sqrt(HEAD_DIM)

    def body(q_ref, k_ref, v_ref, qg_ref, kg_ref, xsrc_ref, o_ref, tot_ref, ot_ref, xout_ref, kT_s, qm_s, vm_s, *sems):
        step = pl.program_id(0) * nhp + pl.program_id(1)
        exch = _ChipExchange(xsrc_ref, xout_ref, *sems, scatter=False)

        @pl.when(step == 0)
        def _():
            exch.start()

        m0, m1 = _head_masks()
        qn, _ = _head_norm(q_ref[...], None, m0, m1)
        qn = qn * (qg_ref[...] * scale)
        kn, _ = _head_norm(k_ref[...], None, m0, m1)
        _transposed_windows(kn * kg_ref[...], kT_s)
        v = v_ref[...]
        for h, m in enumerate((m0, m1)):
            qm_s[h] = (qn * m).astype(BF16)
            vm_s[h] = (v * m).astype(BF16)
        u_after = _pair_matrix("after")

        def window(rows, win, st, mask, kw):
            keys = pl.ds(pl.multiple_of(win * KW, KW), kw)
            zs = [jnp.dot(qm_s[h, rows, :], kT_s[win, :, :kw], preferred_element_type=F32) for h in range(2)]
            lg = [_sb_logits(zs[h], mask) for h in range(2)]
            sums = [_block_sums(lg[h][1], u_after, st[2 * h], True) for h in range(2)]
            out = ()
            for h in range(2):
                w = jnp.exp(lg[h][0] + sums[h][0])
                if mask is not None:
                    w = jnp.where(mask, w, 0.0)
                out += (sums[h][1], st[2 * h + 1] + jnp.dot(w.astype(BF16), vm_s[h, keys, :], preferred_element_type=F32))
            return out

        def qtile(iq, last, kw):
            rows = pl.ds(pl.multiple_of(iq * TQ, TQ), TQ)
            mask = _causal_mask(iq * TQ, last * KW, kw)
            z1, zq = jnp.zeros((TQ, 1), F32), jnp.zeros((TQ, 2 * HEAD_DIM), F32)
            st = window(rows, last, (z1, zq, z1, zq), mask, kw)
            st = lax.fori_loop(0, last, lambda jj, st: window(rows, last - 1 - jj, st, None, KW), st)
            o_ref[rows, :] = st[1] + st[3]
            tot_ref[rows, :] = st[0] * m0 + st[2] * m1

        def qtiles_of_window(a, _):
            for sub in range(KW // TQ):
                qtile(a * (KW // TQ) + sub, a, (sub + 1) * TQ)
            return 0

        lax.fori_loop(0, S // KW, qtiles_of_window, 0)
        ot_ref[...] = o_ref[...].T.astype(BF16)

        @pl.when(step == n_ex * nhp - 1)
        def _():
            exch.wait()

    assert S % KW == 0 and KW % TQ == 0
    nwin = S // KW
    blk = (S, 2 * HEAD_DIM)
    return pl.pallas_call(
        body, name=name, grid=(n_ex, nhp),
        in_specs=[pl.BlockSpec(blk, lambda e, h: (e, h)), pl.BlockSpec(blk, lambda e, h: (e, h)),
                  pl.BlockSpec(blk, lambda e, h: (e, h + nhp)),
                  pl.BlockSpec((1, 2 * HEAD_DIM), lambda e, h: (0, 0)), pl.BlockSpec((1, 2 * HEAD_DIM), lambda e, h: (0, 0)), ANY],
        out_specs=[pl.BlockSpec(blk, lambda e, h: (e, h))] * 2 + [pl.BlockSpec((2 * HEAD_DIM, S), lambda e, h: (h, e)), ANY],
        out_shape=[jax.ShapeDtypeStruct((T, D), F32)] * 2 + [jax.ShapeDtypeStruct((D, T), BF16), _ChipExchange.out_shape(xsrc, False)],
        scratch_shapes=[pltpu.VMEM((nwin, 2 * HEAD_DIM, KW), BF16), pltpu.VMEM((2,) + blk, BF16), pltpu.VMEM((2,) + blk, BF16)]
        + _ChipExchange.SCRATCH,
        compiler_params=_cp(("arbitrary", "arbitrary")),
    )(q, kv, kv, qg, kg, xsrc)


def _attn_bwd(q, kv, tot, do, qg, kg, *, n_ex, name):
    T, D = q.shape
    S = T // n_ex
    nhp = D // (2 * HEAD_DIM)
    nq = S // TQ
    scale = 1.0 / math.sqrt(HEAD_DIM)

    def body(q_ref, k_ref, v_ref, tot_ref, do_ref, qg_ref, kg_ref, dq_ref, dkv_ref, dqg_ref, dkg_ref,
             kT_s, vT_s, km_s, qm_s, dom_s, dqn_s, dkT_s, dvT_s):
        @pl.when(pl.program_id(2) == 0)
        def _():
            work(q_ref, k_ref, v_ref, tot_ref, do_ref, qg_ref, kg_ref, dq_ref, dkv_ref, dqg_ref, dkg_ref,
                 kT_s, vT_s, km_s, qm_s, dom_s, dqn_s, dkT_s, dvT_s)

        @pl.when(pl.program_id(2) == 1)
        def _():
            dkv_ref[...] = jnp.concatenate([dvT_s[w].T for w in range(nwin)], axis=0)

    def work(q_ref, k_ref, v_ref, tot_ref, do_ref, qg_ref, kg_ref, dq_ref, dk_ref, dqg_ref, dkg_ref,
             kT_s, vT_s, km_s, qm_s, dom_s, dqn_s, dkT_s, dvT_s):
        m0, m1 = _head_masks()
        qn, qr = _head_norm(q_ref[...], None, m0, m1)
        kn, kr = _head_norm(k_ref[...], None, m0, m1)
        qs = qn * (qg_ref[...] * scale)
        kk = kn * kg_ref[...]
        _transposed_windows(kk, kT_s)
        _transposed_windows(v_ref[...], vT_s)
        do = do_ref[...]
        for h, m in enumerate((m0, m1)):
            qm_s[h] = (qs * m).astype(BF16)
            km_s[h] = (kk * m).astype(BF16)
            dom_s[h] = (do * m).astype(BF16)
        dkT_s[...] = jnp.zeros_like(dkT_s)
        dvT_s[...] = jnp.zeros_like(dvT_s)
        u_upto, u_before = _pair_matrix("upto"), _pair_matrix("before")

        def both(inv, win, st, mask, kw):
            keys = pl.ds(pl.multiple_of(win * KW, KW), kw)
            lg = [_sb_logits(jnp.dot(inv[h][0], kT_s[win, :, :kw], preferred_element_type=F32), mask) for h in range(2)]
            s_lf = [_block_sums(lg[h][1], u_upto, st[3 * h], False) for h in range(2)]
            ws, ews = [], []
            for h in range(2):
                w = jnp.exp(lg[h][0] - s_lf[h][0])
                if mask is not None:
                    w = jnp.where(mask, w, 0.0)
                ws.append(w)
                ews.append(jnp.dot(inv[h][2], vT_s[win, :, :kw], preferred_element_type=F32) * w)
            s_e = [_block_sums(ews[h], u_before, st[3 * h + 1], False, terms=1) for h in range(2)]
            out, dk, dv = (), None, None
            for h in range(2):
                sig = jnp.exp(lg[h][0])
                dz = ews[h] - sig * (ews[h] + s_e[h][0])
                if mask is not None:
                    dz = jnp.where(mask, dz, 0.0)
                dzb = dz.astype(BF16)
                out += (s_lf[h][1], s_e[h][1], st[3 * h + 2] + jnp.dot(dzb, km_s[h, keys, :], preferred_element_type=F32))
                dkh = jnp.dot(inv[h][1], dzb, preferred_element_type=F32)
                dvh = jnp.dot(inv[h][3], ws[h].astype(BF16), preferred_element_type=F32)
                dk, dv = (dkh, dvh) if h == 0 else (dk + dkh, dv + dvh)
            dkT_s[win, :, :kw] += dk
            dvT_s[win, :, :kw] += dv
            return out

        def qtile(iq, last, kw):
            rows = pl.ds(pl.multiple_of(iq * TQ, TQ), TQ)
            mask = _causal_mask(iq * TQ, last * KW, kw)
            tt = tot_ref[rows, :]
            inv, neg_total = [], []
            for h, m in enumerate((m0, m1)):
                qh, doh = qm_s[h, rows, :], dom_s[h, rows, :]
                neg_total.append(jnp.sum(tt * m, axis=-1, keepdims=True) * (-1.0 / HEAD_DIM))
                inv.append((qh, qh.astype(F32).T.astype(BF16), doh, doh.astype(F32).T.astype(BF16)))

            z1, zq = jnp.zeros((TQ, 1), F32), jnp.zeros((TQ, 2 * HEAD_DIM), F32)
            st = lax.fori_loop(0, last, lambda win, st: both(inv, win, st, None, KW), (neg_total[0], z1, zq, neg_total[1], z1, zq))
            st = both(inv, last, st, mask, kw)
            dqn_s[rows, :] = st[2] + st[5]

        def qtiles_of_window(a, _):
            for sub in range(KW // TQ):
                qtile(a * (KW // TQ) + sub, a, (sub + 1) * TQ)
            return 0

        lax.fori_loop(0, S // KW, qtiles_of_window, 0)
        dkn = jnp.concatenate([dkT_s[w].T for w in range(nwin)], axis=0)
        dq, dqg = _head_norm_bwd(dqn_s[...] * scale, qn, qr, qg_ref[...], m0, m1)
        dk, dkg = _head_norm_bwd(dkn, kn, kr, kg_ref[...], m0, m1)
        dq_ref[...] = dq
        dk_ref[...] = dk
        dqg_ref[...] = dqg
        dkg_ref[...] = dkg

    assert S % KW == 0 and KW % TQ == 0
    nwin = S // KW
    blk = (S, 2 * HEAD_DIM)
    tblk = (nwin, 2 * HEAD_DIM, KW)
    gblk = (None, None, 1, 2 * HEAD_DIM)
    return pl.pallas_call(
        body, name=name, grid=(n_ex, nhp, 2),
        in_specs=[pl.BlockSpec(blk, lambda e, h, t: (e, h)), pl.BlockSpec(blk, lambda e, h, t: (e, h)),
                  pl.BlockSpec(blk, lambda e, h, t: (e, h + nhp)),
                  pl.BlockSpec(blk, lambda e, h, t: (e, h)), pl.BlockSpec(blk, lambda e, h, t: (e, h)),
                  pl.BlockSpec((1, 2 * HEAD_DIM), lambda e, h, t: (0, 0)), pl.BlockSpec((1, 2 * HEAD_DIM), lambda e, h, t: (0, 0))],
        out_specs=[pl.BlockSpec(blk, lambda e, h, t: (e, h)), pl.BlockSpec(blk, lambda e, h, t: (e, h + nhp * t))]
        + [pl.BlockSpec(gblk, lambda e, h, t: (e, h, 0, 0))] * 2,
        out_shape=[jax.ShapeDtypeStruct((T, D), F32), jax.ShapeDtypeStruct((T, 2 * D), F32)]
        + [jax.ShapeDtypeStruct((n_ex, nhp, 1, 2 * HEAD_DIM), F32)] * 2,
        scratch_shapes=[pltpu.VMEM(tblk, BF16), pltpu.VMEM(tblk, BF16),
                        pltpu.VMEM((2,) + blk, BF16), pltpu.VMEM((2,) + blk, BF16), pltpu.VMEM((2,) + blk, BF16),
                        pltpu.VMEM(blk, F32), pltpu.VMEM(tblk, F32), pltpu.VMEM(tblk, F32)],
        compiler_params=_cp(("parallel", "parallel", "arbitrary")),
    )(q, kv, kv, tot, do, qg, kg)


def _place():
    return lax.axis_index("x"), lax.axis_index("y"), lax.axis_index("c")


def _all_gather8(x_shard, *, name):
    m_per, n = x_shard.shape

    def body(x_ref, out_ref, send_sems, recv_sems, local_sem):
        x, y, c = _place()
        me, sibling = (x, y, c), (x, y, 1 - c)
        chips = [(1 - x, y), (x, 1 - y), (1 - x, 1 - y)]

        def rows(px, py, pc):
            return out_ref.at[pl.ds((4 * px + 2 * py + pc) * m_per, m_per), :]

        def copy(k, block, to, src=None):
            return pltpu.make_async_remote_copy(
                src_ref=rows(*block) if src is None else src, dst_ref=rows(*block),
                send_sem=send_sems.at[k], recv_sem=recv_sems.at[k], device_id=to, device_id_type=MESH)

        mine = pltpu.make_async_copy(x_ref, rows(*me), local_sem)
        mine.start()
        first = [copy(0, me, sibling, src=x_ref)]
        first += [copy(1 + j, me, (*chip, c), src=x_ref) for j, chip in enumerate(chips)]
        for cp in first:
            cp.start()
        passed = [copy(4 + j, (*chip, c), sibling) for j, chip in enumerate(chips)]
        for j, chip in enumerate(chips):
            copy(1 + j, (*chip, c), me).wait_recv()
            passed[j].start()
        copy(0, sibling, me).wait_recv()
        for j, chip in enumerate(chips):
            copy(4 + j, (*chip, 1 - c), me).wait_recv()
        for cp in first + passed:
            cp.wait_send()
        mine.wait()

    return pl.pallas_call(
        body, name=name, out_shape=jax.ShapeDtypeStruct((8 * m_per, n), x_shard.dtype),
        in_specs=[pl.BlockSpec(memory_space=pltpu.VMEM)], out_specs=pl.BlockSpec(memory_space=pltpu.VMEM),
        scratch_shapes=[pltpu.SemaphoreType.DMA((7,)), pltpu.SemaphoreType.DMA((7,)), pltpu.SemaphoreType.DMA],
        compiler_params=pltpu.CompilerParams(vmem_limit_bytes=VMEM_LIMIT),
    )(x_shard)


def _sibling_sum_half(x, *, name):
    R, C = x.shape
    half = R // 2

    def body(x_ref, o_ref, theirs, send_sem, recv_sem):
        px, py, pc = _place()
        cp = pltpu.make_async_remote_copy(src_ref=x_ref, dst_ref=theirs, send_sem=send_sem, recv_sem=recv_sem,
                                          device_id=(px, py, 1 - pc), device_id_type=MESH)
        cp.start()
        cp.wait()
        rows = pl.ds(pl.multiple_of(pc * half, 8), half)
        o_ref[...] = x_ref[rows, :] + theirs[rows, :]

    return pl.pallas_call(
        body, name=name, out_shape=jax.ShapeDtypeStruct((half, C), x.dtype),
        in_specs=[pl.BlockSpec(memory_space=pltpu.VMEM)], out_specs=pl.BlockSpec(memory_space=pltpu.VMEM),
        scratch_shapes=[pltpu.VMEM((R, C), x.dtype), pltpu.SemaphoreType.DMA, pltpu.SemaphoreType.DMA],
        compiler_params=pltpu.CompilerParams(vmem_limit_bytes=VMEM_LIMIT),
    )(x)


def _sum_blocks(x, n, *, name):
    R = x.shape[0] // n

    def body(x_ref, o_ref):
        acc = x_ref[pl.ds(0, R), :]
        for k in range(1, n):
            acc = acc + x_ref[pl.ds(k * R, R), :]
        o_ref[...] = acc

    return pl.pallas_call(body, name=name, out_shape=jax.ShapeDtypeStruct((R, x.shape[1]), x.dtype),
                          compiler_params=pltpu.CompilerParams(vmem_limit_bytes=VMEM_LIMIT))(x)


def _colsum(x, *, name):
    def body(x_ref, o_ref):
        o_ref[...] = jnp.sum(x_ref[...], axis=0, keepdims=True)
    return pl.pallas_call(body, name=name, out_shape=jax.ShapeDtypeStruct((1, x.shape[1]), x.dtype))(x)


ANY = pl.BlockSpec(memory_space=pl.ANY)


class _ChipExchange:
    SCRATCH = [pltpu.SemaphoreType.DMA((3,)), pltpu.SemaphoreType.DMA((3,)), pltpu.SemaphoreType.DMA]

    @staticmethod
    def out_shape(src, scatter):
        return jax.ShapeDtypeStruct(((4,) + tuple(src.shape[1:])) if scatter else ((4, 2) + tuple(src.shape[1:])), src.dtype)

    def __init__(self, src_ref, out_ref, send_sems, recv_sems, local_sem, scatter):
        x, y, c = _place()
        myj = 2 * x + y
        chips = [(1 - x, y), (x, 1 - y), (1 - x, 1 - y)]

        def slot(j):
            return out_ref.at[j] if scatter else out_ref.at[j, c]

        def piece(j):
            return src_ref.at[j] if scatter else src_ref.at[c]

        self.mine = pltpu.make_async_copy(piece(myj), slot(myj), local_sem)
        self.sends = [pltpu.make_async_remote_copy(
            src_ref=piece(2 * cx + cy), dst_ref=slot(myj), send_sem=send_sems.at[k], recv_sem=recv_sems.at[k],
            device_id=(cx, cy, c), device_id_type=MESH) for k, (cx, cy) in enumerate(chips)]
        self.recvs = [pltpu.make_async_remote_copy(
            src_ref=slot(2 * cx + cy), dst_ref=slot(2 * cx + cy), send_sem=send_sems.at[k], recv_sem=recv_sems.at[k],
            device_id=(cx, cy, c), device_id_type=MESH) for k, (cx, cy) in enumerate(chips)]

    def start(self):
        self.mine.start()
        for cp in self.sends:
            cp.start()

    def wait(self):
        for cp in self.recvs:
            cp.wait_recv()
        for cp in self.sends:
            cp.wait_send()
        self.mine.wait()


def _sibling_fill(buf, *, axis, name):
    def half(ref, h):
        return ref.at[h] if axis == 0 else ref.at[:, h]

    def body(in_ref, out_ref, send_sem, recv_sem):
        x, y, c = _place()
        cp = pltpu.make_async_remote_copy(src_ref=half(out_ref, c), dst_ref=half(out_ref, c), send_sem=send_sem, recv_sem=recv_sem,
                                          device_id=(x, y, 1 - c), device_id_type=MESH)
        cp.start()
        pltpu.make_async_remote_copy(src_ref=half(out_ref, 1 - c), dst_ref=half(out_ref, 1 - c), send_sem=send_sem, recv_sem=recv_sem,
                                     device_id=(x, y, 1 - c), device_id_type=MESH).wait_recv()
        cp.wait_send()

    return pl.pallas_call(
        body, name=name, out_shape=jax.ShapeDtypeStruct(buf.shape, buf.dtype), in_specs=[ANY], out_specs=ANY,
        input_output_aliases={0: 0}, scratch_shapes=[pltpu.SemaphoreType.DMA, pltpu.SemaphoreType.DMA],
    )(buf)


def _sibling_swap_half(g, *, name):
    def body(g_ref, out_ref, send_sem, recv_sem):
        x, y, c = _place()
        cp = pltpu.make_async_remote_copy(src_ref=g_ref.at[:, 1 - c], dst_ref=out_ref, send_sem=send_sem, recv_sem=recv_sem,
                                          device_id=(x, y, 1 - c), device_id_type=MESH)
        cp.start()
        cp.wait()

    return pl.pallas_call(
        body, name=name, out_shape=jax.ShapeDtypeStruct((g.shape[0],) + g.shape[2:], g.dtype), in_specs=[ANY], out_specs=ANY,
        scratch_shapes=[pltpu.SemaphoreType.DMA, pltpu.SemaphoreType.DMA],
    )(g)


def _add_my_half(g, b, cidx, *, name, tr=256):
    n, _, R, C = g.shape
    tr = math.gcd(tr, R)

    def body(c_ref, g_ref, b_ref, o_ref):
        o_ref[...] = (g_ref[...] + b_ref[...]).astype(o_ref.dtype)

    return pl.pallas_call(
        body, name=name, out_shape=jax.ShapeDtypeStruct((n, R, C), BF16),
        grid_spec=pltpu.PrefetchScalarGridSpec(
            num_scalar_prefetch=1, grid=(n, R // tr),
            in_specs=[pl.BlockSpec((None, None, tr, C), lambda j, i, c: (j, c[0], i, 0)),
                      pl.BlockSpec((None, tr, C), lambda j, i, c: (j, i, 0))],
            out_specs=pl.BlockSpec((None, tr, C), lambda j, i, c: (j, i, 0))),
        compiler_params=_cp(("parallel", "parallel")),
    )(cidx, g, b)


def _sum4_into_half(q, cidx, *, name, tr=256):
    _, R, C = q.shape
    tr = math.gcd(tr, R)

    def body(c_ref, q_ref, o_ref):
        o_ref[...] = ((q_ref[0].astype(F32) + q_ref[1].astype(F32)) + q_ref[2].astype(F32)) + q_ref[3].astype(F32)

    return pl.pallas_call(
        body, name=name, out_shape=jax.ShapeDtypeStruct((2, R, C), F32),
        grid_spec=pltpu.PrefetchScalarGridSpec(
            num_scalar_prefetch=1, grid=(R // tr,),
            in_specs=[pl.BlockSpec((4, tr, C), lambda i, c: (0, i, 0))],
            out_specs=pl.BlockSpec((None, tr, C), lambda i, c: (c[0], i, 0))),
        compiler_params=_cp(("parallel",)),
    )(cidx, q)


def _pack_rows(parts, width=1024):
    rows, spans, r0 = [], [], 0
    for p in parts:
        n = p.size
        nr = 8 * (-(-n // (8 * width)))
        flat = p.reshape(-1)
        if nr * width != n:
            flat = jnp.pad(flat, (0, nr * width - n))
        rows.append(flat.reshape(nr, width))
        spans.append((r0, nr, n, p.shape))
        r0 += nr
    return jnp.concatenate(rows, axis=0), spans


def _unpack_rows(buf, spans):
    return [buf[r0:r0 + nr].reshape(-1)[:n].reshape(shape) for (r0, nr, n, shape) in spans]


def kernel(x, c, ada_w, ada_b, mix_norm_g, mlp_norm_g, mlp_w1, mlp_w2, s5_a_re, s5_a_im, s5_log_dt, s5_b_re, s5_b_im, s5_c_re, s5_c_im, s5_d, s5_w_glu, kv_ada_w, kv_ada_b, kv_norm_g, w_kv, k_norm_g, sb_w_q, q_norm_g, sb_w_o, loss_target, m_ada_w, m_ada_b, m_mix_norm_g, m_mlp_norm_g, m_mlp_w1, m_mlp_w2, m_s5_a_re, m_s5_a_im, m_s5_log_dt, m_s5_b_re, m_s5_b_im, m_s5_c_re, m_s5_c_im, m_s5_d, m_s5_w_glu, m_kv_ada_w, m_kv_ada_b, m_kv_norm_g, m_w_kv, m_k_norm_g, m_sb_w_q, m_q_norm_g, m_sb_w_o, v_ada_w, v_ada_b, v_mix_norm_g, v_mlp_norm_g, v_mlp_w1, v_mlp_w2, v_s5_a_re, v_s5_a_im, v_s5_log_dt, v_s5_b_re, v_s5_b_im, v_s5_c_re, v_s5_c_im, v_s5_d, v_s5_w_glu, v_kv_ada_w, v_kv_ada_b, v_kv_norm_g, v_w_kv, v_k_norm_g, v_sb_w_q, v_q_norm_g, v_sb_w_o):
    E, S, D = x.shape
    T = E * S
    FF = 4 * D
    NB = 8 * E
    px, py, pc = _place()
    chip = 2 * px + py
    dev = 4 * px + 2 * py + pc
    cidx = jnp.reshape(pc, (1,)).astype(jnp.int32)
    x0 = x.reshape(T, D)
    tgt = loss_target.reshape(T, D)

    nc_rows, nd = c.size // 128, s5_d.size // 128
    cd = jnp.concatenate([c.reshape(nc_rows, 128), jnp.pad(s5_d.reshape(nd, 128), ((0, 8 - nd), (0, 0)))], axis=0)
    cd_all = _all_gather8(cd, name="ag_c_d").reshape(8, nc_rows + 8, 128)
    c_all = cd_all[:, :nc_rows].reshape(NB, D)
    d_full = cd_all.reshape(4, 2, nc_rows + 8, 128)[:, 0, nc_rows:nc_rows + nd].reshape(1, D)
    sc_all = (c_all * _sigmoid(c_all)).astype(BF16)
    wa = ada_w.shape[2]
    wk = kv_ada_w.shape[1]
    m_sh = jnp.concatenate([_mm(sc_all, _Layer(ada_w, 0), "nn", name="ada0", tn=256),
                            _mm(sc_all, _Layer(ada_w, 1), "nn", name="ada1", tn=256),
                            _mm(sc_all, kv_ada_w, "nn", name="ada_kv", tn=256)], axis=1)
    m_all = _all_gather8(m_sh, name="ag_m").reshape(4, 2, NB, 2 * wa + wk)[:, 0]
    mods = []
    for l in range(2):
        full = jnp.transpose(m_all[:, :, l * wa:(l + 1) * wa], (1, 0, 2)).reshape(NB, 6 * D) + ada_b[l]
        mine = lax.dynamic_slice_in_dim(full, E * dev, E, axis=0)
        mods.append([mine[:, i * D:(i + 1) * D].reshape(E, 1, D) for i in range(6)])
    full = jnp.transpose(m_all[:, :, 2 * wa:], (1, 0, 2)).reshape(NB, 2 * D) + kv_ada_b
    mine = lax.dynamic_slice_in_dim(full, E * dev, E, axis=0)
    kv_sh, kv_sc = [mine[:, i * D:(i + 1) * D].reshape(E, 1, D) for i in range(2)]

    wpack_a = jnp.concatenate([mlp_w1[0], mlp_w2[0], jnp.concatenate([s5_w_glu[0], w_kv], axis=1), sb_w_q[0]], axis=0).astype(BF16)
    wpack_b = jnp.concatenate([mlp_w1[1], mlp_w2[1], sb_w_o[0]], axis=0).astype(BF16)
    RA, RB = wpack_a.shape[0], wpack_b.shape[0]
    RW = RA + RB

    tm = min(2048, S)
    tm_res = min(1024, S)
    gbuf = [jax.ShapeDtypeStruct((4, RW, D), F32)]

    def grad_mm(act, dout, kind, roff, nr, c0, nc, name, transposed=False):
        gbuf[0] = _mm(act, dout, "nn" if transposed else "tn", name=name, tm=1024, tk=2048,
                      into=_Sharded(gbuf[0], kind, roff, nr, c0, nc))

    def mlp_fwd(xa, l, mod):
        sh_m, sc_m, g_m = mod[3], mod[4], mod[5]
        h, h_t = _norm_mod_fwd(xa, mlp_norm_g[l:l + 1], sh_m, sc_m, n_ex=E, out_dtype=BF16, name=f"mlp_norm{l}", with_transpose=True)

        def relu_sq(acc):
            ra = jnp.maximum(acc, 0.0)
            return ra * ra, ra
        r, ra = _mm(h, W1[l], "nn", name=f"mlp_up{l}", out_dtypes=(BF16, BF16), tm=tm, epilogue=relu_sq)
        xb, ff = _mm(r, W2[l], "nn", name=f"mlp_down{l}", out_dtypes=(F32, F32), tm=tm_res,
                     extras=[_mn_extra(xa), _vec_extra(g_m, S)],
                     epilogue=lambda acc, xat, gt: (xat + gt * acc, acc))
        return xb, (h_t, r, ra, ff)

    def mlp_bwd(dxb, xa, l, mod, saved):
        sc_m, g_m = mod[4], mod[5]
        h_t, r, ra, ff = saved
        (dff,), (dgm,) = _rowwise(lambda d, f, g: ([g * d], [_csum(d * f)]), [(dxb, D, 0), (ff, D, 0)], [g_m], [],
                                  [(D, BF16)], [D], n_ex=E, name=f"mlp_gate_bwd{l}")
        da = _mm(dff, W2[l], "nt", name=f"mlp_down_dx{l}", out_dtypes=(BF16,), tm=tm, extras=[_mn_extra(ra)],
                 epilogue=lambda acc, rat: (acc * (2.0 * rat.astype(F32)),))
        grad_mm(r, dff, "rows", (2 + l) * D, D, 0, D, f"mlp_down_dw{l}")
        dh = _mm(da, W1[l], "nt", name=f"mlp_up_dx{l}", tm=tm)
        grad_mm(h_t, da, "cols", l * D, D, 0, D, f"mlp_up_dw{l}", transposed=True)
        (dxa,), (dsh, dsc, dg) = _norm_mod_bwd(xa, dh, dxb, mlp_norm_g[l:l + 1], sc_m, n_ex=E, name=f"mlp_norm_bwd{l}")
        return dxa, (dsh, dsc, dgm), dg

    ab_re, ab_im, bb_re, bb_im = _s5_disc(s5_a_re[0], s5_a_im[0], s5_log_dt[0], s5_b_re[0], s5_b_im[0])
    cf, cr = _s5_consts(ab_re, ab_im)
    Wb, Wc = _s5_blockdiag(bb_re, bb_im, s5_c_re[0], s5_c_im[0])
    ng = D // U_LANES

    mod0, mod1 = mods
    h0 = _norm_mod_fwd(x0, mix_norm_g[0:1], mod0[0], mod0[1], n_ex=E, out_dtype=F32, name="mix_norm0")
    y, gy, gy_t, s5_states, wfull_a = _s5_fwd(h0, Wb, Wc, cf, d_full, wpack_a.reshape(2, RA // 2, D), n_ex=E, name="s5_fwd")
    wfull_a = _sibling_fill(wfull_a, axis=1, name="wgather_a_d2d").reshape(4, RA, D)

    W1 = [_Sharded(wfull_a, "cols", 0, D, 0, D), None]
    W2 = [_Sharded(wfull_a, "rows", D, D, 0, D), None]
    Wglu = _Sharded(wfull_a, "cols", 2 * D, D, 0, D // 2)
    Wkv = _Sharded(wfull_a, "cols", 2 * D, D, D // 2, D // 2)
    Wq = _Sharded(wfull_a, "rows", 3 * D, D // 4, 0, D)
    vg = _mm(gy, Wglu, "nn", name="glu_up", tm=tm)
    (x1,), _ = _rowwise(lambda v, g, xt, ga: ([xt + ga * (v * _sigmoid(g))], []),
                        [(vg, D, 0), (vg, D, 1), (x0, D, 0)], [mod0[2]], [], [(D, F32)], [], n_ex=E, name="glu_gate")
    x2, saved_mlp0 = mlp_fwd(x1, 0, mod0)

    hkv, hkv_t = _norm_mod_fwd(x2, kv_norm_g.reshape(1, D), kv_sh, kv_sc, n_ex=E, out_dtype=BF16, name="kv_norm", with_transpose=True)
    kvf = _mm(hkv, Wkv, "nn", name="kv_proj", tm=tm)
    h1, h1_t = _norm_mod_fwd(x2, mix_norm_g[1:2], mod1[0], mod1[1], n_ex=E, out_dtype=BF16, name="mix_norm1", with_transpose=True)
    qf = _mm(h1, Wq, "nn", name="q_proj", tm=tm)
    qg2 = jnp.tile(q_norm_g.reshape(1, HEAD_DIM), (1, 2))
    kg2 = jnp.tile(k_norm_g.reshape(1, HEAD_DIM), (1, 2))
    o, lf_tot, o_t, wfull_b = _attn_fwd(qf, kvf, qg2, kg2, wpack_b.reshape(2, RB // 2, D), n_ex=E, name="attn_fwd")
    wfull_b = _sibling_fill(wfull_b, axis=1, name="wgather_b_d2d").reshape(4, RB, D)
    W1[1] = _Sharded(wfull_b, "cols", 0, D, 0, D)
    W2[1] = _Sharded(wfull_b, "rows", D, D, 0, D)
    Wo = _Sharded(wfull_b, "rows", 2 * D, D // 4, 0, D)
    x3, mix1 = _mm(o, Wo, "nn", name="o_proj", out_dtypes=(F32, F32), tm=tm_res,
                   extras=[_mn_extra(x2), _vec_extra(mod1[2], S)],
                   epilogue=lambda acc, xat, gt: (xat + gt * acc, acc))
    x4, saved_mlp1 = mlp_fwd(x3, 1, mod1)

    (dx4,), (lsum,) = _rowwise(lambda xt, tt: ([(xt - tt) * (1.0 / D)], [_csum(jnp.square(xt - tt)) * (0.5 / D)]),
                               [(x4, D, 0), (tgt, D, 0)], [], [], [(D, F32)], [D], n_ex=E, name="loss")
    loss = lax.psum(jnp.sum(lsum), ("x", "y", "c"))

    dx3, (dsh_m1, dsc_m1, dgm1), dg_mlp1 = mlp_bwd(dx4, x3, 1, mod1, saved_mlp1)
    (dmix1,), (dga1,) = _rowwise(lambda d, f, g: ([g * d], [_csum(d * f)]), [(dx3, D, 0), (mix1, D, 0)], [mod1[2]], [],
                                 [(D, BF16)], [D], n_ex=E, name="attn_gate_bwd")
    do = _mm(dmix1, Wo, "nt", name="o_proj_dx", tm=tm)
    grad_mm(o_t, dmix1, "rows", 5 * D + D // 4, D // 4, 0, D, "o_proj_dw", transposed=True)
    dq, dkv, dqg, dkg = _attn_bwd(qf, kvf, lf_tot, do, qg2, kg2, n_ex=E, name="attn_bwd")
    dh1 = _mm(dq, Wq, "nt", name="q_proj_dx", tm=tm)
    grad_mm(h1_t, dq, "rows", 5 * D, D // 4, 0, D, "q_proj_dw", transposed=True)
    (dx2,), (dsh_a1, dsc_a1, dg_mix1) = _norm_mod_bwd(x2, dh1, dx3, mix_norm_g[1:2], mod1[1], n_ex=E, name="mix_norm_bwd1")
    dhkv = _mm(dkv, Wkv, "nt", name="kv_proj_dx", tm=tm)
    grad_mm(hkv_t, dkv, "cols", 4 * D, D, D // 2, D // 2, "kv_proj_dw", transposed=True)
    (dx2,), (dkv_sh, dkv_sc, dg_kv) = _norm_mod_bwd(x2, dhkv, dx2, kv_norm_g.reshape(1, D), kv_sc, n_ex=E, name="kv_norm_bwd")

    dx1, (dsh_m0, dsc_m0, dgm0), dg_mlp0 = mlp_bwd(dx2, x1, 0, mod0, saved_mlp0)

    def glu_bwd(v, g, d, ga):
        sg = _sigmoid(g)
        dm = ga * d
        return [jnp.concatenate([dm * sg, dm * v * sg * (1.0 - sg)], axis=1)], [_csum(d * (v * sg))]
    (dvg,), (dga0,) = _rowwise(glu_bwd, [(vg, D, 0), (vg, D, 1), (dx1, D, 0)], [mod0[2]], [], [(2 * D, BF16)], [D],
                               n_ex=E, name="glu_gate_bwd")
    dgy = _mm(dvg, Wglu, "nt", name="glu_up_dx", tm=tm)
    grad_mm(gy_t, dvg, "cols", 4 * D, D, 0, D // 2, "glu_up_dw", transposed=True)

    gpack = gbuf[0].reshape(4, 2, RW // 2, D)
    theirs = _sibling_swap_half(gpack, name="gscatter_d2d")
    chip_sum = _add_my_half(gpack, theirs, cidx, name="gscatter_add")
    dh0, dWb, dWc, dab, dd, from_chips = _s5_bwd(h0, y, dgy, s5_states, Wb, Wc, cr, d_full, chip_sum, n_ex=E, name="s5_bwd")
    ghalf = _sum4_into_half(from_chips, cidx, name="gscatter_sum")
    gsh = _sibling_fill(ghalf, axis=0, name="gscatter_fill").reshape(RW, D)
    (gx,), (dsh_a0, dsc_a0, dg_mix0) = _norm_mod_bwd(x0, dh0, dx1, mix_norm_g[0:1], mod0[1], n_ex=E, name="mix_norm_bwd0")
    grad_x = gx.reshape(E, S, D)

    dm_mine = jnp.concatenate([t.reshape(E, D) for t in
                               (dsh_a0, dsc_a0, dga0, dsh_m0, dsc_m0, dgm0, dsh_a1, dsc_a1, dga1, dsh_m1, dsc_m1, dgm1, dkv_sh, dkv_sc)], axis=1)
    dm_all = _all_gather8(dm_mine.reshape(8, -1), name="ag_dm").reshape(NB, 14 * D)
    sc_f32 = c_all * _sigmoid(c_all)
    g_ada_w = jax.ShapeDtypeStruct(ada_w.shape, F32)
    for l in range(2):
        g_ada_w = _mm(sc_f32, lax.dynamic_slice_in_dim(dm_all, l * 6 * D + chip * wa, wa, axis=1), "tn", name=f"ada_dw{l}", tn=256,
                      into=_Layer(g_ada_w, l))
    g_kv_ada_w = _mm(sc_f32, lax.dynamic_slice_in_dim(dm_all, 12 * D + chip * wk, wk, axis=1), "tn", name="ada_kv_dw", tn=256)
    db_all = _colsum(dm_all, name="ada_db")
    g_ada_b = db_all[0, :12 * D].reshape(2, 6 * D)
    g_kv_ada_b = db_all[0, 12 * D:]

    dWb_re, dWb_im, dC_re, dC_im = _s5_unblock(dWb, dWc)
    small_parts = [dg_mix0.sum(0), dg_mix1.sum(0), dg_mlp0.sum(0), dg_mlp1.sum(0), dg_kv.sum(0),
                   dqg.sum((0, 1, 2)).reshape(2, HEAD_DIM).sum(0), dkg.sum((0, 1, 2)).reshape(2, HEAD_DIM).sum(0),
                   dd[:, 0, :], dab[:, 0, :], dab[:, 1, :], dWb_re, dWb_im, dC_re, dC_im]
    spack, spans = _pack_rows(small_parts)
    chip_half = _sibling_sum_half(spack, name="small_d2d")
    ssum = _sum_blocks(_all_gather8(chip_half, name="ag_small"), 4, name="sum_small")
    (g_mix0, g_mix1, g_mlp0, g_mlp1, g_kvn, g_qn, g_kn, g_d, g_abr, g_abi, g_bbr, g_bbi, g_cre, g_cim) = _unpack_rows(ssum, spans)
    _, disc_vjp = jax.vjp(_s5_disc, s5_a_re[0], s5_a_im[0], s5_log_dt[0], s5_b_re[0], s5_b_im[0])
    g_are, g_aim, g_ldt, g_bre, g_bim = disc_vjp((g_abr.reshape(ab_re.shape), g_abi.reshape(ab_im.shape), g_bbr, g_bbi))
    g_s5d = lax.dynamic_slice_in_dim(g_d.reshape(1, D), chip * s5_d.shape[1], s5_d.shape[1], axis=1)

    def upd_big(w, m, v, roff, cb, name):
        shape = w.shape
        W = shape[-1]
        d_, m_, v_, g_ = _adamw2d(w.reshape(-1, W), gsh, m.reshape(-1, W), v.reshape(-1, W), name=name, g_roff=roff, g_cb=cb)
        return [t.reshape(shape) for t in (g_, d_, m_, v_)]

    def upd_own(w, g, m, v, name):
        shape = w.shape
        W = shape[-1]
        d_, m_, v_, g_ = _adamw2d(w.reshape(-1, W), g.reshape(-1, W), m.reshape(-1, W), v.reshape(-1, W), name=name)
        return [t.reshape(shape) for t in (g_, d_, m_, v_)]

    res = {}
    res["ada_w"] = upd_own(ada_w, g_ada_w, m_ada_w, v_ada_w, "adam_ada_w")
    res["kv_ada_w"] = upd_own(kv_ada_w, g_kv_ada_w, m_kv_ada_w, v_kv_ada_w, "adam_kv_ada_w")
    res["mlp_w1"] = upd_big(mlp_w1, m_mlp_w1, v_mlp_w1, 0, 0, "adam_w1")
    res["mlp_w2"] = upd_big(mlp_w2, m_mlp_w2, v_mlp_w2, 2 * D, 0, "adam_w2")
    res["s5_w_glu"] = upd_big(s5_w_glu, m_s5_w_glu, v_s5_w_glu, 4 * D, 0, "adam_glu")
    res["w_kv"] = upd_big(w_kv, m_w_kv, v_w_kv, 4 * D, 1, "adam_wkv")
    res["sb_w_q"] = upd_big(sb_w_q, m_sb_w_q, v_sb_w_q, 5 * D, 0, "adam_wq")
    res["sb_w_o"] = upd_big(sb_w_o, m_sb_w_o, v_sb_w_o, 5 * D + D // 4, 0, "adam_wo")

    small = {
        "ada_b": (ada_b, g_ada_b, m_ada_b, v_ada_b),
        "mix_norm_g": (mix_norm_g, jnp.stack([g_mix0, g_mix1]), m_mix_norm_g, v_mix_norm_g),
        "mlp_norm_g": (mlp_norm_g, jnp.stack([g_mlp0, g_mlp1]), m_mlp_norm_g, v_mlp_norm_g),
        "s5_a_re": (s5_a_re, g_are[None], m_s5_a_re, v_s5_a_re),
        "s5_a_im": (s5_a_im, g_aim[None], m_s5_a_im, v_s5_a_im),
        "s5_log_dt": (s5_log_dt, g_ldt[None], m_s5_log_dt, v_s5_log_dt),
        "s5_b_re": (s5_b_re, g_bre[None], m_s5_b_re, v_s5_b_re),
        "s5_b_im": (s5_b_im, g_bim[None], m_s5_b_im, v_s5_b_im),
        "s5_c_re": (s5_c_re, g_cre[None], m_s5_c_re, v_s5_c_re),
        "s5_c_im": (s5_c_im, g_cim[None], m_s5_c_im, v_s5_c_im),
        "s5_d": (s5_d, g_s5d, m_s5_d, v_s5_d),
        "kv_ada_b": (kv_ada_b, g_kv_ada_b, m_kv_ada_b, v_kv_ada_b),
        "kv_norm_g": (kv_norm_g, g_kvn, m_kv_norm_g, v_kv_norm_g),
        "k_norm_g": (k_norm_g, g_kn, m_k_norm_g, v_k_norm_g),
        "q_norm_g": (q_norm_g, g_qn.reshape(q_norm_g.shape), m_q_norm_g, v_q_norm_g),
    }
    names = list(small)
    packs = [_pack_rows([small[n][i].reshape(small[n][0].shape) for n in names]) for i in range(4)]
    sp = packs[0][1]
    d_, m_, v_, g_ = _adamw2d(packs[0][0], packs[1][0], packs[2][0], packs[3][0], name="adam_small")
    for n, gg, dd_, mm_, vv_ in zip(names, _unpack_rows(g_, sp), _unpack_rows(d_, sp), _unpack_rows(m_, sp), _unpack_rows(v_, sp)):
        res[n] = [gg, dd_, mm_, vv_]

    order = ["ada_w", "ada_b", "mix_norm_g", "mlp_norm_g", "mlp_w1", "mlp_w2", "s5_a_re", "s5_a_im", "s5_log_dt", "s5_b_re", "s5_b_im",
             "s5_c_re", "s5_c_im", "s5_d", "s5_w_glu", "kv_ada_w", "kv_ada_b", "kv_norm_g", "w_kv", "k_norm_g", "sb_w_q", "q_norm_g", "sb_w_o"]
    return (loss, grad_x, *[res[n][0] for n in order], *[res[n][1] for n in order], *[res[n][2] for n in order], *[res[n][3] for n in order])
```

```python
import functools
import math

import jax
import jax.numpy as jnp
from jax import lax
from jax.experimental import pallas as pl
from jax.experimental.pallas import tpu as pltpu

F32 = jnp.float32
BF16 = jnp.bfloat16
EPS = 1e-6
HEAD_DIM = 64
S5_GROUP = 16
S5_STATE = 64
GROUPS_PER_STEP = 8
U_LANES = GROUPS_PER_STEP * S5_GROUP
ST_LANES = GROUPS_PER_STEP * S5_STATE
SCAN_LANES = 256
SCAN_UNROLL = 4
VMEM_LIMIT = 56 * 1024 * 1024
ADAM_LR, ADAM_B1, ADAM_B2, ADAM_EPS, ADAM_WD, ADAM_STEP = 0.001, 0.9, 0.999, 1e-08, 0.01, 10
MESH = pl.DeviceIdType.MESH


def _cp(sem):
    return pltpu.CompilerParams(dimension_semantics=sem, vmem_limit_bytes=VMEM_LIMIT)


class _Sharded:
    def __init__(self, buf, kind, roff, nr, c0, nc):
        self.buf, self.kind, self.roff, self.nr, self.c0, self.nc = buf, kind, roff, nr, c0, nc
        self.shape = (nr, 4 * nc) if kind == "cols" else (4 * nr, nc)

    def operand(self, dims, tn, tk):
        roff, nr, c0, nc = self.roff, self.nr, self.c0, self.nc
        if self.kind == "cols" and dims == "nn":
            tk = min(tk, nr)
            assert roff % tk == 0
            return nc, tk, (None, tk, nc), lambda i, j, k: (j, roff // tk + k, c0 // nc)
        if self.kind == "cols":
            tn = min(tn, nr)
            assert roff % tn == 0
            return tn, nc, (None, tn, nc), lambda i, j, k: (k, roff // tn + j, c0 // nc)
        if dims == "nn":
            tn = min(tn, nc)
            assert roff % nr == 0 and c0 % tn == 0
            return tn, nr, (None, nr, tn), lambda i, j, k: (k, roff // nr, c0 // tn + j)
        tk = min(tk, nc)
        assert roff % nr == 0 and c0 % tk == 0
        return nr, tk, (None, nr, tk), lambda i, j, k: (j, roff // nr, c0 // tk + k)

    def result(self, tm, tn):
        roff, nr, c0, nc = self.roff, self.nr, self.c0, self.nc
        if self.kind == "cols":
            tm = min(tm, nr)
            assert roff % tm == 0
            return tm, nc, (None, tm, nc), lambda i, j, k: (j, roff // tm + i, c0 // nc)
        tm, tn = min(tm, nr), min(tn, nc)
        assert roff % tm == 0 and c0 % tn == 0
        per = nr // tm
        return tm, tn, (None, tm, tn), lambda i, j, k: (i // per, roff // tm + i % per, c0 // tn + j)


class _Layer:
    def __init__(self, buf, layer):
        self.buf, self.layer, self.shape = buf, layer, tuple(buf.shape[1:])

    def operand(self, dims, tn, tk):
        assert dims == "nn"
        layer = self.layer
        return tn, tk, (None, tk, tn), lambda i, j, k: (layer, k, j)

    def result(self, tm, tn):
        layer = self.layer
        return tm, tn, (None, tm, tn), lambda i, j, k: (layer, i, j)


def _mm(a, b, dims, *, name, out_dtypes=(F32,), epilogue=None, extras=(), tm=512, tn=1024, tk=1024, into=None):
    bshape = b.shape
    if dims == "nn":
        (M, K), (_, N) = a.shape, bshape
    elif dims == "nt":
        (M, K), (N, _) = a.shape, bshape
    else:
        (K, M), (_, N) = a.shape, bshape
    tm, tn, tk = min(tm, M), min(tn, N), min(tk, K)
    b_arr = b
    if into is not None:
        assert (M, N) == into.shape and len(out_dtypes) == 1 and not isinstance(b, _Sharded)
        tm, tn, o_blk, o_map = into.result(tm, tn)
        out_specs, out_shape = [pl.BlockSpec(o_blk, o_map)], [jax.ShapeDtypeStruct(into.buf.shape, into.buf.dtype)]
    if isinstance(b, (_Sharded, _Layer)):
        tn, tk, b_blk, b_map = b.operand(dims, tn, tk)
        b_spec, b_arr = pl.BlockSpec(b_blk, b_map), b.buf
    else:
        b_spec = pl.BlockSpec((tn, tk), lambda i, j, k: (j, k)) if dims == "nt" else pl.BlockSpec((tk, tn), lambda i, j, k: (k, j))
    if into is None:
        out_specs = [pl.BlockSpec((tm, tn), lambda i, j, k: (i, j)) for _ in out_dtypes]
        out_shape = [jax.ShapeDtypeStruct((M, N), d) for d in out_dtypes]
    assert M % tm == 0 and N % tn == 0 and K % tk == 0, (M, N, K, tm, tn, tk)
    nk = K // tk
    extras = [e(tm, tn) for e in extras]
    a_spec = pl.BlockSpec((tk, tm), lambda i, j, k: (k, i)) if dims == "tn" else pl.BlockSpec((tm, tk), lambda i, j, k: (i, k))
    contract = {"nn": ((1,), (0,)), "nt": ((1,), (1,)), "tn": ((0,), (0,))}[dims]
    n_ex, n_out = len(extras), len(out_dtypes)
    chain = [into.buf] if into is not None and not isinstance(into.buf, jax.ShapeDtypeStruct) else []
    n_in = n_ex + len(chain)

    def finish(r, ex, outs):
        res = epilogue(r, *[e[...] for e in ex]) if epilogue is not None else (r,)
        for o, v in zip(outs, res):
            o[...] = v.astype(o.dtype)

    def product(a_ref, b_ref):
        return lax.dot_general(a_ref[...].astype(BF16), b_ref[...].astype(BF16), (contract, ((), ())), preferred_element_type=F32)

    def body_one(a_ref, b_ref, *rest):
        finish(product(a_ref, b_ref), rest[:n_ex], rest[n_in:])

    def body_acc(a_ref, b_ref, *rest):
        ex, outs, acc = rest[:n_ex], rest[n_in:n_in + n_out], rest[-1]
        k = pl.program_id(2)

        @pl.when(k == 0)
        def _():
            acc[...] = product(a_ref, b_ref)

        @pl.when(jnp.logical_and(k > 0, k < nk - 1))
        def _():
            acc[...] += product(a_ref, b_ref)

        @pl.when(k == nk - 1)
        def _():
            finish(acc[...] + product(a_ref, b_ref), ex, outs)

    out = pl.pallas_call(
        body_one if nk == 1 else body_acc, name=name, grid=(M // tm, N // tn, nk),
        in_specs=[a_spec, b_spec] + [pl.BlockSpec(blk, im) for (_, blk, im) in extras] + [ANY for _ in chain],
        out_specs=out_specs, out_shape=out_shape,
        input_output_aliases={2 + n_ex: 0} if chain else {},
        scratch_shapes=[] if nk == 1 else [pltpu.VMEM((tm, tn), F32)],
        compiler_params=_cp(("parallel", "parallel", "arbitrary")),
    )(a, b_arr, *[e[0] for e in extras], *chain)
    return out if n_out > 1 else out[0]


def _mn_extra(arr):
    return lambda tm, tn: (arr, (tm, tn), lambda i, j, k: (i, j))


def _vec_extra(vec, S):
    return lambda tm, tn: (vec, (None, 1, tn), lambda i, j, k: ((i * tm) // S, 0, j))


def _rowwise(fn, rows, vecs=(), consts=(), out_rows=(), out_sums=(), *, n_ex, name, tr=512):
    rows = [r if len(r) == 4 else (*r, 0) for r in rows]
    S = min(r[0].shape[0] for r in rows if r[3] == 0) // n_ex
    tr = math.gcd(tr, S)
    assert S % tr == 0
    nb = S // tr
    in_specs = []
    for (arr, w, cb, roff) in rows:
        assert roff % tr == 0
        in_specs.append(pl.BlockSpec((tr, w), functools.partial(lambda e, i, cb, ro: (e * nb + i + ro, cb), cb=cb, ro=roff // tr)))
    for v in vecs:
        in_specs.append(pl.BlockSpec((None, 1, v.shape[-1]), lambda e, i: (e, 0, 0)))
    for c in consts:
        in_specs.append(pl.BlockSpec((1, c.shape[-1]), lambda e, i: (0, 0)))
    n_in, n_or, n_os = len(in_specs), len(out_rows), len(out_sums)
    flipped = [len(o) == 3 and o[2] for o in out_rows]
    out_specs = [pl.BlockSpec((o[0], tr), lambda e, i: (0, e * nb + i)) if f else pl.BlockSpec((tr, o[0]), lambda e, i: (e * nb + i, 0))
                 for o, f in zip(out_rows, flipped)]
    out_specs += [pl.BlockSpec((None, 1, w), lambda e, i: (e, 0, 0)) for w in out_sums]
    out_shape = [jax.ShapeDtypeStruct((o[0], n_ex * S) if f else (n_ex * S, o[0]), o[1]) for o, f in zip(out_rows, flipped)]
    out_shape += [jax.ShapeDtypeStruct((n_ex, 1, w), F32) for w in out_sums]

    def body(*refs):
        ins, o_r, o_s = refs[:n_in], refs[n_in:n_in + n_or], refs[n_in + n_or:]
        ro, so = fn(*[r[...] for r in ins])
        for o, v, f in zip(o_r, ro, flipped):
            o[...] = (v.T if f else v).astype(o.dtype)
        i = pl.program_id(1)
        for o, v in zip(o_s, so):
            @pl.when(i == 0)
            def _(o=o, v=v):
                o[...] = v

            @pl.when(i > 0)
            def _(o=o, v=v):
                o[...] += v

    outs = pl.pallas_call(
        body, name=name, grid=(n_ex, nb), in_specs=in_specs, out_specs=out_specs, out_shape=out_shape,
        compiler_params=_cp(("parallel", "arbitrary")),
    )(*[r[0] for r in rows], *vecs, *consts)
    return outs[:n_or], outs[n_or:]


def _csum(x):
    return jnp.sum(x, axis=0, keepdims=True)


def _norm_mod_fwd(x, g, sh, sc, *, n_ex, out_dtype, name, with_transpose=False):
    def fn(xt, sht, sct, gt):
        r = lax.rsqrt(jnp.mean(xt * xt, axis=-1, keepdims=True) + EPS)
        h = (xt * r * gt) * (1.0 + sct) + sht
        return [h, h] if with_transpose else [h], []
    D = x.shape[1]
    outs = [(D, out_dtype), (D, out_dtype, True)] if with_transpose else [(D, out_dtype)]
    res = _rowwise(fn, [(x, D, 0)], [sh, sc], [g], outs, [], n_ex=n_ex, name=name)[0]
    return res if with_transpose else res[0]


def _norm_mod_bwd(x, dh, dres, g, sc, *, n_ex, name):
    def fn(xt, dht, drt, sct, gt):
        dht = dht.astype(F32)
        r = lax.rsqrt(jnp.mean(xt * xt, axis=-1, keepdims=True) + EPS)
        n = xt * r
        y = n * gt
        dy = dht * (1.0 + sct)
        dn = dy * gt
        dx = r * (dn - n * jnp.mean(dn * n, axis=-1, keepdims=True))
        return [drt + dx], [_csum(dht), _csum(dht * y), _csum(dy * n)]
    D = x.shape[1]
    return _rowwise(fn, [(x, D, 0), (dh, D, 0), (dres, D, 0)], [sc], [g], [(D, F32)], [D, D, D], n_ex=n_ex, name=name)


def _sigmoid(x):
    return 1.0 / (1.0 + jnp.exp(-x))


def _gelu(y):
    return 0.5 * y * (1.0 + jnp.tanh(0.7978845608028654 * (y + 0.044715 * y * y * y)))


def _gelu_grad(y):
    t = jnp.tanh(0.7978845608028654 * (y + 0.044715 * y * y * y))
    return 0.5 * (1.0 + t) + 0.5 * y * (1.0 - t * t) * 0.7978845608028654 * (1.0 + 3 * 0.044715 * y * y)


def _adamw_fn(w, g, m, v):
    m2 = ADAM_B1 * m + (1.0 - ADAM_B1) * g
    v2 = ADAM_B2 * v + (1.0 - ADAM_B2) * (g * g)
    m_hat = m2 / (1.0 - ADAM_B1 ** ADAM_STEP)
    v_hat = v2 / (1.0 - ADAM_B2 ** ADAM_STEP)
    delta = -ADAM_LR * (m_hat / (jnp.sqrt(v_hat) + ADAM_EPS) + ADAM_WD * w)
    return delta, m2, v2


def _adamw2d(w, g, m, v, *, name, g_roff=0, g_cb=0):
    R, W = w.shape

    def fn(wt, gt, mt, vt):
        d, m2, v2 = _adamw_fn(wt, gt, mt, vt)
        return [d, m2, v2, gt], []
    return _rowwise(fn, [(w, W, 0), (g, W, g_cb, g_roff), (m, W, 0), (v, W, 0)], [], [],
                    [(W, F32)] * 4, [], n_ex=1, name=name, tr=256)[0]


def _scan_tiles(re_ref, im_ref, cf, lane0, n_chunks, reverse, extra=None):
    L = SCAN_LANES
    lanes = pl.ds(lane0, L)
    A = [cf[i, :, lanes] for i in range(8)]
    shifts = (7, 6, 4) if reverse else (1, 2, 4)
    edge = 0 if reverse else 7

    U = SCAN_UNROLL
    n_groups = n_chunks // U

    def body(c, carry):
        first = ((n_groups - 1 - c) if reverse else c) * U
        rows = pl.ds(pl.multiple_of(first * 8, 8 * U), 8 * U)
        big_r, big_i = re_ref[rows, lanes], im_ref[rows, lanes]
        tiles = []
        for u in range(U):
            xr, xi = big_r[8 * u:8 * u + 8, :], big_i[8 * u:8 * u + 8, :]
            for idx, sft in enumerate(shifts):
                ar, ai = A[2 * idx], A[2 * idx + 1]
                rr, ri = pltpu.roll(xr, sft, 0), pltpu.roll(xi, sft, 0)
                xr, xi = xr + ar * rr - ai * ri, xi + ar * ri + ai * rr
            tiles.append((xr, xi))
        pr, pi = A[6], A[7]
        cr, ci = carry[0], carry[1]
        for u in (range(U - 1, -1, -1) if reverse else range(U)):
            xr, xi = tiles[u]
            xr, xi = xr + pr * cr - pi * ci, xi + pr * ci + pi * cr
            tiles[u] = (xr, xi)
            cr, ci = jnp.broadcast_to(xr[edge:edge + 1, :], (8, L)), jnp.broadcast_to(xi[edge:edge + 1, :], (8, L))
        re_ref[rows, lanes] = jnp.concatenate([t[0] for t in tiles], axis=0)
        im_ref[rows, lanes] = jnp.concatenate([t[1] for t in tiles], axis=0)
        return (cr, ci) if extra is None else (cr, ci) + extra(first, tiles, carry[2:])

    assert n_chunks % U == 0
    z = jnp.zeros((8, L), F32)
    init = (z, z) if extra is None else (z, z, z, z)
    return lax.fori_loop(0, n_groups, body, init)


def _s5_consts(ab_re, ab_im):
    ng = ab_re.shape[0] // GROUPS_PER_STEP
    ar, ai = ab_re.reshape(ng, 1, ST_LANES), ab_im.reshape(ng, 1, ST_LANES)

    def cmul(xr, xi, yr, yi):
        return xr * yr - xi * yi, xr * yi + xi * yr

    def build(ar, ai, reverse):
        pw = [(ar, ai)]
        for _ in range(7):
            pw.append(cmul(*pw[-1], ar, ai))
        row = jnp.arange(8).reshape(1, 8, 1)
        tiles = []
        for k in (1, 2, 4):
            keep = (row <= 7 - k) if reverse else (row >= k)
            tiles += [jnp.where(keep, pw[k - 1][0], 0.0), jnp.where(keep, pw[k - 1][1], 0.0)]
        order = [7 - r for r in range(8)] if reverse else list(range(8))
        tiles += [jnp.concatenate([pw[o][0] for o in order], axis=1), jnp.concatenate([pw[o][1] for o in order], axis=1)]
        return jnp.stack([jnp.broadcast_to(t, (ng, 8, ST_LANES)) for t in tiles], axis=1)

    return build(ar, ai, False), build(ar, -ai, True)


def _s5_blockdiag(bb_re, bb_im, c_re, c_im):
    G = bb_re.shape[0]
    ng = G // GROUPS_PER_STEP
    eye = jnp.eye(GROUPS_PER_STEP, dtype=F32)

    def wb(bb):
        return jnp.einsum("bgph,gk->bghkp", bb.reshape(ng, GROUPS_PER_STEP, S5_STATE, S5_GROUP), eye).reshape(ng, U_LANES, ST_LANES)

    def wc(cc):
        return jnp.einsum("bghp,gk->bkpgh", cc.reshape(ng, GROUPS_PER_STEP, S5_GROUP, S5_STATE), eye).reshape(ng, ST_LANES, U_LANES)

    Wb = jnp.concatenate([wb(bb_re), wb(bb_im)], axis=2).astype(BF16)
    Wc = jnp.concatenate([wc(c_re), -wc(c_im)], axis=1).astype(BF16)
    return Wb, Wc


def _s5_unblock(dWb, dWc):
    ng = dWb.shape[0]
    eye = jnp.eye(GROUPS_PER_STEP, dtype=F32)

    def ub(w):
        return jnp.einsum("bghkp,gk->bgph", w.reshape(ng, GROUPS_PER_STEP, S5_GROUP, GROUPS_PER_STEP, S5_STATE), eye).reshape(-1, S5_STATE, S5_GROUP)

    def uc(w):
        return jnp.einsum("bkpgh,gk->bghp", w.reshape(ng, GROUPS_PER_STEP, S5_STATE, GROUPS_PER_STEP, S5_GROUP), eye).reshape(-1, S5_GROUP, S5_STATE)

    return ub(dWb[:, :, :ST_LANES]), ub(dWb[:, :, ST_LANES:]), uc(dWc[:, :ST_LANES, :]), -uc(dWc[:, ST_LANES:, :])


def _s5_disc(a_re, a_im, log_dt, b_re, b_im):
    dt = jnp.exp(log_dt)[:, None]
    mag = jnp.exp(a_re * dt)
    ab_re = mag * jnp.cos(a_im * dt)
    ab_im = mag * jnp.sin(a_im * dt)
    den = a_re * a_re + a_im * a_im
    nr, ni = ab_re - 1, ab_im
    f_re = (nr * a_re + ni * a_im) / den
    f_im = (ni * a_re - nr * a_im) / den
    bb_re = f_re[..., None] * b_re - f_im[..., None] * b_im
    bb_im = f_re[..., None] * b_im + f_im[..., None] * b_re
    return ab_re, ab_im, bb_re, bb_im


ROW_CHUNK = 512


def _s5_fwd(u, Wb, Wc, cf, d, xsrc, *, n_ex, name):
    T, D = u.shape
    S = T // n_ex
    ng = D // U_LANES
    rc = min(ROW_CHUNK, S)

    def body(u_ref, wb_ref, wc_ref, cf_ref, d_ref, xsrc_ref, y_ref, gy_ref, gyt_ref, st_ref, xout_ref, re_s, im_s, *sems):
        step = pl.program_id(0) * ng + pl.program_id(1)
        exch = _ChipExchange(xsrc_ref, xout_ref, *sems, scatter=False)

        @pl.when(step == 0)
        def _():
            exch.start()

        for r in range(S // rc):
            rows = pl.ds(r * rc, rc)
            bu = jnp.dot(u_ref[rows, :].astype(BF16), wb_ref[...], preferred_element_type=F32)
            re_s[rows, :] = bu[:, :ST_LANES]
            im_s[rows, :] = bu[:, ST_LANES:]
        for l0 in range(0, ST_LANES, SCAN_LANES):
            _scan_tiles(re_s, im_s, cf_ref, l0, S // 8, False)
        for r in range(S // rc):
            rows = pl.ds(r * rc, rc)
            st = jnp.concatenate([re_s[rows, :], im_s[rows, :]], axis=1).astype(BF16)
            st_ref[rows, :] = st
            y = jnp.dot(st, wc_ref[...], preferred_element_type=F32) + d_ref[...] * u_ref[rows, :]
            y_ref[rows, :] = y
            gy = _gelu(y)
            gy_ref[rows, :] = gy.astype(BF16)
            gyt_ref[:, rows] = gy.T.astype(BF16)

        @pl.when(step == n_ex * ng - 1)
        def _():
            exch.wait()

    return pl.pallas_call(
        body, name=name, grid=(n_ex, ng),
        in_specs=[pl.BlockSpec((S, U_LANES), lambda e, g: (e, g)),
                  pl.BlockSpec((None, U_LANES, 2 * ST_LANES), lambda e, g: (g, 0, 0)),
                  pl.BlockSpec((None, 2 * ST_LANES, U_LANES), lambda e, g: (g, 0, 0)),
                  pl.BlockSpec((None, 8, 8, ST_LANES), lambda e, g: (g, 0, 0, 0)),
                  pl.BlockSpec((1, U_LANES), lambda e, g: (0, g)), ANY],
        out_specs=[pl.BlockSpec((S, U_LANES), lambda e, g: (e, g))] * 2 + [pl.BlockSpec((U_LANES, S), lambda e, g: (g, e)),
                   pl.BlockSpec((S, 2 * ST_LANES), lambda e, g: (e, g)), ANY],
        out_shape=[jax.ShapeDtypeStruct((T, D), F32), jax.ShapeDtypeStruct((T, D), BF16), jax.ShapeDtypeStruct((D, T), BF16),
                   jax.ShapeDtypeStruct((T, ng * 2 * ST_LANES), BF16), _ChipExchange.out_shape(xsrc, False)],
        scratch_shapes=[pltpu.VMEM((S, ST_LANES), F32)] * 2 + _ChipExchange.SCRATCH,
        compiler_params=_cp(("arbitrary", "arbitrary")),
    )(u, Wb, Wc, cf, d, xsrc)


def _s5_bwd(u, y, dgy, st, Wb, Wc, cr, d, xsrc, *, n_ex, name):
    T, D = u.shape
    S = T // n_ex
    ng = D // U_LANES
    rc = min(ROW_CHUNK, S)
    nch = S // 8
    grp = 8 * SCAN_UNROLL
    assert grp % 16 == 0

    def body(u_ref, y_ref, dgy_ref, st_ref, wb_ref, wc_ref, cr_ref, d_ref, xsrc_ref,
             du_ref, dwb_ref, dwc_ref, dab_ref, dd_ref, xout_ref, gr_s, gi_s, dy_s, *sems):
        e = pl.program_id(1)
        step = pl.program_id(0) * n_ex + e
        exch = _ChipExchange(xsrc_ref, xout_ref, *sems, scatter=True)

        @pl.when(step == 0)
        def _():
            exch.start()

        @pl.when(e == 0)
        def _():
            dwb_ref[...] = jnp.zeros_like(dwb_ref)
            dwc_ref[...] = jnp.zeros_like(dwc_ref)
            dab_ref[...] = jnp.zeros_like(dab_ref)
            dd_ref[...] = jnp.zeros_like(dd_ref)

        dd = jnp.zeros((1, U_LANES), F32)
        for r in range(S // rc):
            rows = pl.ds(r * rc, rc)
            ut = u_ref[rows, :]
            dy = dgy_ref[rows, :].astype(F32) * _gelu_grad(y_ref[rows, :])
            dy_s[rows, :] = dy
            dd = dd + _csum(dy * ut)
            go = lax.dot_general(dy.astype(BF16), wc_ref[...], (((1,), (1,)), ((), ())), preferred_element_type=F32)
            gr_s[rows, :] = go[:, :ST_LANES]
            gi_s[rows, :] = go[:, ST_LANES:]
        dd_ref[0:1, :] += dd
        row0 = lax.broadcasted_iota(jnp.int32, (8, SCAN_LANES), 0) == 0
        for l0 in range(0, ST_LANES, SCAN_LANES):
            lanes = pl.ds(l0, SCAN_LANES)

            def dab_group(first, tiles, acc, l0=l0):
                def states(r0, n, lane0):
                    return st_ref[pl.ds(pl.multiple_of(r0, 16), n), pl.ds(lane0, SCAN_LANES)].astype(F32)
                r0 = first * 8
                cur = states(r0, grp, l0), states(r0, grp, ST_LANES + l0)
                live = (first > 0).astype(F32)
                p0 = jnp.maximum(r0 - 16, 0)
                before = [states(p0, 16, l0)[8:16, :] * live, states(p0, 16, ST_LANES + l0)[8:16, :] * live]
                a_re, a_im = acc
                for t, (gr, gi) in enumerate(tiles):
                    here = [c[8 * t:8 * t + 8, :] for c in cur]
                    sr, si = [jnp.where(row0, pltpu.roll(b, 1, 0), pltpu.roll(h, 1, 0)) for b, h in zip(before, here)]
                    a_re, a_im = a_re + gr * sr + gi * si, a_im + gi * sr - gr * si
                    before = here
                return a_re, a_im

            res = _scan_tiles(gr_s, gi_s, cr_ref, l0, nch, True, extra=dab_group)
            dab_ref[0:1, lanes] += _csum(res[2])
            dab_ref[1:2, lanes] += _csum(res[3])
        for r in range(S // rc):
            rows = pl.ds(r * rc, rc)
            st = st_ref[rows, :]
            g = jnp.concatenate([gr_s[rows, :], gi_s[rows, :]], axis=1).astype(BF16)
            dyb = dy_s[rows, :].astype(BF16)
            dwc_ref[...] += lax.dot_general(st, dyb, (((0,), (0,)), ((), ())), preferred_element_type=F32)
            dwb_ref[...] += lax.dot_general(u_ref[rows, :].astype(BF16), g, (((0,), (0,)), ((), ())), preferred_element_type=F32)
            du = lax.dot_general(g, wb_ref[...], (((1,), (1,)), ((), ())), preferred_element_type=F32)
            du_ref[rows, :] = du + d_ref[...] * dy_s[rows, :]

        @pl.when(step == ng * n_ex - 1)
        def _():
            exch.wait()

    return pl.pallas_call(
        body, name=name, grid=(ng, n_ex),
        in_specs=[pl.BlockSpec((S, U_LANES), lambda g, e: (e, g))] * 3 + [
            pl.BlockSpec((S, 2 * ST_LANES), lambda g, e: (e, g)),
            pl.BlockSpec((None, U_LANES, 2 * ST_LANES), lambda g, e: (g, 0, 0)),
            pl.BlockSpec((None, 2 * ST_LANES, U_LANES), lambda g, e: (g, 0, 0)),
            pl.BlockSpec((None, 8, 8, ST_LANES), lambda g, e: (g, 0, 0, 0)),
            pl.BlockSpec((1, U_LANES), lambda g, e: (0, g)), ANY],
        out_specs=[pl.BlockSpec((S, U_LANES), lambda g, e: (e, g)),
                   pl.BlockSpec((None, U_LANES, 2 * ST_LANES), lambda g, e: (g, 0, 0)),
                   pl.BlockSpec((None, 2 * ST_LANES, U_LANES), lambda g, e: (g, 0, 0)),
                   pl.BlockSpec((None, 8, ST_LANES), lambda g, e: (g, 0, 0)),
                   pl.BlockSpec((None, 8, U_LANES), lambda g, e: (g, 0, 0)), ANY],
        out_shape=[jax.ShapeDtypeStruct((T, D), F32),
                   jax.ShapeDtypeStruct((ng, U_LANES, 2 * ST_LANES), F32),
                   jax.ShapeDtypeStruct((ng, 2 * ST_LANES, U_LANES), F32),
                   jax.ShapeDtypeStruct((ng, 8, ST_LANES), F32),
                   jax.ShapeDtypeStruct((ng, 8, U_LANES), F32), _ChipExchange.out_shape(xsrc, True)],
        scratch_shapes=[pltpu.VMEM((S, ST_LANES), F32)] * 2 + [pltpu.VMEM((S, U_LANES), F32)] + _ChipExchange.SCRATCH,
        compiler_params=_cp(("arbitrary", "arbitrary")),
    )(u, y, dgy, st, Wb, Wc, cr, d, xsrc)


TQ = 256
KW = 512
SUB = 128


def _head_masks():
    lane = lax.broadcasted_iota(jnp.int32, (1, 2 * HEAD_DIM), 1)
    m0 = (lane < HEAD_DIM).astype(F32)
    return m0, 1.0 - m0


def _head_norm(x, g, m0, m1):
    sq = x * x
    r0 = lax.rsqrt(jnp.sum(sq * m0, axis=-1, keepdims=True) / HEAD_DIM + EPS)
    r1 = lax.rsqrt(jnp.sum(sq * m1, axis=-1, keepdims=True) / HEAD_DIM + EPS)
    r = m0 * r0 + m1 * r1
    return x * r, r


def _head_norm_bwd(dy, n, r, g, m0, m1):
    dn = dy * g
    p = dn * n
    mean = (m0 * jnp.sum(p * m0, axis=-1, keepdims=True) + m1 * jnp.sum(p * m1, axis=-1, keepdims=True)) / HEAD_DIM
    return r * (dn - n * mean), _csum(dy * n)


def _pair_matrix(kind):
    r = lax.broadcasted_iota(jnp.int32, (2 * SUB, 2 * SUB), 0)
    c = lax.broadcasted_iota(jnp.int32, (2 * SUB, 2 * SUB), 1)
    same = (r < SUB) == (c < SUB)
    rel = {"after": r > c, "upto": r <= c, "before": r < c}[kind]
    return jnp.logical_and(same, rel).astype(BF16)


def _block_sums(x, mat, carry, reverse, terms=2):
    hi = x.astype(BF16)
    lo = (x - hi.astype(F32)).astype(BF16) if terms == 2 else None
    npair = x.shape[1] // (2 * SUB)
    parts = [None] * (2 * npair)
    for p in (range(npair - 1, -1, -1) if reverse else range(npair)):
        sl = slice(2 * SUB * p, 2 * SUB * (p + 1))
        loc = jnp.dot(hi[:, sl], mat, preferred_element_type=F32)
        if terms == 2:
            loc = loc + jnp.dot(lo[:, sl], mat, preferred_element_type=F32)
        for b in ((1, 0) if reverse else (0, 1)):
            k = 2 * p + b
            parts[k] = loc[:, SUB * b:SUB * (b + 1)] + carry
            carry = carry + jnp.sum(x[:, SUB * k:SUB * (k + 1)], axis=-1, keepdims=True)
    return jnp.concatenate(parts, axis=1), carry


def _sb_logits(z, mask):
    lp = jnp.minimum(z, 0.0) - jnp.log(1.0 + jnp.exp(-jnp.abs(z)))
    lf = lp - z
    if mask is not None:
        lf = jnp.where(mask, lf, 0.0)
    return lp, lf


def _causal_mask(row0, col0, kw):
    r = row0 + lax.broadcasted_iota(jnp.int32, (TQ, kw), 0)
    c = col0 + lax.broadcasted_iota(jnp.int32, (TQ, kw), 1)
    return c < r


def _transposed_windows(x, ref):
    for w in range(x.shape[0] // KW):
        ref[w] = x[w * KW:(w + 1) * KW, :].T.astype(BF16)


def _attn_fwd(q, kv, qg, kg, xsrc, *, n_ex, name):
    T, D = q.shape
    S = T // n_ex
    nhp = D // (2 * HEAD_DIM)
    nq = S // TQ
    scale = 1.0 / math.sqrt(HEAD_DIM)

    def body(q_ref, k_ref, v_ref, qg_ref, kg_ref, xsrc_ref, o_ref, tot_ref, ot_ref, xout_ref, kT_s, qm_s, vm_s, *sems):
        step = pl.program_id(0) * nhp + pl.program_id(1)
        exch = _ChipExchange(xsrc_ref, xout_ref, *sems, scatter=False)

        @pl.when(step == 0)
        def _():
            exch.start()

        m0, m1 = _head_masks()
        qn, _ = _head_norm(q_ref[...], None, m0, m1)
        qn = qn * (qg_ref[...] * scale)
        kn, _ = _head_norm(k_ref[...], None, m0, m1)
        _transposed_windows(kn * kg_ref[...], kT_s)
        v = v_ref[...]
        for h, m in enumerate((m0, m1)):
            qm_s[h] = (qn * m).astype(BF16)
            vm_s[h] = (v * m).astype(BF16)
        u_after = _pair_matrix("after")

        def window(rows, win, st, mask, kw):
            keys = pl.ds(pl.multiple_of(win * KW, KW), kw)
            zs = [jnp.dot(qm_s[h, rows, :], kT_s[win, :, :kw], preferred_element_type=F32) for h in range(2)]
            lg = [_sb_logits(zs[h], mask) for h in range(2)]
            sums = [_block_sums(lg[h][1], u_after, st[2 * h], True) for h in range(2)]
            out = ()
            for h in range(2):
                w = jnp.exp(lg[h][0] + sums[h][0])
                if mask is not None:
                    w = jnp.where(mask, w, 0.0)
                out += (sums[h][1], st[2 * h + 1] + jnp.dot(w.astype(BF16), vm_s[h, keys, :], preferred_element_type=F32))
            return out

        def qtile(iq, last, kw):
            rows = pl.ds(pl.multiple_of(iq * TQ, TQ), TQ)
            mask = _causal_mask(iq * TQ, last * KW, kw)
            z1, zq = jnp.zeros((TQ, 1), F32), jnp.zeros((TQ, 2 * HEAD_DIM), F32)
            st = window(rows, last, (z1, zq, z1, zq), mask, kw)
            st = lax.fori_loop(0, last, lambda jj, st: window(rows, last - 1 - jj, st, None, KW), st)
            o_ref[rows, :] = st[1] + st[3]
            tot_ref[rows, :] = st[0] * m0 + st[2] * m1

        def qtiles_of_window(a, _):
            for sub in range(KW // TQ):
                qtile(a * (KW // TQ) + sub, a, (sub + 1) * TQ)
            return 0

        lax.fori_loop(0, S // KW, qtiles_of_window, 0)
        ot_ref[...] = o_ref[...].T.astype(BF16)

        @pl.when(step == n_ex * nhp - 1)
        def _():
            exch.wait()

    assert S % KW == 0 and KW % TQ == 0
    nwin = S // KW
    blk = (S, 2 * HEAD_DIM)
    return pl.pallas_call(
        body, name=name, grid=(n_ex, nhp),
        in_specs=[pl.BlockSpec(blk, lambda e, h: (e, h)), pl.BlockSpec(blk, lambda e, h: (e, h)),
                  pl.BlockSpec(blk, lambda e, h: (e, h + nhp)),
                  pl.BlockSpec((1, 2 * HEAD_DIM), lambda e, h: (0, 0)), pl.BlockSpec((1, 2 * HEAD_DIM), lambda e, h: (0, 0)), ANY],
        out_specs=[pl.BlockSpec(blk, lambda e, h: (e, h))] * 2 + [pl.BlockSpec((2 * HEAD_DIM, S), lambda e, h: (h, e)), ANY],
        out_shape=[jax.ShapeDtypeStruct((T, D), F32)] * 2 + [jax.ShapeDtypeStruct((D, T), BF16), _ChipExchange.out_shape(xsrc, False)],
        scratch_shapes=[pltpu.VMEM((nwin, 2 * HEAD_DIM, KW), BF16), pltpu.VMEM((2,) + blk, BF16), pltpu.VMEM((2,) + blk, BF16)]
        + _ChipExchange.SCRATCH,
        compiler_params=_cp(("arbitrary", "arbitrary")),
    )(q, kv, kv, qg, kg, xsrc)


def _attn_bwd(q, kv, tot, do, qg, kg, *, n_ex, name):
    T, D = q.shape
    S = T // n_ex
    nhp = D // (2 * HEAD_DIM)
    nq = S // TQ
    scale = 1.0 / math.sqrt(HEAD_DIM)

    def body(q_ref, k_ref, v_ref, tot_ref, do_ref, qg_ref, kg_ref, dq_ref, dk_ref, dv_ref, dqg_ref, dkg_ref,
             kT_s, vT_s, km_s, qm_s, dom_s, dqn_s, dkT_s, dvT_s):
        m0, m1 = _head_masks()
        qn, qr = _head_norm(q_ref[...], None, m0, m1)
        kn, kr = _head_norm(k_ref[...], None, m0, m1)
        qs = qn * (qg_ref[...] * scale)
        kk = kn * kg_ref[...]
        _transposed_windows(kk, kT_s)
        _transposed_windows(v_ref[...], vT_s)
        do = do_ref[...]
        for h, m in enumerate((m0, m1)):
            qm_s[h] = (qs * m).astype(BF16)
            km_s[h] = (kk * m).astype(BF16)
            dom_s[h] = (do * m).astype(BF16)
        dkT_s[...] = jnp.zeros_like(dkT_s)
        dvT_s[...] = jnp.zeros_like(dvT_s)
        u_upto, u_before = _pair_matrix("upto"), _pair_matrix("before")

        def both(inv, win, st, mask, kw):
            keys = pl.ds(pl.multiple_of(win * KW, KW), kw)
            lg = [_sb_logits(jnp.dot(inv[h][0], kT_s[win, :, :kw], preferred_element_type=F32), mask) for h in range(2)]
            s_lf = [_block_sums(lg[h][1], u_upto, st[3 * h], False) for h in range(2)]
            ws, ews = [], []
            for h in range(2):
                w = jnp.exp(lg[h][0] - s_lf[h][0])
                if mask is not None:
                    w = jnp.where(mask, w, 0.0)
                ws.append(w)
                ews.append(jnp.dot(inv[h][2], vT_s[win, :, :kw], preferred_element_type=F32) * w)
            s_e = [_block_sums(ews[h], u_before, st[3 * h + 1], False, terms=1) for h in range(2)]
            out, dk, dv = (), None, None
            for h in range(2):
                sig = jnp.exp(lg[h][0])
                dz = ews[h] - sig * (ews[h] + s_e[h][0])
                if mask is not None:
                    dz = jnp.where(mask, dz, 0.0)
                dzb = dz.astype(BF16)
                out += (s_lf[h][1], s_e[h][1], st[3 * h + 2] + jnp.dot(dzb, km_s[h, keys, :], preferred_element_type=F32))
                dkh = jnp.dot(inv[h][1], dzb, preferred_element_type=F32)
                dvh = jnp.dot(inv[h][3], ws[h].astype(BF16), preferred_element_type=F32)
                dk, dv = (dkh, dvh) if h == 0 else (dk + dkh, dv + dvh)
            dkT_s[win, :, :kw] += dk
            dvT_s[win, :, :kw] += dv
            return out

        def qtile(iq, last, kw):
            rows = pl.ds(pl.multiple_of(iq * TQ, TQ), TQ)
            mask = _causal_mask(iq * TQ, last * KW, kw)
            tt = tot_ref[rows, :]
            inv, neg_total = [], []
            for h, m in enumerate((m0, m1)):
                qh, doh = qm_s[h, rows, :], dom_s[h, rows, :]
                neg_total.append(jnp.sum(tt * m, axis=-1, keepdims=True) * (-1.0 / HEAD_DIM))
                inv.append((qh, qh.astype(F32).T.astype(BF16), doh, doh.astype(F32).T.astype(BF16)))

            z1, zq = jnp.zeros((TQ, 1), F32), jnp.zeros((TQ, 2 * HEAD_DIM), F32)
            st = lax.fori_loop(0, last, lambda win, st: both(inv, win, st, None, KW), (neg_total[0], z1, zq, neg_total[1], z1, zq))
            st = both(inv, last, st, mask, kw)
            dqn_s[rows, :] = st[2] + st[5]

        def qtiles_of_window(a, _):
            for sub in range(KW // TQ):
                qtile(a * (KW // TQ) + sub, a, (sub + 1) * TQ)
            return 0

        lax.fori_loop(0, S // KW, qtiles_of_window, 0)
        dkn = jnp.concatenate([dkT_s[w].T for w in range(nwin)], axis=0)
        dq, dqg = _head_norm_bwd(dqn_s[...] * scale, qn, qr, qg_ref[...], m0, m1)
        dk, dkg = _head_norm_bwd(dkn, kn, kr, kg_ref[...], m0, m1)
        dq_ref[...] = dq
        dk_ref[...] = dk
        dv_ref[...] = jnp.concatenate([dvT_s[w].T for w in range(nwin)], axis=0)
        dqg_ref[...] = dqg
        dkg_ref[...] = dkg

    assert S % KW == 0 and KW % TQ == 0
    nwin = S // KW
    blk = (S, 2 * HEAD_DIM)
    tblk = (nwin, 2 * HEAD_DIM, KW)
    gblk = (None, None, 1, 2 * HEAD_DIM)
    dq, dk, dv, dqg, dkg = pl.pallas_call(
        body, name=name, grid=(n_ex, nhp),
        in_specs=[pl.BlockSpec(blk, lambda e, h: (e, h)), pl.BlockSpec(blk, lambda e, h: (e, h)),
                  pl.BlockSpec(blk, lambda e, h: (e, h + nhp)),
                  pl.BlockSpec(blk, lambda e, h: (e, h)), pl.BlockSpec(blk, lambda e, h: (e, h)),
                  pl.BlockSpec((1, 2 * HEAD_DIM), lambda e, h: (0, 0)), pl.BlockSpec((1, 2 * HEAD_DIM), lambda e, h: (0, 0))],
        out_specs=[pl.BlockSpec(blk, lambda e, h: (e, h))] * 3 + [pl.BlockSpec(gblk, lambda e, h: (e, h, 0, 0))] * 2,
        out_shape=[jax.ShapeDtypeStruct((T, D), F32)] * 3 + [jax.ShapeDtypeStruct((n_ex, nhp, 1, 2 * HEAD_DIM), F32)] * 2,
        scratch_shapes=[pltpu.VMEM(tblk, BF16), pltpu.VMEM(tblk, BF16),
                        pltpu.VMEM((2,) + blk, BF16), pltpu.VMEM((2,) + blk, BF16), pltpu.VMEM((2,) + blk, BF16),
                        pltpu.VMEM(blk, F32), pltpu.VMEM(tblk, F32), pltpu.VMEM(tblk, F32)],
        compiler_params=_cp(("parallel", "parallel")),
    )(q, kv, kv, tot, do, qg, kg)
    return dq, dk, dv, dqg, dkg


def _place():
    return lax.axis_index("x"), lax.axis_index("y"), lax.axis_index("c")


def _all_gather8(x_shard, *, name):
    m_per, n = x_shard.shape

    def body(x_ref, out_ref, send_sems, recv_sems, local_sem):
        x, y, c = _place()
        me, sibling = (x, y, c), (x, y, 1 - c)
        chips = [(1 - x, y), (x, 1 - y), (1 - x, 1 - y)]

        def rows(px, py, pc):
            return out_ref.at[pl.ds((4 * px + 2 * py + pc) * m_per, m_per), :]

        def copy(k, block, to, src=None):
            return pltpu.make_async_remote_copy(
                src_ref=rows(*block) if src is None else src, dst_ref=rows(*block),
                send_sem=send_sems.at[k], recv_sem=recv_sems.at[k], device_id=to, device_id_type=MESH)

        mine = pltpu.make_async_copy(x_ref, rows(*me), local_sem)
        mine.start()
        first = [copy(0, me, sibling, src=x_ref)]
        first += [copy(1 + j, me, (*chip, c), src=x_ref) for j, chip in enumerate(chips)]
        for cp in first:
            cp.start()
        passed = [copy(4 + j, (*chip, c), sibling) for j, chip in enumerate(chips)]
        for j, chip in enumerate(chips):
            copy(1 + j, (*chip, c), me).wait_recv()
            passed[j].start()
        copy(0, sibling, me).wait_recv()
        for j, chip in enumerate(chips):
            copy(4 + j, (*chip, 1 - c), me).wait_recv()
        for cp in first + passed:
            cp.wait_send()
        mine.wait()

    return pl.pallas_call(
        body, name=name, out_shape=jax.ShapeDtypeStruct((8 * m_per, n), x_shard.dtype),
        in_specs=[pl.BlockSpec(memory_space=pltpu.VMEM)], out_specs=pl.BlockSpec(memory_space=pltpu.VMEM),
        scratch_shapes=[pltpu.SemaphoreType.DMA((7,)), pltpu.SemaphoreType.DMA((7,)), pltpu.SemaphoreType.DMA],
        compiler_params=pltpu.CompilerParams(vmem_limit_bytes=VMEM_LIMIT),
    )(x_shard)


def _sibling_sum_half(x, *, name):
    R, C = x.shape
    half = R // 2

    def body(x_ref, o_ref, theirs, send_sem, recv_sem):
        px, py, pc = _place()
        cp = pltpu.make_async_remote_copy(src_ref=x_ref, dst_ref=theirs, send_sem=send_sem, recv_sem=recv_sem,
                                          device_id=(px, py, 1 - pc), device_id_type=MESH)
        cp.start()
        cp.wait()
        rows = pl.ds(pl.multiple_of(pc * half, 8), half)
        o_ref[...] = x_ref[rows, :] + theirs[rows, :]

    return pl.pallas_call(
        body, name=name, out_shape=jax.ShapeDtypeStruct((half, C), x.dtype),
        in_specs=[pl.BlockSpec(memory_space=pltpu.VMEM)], out_specs=pl.BlockSpec(memory_space=pltpu.VMEM),
        scratch_shapes=[pltpu.VMEM((R, C), x.dtype), pltpu.SemaphoreType.DMA, pltpu.SemaphoreType.DMA],
        compiler_params=pltpu.CompilerParams(vmem_limit_bytes=VMEM_LIMIT),
    )(x)


def _sum_blocks(x, n, *, name):
    R = x.shape[0] // n

    def body(x_ref, o_ref):
        acc = x_ref[pl.ds(0, R), :]
        for k in range(1, n):
            acc = acc + x_ref[pl.ds(k * R, R), :]
        o_ref[...] = acc

    return pl.pallas_call(body, name=name, out_shape=jax.ShapeDtypeStruct((R, x.shape[1]), x.dtype),
                          compiler_params=pltpu.CompilerParams(vmem_limit_bytes=VMEM_LIMIT))(x)


def _colsum(x, *, name):
    def body(x_ref, o_ref):
        o_ref[...] = jnp.sum(x_ref[...], axis=0, keepdims=True)
    return pl.pallas_call(body, name=name, out_shape=jax.ShapeDtypeStruct((1, x.shape[1]), x.dtype))(x)


ANY = pl.BlockSpec(memory_space=pl.ANY)


class _ChipExchange:
    SCRATCH = [pltpu.SemaphoreType.DMA((3,)), pltpu.SemaphoreType.DMA((3,)), pltpu.SemaphoreType.DMA]

    @staticmethod
    def out_shape(src, scatter):
        return jax.ShapeDtypeStruct(((4,) + tuple(src.shape[1:])) if scatter else ((4, 2) + tuple(src.shape[1:])), src.dtype)

    def __init__(self, src_ref, out_ref, send_sems, recv_sems, local_sem, scatter):
        x, y, c = _place()
        myj = 2 * x + y
        chips = [(1 - x, y), (x, 1 - y), (1 - x, 1 - y)]

        def slot(j):
            return out_ref.at[j] if scatter else out_ref.at[j, c]

        def piece(j):
            return src_ref.at[j] if scatter else src_ref.at[c]

        self.mine = pltpu.make_async_copy(piece(myj), slot(myj), local_sem)
        self.sends = [pltpu.make_async_remote_copy(
            src_ref=piece(2 * cx + cy), dst_ref=slot(myj), send_sem=send_sems.at[k], recv_sem=recv_sems.at[k],
            device_id=(cx, cy, c), device_id_type=MESH) for k, (cx, cy) in enumerate(chips)]
        self.recvs = [pltpu.make_async_remote_copy(
            src_ref=slot(2 * cx + cy), dst_ref=slot(2 * cx + cy), send_sem=send_sems.at[k], recv_sem=recv_sems.at[k],
            device_id=(cx, cy, c), device_id_type=MESH) for k, (cx, cy) in enumerate(chips)]

    def start(self):
        self.mine.start()
        for cp in self.sends:
            cp.start()

    def wait(self):
        for cp in self.recvs:
            cp.wait_recv()
        for cp in self.sends:
            cp.wait_send()
        self.mine.wait()


def _sibling_fill(buf, *, axis, name):
    def half(ref, h):
        return ref.at[h] if axis == 0 else ref.at[:, h]

    def body(in_ref, out_ref, send_sem, recv_sem):
        x, y, c = _place()
        cp = pltpu.make_async_remote_copy(src_ref=half(out_ref, c), dst_ref=half(out_ref, c), send_sem=send_sem, recv_sem=recv_sem,
                                          device_id=(x, y, 1 - c), device_id_type=MESH)
        cp.start()
        pltpu.make_async_remote_copy(src_ref=half(out_ref, 1 - c), dst_ref=half(out_ref, 1 - c), send_sem=send_sem, recv_sem=recv_sem,
                                     device_id=(x, y, 1 - c), device_id_type=MESH).wait_recv()
        cp.wait_send()

    return pl.pallas_call(
        body, name=name, out_shape=jax.ShapeDtypeStruct(buf.shape, buf.dtype), in_specs=[ANY], out_specs=ANY,
        input_output_aliases={0: 0}, scratch_shapes=[pltpu.SemaphoreType.DMA, pltpu.SemaphoreType.DMA],
    )(buf)


def _sibling_swap_half(g, *, name):
    def body(g_ref, out_ref, send_sem, recv_sem):
        x, y, c = _place()
        cp = pltpu.make_async_remote_copy(src_ref=g_ref.at[:, 1 - c], dst_ref=out_ref, send_sem=send_sem, recv_sem=recv_sem,
                                          device_id=(x, y, 1 - c), device_id_type=MESH)
        cp.start()
        cp.wait()

    return pl.pallas_call(
        body, name=name, out_shape=jax.ShapeDtypeStruct((g.shape[0],) + g.shape[2:], g.dtype), in_specs=[ANY], out_specs=ANY,
        scratch_shapes=[pltpu.SemaphoreType.DMA, pltpu.SemaphoreType.DMA],
    )(g)


def _add_my_half(g, b, cidx, *, name, tr=1024):
    n, _, R, C = g.shape
    tr = max(t for t in range(16, tr + 1, 16) if R % t == 0)

    def body(c_ref, g_ref, b_ref, o_ref):
        o_ref[...] = (g_ref[...] + b_ref[...]).astype(o_ref.dtype)

    return pl.pallas_call(
        body, name=name, out_shape=jax.ShapeDtypeStruct((n, R, C), BF16),
        grid_spec=pltpu.PrefetchScalarGridSpec(
            num_scalar_prefetch=1, grid=(n, R // tr),
            in_specs=[pl.BlockSpec((None, None, tr, C), lambda j, i, c: (j, c[0], i, 0)),
                      pl.BlockSpec((None, tr, C), lambda j, i, c: (j, i, 0))],
            out_specs=pl.BlockSpec((None, tr, C), lambda j, i, c: (j, i, 0))),
        compiler_params=_cp(("parallel", "parallel")),
    )(cidx, g, b)


def _sum4_into_half(q, cidx, *, name, tr=1024):
    _, R, C = q.shape
    tr = max(t for t in range(16, tr + 1, 16) if R % t == 0)

    def body(c_ref, q_ref, o_ref):
        o_ref[...] = ((q_ref[0].astype(F32) + q_ref[1].astype(F32)) + q_ref[2].astype(F32)) + q_ref[3].astype(F32)

    return pl.pallas_call(
        body, name=name, out_shape=jax.ShapeDtypeStruct((2, R, C), F32),
        grid_spec=pltpu.PrefetchScalarGridSpec(
            num_scalar_prefetch=1, grid=(R // tr,),
            in_specs=[pl.BlockSpec((4, tr, C), lambda i, c: (0, i, 0))],
            out_specs=pl.BlockSpec((None, tr, C), lambda i, c: (c[0], i, 0))),
        compiler_params=_cp(("parallel",)),
    )(cidx, q)


def _pack_rows(parts, width=1024):
    rows, spans, r0 = [], [], 0
    for p in parts:
        n = p.size
        nr = 8 * (-(-n // (8 * width)))
        flat = p.reshape(-1)
        if nr * width != n:
            flat = jnp.pad(flat, (0, nr * width - n))
        rows.append(flat.reshape(nr, width))
        spans.append((r0, nr, n, p.shape))
        r0 += nr
    return jnp.concatenate(rows, axis=0), spans


def _unpack_rows(buf, spans):
    return [buf[r0:r0 + nr].reshape(-1)[:n].reshape(shape) for (r0, nr, n, shape) in spans]


def kernel(x, c, ada_w, ada_b, mix_norm_g, mlp_norm_g, mlp_w1, mlp_w2, s5_a_re, s5_a_im, s5_log_dt, s5_b_re, s5_b_im, s5_c_re, s5_c_im, s5_d, s5_w_glu, kv_ada_w, kv_ada_b, kv_norm_g, w_kv, k_norm_g, sb_w_q, q_norm_g, sb_w_o, loss_target, m_ada_w, m_ada_b, m_mix_norm_g, m_mlp_norm_g, m_mlp_w1, m_mlp_w2, m_s5_a_re, m_s5_a_im, m_s5_log_dt, m_s5_b_re, m_s5_b_im, m_s5_c_re, m_s5_c_im, m_s5_d, m_s5_w_glu, m_kv_ada_w, m_kv_ada_b, m_kv_norm_g, m_w_kv, m_k_norm_g, m_sb_w_q, m_q_norm_g, m_sb_w_o, v_ada_w, v_ada_b, v_mix_norm_g, v_mlp_norm_g, v_mlp_w1, v_mlp_w2, v_s5_a_re, v_s5_a_im, v_s5_log_dt, v_s5_b_re, v_s5_b_im, v_s5_c_re, v_s5_c_im, v_s5_d, v_s5_w_glu, v_kv_ada_w, v_kv_ada_b, v_kv_norm_g, v_w_kv, v_k_norm_g, v_sb_w_q, v_q_norm_g, v_sb_w_o):
    E, S, D = x.shape
    T = E * S
    FF = 4 * D
    NB = 8 * E
    px, py, pc = _place()
    chip = 2 * px + py
    dev = 4 * px + 2 * py + pc
    cidx = jnp.reshape(pc, (1,)).astype(jnp.int32)
    x0 = x.reshape(T, D)
    tgt = loss_target.reshape(T, D)

    nc_rows, nd = c.size // 128, s5_d.size // 128
    cd = jnp.concatenate([c.reshape(nc_rows, 128), jnp.pad(s5_d.reshape(nd, 128), ((0, 8 - nd), (0, 0)))], axis=0)
    cd_all = _all_gather8(cd, name="ag_c_d").reshape(8, nc_rows + 8, 128)
    c_all = cd_all[:, :nc_rows].reshape(NB, D)
    d_full = cd_all.reshape(4, 2, nc_rows + 8, 128)[:, 0, nc_rows:nc_rows + nd].reshape(1, D)
    sc_all = (c_all * _sigmoid(c_all)).astype(BF16)
    wa = ada_w.shape[2]
    wk = kv_ada_w.shape[1]
    m_sh = jnp.concatenate([_mm(sc_all, _Layer(ada_w, 0), "nn", name="ada0", tn=256),
                            _mm(sc_all, _Layer(ada_w, 1), "nn", name="ada1", tn=256),
                            _mm(sc_all, kv_ada_w, "nn", name="ada_kv", tn=256)], axis=1)
    m_all = _all_gather8(m_sh, name="ag_m").reshape(4, 2, NB, 2 * wa + wk)[:, 0]
    mods = []
    for l in range(2):
        full = jnp.transpose(m_all[:, :, l * wa:(l + 1) * wa], (1, 0, 2)).reshape(NB, 6 * D) + ada_b[l]
        mine = lax.dynamic_slice_in_dim(full, E * dev, E, axis=0)
        mods.append([mine[:, i * D:(i + 1) * D].reshape(E, 1, D) for i in range(6)])
    full = jnp.transpose(m_all[:, :, 2 * wa:], (1, 0, 2)).reshape(NB, 2 * D) + kv_ada_b
    mine = lax.dynamic_slice_in_dim(full, E * dev, E, axis=0)
    kv_sh, kv_sc = [mine[:, i * D:(i + 1) * D].reshape(E, 1, D) for i in range(2)]

    wpack_a = jnp.concatenate([mlp_w1[0], mlp_w2[0], jnp.concatenate([s5_w_glu[0], w_kv], axis=1), sb_w_q[0]], axis=0).astype(BF16)
    wpack_b = jnp.concatenate([mlp_w1[1], mlp_w2[1], sb_w_o[0]], axis=0).astype(BF16)
    RA, RB = wpack_a.shape[0], wpack_b.shape[0]
    RW = RA + RB

    tm = min(2048, S)
    tm_res = min(1024, S)
    gbuf = [jax.ShapeDtypeStruct((4, RW, D), F32)]

    def grad_mm(act, dout, kind, roff, nr, c0, nc, name, transposed=False):
        gbuf[0] = _mm(act, dout, "nn" if transposed else "tn", name=name, tm=1024, tk=2048,
                      into=_Sharded(gbuf[0], kind, roff, nr, c0, nc))

    def mlp_fwd(xa, l, mod):
        sh_m, sc_m, g_m = mod[3], mod[4], mod[5]
        h, h_t = _norm_mod_fwd(xa, mlp_norm_g[l:l + 1], sh_m, sc_m, n_ex=E, out_dtype=BF16, name=f"mlp_norm{l}", with_transpose=True)

        def relu_sq(acc):
            ra = jnp.maximum(acc, 0.0)
            return ra * ra, ra
        r, ra = _mm(h, W1[l], "nn", name=f"mlp_up{l}", out_dtypes=(BF16, BF16), tm=tm, epilogue=relu_sq)
        xb, ff = _mm(r, W2[l], "nn", name=f"mlp_down{l}", out_dtypes=(F32, F32), tm=tm_res,
                     extras=[_mn_extra(xa), _vec_extra(g_m, S)],
                     epilogue=lambda acc, xat, gt: (xat + gt * acc, acc))
        return xb, (h_t, r, ra, ff)

    def mlp_bwd(dxb, xa, l, mod, saved):
        sc_m, g_m = mod[4], mod[5]
        h_t, r, ra, ff = saved
        (dff,), (dgm,) = _rowwise(lambda d, f, g: ([g * d], [_csum(d * f)]), [(dxb, D, 0), (ff, D, 0)], [g_m], [],
                                  [(D, BF16)], [D], n_ex=E, name=f"mlp_gate_bwd{l}")
        da = _mm(dff, W2[l], "nt", name=f"mlp_down_dx{l}", out_dtypes=(BF16,), tm=tm, extras=[_mn_extra(ra)],
                 epilogue=lambda acc, rat: (acc * (2.0 * rat.astype(F32)),))
        grad_mm(r, dff, "rows", (2 + l) * D, D, 0, D, f"mlp_down_dw{l}")
        dh = _mm(da, W1[l], "nt", name=f"mlp_up_dx{l}", tm=tm)
        grad_mm(h_t, da, "cols", l * D, D, 0, D, f"mlp_up_dw{l}", transposed=True)
        (dxa,), (dsh, dsc, dg) = _norm_mod_bwd(xa, dh, dxb, mlp_norm_g[l:l + 1], sc_m, n_ex=E, name=f"mlp_norm_bwd{l}")
        return dxa, (dsh, dsc, dgm), dg

    ab_re, ab_im, bb_re, bb_im = _s5_disc(s5_a_re[0], s5_a_im[0], s5_log_dt[0], s5_b_re[0], s5_b_im[0])
    cf, cr = _s5_consts(ab_re, ab_im)
    Wb, Wc = _s5_blockdiag(bb_re, bb_im, s5_c_re[0], s5_c_im[0])
    ng = D // U_LANES

    mod0, mod1 = mods
    h0 = _norm_mod_fwd(x0, mix_norm_g[0:1], mod0[0], mod0[1], n_ex=E, out_dtype=F32, name="mix_norm0")
    y, gy, gy_t, s5_states, wfull_a = _s5_fwd(h0, Wb, Wc, cf, d_full, wpack_a.reshape(2, RA // 2, D), n_ex=E, name="s5_fwd")
    wfull_a = _sibling_fill(wfull_a, axis=1, name="wgather_a_d2d").reshape(4, RA, D)

    W1 = [_Sharded(wfull_a, "cols", 0, D, 0, D), None]
    W2 = [_Sharded(wfull_a, "rows", D, D, 0, D), None]
    Wglu = _Sharded(wfull_a, "cols", 2 * D, D, 0, D // 2)
    Wkv = _Sharded(wfull_a, "cols", 2 * D, D, D // 2, D // 2)
    Wq = _Sharded(wfull_a, "rows", 3 * D, D // 4, 0, D)
    vg = _mm(gy, Wglu, "nn", name="glu_up", tm=tm)
    (x1,), _ = _rowwise(lambda v, g, xt, ga: ([xt + ga * (v * _sigmoid(g))], []),
                        [(vg, D, 0), (vg, D, 1), (x0, D, 0)], [mod0[2]], [], [(D, F32)], [], n_ex=E, name="glu_gate")
    x2, saved_mlp0 = mlp_fwd(x1, 0, mod0)

    hkv, hkv_t = _norm_mod_fwd(x2, kv_norm_g.reshape(1, D), kv_sh, kv_sc, n_ex=E, out_dtype=BF16, name="kv_norm", with_transpose=True)
    kvf = _mm(hkv, Wkv, "nn", name="kv_proj", tm=tm)
    h1, h1_t = _norm_mod_fwd(x2, mix_norm_g[1:2], mod1[0], mod1[1], n_ex=E, out_dtype=BF16, name="mix_norm1", with_transpose=True)
    qf = _mm(h1, Wq, "nn", name="q_proj", tm=tm)
    qg2 = jnp.tile(q_norm_g.reshape(1, HEAD_DIM), (1, 2))
    kg2 = jnp.tile(k_norm_g.reshape(1, HEAD_DIM), (1, 2))
    o, lf_tot, o_t, wfull_b = _attn_fwd(qf, kvf, qg2, kg2, wpack_b.reshape(2, RB // 2, D), n_ex=E, name="attn_fwd")
    wfull_b = _sibling_fill(wfull_b, axis=1, name="wgather_b_d2d").reshape(4, RB, D)
    W1[1] = _Sharded(wfull_b, "cols", 0, D, 0, D)
    W2[1] = _Sharded(wfull_b, "rows", D, D, 0, D)
    Wo = _Sharded(wfull_b, "rows", 2 * D, D // 4, 0, D)
    x3, mix1 = _mm(o, Wo, "nn", name="o_proj", out_dtypes=(F32, F32), tm=tm_res,
                   extras=[_mn_extra(x2), _vec_extra(mod1[2], S)],
                   epilogue=lambda acc, xat, gt: (xat + gt * acc, acc))
    x4, saved_mlp1 = mlp_fwd(x3, 1, mod1)

    (dx4,), (lsum,) = _rowwise(lambda xt, tt: ([(xt - tt) * (1.0 / D)], [_csum(jnp.square(xt - tt)) * (0.5 / D)]),
                               [(x4, D, 0), (tgt, D, 0)], [], [], [(D, F32)], [D], n_ex=E, name="loss")
    loss = lax.psum(jnp.sum(lsum), ("x", "y", "c"))

    dx3, (dsh_m1, dsc_m1, dgm1), dg_mlp1 = mlp_bwd(dx4, x3, 1, mod1, saved_mlp1)
    (dmix1,), (dga1,) = _rowwise(lambda d, f, g: ([g * d], [_csum(d * f)]), [(dx3, D, 0), (mix1, D, 0)], [mod1[2]], [],
                                 [(D, BF16)], [D], n_ex=E, name="attn_gate_bwd")
    do = _mm(dmix1, Wo, "nt", name="o_proj_dx", tm=tm)
    grad_mm(o_t, dmix1, "rows", 5 * D + D // 4, D // 4, 0, D, "o_proj_dw", transposed=True)
    dq, dk, dv, dqg, dkg = _attn_bwd(qf, kvf, lf_tot, do, qg2, kg2, n_ex=E, name="attn_bwd")
    dh1 = _mm(dq, Wq, "nt", name="q_proj_dx", tm=tm)
    grad_mm(h1_t, dq, "rows", 5 * D, D // 4, 0, D, "q_proj_dw", transposed=True)
    (dx2,), (dsh_a1, dsc_a1, dg_mix1) = _norm_mod_bwd(x2, dh1, dx3, mix_norm_g[1:2], mod1[1], n_ex=E, name="mix_norm_bwd1")
    dkv = jnp.concatenate([dk, dv], axis=1)
    dhkv = _mm(dkv, Wkv, "nt", name="kv_proj_dx", tm=tm)
    grad_mm(hkv_t, dkv, "cols", 4 * D, D, D // 2, D // 2, "kv_proj_dw", transposed=True)
    (dx2,), (dkv_sh, dkv_sc, dg_kv) = _norm_mod_bwd(x2, dhkv, dx2, kv_norm_g.reshape(1, D), kv_sc, n_ex=E, name="kv_norm_bwd")

    dx1, (dsh_m0, dsc_m0, dgm0), dg_mlp0 = mlp_bwd(dx2, x1, 0, mod0, saved_mlp0)

    def glu_bwd(v, g, d, ga):
        sg = _sigmoid(g)
        dm = ga * d
        return [jnp.concatenate([dm * sg, dm * v * sg * (1.0 - sg)], axis=1)], [_csum(d * (v * sg))]
    (dvg,), (dga0,) = _rowwise(glu_bwd, [(vg, D, 0), (vg, D, 1), (dx1, D, 0)], [mod0[2]], [], [(2 * D, BF16)], [D],
                               n_ex=E, name="glu_gate_bwd")
    dgy = _mm(dvg, Wglu, "nt", name="glu_up_dx", tm=tm)
    grad_mm(gy_t, dvg, "cols", 4 * D, D, 0, D // 2, "glu_up_dw", transposed=True)

    gpack = gbuf[0].reshape(4, 2, RW // 2, D)
    theirs = _sibling_swap_half(gpack, name="gscatter_d2d")
    chip_sum = _add_my_half(gpack, theirs, cidx, name="gscatter_add")
    dh0, dWb, dWc, dab, dd, from_chips = _s5_bwd(h0, y, dgy, s5_states, Wb, Wc, cr, d_full, chip_sum, n_ex=E, name="s5_bwd")
    ghalf = _sum4_into_half(from_chips, cidx, name="gscatter_sum")
    gsh = _sibling_fill(ghalf, axis=0, name="gscatter_fill").reshape(RW, D)
    (gx,), (dsh_a0, dsc_a0, dg_mix0) = _norm_mod_bwd(x0, dh0, dx1, mix_norm_g[0:1], mod0[1], n_ex=E, name="mix_norm_bwd0")
    grad_x = gx.reshape(E, S, D)

    dm_mine = jnp.concatenate([t.reshape(E, D) for t in
                               (dsh_a0, dsc_a0, dga0, dsh_m0, dsc_m0, dgm0, dsh_a1, dsc_a1, dga1, dsh_m1, dsc_m1, dgm1, dkv_sh, dkv_sc)], axis=1)
    dm_all = _all_gather8(dm_mine.reshape(8, -1), name="ag_dm").reshape(NB, 14 * D)
    sc_f32 = c_all * _sigmoid(c_all)
    g_ada_w = jax.ShapeDtypeStruct(ada_w.shape, F32)
    for l in range(2):
        g_ada_w = _mm(sc_f32, lax.dynamic_slice_in_dim(dm_all, l * 6 * D + chip * wa, wa, axis=1), "tn", name=f"ada_dw{l}", tn=256,
                      into=_Layer(g_ada_w, l))
    g_kv_ada_w = _mm(sc_f32, lax.dynamic_slice_in_dim(dm_all, 12 * D + chip * wk, wk, axis=1), "tn", name="ada_kv_dw", tn=256)
    db_all = _colsum(dm_all, name="ada_db")
    g_ada_b = db_all[0, :12 * D].reshape(2, 6 * D)
    g_kv_ada_b = db_all[0, 12 * D:]

    dWb_re, dWb_im, dC_re, dC_im = _s5_unblock(dWb, dWc)
    small_parts = [dg_mix0.sum(0), dg_mix1.sum(0), dg_mlp0.sum(0), dg_mlp1.sum(0), dg_kv.sum(0),
                   dqg.sum((0, 1, 2)).reshape(2, HEAD_DIM).sum(0), dkg.sum((0, 1, 2)).reshape(2, HEAD_DIM).sum(0),
                   dd[:, 0, :], dab[:, 0, :], dab[:, 1, :], dWb_re, dWb_im, dC_re, dC_im]
    spack, spans = _pack_rows(small_parts)
    chip_half = _sibling_sum_half(spack, name="small_d2d")
    ssum = _sum_blocks(_all_gather8(chip_half, name="ag_small"), 4, name="sum_small")
    (g_mix0, g_mix1, g_mlp0, g_mlp1, g_kvn, g_qn, g_kn, g_d, g_abr, g_abi, g_bbr, g_bbi, g_cre, g_cim) = _unpack_rows(ssum, spans)
    _, disc_vjp = jax.vjp(_s5_disc, s5_a_re[0], s5_a_im[0], s5_log_dt[0], s5_b_re[0], s5_b_im[0])
    g_are, g_aim, g_ldt, g_bre, g_bim = disc_vjp((g_abr.reshape(ab_re.shape), g_abi.reshape(ab_im.shape), g_bbr, g_bbi))
    g_s5d = lax.dynamic_slice_in_dim(g_d.reshape(1, D), chip * s5_d.shape[1], s5_d.shape[1], axis=1)

    def upd_big(w, m, v, roff, cb, name):
        shape = w.shape
        W = shape[-1]
        d_, m_, v_, g_ = _adamw2d(w.reshape(-1, W), gsh, m.reshape(-1, W), v.reshape(-1, W), name=name, g_roff=roff, g_cb=cb)
        return [t.reshape(shape) for t in (g_, d_, m_, v_)]

    def upd_own(w, g, m, v, name):
        shape = w.shape
        W = shape[-1]
        d_, m_, v_, g_ = _adamw2d(w.reshape(-1, W), g.reshape(-1, W), m.reshape(-1, W), v.reshape(-1, W), name=name)
        return [t.reshape(shape) for t in (g_, d_, m_, v_)]

    res = {}
    res["ada_w"] = upd_own(ada_w, g_ada_w, m_ada_w, v_ada_w, "adam_ada_w")
    res["kv_ada_w"] = upd_own(kv_ada_w, g_kv_ada_w, m_kv_ada_w, v_kv_ada_w, "adam_kv_ada_w")
    res["mlp_w1"] = upd_big(mlp_w1, m_mlp_w1, v_mlp_w1, 0, 0, "adam_w1")
    res["mlp_w2"] = upd_big(mlp_w2, m_mlp_w2, v_mlp_w2, 2 * D, 0, "adam_w2")
    res["s5_w_glu"] = upd_big(s5_w_glu, m_s5_w_glu, v_s5_w_glu, 4 * D, 0, "adam_glu")
    res["w_kv"] = upd_big(w_kv, m_w_kv, v_w_kv, 4 * D, 1, "adam_wkv")
    res["sb_w_q"] = upd_big(sb_w_q, m_sb_w_q, v_sb_w_q, 5 * D, 0, "adam_wq")
    res["sb_w_o"] = upd_big(sb_w_o, m_sb_w_o, v_sb_w_o, 5 * D + D // 4, 0, "adam_wo")

    small = {
        "ada_b": (ada_b, g_ada_b, m_ada_b, v_ada_b),
        "mix_norm_g": (mix_norm_g, jnp.stack([g_mix0, g_mix1]), m_mix_norm_g, v_mix_norm_g),
        "mlp_norm_g": (mlp_norm_g, jnp.stack([g_mlp0, g_mlp1]), m_mlp_norm_g, v_mlp_norm_g),
        "s5_a_re": (s5_a_re, g_are[None], m_s5_a_re, v_s5_a_re),
        "s5_a_im": (s5_a_im, g_aim[None], m_s5_a_im, v_s5_a_im),
        "s5_log_dt": (s5_log_dt, g_ldt[None], m_s5_log_dt, v_s5_log_dt),
        "s5_b_re": (s5_b_re, g_bre[None], m_s5_b_re, v_s5_b_re),
        "s5_b_im": (s5_b_im, g_bim[None], m_s5_b_im, v_s5_b_im),
        "s5_c_re": (s5_c_re, g_cre[None], m_s5_c_re, v_s5_c_re),
        "s5_c_im": (s5_c_im, g_cim[None], m_s5_c_im, v_s5_c_im),
        "s5_d": (s5_d, g_s5d, m_s5_d, v_s5_d),
        "kv_ada_b": (kv_ada_b, g_kv_ada_b, m_kv_ada_b, v_kv_ada_b),
        "kv_norm_g": (kv_norm_g, g_kvn, m_kv_norm_g, v_kv_norm_g),
        "k_norm_g": (k_norm_g, g_kn, m_k_norm_g, v_k_norm_g),
        "q_norm_g": (q_norm_g, g_qn.reshape(q_norm_g.shape), m_q_norm_g, v_q_norm_g),
    }
    names = list(small)
    packs = [_pack_rows([small[n][i].reshape(small[n][0].shape) for n in names]) for i in range(4)]
    sp = packs[0][1]
    d_, m_, v_, g_ = _adamw2d(packs[0][0], packs[1][0], packs[2][0], packs[3][0], name="adam_small")
    for n, gg, dd_, mm_, vv_ in zip(names, _unpack_rows(g_, sp), _unpack_rows(d_, sp), _unpack_rows(m_, sp), _unpack_rows(v_, sp)):
        res[n] = [gg, dd_, mm_, vv_]

    order = ["ada_w", "ada_b", "mix_norm_g", "mlp_norm_g", "mlp_w1", "mlp_w2", "s5_a_re", "s5_a_im", "s5_log_dt", "s5_b_re", "s5_b_im",
             "s5_c_re", "s5_c_im", "s5_d", "s5_w_glu", "kv_ada_w", "kv_ada_b", "kv_norm_g", "w_kv", "k_norm_g", "sb_w_q", "q_norm_g", "sb_w_o"]
    return (loss, grad_x, *[res[n][0] for n in order], *[res[n][1] for n in order], *[res[n][2] for n in order], *[res[n][3] for n in order])
```

```python
import functools
import math

import jax
import jax.numpy as jnp
from jax import lax
from jax.experimental import pallas as pl
from jax.experimental.pallas import tpu as pltpu

F32 = jnp.float32
BF16 = jnp.bfloat16
EPS = 1e-6
HEAD_DIM = 64
S5_GROUP = 16
S5_STATE = 64
GROUPS_PER_STEP = 8
U_LANES = GROUPS_PER_STEP * S5_GROUP
ST_LANES = GROUPS_PER_STEP * S5_STATE
SCAN_LANES = 256
SCAN_UNROLL = 4
VMEM_LIMIT = 56 * 1024 * 1024
ADAM_LR, ADAM_B1, ADAM_B2, ADAM_EPS, ADAM_WD, ADAM_STEP = 0.001, 0.9, 0.999, 1e-08, 0.01, 10
MESH = pl.DeviceIdType.MESH


def _cp(sem):
    return pltpu.CompilerParams(dimension_semantics=sem, vmem_limit_bytes=VMEM_LIMIT)


class _Sharded:
    def __init__(self, buf, kind, roff, nr, c0, nc):
        self.buf, self.kind, self.roff, self.nr, self.c0, self.nc = buf, kind, roff, nr, c0, nc
        self.shape = (nr, 4 * nc) if kind == "cols" else (4 * nr, nc)

    def operand(self, dims, tn, tk):
        roff, nr, c0, nc = self.roff, self.nr, self.c0, self.nc
        if self.kind == "cols" and dims == "nn":
            tk = min(tk, nr)
            assert roff % tk == 0
            return nc, tk, (None, tk, nc), lambda i, j, k: (j, roff // tk + k, c0 // nc)
        if self.kind == "cols":
            tn = min(tn, nr)
            assert roff % tn == 0
            return tn, nc, (None, tn, nc), lambda i, j, k: (k, roff // tn + j, c0 // nc)
        if dims == "nn":
            tn = min(tn, nc)
            assert roff % nr == 0 and c0 % tn == 0
            return tn, nr, (None, nr, tn), lambda i, j, k: (k, roff // nr, c0 // tn + j)
        tk = min(tk, nc)
        assert roff % nr == 0 and c0 % tk == 0
        return nr, tk, (None, nr, tk), lambda i, j, k: (j, roff // nr, c0 // tk + k)

    def result(self, tm, tn):
        roff, nr, c0, nc = self.roff, self.nr, self.c0, self.nc
        if self.kind == "cols":
            tm = min(tm, nr)
            assert roff % tm == 0
            return tm, nc, (None, tm, nc), lambda i, j, k: (j, roff // tm + i, c0 // nc)
        tm, tn = min(tm, nr), min(tn, nc)
        assert roff % tm == 0 and c0 % tn == 0
        per = nr // tm
        return tm, tn, (None, tm, tn), lambda i, j, k: (i // per, roff // tm + i % per, c0 // tn + j)


class _Layer:
    def __init__(self, buf, layer):
        self.buf, self.layer, self.shape = buf, layer, tuple(buf.shape[1:])

    def operand(self, dims, tn, tk):
        assert dims == "nn"
        layer = self.layer
        return tn, tk, (None, tk, tn), lambda i, j, k: (layer, k, j)

    def result(self, tm, tn):
        layer = self.layer
        return tm, tn, (None, tm, tn), lambda i, j, k: (layer, i, j)


def _mm(a, b, dims, *, name, out_dtypes=(F32,), epilogue=None, extras=(), tm=512, tn=1024, tk=1024, into=None):
    bshape = b.shape
    if dims == "nn":
        (M, K), (_, N) = a.shape, bshape
    elif dims == "nt":
        (M, K), (N, _) = a.shape, bshape
    else:
        (K, M), (_, N) = a.shape, bshape
    tm, tn, tk = min(tm, M), min(tn, N), min(tk, K)
    b_arr = b
    if into is not None:
        assert (M, N) == into.shape and len(out_dtypes) == 1 and not isinstance(b, _Sharded)
        tm, tn, o_blk, o_map = into.result(tm, tn)
        out_specs, out_shape = [pl.BlockSpec(o_blk, o_map)], [jax.ShapeDtypeStruct(into.buf.shape, into.buf.dtype)]
    if isinstance(b, (_Sharded, _Layer)):
        tn, tk, b_blk, b_map = b.operand(dims, tn, tk)
        b_spec, b_arr = pl.BlockSpec(b_blk, b_map), b.buf
    else:
        b_spec = pl.BlockSpec((tn, tk), lambda i, j, k: (j, k)) if dims == "nt" else pl.BlockSpec((tk, tn), lambda i, j, k: (k, j))
    if into is None:
        out_specs = [pl.BlockSpec((tm, tn), lambda i, j, k: (i, j)) for _ in out_dtypes]
        out_shape = [jax.ShapeDtypeStruct((M, N), d) for d in out_dtypes]
    assert M % tm == 0 and N % tn == 0 and K % tk == 0, (M, N, K, tm, tn, tk)
    nk = K // tk
    extras = [e(tm, tn) for e in extras]
    a_spec = pl.BlockSpec((tk, tm), lambda i, j, k: (k, i)) if dims == "tn" else pl.BlockSpec((tm, tk), lambda i, j, k: (i, k))
    contract = {"nn": ((1,), (0,)), "nt": ((1,), (1,)), "tn": ((0,), (0,))}[dims]
    n_ex, n_out = len(extras), len(out_dtypes)
    chain = [into.buf] if into is not None and not isinstance(into.buf, jax.ShapeDtypeStruct) else []
    n_in = n_ex + len(chain)

    def finish(r, ex, outs):
        res = epilogue(r, *[e[...] for e in ex]) if epilogue is not None else (r,)
        for o, v in zip(outs, res):
            o[...] = v.astype(o.dtype)

    def product(a_ref, b_ref):
        return lax.dot_general(a_ref[...].astype(BF16), b_ref[...].astype(BF16), (contract, ((), ())), preferred_element_type=F32)

    def body_one(a_ref, b_ref, *rest):
        finish(product(a_ref, b_ref), rest[:n_ex], rest[n_in:])

    def body_acc(a_ref, b_ref, *rest):
        ex, outs, acc = rest[:n_ex], rest[n_in:n_in + n_out], rest[-1]
        k = pl.program_id(2)

        @pl.when(k == 0)
        def _():
            acc[...] = product(a_ref, b_ref)

        @pl.when(jnp.logical_and(k > 0, k < nk - 1))
        def _():
            acc[...] += product(a_ref, b_ref)

        @pl.when(k == nk - 1)
        def _():
            finish(acc[...] + product(a_ref, b_ref), ex, outs)

    out = pl.pallas_call(
        body_one if nk == 1 else body_acc, name=name, grid=(M // tm, N // tn, nk),
        in_specs=[a_spec, b_spec] + [pl.BlockSpec(blk, im) for (_, blk, im) in extras] + [ANY for _ in chain],
        out_specs=out_specs, out_shape=out_shape,
        input_output_aliases={2 + n_ex: 0} if chain else {},
        scratch_shapes=[] if nk == 1 else [pltpu.VMEM((tm, tn), F32)],
        compiler_params=_cp(("parallel", "parallel", "arbitrary")),
    )(a, b_arr, *[e[0] for e in extras], *chain)
    return out if n_out > 1 else out[0]


def _mn_extra(arr):
    return lambda tm, tn: (arr, (tm, tn), lambda i, j, k: (i, j))


def _vec_extra(vec, S):
    return lambda tm, tn: (vec, (None, 1, tn), lambda i, j, k: ((i * tm) // S, 0, j))


def _rowwise(fn, rows, vecs=(), consts=(), out_rows=(), out_sums=(), *, n_ex, name, tr=512):
    rows = [r if len(r) == 4 else (*r, 0) for r in rows]
    S = min(r[0].shape[0] for r in rows if r[3] == 0) // n_ex
    tr = math.gcd(tr, S)
    assert S % tr == 0
    nb = S // tr
    in_specs = []
    for (arr, w, cb, roff) in rows:
        assert roff % tr == 0
        in_specs.append(pl.BlockSpec((tr, w), functools.partial(lambda e, i, cb, ro: (e * nb + i + ro, cb), cb=cb, ro=roff // tr)))
    for v in vecs:
        in_specs.append(pl.BlockSpec((None, 1, v.shape[-1]), lambda e, i: (e, 0, 0)))
    for c in consts:
        in_specs.append(pl.BlockSpec((1, c.shape[-1]), lambda e, i: (0, 0)))
    n_in, n_or, n_os = len(in_specs), len(out_rows), len(out_sums)
    flipped = [len(o) == 3 and o[2] for o in out_rows]
    out_specs = [pl.BlockSpec((o[0], tr), lambda e, i: (0, e * nb + i)) if f else pl.BlockSpec((tr, o[0]), lambda e, i: (e * nb + i, 0))
                 for o, f in zip(out_rows, flipped)]
    out_specs += [pl.BlockSpec((None, 1, w), lambda e, i: (e, 0, 0)) for w in out_sums]
    out_shape = [jax.ShapeDtypeStruct((o[0], n_ex * S) if f else (n_ex * S, o[0]), o[1]) for o, f in zip(out_rows, flipped)]
    out_shape += [jax.ShapeDtypeStruct((n_ex, 1, w), F32) for w in out_sums]

    def body(*refs):
        ins, o_r, o_s = refs[:n_in], refs[n_in:n_in + n_or], refs[n_in + n_or:]
        ro, so = fn(*[r[...] for r in ins])
        for o, v, f in zip(o_r, ro, flipped):
            o[...] = (v.T if f else v).astype(o.dtype)
        i = pl.program_id(1)
        for o, v in zip(o_s, so):
            @pl.when(i == 0)
            def _(o=o, v=v):
                o[...] = v

            @pl.when(i > 0)
            def _(o=o, v=v):
                o[...] += v

    outs = pl.pallas_call(
        body, name=name, grid=(n_ex, nb), in_specs=in_specs, out_specs=out_specs, out_shape=out_shape,
        compiler_params=_cp(("parallel", "arbitrary")),
    )(*[r[0] for r in rows], *vecs, *consts)
    return outs[:n_or], outs[n_or:]


def _csum(x):
    return jnp.sum(x, axis=0, keepdims=True)


def _norm_mod_fwd(x, g, sh, sc, *, n_ex, out_dtype, name, with_transpose=False):
    def fn(xt, sht, sct, gt):
        r = lax.rsqrt(jnp.mean(xt * xt, axis=-1, keepdims=True) + EPS)
        h = (xt * r * gt) * (1.0 + sct) + sht
        return [h, h] if with_transpose else [h], []
    D = x.shape[1]
    outs = [(D, out_dtype), (D, out_dtype, True)] if with_transpose else [(D, out_dtype)]
    res = _rowwise(fn, [(x, D, 0)], [sh, sc], [g], outs, [], n_ex=n_ex, name=name)[0]
    return res if with_transpose else res[0]


def _norm_mod_bwd(x, dh, dres, g, sc, *, n_ex, name):
    def fn(xt, dht, drt, sct, gt):
        dht = dht.astype(F32)
        r = lax.rsqrt(jnp.mean(xt * xt, axis=-1, keepdims=True) + EPS)
        n = xt * r
        y = n * gt
        dy = dht * (1.0 + sct)
        dn = dy * gt
        dx = r * (dn - n * jnp.mean(dn * n, axis=-1, keepdims=True))
        return [drt + dx], [_csum(dht), _csum(dht * y), _csum(dy * n)]
    D = x.shape[1]
    return _rowwise(fn, [(x, D, 0), (dh, D, 0), (dres, D, 0)], [sc], [g], [(D, F32)], [D, D, D], n_ex=n_ex, name=name)


def _sigmoid(x):
    return 1.0 / (1.0 + jnp.exp(-x))


def _gelu(y):
    return 0.5 * y * (1.0 + jnp.tanh(0.7978845608028654 * (y + 0.044715 * y * y * y)))


def _gelu_grad(y):
    t = jnp.tanh(0.7978845608028654 * (y + 0.044715 * y * y * y))
    return 0.5 * (1.0 + t) + 0.5 * y * (1.0 - t * t) * 0.7978845608028654 * (1.0 + 3 * 0.044715 * y * y)


def _adamw_fn(w, g, m, v):
    m2 = ADAM_B1 * m + (1.0 - ADAM_B1) * g
    v2 = ADAM_B2 * v + (1.0 - ADAM_B2) * (g * g)
    m_hat = m2 / (1.0 - ADAM_B1 ** ADAM_STEP)
    v_hat = v2 / (1.0 - ADAM_B2 ** ADAM_STEP)
    delta = -ADAM_LR * (m_hat / (jnp.sqrt(v_hat) + ADAM_EPS) + ADAM_WD * w)
    return delta, m2, v2


def _adamw2d(w, g, m, v, *, name, g_roff=0, g_cb=0):
    R, W = w.shape

    def fn(wt, gt, mt, vt):
        d, m2, v2 = _adamw_fn(wt, gt, mt, vt)
        return [d, m2, v2, gt], []
    return _rowwise(fn, [(w, W, 0), (g, W, g_cb, g_roff), (m, W, 0), (v, W, 0)], [], [],
                    [(W, F32)] * 4, [], n_ex=1, name=name, tr=256)[0]


def _scan_tiles(re_ref, im_ref, cf, lane0, n_chunks, reverse, extra=None):
    L = SCAN_LANES
    lanes = pl.ds(lane0, L)
    A = [cf[i, :, lanes] for i in range(8)]
    shifts = (7, 6, 4) if reverse else (1, 2, 4)
    edge = 0 if reverse else 7

    U = SCAN_UNROLL
    n_groups = n_chunks // U

    def body(c, carry):
        first = ((n_groups - 1 - c) if reverse else c) * U
        rows = pl.ds(pl.multiple_of(first * 8, 8 * U), 8 * U)
        big_r, big_i = re_ref[rows, lanes], im_ref[rows, lanes]
        tiles = []
        for u in range(U):
            xr, xi = big_r[8 * u:8 * u + 8, :], big_i[8 * u:8 * u + 8, :]
            for idx, sft in enumerate(shifts):
                ar, ai = A[2 * idx], A[2 * idx + 1]
                rr, ri = pltpu.roll(xr, sft, 0), pltpu.roll(xi, sft, 0)
                xr, xi = xr + ar * rr - ai * ri, xi + ar * ri + ai * rr
            tiles.append((xr, xi))
        pr, pi = A[6], A[7]
        cr, ci = carry[0], carry[1]
        for u in (range(U - 1, -1, -1) if reverse else range(U)):
            xr, xi = tiles[u]
            xr, xi = xr + pr * cr - pi * ci, xi + pr * ci + pi * cr
            tiles[u] = (xr, xi)
            cr, ci = jnp.broadcast_to(xr[edge:edge + 1, :], (8, L)), jnp.broadcast_to(xi[edge:edge + 1, :], (8, L))
        re_ref[rows, lanes] = jnp.concatenate([t[0] for t in tiles], axis=0)
        im_ref[rows, lanes] = jnp.concatenate([t[1] for t in tiles], axis=0)
        return (cr, ci) if extra is None else (cr, ci) + extra(first, tiles, carry[2:])

    assert n_chunks % U == 0
    z = jnp.zeros((8, L), F32)
    init = (z, z) if extra is None else (z, z, z, z)
    return lax.fori_loop(0, n_groups, body, init)


def _s5_consts(ab_re, ab_im):
    ng = ab_re.shape[0] // GROUPS_PER_STEP
    ar, ai = ab_re.reshape(ng, 1, ST_LANES), ab_im.reshape(ng, 1, ST_LANES)

    def cmul(xr, xi, yr, yi):
        return xr * yr - xi * yi, xr * yi + xi * yr

    def build(ar, ai, reverse):
        pw = [(ar, ai)]
        for _ in range(7):
            pw.append(cmul(*pw[-1], ar, ai))
        row = jnp.arange(8).reshape(1, 8, 1)
        tiles = []
        for k in (1, 2, 4):
            keep = (row <= 7 - k) if reverse else (row >= k)
            tiles += [jnp.where(keep, pw[k - 1][0], 0.0), jnp.where(keep, pw[k - 1][1], 0.0)]
        order = [7 - r for r in range(8)] if reverse else list(range(8))
        tiles += [jnp.concatenate([pw[o][0] for o in order], axis=1), jnp.concatenate([pw[o][1] for o in order], axis=1)]
        return jnp.stack([jnp.broadcast_to(t, (ng, 8, ST_LANES)) for t in tiles], axis=1)

    return build(ar, ai, False), build(ar, -ai, True)


def _s5_blockdiag(bb_re, bb_im, c_re, c_im):
    G = bb_re.shape[0]
    ng = G // GROUPS_PER_STEP
    eye = jnp.eye(GROUPS_PER_STEP, dtype=F32)

    def wb(bb):
        return jnp.einsum("bgph,gk->bghkp", bb.reshape(ng, GROUPS_PER_STEP, S5_STATE, S5_GROUP), eye).reshape(ng, U_LANES, ST_LANES)

    def wc(cc):
        return jnp.einsum("bghp,gk->bkpgh", cc.reshape(ng, GROUPS_PER_STEP, S5_GROUP, S5_STATE), eye).reshape(ng, ST_LANES, U_LANES)

    Wb = jnp.concatenate([wb(bb_re), wb(bb_im)], axis=2).astype(BF16)
    Wc = jnp.concatenate([wc(c_re), -wc(c_im)], axis=1).astype(BF16)
    return Wb, Wc


def _s5_unblock(dWb, dWc):
    ng = dWb.shape[0]
    eye = jnp.eye(GROUPS_PER_STEP, dtype=F32)

    def ub(w):
        return jnp.einsum("bghkp,gk->bgph", w.reshape(ng, GROUPS_PER_STEP, S5_GROUP, GROUPS_PER_STEP, S5_STATE), eye).reshape(-1, S5_STATE, S5_GROUP)

    def uc(w):
        return jnp.einsum("bkpgh,gk->bghp", w.reshape(ng, GROUPS_PER_STEP, S5_STATE, GROUPS_PER_STEP, S5_GROUP), eye).reshape(-1, S5_GROUP, S5_STATE)

    return ub(dWb[:, :, :ST_LANES]), ub(dWb[:, :, ST_LANES:]), uc(dWc[:, :ST_LANES, :]), -uc(dWc[:, ST_LANES:, :])


def _s5_disc(a_re, a_im, log_dt, b_re, b_im):
    dt = jnp.exp(log_dt)[:, None]
    mag = jnp.exp(a_re * dt)
    ab_re = mag * jnp.cos(a_im * dt)
    ab_im = mag * jnp.sin(a_im * dt)
    den = a_re * a_re + a_im * a_im
    nr, ni = ab_re - 1, ab_im
    f_re = (nr * a_re + ni * a_im) / den
    f_im = (ni * a_re - nr * a_im) / den
    bb_re = f_re[..., None] * b_re - f_im[..., None] * b_im
    bb_im = f_re[..., None] * b_im + f_im[..., None] * b_re
    return ab_re, ab_im, bb_re, bb_im


ROW_CHUNK = 512


def _gather_copies(src_ref, land_ref, sems):
    x, y, c = _place()
    chips = [(1 - x, y), (x, 1 - y), (1 - x, 1 - y)]
    sends = [pltpu.make_async_remote_copy(src_ref=src_ref.at[c], dst_ref=land_ref.at[2 * x + y, c], send_sem=sems[k], recv_sem=sems[3 + k],
                                          device_id=(cx, cy, c), device_id_type=MESH) for k, (cx, cy) in enumerate(chips)]
    recvs = [pltpu.make_async_remote_copy(src_ref=land_ref.at[2 * cx + cy, c], dst_ref=land_ref.at[2 * cx + cy, c], send_sem=sems[k],
                                          recv_sem=sems[3 + k], device_id=(cx, cy, c), device_id_type=MESH) for k, (cx, cy) in enumerate(chips)]
    return sends, recvs


def _gather_start(src, land, *, name):
    hbm, sem = pl.BlockSpec(memory_space=pltpu.HBM), pl.BlockSpec(memory_space=pltpu.SEMAPHORE)

    def body(src_ref, land_ref, *outs):
        sends, _ = _gather_copies(src_ref, land_ref, outs[:6])
        for cp in sends:
            cp.start()
        outs[8][...] = jnp.zeros_like(outs[8])

    outs = pl.pallas_call(
        body, name=name,
        out_shape=(pltpu.SemaphoreType.DMA(()),) * 6 + (pltpu.HBM(src.shape, src.dtype), pltpu.HBM(land.shape, land.dtype),
                                                         jax.ShapeDtypeStruct((8, 128), F32)),
        in_specs=(hbm, hbm), out_specs=(sem,) * 6 + (hbm, hbm, pl.BlockSpec(memory_space=pltpu.VMEM)),
        input_output_aliases={0: 6, 1: 7},
        compiler_params=pltpu.CompilerParams(has_side_effects=pltpu.SideEffectType.DATAFLOW_SIDE_EFFECTING),
    )(pltpu.with_memory_space_constraint(src, pltpu.HBM), pltpu.with_memory_space_constraint(land, pltpu.HBM))
    return outs[:6], outs[6], outs[7], outs[8]


def _s5_fwd(u, Wb, Wc, cf, d, inflight, *, n_ex, name):
    T, D = u.shape
    S = T // n_ex
    ng = D // U_LANES
    rc = min(ROW_CHUNK, S)
    sems, xsrc, xland = inflight

    def body(u_ref, wb_ref, wc_ref, cf_ref, d_ref, xsrc_ref, xland_ref, *rest):
        sem_refs, (y_ref, gy_ref, gyt_ref, st_ref, _, _, re_s, im_s) = rest[:6], rest[6:]
        step = pl.program_id(0) * ng + pl.program_id(1)

        for r in range(S // rc):
            rows = pl.ds(r * rc, rc)
            bu = jnp.dot(u_ref[rows, :].astype(BF16), wb_ref[...], preferred_element_type=F32)
            re_s[rows, :] = bu[:, :ST_LANES]
            im_s[rows, :] = bu[:, ST_LANES:]
        for l0 in range(0, ST_LANES, SCAN_LANES):
            _scan_tiles(re_s, im_s, cf_ref, l0, S // 8, False)
        for r in range(S // rc):
            rows = pl.ds(r * rc, rc)
            st = jnp.concatenate([re_s[rows, :], im_s[rows, :]], axis=1).astype(BF16)
            st_ref[rows, :] = st
            y = jnp.dot(st, wc_ref[...], preferred_element_type=F32) + d_ref[...] * u_ref[rows, :]
            y_ref[rows, :] = y
            gy = _gelu(y)
            gy_ref[rows, :] = gy.astype(BF16)
            gyt_ref[:, rows] = gy.T.astype(BF16)

        @pl.when(step == n_ex * ng - 1)
        def _():
            sends, recvs = _gather_copies(xsrc_ref, xland_ref, sem_refs)
            for cp in sends:
                cp.wait_send()
            for cp in recvs:
                cp.wait_recv()

    hbm, sem = pl.BlockSpec(memory_space=pltpu.HBM), pl.BlockSpec(memory_space=pltpu.SEMAPHORE)
    outs = pl.pallas_call(
        body, name=name, grid=(n_ex, ng),
        in_specs=[pl.BlockSpec((S, U_LANES), lambda e, g: (e, g)),
                  pl.BlockSpec((None, U_LANES, 2 * ST_LANES), lambda e, g: (g, 0, 0)),
                  pl.BlockSpec((None, 2 * ST_LANES, U_LANES), lambda e, g: (g, 0, 0)),
                  pl.BlockSpec((None, 8, 8, ST_LANES), lambda e, g: (g, 0, 0, 0)),
                  pl.BlockSpec((1, U_LANES), lambda e, g: (0, g)), hbm, hbm] + [sem] * 6,
        out_specs=[pl.BlockSpec((S, U_LANES), lambda e, g: (e, g))] * 2 + [pl.BlockSpec((U_LANES, S), lambda e, g: (g, e)),
                   pl.BlockSpec((S, 2 * ST_LANES), lambda e, g: (e, g)), hbm, hbm],
        out_shape=[jax.ShapeDtypeStruct((T, D), F32), jax.ShapeDtypeStruct((T, D), BF16), jax.ShapeDtypeStruct((D, T), BF16),
                   jax.ShapeDtypeStruct((T, ng * 2 * ST_LANES), BF16), pltpu.HBM(xsrc.shape, xsrc.dtype), pltpu.HBM(xland.shape, xland.dtype)],
        input_output_aliases={5: 4, 6: 5},
        scratch_shapes=[pltpu.VMEM((S, ST_LANES), F32)] * 2,
        compiler_params=pltpu.CompilerParams(dimension_semantics=("arbitrary", "arbitrary"), vmem_limit_bytes=VMEM_LIMIT,
                                             has_side_effects=pltpu.SideEffectType.DATAFLOW_SIDE_EFFECTING),
    )(u, Wb, Wc, cf, d, xsrc, xland, *sems)
    return outs[0], outs[1], outs[2], outs[3], outs[5]


def _s5_bwd(u, y, dgy, st, Wb, Wc, cr, d, xsrc, *, n_ex, name):
    T, D = u.shape
    S = T // n_ex
    ng = D // U_LANES
    rc = min(ROW_CHUNK, S)
    nch = S // 8
    grp = 8 * SCAN_UNROLL
    assert grp % 16 == 0

    def body(u_ref, y_ref, dgy_ref, st_ref, wb_ref, wc_ref, cr_ref, d_ref, xsrc_ref,
             du_ref, dwb_ref, dwc_ref, dab_ref, dd_ref, xout_ref, gr_s, gi_s, dy_s, *sems):
        e = pl.program_id(1)
        step = pl.program_id(0) * n_ex + e
        exch = _ChipExchange(xsrc_ref, xout_ref, *sems, scatter=True)

        @pl.when(step == 0)
        def _():
            exch.start()

        @pl.when(e == 0)
        def _():
            dwb_ref[...] = jnp.zeros_like(dwb_ref)
            dwc_ref[...] = jnp.zeros_like(dwc_ref)
            dab_ref[...] = jnp.zeros_like(dab_ref)
            dd_ref[...] = jnp.zeros_like(dd_ref)

        dd = jnp.zeros((1, U_LANES), F32)
        for r in range(S // rc):
            rows = pl.ds(r * rc, rc)
            ut = u_ref[rows, :]
            dy = dgy_ref[rows, :].astype(F32) * _gelu_grad(y_ref[rows, :])
            dy_s[rows, :] = dy
            dd = dd + _csum(dy * ut)
            go = lax.dot_general(dy.astype(BF16), wc_ref[...], (((1,), (1,)), ((), ())), preferred_element_type=F32)
            gr_s[rows, :] = go[:, :ST_LANES]
            gi_s[rows, :] = go[:, ST_LANES:]
        dd_ref[0:1, :] += dd
        row0 = lax.broadcasted_iota(jnp.int32, (8, SCAN_LANES), 0) == 0
        for l0 in range(0, ST_LANES, SCAN_LANES):
            lanes = pl.ds(l0, SCAN_LANES)

            def dab_group(first, tiles, acc, l0=l0):
                def states(r0, n, lane0):
                    return st_ref[pl.ds(pl.multiple_of(r0, 16), n), pl.ds(lane0, SCAN_LANES)].astype(F32)
                r0 = first * 8
                cur = states(r0, grp, l0), states(r0, grp, ST_LANES + l0)
                live = (first > 0).astype(F32)
                p0 = jnp.maximum(r0 - 16, 0)
                before = [states(p0, 16, l0)[8:16, :] * live, states(p0, 16, ST_LANES + l0)[8:16, :] * live]
                a_re, a_im = acc
                for t, (gr, gi) in enumerate(tiles):
                    here = [c[8 * t:8 * t + 8, :] for c in cur]
                    sr, si = [jnp.where(row0, pltpu.roll(b, 1, 0), pltpu.roll(h, 1, 0)) for b, h in zip(before, here)]
                    a_re, a_im = a_re + gr * sr + gi * si, a_im + gi * sr - gr * si
                    before = here
                return a_re, a_im

            res = _scan_tiles(gr_s, gi_s, cr_ref, l0, nch, True, extra=dab_group)
            dab_ref[0:1, lanes] += _csum(res[2])
            dab_ref[1:2, lanes] += _csum(res[3])
        for r in range(S // rc):
            rows = pl.ds(r * rc, rc)
            st = st_ref[rows, :]
            g = jnp.concatenate([gr_s[rows, :], gi_s[rows, :]], axis=1).astype(BF16)
            dyb = dy_s[rows, :].astype(BF16)
            dwc_ref[...] += lax.dot_general(st, dyb, (((0,), (0,)), ((), ())), preferred_element_type=F32)
            dwb_ref[...] += lax.dot_general(u_ref[rows, :].astype(BF16), g, (((0,), (0,)), ((), ())), preferred_element_type=F32)
            du = lax.dot_general(g, wb_ref[...], (((1,), (1,)), ((), ())), preferred_element_type=F32)
            du_ref[rows, :] = du + d_ref[...] * dy_s[rows, :]

        @pl.when(step == ng * n_ex - 1)
        def _():
            exch.wait()

    return pl.pallas_call(
        body, name=name, grid=(ng, n_ex),
        in_specs=[pl.BlockSpec((S, U_LANES), lambda g, e: (e, g))] * 3 + [
            pl.BlockSpec((S, 2 * ST_LANES), lambda g, e: (e, g)),
            pl.BlockSpec((None, U_LANES, 2 * ST_LANES), lambda g, e: (g, 0, 0)),
            pl.BlockSpec((None, 2 * ST_LANES, U_LANES), lambda g, e: (g, 0, 0)),
            pl.BlockSpec((None, 8, 8, ST_LANES), lambda g, e: (g, 0, 0, 0)),
            pl.BlockSpec((1, U_LANES), lambda g, e: (0, g)), ANY],
        out_specs=[pl.BlockSpec((S, U_LANES), lambda g, e: (e, g)),
                   pl.BlockSpec((None, U_LANES, 2 * ST_LANES), lambda g, e: (g, 0, 0)),
                   pl.BlockSpec((None, 2 * ST_LANES, U_LANES), lambda g, e: (g, 0, 0)),
                   pl.BlockSpec((None, 8, ST_LANES), lambda g, e: (g, 0, 0)),
                   pl.BlockSpec((None, 8, U_LANES), lambda g, e: (g, 0, 0)), ANY],
        out_shape=[jax.ShapeDtypeStruct((T, D), F32),
                   jax.ShapeDtypeStruct((ng, U_LANES, 2 * ST_LANES), F32),
                   jax.ShapeDtypeStruct((ng, 2 * ST_LANES, U_LANES), F32),
                   jax.ShapeDtypeStruct((ng, 8, ST_LANES), F32),
                   jax.ShapeDtypeStruct((ng, 8, U_LANES), F32), _ChipExchange.out_shape(xsrc, True)],
        scratch_shapes=[pltpu.VMEM((S, ST_LANES), F32)] * 2 + [pltpu.VMEM((S, U_LANES), F32)] + _ChipExchange.SCRATCH,
        compiler_params=_cp(("arbitrary", "arbitrary")),
    )(u, y, dgy, st, Wb, Wc, cr, d, xsrc)


TQ = 256
KW = 512
SUB = 128


def _head_masks():
    lane = lax.broadcasted_iota(jnp.int32, (1, 2 * HEAD_DIM), 1)
    m0 = (lane < HEAD_DIM).astype(F32)
    return m0, 1.0 - m0


def _head_norm(x, g, m0, m1):
    sq = x * x
    r0 = lax.rsqrt(jnp.sum(sq * m0, axis=-1, keepdims=True) / HEAD_DIM + EPS)
    r1 = lax.rsqrt(jnp.sum(sq * m1, axis=-1, keepdims=True) / HEAD_DIM + EPS)
    r = m0 * r0 + m1 * r1
    return x * r, r


def _head_norm_bwd(dy, n, r, g, m0, m1):
    dn = dy * g
    p = dn * n
    mean = (m0 * jnp.sum(p * m0, axis=-1, keepdims=True) + m1 * jnp.sum(p * m1, axis=-1, keepdims=True)) / HEAD_DIM
    return r * (dn - n * mean), _csum(dy * n)


def _pair_matrix(kind):
    r = lax.broadcasted_iota(jnp.int32, (2 * SUB, 2 * SUB), 0)
    c = lax.broadcasted_iota(jnp.int32, (2 * SUB, 2 * SUB), 1)
    same = (r < SUB) == (c < SUB)
    rel = {"after": r > c, "upto": r <= c, "before": r < c}[kind]
    return jnp.logical_and(same, rel).astype(BF16)


def _block_sums(x, mat, carry, reverse, terms=2):
    hi = x.astype(BF16)
    lo = (x - hi.astype(F32)).astype(BF16) if terms == 2 else None
    npair = x.shape[1] // (2 * SUB)
    parts = [None] * (2 * npair)
    for p in (range(npair - 1, -1, -1) if reverse else range(npair)):
        sl = slice(2 * SUB * p, 2 * SUB * (p + 1))
        loc = jnp.dot(hi[:, sl], mat, preferred_element_type=F32)
        if terms == 2:
            loc = loc + jnp.dot(lo[:, sl], mat, preferred_element_type=F32)
        for b in ((1, 0) if reverse else (0, 1)):
            k = 2 * p + b
            parts[k] = loc[:, SUB * b:SUB * (b + 1)] + carry
            carry = carry + jnp.sum(x[:, SUB * k:SUB * (k + 1)], axis=-1, keepdims=True)
    return jnp.concatenate(parts, axis=1), carry


def _sb_logits(z, mask):
    lp = jnp.minimum(z, 0.0) - jnp.log(1.0 + jnp.exp(-jnp.abs(z)))
    lf = lp - z
    if mask is not None:
        lf = jnp.where(mask, lf, 0.0)
    return lp, lf


def _causal_mask(row0, col0, kw):
    r = row0 + lax.broadcasted_iota(jnp.int32, (TQ, kw), 0)
    c = col0 + lax.broadcasted_iota(jnp.int32, (TQ, kw), 1)
    return c < r


def _transposed_windows(x, ref):
    for w in range(x.shape[0] // KW):
        ref[w] = x[w * KW:(w + 1) * KW, :].T.astype(BF16)


def _attn_fwd(q, kv, qg, kg, xsrc, *, n_ex, name):
    T, D = q.shape
    S = T // n_ex
    nhp = D // (2 * HEAD_DIM)
    nq = S // TQ
    scale = 1.0 / math.sqrt(HEAD_DIM)

    def body(q_ref, k_ref, v_ref, qg_ref, kg_ref, xsrc_ref, o_ref, tot_ref, ot_ref, xout_ref, kT_s, qm_s, vm_s, *sems):
        step = pl.program_id(0) * nhp + pl.program_id(1)
        exch = _ChipExchange(xsrc_ref, xout_ref, *sems, scatter=False)

        @pl.when(step == 0)
        def _():
            exch.start()

        m0, m1 = _head_masks()
        qn, _ = _head_norm(q_ref[...], None, m0, m1)
        qn = qn * (qg_ref[...] * scale)
        kn, _ = _head_norm(k_ref[...], None, m0, m1)
        _transposed_windows(kn * kg_ref[...], kT_s)
        v = v_ref[...]
        for h, m in enumerate((m0, m1)):
            qm_s[h] = (qn * m).astype(BF16)
            vm_s[h] = (v * m).astype(BF16)
        u_after = _pair_matrix("after")

        def window(rows, win, st, mask, kw):
            keys = pl.ds(pl.multiple_of(win * KW, KW), kw)
            zs = [jnp.dot(qm_s[h, rows, :], kT_s[win, :, :kw], preferred_element_type=F32) for h in range(2)]
            lg = [_sb_logits(zs[h], mask) for h in range(2)]
            sums = [_block_sums(lg[h][1], u_after, st[2 * h], True) for h in range(2)]
            out = ()
            for h in range(2):
                w = jnp.exp(lg[h][0] + sums[h][0])
                if mask is not None:
                    w = jnp.where(mask, w, 0.0)
                out += (sums[h][1], st[2 * h + 1] + jnp.dot(w.astype(BF16), vm_s[h, keys, :], preferred_element_type=F32))
            return out

        def qtile(iq, last, kw):
            rows = pl.ds(pl.multiple_of(iq * TQ, TQ), TQ)
            mask = _causal_mask(iq * TQ, last * KW, kw)
            z1, zq = jnp.zeros((TQ, 1), F32), jnp.zeros((TQ, 2 * HEAD_DIM), F32)
            st = window(rows, last, (z1, zq, z1, zq), mask, kw)
            st = lax.fori_loop(0, last, lambda jj, st: window(rows, last - 1 - jj, st, None, KW), st)
            o_ref[rows, :] = st[1] + st[3]
            tot_ref[rows, :] = st[0] * m0 + st[2] * m1

        def qtiles_of_window(a, _):
            for sub in range(KW // TQ):
                qtile(a * (KW // TQ) + sub, a, (sub + 1) * TQ)
            return 0

        lax.fori_loop(0, S // KW, qtiles_of_window, 0)
        ot_ref[...] = o_ref[...].T.astype(BF16)

        @pl.when(step == n_ex * nhp - 1)
        def _():
            exch.wait()

    assert S % KW == 0 and KW % TQ == 0
    nwin = S // KW
    blk = (S, 2 * HEAD_DIM)
    return pl.pallas_call(
        body, name=name, grid=(n_ex, nhp),
        in_specs=[pl.BlockSpec(blk, lambda e, h: (e, h)), pl.BlockSpec(blk, lambda e, h: (e, h)),
                  pl.BlockSpec(blk, lambda e, h: (e, h + nhp)),
                  pl.BlockSpec((1, 2 * HEAD_DIM), lambda e, h: (0, 0)), pl.BlockSpec((1, 2 * HEAD_DIM), lambda e, h: (0, 0)), ANY],
        out_specs=[pl.BlockSpec(blk, lambda e, h: (e, h))] * 2 + [pl.BlockSpec((2 * HEAD_DIM, S), lambda e, h: (h, e)), ANY],
        out_shape=[jax.ShapeDtypeStruct((T, D), F32)] * 2 + [jax.ShapeDtypeStruct((D, T), BF16), _ChipExchange.out_shape(xsrc, False)],
        scratch_shapes=[pltpu.VMEM((nwin, 2 * HEAD_DIM, KW), BF16), pltpu.VMEM((2,) + blk, BF16), pltpu.VMEM((2,) + blk, BF16)]
        + _ChipExchange.SCRATCH,
        compiler_params=_cp(("arbitrary", "arbitrary")),
    )(q, kv, kv, qg, kg, xsrc)


def _attn_bwd(q, kv, tot, do, qg, kg, *, n_ex, name):
    T, D = q.shape
    S = T // n_ex
    nhp = D // (2 * HEAD_DIM)
    nq = S // TQ
    scale = 1.0 / math.sqrt(HEAD_DIM)

    def body(q_ref, k_ref, v_ref, tot_ref, do_ref, qg_ref, kg_ref, dq_ref, dk_ref, dv_ref, dqg_ref, dkg_ref,
             kT_s, vT_s, km_s, qm_s, dom_s, dqn_s, dkT_s, dvT_s):
        m0, m1 = _head_masks()
        qn, qr = _head_norm(q_ref[...], None, m0, m1)
        kn, kr = _head_norm(k_ref[...], None, m0, m1)
        qs = qn * (qg_ref[...] * scale)
        kk = kn * kg_ref[...]
        _transposed_windows(kk, kT_s)
        _transposed_windows(v_ref[...], vT_s)
        do = do_ref[...]
        for h, m in enumerate((m0, m1)):
            qm_s[h] = (qs * m).astype(BF16)
            km_s[h] = (kk * m).astype(BF16)
            dom_s[h] = (do * m).astype(BF16)
        dkT_s[...] = jnp.zeros_like(dkT_s)
        dvT_s[...] = jnp.zeros_like(dvT_s)
        u_upto, u_before = _pair_matrix("upto"), _pair_matrix("before")

        def both(inv, win, st, mask, kw):
            keys = pl.ds(pl.multiple_of(win * KW, KW), kw)
            lg = [_sb_logits(jnp.dot(inv[h][0], kT_s[win, :, :kw], preferred_element_type=F32), mask) for h in range(2)]
            s_lf = [_block_sums(lg[h][1], u_upto, st[3 * h], False) for h in range(2)]
            ws, ews = [], []
            for h in range(2):
                w = jnp.exp(lg[h][0] - s_lf[h][0])
                if mask is not None:
                    w = jnp.where(mask, w, 0.0)
                ws.append(w)
                ews.append(jnp.dot(inv[h][2], vT_s[win, :, :kw], preferred_element_type=F32) * w)
            s_e = [_block_sums(ews[h], u_before, st[3 * h + 1], False, terms=1) for h in range(2)]
            out, dk, dv = (), None, None
            for h in range(2):
                sig = jnp.exp(lg[h][0])
                dz = ews[h] - sig * (ews[h] + s_e[h][0])
                if mask is not None:
                    dz = jnp.where(mask, dz, 0.0)
                dzb = dz.astype(BF16)
                out += (s_lf[h][1], s_e[h][1], st[3 * h + 2] + jnp.dot(dzb, km_s[h, keys, :], preferred_element_type=F32))
                dkh = jnp.dot(inv[h][1], dzb, preferred_element_type=F32)
                dvh = jnp.dot(inv[h][3], ws[h].astype(BF16), preferred_element_type=F32)
                dk, dv = (dkh, dvh) if h == 0 else (dk + dkh, dv + dvh)
            dkT_s[win, :, :kw] += dk
            dvT_s[win, :, :kw] += dv
            return out

        def qtile(iq, last, kw):
            rows = pl.ds(pl.multiple_of(iq * TQ, TQ), TQ)
            mask = _causal_mask(iq * TQ, last * KW, kw)
            tt = tot_ref[rows, :]
            inv, neg_total = [], []
            for h, m in enumerate((m0, m1)):
                qh, doh = qm_s[h, rows, :], dom_s[h, rows, :]
                neg_total.append(jnp.sum(tt * m, axis=-1, keepdims=True) * (-1.0 / HEAD_DIM))
                inv.append((qh, qh.astype(F32).T.astype(BF16), doh, doh.astype(F32).T.astype(BF16)))

            z1, zq = jnp.zeros((TQ, 1), F32), jnp.zeros((TQ, 2 * HEAD_DIM), F32)
            st = lax.fori_loop(0, last, lambda win, st: both(inv, win, st, None, KW), (neg_total[0], z1, zq, neg_total[1], z1, zq))
            st = both(inv, last, st, mask, kw)
            dqn_s[rows, :] = st[2] + st[5]

        def qtiles_of_window(a, _):
            for sub in range(KW // TQ):
                qtile(a * (KW // TQ) + sub, a, (sub + 1) * TQ)
            return 0

        lax.fori_loop(0, S // KW, qtiles_of_window, 0)
        dkn = jnp.concatenate([dkT_s[w].T for w in range(nwin)], axis=0)
        dq, dqg = _head_norm_bwd(dqn_s[...] * scale, qn, qr, qg_ref[...], m0, m1)
        dk, dkg = _head_norm_bwd(dkn, kn, kr, kg_ref[...], m0, m1)
        dq_ref[...] = dq
        dk_ref[...] = dk
        dv_ref[...] = jnp.concatenate([dvT_s[w].T for w in range(nwin)], axis=0)
        dqg_ref[...] = dqg
        dkg_ref[...] = dkg

    assert S % KW == 0 and KW % TQ == 0
    nwin = S // KW
    blk = (S, 2 * HEAD_DIM)
    tblk = (nwin, 2 * HEAD_DIM, KW)
    gblk = (None, None, 1, 2 * HEAD_DIM)
    dq, dk, dv, dqg, dkg = pl.pallas_call(
        body, name=name, grid=(n_ex, nhp),
        in_specs=[pl.BlockSpec(blk, lambda e, h: (e, h)), pl.BlockSpec(blk, lambda e, h: (e, h)),
                  pl.BlockSpec(blk, lambda e, h: (e, h + nhp)),
                  pl.BlockSpec(blk, lambda e, h: (e, h)), pl.BlockSpec(blk, lambda e, h: (e, h)),
                  pl.BlockSpec((1, 2 * HEAD_DIM), lambda e, h: (0, 0)), pl.BlockSpec((1, 2 * HEAD_DIM), lambda e, h: (0, 0))],
        out_specs=[pl.BlockSpec(blk, lambda e, h: (e, h))] * 3 + [pl.BlockSpec(gblk, lambda e, h: (e, h, 0, 0))] * 2,
        out_shape=[jax.ShapeDtypeStruct((T, D), F32)] * 3 + [jax.ShapeDtypeStruct((n_ex, nhp, 1, 2 * HEAD_DIM), F32)] * 2,
        scratch_shapes=[pltpu.VMEM(tblk, BF16), pltpu.VMEM(tblk, BF16),
                        pltpu.VMEM((2,) + blk, BF16), pltpu.VMEM((2,) + blk, BF16), pltpu.VMEM((2,) + blk, BF16),
                        pltpu.VMEM(blk, F32), pltpu.VMEM(tblk, F32), pltpu.VMEM(tblk, F32)],
        compiler_params=_cp(("parallel", "parallel")),
    )(q, kv, kv, tot, do, qg, kg)
    return dq, dk, dv, dqg, dkg


def _place():
    return lax.axis_index("x"), lax.axis_index("y"), lax.axis_index("c")


def _all_gather8(x_shard, *, name):
    m_per, n = x_shard.shape

    def body(x_ref, out_ref, send_sems, recv_sems, local_sem):
        x, y, c = _place()
        me, sibling = (x, y, c), (x, y, 1 - c)
        chips = [(1 - x, y), (x, 1 - y), (1 - x, 1 - y)]

        def rows(px, py, pc):
            return out_ref.at[pl.ds((4 * px + 2 * py + pc) * m_per, m_per), :]

        def copy(k, block, to, src=None):
            return pltpu.make_async_remote_copy(
                src_ref=rows(*block) if src is None else src, dst_ref=rows(*block),
                send_sem=send_sems.at[k], recv_sem=recv_sems.at[k], device_id=to, device_id_type=MESH)

        mine = pltpu.make_async_copy(x_ref, rows(*me), local_sem)
        mine.start()
        first = [copy(0, me, sibling, src=x_ref)]
        first += [copy(1 + j, me, (*chip, c), src=x_ref) for j, chip in enumerate(chips)]
        for cp in first:
            cp.start()
        passed = [copy(4 + j, (*chip, c), sibling) for j, chip in enumerate(chips)]
        for j, chip in enumerate(chips):
            copy(1 + j, (*chip, c), me).wait_recv()
            passed[j].start()
        copy(0, sibling, me).wait_recv()
        for j, chip in enumerate(chips):
            copy(4 + j, (*chip, 1 - c), me).wait_recv()
        for cp in first + passed:
            cp.wait_send()
        mine.wait()

    return pl.pallas_call(
        body, name=name, out_shape=jax.ShapeDtypeStruct((8 * m_per, n), x_shard.dtype),
        in_specs=[pl.BlockSpec(memory_space=pltpu.VMEM)], out_specs=pl.BlockSpec(memory_space=pltpu.VMEM),
        scratch_shapes=[pltpu.SemaphoreType.DMA((7,)), pltpu.SemaphoreType.DMA((7,)), pltpu.SemaphoreType.DMA],
        compiler_params=pltpu.CompilerParams(vmem_limit_bytes=VMEM_LIMIT),
    )(x_shard)


def _sibling_sum_half(x, *, name):
    R, C = x.shape
    half = R // 2

    def body(x_ref, o_ref, theirs, send_sem, recv_sem):
        px, py, pc = _place()
        cp = pltpu.make_async_remote_copy(src_ref=x_ref, dst_ref=theirs, send_sem=send_sem, recv_sem=recv_sem,
                                          device_id=(px, py, 1 - pc), device_id_type=MESH)
        cp.start()
        cp.wait()
        rows = pl.ds(pl.multiple_of(pc * half, 8), half)
        o_ref[...] = x_ref[rows, :] + theirs[rows, :]

    return pl.pallas_call(
        body, name=name, out_shape=jax.ShapeDtypeStruct((half, C), x.dtype),
        in_specs=[pl.BlockSpec(memory_space=pltpu.VMEM)], out_specs=pl.BlockSpec(memory_space=pltpu.VMEM),
        scratch_shapes=[pltpu.VMEM((R, C), x.dtype), pltpu.SemaphoreType.DMA, pltpu.SemaphoreType.DMA],
        compiler_params=pltpu.CompilerParams(vmem_limit_bytes=VMEM_LIMIT),
    )(x)


def _sum_blocks(x, n, *, name):
    R = x.shape[0] // n

    def body(x_ref, o_ref):
        acc = x_ref[pl.ds(0, R), :]
        for k in range(1, n):
            acc = acc + x_ref[pl.ds(k * R, R), :]
        o_ref[...] = acc

    return pl.pallas_call(body, name=name, out_shape=jax.ShapeDtypeStruct((R, x.shape[1]), x.dtype),
                          compiler_params=pltpu.CompilerParams(vmem_limit_bytes=VMEM_LIMIT))(x)


def _colsum(x, *, name):
    def body(x_ref, o_ref):
        o_ref[...] = jnp.sum(x_ref[...], axis=0, keepdims=True)
    return pl.pallas_call(body, name=name, out_shape=jax.ShapeDtypeStruct((1, x.shape[1]), x.dtype))(x)


ANY = pl.BlockSpec(memory_space=pl.ANY)


class _ChipExchange:
    SCRATCH = [pltpu.SemaphoreType.DMA((3,)), pltpu.SemaphoreType.DMA((3,)), pltpu.SemaphoreType.DMA]

    @staticmethod
    def out_shape(src, scatter):
        return jax.ShapeDtypeStruct(((4,) + tuple(src.shape[1:])) if scatter else ((4, 2) + tuple(src.shape[1:])), src.dtype)

    def __init__(self, src_ref, out_ref, send_sems, recv_sems, local_sem, scatter):
        x, y, c = _place()
        myj = 2 * x + y
        chips = [(1 - x, y), (x, 1 - y), (1 - x, 1 - y)]

        def slot(j):
            return out_ref.at[j] if scatter else out_ref.at[j, c]

        def piece(j):
            return src_ref.at[j] if scatter else src_ref.at[c]

        self.mine = pltpu.make_async_copy(piece(myj), slot(myj), local_sem)
        self.sends = [pltpu.make_async_remote_copy(
            src_ref=piece(2 * cx + cy), dst_ref=slot(myj), send_sem=send_sems.at[k], recv_sem=recv_sems.at[k],
            device_id=(cx, cy, c), device_id_type=MESH) for k, (cx, cy) in enumerate(chips)]
        self.recvs = [pltpu.make_async_remote_copy(
            src_ref=slot(2 * cx + cy), dst_ref=slot(2 * cx + cy), send_sem=send_sems.at[k], recv_sem=recv_sems.at[k],
            device_id=(cx, cy, c), device_id_type=MESH) for k, (cx, cy) in enumerate(chips)]

    def start(self):
        self.mine.start()
        for cp in self.sends:
            cp.start()

    def wait(self):
        for cp in self.recvs:
            cp.wait_recv()
        for cp in self.sends:
            cp.wait_send()
        self.mine.wait()


def _sibling_fill(buf, *, axis, name):
    def half(ref, h):
        return ref.at[h] if axis == 0 else ref.at[:, h]

    def body(in_ref, out_ref, send_sem, recv_sem):
        x, y, c = _place()
        cp = pltpu.make_async_remote_copy(src_ref=half(out_ref, c), dst_ref=half(out_ref, c), send_sem=send_sem, recv_sem=recv_sem,
                                          device_id=(x, y, 1 - c), device_id_type=MESH)
        cp.start()
        pltpu.make_async_remote_copy(src_ref=half(out_ref, 1 - c), dst_ref=half(out_ref, 1 - c), send_sem=send_sem, recv_sem=recv_sem,
                                     device_id=(x, y, 1 - c), device_id_type=MESH).wait_recv()
        cp.wait_send()

    return pl.pallas_call(
        body, name=name, out_shape=jax.ShapeDtypeStruct(buf.shape, buf.dtype), in_specs=[ANY], out_specs=ANY,
        input_output_aliases={0: 0}, scratch_shapes=[pltpu.SemaphoreType.DMA, pltpu.SemaphoreType.DMA],
    )(buf)


def _sibling_swap_half(g, *, name):
    def body(g_ref, out_ref, send_sem, recv_sem):
        x, y, c = _place()
        cp = pltpu.make_async_remote_copy(src_ref=g_ref.at[:, 1 - c], dst_ref=out_ref, send_sem=send_sem, recv_sem=recv_sem,
                                          device_id=(x, y, 1 - c), device_id_type=MESH)
        cp.start()
        cp.wait()

    return pl.pallas_call(
        body, name=name, out_shape=jax.ShapeDtypeStruct((g.shape[0],) + g.shape[2:], g.dtype), in_specs=[ANY], out_specs=ANY,
        scratch_shapes=[pltpu.SemaphoreType.DMA, pltpu.SemaphoreType.DMA],
    )(g)


def _add_my_half(g, b, cidx, *, name, tr=1024):
    n, _, R, C = g.shape
    tr = max(t for t in range(16, tr + 1, 16) if R % t == 0)

    def body(c_ref, g_ref, b_ref, o_ref):
        o_ref[...] = (g_ref[...] + b_ref[...]).astype(o_ref.dtype)

    return pl.pallas_call(
        body, name=name, out_shape=jax.ShapeDtypeStruct((n, R, C), BF16),
        grid_spec=pltpu.PrefetchScalarGridSpec(
            num_scalar_prefetch=1, grid=(n, R // tr),
            in_specs=[pl.BlockSpec((None, None, tr, C), lambda j, i, c: (j, c[0], i, 0)),
                      pl.BlockSpec((None, tr, C), lambda j, i, c: (j, i, 0))],
            out_specs=pl.BlockSpec((None, tr, C), lambda j, i, c: (j, i, 0))),
        compiler_params=_cp(("parallel", "parallel")),
    )(cidx, g, b)


def _sum4_into_half(q, cidx, *, name, tr=1024):
    _, R, C = q.shape
    tr = max(t for t in range(16, tr + 1, 16) if R % t == 0)

    def body(c_ref, q_ref, o_ref):
        o_ref[...] = ((q_ref[0].astype(F32) + q_ref[1].astype(F32)) + q_ref[2].astype(F32)) + q_ref[3].astype(F32)

    return pl.pallas_call(
        body, name=name, out_shape=jax.ShapeDtypeStruct((2, R, C), F32),
        grid_spec=pltpu.PrefetchScalarGridSpec(
            num_scalar_prefetch=1, grid=(R // tr,),
            in_specs=[pl.BlockSpec((4, tr, C), lambda i, c: (0, i, 0))],
            out_specs=pl.BlockSpec((None, tr, C), lambda i, c: (c[0], i, 0))),
        compiler_params=_cp(("parallel",)),
    )(cidx, q)


def _pack_rows(parts, width=1024):
    rows, spans, r0 = [], [], 0
    for p in parts:
        n = p.size
        nr = 8 * (-(-n // (8 * width)))
        flat = p.reshape(-1)
        if nr * width != n:
            flat = jnp.pad(flat, (0, nr * width - n))
        rows.append(flat.reshape(nr, width))
        spans.append((r0, nr, n, p.shape))
        r0 += nr
    return jnp.concatenate(rows, axis=0), spans


def _unpack_rows(buf, spans):
    return [buf[r0:r0 + nr].reshape(-1)[:n].reshape(shape) for (r0, nr, n, shape) in spans]


def kernel(x, c, ada_w, ada_b, mix_norm_g, mlp_norm_g, mlp_w1, mlp_w2, s5_a_re, s5_a_im, s5_log_dt, s5_b_re, s5_b_im, s5_c_re, s5_c_im, s5_d, s5_w_glu, kv_ada_w, kv_ada_b, kv_norm_g, w_kv, k_norm_g, sb_w_q, q_norm_g, sb_w_o, loss_target, m_ada_w, m_ada_b, m_mix_norm_g, m_mlp_norm_g, m_mlp_w1, m_mlp_w2, m_s5_a_re, m_s5_a_im, m_s5_log_dt, m_s5_b_re, m_s5_b_im, m_s5_c_re, m_s5_c_im, m_s5_d, m_s5_w_glu, m_kv_ada_w, m_kv_ada_b, m_kv_norm_g, m_w_kv, m_k_norm_g, m_sb_w_q, m_q_norm_g, m_sb_w_o, v_ada_w, v_ada_b, v_mix_norm_g, v_mlp_norm_g, v_mlp_w1, v_mlp_w2, v_s5_a_re, v_s5_a_im, v_s5_log_dt, v_s5_b_re, v_s5_b_im, v_s5_c_re, v_s5_c_im, v_s5_d, v_s5_w_glu, v_kv_ada_w, v_kv_ada_b, v_kv_norm_g, v_w_kv, v_k_norm_g, v_sb_w_q, v_q_norm_g, v_sb_w_o):
    E, S, D = x.shape
    T = E * S
    FF = 4 * D
    NB = 8 * E
    px, py, pc = _place()
    chip = 2 * px + py
    dev = 4 * px + 2 * py + pc
    cidx = jnp.reshape(pc, (1,)).astype(jnp.int32)
    x0 = x.reshape(T, D)
    tgt = loss_target.reshape(T, D)

    wpack_a = jnp.concatenate([mlp_w1[0], mlp_w2[0], jnp.concatenate([s5_w_glu[0], w_kv], axis=1), sb_w_q[0]], axis=0).astype(BF16)
    RA = wpack_a.shape[0]
    halves_a = wpack_a.reshape(2, RA // 2, D)
    land_a = lax.dynamic_update_slice(lax.empty((4, 2, RA // 2, D), BF16),
                                      lax.dynamic_index_in_dim(halves_a, pc, 0, keepdims=True)[None], (chip, pc, 0, 0))
    sems_a, src_a, land_a, started = _gather_start(halves_a, land_a, name="wgather_a_start")

    nc_rows, nd = c.size // 128, s5_d.size // 128
    cd = jnp.concatenate([c.reshape(nc_rows, 128) + started[0, 0],
                          jnp.pad(s5_d.reshape(nd, 128), ((0, 8 - nd), (0, 0)))], axis=0)
    cd_all = _all_gather8(cd, name="ag_c_d").reshape(8, nc_rows + 8, 128)
    c_all = cd_all[:, :nc_rows].reshape(NB, D)
    d_full = cd_all.reshape(4, 2, nc_rows + 8, 128)[:, 0, nc_rows:nc_rows + nd].reshape(1, D)
    sc_all = (c_all * _sigmoid(c_all)).astype(BF16)
    wa = ada_w.shape[2]
    wk = kv_ada_w.shape[1]
    m_sh = jnp.concatenate([_mm(sc_all, _Layer(ada_w, 0), "nn", name="ada0", tn=256),
                            _mm(sc_all, _Layer(ada_w, 1), "nn", name="ada1", tn=256),
                            _mm(sc_all, kv_ada_w, "nn", name="ada_kv", tn=256)], axis=1)
    m_all = _all_gather8(m_sh, name="ag_m").reshape(4, 2, NB, 2 * wa + wk)[:, 0]
    mods = []
    for l in range(2):
        full = jnp.transpose(m_all[:, :, l * wa:(l + 1) * wa], (1, 0, 2)).reshape(NB, 6 * D) + ada_b[l]
        mine = lax.dynamic_slice_in_dim(full, E * dev, E, axis=0)
        mods.append([mine[:, i * D:(i + 1) * D].reshape(E, 1, D) for i in range(6)])
    full = jnp.transpose(m_all[:, :, 2 * wa:], (1, 0, 2)).reshape(NB, 2 * D) + kv_ada_b
    mine = lax.dynamic_slice_in_dim(full, E * dev, E, axis=0)
    kv_sh, kv_sc = [mine[:, i * D:(i + 1) * D].reshape(E, 1, D) for i in range(2)]

    wpack_b = jnp.concatenate([mlp_w1[1], mlp_w2[1], sb_w_o[0]], axis=0).astype(BF16)
    RB = wpack_b.shape[0]
    RW = RA + RB

    tm = min(2048, S)
    tm_res = min(1024, S)
    gbuf = [jax.ShapeDtypeStruct((4, RW, D), F32)]

    def grad_mm(act, dout, kind, roff, nr, c0, nc, name, transposed=False):
        gbuf[0] = _mm(act, dout, "nn" if transposed else "tn", name=name, tm=1024, tk=2048,
                      into=_Sharded(gbuf[0], kind, roff, nr, c0, nc))

    def mlp_fwd(xa, l, mod):
        sh_m, sc_m, g_m = mod[3], mod[4], mod[5]
        h, h_t = _norm_mod_fwd(xa, mlp_norm_g[l:l + 1], sh_m, sc_m, n_ex=E, out_dtype=BF16, name=f"mlp_norm{l}", with_transpose=True)

        def relu_sq(acc):
            ra = jnp.maximum(acc, 0.0)
            return ra * ra, ra
        r, ra = _mm(h, W1[l], "nn", name=f"mlp_up{l}", out_dtypes=(BF16, BF16), tm=tm, epilogue=relu_sq)
        xb, ff = _mm(r, W2[l], "nn", name=f"mlp_down{l}", out_dtypes=(F32, F32), tm=tm_res,
                     extras=[_mn_extra(xa), _vec_extra(g_m, S)],
                     epilogue=lambda acc, xat, gt: (xat + gt * acc, acc))
        return xb, (h_t, r, ra, ff)

    def mlp_bwd(dxb, xa, l, mod, saved):
        sc_m, g_m = mod[4], mod[5]
        h_t, r, ra, ff = saved
        (dff,), (dgm,) = _rowwise(lambda d, f, g: ([g * d], [_csum(d * f)]), [(dxb, D, 0), (ff, D, 0)], [g_m], [],
                                  [(D, BF16)], [D], n_ex=E, name=f"mlp_gate_bwd{l}")
        da = _mm(dff, W2[l], "nt", name=f"mlp_down_dx{l}", out_dtypes=(BF16,), tm=tm, extras=[_mn_extra(ra)],
                 epilogue=lambda acc, rat: (acc * (2.0 * rat.astype(F32)),))
        grad_mm(r, dff, "rows", (2 + l) * D, D, 0, D, f"mlp_down_dw{l}")
        dh = _mm(da, W1[l], "nt", name=f"mlp_up_dx{l}", tm=tm)
        grad_mm(h_t, da, "cols", l * D, D, 0, D, f"mlp_up_dw{l}", transposed=True)
        (dxa,), (dsh, dsc, dg) = _norm_mod_bwd(xa, dh, dxb, mlp_norm_g[l:l + 1], sc_m, n_ex=E, name=f"mlp_norm_bwd{l}")
        return dxa, (dsh, dsc, dgm), dg

    ab_re, ab_im, bb_re, bb_im = _s5_disc(s5_a_re[0], s5_a_im[0], s5_log_dt[0], s5_b_re[0], s5_b_im[0])
    cf, cr = _s5_consts(ab_re, ab_im)
    Wb, Wc = _s5_blockdiag(bb_re, bb_im, s5_c_re[0], s5_c_im[0])
    ng = D // U_LANES

    mod0, mod1 = mods
    h0 = _norm_mod_fwd(x0, mix_norm_g[0:1], mod0[0], mod0[1], n_ex=E, out_dtype=F32, name="mix_norm0")
    y, gy, gy_t, s5_states, wfull_a = _s5_fwd(h0, Wb, Wc, cf, d_full, (sems_a, src_a, land_a), n_ex=E, name="s5_fwd")
    wfull_a = _sibling_fill(wfull_a, axis=1, name="wgather_a_d2d").reshape(4, RA, D)

    W1 = [_Sharded(wfull_a, "cols", 0, D, 0, D), None]
    W2 = [_Sharded(wfull_a, "rows", D, D, 0, D), None]
    Wglu = _Sharded(wfull_a, "cols", 2 * D, D, 0, D // 2)
    Wkv = _Sharded(wfull_a, "cols", 2 * D, D, D // 2, D // 2)
    Wq = _Sharded(wfull_a, "rows", 3 * D, D // 4, 0, D)
    vg = _mm(gy, Wglu, "nn", name="glu_up", tm=tm)
    (x1,), _ = _rowwise(lambda v, g, xt, ga: ([xt + ga * (v * _sigmoid(g))], []),
                        [(vg, D, 0), (vg, D, 1), (x0, D, 0)], [mod0[2]], [], [(D, F32)], [], n_ex=E, name="glu_gate")
    x2, saved_mlp0 = mlp_fwd(x1, 0, mod0)

    hkv, hkv_t = _norm_mod_fwd(x2, kv_norm_g.reshape(1, D), kv_sh, kv_sc, n_ex=E, out_dtype=BF16, name="kv_norm", with_transpose=True)
    kvf = _mm(hkv, Wkv, "nn", name="kv_proj", tm=tm)
    h1, h1_t = _norm_mod_fwd(x2, mix_norm_g[1:2], mod1[0], mod1[1], n_ex=E, out_dtype=BF16, name="mix_norm1", with_transpose=True)
    qf = _mm(h1, Wq, "nn", name="q_proj", tm=tm)
    qg2 = jnp.tile(q_norm_g.reshape(1, HEAD_DIM), (1, 2))
    kg2 = jnp.tile(k_norm_g.reshape(1, HEAD_DIM), (1, 2))
    o, lf_tot, o_t, wfull_b = _attn_fwd(qf, kvf, qg2, kg2, wpack_b.reshape(2, RB // 2, D), n_ex=E, name="attn_fwd")
    wfull_b = _sibling_fill(wfull_b, axis=1, name="wgather_b_d2d").reshape(4, RB, D)
    W1[1] = _Sharded(wfull_b, "cols", 0, D, 0, D)
    W2[1] = _Sharded(wfull_b, "rows", D, D, 0, D)
    Wo = _Sharded(wfull_b, "rows", 2 * D, D // 4, 0, D)
    x3, mix1 = _mm(o, Wo, "nn", name="o_proj", out_dtypes=(F32, F32), tm=tm_res,
                   extras=[_mn_extra(x2), _vec_extra(mod1[2], S)],
                   epilogue=lambda acc, xat, gt: (xat + gt * acc, acc))
    x4, saved_mlp1 = mlp_fwd(x3, 1, mod1)

    (dx4,), (lsum,) = _rowwise(lambda xt, tt: ([(xt - tt) * (1.0 / D)], [_csum(jnp.square(xt - tt)) * (0.5 / D)]),
                               [(x4, D, 0), (tgt, D, 0)], [], [], [(D, F32)], [D], n_ex=E, name="loss")
    loss = lax.psum(jnp.sum(lsum), ("x", "y", "c"))

    dx3, (dsh_m1, dsc_m1, dgm1), dg_mlp1 = mlp_bwd(dx4, x3, 1, mod1, saved_mlp1)
    (dmix1,), (dga1,) = _rowwise(lambda d, f, g: ([g * d], [_csum(d * f)]), [(dx3, D, 0), (mix1, D, 0)], [mod1[2]], [],
                                 [(D, BF16)], [D], n_ex=E, name="attn_gate_bwd")
    do = _mm(dmix1, Wo, "nt", name="o_proj_dx", tm=tm)
    grad_mm(o_t, dmix1, "rows", 5 * D + D // 4, D // 4, 0, D, "o_proj_dw", transposed=True)
    dq, dk, dv, dqg, dkg = _attn_bwd(qf, kvf, lf_tot, do, qg2, kg2, n_ex=E, name="attn_bwd")
    dh1 = _mm(dq, Wq, "nt", name="q_proj_dx", tm=tm)
    grad_mm(h1_t, dq, "rows", 5 * D, D // 4, 0, D, "q_proj_dw", transposed=True)
    (dx2,), (dsh_a1, dsc_a1, dg_mix1) = _norm_mod_bwd(x2, dh1, dx3, mix_norm_g[1:2], mod1[1], n_ex=E, name="mix_norm_bwd1")
    dkv = jnp.concatenate([dk, dv], axis=1)
    dhkv = _mm(dkv, Wkv, "nt", name="kv_proj_dx", tm=tm)
    grad_mm(hkv_t, dkv, "cols", 4 * D, D, D // 2, D // 2, "kv_proj_dw", transposed=True)
    (dx2,), (dkv_sh, dkv_sc, dg_kv) = _norm_mod_bwd(x2, dhkv, dx2, kv_norm_g.reshape(1, D), kv_sc, n_ex=E, name="kv_norm_bwd")

    dx1, (dsh_m0, dsc_m0, dgm0), dg_mlp0 = mlp_bwd(dx2, x1, 0, mod0, saved_mlp0)

    def glu_bwd(v, g, d, ga):
        sg = _sigmoid(g)
        dm = ga * d
        return [jnp.concatenate([dm * sg, dm * v * sg * (1.0 - sg)], axis=1)], [_csum(d * (v * sg))]
    (dvg,), (dga0,) = _rowwise(glu_bwd, [(vg, D, 0), (vg, D, 1), (dx1, D, 0)], [mod0[2]], [], [(2 * D, BF16)], [D],
                               n_ex=E, name="glu_gate_bwd")
    dgy = _mm(dvg, Wglu, "nt", name="glu_up_dx", tm=tm)
    grad_mm(gy_t, dvg, "cols", 4 * D, D, 0, D // 2, "glu_up_dw", transposed=True)

    gpack = gbuf[0].reshape(4, 2, RW // 2, D)
    theirs = _sibling_swap_half(gpack, name="gscatter_d2d")
    chip_sum = _add_my_half(gpack, theirs, cidx, name="gscatter_add")
    dh0, dWb, dWc, dab, dd, from_chips = _s5_bwd(h0, y, dgy, s5_states, Wb, Wc, cr, d_full, chip_sum, n_ex=E, name="s5_bwd")
    ghalf = _sum4_into_half(from_chips, cidx, name="gscatter_sum")
    gsh = _sibling_fill(ghalf, axis=0, name="gscatter_fill").reshape(RW, D)
    (gx,), (dsh_a0, dsc_a0, dg_mix0) = _norm_mod_bwd(x0, dh0, dx1, mix_norm_g[0:1], mod0[1], n_ex=E, name="mix_norm_bwd0")
    grad_x = gx.reshape(E, S, D)

    dm_mine = jnp.concatenate([t.reshape(E, D) for t in
                               (dsh_a0, dsc_a0, dga0, dsh_m0, dsc_m0, dgm0, dsh_a1, dsc_a1, dga1, dsh_m1, dsc_m1, dgm1, dkv_sh, dkv_sc)], axis=1)
    dm_all = _all_gather8(dm_mine.reshape(8, -1), name="ag_dm").reshape(NB, 14 * D)
    sc_f32 = c_all * _sigmoid(c_all)
    g_ada_w = jax.ShapeDtypeStruct(ada_w.shape, F32)
    for l in range(2):
        g_ada_w = _mm(sc_f32, lax.dynamic_slice_in_dim(dm_all, l * 6 * D + chip * wa, wa, axis=1), "tn", name=f"ada_dw{l}", tn=256,
                      into=_Layer(g_ada_w, l))
    g_kv_ada_w = _mm(sc_f32, lax.dynamic_slice_in_dim(dm_all, 12 * D + chip * wk, wk, axis=1), "tn", name="ada_kv_dw", tn=256)
    db_all = _colsum(dm_all, name="ada_db")
    g_ada_b = db_all[0, :12 * D].reshape(2, 6 * D)
    g_kv_ada_b = db_all[0, 12 * D:]

    dWb_re, dWb_im, dC_re, dC_im = _s5_unblock(dWb, dWc)
    small_parts = [dg_mix0.sum(0), dg_mix1.sum(0), dg_mlp0.sum(0), dg_mlp1.sum(0), dg_kv.sum(0),
                   dqg.sum((0, 1, 2)).reshape(2, HEAD_DIM).sum(0), dkg.sum((0, 1, 2)).reshape(2, HEAD_DIM).sum(0),
                   dd[:, 0, :], dab[:, 0, :], dab[:, 1, :], dWb_re, dWb_im, dC_re, dC_im]
    spack, spans = _pack_rows(small_parts)
    chip_half = _sibling_sum_half(spack, name="small_d2d")
    ssum = _sum_blocks(_all_gather8(chip_half, name="ag_small"), 4, name="sum_small")
    (g_mix0, g_mix1, g_mlp0, g_mlp1, g_kvn, g_qn, g_kn, g_d, g_abr, g_abi, g_bbr, g_bbi, g_cre, g_cim) = _unpack_rows(ssum, spans)
    _, disc_vjp = jax.vjp(_s5_disc, s5_a_re[0], s5_a_im[0], s5_log_dt[0], s5_b_re[0], s5_b_im[0])
    g_are, g_aim, g_ldt, g_bre, g_bim = disc_vjp((g_abr.reshape(ab_re.shape), g_abi.reshape(ab_im.shape), g_bbr, g_bbi))
    g_s5d = lax.dynamic_slice_in_dim(g_d.reshape(1, D), chip * s5_d.shape[1], s5_d.shape[1], axis=1)

    def upd_big(w, m, v, roff, cb, name):
        shape = w.shape
        W = shape[-1]
        d_, m_, v_, g_ = _adamw2d(w.reshape(-1, W), gsh, m.reshape(-1, W), v.reshape(-1, W), name=name, g_roff=roff, g_cb=cb)
        return [t.reshape(shape) for t in (g_, d_, m_, v_)]

    def upd_own(w, g, m, v, name):
        shape = w.shape
        W = shape[-1]
        d_, m_, v_, g_ = _adamw2d(w.reshape(-1, W), g.reshape(-1, W), m.reshape(-1, W), v.reshape(-1, W), name=name)
        return [t.reshape(shape) for t in (g_, d_, m_, v_)]

    res = {}
    res["ada_w"] = upd_own(ada_w, g_ada_w, m_ada_w, v_ada_w, "adam_ada_w")
    res["kv_ada_w"] = upd_own(kv_ada_w, g_kv_ada_w, m_kv_ada_w, v_kv_ada_w, "adam_kv_ada_w")
    res["mlp_w1"] = upd_big(mlp_w1, m_mlp_w1, v_mlp_w1, 0, 0, "adam_w1")
    res["mlp_w2"] = upd_big(mlp_w2, m_mlp_w2, v_mlp_w2, 2 * D, 0, "adam_w2")
    res["s5_w_glu"] = upd_big(s5_w_glu, m_s5_w_glu, v_s5_w_glu, 4 * D, 0, "adam_glu")
    res["w_kv"] = upd_big(w_kv, m_w_kv, v_w_kv, 4 * D, 1, "adam_wkv")
    res["sb_w_q"] = upd_big(sb_w_q, m_sb_w_q, v_sb_w_q, 5 * D, 0, "adam_wq")
    res["sb_w_o"] = upd_big(sb_w_o, m_sb_w_o, v_sb_w_o, 5 * D + D // 4, 0, "adam_wo")

    small = {
        "ada_b": (ada_b, g_ada_b, m_ada_b, v_ada_b),
        "mix_norm_g": (mix_norm_g, jnp.stack([g_mix0, g_mix1]), m_mix_norm_g, v_mix_norm_g),
        "mlp_norm_g": (mlp_norm_g, jnp.stack([g_mlp0, g_mlp1]), m_mlp_norm_g, v_mlp_norm_g),
        "s5_a_re": (s5_a_re, g_are[None], m_s5_a_re, v_s5_a_re),
        "s5_a_im": (s5_a_im, g_aim[None], m_s5_a_im, v_s5_a_im),
        "s5_log_dt": (s5_log_dt, g_ldt[None], m_s5_log_dt, v_s5_log_dt),
        "s5_b_re": (s5_b_re, g_bre[None], m_s5_b_re, v_s5_b_re),
        "s5_b_im": (s5_b_im, g_bim[None], m_s5_b_im, v_s5_b_im),
        "s5_c_re": (s5_c_re, g_cre[None], m_s5_c_re, v_s5_c_re),
        "s5_c_im": (s5_c_im, g_cim[None], m_s5_c_im, v_s5_c_im),
        "s5_d": (s5_d, g_s5d, m_s5_d, v_s5_d),
        "kv_ada_b": (kv_ada_b, g_kv_ada_b, m_kv_ada_b, v_kv_ada_b),
        "kv_norm_g": (kv_norm_g, g_kvn, m_kv_norm_g, v_kv_norm_g),
        "k_norm_g": (k_norm_g, g_kn, m_k_norm_g, v_k_norm_g),
        "q_norm_g": (q_norm_g, g_qn.reshape(q_norm_g.shape), m_q_norm_g, v_q_norm_g),
    }
    names = list(small)
    packs = [_pack_rows([small[n][i].reshape(small[n][0].shape) for n in names]) for i in range(4)]
    sp = packs[0][1]
    d_, m_, v_, g_ = _adamw2d(packs[0][0], packs[1][0], packs[2][0], packs[3][0], name="adam_small")
    for n, gg, dd_, mm_, vv_ in zip(names, _unpack_rows(g_, sp), _unpack_rows(d_, sp), _unpack_rows(m_, sp), _unpack_rows(v_, sp)):
        res[n] = [gg, dd_, mm_, vv_]

    order = ["ada_w", "ada_b", "mix_norm_g", "mlp_norm_g", "mlp_w1", "mlp_w2", "s5_a_re", "s5_a_im", "s5_log_dt", "s5_b_re", "s5_b_im",
             "s5_c_re", "s5_c_im", "s5_d", "s5_w_glu", "kv_ada_w", "kv_ada_b", "kv_norm_g", "w_kv", "k_norm_g", "sb_w_q", "q_norm_g", "sb_w_o"]
    return (loss, grad_x, *[res[n][0] for n in order], *[res[n][1] for n in order], *[res[n][2] for n in order], *[res[n][3] for n in order])
```

```python
import functools
import math

import jax
import jax.numpy as jnp
from jax import lax
from jax.experimental import pallas as pl
from jax.experimental.pallas import tpu as pltpu

F32 = jnp.float32
BF16 = jnp.bfloat16
EPS = 1e-6
HEAD_DIM = 64
S5_GROUP = 16
S5_STATE = 64
GROUPS_PER_STEP = 8
U_LANES = GROUPS_PER_STEP * S5_GROUP
ST_LANES = GROUPS_PER_STEP * S5_STATE
SCAN_LANES = 256
SCAN_UNROLL = 4
VMEM_LIMIT = 56 * 1024 * 1024
ADAM_LR, ADAM_B1, ADAM_B2, ADAM_EPS, ADAM_WD, ADAM_STEP = 0.001, 0.9, 0.999, 1e-08, 0.01, 10
MESH = pl.DeviceIdType.MESH


def _cp(sem):
    return pltpu.CompilerParams(dimension_semantics=sem, vmem_limit_bytes=VMEM_LIMIT)


class _Sharded:
    def __init__(self, buf, kind, roff, nr, c0, nc):
        self.buf, self.kind, self.roff, self.nr, self.c0, self.nc = buf, kind, roff, nr, c0, nc
        self.shape = (nr, 4 * nc) if kind == "cols" else (4 * nr, nc)

    def operand(self, dims, tn, tk):
        roff, nr, c0, nc = self.roff, self.nr, self.c0, self.nc
        if self.kind == "cols" and dims == "nn":
            tk = min(tk, nr)
            assert roff % tk == 0
            return nc, tk, (None, tk, nc), lambda i, j, k: (j, roff // tk + k, c0 // nc)
        if self.kind == "cols":
            tn = min(tn, nr)
            assert roff % tn == 0
            return tn, nc, (None, tn, nc), lambda i, j, k: (k, roff // tn + j, c0 // nc)
        if dims == "nn":
            tn = min(tn, nc)
            assert roff % nr == 0 and c0 % tn == 0
            return tn, nr, (None, nr, tn), lambda i, j, k: (k, roff // nr, c0 // tn + j)
        tk = min(tk, nc)
        assert roff % nr == 0 and c0 % tk == 0
        return nr, tk, (None, nr, tk), lambda i, j, k: (j, roff // nr, c0 // tk + k)

    def result(self, tm, tn):
        roff, nr, c0, nc = self.roff, self.nr, self.c0, self.nc
        if self.kind == "cols":
            tm = min(tm, nr)
            assert roff % tm == 0
            return tm, nc, (None, tm, nc), lambda i, j, k: (j, roff // tm + i, c0 // nc)
        tm, tn = min(tm, nr), min(tn, nc)
        assert roff % tm == 0 and c0 % tn == 0
        per = nr // tm
        return tm, tn, (None, tm, tn), lambda i, j, k: (i // per, roff // tm + i % per, c0 // tn + j)


class _Layer:
    def __init__(self, buf, layer):
        self.buf, self.layer, self.shape = buf, layer, tuple(buf.shape[1:])

    def operand(self, dims, tn, tk):
        assert dims == "nn"
        layer = self.layer
        return tn, tk, (None, tk, tn), lambda i, j, k: (layer, k, j)

    def result(self, tm, tn):
        layer = self.layer
        return tm, tn, (None, tm, tn), lambda i, j, k: (layer, i, j)


def _mm(a, b, dims, *, name, out_dtypes=(F32,), epilogue=None, extras=(), tm=512, tn=1024, tk=1024, into=None):
    bshape = b.shape
    if dims == "nn":
        (M, K), (_, N) = a.shape, bshape
    elif dims == "nt":
        (M, K), (N, _) = a.shape, bshape
    else:
        (K, M), (_, N) = a.shape, bshape
    tm, tn, tk = min(tm, M), min(tn, N), min(tk, K)
    b_arr = b
    if into is not None:
        assert (M, N) == into.shape and len(out_dtypes) == 1 and not isinstance(b, _Sharded)
        tm, tn, o_blk, o_map = into.result(tm, tn)
        out_specs, out_shape = [pl.BlockSpec(o_blk, o_map)], [jax.ShapeDtypeStruct(into.buf.shape, into.buf.dtype)]
    if isinstance(b, (_Sharded, _Layer)):
        tn, tk, b_blk, b_map = b.operand(dims, tn, tk)
        b_spec, b_arr = pl.BlockSpec(b_blk, b_map), b.buf
    else:
        b_spec = pl.BlockSpec((tn, tk), lambda i, j, k: (j, k)) if dims == "nt" else pl.BlockSpec((tk, tn), lambda i, j, k: (k, j))
    if into is None:
        out_specs = [pl.BlockSpec((tm, tn), lambda i, j, k: (i, j)) for _ in out_dtypes]
        out_shape = [jax.ShapeDtypeStruct((M, N), d) for d in out_dtypes]
    assert M % tm == 0 and N % tn == 0 and K % tk == 0, (M, N, K, tm, tn, tk)
    nk = K // tk
    extras = [e(tm, tn) for e in extras]
    a_spec = pl.BlockSpec((tk, tm), lambda i, j, k: (k, i)) if dims == "tn" else pl.BlockSpec((tm, tk), lambda i, j, k: (i, k))
    contract = {"nn": ((1,), (0,)), "nt": ((1,), (1,)), "tn": ((0,), (0,))}[dims]
    n_ex, n_out = len(extras), len(out_dtypes)
    chain = [into.buf] if into is not None and not isinstance(into.buf, jax.ShapeDtypeStruct) else []
    n_in = n_ex + len(chain)

    def finish(r, ex, outs):
        res = epilogue(r, *[e[...] for e in ex]) if epilogue is not None else (r,)
        for o, v in zip(outs, res):
            o[...] = v.astype(o.dtype)

    def product(a_ref, b_ref):
        return lax.dot_general(a_ref[...].astype(BF16), b_ref[...].astype(BF16), (contract, ((), ())), preferred_element_type=F32)

    def body_one(a_ref, b_ref, *rest):
        finish(product(a_ref, b_ref), rest[:n_ex], rest[n_in:])

    def body_acc(a_ref, b_ref, *rest):
        ex, outs, acc = rest[:n_ex], rest[n_in:n_in + n_out], rest[-1]
        k = pl.program_id(2)

        @pl.when(k == 0)
        def _():
            acc[...] = product(a_ref, b_ref)

        @pl.when(jnp.logical_and(k > 0, k < nk - 1))
        def _():
            acc[...] += product(a_ref, b_ref)

        @pl.when(k == nk - 1)
        def _():
            finish(acc[...] + product(a_ref, b_ref), ex, outs)

    out = pl.pallas_call(
        body_one if nk == 1 else body_acc, name=name, grid=(M // tm, N // tn, nk),
        in_specs=[a_spec, b_spec] + [pl.BlockSpec(blk, im) for (_, blk, im) in extras] + [ANY for _ in chain],
        out_specs=out_specs, out_shape=out_shape,
        input_output_aliases={2 + n_ex: 0} if chain else {},
        scratch_shapes=[] if nk == 1 else [pltpu.VMEM((tm, tn), F32)],
        compiler_params=_cp(("parallel", "parallel", "arbitrary")),
    )(a, b_arr, *[e[0] for e in extras], *chain)
    return out if n_out > 1 else out[0]


def _mn_extra(arr):
    return lambda tm, tn: (arr, (tm, tn), lambda i, j, k: (i, j))


def _vec_extra(vec, S):
    return lambda tm, tn: (vec, (None, 1, tn), lambda i, j, k: ((i * tm) // S, 0, j))


def _rowwise(fn, rows, vecs=(), consts=(), out_rows=(), out_sums=(), *, n_ex, name, tr=512):
    rows = [r if len(r) == 4 else (*r, 0) for r in rows]
    S = min(r[0].shape[0] for r in rows if r[3] == 0) // n_ex
    tr = math.gcd(tr, S)
    assert S % tr == 0
    nb = S // tr
    in_specs = []
    for (arr, w, cb, roff) in rows:
        assert roff % tr == 0
        in_specs.append(pl.BlockSpec((tr, w), functools.partial(lambda e, i, cb, ro: (e * nb + i + ro, cb), cb=cb, ro=roff // tr)))
    for v in vecs:
        in_specs.append(pl.BlockSpec((None, 1, v.shape[-1]), lambda e, i: (e, 0, 0)))
    for c in consts:
        in_specs.append(pl.BlockSpec((1, c.shape[-1]), lambda e, i: (0, 0)))
    n_in, n_or, n_os = len(in_specs), len(out_rows), len(out_sums)
    flipped = [len(o) == 3 and o[2] for o in out_rows]
    out_specs = [pl.BlockSpec((o[0], tr), lambda e, i: (0, e * nb + i)) if f else pl.BlockSpec((tr, o[0]), lambda e, i: (e * nb + i, 0))
                 for o, f in zip(out_rows, flipped)]
    out_specs += [pl.BlockSpec((None, 1, w), lambda e, i: (e, 0, 0)) for w in out_sums]
    out_shape = [jax.ShapeDtypeStruct((o[0], n_ex * S) if f else (n_ex * S, o[0]), o[1]) for o, f in zip(out_rows, flipped)]
    out_shape += [jax.ShapeDtypeStruct((n_ex, 1, w), F32) for w in out_sums]

    def body(*refs):
        ins, o_r, o_s = refs[:n_in], refs[n_in:n_in + n_or], refs[n_in + n_or:]
        ro, so = fn(*[r[...] for r in ins])
        for o, v, f in zip(o_r, ro, flipped):
            o[...] = (v.T if f else v).astype(o.dtype)
        i = pl.program_id(1)
        for o, v in zip(o_s, so):
            @pl.when(i == 0)
            def _(o=o, v=v):
                o[...] = v

            @pl.when(i > 0)
            def _(o=o, v=v):
                o[...] += v

    outs = pl.pallas_call(
        body, name=name, grid=(n_ex, nb), in_specs=in_specs, out_specs=out_specs, out_shape=out_shape,
        compiler_params=_cp(("parallel", "arbitrary")),
    )(*[r[0] for r in rows], *vecs, *consts)
    return outs[:n_or], outs[n_or:]


def _csum(x):
    return jnp.sum(x, axis=0, keepdims=True)


def _norm_mod_fwd(x, g, sh, sc, *, n_ex, out_dtype, name, with_transpose=False):
    def fn(xt, sht, sct, gt):
        r = lax.rsqrt(jnp.mean(xt * xt, axis=-1, keepdims=True) + EPS)
        h = (xt * r * gt) * (1.0 + sct) + sht
        return [h, h] if with_transpose else [h], []
    D = x.shape[1]
    outs = [(D, out_dtype), (D, out_dtype, True)] if with_transpose else [(D, out_dtype)]
    res = _rowwise(fn, [(x, D, 0)], [sh, sc], [g], outs, [], n_ex=n_ex, name=name)[0]
    return res if with_transpose else res[0]


def _norm_mod_bwd(x, dh, dres, g, sc, *, n_ex, name):
    def fn(xt, dht, drt, sct, gt):
        dht = dht.astype(F32)
        r = lax.rsqrt(jnp.mean(xt * xt, axis=-1, keepdims=True) + EPS)
        n = xt * r
        y = n * gt
        dy = dht * (1.0 + sct)
        dn = dy * gt
        dx = r * (dn - n * jnp.mean(dn * n, axis=-1, keepdims=True))
        return [drt + dx], [_csum(dht), _csum(dht * y), _csum(dy * n)]
    D = x.shape[1]
    return _rowwise(fn, [(x, D, 0), (dh, D, 0), (dres, D, 0)], [sc], [g], [(D, F32)], [D, D, D], n_ex=n_ex, name=name)


def _sigmoid(x):
    return 1.0 / (1.0 + jnp.exp(-x))


def _gelu(y):
    return 0.5 * y * (1.0 + jnp.tanh(0.7978845608028654 * (y + 0.044715 * y * y * y)))


def _gelu_grad(y):
    t = jnp.tanh(0.7978845608028654 * (y + 0.044715 * y * y * y))
    return 0.5 * (1.0 + t) + 0.5 * y * (1.0 - t * t) * 0.7978845608028654 * (1.0 + 3 * 0.044715 * y * y)


def _adamw_fn(w, g, m, v):
    m2 = ADAM_B1 * m + (1.0 - ADAM_B1) * g
    v2 = ADAM_B2 * v + (1.0 - ADAM_B2) * (g * g)
    m_hat = m2 / (1.0 - ADAM_B1 ** ADAM_STEP)
    v_hat = v2 / (1.0 - ADAM_B2 ** ADAM_STEP)
    delta = -ADAM_LR * (m_hat / (jnp.sqrt(v_hat) + ADAM_EPS) + ADAM_WD * w)
    return delta, m2, v2


def _adamw2d(w, g, m, v, *, name, g_roff=0, g_cb=0):
    R, W = w.shape

    def fn(wt, gt, mt, vt):
        d, m2, v2 = _adamw_fn(wt, gt, mt, vt)
        return [d, m2, v2, gt], []
    return _rowwise(fn, [(w, W, 0), (g, W, g_cb, g_roff), (m, W, 0), (v, W, 0)], [], [],
                    [(W, F32)] * 4, [], n_ex=1, name=name, tr=256)[0]


def _scan_tiles(re_ref, im_ref, cf, lane0, n_chunks, reverse, extra=None):
    L = SCAN_LANES
    lanes = pl.ds(lane0, L)
    A = [cf[i, :, lanes] for i in range(8)]
    shifts = (7, 6, 4) if reverse else (1, 2, 4)
    edge = 0 if reverse else 7

    U = SCAN_UNROLL
    n_groups = n_chunks // U

    def body(c, carry):
        first = ((n_groups - 1 - c) if reverse else c) * U
        rows = pl.ds(pl.multiple_of(first * 8, 8 * U), 8 * U)
        big_r, big_i = re_ref[rows, lanes], im_ref[rows, lanes]
        tiles = []
        for u in range(U):
            xr, xi = big_r[8 * u:8 * u + 8, :], big_i[8 * u:8 * u + 8, :]
            for idx, sft in enumerate(shifts):
                ar, ai = A[2 * idx], A[2 * idx + 1]
                rr, ri = pltpu.roll(xr, sft, 0), pltpu.roll(xi, sft, 0)
                xr, xi = xr + ar * rr - ai * ri, xi + ar * ri + ai * rr
            tiles.append((xr, xi))
        pr, pi = A[6], A[7]
        cr, ci = carry[0], carry[1]
        for u in (range(U - 1, -1, -1) if reverse else range(U)):
            xr, xi = tiles[u]
            xr, xi = xr + pr * cr - pi * ci, xi + pr * ci + pi * cr
            tiles[u] = (xr, xi)
            cr, ci = jnp.broadcast_to(xr[edge:edge + 1, :], (8, L)), jnp.broadcast_to(xi[edge:edge + 1, :], (8, L))
        re_ref[rows, lanes] = jnp.concatenate([t[0] for t in tiles], axis=0)
        im_ref[rows, lanes] = jnp.concatenate([t[1] for t in tiles], axis=0)
        return (cr, ci) if extra is None else (cr, ci) + extra(first, tiles, carry[2:])

    assert n_chunks % U == 0
    z = jnp.zeros((8, L), F32)
    init = (z, z) if extra is None else (z, z, z, z)
    return lax.fori_loop(0, n_groups, body, init)


def _s5_consts(ab_re, ab_im):
    ng = ab_re.shape[0] // GROUPS_PER_STEP
    ar, ai = ab_re.reshape(ng, 1, ST_LANES), ab_im.reshape(ng, 1, ST_LANES)

    def cmul(xr, xi, yr, yi):
        return xr * yr - xi * yi, xr * yi + xi * yr

    def build(ar, ai, reverse):
        pw = [(ar, ai)]
        for _ in range(7):
            pw.append(cmul(*pw[-1], ar, ai))
        row = jnp.arange(8).reshape(1, 8, 1)
        tiles = []
        for k in (1, 2, 4):
            keep = (row <= 7 - k) if reverse else (row >= k)
            tiles += [jnp.where(keep, pw[k - 1][0], 0.0), jnp.where(keep, pw[k - 1][1], 0.0)]
        order = [7 - r for r in range(8)] if reverse else list(range(8))
        tiles += [jnp.concatenate([pw[o][0] for o in order], axis=1), jnp.concatenate([pw[o][1] for o in order], axis=1)]
        return jnp.stack([jnp.broadcast_to(t, (ng, 8, ST_LANES)) for t in tiles], axis=1)

    return build(ar, ai, False), build(ar, -ai, True)


def _s5_blockdiag(bb_re, bb_im, c_re, c_im):
    G = bb_re.shape[0]
    ng = G // GROUPS_PER_STEP
    eye = jnp.eye(GROUPS_PER_STEP, dtype=F32)

    def wb(bb):
        return jnp.einsum("bgph,gk->bghkp", bb.reshape(ng, GROUPS_PER_STEP, S5_STATE, S5_GROUP), eye).reshape(ng, U_LANES, ST_LANES)

    def wc(cc):
        return jnp.einsum("bghp,gk->bkpgh", cc.reshape(ng, GROUPS_PER_STEP, S5_GROUP, S5_STATE), eye).reshape(ng, ST_LANES, U_LANES)

    Wb = jnp.concatenate([wb(bb_re), wb(bb_im)], axis=2).astype(BF16)
    Wc = jnp.concatenate([wc(c_re), -wc(c_im)], axis=1).astype(BF16)
    return Wb, Wc


def _s5_unblock(dWb, dWc):
    ng = dWb.shape[0]
    eye = jnp.eye(GROUPS_PER_STEP, dtype=F32)

    def ub(w):
        return jnp.einsum("bghkp,gk->bgph", w.reshape(ng, GROUPS_PER_STEP, S5_GROUP, GROUPS_PER_STEP, S5_STATE), eye).reshape(-1, S5_STATE, S5_GROUP)

    def uc(w):
        return jnp.einsum("bkpgh,gk->bghp", w.reshape(ng, GROUPS_PER_STEP, S5_STATE, GROUPS_PER_STEP, S5_GROUP), eye).reshape(-1, S5_GROUP, S5_STATE)

    return ub(dWb[:, :, :ST_LANES]), ub(dWb[:, :, ST_LANES:]), uc(dWc[:, :ST_LANES, :]), -uc(dWc[:, ST_LANES:, :])


def _s5_disc(a_re, a_im, log_dt, b_re, b_im):
    dt = jnp.exp(log_dt)[:, None]
    mag = jnp.exp(a_re * dt)
    ab_re = mag * jnp.cos(a_im * dt)
    ab_im = mag * jnp.sin(a_im * dt)
    den = a_re * a_re + a_im * a_im
    nr, ni = ab_re - 1, ab_im
    f_re = (nr * a_re + ni * a_im) / den
    f_im = (ni * a_re - nr * a_im) / den
    bb_re = f_re[..., None] * b_re - f_im[..., None] * b_im
    bb_im = f_re[..., None] * b_im + f_im[..., None] * b_re
    return ab_re, ab_im, bb_re, bb_im


ROW_CHUNK = 512


def _s5_fwd(u, Wb, Wc, cf, d, xsrc, *, n_ex, name):
    T, D = u.shape
    S = T // n_ex
    ng = D // U_LANES
    rc = min(ROW_CHUNK, S)

    def body(u_ref, wb_ref, wc_ref, cf_ref, d_ref, xsrc_ref, y_ref, gy_ref, gyt_ref, st_ref, xout_ref, re_s, im_s, *sems):
        step = pl.program_id(0) * ng + pl.program_id(1)
        exch = _ChipExchange(xsrc_ref, xout_ref, *sems, scatter=False)

        @pl.when(step == 0)
        def _():
            exch.start()

        for r in range(S // rc):
            rows = pl.ds(r * rc, rc)
            bu = jnp.dot(u_ref[rows, :].astype(BF16), wb_ref[...], preferred_element_type=F32)
            re_s[rows, :] = bu[:, :ST_LANES]
            im_s[rows, :] = bu[:, ST_LANES:]
        for l0 in range(0, ST_LANES, SCAN_LANES):
            _scan_tiles(re_s, im_s, cf_ref, l0, S // 8, False)
        for r in range(S // rc):
            rows = pl.ds(r * rc, rc)
            st = jnp.concatenate([re_s[rows, :], im_s[rows, :]], axis=1).astype(BF16)
            st_ref[rows, :] = st
            y = jnp.dot(st, wc_ref[...], preferred_element_type=F32) + d_ref[...] * u_ref[rows, :]
            y_ref[rows, :] = y
            gy = _gelu(y)
            gy_ref[rows, :] = gy.astype(BF16)
            gyt_ref[:, rows] = gy.T.astype(BF16)

        @pl.when(step == n_ex * ng - 1)
        def _():
            exch.wait()

    return pl.pallas_call(
        body, name=name, grid=(n_ex, ng),
        in_specs=[pl.BlockSpec((S, U_LANES), lambda e, g: (e, g)),
                  pl.BlockSpec((None, U_LANES, 2 * ST_LANES), lambda e, g: (g, 0, 0)),
                  pl.BlockSpec((None, 2 * ST_LANES, U_LANES), lambda e, g: (g, 0, 0)),
                  pl.BlockSpec((None, 8, 8, ST_LANES), lambda e, g: (g, 0, 0, 0)),
                  pl.BlockSpec((1, U_LANES), lambda e, g: (0, g)), ANY],
        out_specs=[pl.BlockSpec((S, U_LANES), lambda e, g: (e, g))] * 2 + [pl.BlockSpec((U_LANES, S), lambda e, g: (g, e)),
                   pl.BlockSpec((S, 2 * ST_LANES), lambda e, g: (e, g)), ANY],
        out_shape=[jax.ShapeDtypeStruct((T, D), F32), jax.ShapeDtypeStruct((T, D), BF16), jax.ShapeDtypeStruct((D, T), BF16),
                   jax.ShapeDtypeStruct((T, ng * 2 * ST_LANES), BF16), _ChipExchange.out_shape(xsrc, False)],
        scratch_shapes=[pltpu.VMEM((S, ST_LANES), F32)] * 2 + _ChipExchange.SCRATCH,
        compiler_params=_cp(("arbitrary", "arbitrary")),
    )(u, Wb, Wc, cf, d, xsrc)


def _s5_bwd(u, y, dgy, st, Wb, Wc, cr, d, xsrc, *, n_ex, name):
    T, D = u.shape
    S = T // n_ex
    ng = D // U_LANES
    rc = min(ROW_CHUNK, S)
    nch = S // 8
    grp = 8 * SCAN_UNROLL
    assert grp % 16 == 0

    def body(u_ref, y_ref, dgy_ref, st_ref, wb_ref, wc_ref, cr_ref, d_ref, xsrc_ref,
             du_ref, dwb_ref, dwc_ref, dab_ref, dd_ref, xout_ref, gr_s, gi_s, dy_s, *sems):
        e = pl.program_id(1)
        step = pl.program_id(0) * n_ex + e
        exch = _ChipExchange(xsrc_ref, xout_ref, *sems, scatter=True)

        @pl.when(step == 0)
        def _():
            exch.start()

        @pl.when(e == 0)
        def _():
            dwb_ref[...] = jnp.zeros_like(dwb_ref)
            dwc_ref[...] = jnp.zeros_like(dwc_ref)
            dab_ref[...] = jnp.zeros_like(dab_ref)
            dd_ref[...] = jnp.zeros_like(dd_ref)

        dd = jnp.zeros((1, U_LANES), F32)
        for r in range(S // rc):
            rows = pl.ds(r * rc, rc)
            ut = u_ref[rows, :]
            dy = dgy_ref[rows, :].astype(F32) * _gelu_grad(y_ref[rows, :])
            dy_s[rows, :] = dy
            dd = dd + _csum(dy * ut)
            go = lax.dot_general(dy.astype(BF16), wc_ref[...], (((1,), (1,)), ((), ())), preferred_element_type=F32)
            gr_s[rows, :] = go[:, :ST_LANES]
            gi_s[rows, :] = go[:, ST_LANES:]
        dd_ref[0:1, :] += dd
        row0 = lax.broadcasted_iota(jnp.int32, (8, SCAN_LANES), 0) == 0
        for l0 in range(0, ST_LANES, SCAN_LANES):
            lanes = pl.ds(l0, SCAN_LANES)

            def dab_group(first, tiles, acc, l0=l0):
                def states(r0, n, lane0):
                    return st_ref[pl.ds(pl.multiple_of(r0, 16), n), pl.ds(lane0, SCAN_LANES)].astype(F32)
                r0 = first * 8
                cur = states(r0, grp, l0), states(r0, grp, ST_LANES + l0)
                live = (first > 0).astype(F32)
                p0 = jnp.maximum(r0 - 16, 0)
                before = [states(p0, 16, l0)[8:16, :] * live, states(p0, 16, ST_LANES + l0)[8:16, :] * live]
                a_re, a_im = acc
                for t, (gr, gi) in enumerate(tiles):
                    here = [c[8 * t:8 * t + 8, :] for c in cur]
                    sr, si = [jnp.where(row0, pltpu.roll(b, 1, 0), pltpu.roll(h, 1, 0)) for b, h in zip(before, here)]
                    a_re, a_im = a_re + gr * sr + gi * si, a_im + gi * sr - gr * si
                    before = here
                return a_re, a_im

            res = _scan_tiles(gr_s, gi_s, cr_ref, l0, nch, True, extra=dab_group)
            dab_ref[0:1, lanes] += _csum(res[2])
            dab_ref[1:2, lanes] += _csum(res[3])
        for r in range(S // rc):
            rows = pl.ds(r * rc, rc)
            st = st_ref[rows, :]
            g = jnp.concatenate([gr_s[rows, :], gi_s[rows, :]], axis=1).astype(BF16)
            dyb = dy_s[rows, :].astype(BF16)
            dwc_ref[...] += lax.dot_general(st, dyb, (((0,), (0,)), ((), ())), preferred_element_type=F32)
            dwb_ref[...] += lax.dot_general(u_ref[rows, :].astype(BF16), g, (((0,), (0,)), ((), ())), preferred_element_type=F32)
            du = lax.dot_general(g, wb_ref[...], (((1,), (1,)), ((), ())), preferred_element_type=F32)
            du_ref[rows, :] = du + d_ref[...] * dy_s[rows, :]

        @pl.when(step == ng * n_ex - 1)
        def _():
            exch.wait()

    return pl.pallas_call(
        body, name=name, grid=(ng, n_ex),
        in_specs=[pl.BlockSpec((S, U_LANES), lambda g, e: (e, g))] * 3 + [
            pl.BlockSpec((S, 2 * ST_LANES), lambda g, e: (e, g)),
            pl.BlockSpec((None, U_LANES, 2 * ST_LANES), lambda g, e: (g, 0, 0)),
            pl.BlockSpec((None, 2 * ST_LANES, U_LANES), lambda g, e: (g, 0, 0)),
            pl.BlockSpec((None, 8, 8, ST_LANES), lambda g, e: (g, 0, 0, 0)),
            pl.BlockSpec((1, U_LANES), lambda g, e: (0, g)), ANY],
        out_specs=[pl.BlockSpec((S, U_LANES), lambda g, e: (e, g)),
                   pl.BlockSpec((None, U_LANES, 2 * ST_LANES), lambda g, e: (g, 0, 0)),
                   pl.BlockSpec((None, 2 * ST_LANES, U_LANES), lambda g, e: (g, 0, 0)),
                   pl.BlockSpec((None, 8, ST_LANES), lambda g, e: (g, 0, 0)),
                   pl.BlockSpec((None, 8, U_LANES), lambda g, e: (g, 0, 0)), ANY],
        out_shape=[jax.ShapeDtypeStruct((T, D), F32),
                   jax.ShapeDtypeStruct((ng, U_LANES, 2 * ST_LANES), F32),
                   jax.ShapeDtypeStruct((ng, 2 * ST_LANES, U_LANES), F32),
                   jax.ShapeDtypeStruct((ng, 8, ST_LANES), F32),
                   jax.ShapeDtypeStruct((ng, 8, U_LANES), F32), _ChipExchange.out_shape(xsrc, True)],
        scratch_shapes=[pltpu.VMEM((S, ST_LANES), F32)] * 2 + [pltpu.VMEM((S, U_LANES), F32)] + _ChipExchange.SCRATCH,
        compiler_params=_cp(("arbitrary", "arbitrary")),
    )(u, y, dgy, st, Wb, Wc, cr, d, xsrc)


TQ = 256
KW = 512
SUB = 128


def _head_masks():
    lane = lax.broadcasted_iota(jnp.int32, (1, 2 * HEAD_DIM), 1)
    m0 = (lane < HEAD_DIM).astype(F32)
    return m0, 1.0 - m0


def _head_norm(x, g, m0, m1):
    sq = x * x
    r0 = lax.rsqrt(jnp.sum(sq * m0, axis=-1, keepdims=True) / HEAD_DIM + EPS)
    r1 = lax.rsqrt(jnp.sum(sq * m1, axis=-1, keepdims=True) / HEAD_DIM + EPS)
    r = m0 * r0 + m1 * r1
    return x * r, r


def _head_norm_bwd(dy, n, r, g, m0, m1):
    dn = dy * g
    p = dn * n
    mean = (m0 * jnp.sum(p * m0, axis=-1, keepdims=True) + m1 * jnp.sum(p * m1, axis=-1, keepdims=True)) / HEAD_DIM
    return r * (dn - n * mean), _csum(dy * n)


def _pair_matrix(kind):
    r = lax.broadcasted_iota(jnp.int32, (2 * SUB, 2 * SUB), 0)
    c = lax.broadcasted_iota(jnp.int32, (2 * SUB, 2 * SUB), 1)
    same = (r < SUB) == (c < SUB)
    rel = {"after": r > c, "upto": r <= c, "before": r < c}[kind]
    return jnp.logical_and(same, rel).astype(BF16)


def _block_sums(x, mat, carry, reverse, terms=2):
    hi = x.astype(BF16)
    lo = (x - hi.astype(F32)).astype(BF16) if terms == 2 else None
    npair = x.shape[1] // (2 * SUB)
    parts = [None] * (2 * npair)
    for p in (range(npair - 1, -1, -1) if reverse else range(npair)):
        sl = slice(2 * SUB * p, 2 * SUB * (p + 1))
        loc = jnp.dot(hi[:, sl], mat, preferred_element_type=F32)
        if terms == 2:
            loc = loc + jnp.dot(lo[:, sl], mat, preferred_element_type=F32)
        for b in ((1, 0) if reverse else (0, 1)):
            k = 2 * p + b
            parts[k] = loc[:, SUB * b:SUB * (b + 1)] + carry
            carry = carry + jnp.sum(x[:, SUB * k:SUB * (k + 1)], axis=-1, keepdims=True)
    return jnp.concatenate(parts, axis=1), carry


def _sb_logits(z, mask):
    lp = jnp.minimum(z, 0.0) - jnp.log(1.0 + jnp.exp(-jnp.abs(z)))
    lf = lp - z
    if mask is not None:
        lf = jnp.where(mask, lf, 0.0)
    return lp, lf


def _causal_mask(row0, col0, kw):
    r = row0 + lax.broadcasted_iota(jnp.int32, (TQ, kw), 0)
    c = col0 + lax.broadcasted_iota(jnp.int32, (TQ, kw), 1)
    return c < r


def _transposed_windows(x, ref):
    for w in range(x.shape[0] // KW):
        ref[w] = x[w * KW:(w + 1) * KW, :].T.astype(BF16)


def _attn_fwd(q, kv, qg, kg, xsrc, *, n_ex, name):
    T, D = q.shape
    S = T // n_ex
    nhp = D // (2 * HEAD_DIM)
    nq = S // TQ
    scale = 1.0 / math.sqrt(HEAD_DIM)

    def body(q_ref, k_ref, v_ref, qg_ref, kg_ref, xsrc_ref, o_ref, tot_ref, ot_ref, xout_ref, kT_s, qm_s, vm_s, *sems):
        step = pl.program_id(0) * nhp + pl.program_id(1)
        exch = _ChipExchange(xsrc_ref, xout_ref, *sems, scatter=False)

        @pl.when(step == 0)
        def _():
            exch.start()

        m0, m1 = _head_masks()
        qn, _ = _head_norm(q_ref[...], None, m0, m1)
        qn = qn * (qg_ref[...] * scale)
        kn, _ = _head_norm(k_ref[...], None, m0, m1)
        _transposed_windows(kn * kg_ref[...], kT_s)
        v = v_ref[...]
        for h, m in enumerate((m0, m1)):
            qm_s[h] = (qn * m).astype(BF16)
            vm_s[h] = (v * m).astype(BF16)
        u_after = _pair_matrix("after")

        def window(rows, win, st, mask, kw):
            keys = pl.ds(pl.multiple_of(win * KW, KW), kw)
            zs = [jnp.dot(qm_s[h, rows, :], kT_s[win, :, :kw], preferred_element_type=F32) for h in range(2)]
            lg = [_sb_logits(zs[h], mask) for h in range(2)]
            sums = [_block_sums(lg[h][1], u_after, st[2 * h], True) for h in range(2)]
            out = ()
            for h in range(2):
                w = jnp.exp(lg[h][0] + sums[h][0])
                if mask is not None:
                    w = jnp.where(mask, w, 0.0)
                out += (sums[h][1], st[2 * h + 1] + jnp.dot(w.astype(BF16), vm_s[h, keys, :], preferred_element_type=F32))
            return out

        def qtile(iq, last, kw):
            rows = pl.ds(pl.multiple_of(iq * TQ, TQ), TQ)
            mask = _causal_mask(iq * TQ, last * KW, kw)
            z1, zq = jnp.zeros((TQ, 1), F32), jnp.zeros((TQ, 2 * HEAD_DIM), F32)
            st = window(rows, last, (z1, zq, z1, zq), mask, kw)
            st = lax.fori_loop(0, last, lambda jj, st: window(rows, last - 1 - jj, st, None, KW), st)
            o_ref[rows, :] = st[1] + st[3]
            tot_ref[rows, :] = st[0] * m0 + st[2] * m1

        def qtiles_of_window(a, _):
            for sub in range(KW // TQ):
                qtile(a * (KW // TQ) + sub, a, (sub + 1) * TQ)
            return 0

        lax.fori_loop(0, S // KW, qtiles_of_window, 0)
        ot_ref[...] = o_ref[...].T.astype(BF16)

        @pl.when(step == n_ex * nhp - 1)
        def _():
            exch.wait()

    assert S % KW == 0 and KW % TQ == 0
    nwin = S // KW
    blk = (S, 2 * HEAD_DIM)
    return pl.pallas_call(
        body, name=name, grid=(n_ex, nhp),
        in_specs=[pl.BlockSpec(blk, lambda e, h: (e, h)), pl.BlockSpec(blk, lambda e, h: (e, h)),
                  pl.BlockSpec(blk, lambda e, h: (e, h + nhp)),
                  pl.BlockSpec((1, 2 * HEAD_DIM), lambda e, h: (0, 0)), pl.BlockSpec((1, 2 * HEAD_DIM), lambda e, h: (0, 0)), ANY],
        out_specs=[pl.BlockSpec(blk, lambda e, h: (e, h))] * 2 + [pl.BlockSpec((2 * HEAD_DIM, S), lambda e, h: (h, e)), ANY],
        out_shape=[jax.ShapeDtypeStruct((T, D), F32)] * 2 + [jax.ShapeDtypeStruct((D, T), BF16), _ChipExchange.out_shape(xsrc, False)],
        scratch_shapes=[pltpu.VMEM((nwin, 2 * HEAD_DIM, KW), BF16), pltpu.VMEM((2,) + blk, BF16), pltpu.VMEM((2,) + blk, BF16)]
        + _ChipExchange.SCRATCH,
        compiler_params=_cp(("arbitrary", "arbitrary")),
    )(q, kv, kv, qg, kg, xsrc)


def _attn_bwd(q, kv, tot, do, qg, kg, *, n_ex, name):
    T, D = q.shape
    S = T // n_ex
    nhp = D // (2 * HEAD_DIM)
    nq = S // TQ
    scale = 1.0 / math.sqrt(HEAD_DIM)

    def body(q_ref, k_ref, v_ref, tot_ref, do_ref, qg_ref, kg_ref, dq_ref, dk_ref, dv_ref, dqg_ref, dkg_ref,
             kT_s, vT_s, km_s, qm_s, dom_s, dqn_s, dkT_s, dvT_s):
        m0, m1 = _head_masks()
        qn, qr = _head_norm(q_ref[...], None, m0, m1)
        kn, kr = _head_norm(k_ref[...], None, m0, m1)
        qs = qn * (qg_ref[...] * scale)
        kk = kn * kg_ref[...]
        _transposed_windows(kk, kT_s)
        _transposed_windows(v_ref[...], vT_s)
        do = do_ref[...]
        for h, m in enumerate((m0, m1)):
            qm_s[h] = (qs * m).astype(BF16)
            km_s[h] = (kk * m).astype(BF16)
            dom_s[h] = (do * m).astype(BF16)
        dkT_s[...] = jnp.zeros_like(dkT_s)
        dvT_s[...] = jnp.zeros_like(dvT_s)
        u_upto, u_before = _pair_matrix("upto"), _pair_matrix("before")

        def both(inv, win, st, mask, kw):
            keys = pl.ds(pl.multiple_of(win * KW, KW), kw)
            lg = [_sb_logits(jnp.dot(inv[h][0], kT_s[win, :, :kw], preferred_element_type=F32), mask) for h in range(2)]
            s_lf = [_block_sums(lg[h][1], u_upto, st[3 * h], False) for h in range(2)]
            ws, ews = [], []
            for h in range(2):
                w = jnp.exp(lg[h][0] - s_lf[h][0])
                if mask is not None:
                    w = jnp.where(mask, w, 0.0)
                ws.append(w)
                ews.append(jnp.dot(inv[h][2], vT_s[win, :, :kw], preferred_element_type=F32) * w)
            s_e = [_block_sums(ews[h], u_before, st[3 * h + 1], False, terms=1) for h in range(2)]
            out, dk, dv = (), None, None
            for h in range(2):
                sig = jnp.exp(lg[h][0])
                dz = ews[h] - sig * (ews[h] + s_e[h][0])
                if mask is not None:
                    dz = jnp.where(mask, dz, 0.0)
                dzb = dz.astype(BF16)
                out += (s_lf[h][1], s_e[h][1], st[3 * h + 2] + jnp.dot(dzb, km_s[h, keys, :], preferred_element_type=F32))
                dkh = jnp.dot(inv[h][1], dzb, preferred_element_type=F32)
                dvh = jnp.dot(inv[h][3], ws[h].astype(BF16), preferred_element_type=F32)
                dk, dv = (dkh, dvh) if h == 0 else (dk + dkh, dv + dvh)
            dkT_s[win, :, :kw] += dk
            dvT_s[win, :, :kw] += dv
            return out

        def qtile(iq, last, kw):
            rows = pl.ds(pl.multiple_of(iq * TQ, TQ), TQ)
            mask = _causal_mask(iq * TQ, last * KW, kw)
            tt = tot_ref[rows, :]
            inv, neg_total = [], []
            for h, m in enumerate((m0, m1)):
                qh, doh = qm_s[h, rows, :], dom_s[h, rows, :]
                neg_total.append(jnp.sum(tt * m, axis=-1, keepdims=True) * (-1.0 / HEAD_DIM))
                inv.append((qh, qh.astype(F32).T.astype(BF16), doh, doh.astype(F32).T.astype(BF16)))

            z1, zq = jnp.zeros((TQ, 1), F32), jnp.zeros((TQ, 2 * HEAD_DIM), F32)
            st = lax.fori_loop(0, last, lambda win, st: both(inv, win, st, None, KW), (neg_total[0], z1, zq, neg_total[1], z1, zq))
            st = both(inv, last, st, mask, kw)
            dqn_s[rows, :] = st[2] + st[5]

        def qtiles_of_window(a, _):
            for sub in range(KW // TQ):
                qtile(a * (KW // TQ) + sub, a, (sub + 1) * TQ)
            return 0

        lax.fori_loop(0, S // KW, qtiles_of_window, 0)
        dkn = jnp.concatenate([dkT_s[w].T for w in range(nwin)], axis=0)
        dq, dqg = _head_norm_bwd(dqn_s[...] * scale, qn, qr, qg_ref[...], m0, m1)
        dk, dkg = _head_norm_bwd(dkn, kn, kr, kg_ref[...], m0, m1)
        dq_ref[...] = dq
        dk_ref[...] = dk
        dv_ref[...] = jnp.concatenate([dvT_s[w].T for w in range(nwin)], axis=0)
        dqg_ref[...] = dqg
        dkg_ref[...] = dkg

    assert S % KW == 0 and KW % TQ == 0
    nwin = S // KW
    blk = (S, 2 * HEAD_DIM)
    tblk = (nwin, 2 * HEAD_DIM, KW)
    gblk = (None, None, 1, 2 * HEAD_DIM)
    dq, dk, dv, dqg, dkg = pl.pallas_call(
        body, name=name, grid=(n_ex, nhp),
        in_specs=[pl.BlockSpec(blk, lambda e, h: (e, h)), pl.BlockSpec(blk, lambda e, h: (e, h)),
                  pl.BlockSpec(blk, lambda e, h: (e, h + nhp)),
                  pl.BlockSpec(blk, lambda e, h: (e, h)), pl.BlockSpec(blk, lambda e, h: (e, h)),
                  pl.BlockSpec((1, 2 * HEAD_DIM), lambda e, h: (0, 0)), pl.BlockSpec((1, 2 * HEAD_DIM), lambda e, h: (0, 0))],
        out_specs=[pl.BlockSpec(blk, lambda e, h: (e, h))] * 3 + [pl.BlockSpec(gblk, lambda e, h: (e, h, 0, 0))] * 2,
        out_shape=[jax.ShapeDtypeStruct((T, D), F32)] * 3 + [jax.ShapeDtypeStruct((n_ex, nhp, 1, 2 * HEAD_DIM), F32)] * 2,
        scratch_shapes=[pltpu.VMEM(tblk, BF16), pltpu.VMEM(tblk, BF16),
                        pltpu.VMEM((2,) + blk, BF16), pltpu.VMEM((2,) + blk, BF16), pltpu.VMEM((2,) + blk, BF16),
                        pltpu.VMEM(blk, F32), pltpu.VMEM(tblk, F32), pltpu.VMEM(tblk, F32)],
        compiler_params=_cp(("parallel", "parallel")),
    )(q, kv, kv, tot, do, qg, kg)
    return dq, dk, dv, dqg, dkg


def _place():
    return lax.axis_index("x"), lax.axis_index("y"), lax.axis_index("c")


def _all_gather8(x_shard, *, name):
    m_per, n = x_shard.shape

    def body(x_ref, out_ref, send_sems, recv_sems, local_sem):
        x, y, c = _place()
        me, sibling = (x, y, c), (x, y, 1 - c)
        chips = [(1 - x, y), (x, 1 - y), (1 - x, 1 - y)]

        def rows(px, py, pc):
            return out_ref.at[pl.ds((4 * px + 2 * py + pc) * m_per, m_per), :]

        def copy(k, block, to, src=None):
            return pltpu.make_async_remote_copy(
                src_ref=rows(*block) if src is None else src, dst_ref=rows(*block),
                send_sem=send_sems.at[k], recv_sem=recv_sems.at[k], device_id=to, device_id_type=MESH)

        mine = pltpu.make_async_copy(x_ref, rows(*me), local_sem)
        mine.start()
        first = [copy(0, me, sibling, src=x_ref)]
        first += [copy(1 + j, me, (*chip, c), src=x_ref) for j, chip in enumerate(chips)]
        for cp in first:
            cp.start()
        passed = [copy(4 + j, (*chip, c), sibling) for j, chip in enumerate(chips)]
        for j, chip in enumerate(chips):
            copy(1 + j, (*chip, c), me).wait_recv()
            passed[j].start()
        copy(0, sibling, me).wait_recv()
        for j, chip in enumerate(chips):
            copy(4 + j, (*chip, 1 - c), me).wait_recv()
        for cp in first + passed:
            cp.wait_send()
        mine.wait()

    return pl.pallas_call(
        body, name=name, out_shape=jax.ShapeDtypeStruct((8 * m_per, n), x_shard.dtype),
        in_specs=[pl.BlockSpec(memory_space=pltpu.VMEM)], out_specs=pl.BlockSpec(memory_space=pltpu.VMEM),
        scratch_shapes=[pltpu.SemaphoreType.DMA((7,)), pltpu.SemaphoreType.DMA((7,)), pltpu.SemaphoreType.DMA],
        compiler_params=pltpu.CompilerParams(vmem_limit_bytes=VMEM_LIMIT),
    )(x_shard)


def _sibling_sum_half(x, *, name):
    R, C = x.shape
    half = R // 2
    assert half % 16 == 0

    def body(x_ref, o_ref, theirs, send_sem, recv_sem):
        px, py, pc = _place()
        cp = pltpu.make_async_remote_copy(src_ref=x_ref, dst_ref=theirs, send_sem=send_sem, recv_sem=recv_sem,
                                          device_id=(px, py, 1 - pc), device_id_type=MESH)
        cp.start()
        cp.wait()
        rows = pl.ds(pl.multiple_of(pc * half, 8), half)
        o_ref[...] = (x_ref[rows, :] + theirs[rows, :]).astype(BF16)

    return pl.pallas_call(
        body, name=name, out_shape=jax.ShapeDtypeStruct((half, C), BF16),
        in_specs=[pl.BlockSpec(memory_space=pltpu.VMEM)], out_specs=pl.BlockSpec(memory_space=pltpu.VMEM),
        scratch_shapes=[pltpu.VMEM((R, C), x.dtype), pltpu.SemaphoreType.DMA, pltpu.SemaphoreType.DMA],
        compiler_params=pltpu.CompilerParams(vmem_limit_bytes=VMEM_LIMIT),
    )(x)


def _sum_blocks(x, n, *, name):
    R = x.shape[0] // n

    def body(x_ref, o_ref):
        acc = x_ref[pl.ds(0, R), :].astype(F32)
        for k in range(1, n):
            acc = acc + x_ref[pl.ds(k * R, R), :].astype(F32)
        o_ref[...] = acc

    return pl.pallas_call(body, name=name, out_shape=jax.ShapeDtypeStruct((R, x.shape[1]), F32),
                          compiler_params=pltpu.CompilerParams(vmem_limit_bytes=VMEM_LIMIT))(x)


def _colsum(x, *, name):
    def body(x_ref, o_ref):
        o_ref[...] = jnp.sum(x_ref[...], axis=0, keepdims=True)
    return pl.pallas_call(body, name=name, out_shape=jax.ShapeDtypeStruct((1, x.shape[1]), x.dtype))(x)


ANY = pl.BlockSpec(memory_space=pl.ANY)


class _ChipExchange:
    SCRATCH = [pltpu.SemaphoreType.DMA((3,)), pltpu.SemaphoreType.DMA((3,)), pltpu.SemaphoreType.DMA]

    @staticmethod
    def out_shape(src, scatter):
        return jax.ShapeDtypeStruct(((4,) + tuple(src.shape[1:])) if scatter else ((4, 2) + tuple(src.shape[1:])), src.dtype)

    def __init__(self, src_ref, out_ref, send_sems, recv_sems, local_sem, scatter):
        x, y, c = _place()
        myj = 2 * x + y
        chips = [(1 - x, y), (x, 1 - y), (1 - x, 1 - y)]

        def slot(j):
            return out_ref.at[j] if scatter else out_ref.at[j, c]

        def piece(j):
            return src_ref.at[j] if scatter else src_ref.at[c]

        self.mine = pltpu.make_async_copy(piece(myj), slot(myj), local_sem)
        self.sends = [pltpu.make_async_remote_copy(
            src_ref=piece(2 * cx + cy), dst_ref=slot(myj), send_sem=send_sems.at[k], recv_sem=recv_sems.at[k],
            device_id=(cx, cy, c), device_id_type=MESH) for k, (cx, cy) in enumerate(chips)]
        self.recvs = [pltpu.make_async_remote_copy(
            src_ref=slot(2 * cx + cy), dst_ref=slot(2 * cx + cy), send_sem=send_sems.at[k], recv_sem=recv_sems.at[k],
            device_id=(cx, cy, c), device_id_type=MESH) for k, (cx, cy) in enumerate(chips)]

    def start(self):
        self.mine.start()
        for cp in self.sends:
            cp.start()

    def wait(self):
        for cp in self.recvs:
            cp.wait_recv()
        for cp in self.sends:
            cp.wait_send()
        self.mine.wait()


def _sibling_fill(buf, *, axis, name):
    def half(ref, h):
        return ref.at[h] if axis == 0 else ref.at[:, h]

    def body(in_ref, out_ref, send_sem, recv_sem):
        x, y, c = _place()
        cp = pltpu.make_async_remote_copy(src_ref=half(out_ref, c), dst_ref=half(out_ref, c), send_sem=send_sem, recv_sem=recv_sem,
                                          device_id=(x, y, 1 - c), device_id_type=MESH)
        cp.start()
        pltpu.make_async_remote_copy(src_ref=half(out_ref, 1 - c), dst_ref=half(out_ref, 1 - c), send_sem=send_sem, recv_sem=recv_sem,
                                     device_id=(x, y, 1 - c), device_id_type=MESH).wait_recv()
        cp.wait_send()

    return pl.pallas_call(
        body, name=name, out_shape=jax.ShapeDtypeStruct(buf.shape, buf.dtype), in_specs=[ANY], out_specs=ANY,
        input_output_aliases={0: 0}, scratch_shapes=[pltpu.SemaphoreType.DMA, pltpu.SemaphoreType.DMA],
    )(buf)


def _sibling_swap_half(g, *, name):
    def body(g_ref, out_ref, send_sem, recv_sem):
        x, y, c = _place()
        cp = pltpu.make_async_remote_copy(src_ref=g_ref.at[:, 1 - c], dst_ref=out_ref, send_sem=send_sem, recv_sem=recv_sem,
                                          device_id=(x, y, 1 - c), device_id_type=MESH)
        cp.start()
        cp.wait()

    return pl.pallas_call(
        body, name=name, out_shape=jax.ShapeDtypeStruct((g.shape[0],) + g.shape[2:], g.dtype), in_specs=[ANY], out_specs=ANY,
        scratch_shapes=[pltpu.SemaphoreType.DMA, pltpu.SemaphoreType.DMA],
    )(g)


def _add_my_half(g, b, cidx, *, name, tr=1024):
    n, _, R, C = g.shape
    tr = max(t for t in range(16, tr + 1, 16) if R % t == 0)

    def body(c_ref, g_ref, b_ref, o_ref):
        o_ref[...] = (g_ref[...] + b_ref[...]).astype(o_ref.dtype)

    return pl.pallas_call(
        body, name=name, out_shape=jax.ShapeDtypeStruct((n, R, C), BF16),
        grid_spec=pltpu.PrefetchScalarGridSpec(
            num_scalar_prefetch=1, grid=(n, R // tr),
            in_specs=[pl.BlockSpec((None, None, tr, C), lambda j, i, c: (j, c[0], i, 0)),
                      pl.BlockSpec((None, tr, C), lambda j, i, c: (j, i, 0))],
            out_specs=pl.BlockSpec((None, tr, C), lambda j, i, c: (j, i, 0))),
        compiler_params=_cp(("parallel", "parallel")),
    )(cidx, g, b)


def _sum4_into_half(q, cidx, *, name, tr=1024):
    _, R, C = q.shape
    tr = max(t for t in range(16, tr + 1, 16) if R % t == 0)

    def body(c_ref, q_ref, o_ref):
        o_ref[...] = ((q_ref[0].astype(F32) + q_ref[1].astype(F32)) + q_ref[2].astype(F32)) + q_ref[3].astype(F32)

    return pl.pallas_call(
        body, name=name, out_shape=jax.ShapeDtypeStruct((2, R, C), F32),
        grid_spec=pltpu.PrefetchScalarGridSpec(
            num_scalar_prefetch=1, grid=(R // tr,),
            in_specs=[pl.BlockSpec((4, tr, C), lambda i, c: (0, i, 0))],
            out_specs=pl.BlockSpec((None, tr, C), lambda i, c: (c[0], i, 0))),
        compiler_params=_cp(("parallel",)),
    )(cidx, q)


def _pack_rows(parts, width=1024, row_multiple=8):
    rows, spans, r0 = [], [], 0
    for p in parts:
        n = p.size
        nr = 8 * (-(-n // (8 * width)))
        flat = p.reshape(-1)
        if nr * width != n:
            flat = jnp.pad(flat, (0, nr * width - n))
        rows.append(flat.reshape(nr, width))
        spans.append((r0, nr, n, p.shape))
        r0 += nr
    if r0 % row_multiple:
        rows.append(jnp.zeros((row_multiple - r0 % row_multiple, width), parts[0].dtype))
    return jnp.concatenate(rows, axis=0), spans


def _unpack_rows(buf, spans):
    return [buf[r0:r0 + nr].reshape(-1)[:n].reshape(shape) for (r0, nr, n, shape) in spans]


def kernel(x, c, ada_w, ada_b, mix_norm_g, mlp_norm_g, mlp_w1, mlp_w2, s5_a_re, s5_a_im, s5_log_dt, s5_b_re, s5_b_im, s5_c_re, s5_c_im, s5_d, s5_w_glu, kv_ada_w, kv_ada_b, kv_norm_g, w_kv, k_norm_g, sb_w_q, q_norm_g, sb_w_o, loss_target, m_ada_w, m_ada_b, m_mix_norm_g, m_mlp_norm_g, m_mlp_w1, m_mlp_w2, m_s5_a_re, m_s5_a_im, m_s5_log_dt, m_s5_b_re, m_s5_b_im, m_s5_c_re, m_s5_c_im, m_s5_d, m_s5_w_glu, m_kv_ada_w, m_kv_ada_b, m_kv_norm_g, m_w_kv, m_k_norm_g, m_sb_w_q, m_q_norm_g, m_sb_w_o, v_ada_w, v_ada_b, v_mix_norm_g, v_mlp_norm_g, v_mlp_w1, v_mlp_w2, v_s5_a_re, v_s5_a_im, v_s5_log_dt, v_s5_b_re, v_s5_b_im, v_s5_c_re, v_s5_c_im, v_s5_d, v_s5_w_glu, v_kv_ada_w, v_kv_ada_b, v_kv_norm_g, v_w_kv, v_k_norm_g, v_sb_w_q, v_q_norm_g, v_sb_w_o):
    E, S, D = x.shape
    T = E * S
    FF = 4 * D
    NB = 8 * E
    px, py, pc = _place()
    chip = 2 * px + py
    dev = 4 * px + 2 * py + pc
    cidx = jnp.reshape(pc, (1,)).astype(jnp.int32)
    x0 = x.reshape(T, D)
    tgt = loss_target.reshape(T, D)

    nc_rows, nd = c.size // 128, s5_d.size // 128
    cd = jnp.concatenate([c.reshape(nc_rows, 128), jnp.pad(s5_d.reshape(nd, 128), ((0, 8 - nd), (0, 0)))], axis=0)
    cd_all = _all_gather8(cd, name="ag_c_d").reshape(8, nc_rows + 8, 128)
    c_all = cd_all[:, :nc_rows].reshape(NB, D)
    d_full = cd_all.reshape(4, 2, nc_rows + 8, 128)[:, 0, nc_rows:nc_rows + nd].reshape(1, D)
    sc_all = (c_all * _sigmoid(c_all)).astype(BF16)
    wa = ada_w.shape[2]
    wk = kv_ada_w.shape[1]
    m_sh = jnp.concatenate([_mm(sc_all, _Layer(ada_w, 0), "nn", name="ada0", tn=256),
                            _mm(sc_all, _Layer(ada_w, 1), "nn", name="ada1", tn=256),
                            _mm(sc_all, kv_ada_w, "nn", name="ada_kv", tn=256)], axis=1)
    m_all = _all_gather8(m_sh, name="ag_m").reshape(4, 2, NB, 2 * wa + wk)[:, 0]
    mods = []
    for l in range(2):
        full = jnp.transpose(m_all[:, :, l * wa:(l + 1) * wa], (1, 0, 2)).reshape(NB, 6 * D) + ada_b[l]
        mine = lax.dynamic_slice_in_dim(full, E * dev, E, axis=0)
        mods.append([mine[:, i * D:(i + 1) * D].reshape(E, 1, D) for i in range(6)])
    full = jnp.transpose(m_all[:, :, 2 * wa:], (1, 0, 2)).reshape(NB, 2 * D) + kv_ada_b
    mine = lax.dynamic_slice_in_dim(full, E * dev, E, axis=0)
    kv_sh, kv_sc = [mine[:, i * D:(i + 1) * D].reshape(E, 1, D) for i in range(2)]

    wpack_a = jnp.concatenate([mlp_w1[0], mlp_w2[0], jnp.concatenate([s5_w_glu[0], w_kv], axis=1), sb_w_q[0]], axis=0).astype(BF16)
    wpack_b = jnp.concatenate([mlp_w1[1], mlp_w2[1], sb_w_o[0]], axis=0).astype(BF16)
    RA, RB = wpack_a.shape[0], wpack_b.shape[0]
    RW = RA + RB

    tm = min(2048, S)
    tm_res = min(1024, S)
    gbuf = [jax.ShapeDtypeStruct((4, RW, D), F32)]

    def grad_mm(act, dout, kind, roff, nr, c0, nc, name, transposed=False):
        gbuf[0] = _mm(act, dout, "nn" if transposed else "tn", name=name, tm=1024, tk=2048,
                      into=_Sharded(gbuf[0], kind, roff, nr, c0, nc))

    def mlp_fwd(xa, l, mod):
        sh_m, sc_m, g_m = mod[3], mod[4], mod[5]
        h, h_t = _norm_mod_fwd(xa, mlp_norm_g[l:l + 1], sh_m, sc_m, n_ex=E, out_dtype=BF16, name=f"mlp_norm{l}", with_transpose=True)

        def relu_sq(acc):
            ra = jnp.maximum(acc, 0.0)
            return ra * ra, ra
        r, ra = _mm(h, W1[l], "nn", name=f"mlp_up{l}", out_dtypes=(BF16, BF16), tm=tm, epilogue=relu_sq)
        xb, ff = _mm(r, W2[l], "nn", name=f"mlp_down{l}", out_dtypes=(F32, F32), tm=tm_res,
                     extras=[_mn_extra(xa), _vec_extra(g_m, S)],
                     epilogue=lambda acc, xat, gt: (xat + gt * acc, acc))
        return xb, (h_t, r, ra, ff)

    def mlp_bwd(dxb, xa, l, mod, saved):
        sc_m, g_m = mod[4], mod[5]
        h_t, r, ra, ff = saved
        (dff,), (dgm,) = _rowwise(lambda d, f, g: ([g * d], [_csum(d * f)]), [(dxb, D, 0), (ff, D, 0)], [g_m], [],
                                  [(D, BF16)], [D], n_ex=E, name=f"mlp_gate_bwd{l}")
        da = _mm(dff, W2[l], "nt", name=f"mlp_down_dx{l}", out_dtypes=(BF16,), tm=tm, extras=[_mn_extra(ra)],
                 epilogue=lambda acc, rat: (acc * (2.0 * rat.astype(F32)),))
        grad_mm(r, dff, "rows", (2 + l) * D, D, 0, D, f"mlp_down_dw{l}")
        dh = _mm(da, W1[l], "nt", name=f"mlp_up_dx{l}", tm=tm)
        grad_mm(h_t, da, "cols", l * D, D, 0, D, f"mlp_up_dw{l}", transposed=True)
        (dxa,), (dsh, dsc, dg) = _norm_mod_bwd(xa, dh, dxb, mlp_norm_g[l:l + 1], sc_m, n_ex=E, name=f"mlp_norm_bwd{l}")
        return dxa, (dsh, dsc, dgm), dg

    ab_re, ab_im, bb_re, bb_im = _s5_disc(s5_a_re[0], s5_a_im[0], s5_log_dt[0], s5_b_re[0], s5_b_im[0])
    cf, cr = _s5_consts(ab_re, ab_im)
    Wb, Wc = _s5_blockdiag(bb_re, bb_im, s5_c_re[0], s5_c_im[0])
    ng = D // U_LANES

    mod0, mod1 = mods
    h0 = _norm_mod_fwd(x0, mix_norm_g[0:1], mod0[0], mod0[1], n_ex=E, out_dtype=F32, name="mix_norm0")
    y, gy, gy_t, s5_states, wfull_a = _s5_fwd(h0, Wb, Wc, cf, d_full, wpack_a.reshape(2, RA // 2, D), n_ex=E, name="s5_fwd")
    wfull_a = _sibling_fill(wfull_a, axis=1, name="wgather_a_d2d").reshape(4, RA, D)

    W1 = [_Sharded(wfull_a, "cols", 0, D, 0, D), None]
    W2 = [_Sharded(wfull_a, "rows", D, D, 0, D), None]
    Wglu = _Sharded(wfull_a, "cols", 2 * D, D, 0, D // 2)
    Wkv = _Sharded(wfull_a, "cols", 2 * D, D, D // 2, D // 2)
    Wq = _Sharded(wfull_a, "rows", 3 * D, D // 4, 0, D)
    vg = _mm(gy, Wglu, "nn", name="glu_up", tm=tm)
    (x1,), _ = _rowwise(lambda v, g, xt, ga: ([xt + ga * (v * _sigmoid(g))], []),
                        [(vg, D, 0), (vg, D, 1), (x0, D, 0)], [mod0[2]], [], [(D, F32)], [], n_ex=E, name="glu_gate")
    x2, saved_mlp0 = mlp_fwd(x1, 0, mod0)

    hkv, hkv_t = _norm_mod_fwd(x2, kv_norm_g.reshape(1, D), kv_sh, kv_sc, n_ex=E, out_dtype=BF16, name="kv_norm", with_transpose=True)
    kvf = _mm(hkv, Wkv, "nn", name="kv_proj", tm=tm)
    h1, h1_t = _norm_mod_fwd(x2, mix_norm_g[1:2], mod1[0], mod1[1], n_ex=E, out_dtype=BF16, name="mix_norm1", with_transpose=True)
    qf = _mm(h1, Wq, "nn", name="q_proj", tm=tm)
    qg2 = jnp.tile(q_norm_g.reshape(1, HEAD_DIM), (1, 2))
    kg2 = jnp.tile(k_norm_g.reshape(1, HEAD_DIM), (1, 2))
    o, lf_tot, o_t, wfull_b = _attn_fwd(qf, kvf, qg2, kg2, wpack_b.reshape(2, RB // 2, D), n_ex=E, name="attn_fwd")
    wfull_b = _sibling_fill(wfull_b, axis=1, name="wgather_b_d2d").reshape(4, RB, D)
    W1[1] = _Sharded(wfull_b, "cols", 0, D, 0, D)
    W2[1] = _Sharded(wfull_b, "rows", D, D, 0, D)
    Wo = _Sharded(wfull_b, "rows", 2 * D, D // 4, 0, D)
    x3, mix1 = _mm(o, Wo, "nn", name="o_proj", out_dtypes=(F32, F32), tm=tm_res,
                   extras=[_mn_extra(x2), _vec_extra(mod1[2], S)],
                   epilogue=lambda acc, xat, gt: (xat + gt * acc, acc))
    x4, saved_mlp1 = mlp_fwd(x3, 1, mod1)

    (dx4,), (lsum,) = _rowwise(lambda xt, tt: ([(xt - tt) * (1.0 / D)], [_csum(jnp.square(xt - tt)) * (0.5 / D)]),
                               [(x4, D, 0), (tgt, D, 0)], [], [], [(D, F32)], [D], n_ex=E, name="loss")
    loss = lax.psum(jnp.sum(lsum), ("x", "y", "c"))

    dx3, (dsh_m1, dsc_m1, dgm1), dg_mlp1 = mlp_bwd(dx4, x3, 1, mod1, saved_mlp1)
    (dmix1,), (dga1,) = _rowwise(lambda d, f, g: ([g * d], [_csum(d * f)]), [(dx3, D, 0), (mix1, D, 0)], [mod1[2]], [],
                                 [(D, BF16)], [D], n_ex=E, name="attn_gate_bwd")
    do = _mm(dmix1, Wo, "nt", name="o_proj_dx", tm=tm)
    grad_mm(o_t, dmix1, "rows", 5 * D + D // 4, D // 4, 0, D, "o_proj_dw", transposed=True)
    dq, dk, dv, dqg, dkg = _attn_bwd(qf, kvf, lf_tot, do, qg2, kg2, n_ex=E, name="attn_bwd")
    dh1 = _mm(dq, Wq, "nt", name="q_proj_dx", tm=tm)
    grad_mm(h1_t, dq, "rows", 5 * D, D // 4, 0, D, "q_proj_dw", transposed=True)
    (dx2,), (dsh_a1, dsc_a1, dg_mix1) = _norm_mod_bwd(x2, dh1, dx3, mix_norm_g[1:2], mod1[1], n_ex=E, name="mix_norm_bwd1")
    dkv = jnp.concatenate([dk, dv], axis=1)
    dhkv = _mm(dkv, Wkv, "nt", name="kv_proj_dx", tm=tm)
    grad_mm(hkv_t, dkv, "cols", 4 * D, D, D // 2, D // 2, "kv_proj_dw", transposed=True)
    (dx2,), (dkv_sh, dkv_sc, dg_kv) = _norm_mod_bwd(x2, dhkv, dx2, kv_norm_g.reshape(1, D), kv_sc, n_ex=E, name="kv_norm_bwd")

    dx1, (dsh_m0, dsc_m0, dgm0), dg_mlp0 = mlp_bwd(dx2, x1, 0, mod0, saved_mlp0)

    def glu_bwd(v, g, d, ga):
        sg = _sigmoid(g)
        dm = ga * d
        return [jnp.concatenate([dm * sg, dm * v * sg * (1.0 - sg)], axis=1)], [_csum(d * (v * sg))]
    (dvg,), (dga0,) = _rowwise(glu_bwd, [(vg, D, 0), (vg, D, 1), (dx1, D, 0)], [mod0[2]], [], [(2 * D, BF16)], [D],
                               n_ex=E, name="glu_gate_bwd")
    dgy = _mm(dvg, Wglu, "nt", name="glu_up_dx", tm=tm)
    grad_mm(gy_t, dvg, "cols", 4 * D, D, 0, D // 2, "glu_up_dw", transposed=True)

    gpack = gbuf[0].reshape(4, 2, RW // 2, D)
    theirs = _sibling_swap_half(gpack, name="gscatter_d2d")
    chip_sum = _add_my_half(gpack, theirs, cidx, name="gscatter_add")
    dh0, dWb, dWc, dab, dd, from_chips = _s5_bwd(h0, y, dgy, s5_states, Wb, Wc, cr, d_full, chip_sum, n_ex=E, name="s5_bwd")
    ghalf = _sum4_into_half(from_chips, cidx, name="gscatter_sum")
    gsh = _sibling_fill(ghalf, axis=0, name="gscatter_fill").reshape(RW, D)
    (gx,), (dsh_a0, dsc_a0, dg_mix0) = _norm_mod_bwd(x0, dh0, dx1, mix_norm_g[0:1], mod0[1], n_ex=E, name="mix_norm_bwd0")
    grad_x = gx.reshape(E, S, D)

    dm_mine = jnp.concatenate([t.reshape(E, D) for t in
                               (dsh_a0, dsc_a0, dga0, dsh_m0, dsc_m0, dgm0, dsh_a1, dsc_a1, dga1, dsh_m1, dsc_m1, dgm1, dkv_sh, dkv_sc)], axis=1)
    dm_all = _all_gather8(dm_mine.reshape(8, -1), name="ag_dm").reshape(NB, 14 * D)
    sc_f32 = c_all * _sigmoid(c_all)
    g_ada_w = jax.ShapeDtypeStruct(ada_w.shape, F32)
    for l in range(2):
        g_ada_w = _mm(sc_f32, lax.dynamic_slice_in_dim(dm_all, l * 6 * D + chip * wa, wa, axis=1), "tn", name=f"ada_dw{l}", tn=256,
                      into=_Layer(g_ada_w, l))
    g_kv_ada_w = _mm(sc_f32, lax.dynamic_slice_in_dim(dm_all, 12 * D + chip * wk, wk, axis=1), "tn", name="ada_kv_dw", tn=256)
    db_all = _colsum(dm_all, name="ada_db")
    g_ada_b = db_all[0, :12 * D].reshape(2, 6 * D)
    g_kv_ada_b = db_all[0, 12 * D:]

    dWb_re, dWb_im, dC_re, dC_im = _s5_unblock(dWb, dWc)
    small_parts = [dg_mix0.sum(0), dg_mix1.sum(0), dg_mlp0.sum(0), dg_mlp1.sum(0), dg_kv.sum(0),
                   dqg.sum((0, 1, 2)).reshape(2, HEAD_DIM).sum(0), dkg.sum((0, 1, 2)).reshape(2, HEAD_DIM).sum(0),
                   dd[:, 0, :], dab[:, 0, :], dab[:, 1, :], dWb_re, dWb_im, dC_re, dC_im]
    spack, spans = _pack_rows(small_parts, row_multiple=32)
    chip_half = _sibling_sum_half(spack, name="small_d2d")
    ssum = _sum_blocks(_all_gather8(chip_half, name="ag_small"), 4, name="sum_small")
    (g_mix0, g_mix1, g_mlp0, g_mlp1, g_kvn, g_qn, g_kn, g_d, g_abr, g_abi, g_bbr, g_bbi, g_cre, g_cim) = _unpack_rows(ssum, spans)
    _, disc_vjp = jax.vjp(_s5_disc, s5_a_re[0], s5_a_im[0], s5_log_dt[0], s5_b_re[0], s5_b_im[0])
    g_are, g_aim, g_ldt, g_bre, g_bim = disc_vjp((g_abr.reshape(ab_re.shape), g_abi.reshape(ab_im.shape), g_bbr, g_bbi))
    g_s5d = lax.dynamic_slice_in_dim(g_d.reshape(1, D), chip * s5_d.shape[1], s5_d.shape[1], axis=1)

    def upd_big(w, m, v, roff, cb, name):
        shape = w.shape
        W = shape[-1]
        d_, m_, v_, g_ = _adamw2d(w.reshape(-1, W), gsh, m.reshape(-1, W), v.reshape(-1, W), name=name, g_roff=roff, g_cb=cb)
        return [t.reshape(shape) for t in (g_, d_, m_, v_)]

    def upd_own(w, g, m, v, name):
        shape = w.shape
        W = shape[-1]
        d_, m_, v_, g_ = _adamw2d(w.reshape(-1, W), g.reshape(-1, W), m.reshape(-1, W), v.reshape(-1, W), name=name)
        return [t.reshape(shape) for t in (g_, d_, m_, v_)]

    res = {}
    res["ada_w"] = upd_own(ada_w, g_ada_w, m_ada_w, v_ada_w, "adam_ada_w")
    res["kv_ada_w"] = upd_own(kv_ada_w, g_kv_ada_w, m_kv_ada_w, v_kv_ada_w, "adam_kv_ada_w")
    res["mlp_w1"] = upd_big(mlp_w1, m_mlp_w1, v_mlp_w1, 0, 0, "adam_w1")
    res["mlp_w2"] = upd_big(mlp_w2, m_mlp_w2, v_mlp_w2, 2 * D, 0, "adam_w2")
    res["s5_w_glu"] = upd_big(s5_w_glu, m_s5_w_glu, v_s5_w_glu, 4 * D, 0, "adam_glu")
    res["w_kv"] = upd_big(w_kv, m_w_kv, v_w_kv, 4 * D, 1, "adam_wkv")
    res["sb_w_q"] = upd_big(sb_w_q, m_sb_w_q, v_sb_w_q, 5 * D, 0, "adam_wq")
    res["sb_w_o"] = upd_big(sb_w_o, m_sb_w_o, v_sb_w_o, 5 * D + D // 4, 0, "adam_wo")

    small = {
        "ada_b": (ada_b, g_ada_b, m_ada_b, v_ada_b),
        "mix_norm_g": (mix_norm_g, jnp.stack([g_mix0, g_mix1]), m_mix_norm_g, v_mix_norm_g),
        "mlp_norm_g": (mlp_norm_g, jnp.stack([g_mlp0, g_mlp1]), m_mlp_norm_g, v_mlp_norm_g),
        "s5_a_re": (s5_a_re, g_are[None], m_s5_a_re, v_s5_a_re),
        "s5_a_im": (s5_a_im, g_aim[None], m_s5_a_im, v_s5_a_im),
        "s5_log_dt": (s5_log_dt, g_ldt[None], m_s5_log_dt, v_s5_log_dt),
        "s5_b_re": (s5_b_re, g_bre[None], m_s5_b_re, v_s5_b_re),
        "s5_b_im": (s5_b_im, g_bim[None], m_s5_b_im, v_s5_b_im),
        "s5_c_re": (s5_c_re, g_cre[None], m_s5_c_re, v_s5_c_re),
        "s5_c_im": (s5_c_im, g_cim[None], m_s5_c_im, v_s5_c_im),
        "s5_d": (s5_d, g_s5d, m_s5_d, v_s5_d),
        "kv_ada_b": (kv_ada_b, g_kv_ada_b, m_kv_ada_b, v_kv_ada_b),
        "kv_norm_g": (kv_norm_g, g_kvn, m_kv_norm_g, v_kv_norm_g),
        "k_norm_g": (k_norm_g, g_kn, m_k_norm_g, v_k_norm_g),
        "q_norm_g": (q_norm_g, g_qn.reshape(q_norm_g.shape), m_q_norm_g, v_q_norm_g),
    }
    names = list(small)
    packs = [_pack_rows([small[n][i].reshape(small[n][0].shape) for n in names]) for i in range(4)]
    sp = packs[0][1]
    d_, m_, v_, g_ = _adamw2d(packs[0][0], packs[1][0], packs[2][0], packs[3][0], name="adam_small")
    for n, gg, dd_, mm_, vv_ in zip(names, _unpack_rows(g_, sp), _unpack_rows(d_, sp), _unpack_rows(m_, sp), _unpack_rows(v_, sp)):
        res[n] = [gg, dd_, mm_, vv_]

    order = ["ada_w", "ada_b", "mix_norm_g", "mlp_norm_g", "mlp_w1", "mlp_w2", "s5_a_re", "s5_a_im", "s5_log_dt", "s5_b_re", "s5_b_im",
             "s5_c_re", "s5_c_im", "s5_d", "s5_w_glu", "kv_ada_w", "kv_ada_b", "kv_norm_g", "w_kv", "k_norm_g", "sb_w_q", "q_norm_g", "sb_w_o"]
    return (loss, grad_x, *[res[n][0] for n in order], *[res[n][1] for n in order], *[res[n][2] for n in order], *[res[n][3] for n in order])
```

```python
import functools
import math

import jax
import jax.numpy as jnp
from jax import lax
from jax.experimental import pallas as pl
from jax.experimental.pallas import tpu as pltpu

F32 = jnp.float32
BF16 = jnp.bfloat16
EPS = 1e-6
HEAD_DIM = 64
S5_GROUP = 16
S5_STATE = 64
GROUPS_PER_STEP = 8
U_LANES = GROUPS_PER_STEP * S5_GROUP
ST_LANES = GROUPS_PER_STEP * S5_STATE
SCAN_LANES = 256
SCAN_UNROLL = 4
VMEM_LIMIT = 56 * 1024 * 1024
ADAM_LR, ADAM_B1, ADAM_B2, ADAM_EPS, ADAM_WD, ADAM_STEP = 0.001, 0.9, 0.999, 1e-08, 0.01, 10
MESH = pl.DeviceIdType.MESH


def _cp(sem):
    return pltpu.CompilerParams(dimension_semantics=sem, vmem_limit_bytes=VMEM_LIMIT)


class _Sharded:
    def __init__(self, buf, kind, roff, nr, c0, nc):
        self.buf, self.kind, self.roff, self.nr, self.c0, self.nc = buf, kind, roff, nr, c0, nc
        self.shape = (nr, 4 * nc) if kind == "cols" else (4 * nr, nc)

    def operand(self, dims, tn, tk):
        roff, nr, c0, nc = self.roff, self.nr, self.c0, self.nc
        if self.kind == "cols" and dims == "nn":
            tk = min(tk, nr)
            assert roff % tk == 0
            return nc, tk, (None, tk, nc), lambda i, j, k: (j, roff // tk + k, c0 // nc)
        if self.kind == "cols":
            tn = min(tn, nr)
            assert roff % tn == 0
            return tn, nc, (None, tn, nc), lambda i, j, k: (k, roff // tn + j, c0 // nc)
        if dims == "nn":
            tn = min(tn, nc)
            assert roff % nr == 0 and c0 % tn == 0
            return tn, nr, (None, nr, tn), lambda i, j, k: (k, roff // nr, c0 // tn + j)
        tk = min(tk, nc)
        assert roff % nr == 0 and c0 % tk == 0
        return nr, tk, (None, nr, tk), lambda i, j, k: (j, roff // nr, c0 // tk + k)

    def result(self, tm, tn):
        roff, nr, c0, nc = self.roff, self.nr, self.c0, self.nc
        if self.kind == "cols":
            tm = min(tm, nr)
            assert roff % tm == 0
            return tm, nc, (None, tm, nc), lambda i, j, k: (j, roff // tm + i, c0 // nc)
        tm, tn = min(tm, nr), min(tn, nc)
        assert roff % tm == 0 and c0 % tn == 0
        per = nr // tm
        return tm, tn, (None, tm, tn), lambda i, j, k: (i // per, roff // tm + i % per, c0 // tn + j)


class _Layer:
    def __init__(self, buf, layer):
        self.buf, self.layer, self.shape = buf, layer, tuple(buf.shape[1:])

    def operand(self, dims, tn, tk):
        assert dims == "nn"
        layer = self.layer
        return tn, tk, (None, tk, tn), lambda i, j, k: (layer, k, j)

    def result(self, tm, tn):
        layer = self.layer
        return tm, tn, (None, tm, tn), lambda i, j, k: (layer, i, j)


def _mm(a, b, dims, *, name, out_dtypes=(F32,), epilogue=None, extras=(), tm=512, tn=1024, tk=1024, into=None):
    bshape = b.shape
    if dims == "nn":
        (M, K), (_, N) = a.shape, bshape
    elif dims == "nt":
        (M, K), (N, _) = a.shape, bshape
    else:
        (K, M), (_, N) = a.shape, bshape
    tm, tn, tk = min(tm, M), min(tn, N), min(tk, K)
    b_arr = b
    if into is not None:
        assert (M, N) == into.shape and len(out_dtypes) == 1 and not isinstance(b, _Sharded)
        tm, tn, o_blk, o_map = into.result(tm, tn)
        out_specs, out_shape = [pl.BlockSpec(o_blk, o_map)], [jax.ShapeDtypeStruct(into.buf.shape, into.buf.dtype)]
    if isinstance(b, (_Sharded, _Layer)):
        tn, tk, b_blk, b_map = b.operand(dims, tn, tk)
        b_spec, b_arr = pl.BlockSpec(b_blk, b_map), b.buf
    else:
        b_spec = pl.BlockSpec((tn, tk), lambda i, j, k: (j, k)) if dims == "nt" else pl.BlockSpec((tk, tn), lambda i, j, k: (k, j))
    if into is None:
        out_specs = [pl.BlockSpec((tm, tn), lambda i, j, k: (i, j)) for _ in out_dtypes]
        out_shape = [jax.ShapeDtypeStruct((M, N), d) for d in out_dtypes]
    assert M % tm == 0 and N % tn == 0 and K % tk == 0, (M, N, K, tm, tn, tk)
    nk = K // tk
    extras = [e(tm, tn) for e in extras]
    a_spec = pl.BlockSpec((tk, tm), lambda i, j, k: (k, i)) if dims == "tn" else pl.BlockSpec((tm, tk), lambda i, j, k: (i, k))
    contract = {"nn": ((1,), (0,)), "nt": ((1,), (1,)), "tn": ((0,), (0,))}[dims]
    n_ex, n_out = len(extras), len(out_dtypes)
    chain = [into.buf] if into is not None and not isinstance(into.buf, jax.ShapeDtypeStruct) else []
    n_in = n_ex + len(chain)

    def finish(r, ex, outs):
        res = epilogue(r, *[e[...] for e in ex]) if epilogue is not None else (r,)
        for o, v in zip(outs, res):
            o[...] = v.astype(o.dtype)

    def product(a_ref, b_ref):
        return lax.dot_general(a_ref[...].astype(BF16), b_ref[...].astype(BF16), (contract, ((), ())), preferred_element_type=F32)

    def body_one(a_ref, b_ref, *rest):
        finish(product(a_ref, b_ref), rest[:n_ex], rest[n_in:])

    def body_acc(a_ref, b_ref, *rest):
        ex, outs, acc = rest[:n_ex], rest[n_in:n_in + n_out], rest[-1]
        k = pl.program_id(2)

        @pl.when(k == 0)
        def _():
            acc[...] = product(a_ref, b_ref)

        @pl.when(jnp.logical_and(k > 0, k < nk - 1))
        def _():
            acc[...] += product(a_ref, b_ref)

        @pl.when(k == nk - 1)
        def _():
            finish(acc[...] + product(a_ref, b_ref), ex, outs)

    out = pl.pallas_call(
        body_one if nk == 1 else body_acc, name=name, grid=(M // tm, N // tn, nk),
        in_specs=[a_spec, b_spec] + [pl.BlockSpec(blk, im) for (_, blk, im) in extras] + [ANY for _ in chain],
        out_specs=out_specs, out_shape=out_shape,
        input_output_aliases={2 + n_ex: 0} if chain else {},
        scratch_shapes=[] if nk == 1 else [pltpu.VMEM((tm, tn), F32)],
        compiler_params=_cp(("parallel", "parallel", "arbitrary")),
    )(a, b_arr, *[e[0] for e in extras], *chain)
    return out if n_out > 1 else out[0]


def _mn_extra(arr):
    return lambda tm, tn: (arr, (tm, tn), lambda i, j, k: (i, j))


def _vec_extra(vec, S):
    return lambda tm, tn: (vec, (None, 1, tn), lambda i, j, k: ((i * tm) // S, 0, j))


def _rowwise(fn, rows, vecs=(), consts=(), out_rows=(), out_sums=(), *, n_ex, name, tr=512):
    rows = [r if len(r) == 4 else (*r, 0) for r in rows]
    S = min(r[0].shape[0] for r in rows if r[3] == 0) // n_ex
    tr = math.gcd(tr, S)
    assert S % tr == 0
    nb = S // tr
    in_specs = []
    for (arr, w, cb, roff) in rows:
        assert roff % tr == 0
        in_specs.append(pl.BlockSpec((tr, w), functools.partial(lambda e, i, cb, ro: (e * nb + i + ro, cb), cb=cb, ro=roff // tr)))
    for v in vecs:
        in_specs.append(pl.BlockSpec((None, 1, v.shape[-1]), lambda e, i: (e, 0, 0)))
    for c in consts:
        in_specs.append(pl.BlockSpec((1, c.shape[-1]), lambda e, i: (0, 0)))
    n_in, n_or, n_os = len(in_specs), len(out_rows), len(out_sums)
    flipped = [len(o) == 3 and o[2] for o in out_rows]
    out_specs = [pl.BlockSpec((o[0], tr), lambda e, i: (0, e * nb + i)) if f else pl.BlockSpec((tr, o[0]), lambda e, i: (e * nb + i, 0))
                 for o, f in zip(out_rows, flipped)]
    out_specs += [pl.BlockSpec((None, 1, w), lambda e, i: (e, 0, 0)) for w in out_sums]
    out_shape = [jax.ShapeDtypeStruct((o[0], n_ex * S) if f else (n_ex * S, o[0]), o[1]) for o, f in zip(out_rows, flipped)]
    out_shape += [jax.ShapeDtypeStruct((n_ex, 1, w), F32) for w in out_sums]

    def body(*refs):
        ins, o_r, o_s = refs[:n_in], refs[n_in:n_in + n_or], refs[n_in + n_or:]
        ro, so = fn(*[r[...] for r in ins])
        for o, v, f in zip(o_r, ro, flipped):
            o[...] = (v.T if f else v).astype(o.dtype)
        i = pl.program_id(1)
        for o, v in zip(o_s, so):
            @pl.when(i == 0)
            def _(o=o, v=v):
                o[...] = v

            @pl.when(i > 0)
            def _(o=o, v=v):
                o[...] += v

    outs = pl.pallas_call(
        body, name=name, grid=(n_ex, nb), in_specs=in_specs, out_specs=out_specs, out_shape=out_shape,
        compiler_params=_cp(("parallel", "arbitrary")),
    )(*[r[0] for r in rows], *vecs, *consts)
    return outs[:n_or], outs[n_or:]


def _csum(x):
    return jnp.sum(x, axis=0, keepdims=True)


def _norm_mod_fwd(x, g, sh, sc, *, n_ex, out_dtype, name, with_transpose=False):
    def fn(xt, sht, sct, gt):
        r = lax.rsqrt(jnp.mean(xt * xt, axis=-1, keepdims=True) + EPS)
        h = (xt * r * gt) * (1.0 + sct) + sht
        return [h, h] if with_transpose else [h], []
    D = x.shape[1]
    outs = [(D, out_dtype), (D, out_dtype, True)] if with_transpose else [(D, out_dtype)]
    res = _rowwise(fn, [(x, D, 0)], [sh, sc], [g], outs, [], n_ex=n_ex, name=name)[0]
    return res if with_transpose else res[0]


def _gated(dx, branch, gate):
    return (gate * dx).astype(BF16), _csum(dx * branch)


def _norm_mod_bwd(x, dh, dres, g, sc, *, n_ex, name, gated=None):
    def fn(xt, dht, drt, *rest):
        sct, gt = rest[-2], rest[-1]
        dht = dht.astype(F32)
        r = lax.rsqrt(jnp.mean(xt * xt, axis=-1, keepdims=True) + EPS)
        n = xt * r
        y = n * gt
        dy = dht * (1.0 + sct)
        dn = dy * gt
        dx = drt + r * (dn - n * jnp.mean(dn * n, axis=-1, keepdims=True))
        rows, sums = [dx], [_csum(dht), _csum(dht * y), _csum(dy * n)]
        if gated is not None:
            dbranch, dgate = _gated(dx, rest[0], rest[1])
            rows, sums = rows + [dbranch], sums + [dgate]
        return rows, sums
    D = x.shape[1]
    extra_rows = [(gated[0], D, 0)] if gated is not None else []
    extra_vecs = [gated[1]] if gated is not None else []
    return _rowwise(fn, [(x, D, 0), (dh, D, 0), (dres, D, 0)] + extra_rows, extra_vecs + [sc], [g],
                    [(D, F32)] + ([(D, BF16)] if gated is not None else []), [D, D, D] + ([D] if gated is not None else []),
                    n_ex=n_ex, name=name)


def _sigmoid(x):
    return 1.0 / (1.0 + jnp.exp(-x))


def _gelu(y):
    return 0.5 * y * (1.0 + jnp.tanh(0.7978845608028654 * (y + 0.044715 * y * y * y)))


def _gelu_grad(y):
    t = jnp.tanh(0.7978845608028654 * (y + 0.044715 * y * y * y))
    return 0.5 * (1.0 + t) + 0.5 * y * (1.0 - t * t) * 0.7978845608028654 * (1.0 + 3 * 0.044715 * y * y)


def _adamw_fn(w, g, m, v):
    m2 = ADAM_B1 * m + (1.0 - ADAM_B1) * g
    v2 = ADAM_B2 * v + (1.0 - ADAM_B2) * (g * g)
    m_hat = m2 / (1.0 - ADAM_B1 ** ADAM_STEP)
    v_hat = v2 / (1.0 - ADAM_B2 ** ADAM_STEP)
    delta = -ADAM_LR * (m_hat / (jnp.sqrt(v_hat) + ADAM_EPS) + ADAM_WD * w)
    return delta, m2, v2


def _adamw2d(w, g, m, v, *, name, g_roff=0, g_cb=0):
    R, W = w.shape

    def fn(wt, gt, mt, vt):
        d, m2, v2 = _adamw_fn(wt, gt, mt, vt)
        return [d, m2, v2, gt], []
    return _rowwise(fn, [(w, W, 0), (g, W, g_cb, g_roff), (m, W, 0), (v, W, 0)], [], [],
                    [(W, F32)] * 4, [], n_ex=1, name=name, tr=256)[0]


def _scan_tiles(re_ref, im_ref, cf, lane0, n_chunks, reverse, extra=None):
    L = SCAN_LANES
    lanes = pl.ds(lane0, L)
    A = [cf[i, :, lanes] for i in range(8)]
    shifts = (7, 6, 4) if reverse else (1, 2, 4)
    edge = 0 if reverse else 7

    U = SCAN_UNROLL
    n_groups = n_chunks // U

    def body(c, carry):
        first = ((n_groups - 1 - c) if reverse else c) * U
        rows = pl.ds(pl.multiple_of(first * 8, 8 * U), 8 * U)
        big_r, big_i = re_ref[rows, lanes], im_ref[rows, lanes]
        tiles = []
        for u in range(U):
            xr, xi = big_r[8 * u:8 * u + 8, :], big_i[8 * u:8 * u + 8, :]
            for idx, sft in enumerate(shifts):
                ar, ai = A[2 * idx], A[2 * idx + 1]
                rr, ri = pltpu.roll(xr, sft, 0), pltpu.roll(xi, sft, 0)
                xr, xi = xr + ar * rr - ai * ri, xi + ar * ri + ai * rr
            tiles.append((xr, xi))
        pr, pi = A[6], A[7]
        cr, ci = carry[0], carry[1]
        for u in (range(U - 1, -1, -1) if reverse else range(U)):
            xr, xi = tiles[u]
            xr, xi = xr + pr * cr - pi * ci, xi + pr * ci + pi * cr
            tiles[u] = (xr, xi)
            cr, ci = jnp.broadcast_to(xr[edge:edge + 1, :], (8, L)), jnp.broadcast_to(xi[edge:edge + 1, :], (8, L))
        re_ref[rows, lanes] = jnp.concatenate([t[0] for t in tiles], axis=0)
        im_ref[rows, lanes] = jnp.concatenate([t[1] for t in tiles], axis=0)
        return (cr, ci) if extra is None else (cr, ci) + extra(first, tiles, carry[2:])

    assert n_chunks % U == 0
    z = jnp.zeros((8, L), F32)
    init = (z, z) if extra is None else (z, z, z, z)
    return lax.fori_loop(0, n_groups, body, init)


def _s5_consts(ab_re, ab_im):
    ng = ab_re.shape[0] // GROUPS_PER_STEP
    ar, ai = ab_re.reshape(ng, 1, ST_LANES), ab_im.reshape(ng, 1, ST_LANES)

    def cmul(xr, xi, yr, yi):
        return xr * yr - xi * yi, xr * yi + xi * yr

    def build(ar, ai, reverse):
        pw = [(ar, ai)]
        for _ in range(7):
            pw.append(cmul(*pw[-1], ar, ai))
        row = jnp.arange(8).reshape(1, 8, 1)
        tiles = []
        for k in (1, 2, 4):
            keep = (row <= 7 - k) if reverse else (row >= k)
            tiles += [jnp.where(keep, pw[k - 1][0], 0.0), jnp.where(keep, pw[k - 1][1], 0.0)]
        order = [7 - r for r in range(8)] if reverse else list(range(8))
        tiles += [jnp.concatenate([pw[o][0] for o in order], axis=1), jnp.concatenate([pw[o][1] for o in order], axis=1)]
        return jnp.stack([jnp.broadcast_to(t, (ng, 8, ST_LANES)) for t in tiles], axis=1)

    return build(ar, ai, False), build(ar, -ai, True)


def _s5_blockdiag(bb_re, bb_im, c_re, c_im):
    G = bb_re.shape[0]
    ng = G // GROUPS_PER_STEP
    eye = jnp.eye(GROUPS_PER_STEP, dtype=F32)

    def wb(bb):
        return jnp.einsum("bgph,gk->bghkp", bb.reshape(ng, GROUPS_PER_STEP, S5_STATE, S5_GROUP), eye).reshape(ng, U_LANES, ST_LANES)

    def wc(cc):
        return jnp.einsum("bghp,gk->bkpgh", cc.reshape(ng, GROUPS_PER_STEP, S5_GROUP, S5_STATE), eye).reshape(ng, ST_LANES, U_LANES)

    Wb = jnp.concatenate([wb(bb_re), wb(bb_im)], axis=2).astype(BF16)
    Wc = jnp.concatenate([wc(c_re), -wc(c_im)], axis=1).astype(BF16)
    return Wb, Wc


def _s5_unblock(dWb, dWc):
    ng = dWb.shape[0]
    eye = jnp.eye(GROUPS_PER_STEP, dtype=F32)

    def ub(w):
        return jnp.einsum("bghkp,gk->bgph", w.reshape(ng, GROUPS_PER_STEP, S5_GROUP, GROUPS_PER_STEP, S5_STATE), eye).reshape(-1, S5_STATE, S5_GROUP)

    def uc(w):
        return jnp.einsum("bkpgh,gk->bghp", w.reshape(ng, GROUPS_PER_STEP, S5_STATE, GROUPS_PER_STEP, S5_GROUP), eye).reshape(-1, S5_GROUP, S5_STATE)

    return ub(dWb[:, :, :ST_LANES]), ub(dWb[:, :, ST_LANES:]), uc(dWc[:, :ST_LANES, :]), -uc(dWc[:, ST_LANES:, :])


def _s5_disc(a_re, a_im, log_dt, b_re, b_im):
    dt = jnp.exp(log_dt)[:, None]
    mag = jnp.exp(a_re * dt)
    ab_re = mag * jnp.cos(a_im * dt)
    ab_im = mag * jnp.sin(a_im * dt)
    den = a_re * a_re + a_im * a_im
    nr, ni = ab_re - 1, ab_im
    f_re = (nr * a_re + ni * a_im) / den
    f_im = (ni * a_re - nr * a_im) / den
    bb_re = f_re[..., None] * b_re - f_im[..., None] * b_im
    bb_im = f_re[..., None] * b_im + f_im[..., None] * b_re
    return ab_re, ab_im, bb_re, bb_im


ROW_CHUNK = 512


def _s5_fwd(u, Wb, Wc, cf, d, xsrc, *, n_ex, name):
    T, D = u.shape
    S = T // n_ex
    ng = D // U_LANES
    rc = min(ROW_CHUNK, S)

    def body(u_ref, wb_ref, wc_ref, cf_ref, d_ref, xsrc_ref, y_ref, gy_ref, gyt_ref, st_ref, xout_ref, re_s, im_s, *sems):
        step = pl.program_id(0) * ng + pl.program_id(1)
        exch = _ChipExchange(xsrc_ref, xout_ref, *sems, scatter=False)

        @pl.when(step == 0)
        def _():
            exch.start()

        for r in range(S // rc):
            rows = pl.ds(r * rc, rc)
            bu = jnp.dot(u_ref[rows, :].astype(BF16), wb_ref[...], preferred_element_type=F32)
            re_s[rows, :] = bu[:, :ST_LANES]
            im_s[rows, :] = bu[:, ST_LANES:]
        for l0 in range(0, ST_LANES, SCAN_LANES):
            _scan_tiles(re_s, im_s, cf_ref, l0, S // 8, False)
        for r in range(S // rc):
            rows = pl.ds(r * rc, rc)
            st = jnp.concatenate([re_s[rows, :], im_s[rows, :]], axis=1).astype(BF16)
            st_ref[rows, :] = st
            y = jnp.dot(st, wc_ref[...], preferred_element_type=F32) + d_ref[...] * u_ref[rows, :]
            y_ref[rows, :] = y
            gy = _gelu(y)
            gy_ref[rows, :] = gy.astype(BF16)
            gyt_ref[:, rows] = gy.T.astype(BF16)

        @pl.when(step == n_ex * ng - 1)
        def _():
            exch.wait()

    return pl.pallas_call(
        body, name=name, grid=(n_ex, ng),
        in_specs=[pl.BlockSpec((S, U_LANES), lambda e, g: (e, g)),
                  pl.BlockSpec((None, U_LANES, 2 * ST_LANES), lambda e, g: (g, 0, 0)),
                  pl.BlockSpec((None, 2 * ST_LANES, U_LANES), lambda e, g: (g, 0, 0)),
                  pl.BlockSpec((None, 8, 8, ST_LANES), lambda e, g: (g, 0, 0, 0)),
                  pl.BlockSpec((1, U_LANES), lambda e, g: (0, g)), ANY],
        out_specs=[pl.BlockSpec((S, U_LANES), lambda e, g: (e, g))] * 2 + [pl.BlockSpec((U_LANES, S), lambda e, g: (g, e)),
                   pl.BlockSpec((S, 2 * ST_LANES), lambda e, g: (e, g)), ANY],
        out_shape=[jax.ShapeDtypeStruct((T, D), F32), jax.ShapeDtypeStruct((T, D), BF16), jax.ShapeDtypeStruct((D, T), BF16),
                   jax.ShapeDtypeStruct((T, ng * 2 * ST_LANES), BF16), _ChipExchange.out_shape(xsrc, False)],
        scratch_shapes=[pltpu.VMEM((S, ST_LANES), F32)] * 2 + _ChipExchange.SCRATCH,
        compiler_params=_cp(("arbitrary", "arbitrary")),
    )(u, Wb, Wc, cf, d, xsrc)


def _s5_bwd(u, y, dgy, st, Wb, Wc, cr, d, xsrc, *, n_ex, name):
    T, D = u.shape
    S = T // n_ex
    ng = D // U_LANES
    rc = min(ROW_CHUNK, S)
    nch = S // 8
    grp = 8 * SCAN_UNROLL
    assert grp % 16 == 0

    def body(u_ref, y_ref, dgy_ref, st_ref, wb_ref, wc_ref, cr_ref, d_ref, xsrc_ref,
             du_ref, dwb_ref, dwc_ref, dab_ref, dd_ref, xout_ref, gr_s, gi_s, dy_s, *sems):
        e = pl.program_id(1)
        step = pl.program_id(0) * n_ex + e
        exch = _ChipExchange(xsrc_ref, xout_ref, *sems, scatter=True)

        @pl.when(step == 0)
        def _():
            exch.start()

        @pl.when(e == 0)
        def _():
            dwb_ref[...] = jnp.zeros_like(dwb_ref)
            dwc_ref[...] = jnp.zeros_like(dwc_ref)
            dab_ref[...] = jnp.zeros_like(dab_ref)
            dd_ref[...] = jnp.zeros_like(dd_ref)

        dd = jnp.zeros((1, U_LANES), F32)
        for r in range(S // rc):
            rows = pl.ds(r * rc, rc)
            ut = u_ref[rows, :]
            dy = dgy_ref[rows, :].astype(F32) * _gelu_grad(y_ref[rows, :])
            dy_s[rows, :] = dy
            dd = dd + _csum(dy * ut)
            go = lax.dot_general(dy.astype(BF16), wc_ref[...], (((1,), (1,)), ((), ())), preferred_element_type=F32)
            gr_s[rows, :] = go[:, :ST_LANES]
            gi_s[rows, :] = go[:, ST_LANES:]
        dd_ref[0:1, :] += dd
        row0 = lax.broadcasted_iota(jnp.int32, (8, SCAN_LANES), 0) == 0
        for l0 in range(0, ST_LANES, SCAN_LANES):
            lanes = pl.ds(l0, SCAN_LANES)

            def dab_group(first, tiles, acc, l0=l0):
                def states(r0, n, lane0):
                    return st_ref[pl.ds(pl.multiple_of(r0, 16), n), pl.ds(lane0, SCAN_LANES)].astype(F32)
                r0 = first * 8
                cur = states(r0, grp, l0), states(r0, grp, ST_LANES + l0)
                live = (first > 0).astype(F32)
                p0 = jnp.maximum(r0 - 16, 0)
                before = [states(p0, 16, l0)[8:16, :] * live, states(p0, 16, ST_LANES + l0)[8:16, :] * live]
                a_re, a_im = acc
                for t, (gr, gi) in enumerate(tiles):
                    here = [c[8 * t:8 * t + 8, :] for c in cur]
                    sr, si = [jnp.where(row0, pltpu.roll(b, 1, 0), pltpu.roll(h, 1, 0)) for b, h in zip(before, here)]
                    a_re, a_im = a_re + gr * sr + gi * si, a_im + gi * sr - gr * si
                    before = here
                return a_re, a_im

            res = _scan_tiles(gr_s, gi_s, cr_ref, l0, nch, True, extra=dab_group)
            dab_ref[0:1, lanes] += _csum(res[2])
            dab_ref[1:2, lanes] += _csum(res[3])
        for r in range(S // rc):
            rows = pl.ds(r * rc, rc)
            st = st_ref[rows, :]
            g = jnp.concatenate([gr_s[rows, :], gi_s[rows, :]], axis=1).astype(BF16)
            dyb = dy_s[rows, :].astype(BF16)
            dwc_ref[...] += lax.dot_general(st, dyb, (((0,), (0,)), ((), ())), preferred_element_type=F32)
            dwb_ref[...] += lax.dot_general(u_ref[rows, :].astype(BF16), g, (((0,), (0,)), ((), ())), preferred_element_type=F32)
            du = lax.dot_general(g, wb_ref[...], (((1,), (1,)), ((), ())), preferred_element_type=F32)
            du_ref[rows, :] = du + d_ref[...] * dy_s[rows, :]

        @pl.when(step == ng * n_ex - 1)
        def _():
            exch.wait()

    return pl.pallas_call(
        body, name=name, grid=(ng, n_ex),
        in_specs=[pl.BlockSpec((S, U_LANES), lambda g, e: (e, g))] * 3 + [
            pl.BlockSpec((S, 2 * ST_LANES), lambda g, e: (e, g)),
            pl.BlockSpec((None, U_LANES, 2 * ST_LANES), lambda g, e: (g, 0, 0)),
            pl.BlockSpec((None, 2 * ST_LANES, U_LANES), lambda g, e: (g, 0, 0)),
            pl.BlockSpec((None, 8, 8, ST_LANES), lambda g, e: (g, 0, 0, 0)),
            pl.BlockSpec((1, U_LANES), lambda g, e: (0, g)), ANY],
        out_specs=[pl.BlockSpec((S, U_LANES), lambda g, e: (e, g)),
                   pl.BlockSpec((None, U_LANES, 2 * ST_LANES), lambda g, e: (g, 0, 0)),
                   pl.BlockSpec((None, 2 * ST_LANES, U_LANES), lambda g, e: (g, 0, 0)),
                   pl.BlockSpec((None, 8, ST_LANES), lambda g, e: (g, 0, 0)),
                   pl.BlockSpec((None, 8, U_LANES), lambda g, e: (g, 0, 0)), ANY],
        out_shape=[jax.ShapeDtypeStruct((T, D), F32),
                   jax.ShapeDtypeStruct((ng, U_LANES, 2 * ST_LANES), F32),
                   jax.ShapeDtypeStruct((ng, 2 * ST_LANES, U_LANES), F32),
                   jax.ShapeDtypeStruct((ng, 8, ST_LANES), F32),
                   jax.ShapeDtypeStruct((ng, 8, U_LANES), F32), _ChipExchange.out_shape(xsrc, True)],
        scratch_shapes=[pltpu.VMEM((S, ST_LANES), F32)] * 2 + [pltpu.VMEM((S, U_LANES), F32)] + _ChipExchange.SCRATCH,
        compiler_params=_cp(("arbitrary", "arbitrary")),
    )(u, y, dgy, st, Wb, Wc, cr, d, xsrc)


TQ = 256
KW = 512
SUB = 128


def _head_masks():
    lane = lax.broadcasted_iota(jnp.int32, (1, 2 * HEAD_DIM), 1)
    m0 = (lane < HEAD_DIM).astype(F32)
    return m0, 1.0 - m0


def _head_norm(x, g, m0, m1):
    sq = x * x
    r0 = lax.rsqrt(jnp.sum(sq * m0, axis=-1, keepdims=True) / HEAD_DIM + EPS)
    r1 = lax.rsqrt(jnp.sum(sq * m1, axis=-1, keepdims=True) / HEAD_DIM + EPS)
    r = m0 * r0 + m1 * r1
    return x * r, r


def _head_norm_bwd(dy, n, r, g, m0, m1):
    dn = dy * g
    p = dn * n
    mean = (m0 * jnp.sum(p * m0, axis=-1, keepdims=True) + m1 * jnp.sum(p * m1, axis=-1, keepdims=True)) / HEAD_DIM
    return r * (dn - n * mean), _csum(dy * n)


def _pair_matrix(kind):
    r = lax.broadcasted_iota(jnp.int32, (2 * SUB, 2 * SUB), 0)
    c = lax.broadcasted_iota(jnp.int32, (2 * SUB, 2 * SUB), 1)
    same = (r < SUB) == (c < SUB)
    rel = {"after": r > c, "upto": r <= c, "before": r < c}[kind]
    return jnp.logical_and(same, rel).astype(BF16)


def _block_sums(x, mat, carry, reverse, terms=2):
    hi = x.astype(BF16)
    lo = (x - hi.astype(F32)).astype(BF16) if terms == 2 else None
    npair = x.shape[1] // (2 * SUB)
    parts = [None] * (2 * npair)
    for p in (range(npair - 1, -1, -1) if reverse else range(npair)):
        sl = slice(2 * SUB * p, 2 * SUB * (p + 1))
        loc = jnp.dot(hi[:, sl], mat, preferred_element_type=F32)
        if terms == 2:
            loc = loc + jnp.dot(lo[:, sl], mat, preferred_element_type=F32)
        for b in ((1, 0) if reverse else (0, 1)):
            k = 2 * p + b
            parts[k] = loc[:, SUB * b:SUB * (b + 1)] + carry
            carry = carry + jnp.sum(x[:, SUB * k:SUB * (k + 1)], axis=-1, keepdims=True)
    return jnp.concatenate(parts, axis=1), carry


def _sb_logits(z, mask):
    lp = jnp.minimum(z, 0.0) - jnp.log(1.0 + jnp.exp(-jnp.abs(z)))
    lf = lp - z
    if mask is not None:
        lf = jnp.where(mask, lf, 0.0)
    return lp, lf


def _causal_mask(row0, col0, kw):
    r = row0 + lax.broadcasted_iota(jnp.int32, (TQ, kw), 0)
    c = col0 + lax.broadcasted_iota(jnp.int32, (TQ, kw), 1)
    return c < r


def _transposed_windows(x, ref):
    for w in range(x.shape[0] // KW):
        ref[w] = x[w * KW:(w + 1) * KW, :].T.astype(BF16)


def _attn_fwd(q, kv, qg, kg, xsrc, *, n_ex, name):
    T, D = q.shape
    S = T // n_ex
    nhp = D // (2 * HEAD_DIM)
    nq = S // TQ
    scale = 1.0 / math.sqrt(HEAD_DIM)

    def body(q_ref, k_ref, v_ref, qg_ref, kg_ref, xsrc_ref, o_ref, tot_ref, ot_ref, xout_ref, kT_s, qm_s, vm_s, *sems):
        step = pl.program_id(0) * nhp + pl.program_id(1)
        exch = _ChipExchange(xsrc_ref, xout_ref, *sems, scatter=False)

        @pl.when(step == 0)
        def _():
            exch.start()

        m0, m1 = _head_masks()
        qn, _ = _head_norm(q_ref[...], None, m0, m1)
        qn = qn * (qg_ref[...] * scale)
        kn, _ = _head_norm(k_ref[...], None, m0, m1)
        _transposed_windows(kn * kg_ref[...], kT_s)
        v = v_ref[...]
        for h, m in enumerate((m0, m1)):
            qm_s[h] = (qn * m).astype(BF16)
            vm_s[h] = (v * m).astype(BF16)
        u_after = _pair_matrix("after")

        def window(rows, win, st, mask, kw):
            keys = pl.ds(pl.multiple_of(win * KW, KW), kw)
            zs = [jnp.dot(qm_s[h, rows, :], kT_s[win, :, :kw], preferred_element_type=F32) for h in range(2)]
            lg = [_sb_logits(zs[h], mask) for h in range(2)]
            sums = [_block_sums(lg[h][1], u_after, st[2 * h], True) for h in range(2)]
            out = ()
            for h in range(2):
                w = jnp.exp(lg[h][0] + sums[h][0])
                if mask is not None:
                    w = jnp.where(mask, w, 0.0)
                out += (sums[h][1], st[2 * h + 1] + jnp.dot(w.astype(BF16), vm_s[h, keys, :], preferred_element_type=F32))
            return out

        def qtile(iq, last, kw):
            rows = pl.ds(pl.multiple_of(iq * TQ, TQ), TQ)
            mask = _causal_mask(iq * TQ, last * KW, kw)
            z1, zq = jnp.zeros((TQ, 1), F32), jnp.zeros((TQ, 2 * HEAD_DIM), F32)
            st = window(rows, last, (z1, zq, z1, zq), mask, kw)
            st = lax.fori_loop(0, last, lambda jj, st: window(rows, last - 1 - jj, st, None, KW), st)
            o_ref[rows, :] = st[1] + st[3]
            tot_ref[rows, :] = st[0] * m0 + st[2] * m1

        def qtiles_of_window(a, _):
            for sub in range(KW // TQ):
                qtile(a * (KW // TQ) + sub, a, (sub + 1) * TQ)
            return 0

        lax.fori_loop(0, S // KW, qtiles_of_window, 0)
        ot_ref[...] = o_ref[...].T.astype(BF16)

        @pl.when(step == n_ex * nhp - 1)
        def _():
            exch.wait()

    assert S % KW == 0 and KW % TQ == 0
    nwin = S // KW
    blk = (S, 2 * HEAD_DIM)
    return pl.pallas_call(
        body, name=name, grid=(n_ex, nhp),
        in_specs=[pl.BlockSpec(blk, lambda e, h: (e, h)), pl.BlockSpec(blk, lambda e, h: (e, h)),
                  pl.BlockSpec(blk, lambda e, h: (e, h + nhp)),
                  pl.BlockSpec((1, 2 * HEAD_DIM), lambda e, h: (0, 0)), pl.BlockSpec((1, 2 * HEAD_DIM), lambda e, h: (0, 0)), ANY],
        out_specs=[pl.BlockSpec(blk, lambda e, h: (e, h))] * 2 + [pl.BlockSpec((2 * HEAD_DIM, S), lambda e, h: (h, e)), ANY],
        out_shape=[jax.ShapeDtypeStruct((T, D), F32)] * 2 + [jax.ShapeDtypeStruct((D, T), BF16), _ChipExchange.out_shape(xsrc, False)],
        scratch_shapes=[pltpu.VMEM((nwin, 2 * HEAD_DIM, KW), BF16), pltpu.VMEM((2,) + blk, BF16), pltpu.VMEM((2,) + blk, BF16)]
        + _ChipExchange.SCRATCH,
        compiler_params=_cp(("arbitrary", "arbitrary")),
    )(q, kv, kv, qg, kg, xsrc)


def _attn_bwd(q, kv, tot, do, qg, kg, *, n_ex, name):
    T, D = q.shape
    S = T // n_ex
    nhp = D // (2 * HEAD_DIM)
    nq = S // TQ
    scale = 1.0 / math.sqrt(HEAD_DIM)

    def body(q_ref, k_ref, v_ref, tot_ref, do_ref, qg_ref, kg_ref, dq_ref, dk_ref, dv_ref, dqg_ref, dkg_ref,
             kT_s, vT_s, km_s, qm_s, dom_s, dqn_s, dkT_s, dvT_s):
        m0, m1 = _head_masks()
        qn, qr = _head_norm(q_ref[...], None, m0, m1)
        kn, kr = _head_norm(k_ref[...], None, m0, m1)
        qs = qn * (qg_ref[...] * scale)
        kk = kn * kg_ref[...]
        _transposed_windows(kk, kT_s)
        _transposed_windows(v_ref[...], vT_s)
        do = do_ref[...]
        for h, m in enumerate((m0, m1)):
            qm_s[h] = (qs * m).astype(BF16)
            km_s[h] = (kk * m).astype(BF16)
            dom_s[h] = (do * m).astype(BF16)
        dkT_s[...] = jnp.zeros_like(dkT_s)
        dvT_s[...] = jnp.zeros_like(dvT_s)
        u_upto, u_before = _pair_matrix("upto"), _pair_matrix("before")

        def both(inv, win, st, mask, kw):
            keys = pl.ds(pl.multiple_of(win * KW, KW), kw)
            lg = [_sb_logits(jnp.dot(inv[h][0], kT_s[win, :, :kw], preferred_element_type=F32), mask) for h in range(2)]
            s_lf = [_block_sums(lg[h][1], u_upto, st[3 * h], False) for h in range(2)]
            ws, ews = [], []
            for h in range(2):
                w = jnp.exp(lg[h][0] - s_lf[h][0])
                if mask is not None:
                    w = jnp.where(mask, w, 0.0)
                ws.append(w)
                ews.append(jnp.dot(inv[h][2], vT_s[win, :, :kw], preferred_element_type=F32) * w)
            s_e = [_block_sums(ews[h], u_before, st[3 * h + 1], False, terms=1) for h in range(2)]
            out, dk, dv = (), None, None
            for h in range(2):
                sig = jnp.exp(lg[h][0])
                dz = ews[h] - sig * (ews[h] + s_e[h][0])
                if mask is not None:
                    dz = jnp.where(mask, dz, 0.0)
                dzb = dz.astype(BF16)
                out += (s_lf[h][1], s_e[h][1], st[3 * h + 2] + jnp.dot(dzb, km_s[h, keys, :], preferred_element_type=F32))
                dkh = jnp.dot(inv[h][1], dzb, preferred_element_type=F32)
                dvh = jnp.dot(inv[h][3], ws[h].astype(BF16), preferred_element_type=F32)
                dk, dv = (dkh, dvh) if h == 0 else (dk + dkh, dv + dvh)
            dkT_s[win, :, :kw] += dk
            dvT_s[win, :, :kw] += dv
            return out

        def qtile(iq, last, kw):
            rows = pl.ds(pl.multiple_of(iq * TQ, TQ), TQ)
            mask = _causal_mask(iq * TQ, last * KW, kw)
            tt = tot_ref[rows, :]
            inv, neg_total = [], []
            for h, m in enumerate((m0, m1)):
                qh, doh = qm_s[h, rows, :], dom_s[h, rows, :]
                neg_total.append(jnp.sum(tt * m, axis=-1, keepdims=True) * (-1.0 / HEAD_DIM))
                inv.append((qh, qh.astype(F32).T.astype(BF16), doh, doh.astype(F32).T.astype(BF16)))

            z1, zq = jnp.zeros((TQ, 1), F32), jnp.zeros((TQ, 2 * HEAD_DIM), F32)
            st = lax.fori_loop(0, last, lambda win, st: both(inv, win, st, None, KW), (neg_total[0], z1, zq, neg_total[1], z1, zq))
            st = both(inv, last, st, mask, kw)
            dqn_s[rows, :] = st[2] + st[5]

        def qtiles_of_window(a, _):
            for sub in range(KW // TQ):
                qtile(a * (KW // TQ) + sub, a, (sub + 1) * TQ)
            return 0

        lax.fori_loop(0, S // KW, qtiles_of_window, 0)
        dkn = jnp.concatenate([dkT_s[w].T for w in range(nwin)], axis=0)
        dq, dqg = _head_norm_bwd(dqn_s[...] * scale, qn, qr, qg_ref[...], m0, m1)
        dk, dkg = _head_norm_bwd(dkn, kn, kr, kg_ref[...], m0, m1)
        dq_ref[...] = dq
        dk_ref[...] = dk
        dv_ref[...] = jnp.concatenate([dvT_s[w].T for w in range(nwin)], axis=0)
        dqg_ref[...] = dqg
        dkg_ref[...] = dkg

    assert S % KW == 0 and KW % TQ == 0
    nwin = S // KW
    blk = (S, 2 * HEAD_DIM)
    tblk = (nwin, 2 * HEAD_DIM, KW)
    gblk = (None, None, 1, 2 * HEAD_DIM)
    dq, dk, dv, dqg, dkg = pl.pallas_call(
        body, name=name, grid=(n_ex, nhp),
        in_specs=[pl.BlockSpec(blk, lambda e, h: (e, h)), pl.BlockSpec(blk, lambda e, h: (e, h)),
                  pl.BlockSpec(blk, lambda e, h: (e, h + nhp)),
                  pl.BlockSpec(blk, lambda e, h: (e, h)), pl.BlockSpec(blk, lambda e, h: (e, h)),
                  pl.BlockSpec((1, 2 * HEAD_DIM), lambda e, h: (0, 0)), pl.BlockSpec((1, 2 * HEAD_DIM), lambda e, h: (0, 0))],
        out_specs=[pl.BlockSpec(blk, lambda e, h: (e, h))] * 3 + [pl.BlockSpec(gblk, lambda e, h: (e, h, 0, 0))] * 2,
        out_shape=[jax.ShapeDtypeStruct((T, D), F32)] * 3 + [jax.ShapeDtypeStruct((n_ex, nhp, 1, 2 * HEAD_DIM), F32)] * 2,
        scratch_shapes=[pltpu.VMEM(tblk, BF16), pltpu.VMEM(tblk, BF16),
                        pltpu.VMEM((2,) + blk, BF16), pltpu.VMEM((2,) + blk, BF16), pltpu.VMEM((2,) + blk, BF16),
                        pltpu.VMEM(blk, F32), pltpu.VMEM(tblk, F32), pltpu.VMEM(tblk, F32)],
        compiler_params=_cp(("parallel", "parallel")),
    )(q, kv, kv, tot, do, qg, kg)
    return dq, dk, dv, dqg, dkg


def _place():
    return lax.axis_index("x"), lax.axis_index("y"), lax.axis_index("c")


def _all_gather8(x_shard, *, name):
    m_per, n = x_shard.shape

    def body(x_ref, out_ref, send_sems, recv_sems, local_sem):
        x, y, c = _place()
        me, sibling = (x, y, c), (x, y, 1 - c)
        chips = [(1 - x, y), (x, 1 - y), (1 - x, 1 - y)]

        def rows(px, py, pc):
            return out_ref.at[pl.ds((4 * px + 2 * py + pc) * m_per, m_per), :]

        def copy(k, block, to, src=None):
            return pltpu.make_async_remote_copy(
                src_ref=rows(*block) if src is None else src, dst_ref=rows(*block),
                send_sem=send_sems.at[k], recv_sem=recv_sems.at[k], device_id=to, device_id_type=MESH)

        mine = pltpu.make_async_copy(x_ref, rows(*me), local_sem)
        mine.start()
        first = [copy(0, me, sibling, src=x_ref)]
        first += [copy(1 + j, me, (*chip, c), src=x_ref) for j, chip in enumerate(chips)]
        for cp in first:
            cp.start()
        passed = [copy(4 + j, (*chip, c), sibling) for j, chip in enumerate(chips)]
        for j, chip in enumerate(chips):
            copy(1 + j, (*chip, c), me).wait_recv()
            passed[j].start()
        copy(0, sibling, me).wait_recv()
        for j, chip in enumerate(chips):
            copy(4 + j, (*chip, 1 - c), me).wait_recv()
        for cp in first + passed:
            cp.wait_send()
        mine.wait()

    return pl.pallas_call(
        body, name=name, out_shape=jax.ShapeDtypeStruct((8 * m_per, n), x_shard.dtype),
        in_specs=[pl.BlockSpec(memory_space=pltpu.VMEM)], out_specs=pl.BlockSpec(memory_space=pltpu.VMEM),
        scratch_shapes=[pltpu.SemaphoreType.DMA((7,)), pltpu.SemaphoreType.DMA((7,)), pltpu.SemaphoreType.DMA],
        compiler_params=pltpu.CompilerParams(vmem_limit_bytes=VMEM_LIMIT),
    )(x_shard)


def _sibling_sum_half(x, *, name):
    R, C = x.shape
    half = R // 2
    assert half % 16 == 0

    def body(x_ref, o_ref, theirs, send_sem, recv_sem):
        px, py, pc = _place()
        cp = pltpu.make_async_remote_copy(src_ref=x_ref, dst_ref=theirs, send_sem=send_sem, recv_sem=recv_sem,
                                          device_id=(px, py, 1 - pc), device_id_type=MESH)
        cp.start()
        cp.wait()
        rows = pl.ds(pl.multiple_of(pc * half, 8), half)
        o_ref[...] = (x_ref[rows, :] + theirs[rows, :]).astype(BF16)

    return pl.pallas_call(
        body, name=name, out_shape=jax.ShapeDtypeStruct((half, C), BF16),
        in_specs=[pl.BlockSpec(memory_space=pltpu.VMEM)], out_specs=pl.BlockSpec(memory_space=pltpu.VMEM),
        scratch_shapes=[pltpu.VMEM((R, C), x.dtype), pltpu.SemaphoreType.DMA, pltpu.SemaphoreType.DMA],
        compiler_params=pltpu.CompilerParams(vmem_limit_bytes=VMEM_LIMIT),
    )(x)


def _sum_blocks(x, n, *, name):
    R = x.shape[0] // n

    def body(x_ref, o_ref):
        acc = x_ref[pl.ds(0, R), :].astype(F32)
        for k in range(1, n):
            acc = acc + x_ref[pl.ds(k * R, R), :].astype(F32)
        o_ref[...] = acc

    return pl.pallas_call(body, name=name, out_shape=jax.ShapeDtypeStruct((R, x.shape[1]), F32),
                          compiler_params=pltpu.CompilerParams(vmem_limit_bytes=VMEM_LIMIT))(x)


def _colsum(x, *, name):
    def body(x_ref, o_ref):
        o_ref[...] = jnp.sum(x_ref[...], axis=0, keepdims=True)
    return pl.pallas_call(body, name=name, out_shape=jax.ShapeDtypeStruct((1, x.shape[1]), x.dtype))(x)


ANY = pl.BlockSpec(memory_space=pl.ANY)


class _ChipExchange:
    SCRATCH = [pltpu.SemaphoreType.DMA((3,)), pltpu.SemaphoreType.DMA((3,)), pltpu.SemaphoreType.DMA]

    @staticmethod
    def out_shape(src, scatter):
        return jax.ShapeDtypeStruct(((4,) + tuple(src.shape[1:])) if scatter else ((4, 2) + tuple(src.shape[1:])), src.dtype)

    def __init__(self, src_ref, out_ref, send_sems, recv_sems, local_sem, scatter):
        x, y, c = _place()
        myj = 2 * x + y
        chips = [(1 - x, y), (x, 1 - y), (1 - x, 1 - y)]

        def slot(j):
            return out_ref.at[j] if scatter else out_ref.at[j, c]

        def piece(j):
            return src_ref.at[j] if scatter else src_ref.at[c]

        self.mine = pltpu.make_async_copy(piece(myj), slot(myj), local_sem)
        self.sends = [pltpu.make_async_remote_copy(
            src_ref=piece(2 * cx + cy), dst_ref=slot(myj), send_sem=send_sems.at[k], recv_sem=recv_sems.at[k],
            device_id=(cx, cy, c), device_id_type=MESH) for k, (cx, cy) in enumerate(chips)]
        self.recvs = [pltpu.make_async_remote_copy(
            src_ref=slot(2 * cx + cy), dst_ref=slot(2 * cx + cy), send_sem=send_sems.at[k], recv_sem=recv_sems.at[k],
            device_id=(cx, cy, c), device_id_type=MESH) for k, (cx, cy) in enumerate(chips)]

    def start(self):
        self.mine.start()
        for cp in self.sends:
            cp.start()

    def wait(self):
        for cp in self.recvs:
            cp.wait_recv()
        for cp in self.sends:
            cp.wait_send()
        self.mine.wait()


def _sibling_fill(buf, *, axis, name):
    def half(ref, h):
        return ref.at[h] if axis == 0 else ref.at[:, h]

    def body(in_ref, out_ref, send_sem, recv_sem):
        x, y, c = _place()
        cp = pltpu.make_async_remote_copy(src_ref=half(out_ref, c), dst_ref=half(out_ref, c), send_sem=send_sem, recv_sem=recv_sem,
                                          device_id=(x, y, 1 - c), device_id_type=MESH)
        cp.start()
        pltpu.make_async_remote_copy(src_ref=half(out_ref, 1 - c), dst_ref=half(out_ref, 1 - c), send_sem=send_sem, recv_sem=recv_sem,
                                     device_id=(x, y, 1 - c), device_id_type=MESH).wait_recv()
        cp.wait_send()

    return pl.pallas_call(
        body, name=name, out_shape=jax.ShapeDtypeStruct(buf.shape, buf.dtype), in_specs=[ANY], out_specs=ANY,
        input_output_aliases={0: 0}, scratch_shapes=[pltpu.SemaphoreType.DMA, pltpu.SemaphoreType.DMA],
    )(buf)


def _sibling_swap_half(g, *, name):
    def body(g_ref, out_ref, send_sem, recv_sem):
        x, y, c = _place()
        cp = pltpu.make_async_remote_copy(src_ref=g_ref.at[:, 1 - c], dst_ref=out_ref, send_sem=send_sem, recv_sem=recv_sem,
                                          device_id=(x, y, 1 - c), device_id_type=MESH)
        cp.start()
        cp.wait()

    return pl.pallas_call(
        body, name=name, out_shape=jax.ShapeDtypeStruct((g.shape[0],) + g.shape[2:], g.dtype), in_specs=[ANY], out_specs=ANY,
        scratch_shapes=[pltpu.SemaphoreType.DMA, pltpu.SemaphoreType.DMA],
    )(g)


def _add_my_half(g, b, cidx, *, name, tr=1024):
    n, _, R, C = g.shape
    tr = max(t for t in range(16, tr + 1, 16) if R % t == 0)

    def body(c_ref, g_ref, b_ref, o_ref):
        o_ref[...] = (g_ref[...] + b_ref[...]).astype(o_ref.dtype)

    return pl.pallas_call(
        body, name=name, out_shape=jax.ShapeDtypeStruct((n, R, C), BF16),
        grid_spec=pltpu.PrefetchScalarGridSpec(
            num_scalar_prefetch=1, grid=(n, R // tr),
            in_specs=[pl.BlockSpec((None, None, tr, C), lambda j, i, c: (j, c[0], i, 0)),
                      pl.BlockSpec((None, tr, C), lambda j, i, c: (j, i, 0))],
            out_specs=pl.BlockSpec((None, tr, C), lambda j, i, c: (j, i, 0))),
        compiler_params=_cp(("parallel", "parallel")),
    )(cidx, g, b)


def _sum4_into_half(q, cidx, *, name, tr=1024):
    _, R, C = q.shape
    tr = max(t for t in range(16, tr + 1, 16) if R % t == 0)

    def body(c_ref, q_ref, o_ref):
        o_ref[...] = ((q_ref[0].astype(F32) + q_ref[1].astype(F32)) + q_ref[2].astype(F32)) + q_ref[3].astype(F32)

    return pl.pallas_call(
        body, name=name, out_shape=jax.ShapeDtypeStruct((2, R, C), F32),
        grid_spec=pltpu.PrefetchScalarGridSpec(
            num_scalar_prefetch=1, grid=(R // tr,),
            in_specs=[pl.BlockSpec((4, tr, C), lambda i, c: (0, i, 0))],
            out_specs=pl.BlockSpec((None, tr, C), lambda i, c: (c[0], i, 0))),
        compiler_params=_cp(("parallel",)),
    )(cidx, q)


def _pack_rows(parts, width=1024, row_multiple=8):
    rows, spans, r0 = [], [], 0
    for p in parts:
        n = p.size
        nr = 8 * (-(-n // (8 * width)))
        flat = p.reshape(-1)
        if nr * width != n:
            flat = jnp.pad(flat, (0, nr * width - n))
        rows.append(flat.reshape(nr, width))
        spans.append((r0, nr, n, p.shape))
        r0 += nr
    if r0 % row_multiple:
        rows.append(jnp.zeros((row_multiple - r0 % row_multiple, width), parts[0].dtype))
    return jnp.concatenate(rows, axis=0), spans


def _unpack_rows(buf, spans):
    return [buf[r0:r0 + nr].reshape(-1)[:n].reshape(shape) for (r0, nr, n, shape) in spans]


def kernel(x, c, ada_w, ada_b, mix_norm_g, mlp_norm_g, mlp_w1, mlp_w2, s5_a_re, s5_a_im, s5_log_dt, s5_b_re, s5_b_im, s5_c_re, s5_c_im, s5_d, s5_w_glu, kv_ada_w, kv_ada_b, kv_norm_g, w_kv, k_norm_g, sb_w_q, q_norm_g, sb_w_o, loss_target, m_ada_w, m_ada_b, m_mix_norm_g, m_mlp_norm_g, m_mlp_w1, m_mlp_w2, m_s5_a_re, m_s5_a_im, m_s5_log_dt, m_s5_b_re, m_s5_b_im, m_s5_c_re, m_s5_c_im, m_s5_d, m_s5_w_glu, m_kv_ada_w, m_kv_ada_b, m_kv_norm_g, m_w_kv, m_k_norm_g, m_sb_w_q, m_q_norm_g, m_sb_w_o, v_ada_w, v_ada_b, v_mix_norm_g, v_mlp_norm_g, v_mlp_w1, v_mlp_w2, v_s5_a_re, v_s5_a_im, v_s5_log_dt, v_s5_b_re, v_s5_b_im, v_s5_c_re, v_s5_c_im, v_s5_d, v_s5_w_glu, v_kv_ada_w, v_kv_ada_b, v_kv_norm_g, v_w_kv, v_k_norm_g, v_sb_w_q, v_q_norm_g, v_sb_w_o):
    E, S, D = x.shape
    T = E * S
    FF = 4 * D
    NB = 8 * E
    px, py, pc = _place()
    chip = 2 * px + py
    dev = 4 * px + 2 * py + pc
    cidx = jnp.reshape(pc, (1,)).astype(jnp.int32)
    x0 = x.reshape(T, D)
    tgt = loss_target.reshape(T, D)

    nc_rows, nd = c.size // 128, s5_d.size // 128
    cd = jnp.concatenate([c.reshape(nc_rows, 128), jnp.pad(s5_d.reshape(nd, 128), ((0, 8 - nd), (0, 0)))], axis=0)
    cd_all = _all_gather8(cd, name="ag_c_d").reshape(8, nc_rows + 8, 128)
    c_all = cd_all[:, :nc_rows].reshape(NB, D)
    d_full = cd_all.reshape(4, 2, nc_rows + 8, 128)[:, 0, nc_rows:nc_rows + nd].reshape(1, D)
    sc_all = (c_all * _sigmoid(c_all)).astype(BF16)
    wa = ada_w.shape[2]
    wk = kv_ada_w.shape[1]
    m_sh = jnp.concatenate([_mm(sc_all, _Layer(ada_w, 0), "nn", name="ada0", tn=256),
                            _mm(sc_all, _Layer(ada_w, 1), "nn", name="ada1", tn=256),
                            _mm(sc_all, kv_ada_w, "nn", name="ada_kv", tn=256)], axis=1)
    m_all = _all_gather8(m_sh, name="ag_m").reshape(4, 2, NB, 2 * wa + wk)[:, 0]
    mods = []
    for l in range(2):
        full = jnp.transpose(m_all[:, :, l * wa:(l + 1) * wa], (1, 0, 2)).reshape(NB, 6 * D) + ada_b[l]
        mine = lax.dynamic_slice_in_dim(full, E * dev, E, axis=0)
        mods.append([mine[:, i * D:(i + 1) * D].reshape(E, 1, D) for i in range(6)])
    full = jnp.transpose(m_all[:, :, 2 * wa:], (1, 0, 2)).reshape(NB, 2 * D) + kv_ada_b
    mine = lax.dynamic_slice_in_dim(full, E * dev, E, axis=0)
    kv_sh, kv_sc = [mine[:, i * D:(i + 1) * D].reshape(E, 1, D) for i in range(2)]

    wpack_a = jnp.concatenate([mlp_w1[0], mlp_w2[0], jnp.concatenate([s5_w_glu[0], w_kv], axis=1), sb_w_q[0]], axis=0).astype(BF16)
    wpack_b = jnp.concatenate([mlp_w1[1], mlp_w2[1], sb_w_o[0]], axis=0).astype(BF16)
    RA, RB = wpack_a.shape[0], wpack_b.shape[0]
    RW = RA + RB

    tm = min(2048, S)
    tm_res = min(1024, S)
    gbuf = [jax.ShapeDtypeStruct((4, RW, D), F32)]

    def grad_mm(act, dout, kind, roff, nr, c0, nc, name, transposed=False):
        gbuf[0] = _mm(act, dout, "nn" if transposed else "tn", name=name, tm=1024, tk=2048,
                      into=_Sharded(gbuf[0], kind, roff, nr, c0, nc))

    def mlp_fwd(xa, l, mod):
        sh_m, sc_m, g_m = mod[3], mod[4], mod[5]
        h, h_t = _norm_mod_fwd(xa, mlp_norm_g[l:l + 1], sh_m, sc_m, n_ex=E, out_dtype=BF16, name=f"mlp_norm{l}", with_transpose=True)

        def relu_sq(acc):
            ra = jnp.maximum(acc, 0.0)
            return ra * ra, ra
        r, ra = _mm(h, W1[l], "nn", name=f"mlp_up{l}", out_dtypes=(BF16, BF16), tm=tm, epilogue=relu_sq)
        xb, ff = _mm(r, W2[l], "nn", name=f"mlp_down{l}", out_dtypes=(F32, F32), tm=tm_res,
                     extras=[_mn_extra(xa), _vec_extra(g_m, S)],
                     epilogue=lambda acc, xat, gt: (xat + gt * acc, acc))
        return xb, (h_t, r, ra, ff)

    def mlp_bwd(dxb, dff, xa, l, mod, saved, gated=None):
        sc_m = mod[4]
        h_t, r, ra, _ = saved
        da = _mm(dff, W2[l], "nt", name=f"mlp_down_dx{l}", out_dtypes=(BF16,), tm=tm, extras=[_mn_extra(ra)],
                 epilogue=lambda acc, rat: (acc * (2.0 * rat.astype(F32)),))
        grad_mm(r, dff, "rows", (2 + l) * D, D, 0, D, f"mlp_down_dw{l}")
        dh = _mm(da, W1[l], "nt", name=f"mlp_up_dx{l}", tm=tm)
        grad_mm(h_t, da, "cols", l * D, D, 0, D, f"mlp_up_dw{l}", transposed=True)
        return _norm_mod_bwd(xa, dh, dxb, mlp_norm_g[l:l + 1], sc_m, n_ex=E, name=f"mlp_norm_bwd{l}", gated=gated)

    ab_re, ab_im, bb_re, bb_im = _s5_disc(s5_a_re[0], s5_a_im[0], s5_log_dt[0], s5_b_re[0], s5_b_im[0])
    cf, cr = _s5_consts(ab_re, ab_im)
    Wb, Wc = _s5_blockdiag(bb_re, bb_im, s5_c_re[0], s5_c_im[0])
    ng = D // U_LANES

    mod0, mod1 = mods
    h0 = _norm_mod_fwd(x0, mix_norm_g[0:1], mod0[0], mod0[1], n_ex=E, out_dtype=F32, name="mix_norm0")
    y, gy, gy_t, s5_states, wfull_a = _s5_fwd(h0, Wb, Wc, cf, d_full, wpack_a.reshape(2, RA // 2, D), n_ex=E, name="s5_fwd")
    wfull_a = _sibling_fill(wfull_a, axis=1, name="wgather_a_d2d").reshape(4, RA, D)

    W1 = [_Sharded(wfull_a, "cols", 0, D, 0, D), None]
    W2 = [_Sharded(wfull_a, "rows", D, D, 0, D), None]
    Wglu = _Sharded(wfull_a, "cols", 2 * D, D, 0, D // 2)
    Wkv = _Sharded(wfull_a, "cols", 2 * D, D, D // 2, D // 2)
    Wq = _Sharded(wfull_a, "rows", 3 * D, D // 4, 0, D)
    vg = _mm(gy, Wglu, "nn", name="glu_up", tm=tm)
    (x1,), _ = _rowwise(lambda v, g, xt, ga: ([xt + ga * (v * _sigmoid(g))], []),
                        [(vg, D, 0), (vg, D, 1), (x0, D, 0)], [mod0[2]], [], [(D, F32)], [], n_ex=E, name="glu_gate")
    x2, saved_mlp0 = mlp_fwd(x1, 0, mod0)

    hkv, hkv_t = _norm_mod_fwd(x2, kv_norm_g.reshape(1, D), kv_sh, kv_sc, n_ex=E, out_dtype=BF16, name="kv_norm", with_transpose=True)
    kvf = _mm(hkv, Wkv, "nn", name="kv_proj", tm=tm)
    h1, h1_t = _norm_mod_fwd(x2, mix_norm_g[1:2], mod1[0], mod1[1], n_ex=E, out_dtype=BF16, name="mix_norm1", with_transpose=True)
    qf = _mm(h1, Wq, "nn", name="q_proj", tm=tm)
    qg2 = jnp.tile(q_norm_g.reshape(1, HEAD_DIM), (1, 2))
    kg2 = jnp.tile(k_norm_g.reshape(1, HEAD_DIM), (1, 2))
    o, lf_tot, o_t, wfull_b = _attn_fwd(qf, kvf, qg2, kg2, wpack_b.reshape(2, RB // 2, D), n_ex=E, name="attn_fwd")
    wfull_b = _sibling_fill(wfull_b, axis=1, name="wgather_b_d2d").reshape(4, RB, D)
    W1[1] = _Sharded(wfull_b, "cols", 0, D, 0, D)
    W2[1] = _Sharded(wfull_b, "rows", D, D, 0, D)
    Wo = _Sharded(wfull_b, "rows", 2 * D, D // 4, 0, D)
    x3, mix1 = _mm(o, Wo, "nn", name="o_proj", out_dtypes=(F32, F32), tm=tm_res,
                   extras=[_mn_extra(x2), _vec_extra(mod1[2], S)],
                   epilogue=lambda acc, xat, gt: (xat + gt * acc, acc))
    x4, saved_mlp1 = mlp_fwd(x3, 1, mod1)

    def loss_fn(xt, tt, fft, gmt):
        dx = (xt - tt) * (1.0 / D)
        dff, dgm = _gated(dx, fft, gmt)
        return [dx, dff], [_csum(jnp.square(xt - tt)) * (0.5 / D), dgm]
    (dx4, dff1), (lsum, dgm1) = _rowwise(loss_fn, [(x4, D, 0), (tgt, D, 0), (saved_mlp1[3], D, 0)], [mod1[5]], [],
                                         [(D, F32), (D, BF16)], [D, D], n_ex=E, name="loss")
    loss = lax.psum(jnp.sum(lsum), ("x", "y", "c"))

    (dx3, dmix1), (dsh_m1, dsc_m1, dg_mlp1, dga1) = mlp_bwd(dx4, dff1, x3, 1, mod1, saved_mlp1, gated=(mix1, mod1[2]))
    do = _mm(dmix1, Wo, "nt", name="o_proj_dx", tm=tm)
    grad_mm(o_t, dmix1, "rows", 5 * D + D // 4, D // 4, 0, D, "o_proj_dw", transposed=True)
    dq, dk, dv, dqg, dkg = _attn_bwd(qf, kvf, lf_tot, do, qg2, kg2, n_ex=E, name="attn_bwd")
    dh1 = _mm(dq, Wq, "nt", name="q_proj_dx", tm=tm)
    grad_mm(h1_t, dq, "rows", 5 * D, D // 4, 0, D, "q_proj_dw", transposed=True)
    (dx2,), (dsh_a1, dsc_a1, dg_mix1) = _norm_mod_bwd(x2, dh1, dx3, mix_norm_g[1:2], mod1[1], n_ex=E, name="mix_norm_bwd1")
    dkv = jnp.concatenate([dk, dv], axis=1)
    dhkv = _mm(dkv, Wkv, "nt", name="kv_proj_dx", tm=tm)
    grad_mm(hkv_t, dkv, "cols", 4 * D, D, D // 2, D // 2, "kv_proj_dw", transposed=True)
    (dx2, dff0), (dkv_sh, dkv_sc, dg_kv, dgm0) = _norm_mod_bwd(x2, dhkv, dx2, kv_norm_g.reshape(1, D), kv_sc, n_ex=E, name="kv_norm_bwd",
                                                               gated=(saved_mlp0[3], mod0[5]))

    (dx1,), (dsh_m0, dsc_m0, dg_mlp0) = mlp_bwd(dx2, dff0, x1, 0, mod0, saved_mlp0)

    def glu_bwd(v, g, d, ga):
        sg = _sigmoid(g)
        dm = ga * d
        return [jnp.concatenate([dm * sg, dm * v * sg * (1.0 - sg)], axis=1)], [_csum(d * (v * sg))]
    (dvg,), (dga0,) = _rowwise(glu_bwd, [(vg, D, 0), (vg, D, 1), (dx1, D, 0)], [mod0[2]], [], [(2 * D, BF16)], [D],
                               n_ex=E, name="glu_gate_bwd")
    dgy = _mm(dvg, Wglu, "nt", name="glu_up_dx", tm=tm)
    grad_mm(gy_t, dvg, "cols", 4 * D, D, 0, D // 2, "glu_up_dw", transposed=True)

    gpack = gbuf[0].reshape(4, 2, RW // 2, D)
    theirs = _sibling_swap_half(gpack, name="gscatter_d2d")
    chip_sum = _add_my_half(gpack, theirs, cidx, name="gscatter_add")
    dh0, dWb, dWc, dab, dd, from_chips = _s5_bwd(h0, y, dgy, s5_states, Wb, Wc, cr, d_full, chip_sum, n_ex=E, name="s5_bwd")
    ghalf = _sum4_into_half(from_chips, cidx, name="gscatter_sum")
    gsh = _sibling_fill(ghalf, axis=0, name="gscatter_fill").reshape(RW, D)
    (gx,), (dsh_a0, dsc_a0, dg_mix0) = _norm_mod_bwd(x0, dh0, dx1, mix_norm_g[0:1], mod0[1], n_ex=E, name="mix_norm_bwd0")
    grad_x = gx.reshape(E, S, D)

    dm_mine = jnp.concatenate([t.reshape(E, D) for t in
                               (dsh_a0, dsc_a0, dga0, dsh_m0, dsc_m0, dgm0, dsh_a1, dsc_a1, dga1, dsh_m1, dsc_m1, dgm1, dkv_sh, dkv_sc)], axis=1)
    dm_all = _all_gather8(dm_mine.reshape(8, -1), name="ag_dm").reshape(NB, 14 * D)
    sc_f32 = c_all * _sigmoid(c_all)
    g_ada_w = jax.ShapeDtypeStruct(ada_w.shape, F32)
    for l in range(2):
        g_ada_w = _mm(sc_f32, lax.dynamic_slice_in_dim(dm_all, l * 6 * D + chip * wa, wa, axis=1), "tn", name=f"ada_dw{l}", tn=256,
                      into=_Layer(g_ada_w, l))
    g_kv_ada_w = _mm(sc_f32, lax.dynamic_slice_in_dim(dm_all, 12 * D + chip * wk, wk, axis=1), "tn", name="ada_kv_dw", tn=256)
    db_all = _colsum(dm_all, name="ada_db")
    g_ada_b = db_all[0, :12 * D].reshape(2, 6 * D)
    g_kv_ada_b = db_all[0, 12 * D:]

    dWb_re, dWb_im, dC_re, dC_im = _s5_unblock(dWb, dWc)
    small_parts = [dg_mix0.sum(0), dg_mix1.sum(0), dg_mlp0.sum(0), dg_mlp1.sum(0), dg_kv.sum(0),
                   dqg.sum((0, 1, 2)).reshape(2, HEAD_DIM).sum(0), dkg.sum((0, 1, 2)).reshape(2, HEAD_DIM).sum(0),
                   dd[:, 0, :], dab[:, 0, :], dab[:, 1, :], dWb_re, dWb_im, dC_re, dC_im]
    spack, spans = _pack_rows(small_parts, row_multiple=32)
    chip_half = _sibling_sum_half(spack, name="small_d2d")
    ssum = _sum_blocks(_all_gather8(chip_half, name="ag_small"), 4, name="sum_small")
    (g_mix0, g_mix1, g_mlp0, g_mlp1, g_kvn, g_qn, g_kn, g_d, g_abr, g_abi, g_bbr, g_bbi, g_cre, g_cim) = _unpack_rows(ssum, spans)
    _, disc_vjp = jax.vjp(_s5_disc, s5_a_re[0], s5_a_im[0], s5_log_dt[0], s5_b_re[0], s5_b_im[0])
    g_are, g_aim, g_ldt, g_bre, g_bim = disc_vjp((g_abr.reshape(ab_re.shape), g_abi.reshape(ab_im.shape), g_bbr, g_bbi))
    g_s5d = lax.dynamic_slice_in_dim(g_d.reshape(1, D), chip * s5_d.shape[1], s5_d.shape[1], axis=1)

    def upd_big(w, m, v, roff, cb, name):
        shape = w.shape
        W = shape[-1]
        d_, m_, v_, g_ = _adamw2d(w.reshape(-1, W), gsh, m.reshape(-1, W), v.reshape(-1, W), name=name, g_roff=roff, g_cb=cb)
        return [t.reshape(shape) for t in (g_, d_, m_, v_)]

    def upd_own(w, g, m, v, name):
        shape = w.shape
        W = shape[-1]
        d_, m_, v_, g_ = _adamw2d(w.reshape(-1, W), g.reshape(-1, W), m.reshape(-1, W), v.reshape(-1, W), name=name)
        return [t.reshape(shape) for t in (g_, d_, m_, v_)]

    res = {}
    res["ada_w"] = upd_own(ada_w, g_ada_w, m_ada_w, v_ada_w, "adam_ada_w")
    res["kv_ada_w"] = upd_own(kv_ada_w, g_kv_ada_w, m_kv_ada_w, v_kv_ada_w, "adam_kv_ada_w")
    res["mlp_w1"] = upd_big(mlp_w1, m_mlp_w1, v_mlp_w1, 0, 0, "adam_w1")
    res["mlp_w2"] = upd_big(mlp_w2, m_mlp_w2, v_mlp_w2, 2 * D, 0, "adam_w2")
    res["s5_w_glu"] = upd_big(s5_w_glu, m_s5_w_glu, v_s5_w_glu, 4 * D, 0, "adam_glu")
    res["w_kv"] = upd_big(w_kv, m_w_kv, v_w_kv, 4 * D, 1, "adam_wkv")
    res["sb_w_q"] = upd_big(sb_w_q, m_sb_w_q, v_sb_w_q, 5 * D, 0, "adam_wq")
    res["sb_w_o"] = upd_big(sb_w_o, m_sb_w_o, v_sb_w_o, 5 * D + D // 4, 0, "adam_wo")

    small = {
        "ada_b": (ada_b, g_ada_b, m_ada_b, v_ada_b),
        "mix_norm_g": (mix_norm_g, jnp.stack([g_mix0, g_mix1]), m_mix_norm_g, v_mix_norm_g),
        "mlp_norm_g": (mlp_norm_g, jnp.stack([g_mlp0, g_mlp1]), m_mlp_norm_g, v_mlp_norm_g),
        "s5_a_re": (s5_a_re, g_are[None], m_s5_a_re, v_s5_a_re),
        "s5_a_im": (s5_a_im, g_aim[None], m_s5_a_im, v_s5_a_im),
        "s5_log_dt": (s5_log_dt, g_ldt[None], m_s5_log_dt, v_s5_log_dt),
        "s5_b_re": (s5_b_re, g_bre[None], m_s5_b_re, v_s5_b_re),
        "s5_b_im": (s5_b_im, g_bim[None], m_s5_b_im, v_s5_b_im),
        "s5_c_re": (s5_c_re, g_cre[None], m_s5_c_re, v_s5_c_re),
        "s5_c_im": (s5_c_im, g_cim[None], m_s5_c_im, v_s5_c_im),
        "s5_d": (s5_d, g_s5d, m_s5_d, v_s5_d),
        "kv_ada_b": (kv_ada_b, g_kv_ada_b, m_kv_ada_b, v_kv_ada_b),
        "kv_norm_g": (kv_norm_g, g_kvn, m_kv_norm_g, v_kv_norm_g),
        "k_norm_g": (k_norm_g, g_kn, m_k_norm_g, v_k_norm_g),
        "q_norm_g": (q_norm_g, g_qn.reshape(q_norm_g.shape), m_q_norm_g, v_q_norm_g),
    }
    names = list(small)
    packs = [_pack_rows([small[n][i].reshape(small[n][0].shape) for n in names]) for i in range(4)]
    sp = packs[0][1]
    d_, m_, v_, g_ = _adamw2d(packs[0][0], packs[1][0], packs[2][0], packs[3][0], name="adam_small")
    for n, gg, dd_, mm_, vv_ in zip(names, _unpack_rows(g_, sp), _unpack_rows(d_, sp), _unpack_rows(m_, sp), _unpack_rows(v_, sp)):
        res[n] = [gg, dd_, mm_, vv_]

    order = ["ada_w", "ada_b", "mix_norm_g", "mlp_norm_g", "mlp_w1", "mlp_w2", "s5_a_re", "s5_a_im", "s5_log_dt", "s5_b_re", "s5_b_im",
             "s5_c_re", "s5_c_im", "s5_d", "s5_w_glu", "kv_ada_w", "kv_ada_b", "kv_norm_g", "w_kv", "k_norm_g", "sb_w_q", "q_norm_g", "sb_w_o"]
    return (loss, grad_x, *[res[n][0] for n in order], *[res[n][1] for n in order], *[res[n][2] for n in order], *[res[n][3] for n in order])
```

```python
import functools
import math

import jax
import jax.numpy as jnp
from jax import lax
from jax.experimental import pallas as pl
from jax.experimental.pallas import tpu as pltpu

F32 = jnp.float32
BF16 = jnp.bfloat16
EPS = 1e-6
HEAD_DIM = 64
S5_GROUP = 16
S5_STATE = 64
GROUPS_PER_STEP = 8
U_LANES = GROUPS_PER_STEP * S5_GROUP
ST_LANES = GROUPS_PER_STEP * S5_STATE
SCAN_LANES = 256
SCAN_UNROLL = 4
VMEM_LIMIT = 56 * 1024 * 1024
ADAM_LR, ADAM_B1, ADAM_B2, ADAM_EPS, ADAM_WD, ADAM_STEP = 0.001, 0.9, 0.999, 1e-08, 0.01, 10
MESH = pl.DeviceIdType.MESH


def _cp(sem):
    return pltpu.CompilerParams(dimension_semantics=sem, vmem_limit_bytes=VMEM_LIMIT)


class _Sharded:
    def __init__(self, buf, kind, roff, nr, c0, nc):
        self.buf, self.kind, self.roff, self.nr, self.c0, self.nc = buf, kind, roff, nr, c0, nc
        self.shape = (nr, 4 * nc) if kind == "cols" else (4 * nr, nc)

    def operand(self, dims, tn, tk):
        roff, nr, c0, nc = self.roff, self.nr, self.c0, self.nc
        if self.kind == "cols" and dims == "nn":
            tk = min(tk, nr)
            assert roff % tk == 0
            return nc, tk, (None, tk, nc), lambda i, j, k: (j, roff // tk + k, c0 // nc)
        if self.kind == "cols":
            tn = min(tn, nr)
            assert roff % tn == 0
            return tn, nc, (None, tn, nc), lambda i, j, k: (k, roff // tn + j, c0 // nc)
        if dims == "nn":
            tn = min(tn, nc)
            assert roff % nr == 0 and c0 % tn == 0
            return tn, nr, (None, nr, tn), lambda i, j, k: (k, roff // nr, c0 // tn + j)
        tk = min(tk, nc)
        assert roff % nr == 0 and c0 % tk == 0
        return nr, tk, (None, nr, tk), lambda i, j, k: (j, roff // nr, c0 // tk + k)

    def result(self, tm, tn):
        roff, nr, c0, nc = self.roff, self.nr, self.c0, self.nc
        if self.kind == "cols":
            tm = min(tm, nr)
            assert roff % tm == 0
            return tm, nc, (None, tm, nc), lambda i, j, k: (j, roff // tm + i, c0 // nc)
        tm, tn = min(tm, nr), min(tn, nc)
        assert roff % tm == 0 and c0 % tn == 0
        per = nr // tm
        return tm, tn, (None, tm, tn), lambda i, j, k: (i // per, roff // tm + i % per, c0 // tn + j)


class _Layer:
    def __init__(self, buf, layer):
        self.buf, self.layer, self.shape = buf, layer, tuple(buf.shape[1:])

    def operand(self, dims, tn, tk):
        assert dims == "nn"
        layer = self.layer
        return tn, tk, (None, tk, tn), lambda i, j, k: (layer, k, j)

    def result(self, tm, tn):
        layer = self.layer
        return tm, tn, (None, tm, tn), lambda i, j, k: (layer, i, j)


def _mm(a, b, dims, *, name, out_dtypes=(F32,), epilogue=None, extras=(), tm=512, tn=1024, tk=1024, into=None):
    bshape = b.shape
    if dims == "nn":
        (M, K), (_, N) = a.shape, bshape
    elif dims == "nt":
        (M, K), (N, _) = a.shape, bshape
    else:
        (K, M), (_, N) = a.shape, bshape
    tm, tn, tk = min(tm, M), min(tn, N), min(tk, K)
    b_arr = b
    if into is not None:
        assert (M, N) == into.shape and len(out_dtypes) == 1 and not isinstance(b, _Sharded)
        tm, tn, o_blk, o_map = into.result(tm, tn)
        out_specs, out_shape = [pl.BlockSpec(o_blk, o_map)], [jax.ShapeDtypeStruct(into.buf.shape, into.buf.dtype)]
    if isinstance(b, (_Sharded, _Layer)):
        tn, tk, b_blk, b_map = b.operand(dims, tn, tk)
        b_spec, b_arr = pl.BlockSpec(b_blk, b_map), b.buf
    else:
        b_spec = pl.BlockSpec((tn, tk), lambda i, j, k: (j, k)) if dims == "nt" else pl.BlockSpec((tk, tn), lambda i, j, k: (k, j))
    if into is None:
        out_specs = [pl.BlockSpec((tm, tn), lambda i, j, k: (i, j)) for _ in out_dtypes]
        out_shape = [jax.ShapeDtypeStruct((M, N), d) for d in out_dtypes]
    assert M % tm == 0 and N % tn == 0 and K % tk == 0, (M, N, K, tm, tn, tk)
    nk = K // tk
    extras = [e(tm, tn) for e in extras]
    a_spec = pl.BlockSpec((tk, tm), lambda i, j, k: (k, i)) if dims == "tn" else pl.BlockSpec((tm, tk), lambda i, j, k: (i, k))
    contract = {"nn": ((1,), (0,)), "nt": ((1,), (1,)), "tn": ((0,), (0,))}[dims]
    n_ex, n_out = len(extras), len(out_dtypes)
    chain = [into.buf] if into is not None and not isinstance(into.buf, jax.ShapeDtypeStruct) else []
    n_in = n_ex + len(chain)

    def finish(r, ex, outs):
        res = epilogue(r, *[e[...] for e in ex]) if epilogue is not None else (r,)
        for o, v in zip(outs, res):
            o[...] = v.astype(o.dtype)

    def product(a_ref, b_ref):
        return lax.dot_general(a_ref[...].astype(BF16), b_ref[...].astype(BF16), (contract, ((), ())), preferred_element_type=F32)

    def body_one(a_ref, b_ref, *rest):
        finish(product(a_ref, b_ref), rest[:n_ex], rest[n_in:])

    def body_acc(a_ref, b_ref, *rest):
        ex, outs, acc = rest[:n_ex], rest[n_in:n_in + n_out], rest[-1]
        k = pl.program_id(2)

        @pl.when(k == 0)
        def _():
            acc[...] = product(a_ref, b_ref)

        @pl.when(jnp.logical_and(k > 0, k < nk - 1))
        def _():
            acc[...] += product(a_ref, b_ref)

        @pl.when(k == nk - 1)
        def _():
            finish(acc[...] + product(a_ref, b_ref), ex, outs)

    out = pl.pallas_call(
        body_one if nk == 1 else body_acc, name=name, grid=(M // tm, N // tn, nk),
        in_specs=[a_spec, b_spec] + [pl.BlockSpec(blk, im) for (_, blk, im) in extras] + [ANY for _ in chain],
        out_specs=out_specs, out_shape=out_shape,
        input_output_aliases={2 + n_ex: 0} if chain else {},
        scratch_shapes=[] if nk == 1 else [pltpu.VMEM((tm, tn), F32)],
        compiler_params=_cp(("parallel", "parallel", "arbitrary")),
    )(a, b_arr, *[e[0] for e in extras], *chain)
    return out if n_out > 1 else out[0]


def _mn_extra(arr):
    return lambda tm, tn: (arr, (tm, tn), lambda i, j, k: (i, j))


def _vec_extra(vec, S):
    return lambda tm, tn: (vec, (None, 1, tn), lambda i, j, k: ((i * tm) // S, 0, j))


def _rowwise(fn, rows, vecs=(), consts=(), out_rows=(), out_sums=(), *, n_ex, name, tr=512):
    rows = [r if len(r) == 4 else (*r, 0) for r in rows]
    S = min(r[0].shape[0] for r in rows if r[3] == 0) // n_ex
    tr = math.gcd(tr, S)
    assert S % tr == 0
    nb = S // tr
    in_specs = []
    for (arr, w, cb, roff) in rows:
        assert roff % tr == 0
        in_specs.append(pl.BlockSpec((tr, w), functools.partial(lambda e, i, cb, ro: (e * nb + i + ro, cb), cb=cb, ro=roff // tr)))
    for v in vecs:
        in_specs.append(pl.BlockSpec((None, 1, v.shape[-1]), lambda e, i: (e, 0, 0)))
    for c in consts:
        in_specs.append(pl.BlockSpec((1, c.shape[-1]), lambda e, i: (0, 0)))
    n_in, n_or, n_os = len(in_specs), len(out_rows), len(out_sums)
    flipped = [len(o) == 3 and o[2] for o in out_rows]
    out_specs = [pl.BlockSpec((o[0], tr), lambda e, i: (0, e * nb + i)) if f else pl.BlockSpec((tr, o[0]), lambda e, i: (e * nb + i, 0))
                 for o, f in zip(out_rows, flipped)]
    out_specs += [pl.BlockSpec((None, 1, w), lambda e, i: (e, 0, 0)) for w in out_sums]
    out_shape = [jax.ShapeDtypeStruct((o[0], n_ex * S) if f else (n_ex * S, o[0]), o[1]) for o, f in zip(out_rows, flipped)]
    out_shape += [jax.ShapeDtypeStruct((n_ex, 1, w), F32) for w in out_sums]

    def body(*refs):
        ins, o_r, o_s = refs[:n_in], refs[n_in:n_in + n_or], refs[n_in + n_or:]
        ro, so = fn(*[r[...] for r in ins])
        for o, v, f in zip(o_r, ro, flipped):
            o[...] = (v.T if f else v).astype(o.dtype)
        i = pl.program_id(1)
        for o, v in zip(o_s, so):
            @pl.when(i == 0)
            def _(o=o, v=v):
                o[...] = v

            @pl.when(i > 0)
            def _(o=o, v=v):
                o[...] += v

    outs = pl.pallas_call(
        body, name=name, grid=(n_ex, nb), in_specs=in_specs, out_specs=out_specs, out_shape=out_shape,
        compiler_params=_cp(("parallel", "arbitrary")),
    )(*[r[0] for r in rows], *vecs, *consts)
    return outs[:n_or], outs[n_or:]


def _csum(x):
    return jnp.sum(x, axis=0, keepdims=True)


def _norm_mod_fwd(x, g, sh, sc, *, n_ex, out_dtype, name, with_transpose=False):
    def fn(xt, sht, sct, gt):
        r = lax.rsqrt(jnp.mean(xt * xt, axis=-1, keepdims=True) + EPS)
        h = (xt * r * gt) * (1.0 + sct) + sht
        return [h, h] if with_transpose else [h], []
    D = x.shape[1]
    outs = [(D, out_dtype), (D, out_dtype, True)] if with_transpose else [(D, out_dtype)]
    res = _rowwise(fn, [(x, D, 0)], [sh, sc], [g], outs, [], n_ex=n_ex, name=name)[0]
    return res if with_transpose else res[0]


def _gated(dx, branch, gate):
    return (gate * dx).astype(BF16), _csum(dx * branch)


def _norm_mod_bwd(x, dh, dres, g, sc, *, n_ex, name, gated=None):
    n_rows = len(gated[0]) if gated is not None else 0

    def fn(xt, dht, drt, *rest):
        sct, gt = rest[-2], rest[-1]
        dht = dht.astype(F32)
        r = lax.rsqrt(jnp.mean(xt * xt, axis=-1, keepdims=True) + EPS)
        n = xt * r
        y = n * gt
        dy = dht * (1.0 + sct)
        dn = dy * gt
        dx = drt + r * (dn - n * jnp.mean(dn * n, axis=-1, keepdims=True))
        rows, sums = [dx], [_csum(dht), _csum(dht * y), _csum(dy * n)]
        if gated is not None:
            dbranch, dgate = gated[2](dx, *rest[:n_rows + 1])
            rows, sums = rows + [dbranch], sums + [dgate]
        return rows, sums
    D = x.shape[1]
    extra_rows, extra_vecs = (list(gated[0]), [gated[1]]) if gated is not None else ([], [])
    return _rowwise(fn, [(x, D, 0), (dh, D, 0), (dres, D, 0)] + extra_rows, extra_vecs + [sc], [g],
                    [(D, F32)] + ([(gated[3], BF16)] if gated is not None else []), [D, D, D] + ([D] if gated is not None else []),
                    n_ex=n_ex, name=name)


def _sigmoid(x):
    return 1.0 / (1.0 + jnp.exp(-x))


def _gelu(y):
    return 0.5 * y * (1.0 + jnp.tanh(0.7978845608028654 * (y + 0.044715 * y * y * y)))


def _gelu_grad(y):
    t = jnp.tanh(0.7978845608028654 * (y + 0.044715 * y * y * y))
    return 0.5 * (1.0 + t) + 0.5 * y * (1.0 - t * t) * 0.7978845608028654 * (1.0 + 3 * 0.044715 * y * y)


def _adamw_fn(w, g, m, v):
    m2 = ADAM_B1 * m + (1.0 - ADAM_B1) * g
    v2 = ADAM_B2 * v + (1.0 - ADAM_B2) * (g * g)
    m_hat = m2 / (1.0 - ADAM_B1 ** ADAM_STEP)
    v_hat = v2 / (1.0 - ADAM_B2 ** ADAM_STEP)
    delta = -ADAM_LR * (m_hat / (jnp.sqrt(v_hat) + ADAM_EPS) + ADAM_WD * w)
    return delta, m2, v2


def _adamw2d(w, g, m, v, *, name, g_roff=0, g_cb=0):
    R, W = w.shape

    def fn(wt, gt, mt, vt):
        d, m2, v2 = _adamw_fn(wt, gt, mt, vt)
        return [d, m2, v2, gt], []
    return _rowwise(fn, [(w, W, 0), (g, W, g_cb, g_roff), (m, W, 0), (v, W, 0)], [], [],
                    [(W, F32)] * 4, [], n_ex=1, name=name, tr=256)[0]


def _scan_tiles(re_ref, im_ref, cf, lane0, n_chunks, reverse, extra=None):
    L = SCAN_LANES
    lanes = pl.ds(lane0, L)
    A = [cf[i, :, lanes] for i in range(8)]
    shifts = (7, 6, 4) if reverse else (1, 2, 4)
    edge = 0 if reverse else 7

    U = SCAN_UNROLL
    n_groups = n_chunks // U

    def body(c, carry):
        first = ((n_groups - 1 - c) if reverse else c) * U
        rows = pl.ds(pl.multiple_of(first * 8, 8 * U), 8 * U)
        big_r, big_i = re_ref[rows, lanes], im_ref[rows, lanes]
        tiles = []
        for u in range(U):
            xr, xi = big_r[8 * u:8 * u + 8, :], big_i[8 * u:8 * u + 8, :]
            for idx, sft in enumerate(shifts):
                ar, ai = A[2 * idx], A[2 * idx + 1]
                rr, ri = pltpu.roll(xr, sft, 0), pltpu.roll(xi, sft, 0)
                xr, xi = xr + ar * rr - ai * ri, xi + ar * ri + ai * rr
            tiles.append((xr, xi))
        pr, pi = A[6], A[7]
        cr, ci = carry[0], carry[1]
        for u in (range(U - 1, -1, -1) if reverse else range(U)):
            xr, xi = tiles[u]
            xr, xi = xr + pr * cr - pi * ci, xi + pr * ci + pi * cr
            tiles[u] = (xr, xi)
            cr, ci = jnp.broadcast_to(xr[edge:edge + 1, :], (8, L)), jnp.broadcast_to(xi[edge:edge + 1, :], (8, L))
        re_ref[rows, lanes] = jnp.concatenate([t[0] for t in tiles], axis=0)
        im_ref[rows, lanes] = jnp.concatenate([t[1] for t in tiles], axis=0)
        return (cr, ci) if extra is None else (cr, ci) + extra(first, tiles, carry[2:])

    assert n_chunks % U == 0
    z = jnp.zeros((8, L), F32)
    init = (z, z) if extra is None else (z, z, z, z)
    return lax.fori_loop(0, n_groups, body, init)


def _s5_consts(ab_re, ab_im):
    ng = ab_re.shape[0] // GROUPS_PER_STEP
    ar, ai = ab_re.reshape(ng, 1, ST_LANES), ab_im.reshape(ng, 1, ST_LANES)

    def cmul(xr, xi, yr, yi):
        return xr * yr - xi * yi, xr * yi + xi * yr

    def build(ar, ai, reverse):
        pw = [(ar, ai)]
        for _ in range(7):
            pw.append(cmul(*pw[-1], ar, ai))
        row = jnp.arange(8).reshape(1, 8, 1)
        tiles = []
        for k in (1, 2, 4):
            keep = (row <= 7 - k) if reverse else (row >= k)
            tiles += [jnp.where(keep, pw[k - 1][0], 0.0), jnp.where(keep, pw[k - 1][1], 0.0)]
        order = [7 - r for r in range(8)] if reverse else list(range(8))
        tiles += [jnp.concatenate([pw[o][0] for o in order], axis=1), jnp.concatenate([pw[o][1] for o in order], axis=1)]
        return jnp.stack([jnp.broadcast_to(t, (ng, 8, ST_LANES)) for t in tiles], axis=1)

    return build(ar, ai, False), build(ar, -ai, True)


def _s5_blockdiag(bb_re, bb_im, c_re, c_im):
    G = bb_re.shape[0]
    ng = G // GROUPS_PER_STEP
    eye = jnp.eye(GROUPS_PER_STEP, dtype=F32)

    def wb(bb):
        return jnp.einsum("bgph,gk->bghkp", bb.reshape(ng, GROUPS_PER_STEP, S5_STATE, S5_GROUP), eye).reshape(ng, U_LANES, ST_LANES)

    def wc(cc):
        return jnp.einsum("bghp,gk->bkpgh", cc.reshape(ng, GROUPS_PER_STEP, S5_GROUP, S5_STATE), eye).reshape(ng, ST_LANES, U_LANES)

    Wb = jnp.concatenate([wb(bb_re), wb(bb_im)], axis=2).astype(BF16)
    Wc = jnp.concatenate([wc(c_re), -wc(c_im)], axis=1).astype(BF16)
    return Wb, Wc


def _s5_unblock(dWb, dWc):
    ng = dWb.shape[0]
    eye = jnp.eye(GROUPS_PER_STEP, dtype=F32)

    def ub(w):
        return jnp.einsum("bghkp,gk->bgph", w.reshape(ng, GROUPS_PER_STEP, S5_GROUP, GROUPS_PER_STEP, S5_STATE), eye).reshape(-1, S5_STATE, S5_GROUP)

    def uc(w):
        return jnp.einsum("bkpgh,gk->bghp", w.reshape(ng, GROUPS_PER_STEP, S5_STATE, GROUPS_PER_STEP, S5_GROUP), eye).reshape(-1, S5_GROUP, S5_STATE)

    return ub(dWb[:, :, :ST_LANES]), ub(dWb[:, :, ST_LANES:]), uc(dWc[:, :ST_LANES, :]), -uc(dWc[:, ST_LANES:, :])


def _s5_disc(a_re, a_im, log_dt, b_re, b_im):
    dt = jnp.exp(log_dt)[:, None]
    mag = jnp.exp(a_re * dt)
    ab_re = mag * jnp.cos(a_im * dt)
    ab_im = mag * jnp.sin(a_im * dt)
    den = a_re * a_re + a_im * a_im
    nr, ni = ab_re - 1, ab_im
    f_re = (nr * a_re + ni * a_im) / den
    f_im = (ni * a_re - nr * a_im) / den
    bb_re = f_re[..., None] * b_re - f_im[..., None] * b_im
    bb_im = f_re[..., None] * b_im + f_im[..., None] * b_re
    return ab_re, ab_im, bb_re, bb_im


ROW_CHUNK = 512


def _s5_fwd(u, Wb, Wc, cf, d, xsrc, *, n_ex, name):
    T, D = u.shape
    S = T // n_ex
    ng = D // U_LANES
    rc = min(ROW_CHUNK, S)

    def body(u_ref, wb_ref, wc_ref, cf_ref, d_ref, xsrc_ref, y_ref, gy_ref, gyt_ref, st_ref, xout_ref, re_s, im_s, *sems):
        step = pl.program_id(0) * ng + pl.program_id(1)
        exch = _ChipExchange(xsrc_ref, xout_ref, *sems, scatter=False)

        @pl.when(step == 0)
        def _():
            exch.start()

        for r in range(S // rc):
            rows = pl.ds(r * rc, rc)
            bu = jnp.dot(u_ref[rows, :].astype(BF16), wb_ref[...], preferred_element_type=F32)
            re_s[rows, :] = bu[:, :ST_LANES]
            im_s[rows, :] = bu[:, ST_LANES:]
        for l0 in range(0, ST_LANES, SCAN_LANES):
            _scan_tiles(re_s, im_s, cf_ref, l0, S // 8, False)
        for r in range(S // rc):
            rows = pl.ds(r * rc, rc)
            st = jnp.concatenate([re_s[rows, :], im_s[rows, :]], axis=1).astype(BF16)
            st_ref[rows, :] = st
            y = jnp.dot(st, wc_ref[...], preferred_element_type=F32) + d_ref[...] * u_ref[rows, :]
            y_ref[rows, :] = y
            gy = _gelu(y)
            gy_ref[rows, :] = gy.astype(BF16)
            gyt_ref[:, rows] = gy.T.astype(BF16)

        @pl.when(step == n_ex * ng - 1)
        def _():
            exch.wait()

    return pl.pallas_call(
        body, name=name, grid=(n_ex, ng),
        in_specs=[pl.BlockSpec((S, U_LANES), lambda e, g: (e, g)),
                  pl.BlockSpec((None, U_LANES, 2 * ST_LANES), lambda e, g: (g, 0, 0)),
                  pl.BlockSpec((None, 2 * ST_LANES, U_LANES), lambda e, g: (g, 0, 0)),
                  pl.BlockSpec((None, 8, 8, ST_LANES), lambda e, g: (g, 0, 0, 0)),
                  pl.BlockSpec((1, U_LANES), lambda e, g: (0, g)), ANY],
        out_specs=[pl.BlockSpec((S, U_LANES), lambda e, g: (e, g))] * 2 + [pl.BlockSpec((U_LANES, S), lambda e, g: (g, e)),
                   pl.BlockSpec((S, 2 * ST_LANES), lambda e, g: (e, g)), ANY],
        out_shape=[jax.ShapeDtypeStruct((T, D), F32), jax.ShapeDtypeStruct((T, D), BF16), jax.ShapeDtypeStruct((D, T), BF16),
                   jax.ShapeDtypeStruct((T, ng * 2 * ST_LANES), BF16), _ChipExchange.out_shape(xsrc, False)],
        scratch_shapes=[pltpu.VMEM((S, ST_LANES), F32)] * 2 + _ChipExchange.SCRATCH,
        compiler_params=_cp(("arbitrary", "arbitrary")),
    )(u, Wb, Wc, cf, d, xsrc)


def _s5_bwd(u, y, dgy, st, Wb, Wc, cr, d, xsrc, *, n_ex, name):
    T, D = u.shape
    S = T // n_ex
    ng = D // U_LANES
    rc = min(ROW_CHUNK, S)
    nch = S // 8
    grp = 8 * SCAN_UNROLL
    assert grp % 16 == 0

    def body(u_ref, y_ref, dgy_ref, st_ref, wb_ref, wc_ref, cr_ref, d_ref, xsrc_ref,
             du_ref, dwb_ref, dwc_ref, dab_ref, dd_ref, xout_ref, gr_s, gi_s, dy_s, *sems):
        e = pl.program_id(1)
        step = pl.program_id(0) * n_ex + e
        exch = _ChipExchange(xsrc_ref, xout_ref, *sems, scatter=True)

        @pl.when(step == 0)
        def _():
            exch.start()

        @pl.when(e == 0)
        def _():
            dwb_ref[...] = jnp.zeros_like(dwb_ref)
            dwc_ref[...] = jnp.zeros_like(dwc_ref)
            dab_ref[...] = jnp.zeros_like(dab_ref)
            dd_ref[...] = jnp.zeros_like(dd_ref)

        dd = jnp.zeros((1, U_LANES), F32)
        for r in range(S // rc):
            rows = pl.ds(r * rc, rc)
            ut = u_ref[rows, :]
            dy = dgy_ref[rows, :].astype(F32) * _gelu_grad(y_ref[rows, :])
            dy_s[rows, :] = dy
            dd = dd + _csum(dy * ut)
            go = lax.dot_general(dy.astype(BF16), wc_ref[...], (((1,), (1,)), ((), ())), preferred_element_type=F32)
            gr_s[rows, :] = go[:, :ST_LANES]
            gi_s[rows, :] = go[:, ST_LANES:]
        dd_ref[0:1, :] += dd
        row0 = lax.broadcasted_iota(jnp.int32, (8, SCAN_LANES), 0) == 0
        for l0 in range(0, ST_LANES, SCAN_LANES):
            lanes = pl.ds(l0, SCAN_LANES)

            def dab_group(first, tiles, acc, l0=l0):
                def states(r0, n, lane0):
                    return st_ref[pl.ds(pl.multiple_of(r0, 16), n), pl.ds(lane0, SCAN_LANES)].astype(F32)
                r0 = first * 8
                cur = states(r0, grp, l0), states(r0, grp, ST_LANES + l0)
                live = (first > 0).astype(F32)
                p0 = jnp.maximum(r0 - 16, 0)
                before = [states(p0, 16, l0)[8:16, :] * live, states(p0, 16, ST_LANES + l0)[8:16, :] * live]
                a_re, a_im = acc
                for t, (gr, gi) in enumerate(tiles):
                    here = [c[8 * t:8 * t + 8, :] for c in cur]
                    sr, si = [jnp.where(row0, pltpu.roll(b, 1, 0), pltpu.roll(h, 1, 0)) for b, h in zip(before, here)]
                    a_re, a_im = a_re + gr * sr + gi * si, a_im + gi * sr - gr * si
                    before = here
                return a_re, a_im

            res = _scan_tiles(gr_s, gi_s, cr_ref, l0, nch, True, extra=dab_group)
            dab_ref[0:1, lanes] += _csum(res[2])
            dab_ref[1:2, lanes] += _csum(res[3])
        for r in range(S // rc):
            rows = pl.ds(r * rc, rc)
            st = st_ref[rows, :]
            g = jnp.concatenate([gr_s[rows, :], gi_s[rows, :]], axis=1).astype(BF16)
            dyb = dy_s[rows, :].astype(BF16)
            dwc_ref[...] += lax.dot_general(st, dyb, (((0,), (0,)), ((), ())), preferred_element_type=F32)
            dwb_ref[...] += lax.dot_general(u_ref[rows, :].astype(BF16), g, (((0,), (0,)), ((), ())), preferred_element_type=F32)
            du = lax.dot_general(g, wb_ref[...], (((1,), (1,)), ((), ())), preferred_element_type=F32)
            du_ref[rows, :] = du + d_ref[...] * dy_s[rows, :]

        @pl.when(step == ng * n_ex - 1)
        def _():
            exch.wait()

    return pl.pallas_call(
        body, name=name, grid=(ng, n_ex),
        in_specs=[pl.BlockSpec((S, U_LANES), lambda g, e: (e, g))] * 3 + [
            pl.BlockSpec((S, 2 * ST_LANES), lambda g, e: (e, g)),
            pl.BlockSpec((None, U_LANES, 2 * ST_LANES), lambda g, e: (g, 0, 0)),
            pl.BlockSpec((None, 2 * ST_LANES, U_LANES), lambda g, e: (g, 0, 0)),
            pl.BlockSpec((None, 8, 8, ST_LANES), lambda g, e: (g, 0, 0, 0)),
            pl.BlockSpec((1, U_LANES), lambda g, e: (0, g)), ANY],
        out_specs=[pl.BlockSpec((S, U_LANES), lambda g, e: (e, g)),
                   pl.BlockSpec((None, U_LANES, 2 * ST_LANES), lambda g, e: (g, 0, 0)),
                   pl.BlockSpec((None, 2 * ST_LANES, U_LANES), lambda g, e: (g, 0, 0)),
                   pl.BlockSpec((None, 8, ST_LANES), lambda g, e: (g, 0, 0)),
                   pl.BlockSpec((None, 8, U_LANES), lambda g, e: (g, 0, 0)), ANY],
        out_shape=[jax.ShapeDtypeStruct((T, D), F32),
                   jax.ShapeDtypeStruct((ng, U_LANES, 2 * ST_LANES), F32),
                   jax.ShapeDtypeStruct((ng, 2 * ST_LANES, U_LANES), F32),
                   jax.ShapeDtypeStruct((ng, 8, ST_LANES), F32),
                   jax.ShapeDtypeStruct((ng, 8, U_LANES), F32), _ChipExchange.out_shape(xsrc, True)],
        scratch_shapes=[pltpu.VMEM((S, ST_LANES), F32)] * 2 + [pltpu.VMEM((S, U_LANES), F32)] + _ChipExchange.SCRATCH,
        compiler_params=_cp(("arbitrary", "arbitrary")),
    )(u, y, dgy, st, Wb, Wc, cr, d, xsrc)


TQ = 256
KW = 512
SUB = 128


def _head_masks():
    lane = lax.broadcasted_iota(jnp.int32, (1, 2 * HEAD_DIM), 1)
    m0 = (lane < HEAD_DIM).astype(F32)
    return m0, 1.0 - m0


def _head_norm(x, g, m0, m1):
    sq = x * x
    r0 = lax.rsqrt(jnp.sum(sq * m0, axis=-1, keepdims=True) / HEAD_DIM + EPS)
    r1 = lax.rsqrt(jnp.sum(sq * m1, axis=-1, keepdims=True) / HEAD_DIM + EPS)
    r = m0 * r0 + m1 * r1
    return x * r, r


def _head_norm_bwd(dy, n, r, g, m0, m1):
    dn = dy * g
    p = dn * n
    mean = (m0 * jnp.sum(p * m0, axis=-1, keepdims=True) + m1 * jnp.sum(p * m1, axis=-1, keepdims=True)) / HEAD_DIM
    return r * (dn - n * mean), _csum(dy * n)


def _pair_matrix(kind):
    r = lax.broadcasted_iota(jnp.int32, (2 * SUB, 2 * SUB), 0)
    c = lax.broadcasted_iota(jnp.int32, (2 * SUB, 2 * SUB), 1)
    same = (r < SUB) == (c < SUB)
    rel = {"after": r > c, "upto": r <= c, "before": r < c}[kind]
    return jnp.logical_and(same, rel).astype(BF16)


def _block_sums(x, mat, carry, reverse, terms=2):
    hi = x.astype(BF16)
    lo = (x - hi.astype(F32)).astype(BF16) if terms == 2 else None
    npair = x.shape[1] // (2 * SUB)
    parts = [None] * (2 * npair)
    for p in (range(npair - 1, -1, -1) if reverse else range(npair)):
        sl = slice(2 * SUB * p, 2 * SUB * (p + 1))
        loc = jnp.dot(hi[:, sl], mat, preferred_element_type=F32)
        if terms == 2:
            loc = loc + jnp.dot(lo[:, sl], mat, preferred_element_type=F32)
        for b in ((1, 0) if reverse else (0, 1)):
            k = 2 * p + b
            parts[k] = loc[:, SUB * b:SUB * (b + 1)] + carry
            carry = carry + jnp.sum(x[:, SUB * k:SUB * (k + 1)], axis=-1, keepdims=True)
    return jnp.concatenate(parts, axis=1), carry


def _sb_logits(z, mask):
    lp = jnp.minimum(z, 0.0) - jnp.log(1.0 + jnp.exp(-jnp.abs(z)))
    lf = lp - z
    if mask is not None:
        lf = jnp.where(mask, lf, 0.0)
    return lp, lf


def _causal_mask(row0, col0, kw):
    r = row0 + lax.broadcasted_iota(jnp.int32, (TQ, kw), 0)
    c = col0 + lax.broadcasted_iota(jnp.int32, (TQ, kw), 1)
    return c < r


def _transposed_windows(x, ref):
    for w in range(x.shape[0] // KW):
        ref[w] = x[w * KW:(w + 1) * KW, :].T.astype(BF16)


def _attn_fwd(q, kv, qg, kg, xsrc, *, n_ex, name):
    T, D = q.shape
    S = T // n_ex
    nhp = D // (2 * HEAD_DIM)
    nq = S // TQ
    scale = 1.0 / math.sqrt(HEAD_DIM)

    def body(q_ref, k_ref, v_ref, qg_ref, kg_ref, xsrc_ref, o_ref, tot_ref, ot_ref, xout_ref, kT_s, qm_s, vm_s, *sems):
        step = pl.program_id(0) * nhp + pl.program_id(1)
        exch = _ChipExchange(xsrc_ref, xout_ref, *sems, scatter=False)

        @pl.when(step == 0)
        def _():
            exch.start()

        m0, m1 = _head_masks()
        qn, _ = _head_norm(q_ref[...], None, m0, m1)
        qn = qn * (qg_ref[...] * scale)
        kn, _ = _head_norm(k_ref[...], None, m0, m1)
        _transposed_windows(kn * kg_ref[...], kT_s)
        v = v_ref[...]
        for h, m in enumerate((m0, m1)):
            qm_s[h] = (qn * m).astype(BF16)
            vm_s[h] = (v * m).astype(BF16)
        u_after = _pair_matrix("after")

        def window(rows, win, st, mask, kw):
            keys = pl.ds(pl.multiple_of(win * KW, KW), kw)
            zs = [jnp.dot(qm_s[h, rows, :], kT_s[win, :, :kw], preferred_element_type=F32) for h in range(2)]
            lg = [_sb_logits(zs[h], mask) for h in range(2)]
            sums = [_block_sums(lg[h][1], u_after, st[2 * h], True) for h in range(2)]
            out = ()
            for h in range(2):
                w = jnp.exp(lg[h][0] + sums[h][0])
                if mask is not None:
                    w = jnp.where(mask, w, 0.0)
                out += (sums[h][1], st[2 * h + 1] + jnp.dot(w.astype(BF16), vm_s[h, keys, :], preferred_element_type=F32))
            return out

        def qtile(iq, last, kw):
            rows = pl.ds(pl.multiple_of(iq * TQ, TQ), TQ)
            mask = _causal_mask(iq * TQ, last * KW, kw)
            z1, zq = jnp.zeros((TQ, 1), F32), jnp.zeros((TQ, 2 * HEAD_DIM), F32)
            st = window(rows, last, (z1, zq, z1, zq), mask, kw)
            st = lax.fori_loop(0, last, lambda jj, st: window(rows, last - 1 - jj, st, None, KW), st)
            o_ref[rows, :] = st[1] + st[3]
            tot_ref[rows, :] = st[0] * m0 + st[2] * m1

        def qtiles_of_window(a, _):
            for sub in range(KW // TQ):
                qtile(a * (KW // TQ) + sub, a, (sub + 1) * TQ)
            return 0

        lax.fori_loop(0, S // KW, qtiles_of_window, 0)
        ot_ref[...] = o_ref[...].T.astype(BF16)

        @pl.when(step == n_ex * nhp - 1)
        def _():
            exch.wait()

    assert S % KW == 0 and KW % TQ == 0
    nwin = S // KW
    blk = (S, 2 * HEAD_DIM)
    return pl.pallas_call(
        body, name=name, grid=(n_ex, nhp),
        in_specs=[pl.BlockSpec(blk, lambda e, h: (e, h)), pl.BlockSpec(blk, lambda e, h: (e, h)),
                  pl.BlockSpec(blk, lambda e, h: (e, h + nhp)),
                  pl.BlockSpec((1, 2 * HEAD_DIM), lambda e, h: (0, 0)), pl.BlockSpec((1, 2 * HEAD_DIM), lambda e, h: (0, 0)), ANY],
        out_specs=[pl.BlockSpec(blk, lambda e, h: (e, h))] * 2 + [pl.BlockSpec((2 * HEAD_DIM, S), lambda e, h: (h, e)), ANY],
        out_shape=[jax.ShapeDtypeStruct((T, D), F32)] * 2 + [jax.ShapeDtypeStruct((D, T), BF16), _ChipExchange.out_shape(xsrc, False)],
        scratch_shapes=[pltpu.VMEM((nwin, 2 * HEAD_DIM, KW), BF16), pltpu.VMEM((2,) + blk, BF16), pltpu.VMEM((2,) + blk, BF16)]
        + _ChipExchange.SCRATCH,
        compiler_params=_cp(("arbitrary", "arbitrary")),
    )(q, kv, kv, qg, kg, xsrc)


def _attn_bwd(q, kv, tot, do, qg, kg, *, n_ex, name):
    T, D = q.shape
    S = T // n_ex
    nhp = D // (2 * HEAD_DIM)
    nq = S // TQ
    scale = 1.0 / math.sqrt(HEAD_DIM)

    def body(q_ref, k_ref, v_ref, tot_ref, do_ref, qg_ref, kg_ref, dq_ref, dk_ref, dv_ref, dqg_ref, dkg_ref,
             kT_s, vT_s, km_s, qm_s, dom_s, dqn_s, dkT_s, dvT_s):
        m0, m1 = _head_masks()
        qn, qr = _head_norm(q_ref[...], None, m0, m1)
        kn, kr = _head_norm(k_ref[...], None, m0, m1)
        qs = qn * (qg_ref[...] * scale)
        kk = kn * kg_ref[...]
        _transposed_windows(kk, kT_s)
        _transposed_windows(v_ref[...], vT_s)
        do = do_ref[...]
        for h, m in enumerate((m0, m1)):
            qm_s[h] = (qs * m).astype(BF16)
            km_s[h] = (kk * m).astype(BF16)
            dom_s[h] = (do * m).astype(BF16)
        dkT_s[...] = jnp.zeros_like(dkT_s)
        dvT_s[...] = jnp.zeros_like(dvT_s)
        u_upto, u_before = _pair_matrix("upto"), _pair_matrix("before")

        def both(inv, win, st, mask, kw):
            keys = pl.ds(pl.multiple_of(win * KW, KW), kw)
            lg = [_sb_logits(jnp.dot(inv[h][0], kT_s[win, :, :kw], preferred_element_type=F32), mask) for h in range(2)]
            s_lf = [_block_sums(lg[h][1], u_upto, st[3 * h], False) for h in range(2)]
            ws, ews = [], []
            for h in range(2):
                w = jnp.exp(lg[h][0] - s_lf[h][0])
                if mask is not None:
                    w = jnp.where(mask, w, 0.0)
                ws.append(w)
                ews.append(jnp.dot(inv[h][2], vT_s[win, :, :kw], preferred_element_type=F32) * w)
            s_e = [_block_sums(ews[h], u_before, st[3 * h + 1], False, terms=1) for h in range(2)]
            out, dk, dv = (), None, None
            for h in range(2):
                sig = jnp.exp(lg[h][0])
                dz = ews[h] - sig * (ews[h] + s_e[h][0])
                if mask is not None:
                    dz = jnp.where(mask, dz, 0.0)
                dzb = dz.astype(BF16)
                out += (s_lf[h][1], s_e[h][1], st[3 * h + 2] + jnp.dot(dzb, km_s[h, keys, :], preferred_element_type=F32))
                dkh = jnp.dot(inv[h][1], dzb, preferred_element_type=F32)
                dvh = jnp.dot(inv[h][3], ws[h].astype(BF16), preferred_element_type=F32)
                dk, dv = (dkh, dvh) if h == 0 else (dk + dkh, dv + dvh)
            dkT_s[win, :, :kw] += dk
            dvT_s[win, :, :kw] += dv
            return out

        def qtile(iq, last, kw):
            rows = pl.ds(pl.multiple_of(iq * TQ, TQ), TQ)
            mask = _causal_mask(iq * TQ, last * KW, kw)
            tt = tot_ref[rows, :]
            inv, neg_total = [], []
            for h, m in enumerate((m0, m1)):
                qh, doh = qm_s[h, rows, :], dom_s[h, rows, :]
                neg_total.append(jnp.sum(tt * m, axis=-1, keepdims=True) * (-1.0 / HEAD_DIM))
                inv.append((qh, qh.astype(F32).T.astype(BF16), doh, doh.astype(F32).T.astype(BF16)))

            z1, zq = jnp.zeros((TQ, 1), F32), jnp.zeros((TQ, 2 * HEAD_DIM), F32)
            st = lax.fori_loop(0, last, lambda win, st: both(inv, win, st, None, KW), (neg_total[0], z1, zq, neg_total[1], z1, zq))
            st = both(inv, last, st, mask, kw)
            dqn_s[rows, :] = st[2] + st[5]

        def qtiles_of_window(a, _):
            for sub in range(KW // TQ):
                qtile(a * (KW // TQ) + sub, a, (sub + 1) * TQ)
            return 0

        lax.fori_loop(0, S // KW, qtiles_of_window, 0)
        dkn = jnp.concatenate([dkT_s[w].T for w in range(nwin)], axis=0)
        dq, dqg = _head_norm_bwd(dqn_s[...] * scale, qn, qr, qg_ref[...], m0, m1)
        dk, dkg = _head_norm_bwd(dkn, kn, kr, kg_ref[...], m0, m1)
        dq_ref[...] = dq
        dk_ref[...] = dk
        dv_ref[...] = jnp.concatenate([dvT_s[w].T for w in range(nwin)], axis=0)
        dqg_ref[...] = dqg
        dkg_ref[...] = dkg

    assert S % KW == 0 and KW % TQ == 0
    nwin = S // KW
    blk = (S, 2 * HEAD_DIM)
    tblk = (nwin, 2 * HEAD_DIM, KW)
    gblk = (None, None, 1, 2 * HEAD_DIM)
    dq, dk, dv, dqg, dkg = pl.pallas_call(
        body, name=name, grid=(n_ex, nhp),
        in_specs=[pl.BlockSpec(blk, lambda e, h: (e, h)), pl.BlockSpec(blk, lambda e, h: (e, h)),
                  pl.BlockSpec(blk, lambda e, h: (e, h + nhp)),
                  pl.BlockSpec(blk, lambda e, h: (e, h)), pl.BlockSpec(blk, lambda e, h: (e, h)),
                  pl.BlockSpec((1, 2 * HEAD_DIM), lambda e, h: (0, 0)), pl.BlockSpec((1, 2 * HEAD_DIM), lambda e, h: (0, 0))],
        out_specs=[pl.BlockSpec(blk, lambda e, h: (e, h))] * 3 + [pl.BlockSpec(gblk, lambda e, h: (e, h, 0, 0))] * 2,
        out_shape=[jax.ShapeDtypeStruct((T, D), F32)] * 3 + [jax.ShapeDtypeStruct((n_ex, nhp, 1, 2 * HEAD_DIM), F32)] * 2,
        scratch_shapes=[pltpu.VMEM(tblk, BF16), pltpu.VMEM(tblk, BF16),
                        pltpu.VMEM((2,) + blk, BF16), pltpu.VMEM((2,) + blk, BF16), pltpu.VMEM((2,) + blk, BF16),
                        pltpu.VMEM(blk, F32), pltpu.VMEM(tblk, F32), pltpu.VMEM(tblk, F32)],
        compiler_params=_cp(("parallel", "parallel")),
    )(q, kv, kv, tot, do, qg, kg)
    return dq, dk, dv, dqg, dkg


def _place():
    return lax.axis_index("x"), lax.axis_index("y"), lax.axis_index("c")


def _all_gather8(x_shard, *, name):
    m_per, n = x_shard.shape

    def body(x_ref, out_ref, send_sems, recv_sems, local_sem):
        x, y, c = _place()
        me, sibling = (x, y, c), (x, y, 1 - c)
        chips = [(1 - x, y), (x, 1 - y), (1 - x, 1 - y)]

        def rows(px, py, pc):
            return out_ref.at[pl.ds((4 * px + 2 * py + pc) * m_per, m_per), :]

        def copy(k, block, to, src=None):
            return pltpu.make_async_remote_copy(
                src_ref=rows(*block) if src is None else src, dst_ref=rows(*block),
                send_sem=send_sems.at[k], recv_sem=recv_sems.at[k], device_id=to, device_id_type=MESH)

        mine = pltpu.make_async_copy(x_ref, rows(*me), local_sem)
        mine.start()
        first = [copy(0, me, sibling, src=x_ref)]
        first += [copy(1 + j, me, (*chip, c), src=x_ref) for j, chip in enumerate(chips)]
        for cp in first:
            cp.start()
        passed = [copy(4 + j, (*chip, c), sibling) for j, chip in enumerate(chips)]
        for j, chip in enumerate(chips):
            copy(1 + j, (*chip, c), me).wait_recv()
            passed[j].start()
        copy(0, sibling, me).wait_recv()
        for j, chip in enumerate(chips):
            copy(4 + j, (*chip, 1 - c), me).wait_recv()
        for cp in first + passed:
            cp.wait_send()
        mine.wait()

    return pl.pallas_call(
        body, name=name, out_shape=jax.ShapeDtypeStruct((8 * m_per, n), x_shard.dtype),
        in_specs=[pl.BlockSpec(memory_space=pltpu.VMEM)], out_specs=pl.BlockSpec(memory_space=pltpu.VMEM),
        scratch_shapes=[pltpu.SemaphoreType.DMA((7,)), pltpu.SemaphoreType.DMA((7,)), pltpu.SemaphoreType.DMA],
        compiler_params=pltpu.CompilerParams(vmem_limit_bytes=VMEM_LIMIT),
    )(x_shard)


def _sibling_sum_half(x, *, name):
    R, C = x.shape
    half = R // 2
    assert half % 16 == 0

    def body(x_ref, o_ref, theirs, send_sem, recv_sem):
        px, py, pc = _place()
        cp = pltpu.make_async_remote_copy(src_ref=x_ref, dst_ref=theirs, send_sem=send_sem, recv_sem=recv_sem,
                                          device_id=(px, py, 1 - pc), device_id_type=MESH)
        cp.start()
        cp.wait()
        rows = pl.ds(pl.multiple_of(pc * half, 8), half)
        o_ref[...] = (x_ref[rows, :] + theirs[rows, :]).astype(BF16)

    return pl.pallas_call(
        body, name=name, out_shape=jax.ShapeDtypeStruct((half, C), BF16),
        in_specs=[pl.BlockSpec(memory_space=pltpu.VMEM)], out_specs=pl.BlockSpec(memory_space=pltpu.VMEM),
        scratch_shapes=[pltpu.VMEM((R, C), x.dtype), pltpu.SemaphoreType.DMA, pltpu.SemaphoreType.DMA],
        compiler_params=pltpu.CompilerParams(vmem_limit_bytes=VMEM_LIMIT),
    )(x)


def _sum_blocks(x, n, *, name):
    R = x.shape[0] // n

    def body(x_ref, o_ref):
        acc = x_ref[pl.ds(0, R), :].astype(F32)
        for k in range(1, n):
            acc = acc + x_ref[pl.ds(k * R, R), :].astype(F32)
        o_ref[...] = acc

    return pl.pallas_call(body, name=name, out_shape=jax.ShapeDtypeStruct((R, x.shape[1]), F32),
                          compiler_params=pltpu.CompilerParams(vmem_limit_bytes=VMEM_LIMIT))(x)


def _colsum(x, *, name):
    def body(x_ref, o_ref):
        o_ref[...] = jnp.sum(x_ref[...], axis=0, keepdims=True)
    return pl.pallas_call(body, name=name, out_shape=jax.ShapeDtypeStruct((1, x.shape[1]), x.dtype))(x)


ANY = pl.BlockSpec(memory_space=pl.ANY)


class _ChipExchange:
    SCRATCH = [pltpu.SemaphoreType.DMA((3,)), pltpu.SemaphoreType.DMA((3,)), pltpu.SemaphoreType.DMA]

    @staticmethod
    def out_shape(src, scatter):
        return jax.ShapeDtypeStruct(((4,) + tuple(src.shape[1:])) if scatter else ((4, 2) + tuple(src.shape[1:])), src.dtype)

    def __init__(self, src_ref, out_ref, send_sems, recv_sems, local_sem, scatter):
        x, y, c = _place()
        myj = 2 * x + y
        chips = [(1 - x, y), (x, 1 - y), (1 - x, 1 - y)]

        def slot(j):
            return out_ref.at[j] if scatter else out_ref.at[j, c]

        def piece(j):
            return src_ref.at[j] if scatter else src_ref.at[c]

        self.mine = pltpu.make_async_copy(piece(myj), slot(myj), local_sem)
        self.sends = [pltpu.make_async_remote_copy(
            src_ref=piece(2 * cx + cy), dst_ref=slot(myj), send_sem=send_sems.at[k], recv_sem=recv_sems.at[k],
            device_id=(cx, cy, c), device_id_type=MESH) for k, (cx, cy) in enumerate(chips)]
        self.recvs = [pltpu.make_async_remote_copy(
            src_ref=slot(2 * cx + cy), dst_ref=slot(2 * cx + cy), send_sem=send_sems.at[k], recv_sem=recv_sems.at[k],
            device_id=(cx, cy, c), device_id_type=MESH) for k, (cx, cy) in enumerate(chips)]

    def start(self):
        self.mine.start()
        for cp in self.sends:
            cp.start()

    def wait(self):
        for cp in self.recvs:
            cp.wait_recv()
        for cp in self.sends:
            cp.wait_send()
        self.mine.wait()


def _sibling_fill(buf, *, axis, name):
    def half(ref, h):
        return ref.at[h] if axis == 0 else ref.at[:, h]

    def body(in_ref, out_ref, send_sem, recv_sem):
        x, y, c = _place()
        cp = pltpu.make_async_remote_copy(src_ref=half(out_ref, c), dst_ref=half(out_ref, c), send_sem=send_sem, recv_sem=recv_sem,
                                          device_id=(x, y, 1 - c), device_id_type=MESH)
        cp.start()
        pltpu.make_async_remote_copy(src_ref=half(out_ref, 1 - c), dst_ref=half(out_ref, 1 - c), send_sem=send_sem, recv_sem=recv_sem,
                                     device_id=(x, y, 1 - c), device_id_type=MESH).wait_recv()
        cp.wait_send()

    return pl.pallas_call(
        body, name=name, out_shape=jax.ShapeDtypeStruct(buf.shape, buf.dtype), in_specs=[ANY], out_specs=ANY,
        input_output_aliases={0: 0}, scratch_shapes=[pltpu.SemaphoreType.DMA, pltpu.SemaphoreType.DMA],
    )(buf)


def _sibling_swap_half(g, *, name):
    def body(g_ref, out_ref, send_sem, recv_sem):
        x, y, c = _place()
        cp = pltpu.make_async_remote_copy(src_ref=g_ref.at[:, 1 - c], dst_ref=out_ref, send_sem=send_sem, recv_sem=recv_sem,
                                          device_id=(x, y, 1 - c), device_id_type=MESH)
        cp.start()
        cp.wait()

    return pl.pallas_call(
        body, name=name, out_shape=jax.ShapeDtypeStruct((g.shape[0],) + g.shape[2:], g.dtype), in_specs=[ANY], out_specs=ANY,
        scratch_shapes=[pltpu.SemaphoreType.DMA, pltpu.SemaphoreType.DMA],
    )(g)


def _add_my_half(g, b, cidx, *, name, tr=1024):
    n, _, R, C = g.shape
    tr = max(t for t in range(16, tr + 1, 16) if R % t == 0)

    def body(c_ref, g_ref, b_ref, o_ref):
        o_ref[...] = (g_ref[...] + b_ref[...]).astype(o_ref.dtype)

    return pl.pallas_call(
        body, name=name, out_shape=jax.ShapeDtypeStruct((n, R, C), BF16),
        grid_spec=pltpu.PrefetchScalarGridSpec(
            num_scalar_prefetch=1, grid=(n, R // tr),
            in_specs=[pl.BlockSpec((None, None, tr, C), lambda j, i, c: (j, c[0], i, 0)),
                      pl.BlockSpec((None, tr, C), lambda j, i, c: (j, i, 0))],
            out_specs=pl.BlockSpec((None, tr, C), lambda j, i, c: (j, i, 0))),
        compiler_params=_cp(("parallel", "parallel")),
    )(cidx, g, b)


def _sum4_into_half(q, cidx, *, name, tr=1024):
    _, R, C = q.shape
    tr = max(t for t in range(16, tr + 1, 16) if R % t == 0)

    def body(c_ref, q_ref, o_ref):
        o_ref[...] = ((q_ref[0].astype(F32) + q_ref[1].astype(F32)) + q_ref[2].astype(F32)) + q_ref[3].astype(F32)

    return pl.pallas_call(
        body, name=name, out_shape=jax.ShapeDtypeStruct((2, R, C), F32),
        grid_spec=pltpu.PrefetchScalarGridSpec(
            num_scalar_prefetch=1, grid=(R // tr,),
            in_specs=[pl.BlockSpec((4, tr, C), lambda i, c: (0, i, 0))],
            out_specs=pl.BlockSpec((None, tr, C), lambda i, c: (c[0], i, 0))),
        compiler_params=_cp(("parallel",)),
    )(cidx, q)


def _pack_rows(parts, width=1024, row_multiple=8):
    rows, spans, r0 = [], [], 0
    for p in parts:
        n = p.size
        nr = 8 * (-(-n // (8 * width)))
        flat = p.reshape(-1)
        if nr * width != n:
            flat = jnp.pad(flat, (0, nr * width - n))
        rows.append(flat.reshape(nr, width))
        spans.append((r0, nr, n, p.shape))
        r0 += nr
    if r0 % row_multiple:
        rows.append(jnp.zeros((row_multiple - r0 % row_multiple, width), parts[0].dtype))
    return jnp.concatenate(rows, axis=0), spans


def _unpack_rows(buf, spans):
    return [buf[r0:r0 + nr].reshape(-1)[:n].reshape(shape) for (r0, nr, n, shape) in spans]


def kernel(x, c, ada_w, ada_b, mix_norm_g, mlp_norm_g, mlp_w1, mlp_w2, s5_a_re, s5_a_im, s5_log_dt, s5_b_re, s5_b_im, s5_c_re, s5_c_im, s5_d, s5_w_glu, kv_ada_w, kv_ada_b, kv_norm_g, w_kv, k_norm_g, sb_w_q, q_norm_g, sb_w_o, loss_target, m_ada_w, m_ada_b, m_mix_norm_g, m_mlp_norm_g, m_mlp_w1, m_mlp_w2, m_s5_a_re, m_s5_a_im, m_s5_log_dt, m_s5_b_re, m_s5_b_im, m_s5_c_re, m_s5_c_im, m_s5_d, m_s5_w_glu, m_kv_ada_w, m_kv_ada_b, m_kv_norm_g, m_w_kv, m_k_norm_g, m_sb_w_q, m_q_norm_g, m_sb_w_o, v_ada_w, v_ada_b, v_mix_norm_g, v_mlp_norm_g, v_mlp_w1, v_mlp_w2, v_s5_a_re, v_s5_a_im, v_s5_log_dt, v_s5_b_re, v_s5_b_im, v_s5_c_re, v_s5_c_im, v_s5_d, v_s5_w_glu, v_kv_ada_w, v_kv_ada_b, v_kv_norm_g, v_w_kv, v_k_norm_g, v_sb_w_q, v_q_norm_g, v_sb_w_o):
    E, S, D = x.shape
    T = E * S
    FF = 4 * D
    NB = 8 * E
    px, py, pc = _place()
    chip = 2 * px + py
    dev = 4 * px + 2 * py + pc
    cidx = jnp.reshape(pc, (1,)).astype(jnp.int32)
    x0 = x.reshape(T, D)
    tgt = loss_target.reshape(T, D)

    nc_rows, nd = c.size // 128, s5_d.size // 128
    cd = jnp.concatenate([c.reshape(nc_rows, 128), jnp.pad(s5_d.reshape(nd, 128), ((0, 8 - nd), (0, 0)))], axis=0)
    cd_all = _all_gather8(cd, name="ag_c_d").reshape(8, nc_rows + 8, 128)
    c_all = cd_all[:, :nc_rows].reshape(NB, D)
    d_full = cd_all.reshape(4, 2, nc_rows + 8, 128)[:, 0, nc_rows:nc_rows + nd].reshape(1, D)
    sc_all = (c_all * _sigmoid(c_all)).astype(BF16)
    wa = ada_w.shape[2]
    wk = kv_ada_w.shape[1]
    m_sh = jnp.concatenate([_mm(sc_all, _Layer(ada_w, 0), "nn", name="ada0", tn=256),
                            _mm(sc_all, _Layer(ada_w, 1), "nn", name="ada1", tn=256),
                            _mm(sc_all, kv_ada_w, "nn", name="ada_kv", tn=256)], axis=1)
    m_all = _all_gather8(m_sh, name="ag_m").reshape(4, 2, NB, 2 * wa + wk)[:, 0]
    mods = []
    for l in range(2):
        full = jnp.transpose(m_all[:, :, l * wa:(l + 1) * wa], (1, 0, 2)).reshape(NB, 6 * D) + ada_b[l]
        mine = lax.dynamic_slice_in_dim(full, E * dev, E, axis=0)
        mods.append([mine[:, i * D:(i + 1) * D].reshape(E, 1, D) for i in range(6)])
    full = jnp.transpose(m_all[:, :, 2 * wa:], (1, 0, 2)).reshape(NB, 2 * D) + kv_ada_b
    mine = lax.dynamic_slice_in_dim(full, E * dev, E, axis=0)
    kv_sh, kv_sc = [mine[:, i * D:(i + 1) * D].reshape(E, 1, D) for i in range(2)]

    wpack_a = jnp.concatenate([mlp_w1[0], mlp_w2[0], jnp.concatenate([s5_w_glu[0], w_kv], axis=1), sb_w_q[0]], axis=0).astype(BF16)
    wpack_b = jnp.concatenate([mlp_w1[1], mlp_w2[1], sb_w_o[0]], axis=0).astype(BF16)
    RA, RB = wpack_a.shape[0], wpack_b.shape[0]
    RW = RA + RB

    tm = min(2048, S)
    tm_res = min(1024, S)
    gbuf = [jax.ShapeDtypeStruct((4, RW, D), F32)]

    def grad_mm(act, dout, kind, roff, nr, c0, nc, name, transposed=False):
        gbuf[0] = _mm(act, dout, "nn" if transposed else "tn", name=name, tm=1024, tk=2048,
                      into=_Sharded(gbuf[0], kind, roff, nr, c0, nc))

    def mlp_fwd(xa, l, mod):
        sh_m, sc_m, g_m = mod[3], mod[4], mod[5]
        h, h_t = _norm_mod_fwd(xa, mlp_norm_g[l:l + 1], sh_m, sc_m, n_ex=E, out_dtype=BF16, name=f"mlp_norm{l}", with_transpose=True)

        def relu_sq(acc):
            ra = jnp.maximum(acc, 0.0)
            return ra * ra, ra
        r, ra = _mm(h, W1[l], "nn", name=f"mlp_up{l}", out_dtypes=(BF16, BF16), tm=tm, epilogue=relu_sq)
        xb, ff = _mm(r, W2[l], "nn", name=f"mlp_down{l}", out_dtypes=(F32, F32), tm=tm_res,
                     extras=[_mn_extra(xa), _vec_extra(g_m, S)],
                     epilogue=lambda acc, xat, gt: (xat + gt * acc, acc))
        return xb, (h_t, r, ra, ff)

    def mlp_bwd(dxb, dff, xa, l, mod, saved, gated=None):
        sc_m = mod[4]
        h_t, r, ra, _ = saved
        da = _mm(dff, W2[l], "nt", name=f"mlp_down_dx{l}", out_dtypes=(BF16,), tm=tm, extras=[_mn_extra(ra)],
                 epilogue=lambda acc, rat: (acc * (2.0 * rat.astype(F32)),))
        grad_mm(r, dff, "rows", (2 + l) * D, D, 0, D, f"mlp_down_dw{l}")
        dh = _mm(da, W1[l], "nt", name=f"mlp_up_dx{l}", tm=tm)
        grad_mm(h_t, da, "cols", l * D, D, 0, D, f"mlp_up_dw{l}", transposed=True)
        return _norm_mod_bwd(xa, dh, dxb, mlp_norm_g[l:l + 1], sc_m, n_ex=E, name=f"mlp_norm_bwd{l}", gated=gated)

    ab_re, ab_im, bb_re, bb_im = _s5_disc(s5_a_re[0], s5_a_im[0], s5_log_dt[0], s5_b_re[0], s5_b_im[0])
    cf, cr = _s5_consts(ab_re, ab_im)
    Wb, Wc = _s5_blockdiag(bb_re, bb_im, s5_c_re[0], s5_c_im[0])
    ng = D // U_LANES

    mod0, mod1 = mods
    h0 = _norm_mod_fwd(x0, mix_norm_g[0:1], mod0[0], mod0[1], n_ex=E, out_dtype=F32, name="mix_norm0")
    y, gy, gy_t, s5_states, wfull_a = _s5_fwd(h0, Wb, Wc, cf, d_full, wpack_a.reshape(2, RA // 2, D), n_ex=E, name="s5_fwd")
    wfull_a = _sibling_fill(wfull_a, axis=1, name="wgather_a_d2d").reshape(4, RA, D)

    W1 = [_Sharded(wfull_a, "cols", 0, D, 0, D), None]
    W2 = [_Sharded(wfull_a, "rows", D, D, 0, D), None]
    Wglu = _Sharded(wfull_a, "cols", 2 * D, D, 0, D // 2)
    Wkv = _Sharded(wfull_a, "cols", 2 * D, D, D // 2, D // 2)
    Wq = _Sharded(wfull_a, "rows", 3 * D, D // 4, 0, D)
    vg = _mm(gy, Wglu, "nn", name="glu_up", tm=tm)
    (x1,), _ = _rowwise(lambda v, g, xt, ga: ([xt + ga * (v * _sigmoid(g))], []),
                        [(vg, D, 0), (vg, D, 1), (x0, D, 0)], [mod0[2]], [], [(D, F32)], [], n_ex=E, name="glu_gate")
    x2, saved_mlp0 = mlp_fwd(x1, 0, mod0)

    hkv, hkv_t = _norm_mod_fwd(x2, kv_norm_g.reshape(1, D), kv_sh, kv_sc, n_ex=E, out_dtype=BF16, name="kv_norm", with_transpose=True)
    kvf = _mm(hkv, Wkv, "nn", name="kv_proj", tm=tm)
    h1, h1_t = _norm_mod_fwd(x2, mix_norm_g[1:2], mod1[0], mod1[1], n_ex=E, out_dtype=BF16, name="mix_norm1", with_transpose=True)
    qf = _mm(h1, Wq, "nn", name="q_proj", tm=tm)
    qg2 = jnp.tile(q_norm_g.reshape(1, HEAD_DIM), (1, 2))
    kg2 = jnp.tile(k_norm_g.reshape(1, HEAD_DIM), (1, 2))
    o, lf_tot, o_t, wfull_b = _attn_fwd(qf, kvf, qg2, kg2, wpack_b.reshape(2, RB // 2, D), n_ex=E, name="attn_fwd")
    wfull_b = _sibling_fill(wfull_b, axis=1, name="wgather_b_d2d").reshape(4, RB, D)
    W1[1] = _Sharded(wfull_b, "cols", 0, D, 0, D)
    W2[1] = _Sharded(wfull_b, "rows", D, D, 0, D)
    Wo = _Sharded(wfull_b, "rows", 2 * D, D // 4, 0, D)
    x3, mix1 = _mm(o, Wo, "nn", name="o_proj", out_dtypes=(F32, F32), tm=tm_res,
                   extras=[_mn_extra(x2), _vec_extra(mod1[2], S)],
                   epilogue=lambda acc, xat, gt: (xat + gt * acc, acc))
    x4, saved_mlp1 = mlp_fwd(x3, 1, mod1)

    def loss_fn(xt, tt, fft, gmt):
        dx = (xt - tt) * (1.0 / D)
        dff, dgm = _gated(dx, fft, gmt)
        return [dx, dff], [_csum(jnp.square(xt - tt)) * (0.5 / D), dgm]
    (dx4, dff1), (lsum, dgm1) = _rowwise(loss_fn, [(x4, D, 0), (tgt, D, 0), (saved_mlp1[3], D, 0)], [mod1[5]], [],
                                         [(D, F32), (D, BF16)], [D, D], n_ex=E, name="loss")
    loss = lax.psum(jnp.sum(lsum), ("x", "y", "c"))

    (dx3, dmix1), (dsh_m1, dsc_m1, dg_mlp1, dga1) = mlp_bwd(dx4, dff1, x3, 1, mod1, saved_mlp1, gated=([(mix1, D, 0)], mod1[2], _gated, D))
    do = _mm(dmix1, Wo, "nt", name="o_proj_dx", tm=tm)
    grad_mm(o_t, dmix1, "rows", 5 * D + D // 4, D // 4, 0, D, "o_proj_dw", transposed=True)
    dq, dk, dv, dqg, dkg = _attn_bwd(qf, kvf, lf_tot, do, qg2, kg2, n_ex=E, name="attn_bwd")
    dh1 = _mm(dq, Wq, "nt", name="q_proj_dx", tm=tm)
    grad_mm(h1_t, dq, "rows", 5 * D, D // 4, 0, D, "q_proj_dw", transposed=True)
    (dx2,), (dsh_a1, dsc_a1, dg_mix1) = _norm_mod_bwd(x2, dh1, dx3, mix_norm_g[1:2], mod1[1], n_ex=E, name="mix_norm_bwd1")
    dkv = jnp.concatenate([dk, dv], axis=1)
    dhkv = _mm(dkv, Wkv, "nt", name="kv_proj_dx", tm=tm)
    grad_mm(hkv_t, dkv, "cols", 4 * D, D, D // 2, D // 2, "kv_proj_dw", transposed=True)
    (dx2, dff0), (dkv_sh, dkv_sc, dg_kv, dgm0) = _norm_mod_bwd(x2, dhkv, dx2, kv_norm_g.reshape(1, D), kv_sc, n_ex=E, name="kv_norm_bwd",
                                                               gated=([(saved_mlp0[3], D, 0)], mod0[5], _gated, D))

    def glu_bwd(d, v, g, ga):
        sg = _sigmoid(g)
        dm = ga * d
        return jnp.concatenate([dm * sg, dm * v * sg * (1.0 - sg)], axis=1).astype(BF16), _csum(d * (v * sg))
    (dx1, dvg), (dsh_m0, dsc_m0, dg_mlp0, dga0) = mlp_bwd(dx2, dff0, x1, 0, mod0, saved_mlp0,
                                                          gated=([(vg, D, 0), (vg, D, 1)], mod0[2], glu_bwd, 2 * D))
    dgy = _mm(dvg, Wglu, "nt", name="glu_up_dx", tm=tm)
    grad_mm(gy_t, dvg, "cols", 4 * D, D, 0, D // 2, "glu_up_dw", transposed=True)

    gpack = gbuf[0].reshape(4, 2, RW // 2, D)
    theirs = _sibling_swap_half(gpack, name="gscatter_d2d")
    chip_sum = _add_my_half(gpack, theirs, cidx, name="gscatter_add")
    dh0, dWb, dWc, dab, dd, from_chips = _s5_bwd(h0, y, dgy, s5_states, Wb, Wc, cr, d_full, chip_sum, n_ex=E, name="s5_bwd")
    ghalf = _sum4_into_half(from_chips, cidx, name="gscatter_sum")
    gsh = _sibling_fill(ghalf, axis=0, name="gscatter_fill").reshape(RW, D)
    (gx,), (dsh_a0, dsc_a0, dg_mix0) = _norm_mod_bwd(x0, dh0, dx1, mix_norm_g[0:1], mod0[1], n_ex=E, name="mix_norm_bwd0")
    grad_x = gx.reshape(E, S, D)

    dm_mine = jnp.concatenate([t.reshape(E, D) for t in
                               (dsh_a0, dsc_a0, dga0, dsh_m0, dsc_m0, dgm0, dsh_a1, dsc_a1, dga1, dsh_m1, dsc_m1, dgm1, dkv_sh, dkv_sc)], axis=1)
    dm_all = _all_gather8(dm_mine.reshape(8, -1), name="ag_dm").reshape(NB, 14 * D)
    sc_f32 = c_all * _sigmoid(c_all)
    g_ada_w = jax.ShapeDtypeStruct(ada_w.shape, F32)
    for l in range(2):
        g_ada_w = _mm(sc_f32, lax.dynamic_slice_in_dim(dm_all, l * 6 * D + chip * wa, wa, axis=1), "tn", name=f"ada_dw{l}", tn=256,
                      into=_Layer(g_ada_w, l))
    g_kv_ada_w = _mm(sc_f32, lax.dynamic_slice_in_dim(dm_all, 12 * D + chip * wk, wk, axis=1), "tn", name="ada_kv_dw", tn=256)
    db_all = _colsum(dm_all, name="ada_db")
    g_ada_b = db_all[0, :12 * D].reshape(2, 6 * D)
    g_kv_ada_b = db_all[0, 12 * D:]

    dWb_re, dWb_im, dC_re, dC_im = _s5_unblock(dWb, dWc)
    small_parts = [dg_mix0.sum(0), dg_mix1.sum(0), dg_mlp0.sum(0), dg_mlp1.sum(0), dg_kv.sum(0),
                   dqg.sum((0, 1, 2)).reshape(2, HEAD_DIM).sum(0), dkg.sum((0, 1, 2)).reshape(2, HEAD_DIM).sum(0),
                   dd[:, 0, :], dab[:, 0, :], dab[:, 1, :], dWb_re, dWb_im, dC_re, dC_im]
    spack, spans = _pack_rows(small_parts, row_multiple=32)
    chip_half = _sibling_sum_half(spack, name="small_d2d")
    ssum = _sum_blocks(_all_gather8(chip_half, name="ag_small"), 4, name="sum_small")
    (g_mix0, g_mix1, g_mlp0, g_mlp1, g_kvn, g_qn, g_kn, g_d, g_abr, g_abi, g_bbr, g_bbi, g_cre, g_cim) = _unpack_rows(ssum, spans)
    _, disc_vjp = jax.vjp(_s5_disc, s5_a_re[0], s5_a_im[0], s5_log_dt[0], s5_b_re[0], s5_b_im[0])
    g_are, g_aim, g_ldt, g_bre, g_bim = disc_vjp((g_abr.reshape(ab_re.shape), g_abi.reshape(ab_im.shape), g_bbr, g_bbi))
    g_s5d = lax.dynamic_slice_in_dim(g_d.reshape(1, D), chip * s5_d.shape[1], s5_d.shape[1], axis=1)

    def upd_big(w, m, v, roff, cb, name):
        shape = w.shape
        W = shape[-1]
        d_, m_, v_, g_ = _adamw2d(w.reshape(-1, W), gsh, m.reshape(-1, W), v.reshape(-1, W), name=name, g_roff=roff, g_cb=cb)
        return [t.reshape(shape) for t in (g_, d_, m_, v_)]

    def upd_own(w, g, m, v, name):
        shape = w.shape
        W = shape[-1]
        d_, m_, v_, g_ = _adamw2d(w.reshape(-1, W), g.reshape(-1, W), m.reshape(-1, W), v.reshape(-1, W), name=name)
        return [t.reshape(shape) for t in (g_, d_, m_, v_)]

    res = {}
    res["ada_w"] = upd_own(ada_w, g_ada_w, m_ada_w, v_ada_w, "adam_ada_w")
    res["kv_ada_w"] = upd_own(kv_ada_w, g_kv_ada_w, m_kv_ada_w, v_kv_ada_w, "adam_kv_ada_w")
    res["mlp_w1"] = upd_big(mlp_w1, m_mlp_w1, v_mlp_w1, 0, 0, "adam_w1")
    res["mlp_w2"] = upd_big(mlp_w2, m_mlp_w2, v_mlp_w2, 2 * D, 0, "adam_w2")
    res["s5_w_glu"] = upd_big(s5_w_glu, m_s5_w_glu, v_s5_w_glu, 4 * D, 0, "adam_glu")
    res["w_kv"] = upd_big(w_kv, m_w_kv, v_w_kv, 4 * D, 1, "adam_wkv")
    res["sb_w_q"] = upd_big(sb_w_q, m_sb_w_q, v_sb_w_q, 5 * D, 0, "adam_wq")
    res["sb_w_o"] = upd_big(sb_w_o, m_sb_w_o, v_sb_w_o, 5 * D + D // 4, 0, "adam_wo")

    small = {
        "ada_b": (ada_b, g_ada_b, m_ada_b, v_ada_b),
        "mix_norm_g": (mix_norm_g, jnp.stack([g_mix0, g_mix1]), m_mix_norm_g, v_mix_norm_g),
        "mlp_norm_g": (mlp_norm_g, jnp.stack([g_mlp0, g_mlp1]), m_mlp_norm_g, v_mlp_norm_g),
        "s5_a_re": (s5_a_re, g_are[None], m_s5_a_re, v_s5_a_re),
        "s5_a_im": (s5_a_im, g_aim[None], m_s5_a_im, v_s5_a_im),
        "s5_log_dt": (s5_log_dt, g_ldt[None], m_s5_log_dt, v_s5_log_dt),
        "s5_b_re": (s5_b_re, g_bre[None], m_s5_b_re, v_s5_b_re),
        "s5_b_im": (s5_b_im, g_bim[None], m_s5_b_im, v_s5_b_im),
        "s5_c_re": (s5_c_re, g_cre[None], m_s5_c_re, v_s5_c_re),
        "s5_c_im": (s5_c_im, g_cim[None], m_s5_c_im, v_s5_c_im),
        "s5_d": (s5_d, g_s5d, m_s5_d, v_s5_d),
        "kv_ada_b": (kv_ada_b, g_kv_ada_b, m_kv_ada_b, v_kv_ada_b),
        "kv_norm_g": (kv_norm_g, g_kvn, m_kv_norm_g, v_kv_norm_g),
        "k_norm_g": (k_norm_g, g_kn, m_k_norm_g, v_k_norm_g),
        "q_norm_g": (q_norm_g, g_qn.reshape(q_norm_g.shape), m_q_norm_g, v_q_norm_g),
    }
    names = list(small)
    packs = [_pack_rows([small[n][i].reshape(small[n][0].shape) for n in names]) for i in range(4)]
    sp = packs[0][1]
    d_, m_, v_, g_ = _adamw2d(packs[0][0], packs[1][0], packs[2][0], packs[3][0], name="adam_small")
    for n, gg, dd_, mm_, vv_ in zip(names, _unpack_rows(g_, sp), _unpack_rows(d_, sp), _unpack_rows(m_, sp), _unpack_rows(v_, sp)):
        res[n] = [gg, dd_, mm_, vv_]

    order = ["ada_w", "ada_b", "mix_norm_g", "mlp_norm_g", "mlp_w1", "mlp_w2", "s5_a_re", "s5_a_im", "s5_log_dt", "s5_b_re", "s5_b_im",
             "s5_c_re", "s5_c_im", "s5_d", "s5_w_glu", "kv_ada_w", "kv_ada_b", "kv_norm_g", "w_kv", "k_norm_g", "sb_w_q", "q_norm_g", "sb_w_o"]
    return (loss, grad_x, *[res[n][0] for n in order], *[res[n][1] for n in order], *[res[n][2] for n in order], *[res[n][3] for n in order])
```

```python
import functools
import math

import jax
import jax.numpy as jnp
from jax import lax
from jax.experimental import pallas as pl
from jax.experimental.pallas import tpu as pltpu

F32 = jnp.float32
BF16 = jnp.bfloat16
EPS = 1e-6
HEAD_DIM = 64
S5_GROUP = 16
S5_STATE = 64
GROUPS_PER_STEP = 8
U_LANES = GROUPS_PER_STEP * S5_GROUP
ST_LANES = GROUPS_PER_STEP * S5_STATE
SCAN_LANES = 256
SCAN_UNROLL = 4
VMEM_LIMIT = 56 * 1024 * 1024
ADAM_LR, ADAM_B1, ADAM_B2, ADAM_EPS, ADAM_WD, ADAM_STEP = 0.001, 0.9, 0.999, 1e-08, 0.01, 10
MESH = pl.DeviceIdType.MESH


def _cp(sem):
    return pltpu.CompilerParams(dimension_semantics=sem, vmem_limit_bytes=VMEM_LIMIT)


class _Sharded:
    def __init__(self, buf, kind, roff, nr, c0, nc):
        self.buf, self.kind, self.roff, self.nr, self.c0, self.nc = buf, kind, roff, nr, c0, nc
        self.shape = (nr, 4 * nc) if kind == "cols" else (4 * nr, nc)

    def operand(self, dims, tn, tk):
        roff, nr, c0, nc = self.roff, self.nr, self.c0, self.nc
        if self.kind == "cols" and dims == "nn":
            tk = min(tk, nr)
            assert roff % tk == 0
            return nc, tk, (None, tk, nc), lambda i, j, k: (j, roff // tk + k, c0 // nc)
        if self.kind == "cols":
            tn = min(tn, nr)
            assert roff % tn == 0
            return tn, nc, (None, tn, nc), lambda i, j, k: (k, roff // tn + j, c0 // nc)
        if dims == "nn":
            tn = min(tn, nc)
            assert roff % nr == 0 and c0 % tn == 0
            return tn, nr, (None, nr, tn), lambda i, j, k: (k, roff // nr, c0 // tn + j)
        tk = min(tk, nc)
        assert roff % nr == 0 and c0 % tk == 0
        return nr, tk, (None, nr, tk), lambda i, j, k: (j, roff // nr, c0 // tk + k)

    def result(self, tm, tn):
        roff, nr, c0, nc = self.roff, self.nr, self.c0, self.nc
        if self.kind == "cols":
            tm = min(tm, nr)
            assert roff % tm == 0
            return tm, nc, (None, tm, nc), lambda i, j, k: (j, roff // tm + i, c0 // nc)
        tm, tn = min(tm, nr), min(tn, nc)
        assert roff % tm == 0 and c0 % tn == 0
        per = nr // tm
        return tm, tn, (None, tm, tn), lambda i, j, k: (i // per, roff // tm + i % per, c0 // tn + j)


class _Layer:
    def __init__(self, buf, layer):
        self.buf, self.layer, self.shape = buf, layer, tuple(buf.shape[1:])

    def operand(self, dims, tn, tk):
        assert dims == "nn"
        layer = self.layer
        return tn, tk, (None, tk, tn), lambda i, j, k: (layer, k, j)

    def result(self, tm, tn):
        layer = self.layer
        return tm, tn, (None, tm, tn), lambda i, j, k: (layer, i, j)


def _mm(a, b, dims, *, name, out_dtypes=(F32,), epilogue=None, extras=(), tm=512, tn=1024, tk=1024, into=None):
    bshape = b.shape
    if dims == "nn":
        (M, K), (_, N) = a.shape, bshape
    elif dims == "nt":
        (M, K), (N, _) = a.shape, bshape
    else:
        (K, M), (_, N) = a.shape, bshape
    tm, tn, tk = min(tm, M), min(tn, N), min(tk, K)
    b_arr = b
    if into is not None:
        assert (M, N) == into.shape and len(out_dtypes) == 1 and not isinstance(b, _Sharded)
        tm, tn, o_blk, o_map = into.result(tm, tn)
        out_specs, out_shape = [pl.BlockSpec(o_blk, o_map)], [jax.ShapeDtypeStruct(into.buf.shape, into.buf.dtype)]
    if isinstance(b, (_Sharded, _Layer)):
        tn, tk, b_blk, b_map = b.operand(dims, tn, tk)
        b_spec, b_arr = pl.BlockSpec(b_blk, b_map), b.buf
    else:
        b_spec = pl.BlockSpec((tn, tk), lambda i, j, k: (j, k)) if dims == "nt" else pl.BlockSpec((tk, tn), lambda i, j, k: (k, j))
    if into is None:
        out_specs = [pl.BlockSpec((tm, tn), lambda i, j, k: (i, j)) for _ in out_dtypes]
        out_shape = [jax.ShapeDtypeStruct((M, N), d) for d in out_dtypes]
    assert M % tm == 0 and N % tn == 0 and K % tk == 0, (M, N, K, tm, tn, tk)
    nk = K // tk
    extras = [e(tm, tn) for e in extras]
    a_spec = pl.BlockSpec((tk, tm), lambda i, j, k: (k, i)) if dims == "tn" else pl.BlockSpec((tm, tk), lambda i, j, k: (i, k))
    contract = {"nn": ((1,), (0,)), "nt": ((1,), (1,)), "tn": ((0,), (0,))}[dims]
    n_ex, n_out = len(extras), len(out_dtypes)
    chain = [into.buf] if into is not None and not isinstance(into.buf, jax.ShapeDtypeStruct) else []
    n_in = n_ex + len(chain)

    def finish(r, ex, outs):
        res = epilogue(r, *[e[...] for e in ex]) if epilogue is not None else (r,)
        for o, v in zip(outs, res):
            o[...] = v.astype(o.dtype)

    def product(a_ref, b_ref):
        return lax.dot_general(a_ref[...].astype(BF16), b_ref[...].astype(BF16), (contract, ((), ())), preferred_element_type=F32)

    def body_one(a_ref, b_ref, *rest):
        finish(product(a_ref, b_ref), rest[:n_ex], rest[n_in:])

    def body_acc(a_ref, b_ref, *rest):
        ex, outs, acc = rest[:n_ex], rest[n_in:n_in + n_out], rest[-1]
        k = pl.program_id(2)

        @pl.when(k == 0)
        def _():
            acc[...] = product(a_ref, b_ref)

        @pl.when(jnp.logical_and(k > 0, k < nk - 1))
        def _():
            acc[...] += product(a_ref, b_ref)

        @pl.when(k == nk - 1)
        def _():
            finish(acc[...] + product(a_ref, b_ref), ex, outs)

    out = pl.pallas_call(
        body_one if nk == 1 else body_acc, name=name, grid=(M // tm, N // tn, nk),
        in_specs=[a_spec, b_spec] + [pl.BlockSpec(blk, im) for (_, blk, im) in extras] + [ANY for _ in chain],
        out_specs=out_specs, out_shape=out_shape,
        input_output_aliases={2 + n_ex: 0} if chain else {},
        scratch_shapes=[] if nk == 1 else [pltpu.VMEM((tm, tn), F32)],
        compiler_params=_cp(("parallel", "parallel", "arbitrary")),
    )(a, b_arr, *[e[0] for e in extras], *chain)
    return out if n_out > 1 else out[0]


def _mn_extra(arr):
    return lambda tm, tn: (arr, (tm, tn), lambda i, j, k: (i, j))


def _vec_extra(vec, S):
    return lambda tm, tn: (vec, (None, 1, tn), lambda i, j, k: ((i * tm) // S, 0, j))


def _rowwise(fn, rows, vecs=(), consts=(), out_rows=(), out_sums=(), *, n_ex, name, tr=512):
    rows = [r if len(r) == 4 else (*r, 0) for r in rows]
    S = min(r[0].shape[0] for r in rows if r[3] == 0) // n_ex
    tr = math.gcd(tr, S)
    assert S % tr == 0
    nb = S // tr
    in_specs = []
    for (arr, w, cb, roff) in rows:
        assert roff % tr == 0
        in_specs.append(pl.BlockSpec((tr, w), functools.partial(lambda e, i, cb, ro: (e * nb + i + ro, cb), cb=cb, ro=roff // tr)))
    for v in vecs:
        in_specs.append(pl.BlockSpec((None, 1, v.shape[-1]), lambda e, i: (e, 0, 0)))
    for c in consts:
        in_specs.append(pl.BlockSpec((1, c.shape[-1]), lambda e, i: (0, 0)))
    n_in, n_or, n_os = len(in_specs), len(out_rows), len(out_sums)
    flipped = [len(o) == 3 and o[2] for o in out_rows]
    out_specs = [pl.BlockSpec((o[0], tr), lambda e, i: (0, e * nb + i)) if f else pl.BlockSpec((tr, o[0]), lambda e, i: (e * nb + i, 0))
                 for o, f in zip(out_rows, flipped)]
    out_specs += [pl.BlockSpec((None, 1, w), lambda e, i: (e, 0, 0)) for w in out_sums]
    out_shape = [jax.ShapeDtypeStruct((o[0], n_ex * S) if f else (n_ex * S, o[0]), o[1]) for o, f in zip(out_rows, flipped)]
    out_shape += [jax.ShapeDtypeStruct((n_ex, 1, w), F32) for w in out_sums]

    def body(*refs):
        ins, o_r, o_s = refs[:n_in], refs[n_in:n_in + n_or], refs[n_in + n_or:]
        ro, so = fn(*[r[...] for r in ins])
        for o, v, f in zip(o_r, ro, flipped):
            o[...] = (v.T if f else v).astype(o.dtype)
        i = pl.program_id(1)
        for o, v in zip(o_s, so):
            @pl.when(i == 0)
            def _(o=o, v=v):
                o[...] = v

            @pl.when(i > 0)
            def _(o=o, v=v):
                o[...] += v

    outs = pl.pallas_call(
        body, name=name, grid=(n_ex, nb), in_specs=in_specs, out_specs=out_specs, out_shape=out_shape,
        compiler_params=_cp(("parallel", "arbitrary")),
    )(*[r[0] for r in rows], *vecs, *consts)
    return outs[:n_or], outs[n_or:]


def _csum(x):
    return jnp.sum(x, axis=0, keepdims=True)


def _norm_mod_fwd(x, g, sh, sc, *, n_ex, out_dtype, name, with_transpose=False):
    def fn(xt, sht, sct, gt):
        r = lax.rsqrt(jnp.mean(xt * xt, axis=-1, keepdims=True) + EPS)
        h = (xt * r * gt) * (1.0 + sct) + sht
        return [h, h] if with_transpose else [h], []
    D = x.shape[1]
    outs = [(D, out_dtype), (D, out_dtype, True)] if with_transpose else [(D, out_dtype)]
    res = _rowwise(fn, [(x, D, 0)], [sh, sc], [g], outs, [], n_ex=n_ex, name=name)[0]
    return res if with_transpose else res[0]


def _norm_mod_fwd_pair(x, first, second, *, n_ex, name):
    def fn(xt, sh1, sc1, sh2, sc2, g1, g2):
        n = xt * lax.rsqrt(jnp.mean(xt * xt, axis=-1, keepdims=True) + EPS)
        h1 = (n * g1) * (1.0 + sc1) + sh1
        h2 = (n * g2) * (1.0 + sc2) + sh2
        return [h1, h1, h2, h2], []
    D = x.shape[1]
    outs = [(D, BF16), (D, BF16, True)] * 2
    return _rowwise(fn, [(x, D, 0)], [first[1], first[2], second[1], second[2]], [first[0], second[0]], outs, [], n_ex=n_ex, name=name)[0]


def _gated(dx, branch, gate):
    return (gate * dx).astype(BF16), _csum(dx * branch)


def _norm_mod_bwd(x, dh, dres, g, sc, *, n_ex, name, gated=None):
    n_rows = len(gated[0]) if gated is not None else 0

    def fn(xt, dht, drt, *rest):
        sct, gt = rest[-2], rest[-1]
        dht = dht.astype(F32)
        r = lax.rsqrt(jnp.mean(xt * xt, axis=-1, keepdims=True) + EPS)
        n = xt * r
        y = n * gt
        dy = dht * (1.0 + sct)
        dn = dy * gt
        dx = drt + r * (dn - n * jnp.mean(dn * n, axis=-1, keepdims=True))
        rows, sums = [dx], [_csum(dht), _csum(dht * y), _csum(dy * n)]
        if gated is not None:
            dbranch, dgate = gated[2](dx, *rest[:n_rows + 1])
            rows, sums = rows + [dbranch], sums + [dgate]
        return rows, sums
    D = x.shape[1]
    extra_rows, extra_vecs = (list(gated[0]), [gated[1]]) if gated is not None else ([], [])
    return _rowwise(fn, [(x, D, 0), (dh, D, 0), (dres, D, 0)] + extra_rows, extra_vecs + [sc], [g],
                    [(D, F32)] + ([(gated[3], BF16)] if gated is not None else []), [D, D, D] + ([D] if gated is not None else []),
                    n_ex=n_ex, name=name)


def _norm_mod_bwd_pair(x, dh1, dh2, dres, first, second, gated, *, n_ex, name):
    n_rows = len(gated[0])

    def fn(xt, d1, d2, drt, *rest):
        sc1, sc2, g1, g2 = rest[-4:]
        r = lax.rsqrt(jnp.mean(xt * xt, axis=-1, keepdims=True) + EPS)
        n = xt * r
        dx, sums = drt, []
        for dht, sct, gt in ((d1.astype(F32), sc1, g1), (d2.astype(F32), sc2, g2)):
            dy = dht * (1.0 + sct)
            dn = dy * gt
            dx = dx + r * (dn - n * jnp.mean(dn * n, axis=-1, keepdims=True))
            sums += [_csum(dht), _csum(dht * (n * gt)), _csum(dy * n)]
        dbranch, dgate = gated[2](dx, *rest[:n_rows + 1])
        return [dx, dbranch], sums + [dgate]
    D = x.shape[1]
    return _rowwise(fn, [(x, D, 0), (dh1, D, 0), (dh2, D, 0), (dres, D, 0)] + list(gated[0]), [gated[1], first[1], second[1]],
                    [first[0], second[0]], [(D, F32), (gated[3], BF16)], [D] * 7, n_ex=n_ex, name=name)


def _sigmoid(x):
    return 1.0 / (1.0 + jnp.exp(-x))


def _gelu(y):
    return 0.5 * y * (1.0 + jnp.tanh(0.7978845608028654 * (y + 0.044715 * y * y * y)))


def _gelu_grad(y):
    t = jnp.tanh(0.7978845608028654 * (y + 0.044715 * y * y * y))
    return 0.5 * (1.0 + t) + 0.5 * y * (1.0 - t * t) * 0.7978845608028654 * (1.0 + 3 * 0.044715 * y * y)


def _adamw_fn(w, g, m, v):
    m2 = ADAM_B1 * m + (1.0 - ADAM_B1) * g
    v2 = ADAM_B2 * v + (1.0 - ADAM_B2) * (g * g)
    m_hat = m2 / (1.0 - ADAM_B1 ** ADAM_STEP)
    v_hat = v2 / (1.0 - ADAM_B2 ** ADAM_STEP)
    delta = -ADAM_LR * (m_hat / (jnp.sqrt(v_hat) + ADAM_EPS) + ADAM_WD * w)
    return delta, m2, v2


def _adamw2d(w, g, m, v, *, name, g_roff=0, g_cb=0):
    R, W = w.shape

    def fn(wt, gt, mt, vt):
        d, m2, v2 = _adamw_fn(wt, gt, mt, vt)
        return [d, m2, v2, gt], []
    return _rowwise(fn, [(w, W, 0), (g, W, g_cb, g_roff), (m, W, 0), (v, W, 0)], [], [],
                    [(W, F32)] * 4, [], n_ex=1, name=name, tr=256)[0]


def _scan_tiles(re_ref, im_ref, cf, lane0, n_chunks, reverse, extra=None):
    L = SCAN_LANES
    lanes = pl.ds(lane0, L)
    A = [cf[i, :, lanes] for i in range(8)]
    shifts = (7, 6, 4) if reverse else (1, 2, 4)
    edge = 0 if reverse else 7

    U = SCAN_UNROLL
    n_groups = n_chunks // U

    def body(c, carry):
        first = ((n_groups - 1 - c) if reverse else c) * U
        rows = pl.ds(pl.multiple_of(first * 8, 8 * U), 8 * U)
        big_r, big_i = re_ref[rows, lanes], im_ref[rows, lanes]
        tiles = []
        for u in range(U):
            xr, xi = big_r[8 * u:8 * u + 8, :], big_i[8 * u:8 * u + 8, :]
            for idx, sft in enumerate(shifts):
                ar, ai = A[2 * idx], A[2 * idx + 1]
                rr, ri = pltpu.roll(xr, sft, 0), pltpu.roll(xi, sft, 0)
                xr, xi = xr + ar * rr - ai * ri, xi + ar * ri + ai * rr
            tiles.append((xr, xi))
        pr, pi = A[6], A[7]
        cr, ci = carry[0], carry[1]
        for u in (range(U - 1, -1, -1) if reverse else range(U)):
            xr, xi = tiles[u]
            xr, xi = xr + pr * cr - pi * ci, xi + pr * ci + pi * cr
            tiles[u] = (xr, xi)
            cr, ci = jnp.broadcast_to(xr[edge:edge + 1, :], (8, L)), jnp.broadcast_to(xi[edge:edge + 1, :], (8, L))
        re_ref[rows, lanes] = jnp.concatenate([t[0] for t in tiles], axis=0)
        im_ref[rows, lanes] = jnp.concatenate([t[1] for t in tiles], axis=0)
        return (cr, ci) if extra is None else (cr, ci) + extra(first, tiles, carry[2:])

    assert n_chunks % U == 0
    z = jnp.zeros((8, L), F32)
    init = (z, z) if extra is None else (z, z, z, z)
    return lax.fori_loop(0, n_groups, body, init)


def _s5_consts(ab_re, ab_im):
    ng = ab_re.shape[0] // GROUPS_PER_STEP
    ar, ai = ab_re.reshape(ng, 1, ST_LANES), ab_im.reshape(ng, 1, ST_LANES)

    def cmul(xr, xi, yr, yi):
        return xr * yr - xi * yi, xr * yi + xi * yr

    def build(ar, ai, reverse):
        pw = [(ar, ai)]
        for _ in range(7):
            pw.append(cmul(*pw[-1], ar, ai))
        row = jnp.arange(8).reshape(1, 8, 1)
        tiles = []
        for k in (1, 2, 4):
            keep = (row <= 7 - k) if reverse else (row >= k)
            tiles += [jnp.where(keep, pw[k - 1][0], 0.0), jnp.where(keep, pw[k - 1][1], 0.0)]
        order = [7 - r for r in range(8)] if reverse else list(range(8))
        tiles += [jnp.concatenate([pw[o][0] for o in order], axis=1), jnp.concatenate([pw[o][1] for o in order], axis=1)]
        return jnp.stack([jnp.broadcast_to(t, (ng, 8, ST_LANES)) for t in tiles], axis=1)

    return build(ar, ai, False), build(ar, -ai, True)


def _s5_blockdiag(bb_re, bb_im, c_re, c_im):
    G = bb_re.shape[0]
    ng = G // GROUPS_PER_STEP
    eye = jnp.eye(GROUPS_PER_STEP, dtype=F32)

    def wb(bb):
        return jnp.einsum("bgph,gk->bghkp", bb.reshape(ng, GROUPS_PER_STEP, S5_STATE, S5_GROUP), eye).reshape(ng, U_LANES, ST_LANES)

    def wc(cc):
        return jnp.einsum("bghp,gk->bkpgh", cc.reshape(ng, GROUPS_PER_STEP, S5_GROUP, S5_STATE), eye).reshape(ng, ST_LANES, U_LANES)

    Wb = jnp.concatenate([wb(bb_re), wb(bb_im)], axis=2).astype(BF16)
    Wc = jnp.concatenate([wc(c_re), -wc(c_im)], axis=1).astype(BF16)
    return Wb, Wc


def _s5_unblock(dWb, dWc):
    ng = dWb.shape[0]
    eye = jnp.eye(GROUPS_PER_STEP, dtype=F32)

    def ub(w):
        return jnp.einsum("bghkp,gk->bgph", w.reshape(ng, GROUPS_PER_STEP, S5_GROUP, GROUPS_PER_STEP, S5_STATE), eye).reshape(-1, S5_STATE, S5_GROUP)

    def uc(w):
        return jnp.einsum("bkpgh,gk->bghp", w.reshape(ng, GROUPS_PER_STEP, S5_STATE, GROUPS_PER_STEP, S5_GROUP), eye).reshape(-1, S5_GROUP, S5_STATE)

    return ub(dWb[:, :, :ST_LANES]), ub(dWb[:, :, ST_LANES:]), uc(dWc[:, :ST_LANES, :]), -uc(dWc[:, ST_LANES:, :])


def _s5_disc(a_re, a_im, log_dt, b_re, b_im):
    dt = jnp.exp(log_dt)[:, None]
    mag = jnp.exp(a_re * dt)
    ab_re = mag * jnp.cos(a_im * dt)
    ab_im = mag * jnp.sin(a_im * dt)
    den = a_re * a_re + a_im * a_im
    nr, ni = ab_re - 1, ab_im
    f_re = (nr * a_re + ni * a_im) / den
    f_im = (ni * a_re - nr * a_im) / den
    bb_re = f_re[..., None] * b_re - f_im[..., None] * b_im
    bb_im = f_re[..., None] * b_im + f_im[..., None] * b_re
    return ab_re, ab_im, bb_re, bb_im


ROW_CHUNK = 512


def _s5_fwd(u, Wb, Wc, cf, d, xsrc, *, n_ex, name):
    T, D = u.shape
    S = T // n_ex
    ng = D // U_LANES
    rc = min(ROW_CHUNK, S)

    def body(u_ref, wb_ref, wc_ref, cf_ref, d_ref, xsrc_ref, y_ref, gy_ref, gyt_ref, st_ref, xout_ref, re_s, im_s, *sems):
        step = pl.program_id(0) * ng + pl.program_id(1)
        exch = _ChipExchange(xsrc_ref, xout_ref, *sems, scatter=False)

        @pl.when(step == 0)
        def _():
            exch.start()

        for r in range(S // rc):
            rows = pl.ds(r * rc, rc)
            bu = jnp.dot(u_ref[rows, :].astype(BF16), wb_ref[...], preferred_element_type=F32)
            re_s[rows, :] = bu[:, :ST_LANES]
            im_s[rows, :] = bu[:, ST_LANES:]
        for l0 in range(0, ST_LANES, SCAN_LANES):
            _scan_tiles(re_s, im_s, cf_ref, l0, S // 8, False)
        for r in range(S // rc):
            rows = pl.ds(r * rc, rc)
            st = jnp.concatenate([re_s[rows, :], im_s[rows, :]], axis=1).astype(BF16)
            st_ref[rows, :] = st
            y = jnp.dot(st, wc_ref[...], preferred_element_type=F32) + d_ref[...] * u_ref[rows, :]
            y_ref[rows, :] = y
            gy = _gelu(y)
            gy_ref[rows, :] = gy.astype(BF16)
            gyt_ref[:, rows] = gy.T.astype(BF16)

        @pl.when(step == n_ex * ng - 1)
        def _():
            exch.wait()

    return pl.pallas_call(
        body, name=name, grid=(n_ex, ng),
        in_specs=[pl.BlockSpec((S, U_LANES), lambda e, g: (e, g)),
                  pl.BlockSpec((None, U_LANES, 2 * ST_LANES), lambda e, g: (g, 0, 0)),
                  pl.BlockSpec((None, 2 * ST_LANES, U_LANES), lambda e, g: (g, 0, 0)),
                  pl.BlockSpec((None, 8, 8, ST_LANES), lambda e, g: (g, 0, 0, 0)),
                  pl.BlockSpec((1, U_LANES), lambda e, g: (0, g)), ANY],
        out_specs=[pl.BlockSpec((S, U_LANES), lambda e, g: (e, g))] * 2 + [pl.BlockSpec((U_LANES, S), lambda e, g: (g, e)),
                   pl.BlockSpec((S, 2 * ST_LANES), lambda e, g: (e, g)), ANY],
        out_shape=[jax.ShapeDtypeStruct((T, D), F32), jax.ShapeDtypeStruct((T, D), BF16), jax.ShapeDtypeStruct((D, T), BF16),
                   jax.ShapeDtypeStruct((T, ng * 2 * ST_LANES), BF16), _ChipExchange.out_shape(xsrc, False)],
        scratch_shapes=[pltpu.VMEM((S, ST_LANES), F32)] * 2 + _ChipExchange.SCRATCH,
        compiler_params=_cp(("arbitrary", "arbitrary")),
    )(u, Wb, Wc, cf, d, xsrc)


def _s5_bwd(u, y, dgy, st, Wb, Wc, cr, d, xsrc, *, n_ex, name):
    T, D = u.shape
    S = T // n_ex
    ng = D // U_LANES
    rc = min(ROW_CHUNK, S)
    nch = S // 8
    grp = 8 * SCAN_UNROLL
    assert grp % 16 == 0

    def body(u_ref, y_ref, dgy_ref, st_ref, wb_ref, wc_ref, cr_ref, d_ref, xsrc_ref,
             du_ref, dwb_ref, dwc_ref, dab_ref, dd_ref, xout_ref, gr_s, gi_s, dy_s, *sems):
        e = pl.program_id(1)
        step = pl.program_id(0) * n_ex + e
        exch = _ChipExchange(xsrc_ref, xout_ref, *sems, scatter=True)

        @pl.when(step == 0)
        def _():
            exch.start()

        @pl.when(e == 0)
        def _():
            dwb_ref[...] = jnp.zeros_like(dwb_ref)
            dwc_ref[...] = jnp.zeros_like(dwc_ref)
            dab_ref[...] = jnp.zeros_like(dab_ref)
            dd_ref[...] = jnp.zeros_like(dd_ref)

        dd = jnp.zeros((1, U_LANES), F32)
        for r in range(S // rc):
            rows = pl.ds(r * rc, rc)
            ut = u_ref[rows, :]
            dy = dgy_ref[rows, :].astype(F32) * _gelu_grad(y_ref[rows, :])
            dy_s[rows, :] = dy
            dd = dd + _csum(dy * ut)
            go = lax.dot_general(dy.astype(BF16), wc_ref[...], (((1,), (1,)), ((), ())), preferred_element_type=F32)
            gr_s[rows, :] = go[:, :ST_LANES]
            gi_s[rows, :] = go[:, ST_LANES:]
        dd_ref[0:1, :] += dd
        row0 = lax.broadcasted_iota(jnp.int32, (8, SCAN_LANES), 0) == 0
        for l0 in range(0, ST_LANES, SCAN_LANES):
            lanes = pl.ds(l0, SCAN_LANES)

            def dab_group(first, tiles, acc, l0=l0):
                def states(r0, n, lane0):
                    return st_ref[pl.ds(pl.multiple_of(r0, 16), n), pl.ds(lane0, SCAN_LANES)].astype(F32)
                r0 = first * 8
                cur = states(r0, grp, l0), states(r0, grp, ST_LANES + l0)
                live = (first > 0).astype(F32)
                p0 = jnp.maximum(r0 - 16, 0)
                before = [states(p0, 16, l0)[8:16, :] * live, states(p0, 16, ST_LANES + l0)[8:16, :] * live]
                a_re, a_im = acc
                for t, (gr, gi) in enumerate(tiles):
                    here = [c[8 * t:8 * t + 8, :] for c in cur]
                    sr, si = [jnp.where(row0, pltpu.roll(b, 1, 0), pltpu.roll(h, 1, 0)) for b, h in zip(before, here)]
                    a_re, a_im = a_re + gr * sr + gi * si, a_im + gi * sr - gr * si
                    before = here
                return a_re, a_im

            res = _scan_tiles(gr_s, gi_s, cr_ref, l0, nch, True, extra=dab_group)
            dab_ref[0:1, lanes] += _csum(res[2])
            dab_ref[1:2, lanes] += _csum(res[3])
        for r in range(S // rc):
            rows = pl.ds(r * rc, rc)
            st = st_ref[rows, :]
            g = jnp.concatenate([gr_s[rows, :], gi_s[rows, :]], axis=1).astype(BF16)
            dyb = dy_s[rows, :].astype(BF16)
            dwc_ref[...] += lax.dot_general(st, dyb, (((0,), (0,)), ((), ())), preferred_element_type=F32)
            dwb_ref[...] += lax.dot_general(u_ref[rows, :].astype(BF16), g, (((0,), (0,)), ((), ())), preferred_element_type=F32)
            du = lax.dot_general(g, wb_ref[...], (((1,), (1,)), ((), ())), preferred_element_type=F32)
            du_ref[rows, :] = du + d_ref[...] * dy_s[rows, :]

        @pl.when(step == ng * n_ex - 1)
        def _():
            exch.wait()

    return pl.pallas_call(
        body, name=name, grid=(ng, n_ex),
        in_specs=[pl.BlockSpec((S, U_LANES), lambda g, e: (e, g))] * 3 + [
            pl.BlockSpec((S, 2 * ST_LANES), lambda g, e: (e, g)),
            pl.BlockSpec((None, U_LANES, 2 * ST_LANES), lambda g, e: (g, 0, 0)),
            pl.BlockSpec((None, 2 * ST_LANES, U_LANES), lambda g, e: (g, 0, 0)),
            pl.BlockSpec((None, 8, 8, ST_LANES), lambda g, e: (g, 0, 0, 0)),
            pl.BlockSpec((1, U_LANES), lambda g, e: (0, g)), ANY],
        out_specs=[pl.BlockSpec((S, U_LANES), lambda g, e: (e, g)),
                   pl.BlockSpec((None, U_LANES, 2 * ST_LANES), lambda g, e: (g, 0, 0)),
                   pl.BlockSpec((None, 2 * ST_LANES, U_LANES), lambda g, e: (g, 0, 0)),
                   pl.BlockSpec((None, 8, ST_LANES), lambda g, e: (g, 0, 0)),
                   pl.BlockSpec((None, 8, U_LANES), lambda g, e: (g, 0, 0)), ANY],
        out_shape=[jax.ShapeDtypeStruct((T, D), F32),
                   jax.ShapeDtypeStruct((ng, U_LANES, 2 * ST_LANES), F32),
                   jax.ShapeDtypeStruct((ng, 2 * ST_LANES, U_LANES), F32),
                   jax.ShapeDtypeStruct((ng, 8, ST_LANES), F32),
                   jax.ShapeDtypeStruct((ng, 8, U_LANES), F32), _ChipExchange.out_shape(xsrc, True)],
        scratch_shapes=[pltpu.VMEM((S, ST_LANES), F32)] * 2 + [pltpu.VMEM((S, U_LANES), F32)] + _ChipExchange.SCRATCH,
        compiler_params=_cp(("arbitrary", "arbitrary")),
    )(u, y, dgy, st, Wb, Wc, cr, d, xsrc)


TQ = 256
KW = 512
SUB = 128


def _head_masks():
    lane = lax.broadcasted_iota(jnp.int32, (1, 2 * HEAD_DIM), 1)
    m0 = (lane < HEAD_DIM).astype(F32)
    return m0, 1.0 - m0


def _head_norm(x, g, m0, m1):
    sq = x * x
    r0 = lax.rsqrt(jnp.sum(sq * m0, axis=-1, keepdims=True) / HEAD_DIM + EPS)
    r1 = lax.rsqrt(jnp.sum(sq * m1, axis=-1, keepdims=True) / HEAD_DIM + EPS)
    r = m0 * r0 + m1 * r1
    return x * r, r


def _head_norm_bwd(dy, n, r, g, m0, m1):
    dn = dy * g
    p = dn * n
    mean = (m0 * jnp.sum(p * m0, axis=-1, keepdims=True) + m1 * jnp.sum(p * m1, axis=-1, keepdims=True)) / HEAD_DIM
    return r * (dn - n * mean), _csum(dy * n)


def _pair_matrix(kind):
    r = lax.broadcasted_iota(jnp.int32, (2 * SUB, 2 * SUB), 0)
    c = lax.broadcasted_iota(jnp.int32, (2 * SUB, 2 * SUB), 1)
    same = (r < SUB) == (c < SUB)
    rel = {"after": r > c, "upto": r <= c, "before": r < c}[kind]
    return jnp.logical_and(same, rel).astype(BF16)


def _block_sums(x, mat, carry, reverse, terms=2):
    hi = x.astype(BF16)
    lo = (x - hi.astype(F32)).astype(BF16) if terms == 2 else None
    npair = x.shape[1] // (2 * SUB)
    parts = [None] * (2 * npair)
    for p in (range(npair - 1, -1, -1) if reverse else range(npair)):
        sl = slice(2 * SUB * p, 2 * SUB * (p + 1))
        loc = jnp.dot(hi[:, sl], mat, preferred_element_type=F32)
        if terms == 2:
            loc = loc + jnp.dot(lo[:, sl], mat, preferred_element_type=F32)
        for b in ((1, 0) if reverse else (0, 1)):
            k = 2 * p + b
            parts[k] = loc[:, SUB * b:SUB * (b + 1)] + carry
            carry = carry + jnp.sum(x[:, SUB * k:SUB * (k + 1)], axis=-1, keepdims=True)
    return jnp.concatenate(parts, axis=1), carry


def _sb_logits(z, mask):
    lp = jnp.minimum(z, 0.0) - jnp.log(1.0 + jnp.exp(-jnp.abs(z)))
    lf = lp - z
    if mask is not None:
        lf = jnp.where(mask, lf, 0.0)
    return lp, lf


def _causal_mask(row0, col0, kw):
    r = row0 + lax.broadcasted_iota(jnp.int32, (TQ, kw), 0)
    c = col0 + lax.broadcasted_iota(jnp.int32, (TQ, kw), 1)
    return c < r


def _transposed_windows(x, ref):
    for w in range(x.shape[0] // KW):
        ref[w] = x[w * KW:(w + 1) * KW, :].T.astype(BF16)


def _attn_fwd(q, kv, qg, kg, xsrc, *, n_ex, name):
    T, D = q.shape
    S = T // n_ex
    nhp = D // (2 * HEAD_DIM)
    nq = S // TQ
    scale = 1.0 / math.sqrt(HEAD_DIM)

    def body(q_ref, k_ref, v_ref, qg_ref, kg_ref, xsrc_ref, o_ref, tot_ref, ot_ref, xout_ref, kT_s, qm_s, vm_s, *sems):
        step = pl.program_id(0) * nhp + pl.program_id(1)
        exch = _ChipExchange(xsrc_ref, xout_ref, *sems, scatter=False)

        @pl.when(step == 0)
        def _():
            exch.start()

        m0, m1 = _head_masks()
        qn, _ = _head_norm(q_ref[...], None, m0, m1)
        qn = qn * (qg_ref[...] * scale)
        kn, _ = _head_norm(k_ref[...], None, m0, m1)
        _transposed_windows(kn * kg_ref[...], kT_s)
        v = v_ref[...]
        for h, m in enumerate((m0, m1)):
            qm_s[h] = (qn * m).astype(BF16)
            vm_s[h] = (v * m).astype(BF16)
        u_after = _pair_matrix("after")

        def window(rows, win, st, mask, kw):
            keys = pl.ds(pl.multiple_of(win * KW, KW), kw)
            zs = [jnp.dot(qm_s[h, rows, :], kT_s[win, :, :kw], preferred_element_type=F32) for h in range(2)]
            lg = [_sb_logits(zs[h], mask) for h in range(2)]
            sums = [_block_sums(lg[h][1], u_after, st[2 * h], True) for h in range(2)]
            out = ()
            for h in range(2):
                w = jnp.exp(lg[h][0] + sums[h][0])
                if mask is not None:
                    w = jnp.where(mask, w, 0.0)
                out += (sums[h][1], st[2 * h + 1] + jnp.dot(w.astype(BF16), vm_s[h, keys, :], preferred_element_type=F32))
            return out

        def qtile(iq, last, kw):
            rows = pl.ds(pl.multiple_of(iq * TQ, TQ), TQ)
            mask = _causal_mask(iq * TQ, last * KW, kw)
            z1, zq = jnp.zeros((TQ, 1), F32), jnp.zeros((TQ, 2 * HEAD_DIM), F32)
            st = window(rows, last, (z1, zq, z1, zq), mask, kw)
            st = lax.fori_loop(0, last, lambda jj, st: window(rows, last - 1 - jj, st, None, KW), st)
            o_ref[rows, :] = st[1] + st[3]
            tot_ref[rows, :] = st[0] * m0 + st[2] * m1

        def qtiles_of_window(a, _):
            for sub in range(KW // TQ):
                qtile(a * (KW // TQ) + sub, a, (sub + 1) * TQ)
            return 0

        lax.fori_loop(0, S // KW, qtiles_of_window, 0)
        ot_ref[...] = o_ref[...].T.astype(BF16)

        @pl.when(step == n_ex * nhp - 1)
        def _():
            exch.wait()

    assert S % KW == 0 and KW % TQ == 0
    nwin = S // KW
    blk = (S, 2 * HEAD_DIM)
    return pl.pallas_call(
        body, name=name, grid=(n_ex, nhp),
        in_specs=[pl.BlockSpec(blk, lambda e, h: (e, h)), pl.BlockSpec(blk, lambda e, h: (e, h)),
                  pl.BlockSpec(blk, lambda e, h: (e, h + nhp)),
                  pl.BlockSpec((1, 2 * HEAD_DIM), lambda e, h: (0, 0)), pl.BlockSpec((1, 2 * HEAD_DIM), lambda e, h: (0, 0)), ANY],
        out_specs=[pl.BlockSpec(blk, lambda e, h: (e, h))] * 2 + [pl.BlockSpec((2 * HEAD_DIM, S), lambda e, h: (h, e)), ANY],
        out_shape=[jax.ShapeDtypeStruct((T, D), F32)] * 2 + [jax.ShapeDtypeStruct((D, T), BF16), _ChipExchange.out_shape(xsrc, False)],
        scratch_shapes=[pltpu.VMEM((nwin, 2 * HEAD_DIM, KW), BF16), pltpu.VMEM((2,) + blk, BF16), pltpu.VMEM((2,) + blk, BF16)]
        + _ChipExchange.SCRATCH,
        compiler_params=_cp(("arbitrary", "arbitrary")),
    )(q, kv, kv, qg, kg, xsrc)


def _attn_bwd(q, kv, tot, do, qg, kg, *, n_ex, name):
    T, D = q.shape
    S = T // n_ex
    nhp = D // (2 * HEAD_DIM)
    nq = S // TQ
    scale = 1.0 / math.sqrt(HEAD_DIM)

    def body(q_ref, k_ref, v_ref, tot_ref, do_ref, qg_ref, kg_ref, dq_ref, dk_ref, dv_ref, dqg_ref, dkg_ref,
             kT_s, vT_s, km_s, qm_s, dom_s, dqn_s, dkT_s, dvT_s):
        m0, m1 = _head_masks()
        qn, qr = _head_norm(q_ref[...], None, m0, m1)
        kn, kr = _head_norm(k_ref[...], None, m0, m1)
        qs = qn * (qg_ref[...] * scale)
        kk = kn * kg_ref[...]
        _transposed_windows(kk, kT_s)
        _transposed_windows(v_ref[...], vT_s)
        do = do_ref[...]
        for h, m in enumerate((m0, m1)):
            qm_s[h] = (qs * m).astype(BF16)
            km_s[h] = (kk * m).astype(BF16)
            dom_s[h] = (do * m).astype(BF16)
        dkT_s[...] = jnp.zeros_like(dkT_s)
        dvT_s[...] = jnp.zeros_like(dvT_s)
        u_upto, u_before = _pair_matrix("upto"), _pair_matrix("before")

        def both(inv, win, st, mask, kw):
            keys = pl.ds(pl.multiple_of(win * KW, KW), kw)
            lg = [_sb_logits(jnp.dot(inv[h][0], kT_s[win, :, :kw], preferred_element_type=F32), mask) for h in range(2)]
            s_lf = [_block_sums(lg[h][1], u_upto, st[3 * h], False) for h in range(2)]
            ws, ews = [], []
            for h in range(2):
                w = jnp.exp(lg[h][0] - s_lf[h][0])
                if mask is not None:
                    w = jnp.where(mask, w, 0.0)
                ws.append(w)
                ews.append(jnp.dot(inv[h][2], vT_s[win, :, :kw], preferred_element_type=F32) * w)
            s_e = [_block_sums(ews[h], u_before, st[3 * h + 1], False, terms=1) for h in range(2)]
            out, dk, dv = (), None, None
            for h in range(2):
                sig = jnp.exp(lg[h][0])
                dz = ews[h] - sig * (ews[h] + s_e[h][0])
                if mask is not None:
                    dz = jnp.where(mask, dz, 0.0)
                dzb = dz.astype(BF16)
                out += (s_lf[h][1], s_e[h][1], st[3 * h + 2] + jnp.dot(dzb, km_s[h, keys, :], preferred_element_type=F32))
                dkh = jnp.dot(inv[h][1], dzb, preferred_element_type=F32)
                dvh = jnp.dot(inv[h][3], ws[h].astype(BF16), preferred_element_type=F32)
                dk, dv = (dkh, dvh) if h == 0 else (dk + dkh, dv + dvh)
            dkT_s[win, :, :kw] += dk
            dvT_s[win, :, :kw] += dv
            return out

        def qtile(iq, last, kw):
            rows = pl.ds(pl.multiple_of(iq * TQ, TQ), TQ)
            mask = _causal_mask(iq * TQ, last * KW, kw)
            tt = tot_ref[rows, :]
            inv, neg_total = [], []
            for h, m in enumerate((m0, m1)):
                qh, doh = qm_s[h, rows, :], dom_s[h, rows, :]
                neg_total.append(jnp.sum(tt * m, axis=-1, keepdims=True) * (-1.0 / HEAD_DIM))
                inv.append((qh, qh.astype(F32).T.astype(BF16), doh, doh.astype(F32).T.astype(BF16)))

            z1, zq = jnp.zeros((TQ, 1), F32), jnp.zeros((TQ, 2 * HEAD_DIM), F32)
            st = lax.fori_loop(0, last, lambda win, st: both(inv, win, st, None, KW), (neg_total[0], z1, zq, neg_total[1], z1, zq))
            st = both(inv, last, st, mask, kw)
            dqn_s[rows, :] = st[2] + st[5]

        def qtiles_of_window(a, _):
            for sub in range(KW // TQ):
                qtile(a * (KW // TQ) + sub, a, (sub + 1) * TQ)
            return 0

        lax.fori_loop(0, S // KW, qtiles_of_window, 0)
        dkn = jnp.concatenate([dkT_s[w].T for w in range(nwin)], axis=0)
        dq, dqg = _head_norm_bwd(dqn_s[...] * scale, qn, qr, qg_ref[...], m0, m1)
        dk, dkg = _head_norm_bwd(dkn, kn, kr, kg_ref[...], m0, m1)
        dq_ref[...] = dq
        dk_ref[...] = dk
        dv_ref[...] = jnp.concatenate([dvT_s[w].T for w in range(nwin)], axis=0)
        dqg_ref[...] = dqg
        dkg_ref[...] = dkg

    assert S % KW == 0 and KW % TQ == 0
    nwin = S // KW
    blk = (S, 2 * HEAD_DIM)
    tblk = (nwin, 2 * HEAD_DIM, KW)
    gblk = (None, None, 1, 2 * HEAD_DIM)
    dq, dk, dv, dqg, dkg = pl.pallas_call(
        body, name=name, grid=(n_ex, nhp),
        in_specs=[pl.BlockSpec(blk, lambda e, h: (e, h)), pl.BlockSpec(blk, lambda e, h: (e, h)),
                  pl.BlockSpec(blk, lambda e, h: (e, h + nhp)),
                  pl.BlockSpec(blk, lambda e, h: (e, h)), pl.BlockSpec(blk, lambda e, h: (e, h)),
                  pl.BlockSpec((1, 2 * HEAD_DIM), lambda e, h: (0, 0)), pl.BlockSpec((1, 2 * HEAD_DIM), lambda e, h: (0, 0))],
        out_specs=[pl.BlockSpec(blk, lambda e, h: (e, h))] * 3 + [pl.BlockSpec(gblk, lambda e, h: (e, h, 0, 0))] * 2,
        out_shape=[jax.ShapeDtypeStruct((T, D), F32)] * 3 + [jax.ShapeDtypeStruct((n_ex, nhp, 1, 2 * HEAD_DIM), F32)] * 2,
        scratch_shapes=[pltpu.VMEM(tblk, BF16), pltpu.VMEM(tblk, BF16),
                        pltpu.VMEM((2,) + blk, BF16), pltpu.VMEM((2,) + blk, BF16), pltpu.VMEM((2,) + blk, BF16),
                        pltpu.VMEM(blk, F32), pltpu.VMEM(tblk, F32), pltpu.VMEM(tblk, F32)],
        compiler_params=_cp(("parallel", "parallel")),
    )(q, kv, kv, tot, do, qg, kg)
    return dq, dk, dv, dqg, dkg


def _place():
    return lax.axis_index("x"), lax.axis_index("y"), lax.axis_index("c")


def _all_gather8(x_shard, *, name):
    m_per, n = x_shard.shape

    def body(x_ref, out_ref, send_sems, recv_sems, local_sem):
        x, y, c = _place()
        me, sibling = (x, y, c), (x, y, 1 - c)
        chips = [(1 - x, y), (x, 1 - y), (1 - x, 1 - y)]

        def rows(px, py, pc):
            return out_ref.at[pl.ds((4 * px + 2 * py + pc) * m_per, m_per), :]

        def copy(k, block, to, src=None):
            return pltpu.make_async_remote_copy(
                src_ref=rows(*block) if src is None else src, dst_ref=rows(*block),
                send_sem=send_sems.at[k], recv_sem=recv_sems.at[k], device_id=to, device_id_type=MESH)

        mine = pltpu.make_async_copy(x_ref, rows(*me), local_sem)
        mine.start()
        first = [copy(0, me, sibling, src=x_ref)]
        first += [copy(1 + j, me, (*chip, c), src=x_ref) for j, chip in enumerate(chips)]
        for cp in first:
            cp.start()
        passed = [copy(4 + j, (*chip, c), sibling) for j, chip in enumerate(chips)]
        for j, chip in enumerate(chips):
            copy(1 + j, (*chip, c), me).wait_recv()
            passed[j].start()
        copy(0, sibling, me).wait_recv()
        for j, chip in enumerate(chips):
            copy(4 + j, (*chip, 1 - c), me).wait_recv()
        for cp in first + passed:
            cp.wait_send()
        mine.wait()

    return pl.pallas_call(
        body, name=name, out_shape=jax.ShapeDtypeStruct((8 * m_per, n), x_shard.dtype),
        in_specs=[pl.BlockSpec(memory_space=pltpu.VMEM)], out_specs=pl.BlockSpec(memory_space=pltpu.VMEM),
        scratch_shapes=[pltpu.SemaphoreType.DMA((7,)), pltpu.SemaphoreType.DMA((7,)), pltpu.SemaphoreType.DMA],
        compiler_params=pltpu.CompilerParams(vmem_limit_bytes=VMEM_LIMIT),
    )(x_shard)


def _sibling_sum_half(x, *, name):
    R, C = x.shape
    half = R // 2
    assert half % 16 == 0

    def body(x_ref, o_ref, theirs, send_sem, recv_sem):
        px, py, pc = _place()
        cp = pltpu.make_async_remote_copy(src_ref=x_ref, dst_ref=theirs, send_sem=send_sem, recv_sem=recv_sem,
                                          device_id=(px, py, 1 - pc), device_id_type=MESH)
        cp.start()
        cp.wait()
        rows = pl.ds(pl.multiple_of(pc * half, 8), half)
        o_ref[...] = (x_ref[rows, :] + theirs[rows, :]).astype(BF16)

    return pl.pallas_call(
        body, name=name, out_shape=jax.ShapeDtypeStruct((half, C), BF16),
        in_specs=[pl.BlockSpec(memory_space=pltpu.VMEM)], out_specs=pl.BlockSpec(memory_space=pltpu.VMEM),
        scratch_shapes=[pltpu.VMEM((R, C), x.dtype), pltpu.SemaphoreType.DMA, pltpu.SemaphoreType.DMA],
        compiler_params=pltpu.CompilerParams(vmem_limit_bytes=VMEM_LIMIT),
    )(x)


def _sum_blocks(x, n, *, name):
    R = x.shape[0] // n

    def body(x_ref, o_ref):
        acc = x_ref[pl.ds(0, R), :].astype(F32)
        for k in range(1, n):
            acc = acc + x_ref[pl.ds(k * R, R), :].astype(F32)
        o_ref[...] = acc

    return pl.pallas_call(body, name=name, out_shape=jax.ShapeDtypeStruct((R, x.shape[1]), F32),
                          compiler_params=pltpu.CompilerParams(vmem_limit_bytes=VMEM_LIMIT))(x)


def _colsum(x, *, name):
    def body(x_ref, o_ref):
        o_ref[...] = jnp.sum(x_ref[...], axis=0, keepdims=True)
    return pl.pallas_call(body, name=name, out_shape=jax.ShapeDtypeStruct((1, x.shape[1]), x.dtype))(x)


ANY = pl.BlockSpec(memory_space=pl.ANY)


class _ChipExchange:
    SCRATCH = [pltpu.SemaphoreType.DMA((3,)), pltpu.SemaphoreType.DMA((3,)), pltpu.SemaphoreType.DMA]

    @staticmethod
    def out_shape(src, scatter):
        return jax.ShapeDtypeStruct(((4,) + tuple(src.shape[1:])) if scatter else ((4, 2) + tuple(src.shape[1:])), src.dtype)

    def __init__(self, src_ref, out_ref, send_sems, recv_sems, local_sem, scatter):
        x, y, c = _place()
        myj = 2 * x + y
        chips = [(1 - x, y), (x, 1 - y), (1 - x, 1 - y)]

        def slot(j):
            return out_ref.at[j] if scatter else out_ref.at[j, c]

        def piece(j):
            return src_ref.at[j] if scatter else src_ref.at[c]

        self.mine = pltpu.make_async_copy(piece(myj), slot(myj), local_sem)
        self.sends = [pltpu.make_async_remote_copy(
            src_ref=piece(2 * cx + cy), dst_ref=slot(myj), send_sem=send_sems.at[k], recv_sem=recv_sems.at[k],
            device_id=(cx, cy, c), device_id_type=MESH) for k, (cx, cy) in enumerate(chips)]
        self.recvs = [pltpu.make_async_remote_copy(
            src_ref=slot(2 * cx + cy), dst_ref=slot(2 * cx + cy), send_sem=send_sems.at[k], recv_sem=recv_sems.at[k],
            device_id=(cx, cy, c), device_id_type=MESH) for k, (cx, cy) in enumerate(chips)]

    def start(self):
        self.mine.start()
        for cp in self.sends:
            cp.start()

    def wait(self):
        for cp in self.recvs:
            cp.wait_recv()
        for cp in self.sends:
            cp.wait_send()
        self.mine.wait()


def _sibling_fill(buf, *, axis, name):
    def half(ref, h):
        return ref.at[h] if axis == 0 else ref.at[:, h]

    def body(in_ref, out_ref, send_sem, recv_sem):
        x, y, c = _place()
        cp = pltpu.make_async_remote_copy(src_ref=half(out_ref, c), dst_ref=half(out_ref, c), send_sem=send_sem, recv_sem=recv_sem,
                                          device_id=(x, y, 1 - c), device_id_type=MESH)
        cp.start()
        pltpu.make_async_remote_copy(src_ref=half(out_ref, 1 - c), dst_ref=half(out_ref, 1 - c), send_sem=send_sem, recv_sem=recv_sem,
                                     device_id=(x, y, 1 - c), device_id_type=MESH).wait_recv()
        cp.wait_send()

    return pl.pallas_call(
        body, name=name, out_shape=jax.ShapeDtypeStruct(buf.shape, buf.dtype), in_specs=[ANY], out_specs=ANY,
        input_output_aliases={0: 0}, scratch_shapes=[pltpu.SemaphoreType.DMA, pltpu.SemaphoreType.DMA],
    )(buf)


def _sibling_swap_half(g, *, name):
    def body(g_ref, out_ref, send_sem, recv_sem):
        x, y, c = _place()
        cp = pltpu.make_async_remote_copy(src_ref=g_ref.at[:, 1 - c], dst_ref=out_ref, send_sem=send_sem, recv_sem=recv_sem,
                                          device_id=(x, y, 1 - c), device_id_type=MESH)
        cp.start()
        cp.wait()

    return pl.pallas_call(
        body, name=name, out_shape=jax.ShapeDtypeStruct((g.shape[0],) + g.shape[2:], g.dtype), in_specs=[ANY], out_specs=ANY,
        scratch_shapes=[pltpu.SemaphoreType.DMA, pltpu.SemaphoreType.DMA],
    )(g)


def _add_my_half(g, b, cidx, *, name, tr=1024):
    n, _, R, C = g.shape
    tr = max(t for t in range(16, tr + 1, 16) if R % t == 0)

    def body(c_ref, g_ref, b_ref, o_ref):
        o_ref[...] = (g_ref[...] + b_ref[...]).astype(o_ref.dtype)

    return pl.pallas_call(
        body, name=name, out_shape=jax.ShapeDtypeStruct((n, R, C), BF16),
        grid_spec=pltpu.PrefetchScalarGridSpec(
            num_scalar_prefetch=1, grid=(n, R // tr),
            in_specs=[pl.BlockSpec((None, None, tr, C), lambda j, i, c: (j, c[0], i, 0)),
                      pl.BlockSpec((None, tr, C), lambda j, i, c: (j, i, 0))],
            out_specs=pl.BlockSpec((None, tr, C), lambda j, i, c: (j, i, 0))),
        compiler_params=_cp(("parallel", "parallel")),
    )(cidx, g, b)


def _sum4_into_half(q, cidx, *, name, tr=1024):
    _, R, C = q.shape
    tr = max(t for t in range(16, tr + 1, 16) if R % t == 0)

    def body(c_ref, q_ref, o_ref):
        o_ref[...] = ((q_ref[0].astype(F32) + q_ref[1].astype(F32)) + q_ref[2].astype(F32)) + q_ref[3].astype(F32)

    return pl.pallas_call(
        body, name=name, out_shape=jax.ShapeDtypeStruct((2, R, C), F32),
        grid_spec=pltpu.PrefetchScalarGridSpec(
            num_scalar_prefetch=1, grid=(R // tr,),
            in_specs=[pl.BlockSpec((4, tr, C), lambda i, c: (0, i, 0))],
            out_specs=pl.BlockSpec((None, tr, C), lambda i, c: (c[0], i, 0))),
        compiler_params=_cp(("parallel",)),
    )(cidx, q)


def _pack_rows(parts, width=1024, row_multiple=8):
    rows, spans, r0 = [], [], 0
    for p in parts:
        n = p.size
        nr = 8 * (-(-n // (8 * width)))
        flat = p.reshape(-1)
        if nr * width != n:
            flat = jnp.pad(flat, (0, nr * width - n))
        rows.append(flat.reshape(nr, width))
        spans.append((r0, nr, n, p.shape))
        r0 += nr
    if r0 % row_multiple:
        rows.append(jnp.zeros((row_multiple - r0 % row_multiple, width), parts[0].dtype))
    return jnp.concatenate(rows, axis=0), spans


def _unpack_rows(buf, spans):
    return [buf[r0:r0 + nr].reshape(-1)[:n].reshape(shape) for (r0, nr, n, shape) in spans]


def kernel(x, c, ada_w, ada_b, mix_norm_g, mlp_norm_g, mlp_w1, mlp_w2, s5_a_re, s5_a_im, s5_log_dt, s5_b_re, s5_b_im, s5_c_re, s5_c_im, s5_d, s5_w_glu, kv_ada_w, kv_ada_b, kv_norm_g, w_kv, k_norm_g, sb_w_q, q_norm_g, sb_w_o, loss_target, m_ada_w, m_ada_b, m_mix_norm_g, m_mlp_norm_g, m_mlp_w1, m_mlp_w2, m_s5_a_re, m_s5_a_im, m_s5_log_dt, m_s5_b_re, m_s5_b_im, m_s5_c_re, m_s5_c_im, m_s5_d, m_s5_w_glu, m_kv_ada_w, m_kv_ada_b, m_kv_norm_g, m_w_kv, m_k_norm_g, m_sb_w_q, m_q_norm_g, m_sb_w_o, v_ada_w, v_ada_b, v_mix_norm_g, v_mlp_norm_g, v_mlp_w1, v_mlp_w2, v_s5_a_re, v_s5_a_im, v_s5_log_dt, v_s5_b_re, v_s5_b_im, v_s5_c_re, v_s5_c_im, v_s5_d, v_s5_w_glu, v_kv_ada_w, v_kv_ada_b, v_kv_norm_g, v_w_kv, v_k_norm_g, v_sb_w_q, v_q_norm_g, v_sb_w_o):
    E, S, D = x.shape
    T = E * S
    FF = 4 * D
    NB = 8 * E
    px, py, pc = _place()
    chip = 2 * px + py
    dev = 4 * px + 2 * py + pc
    cidx = jnp.reshape(pc, (1,)).astype(jnp.int32)
    x0 = x.reshape(T, D)
    tgt = loss_target.reshape(T, D)

    nc_rows, nd = c.size // 128, s5_d.size // 128
    cd = jnp.concatenate([c.reshape(nc_rows, 128), jnp.pad(s5_d.reshape(nd, 128), ((0, 8 - nd), (0, 0)))], axis=0)
    cd_all = _all_gather8(cd, name="ag_c_d").reshape(8, nc_rows + 8, 128)
    c_all = cd_all[:, :nc_rows].reshape(NB, D)
    d_full = cd_all.reshape(4, 2, nc_rows + 8, 128)[:, 0, nc_rows:nc_rows + nd].reshape(1, D)
    sc_all = (c_all * _sigmoid(c_all)).astype(BF16)
    wa = ada_w.shape[2]
    wk = kv_ada_w.shape[1]
    m_sh = jnp.concatenate([_mm(sc_all, _Layer(ada_w, 0), "nn", name="ada0", tn=256),
                            _mm(sc_all, _Layer(ada_w, 1), "nn", name="ada1", tn=256),
                            _mm(sc_all, kv_ada_w, "nn", name="ada_kv", tn=256)], axis=1)
    m_all = _all_gather8(m_sh, name="ag_m").reshape(4, 2, NB, 2 * wa + wk)[:, 0]
    mods = []
    for l in range(2):
        full = jnp.transpose(m_all[:, :, l * wa:(l + 1) * wa], (1, 0, 2)).reshape(NB, 6 * D) + ada_b[l]
        mine = lax.dynamic_slice_in_dim(full, E * dev, E, axis=0)
        mods.append([mine[:, i * D:(i + 1) * D].reshape(E, 1, D) for i in range(6)])
    full = jnp.transpose(m_all[:, :, 2 * wa:], (1, 0, 2)).reshape(NB, 2 * D) + kv_ada_b
    mine = lax.dynamic_slice_in_dim(full, E * dev, E, axis=0)
    kv_sh, kv_sc = [mine[:, i * D:(i + 1) * D].reshape(E, 1, D) for i in range(2)]

    wpack_a = jnp.concatenate([mlp_w1[0], mlp_w2[0], jnp.concatenate([s5_w_glu[0], w_kv], axis=1), sb_w_q[0]], axis=0).astype(BF16)
    wpack_b = jnp.concatenate([mlp_w1[1], mlp_w2[1], sb_w_o[0]], axis=0).astype(BF16)
    RA, RB = wpack_a.shape[0], wpack_b.shape[0]
    RW = RA + RB

    tm = min(2048, S)
    tm_res = min(1024, S)
    gbuf = [jax.ShapeDtypeStruct((4, RW, D), F32)]

    def grad_mm(act, dout, kind, roff, nr, c0, nc, name, transposed=False):
        gbuf[0] = _mm(act, dout, "nn" if transposed else "tn", name=name, tm=1024, tk=2048,
                      into=_Sharded(gbuf[0], kind, roff, nr, c0, nc))

    def mlp_fwd(xa, l, mod):
        sh_m, sc_m, g_m = mod[3], mod[4], mod[5]
        h, h_t = _norm_mod_fwd(xa, mlp_norm_g[l:l + 1], sh_m, sc_m, n_ex=E, out_dtype=BF16, name=f"mlp_norm{l}", with_transpose=True)

        def relu_sq(acc):
            ra = jnp.maximum(acc, 0.0)
            return ra * ra, ra
        r, ra = _mm(h, W1[l], "nn", name=f"mlp_up{l}", out_dtypes=(BF16, BF16), tm=tm, epilogue=relu_sq)
        xb, ff = _mm(r, W2[l], "nn", name=f"mlp_down{l}", out_dtypes=(F32, F32), tm=tm_res,
                     extras=[_mn_extra(xa), _vec_extra(g_m, S)],
                     epilogue=lambda acc, xat, gt: (xat + gt * acc, acc))
        return xb, (h_t, r, ra, ff)

    def mlp_bwd(dxb, dff, xa, l, mod, saved, gated=None):
        sc_m = mod[4]
        h_t, r, ra, _ = saved
        da = _mm(dff, W2[l], "nt", name=f"mlp_down_dx{l}", out_dtypes=(BF16,), tm=tm, extras=[_mn_extra(ra)],
                 epilogue=lambda acc, rat: (acc * (2.0 * rat.astype(F32)),))
        grad_mm(r, dff, "rows", (2 + l) * D, D, 0, D, f"mlp_down_dw{l}")
        dh = _mm(da, W1[l], "nt", name=f"mlp_up_dx{l}", tm=tm)
        grad_mm(h_t, da, "cols", l * D, D, 0, D, f"mlp_up_dw{l}", transposed=True)
        return _norm_mod_bwd(xa, dh, dxb, mlp_norm_g[l:l + 1], sc_m, n_ex=E, name=f"mlp_norm_bwd{l}", gated=gated)

    ab_re, ab_im, bb_re, bb_im = _s5_disc(s5_a_re[0], s5_a_im[0], s5_log_dt[0], s5_b_re[0], s5_b_im[0])
    cf, cr = _s5_consts(ab_re, ab_im)
    Wb, Wc = _s5_blockdiag(bb_re, bb_im, s5_c_re[0], s5_c_im[0])
    ng = D // U_LANES

    mod0, mod1 = mods
    h0 = _norm_mod_fwd(x0, mix_norm_g[0:1], mod0[0], mod0[1], n_ex=E, out_dtype=F32, name="mix_norm0")
    y, gy, gy_t, s5_states, wfull_a = _s5_fwd(h0, Wb, Wc, cf, d_full, wpack_a.reshape(2, RA // 2, D), n_ex=E, name="s5_fwd")
    wfull_a = _sibling_fill(wfull_a, axis=1, name="wgather_a_d2d").reshape(4, RA, D)

    W1 = [_Sharded(wfull_a, "cols", 0, D, 0, D), None]
    W2 = [_Sharded(wfull_a, "rows", D, D, 0, D), None]
    Wglu = _Sharded(wfull_a, "cols", 2 * D, D, 0, D // 2)
    Wkv = _Sharded(wfull_a, "cols", 2 * D, D, D // 2, D // 2)
    Wq = _Sharded(wfull_a, "rows", 3 * D, D // 4, 0, D)
    vg = _mm(gy, Wglu, "nn", name="glu_up", tm=tm)
    (x1,), _ = _rowwise(lambda v, g, xt, ga: ([xt + ga * (v * _sigmoid(g))], []),
                        [(vg, D, 0), (vg, D, 1), (x0, D, 0)], [mod0[2]], [], [(D, F32)], [], n_ex=E, name="glu_gate")
    x2, saved_mlp0 = mlp_fwd(x1, 0, mod0)

    hkv, hkv_t, h1, h1_t = _norm_mod_fwd_pair(x2, (kv_norm_g.reshape(1, D), kv_sh, kv_sc), (mix_norm_g[1:2], mod1[0], mod1[1]),
                                              n_ex=E, name="kv_mix_norm")
    kvf = _mm(hkv, Wkv, "nn", name="kv_proj", tm=tm)
    qf = _mm(h1, Wq, "nn", name="q_proj", tm=tm)
    qg2 = jnp.tile(q_norm_g.reshape(1, HEAD_DIM), (1, 2))
    kg2 = jnp.tile(k_norm_g.reshape(1, HEAD_DIM), (1, 2))
    o, lf_tot, o_t, wfull_b = _attn_fwd(qf, kvf, qg2, kg2, wpack_b.reshape(2, RB // 2, D), n_ex=E, name="attn_fwd")
    wfull_b = _sibling_fill(wfull_b, axis=1, name="wgather_b_d2d").reshape(4, RB, D)
    W1[1] = _Sharded(wfull_b, "cols", 0, D, 0, D)
    W2[1] = _Sharded(wfull_b, "rows", D, D, 0, D)
    Wo = _Sharded(wfull_b, "rows", 2 * D, D // 4, 0, D)
    x3, mix1 = _mm(o, Wo, "nn", name="o_proj", out_dtypes=(F32, F32), tm=tm_res,
                   extras=[_mn_extra(x2), _vec_extra(mod1[2], S)],
                   epilogue=lambda acc, xat, gt: (xat + gt * acc, acc))
    x4, saved_mlp1 = mlp_fwd(x3, 1, mod1)

    def loss_fn(xt, tt, fft, gmt):
        dx = (xt - tt) * (1.0 / D)
        dff, dgm = _gated(dx, fft, gmt)
        return [dx, dff], [_csum(jnp.square(xt - tt)) * (0.5 / D), dgm]
    (dx4, dff1), (lsum, dgm1) = _rowwise(loss_fn, [(x4, D, 0), (tgt, D, 0), (saved_mlp1[3], D, 0)], [mod1[5]], [],
                                         [(D, F32), (D, BF16)], [D, D], n_ex=E, name="loss")
    loss = lax.psum(jnp.sum(lsum), ("x", "y", "c"))

    (dx3, dmix1), (dsh_m1, dsc_m1, dg_mlp1, dga1) = mlp_bwd(dx4, dff1, x3, 1, mod1, saved_mlp1, gated=([(mix1, D, 0)], mod1[2], _gated, D))
    do = _mm(dmix1, Wo, "nt", name="o_proj_dx", tm=tm)
    grad_mm(o_t, dmix1, "rows", 5 * D + D // 4, D // 4, 0, D, "o_proj_dw", transposed=True)
    dq, dk, dv, dqg, dkg = _attn_bwd(qf, kvf, lf_tot, do, qg2, kg2, n_ex=E, name="attn_bwd")
    dh1 = _mm(dq, Wq, "nt", name="q_proj_dx", tm=tm)
    grad_mm(h1_t, dq, "rows", 5 * D, D // 4, 0, D, "q_proj_dw", transposed=True)
    dkv = jnp.concatenate([dk, dv], axis=1)
    dhkv = _mm(dkv, Wkv, "nt", name="kv_proj_dx", tm=tm)
    grad_mm(hkv_t, dkv, "cols", 4 * D, D, D // 2, D // 2, "kv_proj_dw", transposed=True)
    (dx2, dff0), (dsh_a1, dsc_a1, dg_mix1, dkv_sh, dkv_sc, dg_kv, dgm0) = _norm_mod_bwd_pair(
        x2, dh1, dhkv, dx3, (mix_norm_g[1:2], mod1[1]), (kv_norm_g.reshape(1, D), kv_sc),
        ([(saved_mlp0[3], D, 0)], mod0[5], _gated, D), n_ex=E, name="kv_mix_norm_bwd")

    def glu_bwd(d, v, g, ga):
        sg = _sigmoid(g)
        dm = ga * d
        return jnp.concatenate([dm * sg, dm * v * sg * (1.0 - sg)], axis=1).astype(BF16), _csum(d * (v * sg))
    (dx1, dvg), (dsh_m0, dsc_m0, dg_mlp0, dga0) = mlp_bwd(dx2, dff0, x1, 0, mod0, saved_mlp0,
                                                          gated=([(vg, D, 0), (vg, D, 1)], mod0[2], glu_bwd, 2 * D))
    dgy = _mm(dvg, Wglu, "nt", name="glu_up_dx", tm=tm)
    grad_mm(gy_t, dvg, "cols", 4 * D, D, 0, D // 2, "glu_up_dw", transposed=True)

    gpack = gbuf[0].reshape(4, 2, RW // 2, D)
    theirs = _sibling_swap_half(gpack, name="gscatter_d2d")
    chip_sum = _add_my_half(gpack, theirs, cidx, name="gscatter_add")
    dh0, dWb, dWc, dab, dd, from_chips = _s5_bwd(h0, y, dgy, s5_states, Wb, Wc, cr, d_full, chip_sum, n_ex=E, name="s5_bwd")
    ghalf = _sum4_into_half(from_chips, cidx, name="gscatter_sum")
    gsh = _sibling_fill(ghalf, axis=0, name="gscatter_fill").reshape(RW, D)
    (gx,), (dsh_a0, dsc_a0, dg_mix0) = _norm_mod_bwd(x0, dh0, dx1, mix_norm_g[0:1], mod0[1], n_ex=E, name="mix_norm_bwd0")
    grad_x = gx.reshape(E, S, D)

    dm_mine = jnp.concatenate([t.reshape(E, D) for t in
                               (dsh_a0, dsc_a0, dga0, dsh_m0, dsc_m0, dgm0, dsh_a1, dsc_a1, dga1, dsh_m1, dsc_m1, dgm1, dkv_sh, dkv_sc)], axis=1)
    dm_all = _all_gather8(dm_mine.reshape(8, -1), name="ag_dm").reshape(NB, 14 * D)
    sc_f32 = c_all * _sigmoid(c_all)
    g_ada_w = jax.ShapeDtypeStruct(ada_w.shape, F32)
    for l in range(2):
        g_ada_w = _mm(sc_f32, lax.dynamic_slice_in_dim(dm_all, l * 6 * D + chip * wa, wa, axis=1), "tn", name=f"ada_dw{l}", tn=256,
                      into=_Layer(g_ada_w, l))
    g_kv_ada_w = _mm(sc_f32, lax.dynamic_slice_in_dim(dm_all, 12 * D + chip * wk, wk, axis=1), "tn", name="ada_kv_dw", tn=256)
    db_all = _colsum(dm_all, name="ada_db")
    g_ada_b = db_all[0, :12 * D].reshape(2, 6 * D)
    g_kv_ada_b = db_all[0, 12 * D:]

    dWb_re, dWb_im, dC_re, dC_im = _s5_unblock(dWb, dWc)
    small_parts = [dg_mix0.sum(0), dg_mix1.sum(0), dg_mlp0.sum(0), dg_mlp1.sum(0), dg_kv.sum(0),
                   dqg.sum((0, 1, 2)).reshape(2, HEAD_DIM).sum(0), dkg.sum((0, 1, 2)).reshape(2, HEAD_DIM).sum(0),
                   dd[:, 0, :], dab[:, 0, :], dab[:, 1, :], dWb_re, dWb_im, dC_re, dC_im]
    spack, spans = _pack_rows(small_parts, row_multiple=32)
    chip_half = _sibling_sum_half(spack, name="small_d2d")
    ssum = _sum_blocks(_all_gather8(chip_half, name="ag_small"), 4, name="sum_small")
    (g_mix0, g_mix1, g_mlp0, g_mlp1, g_kvn, g_qn, g_kn, g_d, g_abr, g_abi, g_bbr, g_bbi, g_cre, g_cim) = _unpack_rows(ssum, spans)
    _, disc_vjp = jax.vjp(_s5_disc, s5_a_re[0], s5_a_im[0], s5_log_dt[0], s5_b_re[0], s5_b_im[0])
    g_are, g_aim, g_ldt, g_bre, g_bim = disc_vjp((g_abr.reshape(ab_re.shape), g_abi.reshape(ab_im.shape), g_bbr, g_bbi))
    g_s5d = lax.dynamic_slice_in_dim(g_d.reshape(1, D), chip * s5_d.shape[1], s5_d.shape[1], axis=1)

    def upd_big(w, m, v, roff, cb, name):
        shape = w.shape
        W = shape[-1]
        d_, m_, v_, g_ = _adamw2d(w.reshape(-1, W), gsh, m.reshape(-1, W), v.reshape(-1, W), name=name, g_roff=roff, g_cb=cb)
        return [t.reshape(shape) for t in (g_, d_, m_, v_)]

    def upd_own(w, g, m, v, name):
        shape = w.shape
        W = shape[-1]
        d_, m_, v_, g_ = _adamw2d(w.reshape(-1, W), g.reshape(-1, W), m.reshape(-1, W), v.reshape(-1, W), name=name)
        return [t.reshape(shape) for t in (g_, d_, m_, v_)]

    res = {}
    res["ada_w"] = upd_own(ada_w, g_ada_w, m_ada_w, v_ada_w, "adam_ada_w")
    res["kv_ada_w"] = upd_own(kv_ada_w, g_kv_ada_w, m_kv_ada_w, v_kv_ada_w, "adam_kv_ada_w")
    res["mlp_w1"] = upd_big(mlp_w1, m_mlp_w1, v_mlp_w1, 0, 0, "adam_w1")
    res["mlp_w2"] = upd_big(mlp_w2, m_mlp_w2, v_mlp_w2, 2 * D, 0, "adam_w2")
    res["s5_w_glu"] = upd_big(s5_w_glu, m_s5_w_glu, v_s5_w_glu, 4 * D, 0, "adam_glu")
    res["w_kv"] = upd_big(w_kv, m_w_kv, v_w_kv, 4 * D, 1, "adam_wkv")
    res["sb_w_q"] = upd_big(sb_w_q, m_sb_w_q, v_sb_w_q, 5 * D, 0, "adam_wq")
    res["sb_w_o"] = upd_big(sb_w_o, m_sb_w_o, v_sb_w_o, 5 * D + D // 4, 0, "adam_wo")

    small = {
        "ada_b": (ada_b, g_ada_b, m_ada_b, v_ada_b),
        "mix_norm_g": (mix_norm_g, jnp.stack([g_mix0, g_mix1]), m_mix_norm_g, v_mix_norm_g),
        "mlp_norm_g": (mlp_norm_g, jnp.stack([g_mlp0, g_mlp1]), m_mlp_norm_g, v_mlp_norm_g),
        "s5_a_re": (s5_a_re, g_are[None], m_s5_a_re, v_s5_a_re),
        "s5_a_im": (s5_a_im, g_aim[None], m_s5_a_im, v_s5_a_im),
        "s5_log_dt": (s5_log_dt, g_ldt[None], m_s5_log_dt, v_s5_log_dt),
        "s5_b_re": (s5_b_re, g_bre[None], m_s5_b_re, v_s5_b_re),
        "s5_b_im": (s5_b_im, g_bim[None], m_s5_b_im, v_s5_b_im),
        "s5_c_re": (s5_c_re, g_cre[None], m_s5_c_re, v_s5_c_re),
        "s5_c_im": (s5_c_im, g_cim[None], m_s5_c_im, v_s5_c_im),
        "s5_d": (s5_d, g_s5d, m_s5_d, v_s5_d),
        "kv_ada_b": (kv_ada_b, g_kv_ada_b, m_kv_ada_b, v_kv_ada_b),
        "kv_norm_g": (kv_norm_g, g_kvn, m_kv_norm_g, v_kv_norm_g),
        "k_norm_g": (k_norm_g, g_kn, m_k_norm_g, v_k_norm_g),
        "q_norm_g": (q_norm_g, g_qn.reshape(q_norm_g.shape), m_q_norm_g, v_q_norm_g),
    }
    names = list(small)
    packs = [_pack_rows([small[n][i].reshape(small[n][0].shape) for n in names]) for i in range(4)]
    sp = packs[0][1]
    d_, m_, v_, g_ = _adamw2d(packs[0][0], packs[1][0], packs[2][0], packs[3][0], name="adam_small")
    for n, gg, dd_, mm_, vv_ in zip(names, _unpack_rows(g_, sp), _unpack_rows(d_, sp), _unpack_rows(m_, sp), _unpack_rows(v_, sp)):
        res[n] = [gg, dd_, mm_, vv_]

    order = ["ada_w", "ada_b", "mix_norm_g", "mlp_norm_g", "mlp_w1", "mlp_w2", "s5_a_re", "s5_a_im", "s5_log_dt", "s5_b_re", "s5_b_im",
             "s5_c_re", "s5_c_im", "s5_d", "s5_w_glu", "kv_ada_w", "kv_ada_b", "kv_norm_g", "w_kv", "k_norm_g", "sb_w_q", "q_norm_g", "sb_w_o"]
    return (loss, grad_x, *[res[n][0] for n in order], *[res[n][1] for n in order], *[res[n][2] for n in order], *[res[n][3] for n in order])
```

```python
import functools
import math

import jax
import jax.numpy as jnp
from jax import lax
from jax.experimental import pallas as pl
from jax.experimental.pallas import tpu as pltpu

F32 = jnp.float32
BF16 = jnp.bfloat16
EPS = 1e-6
HEAD_DIM = 64
S5_GROUP = 16
S5_STATE = 64
GROUPS_PER_STEP = 8
U_LANES = GROUPS_PER_STEP * S5_GROUP
ST_LANES = GROUPS_PER_STEP * S5_STATE
SCAN_LANES = 256
SCAN_UNROLL = 4
VMEM_LIMIT = 56 * 1024 * 1024
ADAM_LR, ADAM_B1, ADAM_B2, ADAM_EPS, ADAM_WD, ADAM_STEP = 0.001, 0.9, 0.999, 1e-08, 0.01, 10
MESH = pl.DeviceIdType.MESH


def _cp(sem):
    return pltpu.CompilerParams(dimension_semantics=sem, vmem_limit_bytes=VMEM_LIMIT)


class _Sharded:
    def __init__(self, buf, kind, roff, nr, c0, nc):
        self.buf, self.kind, self.roff, self.nr, self.c0, self.nc = buf, kind, roff, nr, c0, nc
        self.shape = (nr, 4 * nc) if kind == "cols" else (4 * nr, nc)

    def operand(self, dims, tn, tk):
        roff, nr, c0, nc = self.roff, self.nr, self.c0, self.nc
        if self.kind == "cols" and dims == "nn":
            tk = min(tk, nr)
            assert roff % tk == 0
            return nc, tk, (None, tk, nc), lambda i, j, k: (j, roff // tk + k, c0 // nc)
        if self.kind == "cols":
            tn = min(tn, nr)
            assert roff % tn == 0
            return tn, nc, (None, tn, nc), lambda i, j, k: (k, roff // tn + j, c0 // nc)
        if dims == "nn":
            tn = min(tn, nc)
            assert roff % nr == 0 and c0 % tn == 0
            return tn, nr, (None, nr, tn), lambda i, j, k: (k, roff // nr, c0 // tn + j)
        tk = min(tk, nc)
        assert roff % nr == 0 and c0 % tk == 0
        return nr, tk, (None, nr, tk), lambda i, j, k: (j, roff // nr, c0 // tk + k)

    def result(self, tm, tn):
        roff, nr, c0, nc = self.roff, self.nr, self.c0, self.nc
        if self.kind == "cols":
            tm = min(tm, nr)
            assert roff % tm == 0
            return tm, nc, (None, tm, nc), lambda i, j, k: (j, roff // tm + i, c0 // nc)
        tm, tn = min(tm, nr), min(tn, nc)
        assert roff % tm == 0 and c0 % tn == 0
        per = nr // tm
        return tm, tn, (None, tm, tn), lambda i, j, k: (i // per, roff // tm + i % per, c0 // tn + j)


class _Layer:
    def __init__(self, buf, layer):
        self.buf, self.layer, self.shape = buf, layer, tuple(buf.shape[1:])

    def operand(self, dims, tn, tk):
        assert dims == "nn"
        layer = self.layer
        return tn, tk, (None, tk, tn), lambda i, j, k: (layer, k, j)

    def result(self, tm, tn):
        layer = self.layer
        return tm, tn, (None, tm, tn), lambda i, j, k: (layer, i, j)


def _mm(a, b, dims, *, name, out_dtypes=(F32,), epilogue=None, extras=(), tm=512, tn=1024, tk=1024, into=None):
    bshape = b.shape
    if dims == "nn":
        (M, K), (_, N) = a.shape, bshape
    elif dims == "nt":
        (M, K), (N, _) = a.shape, bshape
    else:
        (K, M), (_, N) = a.shape, bshape
    tm, tn, tk = min(tm, M), min(tn, N), min(tk, K)
    b_arr = b
    if into is not None:
        assert (M, N) == into.shape and len(out_dtypes) == 1 and not isinstance(b, _Sharded)
        tm, tn, o_blk, o_map = into.result(tm, tn)
        out_specs, out_shape = [pl.BlockSpec(o_blk, o_map)], [jax.ShapeDtypeStruct(into.buf.shape, into.buf.dtype)]
    if isinstance(b, (_Sharded, _Layer)):
        tn, tk, b_blk, b_map = b.operand(dims, tn, tk)
        b_spec, b_arr = pl.BlockSpec(b_blk, b_map), b.buf
    else:
        b_spec = pl.BlockSpec((tn, tk), lambda i, j, k: (j, k)) if dims == "nt" else pl.BlockSpec((tk, tn), lambda i, j, k: (k, j))
    if into is None:
        out_specs = [pl.BlockSpec((tm, tn), lambda i, j, k: (i, j)) for _ in out_dtypes]
        out_shape = [jax.ShapeDtypeStruct((M, N), d) for d in out_dtypes]
    assert M % tm == 0 and N % tn == 0 and K % tk == 0, (M, N, K, tm, tn, tk)
    nk = K // tk
    extras = [e(tm, tn) for e in extras]
    a_spec = pl.BlockSpec((tk, tm), lambda i, j, k: (k, i)) if dims == "tn" else pl.BlockSpec((tm, tk), lambda i, j, k: (i, k))
    contract = {"nn": ((1,), (0,)), "nt": ((1,), (1,)), "tn": ((0,), (0,))}[dims]
    n_ex, n_out = len(extras), len(out_dtypes)
    chain = [into.buf] if into is not None and not isinstance(into.buf, jax.ShapeDtypeStruct) else []
    n_in = n_ex + len(chain)

    def finish(r, ex, outs):
        res = epilogue(r, *[e[...] for e in ex]) if epilogue is not None else (r,)
        for o, v in zip(outs, res):
            o[...] = v.astype(o.dtype)

    def product(a_ref, b_ref):
        return lax.dot_general(a_ref[...].astype(BF16), b_ref[...].astype(BF16), (contract, ((), ())), preferred_element_type=F32)

    def body_one(a_ref, b_ref, *rest):
        finish(product(a_ref, b_ref), rest[:n_ex], rest[n_in:])

    def body_acc(a_ref, b_ref, *rest):
        ex, outs, acc = rest[:n_ex], rest[n_in:n_in + n_out], rest[-1]
        k = pl.program_id(2)

        @pl.when(k == 0)
        def _():
            acc[...] = product(a_ref, b_ref)

        @pl.when(jnp.logical_and(k > 0, k < nk - 1))
        def _():
            acc[...] += product(a_ref, b_ref)

        @pl.when(k == nk - 1)
        def _():
            finish(acc[...] + product(a_ref, b_ref), ex, outs)

    out = pl.pallas_call(
        body_one if nk == 1 else body_acc, name=name, grid=(M // tm, N // tn, nk),
        in_specs=[a_spec, b_spec] + [pl.BlockSpec(blk, im) for (_, blk, im) in extras] + [ANY for _ in chain],
        out_specs=out_specs, out_shape=out_shape,
        input_output_aliases={2 + n_ex: 0} if chain else {},
        scratch_shapes=[] if nk == 1 else [pltpu.VMEM((tm, tn), F32)],
        compiler_params=_cp(("parallel", "parallel", "arbitrary")),
    )(a, b_arr, *[e[0] for e in extras], *chain)
    return out if n_out > 1 else out[0]


def _mn_extra(arr):
    return lambda tm, tn: (arr, (tm, tn), lambda i, j, k: (i, j))


def _vec_extra(vec, S):
    return lambda tm, tn: (vec, (None, 1, tn), lambda i, j, k: ((i * tm) // S, 0, j))


def _rowwise(fn, rows, vecs=(), consts=(), out_rows=(), out_sums=(), *, n_ex, name, tr=512):
    rows = [r if len(r) == 4 else (*r, 0) for r in rows]
    S = min(r[0].shape[0] for r in rows if r[3] == 0) // n_ex
    tr = math.gcd(tr, S)
    assert S % tr == 0
    nb = S // tr
    in_specs = []
    for (arr, w, cb, roff) in rows:
        assert roff % tr == 0
        in_specs.append(pl.BlockSpec((tr, w), functools.partial(lambda e, i, cb, ro: (e * nb + i + ro, cb), cb=cb, ro=roff // tr)))
    for v in vecs:
        in_specs.append(pl.BlockSpec((None, 1, v.shape[-1]), lambda e, i: (e, 0, 0)))
    for c in consts:
        in_specs.append(pl.BlockSpec((1, c.shape[-1]), lambda e, i: (0, 0)))
    n_in, n_or, n_os = len(in_specs), len(out_rows), len(out_sums)
    flipped = [len(o) == 3 and o[2] for o in out_rows]
    out_specs = [pl.BlockSpec((o[0], tr), lambda e, i: (0, e * nb + i)) if f else pl.BlockSpec((tr, o[0]), lambda e, i: (e * nb + i, 0))
                 for o, f in zip(out_rows, flipped)]
    out_specs += [pl.BlockSpec((None, 1, w), lambda e, i: (e, 0, 0)) for w in out_sums]
    out_shape = [jax.ShapeDtypeStruct((o[0], n_ex * S) if f else (n_ex * S, o[0]), o[1]) for o, f in zip(out_rows, flipped)]
    out_shape += [jax.ShapeDtypeStruct((n_ex, 1, w), F32) for w in out_sums]

    def body(*refs):
        ins, o_r, o_s = refs[:n_in], refs[n_in:n_in + n_or], refs[n_in + n_or:]
        ro, so = fn(*[r[...] for r in ins])
        for o, v, f in zip(o_r, ro, flipped):
            o[...] = (v.T if f else v).astype(o.dtype)
        i = pl.program_id(1)
        for o, v in zip(o_s, so):
            @pl.when(i == 0)
            def _(o=o, v=v):
                o[...] = v

            @pl.when(i > 0)
            def _(o=o, v=v):
                o[...] += v

    outs = pl.pallas_call(
        body, name=name, grid=(n_ex, nb), in_specs=in_specs, out_specs=out_specs, out_shape=out_shape,
        compiler_params=_cp(("parallel", "arbitrary")),
    )(*[r[0] for r in rows], *vecs, *consts)
    return outs[:n_or], outs[n_or:]


def _csum(x):
    return jnp.sum(x, axis=0, keepdims=True)


def _norm_mod_fwd(x, g, sh, sc, *, n_ex, out_dtype, name, with_transpose=False):
    def fn(xt, sht, sct, gt):
        r = lax.rsqrt(jnp.mean(xt * xt, axis=-1, keepdims=True) + EPS)
        h = (xt * r * gt) * (1.0 + sct) + sht
        return [h, h] if with_transpose else [h], []
    D = x.shape[1]
    outs = [(D, out_dtype), (D, out_dtype, True)] if with_transpose else [(D, out_dtype)]
    res = _rowwise(fn, [(x, D, 0)], [sh, sc], [g], outs, [], n_ex=n_ex, name=name)[0]
    return res if with_transpose else res[0]


def _norm_mod_fwd_pair(x, first, second, *, n_ex, name):
    def fn(xt, sh1, sc1, sh2, sc2, g1, g2):
        n = xt * lax.rsqrt(jnp.mean(xt * xt, axis=-1, keepdims=True) + EPS)
        h1 = (n * g1) * (1.0 + sc1) + sh1
        h2 = (n * g2) * (1.0 + sc2) + sh2
        return [h1, h1, h2, h2], []
    D = x.shape[1]
    outs = [(D, BF16), (D, BF16, True)] * 2
    return _rowwise(fn, [(x, D, 0)], [first[1], first[2], second[1], second[2]], [first[0], second[0]], outs, [], n_ex=n_ex, name=name)[0]


def _gated(dx, branch, gate):
    return (gate * dx).astype(BF16), _csum(dx * branch)


def _norm_mod_bwd(x, dh, dres, g, sc, *, n_ex, name, gated=None):
    n_rows = len(gated[0]) if gated is not None else 0

    def fn(xt, dht, drt, *rest):
        sct, gt = rest[-2], rest[-1]
        dht = dht.astype(F32)
        r = lax.rsqrt(jnp.mean(xt * xt, axis=-1, keepdims=True) + EPS)
        n = xt * r
        y = n * gt
        dy = dht * (1.0 + sct)
        dn = dy * gt
        dx = drt + r * (dn - n * jnp.mean(dn * n, axis=-1, keepdims=True))
        rows, sums = [dx], [_csum(dht), _csum(dht * y), _csum(dy * n)]
        if gated is not None:
            dbranch, dgate = gated[2](dx, *rest[:n_rows + 1])
            rows, sums = rows + [dbranch], sums + [dgate]
        return rows, sums
    D = x.shape[1]
    extra_rows, extra_vecs = (list(gated[0]), [gated[1]]) if gated is not None else ([], [])
    return _rowwise(fn, [(x, D, 0), (dh, D, 0), (dres, D, 0)] + extra_rows, extra_vecs + [sc], [g],
                    [(D, F32)] + ([(gated[3], BF16)] if gated is not None else []), [D, D, D] + ([D] if gated is not None else []),
                    n_ex=n_ex, name=name)


def _norm_mod_bwd_pair(x, dh1, dh2, dres, first, second, gated, *, n_ex, name):
    n_rows = len(gated[0])

    def fn(xt, d1, d2, drt, *rest):
        sc1, sc2, g1, g2 = rest[-4:]
        r = lax.rsqrt(jnp.mean(xt * xt, axis=-1, keepdims=True) + EPS)
        n = xt * r
        dx, sums = drt, []
        for dht, sct, gt in ((d1.astype(F32), sc1, g1), (d2.astype(F32), sc2, g2)):
            dy = dht * (1.0 + sct)
            dn = dy * gt
            dx = dx + r * (dn - n * jnp.mean(dn * n, axis=-1, keepdims=True))
            sums += [_csum(dht), _csum(dht * (n * gt)), _csum(dy * n)]
        dbranch, dgate = gated[2](dx, *rest[:n_rows + 1])
        return [dx, dbranch], sums + [dgate]
    D = x.shape[1]
    return _rowwise(fn, [(x, D, 0), (dh1, D, 0), (dh2, D, 0), (dres, D, 0)] + list(gated[0]), [gated[1], first[1], second[1]],
                    [first[0], second[0]], [(D, F32), (gated[3], BF16)], [D] * 7, n_ex=n_ex, name=name)


def _sigmoid(x):
    return 1.0 / (1.0 + jnp.exp(-x))


def _gelu(y):
    return 0.5 * y * (1.0 + jnp.tanh(0.7978845608028654 * (y + 0.044715 * y * y * y)))


def _gelu_grad(y):
    t = jnp.tanh(0.7978845608028654 * (y + 0.044715 * y * y * y))
    return 0.5 * (1.0 + t) + 0.5 * y * (1.0 - t * t) * 0.7978845608028654 * (1.0 + 3 * 0.044715 * y * y)


def _adamw_fn(w, g, m, v):
    m2 = ADAM_B1 * m + (1.0 - ADAM_B1) * g
    v2 = ADAM_B2 * v + (1.0 - ADAM_B2) * (g * g)
    m_hat = m2 / (1.0 - ADAM_B1 ** ADAM_STEP)
    v_hat = v2 / (1.0 - ADAM_B2 ** ADAM_STEP)
    delta = -ADAM_LR * (m_hat / (jnp.sqrt(v_hat) + ADAM_EPS) + ADAM_WD * w)
    return delta, m2, v2


def _adamw2d(w, g, m, v, *, name, g_roff=0, g_cb=0):
    R, W = w.shape

    def fn(wt, gt, mt, vt):
        d, m2, v2 = _adamw_fn(wt, gt, mt, vt)
        return [d, m2, v2, gt], []
    return _rowwise(fn, [(w, W, 0), (g, W, g_cb, g_roff), (m, W, 0), (v, W, 0)], [], [],
                    [(W, F32)] * 4, [], n_ex=1, name=name, tr=256)[0]


def _scan_tiles(re_ref, im_ref, cf, lane0, n_chunks, reverse, extra=None):
    L = SCAN_LANES
    lanes = pl.ds(lane0, L)
    A = [cf[i, :, lanes] for i in range(8)]
    shifts = (7, 6, 4) if reverse else (1, 2, 4)
    edge = 0 if reverse else 7

    U = SCAN_UNROLL
    n_groups = n_chunks // U

    def body(c, carry):
        first = ((n_groups - 1 - c) if reverse else c) * U
        rows = pl.ds(pl.multiple_of(first * 8, 8 * U), 8 * U)
        big_r, big_i = re_ref[rows, lanes], im_ref[rows, lanes]
        tiles = []
        for u in range(U):
            xr, xi = big_r[8 * u:8 * u + 8, :], big_i[8 * u:8 * u + 8, :]
            for idx, sft in enumerate(shifts):
                ar, ai = A[2 * idx], A[2 * idx + 1]
                rr, ri = pltpu.roll(xr, sft, 0), pltpu.roll(xi, sft, 0)
                xr, xi = xr + ar * rr - ai * ri, xi + ar * ri + ai * rr
            tiles.append((xr, xi))
        pr, pi = A[6], A[7]
        cr, ci = carry[0], carry[1]
        for u in (range(U - 1, -1, -1) if reverse else range(U)):
            xr, xi = tiles[u]
            xr, xi = xr + pr * cr - pi * ci, xi + pr * ci + pi * cr
            tiles[u] = (xr, xi)
            cr, ci = jnp.broadcast_to(xr[edge:edge + 1, :], (8, L)), jnp.broadcast_to(xi[edge:edge + 1, :], (8, L))
        re_ref[rows, lanes] = jnp.concatenate([t[0] for t in tiles], axis=0)
        im_ref[rows, lanes] = jnp.concatenate([t[1] for t in tiles], axis=0)
        return (cr, ci) if extra is None else (cr, ci) + extra(first, tiles, carry[2:])

    assert n_chunks % U == 0
    z = jnp.zeros((8, L), F32)
    init = (z, z) if extra is None else (z, z, z, z)
    return lax.fori_loop(0, n_groups, body, init)


def _s5_consts(ab_re, ab_im):
    ng = ab_re.shape[0] // GROUPS_PER_STEP
    ar, ai = ab_re.reshape(ng, 1, ST_LANES), ab_im.reshape(ng, 1, ST_LANES)

    def cmul(xr, xi, yr, yi):
        return xr * yr - xi * yi, xr * yi + xi * yr

    def build(ar, ai, reverse):
        pw = [(ar, ai)]
        for _ in range(7):
            pw.append(cmul(*pw[-1], ar, ai))
        row = jnp.arange(8).reshape(1, 8, 1)
        tiles = []
        for k in (1, 2, 4):
            keep = (row <= 7 - k) if reverse else (row >= k)
            tiles += [jnp.where(keep, pw[k - 1][0], 0.0), jnp.where(keep, pw[k - 1][1], 0.0)]
        order = [7 - r for r in range(8)] if reverse else list(range(8))
        tiles += [jnp.concatenate([pw[o][0] for o in order], axis=1), jnp.concatenate([pw[o][1] for o in order], axis=1)]
        return jnp.stack([jnp.broadcast_to(t, (ng, 8, ST_LANES)) for t in tiles], axis=1)

    return build(ar, ai, False), build(ar, -ai, True)


def _s5_blockdiag(bb_re, bb_im, c_re, c_im):
    G = bb_re.shape[0]
    ng = G // GROUPS_PER_STEP
    eye = jnp.eye(GROUPS_PER_STEP, dtype=F32)

    def wb(bb):
        return jnp.einsum("bgph,gk->bghkp", bb.reshape(ng, GROUPS_PER_STEP, S5_STATE, S5_GROUP), eye).reshape(ng, U_LANES, ST_LANES)

    def wc(cc):
        return jnp.einsum("bghp,gk->bkpgh", cc.reshape(ng, GROUPS_PER_STEP, S5_GROUP, S5_STATE), eye).reshape(ng, ST_LANES, U_LANES)

    Wb = jnp.concatenate([wb(bb_re), wb(bb_im)], axis=2).astype(BF16)
    Wc = jnp.concatenate([wc(c_re), -wc(c_im)], axis=1).astype(BF16)
    return Wb, Wc


def _s5_unblock(dWb, dWc):
    ng = dWb.shape[0]
    eye = jnp.eye(GROUPS_PER_STEP, dtype=F32)

    def ub(w):
        return jnp.einsum("bghkp,gk->bgph", w.reshape(ng, GROUPS_PER_STEP, S5_GROUP, GROUPS_PER_STEP, S5_STATE), eye).reshape(-1, S5_STATE, S5_GROUP)

    def uc(w):
        return jnp.einsum("bkpgh,gk->bghp", w.reshape(ng, GROUPS_PER_STEP, S5_STATE, GROUPS_PER_STEP, S5_GROUP), eye).reshape(-1, S5_GROUP, S5_STATE)

    return ub(dWb[:, :, :ST_LANES]), ub(dWb[:, :, ST_LANES:]), uc(dWc[:, :ST_LANES, :]), -uc(dWc[:, ST_LANES:, :])


def _s5_disc(a_re, a_im, log_dt, b_re, b_im):
    dt = jnp.exp(log_dt)[:, None]
    mag = jnp.exp(a_re * dt)
    ab_re = mag * jnp.cos(a_im * dt)
    ab_im = mag * jnp.sin(a_im * dt)
    den = a_re * a_re + a_im * a_im
    nr, ni = ab_re - 1, ab_im
    f_re = (nr * a_re + ni * a_im) / den
    f_im = (ni * a_re - nr * a_im) / den
    bb_re = f_re[..., None] * b_re - f_im[..., None] * b_im
    bb_im = f_re[..., None] * b_im + f_im[..., None] * b_re
    return ab_re, ab_im, bb_re, bb_im


ROW_CHUNK = 512


def _s5_fwd(u, Wb, Wc, cf, d, xsrc, *, n_ex, name):
    T, D = u.shape
    S = T // n_ex
    ng = D // U_LANES
    rc = min(ROW_CHUNK, S)

    def body(u_ref, wb_ref, wc_ref, cf_ref, d_ref, xsrc_ref, y_ref, gy_ref, gyt_ref, st_ref, xout_ref, re_s, im_s, *sems):
        step = pl.program_id(0) * ng + pl.program_id(1)
        exch = _ChipExchange(xsrc_ref, xout_ref, *sems, scatter=False)

        @pl.when(step == 0)
        def _():
            exch.start()

        for r in range(S // rc):
            rows = pl.ds(r * rc, rc)
            bu = jnp.dot(u_ref[rows, :].astype(BF16), wb_ref[...], preferred_element_type=F32)
            re_s[rows, :] = bu[:, :ST_LANES]
            im_s[rows, :] = bu[:, ST_LANES:]
        for l0 in range(0, ST_LANES, SCAN_LANES):
            _scan_tiles(re_s, im_s, cf_ref, l0, S // 8, False)
        for r in range(S // rc):
            rows = pl.ds(r * rc, rc)
            st = jnp.concatenate([re_s[rows, :], im_s[rows, :]], axis=1).astype(BF16)
            st_ref[rows, :] = st
            y = jnp.dot(st, wc_ref[...], preferred_element_type=F32) + d_ref[...] * u_ref[rows, :]
            y_ref[rows, :] = y
            gy = _gelu(y)
            gy_ref[rows, :] = gy.astype(BF16)
            gyt_ref[:, rows] = gy.T.astype(BF16)

        @pl.when(step == n_ex * ng - 1)
        def _():
            exch.wait()

    return pl.pallas_call(
        body, name=name, grid=(n_ex, ng),
        in_specs=[pl.BlockSpec((S, U_LANES), lambda e, g: (e, g)),
                  pl.BlockSpec((None, U_LANES, 2 * ST_LANES), lambda e, g: (g, 0, 0)),
                  pl.BlockSpec((None, 2 * ST_LANES, U_LANES), lambda e, g: (g, 0, 0)),
                  pl.BlockSpec((None, 8, 8, ST_LANES), lambda e, g: (g, 0, 0, 0)),
                  pl.BlockSpec((1, U_LANES), lambda e, g: (0, g)), ANY],
        out_specs=[pl.BlockSpec((S, U_LANES), lambda e, g: (e, g))] * 2 + [pl.BlockSpec((U_LANES, S), lambda e, g: (g, e)),
                   pl.BlockSpec((S, 2 * ST_LANES), lambda e, g: (e, g)), ANY],
        out_shape=[jax.ShapeDtypeStruct((T, D), F32), jax.ShapeDtypeStruct((T, D), BF16), jax.ShapeDtypeStruct((D, T), BF16),
                   jax.ShapeDtypeStruct((T, ng * 2 * ST_LANES), BF16), _ChipExchange.out_shape(xsrc, False)],
        scratch_shapes=[pltpu.VMEM((S, ST_LANES), F32)] * 2 + _ChipExchange.SCRATCH,
        compiler_params=_cp(("arbitrary", "arbitrary")),
    )(u, Wb, Wc, cf, d, xsrc)


def _s5_bwd(u, y, dgy, st, Wb, Wc, cr, d, xsrc, *, n_ex, name):
    T, D = u.shape
    S = T // n_ex
    ng = D // U_LANES
    rc = min(ROW_CHUNK, S)
    nch = S // 8
    grp = 8 * SCAN_UNROLL
    assert grp % 16 == 0

    def body(u_ref, y_ref, dgy_ref, st_ref, wb_ref, wc_ref, cr_ref, d_ref, xsrc_ref,
             du_ref, dwb_ref, dwc_ref, dab_ref, dd_ref, xout_ref, gr_s, gi_s, dy_s, *sems):
        e = pl.program_id(1)
        step = pl.program_id(0) * n_ex + e
        exch = _ChipExchange(xsrc_ref, xout_ref, *sems, scatter=True)

        @pl.when(step == 0)
        def _():
            exch.start()

        @pl.when(e == 0)
        def _():
            dwb_ref[...] = jnp.zeros_like(dwb_ref)
            dwc_ref[...] = jnp.zeros_like(dwc_ref)
            dab_ref[...] = jnp.zeros_like(dab_ref)
            dd_ref[...] = jnp.zeros_like(dd_ref)

        dd = jnp.zeros((1, U_LANES), F32)
        for r in range(S // rc):
            rows = pl.ds(r * rc, rc)
            ut = u_ref[rows, :]
            dy = dgy_ref[rows, :].astype(F32) * _gelu_grad(y_ref[rows, :])
            dy_s[rows, :] = dy
            dd = dd + _csum(dy * ut)
            go = lax.dot_general(dy.astype(BF16), wc_ref[...], (((1,), (1,)), ((), ())), preferred_element_type=F32)
            gr_s[rows, :] = go[:, :ST_LANES]
            gi_s[rows, :] = go[:, ST_LANES:]
        dd_ref[0:1, :] += dd
        row0 = lax.broadcasted_iota(jnp.int32, (8, SCAN_LANES), 0) == 0
        for l0 in range(0, ST_LANES, SCAN_LANES):
            lanes = pl.ds(l0, SCAN_LANES)

            def dab_group(first, tiles, acc, l0=l0):
                def states(r0, n, lane0):
                    return st_ref[pl.ds(pl.multiple_of(r0, 16), n), pl.ds(lane0, SCAN_LANES)].astype(F32)
                r0 = first * 8
                cur = states(r0, grp, l0), states(r0, grp, ST_LANES + l0)
                live = (first > 0).astype(F32)
                p0 = jnp.maximum(r0 - 16, 0)
                before = [states(p0, 16, l0)[8:16, :] * live, states(p0, 16, ST_LANES + l0)[8:16, :] * live]
                a_re, a_im = acc
                for t, (gr, gi) in enumerate(tiles):
                    here = [c[8 * t:8 * t + 8, :] for c in cur]
                    sr, si = [jnp.where(row0, pltpu.roll(b, 1, 0), pltpu.roll(h, 1, 0)) for b, h in zip(before, here)]
                    a_re, a_im = a_re + gr * sr + gi * si, a_im + gi * sr - gr * si
                    before = here
                return a_re, a_im

            res = _scan_tiles(gr_s, gi_s, cr_ref, l0, nch, True, extra=dab_group)
            dab_ref[0:1, lanes] += _csum(res[2])
            dab_ref[1:2, lanes] += _csum(res[3])
        for r in range(S // rc):
            rows = pl.ds(r * rc, rc)
            st = st_ref[rows, :]
            g = jnp.concatenate([gr_s[rows, :], gi_s[rows, :]], axis=1).astype(BF16)
            dyb = dy_s[rows, :].astype(BF16)
            dwc_ref[...] += lax.dot_general(st, dyb, (((0,), (0,)), ((), ())), preferred_element_type=F32)
            dwb_ref[...] += lax.dot_general(u_ref[rows, :].astype(BF16), g, (((0,), (0,)), ((), ())), preferred_element_type=F32)
            du = lax.dot_general(g, wb_ref[...], (((1,), (1,)), ((), ())), preferred_element_type=F32)
            du_ref[rows, :] = du + d_ref[...] * dy_s[rows, :]

        @pl.when(step == ng * n_ex - 1)
        def _():
            exch.wait()

    return pl.pallas_call(
        body, name=name, grid=(ng, n_ex),
        in_specs=[pl.BlockSpec((S, U_LANES), lambda g, e: (e, g))] * 3 + [
            pl.BlockSpec((S, 2 * ST_LANES), lambda g, e: (e, g)),
            pl.BlockSpec((None, U_LANES, 2 * ST_LANES), lambda g, e: (g, 0, 0)),
            pl.BlockSpec((None, 2 * ST_LANES, U_LANES), lambda g, e: (g, 0, 0)),
            pl.BlockSpec((None, 8, 8, ST_LANES), lambda g, e: (g, 0, 0, 0)),
            pl.BlockSpec((1, U_LANES), lambda g, e: (0, g)), ANY],
        out_specs=[pl.BlockSpec((S, U_LANES), lambda g, e: (e, g)),
                   pl.BlockSpec((None, U_LANES, 2 * ST_LANES), lambda g, e: (g, 0, 0)),
                   pl.BlockSpec((None, 2 * ST_LANES, U_LANES), lambda g, e: (g, 0, 0)),
                   pl.BlockSpec((None, 8, ST_LANES), lambda g, e: (g, 0, 0)),
                   pl.BlockSpec((None, 8, U_LANES), lambda g, e: (g, 0, 0)), ANY],
        out_shape=[jax.ShapeDtypeStruct((T, D), F32),
                   jax.ShapeDtypeStruct((ng, U_LANES, 2 * ST_LANES), F32),
                   jax.ShapeDtypeStruct((ng, 2 * ST_LANES, U_LANES), F32),
                   jax.ShapeDtypeStruct((ng, 8, ST_LANES), F32),
                   jax.ShapeDtypeStruct((ng, 8, U_LANES), F32), _ChipExchange.out_shape(xsrc, True)],
        scratch_shapes=[pltpu.VMEM((S, ST_LANES), F32)] * 2 + [pltpu.VMEM((S, U_LANES), F32)] + _ChipExchange.SCRATCH,
        compiler_params=_cp(("arbitrary", "arbitrary")),
    )(u, y, dgy, st, Wb, Wc, cr, d, xsrc)


TQ = 256
KW = 512
SUB = 128


def _head_masks():
    lane = lax.broadcasted_iota(jnp.int32, (1, 2 * HEAD_DIM), 1)
    m0 = (lane < HEAD_DIM).astype(F32)
    return m0, 1.0 - m0


def _head_norm(x, g, m0, m1):
    sq = x * x
    r0 = lax.rsqrt(jnp.sum(sq * m0, axis=-1, keepdims=True) / HEAD_DIM + EPS)
    r1 = lax.rsqrt(jnp.sum(sq * m1, axis=-1, keepdims=True) / HEAD_DIM + EPS)
    r = m0 * r0 + m1 * r1
    return x * r, r


def _head_norm_bwd(dy, n, r, g, m0, m1):
    dn = dy * g
    p = dn * n
    mean = (m0 * jnp.sum(p * m0, axis=-1, keepdims=True) + m1 * jnp.sum(p * m1, axis=-1, keepdims=True)) / HEAD_DIM
    return r * (dn - n * mean), _csum(dy * n)


def _pair_matrix(kind):
    r = lax.broadcasted_iota(jnp.int32, (2 * SUB, 2 * SUB), 0)
    c = lax.broadcasted_iota(jnp.int32, (2 * SUB, 2 * SUB), 1)
    same = (r < SUB) == (c < SUB)
    rel = {"after": r > c, "upto": r <= c, "before": r < c}[kind]
    return jnp.logical_and(same, rel).astype(BF16)


def _block_sums(x, mat, carry, reverse, terms=2):
    hi = x.astype(BF16)
    lo = (x - hi.astype(F32)).astype(BF16) if terms == 2 else None
    npair = x.shape[1] // (2 * SUB)
    parts = [None] * (2 * npair)
    for p in (range(npair - 1, -1, -1) if reverse else range(npair)):
        sl = slice(2 * SUB * p, 2 * SUB * (p + 1))
        loc = jnp.dot(hi[:, sl], mat, preferred_element_type=F32)
        if terms == 2:
            loc = loc + jnp.dot(lo[:, sl], mat, preferred_element_type=F32)
        for b in ((1, 0) if reverse else (0, 1)):
            k = 2 * p + b
            parts[k] = loc[:, SUB * b:SUB * (b + 1)] + carry
            carry = carry + jnp.sum(x[:, SUB * k:SUB * (k + 1)], axis=-1, keepdims=True)
    return jnp.concatenate(parts, axis=1), carry


def _sb_logits(z, mask):
    lp = jnp.minimum(z, 0.0) - jnp.log(1.0 + jnp.exp(-jnp.abs(z)))
    lf = lp - z
    if mask is not None:
        lf = jnp.where(mask, lf, 0.0)
    return lp, lf


def _causal_mask(row0, col0, kw):
    r = row0 + lax.broadcasted_iota(jnp.int32, (TQ, kw), 0)
    c = col0 + lax.broadcasted_iota(jnp.int32, (TQ, kw), 1)
    return c < r


def _transposed_windows(x, ref):
    for w in range(x.shape[0] // KW):
        ref[w] = x[w * KW:(w + 1) * KW, :].T.astype(BF16)


def _attn_fwd(q, kv, qg, kg, xsrc, *, n_ex, name):
    T, D = q.shape
    S = T // n_ex
    nhp = D // (2 * HEAD_DIM)
    nq = S // TQ
    scale = 1.0 / math.sqrt(HEAD_DIM)

    def body(q_ref, k_ref, v_ref, qg_ref, kg_ref, xsrc_ref, o_ref, tot_ref, ot_ref, xout_ref, kT_s, qm_s, vm_s, *sems):
        step = pl.program_id(0) * nhp + pl.program_id(1)
        exch = _ChipExchange(xsrc_ref, xout_ref, *sems, scatter=False)

        @pl.when(step == 0)
        def _():
            exch.start()

        m0, m1 = _head_masks()
        qn, _ = _head_norm(q_ref[...], None, m0, m1)
        qn = qn * (qg_ref[...] * scale)
        kn, _ = _head_norm(k_ref[...], None, m0, m1)
        _transposed_windows(kn * kg_ref[...], kT_s)
        v = v_ref[...]
        for h, m in enumerate((m0, m1)):
            qm_s[h] = (qn * m).astype(BF16)
            vm_s[h] = (v * m).astype(BF16)
        u_after = _pair_matrix("after")

        def window(rows, win, st, mask, kw):
            keys = pl.ds(pl.multiple_of(win * KW, KW), kw)
            zs = [jnp.dot(qm_s[h, rows, :], kT_s[win, :, :kw], preferred_element_type=F32) for h in range(2)]
            lg = [_sb_logits(zs[h], mask) for h in range(2)]
            sums = [_block_sums(lg[h][1], u_after, st[2 * h], True) for h in range(2)]
            out = ()
            for h in range(2):
                w = jnp.exp(lg[h][0] + sums[h][0])
                if mask is not None:
                    w = jnp.where(mask, w, 0.0)
                out += (sums[h][1], st[2 * h + 1] + jnp.dot(w.astype(BF16), vm_s[h, keys, :], preferred_element_type=F32))
            return out

        def qtile(iq, last, kw):
            rows = pl.ds(pl.multiple_of(iq * TQ, TQ), TQ)
            mask = _causal_mask(iq * TQ, last * KW, kw)
            z1, zq = jnp.zeros((TQ, 1), F32), jnp.zeros((TQ, 2 * HEAD_DIM), F32)
            st = window(rows, last, (z1, zq, z1, zq), mask, kw)
            st = lax.fori_loop(0, last, lambda jj, st: window(rows, last - 1 - jj, st, None, KW), st)
            o_ref[rows, :] = st[1] + st[3]
            tot_ref[rows, :] = st[0] * m0 + st[2] * m1

        def qtiles_of_window(a, _):
            for sub in range(KW // TQ):
                qtile(a * (KW // TQ) + sub, a, (sub + 1) * TQ)
            return 0

        lax.fori_loop(0, S // KW, qtiles_of_window, 0)
        ot_ref[...] = o_ref[...].T.astype(BF16)

        @pl.when(step == n_ex * nhp - 1)
        def _():
            exch.wait()

    assert S % KW == 0 and KW % TQ == 0
    nwin = S // KW
    blk = (S, 2 * HEAD_DIM)
    return pl.pallas_call(
        body, name=name, grid=(n_ex, nhp),
        in_specs=[pl.BlockSpec(blk, lambda e, h: (e, h)), pl.BlockSpec(blk, lambda e, h: (e, h)),
                  pl.BlockSpec(blk, lambda e, h: (e, h + nhp)),
                  pl.BlockSpec((1, 2 * HEAD_DIM), lambda e, h: (0, 0)), pl.BlockSpec((1, 2 * HEAD_DIM), lambda e, h: (0, 0)), ANY],
        out_specs=[pl.BlockSpec(blk, lambda e, h: (e, h))] * 2 + [pl.BlockSpec((2 * HEAD_DIM, S), lambda e, h: (h, e)), ANY],
        out_shape=[jax.ShapeDtypeStruct((T, D), F32)] * 2 + [jax.ShapeDtypeStruct((D, T), BF16), _ChipExchange.out_shape(xsrc, False)],
        scratch_shapes=[pltpu.VMEM((nwin, 2 * HEAD_DIM, KW), BF16), pltpu.VMEM((2,) + blk, BF16), pltpu.VMEM((2,) + blk, BF16)]
        + _ChipExchange.SCRATCH,
        compiler_params=_cp(("arbitrary", "arbitrary")),
    )(q, kv, kv, qg, kg, xsrc)


def _attn_bwd(q, kv, tot, do, qg, kg, *, n_ex, name):
    T, D = q.shape
    S = T // n_ex
    nhp = D // (2 * HEAD_DIM)
    nq = S // TQ
    scale = 1.0 / math.sqrt(HEAD_DIM)

    def body(q_ref, k_ref, v_ref, tot_ref, do_ref, qg_ref, kg_ref, dq_ref, dk_ref, dv_ref, dqg_ref, dkg_ref,
             kT_s, vT_s, km_s, qm_s, dom_s, dqn_s, dkT_s, dvT_s):
        m0, m1 = _head_masks()
        qn, qr = _head_norm(q_ref[...], None, m0, m1)
        kn, kr = _head_norm(k_ref[...], None, m0, m1)
        qs = qn * (qg_ref[...] * scale)
        kk = kn * kg_ref[...]
        _transposed_windows(kk, kT_s)
        _transposed_windows(v_ref[...], vT_s)
        do = do_ref[...]
        for h, m in enumerate((m0, m1)):
            qm_s[h] = (qs * m).astype(BF16)
            km_s[h] = (kk * m).astype(BF16)
            dom_s[h] = (do * m).astype(BF16)
        dkT_s[...] = jnp.zeros_like(dkT_s)
        dvT_s[...] = jnp.zeros_like(dvT_s)
        u_upto, u_before = _pair_matrix("upto"), _pair_matrix("before")

        def both(inv, win, st, mask, kw):
            keys = pl.ds(pl.multiple_of(win * KW, KW), kw)
            lg = [_sb_logits(jnp.dot(inv[h][0], kT_s[win, :, :kw], preferred_element_type=F32), mask) for h in range(2)]
            s_lf = [_block_sums(lg[h][1], u_upto, st[3 * h], False) for h in range(2)]
            ws, ews = [], []
            for h in range(2):
                w = jnp.exp(lg[h][0] - s_lf[h][0])
                if mask is not None:
                    w = jnp.where(mask, w, 0.0)
                ws.append(w)
                ews.append(jnp.dot(inv[h][2], vT_s[win, :, :kw], preferred_element_type=F32) * w)
            s_e = [_block_sums(ews[h], u_before, st[3 * h + 1], False, terms=1) for h in range(2)]
            out, dk, dv = (), None, None
            for h in range(2):
                sig = jnp.exp(lg[h][0])
                dz = ews[h] - sig * (ews[h] + s_e[h][0])
                if mask is not None:
                    dz = jnp.where(mask, dz, 0.0)
                dzb = dz.astype(BF16)
                out += (s_lf[h][1], s_e[h][1], st[3 * h + 2] + jnp.dot(dzb, km_s[h, keys, :], preferred_element_type=F32))
                dkh = jnp.dot(inv[h][1], dzb, preferred_element_type=F32)
                dvh = jnp.dot(inv[h][3], ws[h].astype(BF16), preferred_element_type=F32)
                dk, dv = (dkh, dvh) if h == 0 else (dk + dkh, dv + dvh)
            dkT_s[win, :, :kw] += dk
            dvT_s[win, :, :kw] += dv
            return out

        def qtile(iq, last, kw):
            rows = pl.ds(pl.multiple_of(iq * TQ, TQ), TQ)
            mask = _causal_mask(iq * TQ, last * KW, kw)
            tt = tot_ref[rows, :]
            inv, neg_total = [], []
            for h, m in enumerate((m0, m1)):
                qh, doh = qm_s[h, rows, :], dom_s[h, rows, :]
                neg_total.append(jnp.sum(tt * m, axis=-1, keepdims=True) * (-1.0 / HEAD_DIM))
                inv.append((qh, qh.astype(F32).T.astype(BF16), doh, doh.astype(F32).T.astype(BF16)))

            z1, zq = jnp.zeros((TQ, 1), F32), jnp.zeros((TQ, 2 * HEAD_DIM), F32)
            st = lax.fori_loop(0, last, lambda win, st: both(inv, win, st, None, KW), (neg_total[0], z1, zq, neg_total[1], z1, zq))
            st = both(inv, last, st, mask, kw)
            dqn_s[rows, :] = st[2] + st[5]

        def qtiles_of_window(a, _):
            for sub in range(KW // TQ):
                qtile(a * (KW // TQ) + sub, a, (sub + 1) * TQ)
            return 0

        lax.fori_loop(0, S // KW, qtiles_of_window, 0)
        dkn = jnp.concatenate([dkT_s[w].T for w in range(nwin)], axis=0)
        dq, dqg = _head_norm_bwd(dqn_s[...] * scale, qn, qr, qg_ref[...], m0, m1)
        dk, dkg = _head_norm_bwd(dkn, kn, kr, kg_ref[...], m0, m1)
        dq_ref[...] = dq
        dk_ref[...] = dk
        dv_ref[...] = jnp.concatenate([dvT_s[w].T for w in range(nwin)], axis=0)
        dqg_ref[...] = dqg
        dkg_ref[...] = dkg

    assert S % KW == 0 and KW % TQ == 0
    nwin = S // KW
    blk = (S, 2 * HEAD_DIM)
    tblk = (nwin, 2 * HEAD_DIM, KW)
    gblk = (None, None, 1, 2 * HEAD_DIM)
    dq, dk, dv, dqg, dkg = pl.pallas_call(
        body, name=name, grid=(n_ex, nhp),
        in_specs=[pl.BlockSpec(blk, lambda e, h: (e, h)), pl.BlockSpec(blk, lambda e, h: (e, h)),
                  pl.BlockSpec(blk, lambda e, h: (e, h + nhp)),
                  pl.BlockSpec(blk, lambda e, h: (e, h)), pl.BlockSpec(blk, lambda e, h: (e, h)),
                  pl.BlockSpec((1, 2 * HEAD_DIM), lambda e, h: (0, 0)), pl.BlockSpec((1, 2 * HEAD_DIM), lambda e, h: (0, 0))],
        out_specs=[pl.BlockSpec(blk, lambda e, h: (e, h))] * 3 + [pl.BlockSpec(gblk, lambda e, h: (e, h, 0, 0))] * 2,
        out_shape=[jax.ShapeDtypeStruct((T, D), F32)] * 3 + [jax.ShapeDtypeStruct((n_ex, nhp, 1, 2 * HEAD_DIM), F32)] * 2,
        scratch_shapes=[pltpu.VMEM(tblk, BF16), pltpu.VMEM(tblk, BF16),
                        pltpu.VMEM((2,) + blk, BF16), pltpu.VMEM((2,) + blk, BF16), pltpu.VMEM((2,) + blk, BF16),
                        pltpu.VMEM(blk, F32), pltpu.VMEM(tblk, F32), pltpu.VMEM(tblk, F32)],
        compiler_params=_cp(("parallel", "parallel")),
    )(q, kv, kv, tot, do, qg, kg)
    return dq, dk, dv, dqg, dkg


def _place():
    return lax.axis_index("x"), lax.axis_index("y"), lax.axis_index("c")


def _all_gather8(x_shard, *, name):
    m_per, n = x_shard.shape

    def body(x_ref, out_ref, send_sems, recv_sems, local_sem):
        x, y, c = _place()
        me, sibling = (x, y, c), (x, y, 1 - c)
        chips = [(1 - x, y), (x, 1 - y), (1 - x, 1 - y)]

        def rows(px, py, pc):
            return out_ref.at[pl.ds((4 * px + 2 * py + pc) * m_per, m_per), :]

        def copy(k, block, to, src=None):
            return pltpu.make_async_remote_copy(
                src_ref=rows(*block) if src is None else src, dst_ref=rows(*block),
                send_sem=send_sems.at[k], recv_sem=recv_sems.at[k], device_id=to, device_id_type=MESH)

        mine = pltpu.make_async_copy(x_ref, rows(*me), local_sem)
        mine.start()
        first = [copy(0, me, sibling, src=x_ref)]
        first += [copy(1 + j, me, (*chip, c), src=x_ref) for j, chip in enumerate(chips)]
        for cp in first:
            cp.start()
        passed = [copy(4 + j, (*chip, c), sibling) for j, chip in enumerate(chips)]
        for j, chip in enumerate(chips):
            copy(1 + j, (*chip, c), me).wait_recv()
            passed[j].start()
        copy(0, sibling, me).wait_recv()
        for j, chip in enumerate(chips):
            copy(4 + j, (*chip, 1 - c), me).wait_recv()
        for cp in first + passed:
            cp.wait_send()
        mine.wait()

    return pl.pallas_call(
        body, name=name, out_shape=jax.ShapeDtypeStruct((8 * m_per, n), x_shard.dtype),
        in_specs=[pl.BlockSpec(memory_space=pltpu.VMEM)], out_specs=pl.BlockSpec(memory_space=pltpu.VMEM),
        scratch_shapes=[pltpu.SemaphoreType.DMA((7,)), pltpu.SemaphoreType.DMA((7,)), pltpu.SemaphoreType.DMA],
        compiler_params=pltpu.CompilerParams(vmem_limit_bytes=VMEM_LIMIT),
    )(x_shard)


def _sibling_sum_half(x, *, name):
    R, C = x.shape
    half = R // 2
    assert half % 16 == 0

    def body(x_ref, o_ref, theirs, send_sem, recv_sem):
        px, py, pc = _place()
        cp = pltpu.make_async_remote_copy(src_ref=x_ref, dst_ref=theirs, send_sem=send_sem, recv_sem=recv_sem,
                                          device_id=(px, py, 1 - pc), device_id_type=MESH)
        cp.start()
        cp.wait()
        rows = pl.ds(pl.multiple_of(pc * half, 8), half)
        o_ref[...] = (x_ref[rows, :] + theirs[rows, :]).astype(BF16)

    return pl.pallas_call(
        body, name=name, out_shape=jax.ShapeDtypeStruct((half, C), BF16),
        in_specs=[pl.BlockSpec(memory_space=pltpu.VMEM)], out_specs=pl.BlockSpec(memory_space=pltpu.VMEM),
        scratch_shapes=[pltpu.VMEM((R, C), x.dtype), pltpu.SemaphoreType.DMA, pltpu.SemaphoreType.DMA],
        compiler_params=pltpu.CompilerParams(vmem_limit_bytes=VMEM_LIMIT),
    )(x)


def _sum_blocks(x, n, *, name):
    R = x.shape[0] // n

    def body(x_ref, o_ref):
        acc = x_ref[pl.ds(0, R), :].astype(F32)
        for k in range(1, n):
            acc = acc + x_ref[pl.ds(k * R, R), :].astype(F32)
        o_ref[...] = acc

    return pl.pallas_call(body, name=name, out_shape=jax.ShapeDtypeStruct((R, x.shape[1]), F32),
                          compiler_params=pltpu.CompilerParams(vmem_limit_bytes=VMEM_LIMIT))(x)


def _colsum(x, *, name):
    def body(x_ref, o_ref):
        o_ref[...] = jnp.sum(x_ref[...], axis=0, keepdims=True)
    return pl.pallas_call(body, name=name, out_shape=jax.ShapeDtypeStruct((1, x.shape[1]), x.dtype))(x)


ANY = pl.BlockSpec(memory_space=pl.ANY)


class _ChipExchange:
    SCRATCH = [pltpu.SemaphoreType.DMA((3,)), pltpu.SemaphoreType.DMA((3,)), pltpu.SemaphoreType.DMA]

    @staticmethod
    def out_shape(src, scatter):
        return jax.ShapeDtypeStruct(((4,) + tuple(src.shape[1:])) if scatter else ((4, 2) + tuple(src.shape[1:])), src.dtype)

    def __init__(self, src_ref, out_ref, send_sems, recv_sems, local_sem, scatter):
        x, y, c = _place()
        myj = 2 * x + y
        chips = [(1 - x, y), (x, 1 - y), (1 - x, 1 - y)]

        def slot(j):
            return out_ref.at[j] if scatter else out_ref.at[j, c]

        def piece(j):
            return src_ref.at[j] if scatter else src_ref.at[c]

        self.mine = pltpu.make_async_copy(piece(myj), slot(myj), local_sem)
        self.sends = [pltpu.make_async_remote_copy(
            src_ref=piece(2 * cx + cy), dst_ref=slot(myj), send_sem=send_sems.at[k], recv_sem=recv_sems.at[k],
            device_id=(cx, cy, c), device_id_type=MESH) for k, (cx, cy) in enumerate(chips)]
        self.recvs = [pltpu.make_async_remote_copy(
            src_ref=slot(2 * cx + cy), dst_ref=slot(2 * cx + cy), send_sem=send_sems.at[k], recv_sem=recv_sems.at[k],
            device_id=(cx, cy, c), device_id_type=MESH) for k, (cx, cy) in enumerate(chips)]

    def start(self):
        self.mine.start()
        for cp in self.sends:
            cp.start()

    def wait(self):
        for cp in self.recvs:
            cp.wait_recv()
        for cp in self.sends:
            cp.wait_send()
        self.mine.wait()


def _sibling_fill(buf, *, axis, name):
    def half(ref, h):
        return ref.at[h] if axis == 0 else ref.at[:, h]

    def body(in_ref, out_ref, send_sem, recv_sem):
        x, y, c = _place()
        cp = pltpu.make_async_remote_copy(src_ref=half(out_ref, c), dst_ref=half(out_ref, c), send_sem=send_sem, recv_sem=recv_sem,
                                          device_id=(x, y, 1 - c), device_id_type=MESH)
        cp.start()
        pltpu.make_async_remote_copy(src_ref=half(out_ref, 1 - c), dst_ref=half(out_ref, 1 - c), send_sem=send_sem, recv_sem=recv_sem,
                                     device_id=(x, y, 1 - c), device_id_type=MESH).wait_recv()
        cp.wait_send()

    return pl.pallas_call(
        body, name=name, out_shape=jax.ShapeDtypeStruct(buf.shape, buf.dtype), in_specs=[ANY], out_specs=ANY,
        input_output_aliases={0: 0}, scratch_shapes=[pltpu.SemaphoreType.DMA, pltpu.SemaphoreType.DMA],
    )(buf)


def _sibling_swap_half(g, *, name):
    def body(g_ref, out_ref, send_sem, recv_sem):
        x, y, c = _place()
        cp = pltpu.make_async_remote_copy(src_ref=g_ref.at[:, 1 - c], dst_ref=out_ref, send_sem=send_sem, recv_sem=recv_sem,
                                          device_id=(x, y, 1 - c), device_id_type=MESH)
        cp.start()
        cp.wait()

    return pl.pallas_call(
        body, name=name, out_shape=jax.ShapeDtypeStruct((g.shape[0],) + g.shape[2:], g.dtype), in_specs=[ANY], out_specs=ANY,
        scratch_shapes=[pltpu.SemaphoreType.DMA, pltpu.SemaphoreType.DMA],
    )(g)


def _add_my_half(g, b, cidx, *, name, tr=1024):
    n, _, R, C = g.shape
    tr = max(t for t in range(16, tr + 1, 16) if R % t == 0)

    def body(c_ref, g_ref, b_ref, o_ref):
        o_ref[...] = (g_ref[...] + b_ref[...]).astype(o_ref.dtype)

    return pl.pallas_call(
        body, name=name, out_shape=jax.ShapeDtypeStruct((n, R, C), BF16),
        grid_spec=pltpu.PrefetchScalarGridSpec(
            num_scalar_prefetch=1, grid=(n, R // tr),
            in_specs=[pl.BlockSpec((None, None, tr, C), lambda j, i, c: (j, c[0], i, 0)),
                      pl.BlockSpec((None, tr, C), lambda j, i, c: (j, i, 0))],
            out_specs=pl.BlockSpec((None, tr, C), lambda j, i, c: (j, i, 0))),
        compiler_params=_cp(("parallel", "parallel")),
    )(cidx, g, b)


def _sum4_into_half(q, cidx, *, name, tr=1024):
    _, R, C = q.shape
    tr = max(t for t in range(16, tr + 1, 16) if R % t == 0)

    def body(c_ref, q_ref, o_ref):
        o_ref[...] = ((q_ref[0].astype(F32) + q_ref[1].astype(F32)) + q_ref[2].astype(F32)) + q_ref[3].astype(F32)

    return pl.pallas_call(
        body, name=name, out_shape=jax.ShapeDtypeStruct((2, R, C), F32),
        grid_spec=pltpu.PrefetchScalarGridSpec(
            num_scalar_prefetch=1, grid=(R // tr,),
            in_specs=[pl.BlockSpec((4, tr, C), lambda i, c: (0, i, 0))],
            out_specs=pl.BlockSpec((None, tr, C), lambda i, c: (c[0], i, 0))),
        compiler_params=_cp(("parallel",)),
    )(cidx, q)


def _pack_rows(parts, width=1024, row_multiple=8):
    rows, spans, r0 = [], [], 0
    for p in parts:
        n = p.size
        nr = 8 * (-(-n // (8 * width)))
        flat = p.reshape(-1)
        if nr * width != n:
            flat = jnp.pad(flat, (0, nr * width - n))
        rows.append(flat.reshape(nr, width))
        spans.append((r0, nr, n, p.shape))
        r0 += nr
    if r0 % row_multiple:
        rows.append(jnp.zeros((row_multiple - r0 % row_multiple, width), parts[0].dtype))
    return jnp.concatenate(rows, axis=0), spans


def _unpack_rows(buf, spans):
    return [buf[r0:r0 + nr].reshape(-1)[:n].reshape(shape) for (r0, nr, n, shape) in spans]


def kernel(x, c, ada_w, ada_b, mix_norm_g, mlp_norm_g, mlp_w1, mlp_w2, s5_a_re, s5_a_im, s5_log_dt, s5_b_re, s5_b_im, s5_c_re, s5_c_im, s5_d, s5_w_glu, kv_ada_w, kv_ada_b, kv_norm_g, w_kv, k_norm_g, sb_w_q, q_norm_g, sb_w_o, loss_target, m_ada_w, m_ada_b, m_mix_norm_g, m_mlp_norm_g, m_mlp_w1, m_mlp_w2, m_s5_a_re, m_s5_a_im, m_s5_log_dt, m_s5_b_re, m_s5_b_im, m_s5_c_re, m_s5_c_im, m_s5_d, m_s5_w_glu, m_kv_ada_w, m_kv_ada_b, m_kv_norm_g, m_w_kv, m_k_norm_g, m_sb_w_q, m_q_norm_g, m_sb_w_o, v_ada_w, v_ada_b, v_mix_norm_g, v_mlp_norm_g, v_mlp_w1, v_mlp_w2, v_s5_a_re, v_s5_a_im, v_s5_log_dt, v_s5_b_re, v_s5_b_im, v_s5_c_re, v_s5_c_im, v_s5_d, v_s5_w_glu, v_kv_ada_w, v_kv_ada_b, v_kv_norm_g, v_w_kv, v_k_norm_g, v_sb_w_q, v_q_norm_g, v_sb_w_o):
    E, S, D = x.shape
    T = E * S
    FF = 4 * D
    NB = 8 * E
    px, py, pc = _place()
    chip = 2 * px + py
    dev = 4 * px + 2 * py + pc
    cidx = jnp.reshape(pc, (1,)).astype(jnp.int32)
    x0 = x.reshape(T, D)
    tgt = loss_target.reshape(T, D)

    nc_rows, nd = c.size // 128, s5_d.size // 128
    cd = jnp.concatenate([c.reshape(nc_rows, 128), jnp.pad(s5_d.reshape(nd, 128), ((0, 8 - nd), (0, 0)))], axis=0)
    cd_all = _all_gather8(cd, name="ag_c_d").reshape(8, nc_rows + 8, 128)
    c_all = cd_all[:, :nc_rows].reshape(NB, D)
    d_full = cd_all.reshape(4, 2, nc_rows + 8, 128)[:, 0, nc_rows:nc_rows + nd].reshape(1, D)
    sc_all = (c_all * _sigmoid(c_all)).astype(BF16)
    wa = ada_w.shape[2]
    wk = kv_ada_w.shape[1]
    m_sh = jnp.concatenate([_mm(sc_all, _Layer(ada_w, 0), "nn", name="ada0", tn=256),
                            _mm(sc_all, _Layer(ada_w, 1), "nn", name="ada1", tn=256),
                            _mm(sc_all, kv_ada_w, "nn", name="ada_kv", tn=256)], axis=1)
    m_all = _all_gather8(m_sh, name="ag_m").reshape(4, 2, NB, 2 * wa + wk)[:, 0]
    mods = []
    for l in range(2):
        full = jnp.transpose(m_all[:, :, l * wa:(l + 1) * wa], (1, 0, 2)).reshape(NB, 6 * D) + ada_b[l]
        mine = lax.dynamic_slice_in_dim(full, E * dev, E, axis=0)
        mods.append([mine[:, i * D:(i + 1) * D].reshape(E, 1, D) for i in range(6)])
    full = jnp.transpose(m_all[:, :, 2 * wa:], (1, 0, 2)).reshape(NB, 2 * D) + kv_ada_b
    mine = lax.dynamic_slice_in_dim(full, E * dev, E, axis=0)
    kv_sh, kv_sc = [mine[:, i * D:(i + 1) * D].reshape(E, 1, D) for i in range(2)]

    wpack_a = jnp.concatenate([mlp_w1[0], mlp_w2[0], jnp.concatenate([s5_w_glu[0], w_kv], axis=1), sb_w_q[0]], axis=0).astype(BF16)
    wpack_b = jnp.concatenate([mlp_w1[1], mlp_w2[1], sb_w_o[0]], axis=0).astype(BF16)
    RA, RB = wpack_a.shape[0], wpack_b.shape[0]
    RW = RA + RB

    tm = min(2048, S)
    tm_res = min(1024, S)
    gbuf = [jax.ShapeDtypeStruct((4, RW, D), F32)]

    def grad_mm(act, dout, kind, roff, nr, c0, nc, name, transposed=False):
        gbuf[0] = _mm(act, dout, "nn" if transposed else "tn", name=name, tm=1024, tk=2048,
                      into=_Sharded(gbuf[0], kind, roff, nr, c0, nc))

    def mlp_fwd(xa, l, mod, normed=None):
        sh_m, sc_m, g_m = mod[3], mod[4], mod[5]
        h, h_t = normed if normed is not None else _norm_mod_fwd(
            xa, mlp_norm_g[l:l + 1], sh_m, sc_m, n_ex=E, out_dtype=BF16, name=f"mlp_norm{l}", with_transpose=True)

        def relu_sq(acc):
            ra = jnp.maximum(acc, 0.0)
            return ra * ra, ra
        r, ra = _mm(h, W1[l], "nn", name=f"mlp_up{l}", out_dtypes=(BF16, BF16), tm=tm, epilogue=relu_sq)
        xb, ff = _mm(r, W2[l], "nn", name=f"mlp_down{l}", out_dtypes=(F32, F32), tm=tm_res,
                     extras=[_mn_extra(xa), _vec_extra(g_m, S)],
                     epilogue=lambda acc, xat, gt: (xat + gt * acc, acc))
        return xb, (h_t, r, ra, ff)

    def mlp_bwd(dxb, dff, xa, l, mod, saved, gated=None):
        sc_m = mod[4]
        h_t, r, ra, _ = saved
        da = _mm(dff, W2[l], "nt", name=f"mlp_down_dx{l}", out_dtypes=(BF16,), tm=tm, extras=[_mn_extra(ra)],
                 epilogue=lambda acc, rat: (acc * (2.0 * rat.astype(F32)),))
        grad_mm(r, dff, "rows", (2 + l) * D, D, 0, D, f"mlp_down_dw{l}")
        dh = _mm(da, W1[l], "nt", name=f"mlp_up_dx{l}", tm=tm)
        grad_mm(h_t, da, "cols", l * D, D, 0, D, f"mlp_up_dw{l}", transposed=True)
        return _norm_mod_bwd(xa, dh, dxb, mlp_norm_g[l:l + 1], sc_m, n_ex=E, name=f"mlp_norm_bwd{l}", gated=gated)

    ab_re, ab_im, bb_re, bb_im = _s5_disc(s5_a_re[0], s5_a_im[0], s5_log_dt[0], s5_b_re[0], s5_b_im[0])
    cf, cr = _s5_consts(ab_re, ab_im)
    Wb, Wc = _s5_blockdiag(bb_re, bb_im, s5_c_re[0], s5_c_im[0])
    ng = D // U_LANES

    mod0, mod1 = mods
    h0 = _norm_mod_fwd(x0, mix_norm_g[0:1], mod0[0], mod0[1], n_ex=E, out_dtype=F32, name="mix_norm0")
    y, gy, gy_t, s5_states, wfull_a = _s5_fwd(h0, Wb, Wc, cf, d_full, wpack_a.reshape(2, RA // 2, D), n_ex=E, name="s5_fwd")
    wfull_a = _sibling_fill(wfull_a, axis=1, name="wgather_a_d2d").reshape(4, RA, D)

    W1 = [_Sharded(wfull_a, "cols", 0, D, 0, D), None]
    W2 = [_Sharded(wfull_a, "rows", D, D, 0, D), None]
    Wglu = _Sharded(wfull_a, "cols", 2 * D, D, 0, D // 2)
    Wkv = _Sharded(wfull_a, "cols", 2 * D, D, D // 2, D // 2)
    Wq = _Sharded(wfull_a, "rows", 3 * D, D // 4, 0, D)
    vg = _mm(gy, Wglu, "nn", name="glu_up", tm=tm)
    def glu_gate_and_norm(v, g, xt, ga, sh, sc, gain):
        x1t = xt + ga * (v * _sigmoid(g))
        n = x1t * lax.rsqrt(jnp.mean(x1t * x1t, axis=-1, keepdims=True) + EPS)
        h = (n * gain) * (1.0 + sc) + sh
        return [x1t, h, h], []
    (x1, hm0, hm0_t), _ = _rowwise(glu_gate_and_norm, [(vg, D, 0), (vg, D, 1), (x0, D, 0)], [mod0[2], mod0[3], mod0[4]],
                                   [mlp_norm_g[0:1]], [(D, F32), (D, BF16), (D, BF16, True)], [], n_ex=E, name="glu_gate_norm")
    x2, saved_mlp0 = mlp_fwd(x1, 0, mod0, normed=(hm0, hm0_t))

    hkv, hkv_t, h1, h1_t = _norm_mod_fwd_pair(x2, (kv_norm_g.reshape(1, D), kv_sh, kv_sc), (mix_norm_g[1:2], mod1[0], mod1[1]),
                                              n_ex=E, name="kv_mix_norm")
    kvf = _mm(hkv, Wkv, "nn", name="kv_proj", tm=tm)
    qf = _mm(h1, Wq, "nn", name="q_proj", tm=tm)
    qg2 = jnp.tile(q_norm_g.reshape(1, HEAD_DIM), (1, 2))
    kg2 = jnp.tile(k_norm_g.reshape(1, HEAD_DIM), (1, 2))
    o, lf_tot, o_t, wfull_b = _attn_fwd(qf, kvf, qg2, kg2, wpack_b.reshape(2, RB // 2, D), n_ex=E, name="attn_fwd")
    wfull_b = _sibling_fill(wfull_b, axis=1, name="wgather_b_d2d").reshape(4, RB, D)
    W1[1] = _Sharded(wfull_b, "cols", 0, D, 0, D)
    W2[1] = _Sharded(wfull_b, "rows", D, D, 0, D)
    Wo = _Sharded(wfull_b, "rows", 2 * D, D // 4, 0, D)
    x3, mix1 = _mm(o, Wo, "nn", name="o_proj", out_dtypes=(F32, F32), tm=tm_res,
                   extras=[_mn_extra(x2), _vec_extra(mod1[2], S)],
                   epilogue=lambda acc, xat, gt: (xat + gt * acc, acc))
    x4, saved_mlp1 = mlp_fwd(x3, 1, mod1)

    def loss_fn(xt, tt, fft, gmt):
        dx = (xt - tt) * (1.0 / D)
        dff, dgm = _gated(dx, fft, gmt)
        return [dx, dff], [_csum(jnp.square(xt - tt)) * (0.5 / D), dgm]
    (dx4, dff1), (lsum, dgm1) = _rowwise(loss_fn, [(x4, D, 0), (tgt, D, 0), (saved_mlp1[3], D, 0)], [mod1[5]], [],
                                         [(D, F32), (D, BF16)], [D, D], n_ex=E, name="loss")
    loss = lax.psum(jnp.sum(lsum), ("x", "y", "c"))

    (dx3, dmix1), (dsh_m1, dsc_m1, dg_mlp1, dga1) = mlp_bwd(dx4, dff1, x3, 1, mod1, saved_mlp1, gated=([(mix1, D, 0)], mod1[2], _gated, D))
    do = _mm(dmix1, Wo, "nt", name="o_proj_dx", tm=tm)
    grad_mm(o_t, dmix1, "rows", 5 * D + D // 4, D // 4, 0, D, "o_proj_dw", transposed=True)
    dq, dk, dv, dqg, dkg = _attn_bwd(qf, kvf, lf_tot, do, qg2, kg2, n_ex=E, name="attn_bwd")
    dh1 = _mm(dq, Wq, "nt", name="q_proj_dx", tm=tm)
    grad_mm(h1_t, dq, "rows", 5 * D, D // 4, 0, D, "q_proj_dw", transposed=True)
    dkv = jnp.concatenate([dk, dv], axis=1)
    dhkv = _mm(dkv, Wkv, "nt", name="kv_proj_dx", tm=tm)
    grad_mm(hkv_t, dkv, "cols", 4 * D, D, D // 2, D // 2, "kv_proj_dw", transposed=True)
    (dx2, dff0), (dsh_a1, dsc_a1, dg_mix1, dkv_sh, dkv_sc, dg_kv, dgm0) = _norm_mod_bwd_pair(
        x2, dh1, dhkv, dx3, (mix_norm_g[1:2], mod1[1]), (kv_norm_g.reshape(1, D), kv_sc),
        ([(saved_mlp0[3], D, 0)], mod0[5], _gated, D), n_ex=E, name="kv_mix_norm_bwd")

    def glu_bwd(d, v, g, ga):
        sg = _sigmoid(g)
        dm = ga * d
        return jnp.concatenate([dm * sg, dm * v * sg * (1.0 - sg)], axis=1).astype(BF16), _csum(d * (v * sg))
    (dx1, dvg), (dsh_m0, dsc_m0, dg_mlp0, dga0) = mlp_bwd(dx2, dff0, x1, 0, mod0, saved_mlp0,
                                                          gated=([(vg, D, 0), (vg, D, 1)], mod0[2], glu_bwd, 2 * D))
    dgy = _mm(dvg, Wglu, "nt", name="glu_up_dx", tm=tm)
    grad_mm(gy_t, dvg, "cols", 4 * D, D, 0, D // 2, "glu_up_dw", transposed=True)

    gpack = gbuf[0].reshape(4, 2, RW // 2, D)
    theirs = _sibling_swap_half(gpack, name="gscatter_d2d")
    chip_sum = _add_my_half(gpack, theirs, cidx, name="gscatter_add")
    dh0, dWb, dWc, dab, dd, from_chips = _s5_bwd(h0, y, dgy, s5_states, Wb, Wc, cr, d_full, chip_sum, n_ex=E, name="s5_bwd")
    ghalf = _sum4_into_half(from_chips, cidx, name="gscatter_sum")
    gsh = _sibling_fill(ghalf, axis=0, name="gscatter_fill").reshape(RW, D)
    (gx,), (dsh_a0, dsc_a0, dg_mix0) = _norm_mod_bwd(x0, dh0, dx1, mix_norm_g[0:1], mod0[1], n_ex=E, name="mix_norm_bwd0")
    grad_x = gx.reshape(E, S, D)

    dm_mine = jnp.concatenate([t.reshape(E, D) for t in
                               (dsh_a0, dsc_a0, dga0, dsh_m0, dsc_m0, dgm0, dsh_a1, dsc_a1, dga1, dsh_m1, dsc_m1, dgm1, dkv_sh, dkv_sc)], axis=1)
    dm_all = _all_gather8(dm_mine.reshape(8, -1), name="ag_dm").reshape(NB, 14 * D)
    sc_f32 = c_all * _sigmoid(c_all)
    g_ada_w = jax.ShapeDtypeStruct(ada_w.shape, F32)
    for l in range(2):
        g_ada_w = _mm(sc_f32, lax.dynamic_slice_in_dim(dm_all, l * 6 * D + chip * wa, wa, axis=1), "tn", name=f"ada_dw{l}", tn=256,
                      into=_Layer(g_ada_w, l))
    g_kv_ada_w = _mm(sc_f32, lax.dynamic_slice_in_dim(dm_all, 12 * D + chip * wk, wk, axis=1), "tn", name="ada_kv_dw", tn=256)
    db_all = _colsum(dm_all, name="ada_db")
    g_ada_b = db_all[0, :12 * D].reshape(2, 6 * D)
    g_kv_ada_b = db_all[0, 12 * D:]

    dWb_re, dWb_im, dC_re, dC_im = _s5_unblock(dWb, dWc)
    small_parts = [dg_mix0.sum(0), dg_mix1.sum(0), dg_mlp0.sum(0), dg_mlp1.sum(0), dg_kv.sum(0),
                   dqg.sum((0, 1, 2)).reshape(2, HEAD_DIM).sum(0), dkg.sum((0, 1, 2)).reshape(2, HEAD_DIM).sum(0),
                   dd[:, 0, :], dab[:, 0, :], dab[:, 1, :], dWb_re, dWb_im, dC_re, dC_im]
    spack, spans = _pack_rows(small_parts, row_multiple=32)
    chip_half = _sibling_sum_half(spack, name="small_d2d")
    ssum = _sum_blocks(_all_gather8(chip_half, name="ag_small"), 4, name="sum_small")
    (g_mix0, g_mix1, g_mlp0, g_mlp1, g_kvn, g_qn, g_kn, g_d, g_abr, g_abi, g_bbr, g_bbi, g_cre, g_cim) = _unpack_rows(ssum, spans)
    _, disc_vjp = jax.vjp(_s5_disc, s5_a_re[0], s5_a_im[0], s5_log_dt[0], s5_b_re[0], s5_b_im[0])
    g_are, g_aim, g_ldt, g_bre, g_bim = disc_vjp((g_abr.reshape(ab_re.shape), g_abi.reshape(ab_im.shape), g_bbr, g_bbi))
    g_s5d = lax.dynamic_slice_in_dim(g_d.reshape(1, D), chip * s5_d.shape[1], s5_d.shape[1], axis=1)

    def upd_big(w, m, v, roff, cb, name):
        shape = w.shape
        W = shape[-1]
        d_, m_, v_, g_ = _adamw2d(w.reshape(-1, W), gsh, m.reshape(-1, W), v.reshape(-1, W), name=name, g_roff=roff, g_cb=cb)
        return [t.reshape(shape) for t in (g_, d_, m_, v_)]

    def upd_own(w, g, m, v, name):
        shape = w.shape
        W = shape[-1]
        d_, m_, v_, g_ = _adamw2d(w.reshape(-1, W), g.reshape(-1, W), m.reshape(-1, W), v.reshape(-1, W), name=name)
        return [t.reshape(shape) for t in (g_, d_, m_, v_)]

    res = {}
    res["ada_w"] = upd_own(ada_w, g_ada_w, m_ada_w, v_ada_w, "adam_ada_w")
    res["kv_ada_w"] = upd_own(kv_ada_w, g_kv_ada_w, m_kv_ada_w, v_kv_ada_w, "adam_kv_ada_w")
    res["mlp_w1"] = upd_big(mlp_w1, m_mlp_w1, v_mlp_w1, 0, 0, "adam_w1")
    res["mlp_w2"] = upd_big(mlp_w2, m_mlp_w2, v_mlp_w2, 2 * D, 0, "adam_w2")
    res["s5_w_glu"] = upd_big(s5_w_glu, m_s5_w_glu, v_s5_w_glu, 4 * D, 0, "adam_glu")
    res["w_kv"] = upd_big(w_kv, m_w_kv, v_w_kv, 4 * D, 1, "adam_wkv")
    res["sb_w_q"] = upd_big(sb_w_q, m_sb_w_q, v_sb_w_q, 5 * D, 0, "adam_wq")
    res["sb_w_o"] = upd_big(sb_w_o, m_sb_w_o, v_sb_w_o, 5 * D + D // 4, 0, "adam_wo")

    small = {
        "ada_b": (ada_b, g_ada_b, m_ada_b, v_ada_b),
        "mix_norm_g": (mix_norm_g, jnp.stack([g_mix0, g_mix1]), m_mix_norm_g, v_mix_norm_g),
        "mlp_norm_g": (mlp_norm_g, jnp.stack([g_mlp0, g_mlp1]), m_mlp_norm_g, v_mlp_norm_g),
        "s5_a_re": (s5_a_re, g_are[None], m_s5_a_re, v_s5_a_re),
        "s5_a_im": (s5_a_im, g_aim[None], m_s5_a_im, v_s5_a_im),
        "s5_log_dt": (s5_log_dt, g_ldt[None], m_s5_log_dt, v_s5_log_dt),
        "s5_b_re": (s5_b_re, g_bre[None], m_s5_b_re, v_s5_b_re),
        "s5_b_im": (s5_b_im, g_bim[None], m_s5_b_im, v_s5_b_im),
        "s5_c_re": (s5_c_re, g_cre[None], m_s5_c_re, v_s5_c_re),
        "s5_c_im": (s5_c_im, g_cim[None], m_s5_c_im, v_s5_c_im),
        "s5_d": (s5_d, g_s5d, m_s5_d, v_s5_d),
        "kv_ada_b": (kv_ada_b, g_kv_ada_b, m_kv_ada_b, v_kv_ada_b),
        "kv_norm_g": (kv_norm_g, g_kvn, m_kv_norm_g, v_kv_norm_g),
        "k_norm_g": (k_norm_g, g_kn, m_k_norm_g, v_k_norm_g),
        "q_norm_g": (q_norm_g, g_qn.reshape(q_norm_g.shape), m_q_norm_g, v_q_norm_g),
    }
    names = list(small)
    packs = [_pack_rows([small[n][i].reshape(small[n][0].shape) for n in names]) for i in range(4)]
    sp = packs[0][1]
    d_, m_, v_, g_ = _adamw2d(packs[0][0], packs[1][0], packs[2][0], packs[3][0], name="adam_small")
    for n, gg, dd_, mm_, vv_ in zip(names, _unpack_rows(g_, sp), _unpack_rows(d_, sp), _unpack_rows(m_, sp), _unpack_rows(v_, sp)):
        res[n] = [gg, dd_, mm_, vv_]

    order = ["ada_w", "ada_b", "mix_norm_g", "mlp_norm_g", "mlp_w1", "mlp_w2", "s5_a_re", "s5_a_im", "s5_log_dt", "s5_b_re", "s5_b_im",
             "s5_c_re", "s5_c_im", "s5_d", "s5_w_glu", "kv_ada_w", "kv_ada_b", "kv_norm_g", "w_kv", "k_norm_g", "sb_w_q", "q_norm_g", "sb_w_o"]
    return (loss, grad_x, *[res[n][0] for n in order], *[res[n][1] for n in order], *[res[n][2] for n in order], *[res[n][3] for n in order])
```

```python
import functools
import math

import jax
import jax.numpy as jnp
from jax import lax
from jax.experimental import pallas as pl
from jax.experimental.pallas import tpu as pltpu

F32 = jnp.float32
BF16 = jnp.bfloat16
EPS = 1e-6
HEAD_DIM = 64
S5_GROUP = 16
S5_STATE = 64
GROUPS_PER_STEP = 8
U_LANES = GROUPS_PER_STEP * S5_GROUP
ST_LANES = GROUPS_PER_STEP * S5_STATE
SCAN_LANES = 256
SCAN_UNROLL = 4
VMEM_LIMIT = 56 * 1024 * 1024
ADAM_LR, ADAM_B1, ADAM_B2, ADAM_EPS, ADAM_WD, ADAM_STEP = 0.001, 0.9, 0.999, 1e-08, 0.01, 10
MESH = pl.DeviceIdType.MESH


def _cp(sem):
    return pltpu.CompilerParams(dimension_semantics=sem, vmem_limit_bytes=VMEM_LIMIT)


class _Sharded:
    def __init__(self, buf, kind, roff, nr, c0, nc):
        self.buf, self.kind, self.roff, self.nr, self.c0, self.nc = buf, kind, roff, nr, c0, nc
        self.shape = (nr, 4 * nc) if kind == "cols" else (4 * nr, nc)

    def operand(self, dims, tn, tk):
        roff, nr, c0, nc = self.roff, self.nr, self.c0, self.nc
        if self.kind == "cols" and dims == "nn":
            tk = min(tk, nr)
            assert roff % tk == 0
            return nc, tk, (None, tk, nc), lambda i, j, k: (j, roff // tk + k, c0 // nc)
        if self.kind == "cols":
            tn = min(tn, nr)
            assert roff % tn == 0
            return tn, nc, (None, tn, nc), lambda i, j, k: (k, roff // tn + j, c0 // nc)
        if dims == "nn":
            tn = min(tn, nc)
            assert roff % nr == 0 and c0 % tn == 0
            return tn, nr, (None, nr, tn), lambda i, j, k: (k, roff // nr, c0 // tn + j)
        tk = min(tk, nc)
        assert roff % nr == 0 and c0 % tk == 0
        return nr, tk, (None, nr, tk), lambda i, j, k: (j, roff // nr, c0 // tk + k)

    def result(self, tm, tn):
        roff, nr, c0, nc = self.roff, self.nr, self.c0, self.nc
        if self.kind == "cols":
            tm = min(tm, nr)
            assert roff % tm == 0
            return tm, nc, (None, tm, nc), lambda i, j, k: (j, roff // tm + i, c0 // nc)
        tm, tn = min(tm, nr), min(tn, nc)
        assert roff % tm == 0 and c0 % tn == 0
        per = nr // tm
        return tm, tn, (None, tm, tn), lambda i, j, k: (i // per, roff // tm + i % per, c0 // tn + j)


class _Layer:
    def __init__(self, buf, layer):
        self.buf, self.layer, self.shape = buf, layer, tuple(buf.shape[1:])

    def operand(self, dims, tn, tk):
        assert dims == "nn"
        layer = self.layer
        return tn, tk, (None, tk, tn), lambda i, j, k: (layer, k, j)

    def result(self, tm, tn):
        layer = self.layer
        return tm, tn, (None, tm, tn), lambda i, j, k: (layer, i, j)


def _mm(a, b, dims, *, name, out_dtypes=(F32,), epilogue=None, extras=(), tm=512, tn=1024, tk=1024, into=None):
    bshape = b.shape
    if dims == "nn":
        (M, K), (_, N) = a.shape, bshape
    elif dims == "nt":
        (M, K), (N, _) = a.shape, bshape
    else:
        (K, M), (_, N) = a.shape, bshape
    tm, tn, tk = min(tm, M), min(tn, N), min(tk, K)
    b_arr = b
    if into is not None:
        assert (M, N) == into.shape and len(out_dtypes) == 1 and not isinstance(b, _Sharded)
        tm, tn, o_blk, o_map = into.result(tm, tn)
        out_specs, out_shape = [pl.BlockSpec(o_blk, o_map)], [jax.ShapeDtypeStruct(into.buf.shape, into.buf.dtype)]
    if isinstance(b, (_Sharded, _Layer)):
        tn, tk, b_blk, b_map = b.operand(dims, tn, tk)
        b_spec, b_arr = pl.BlockSpec(b_blk, b_map), b.buf
    else:
        b_spec = pl.BlockSpec((tn, tk), lambda i, j, k: (j, k)) if dims == "nt" else pl.BlockSpec((tk, tn), lambda i, j, k: (k, j))
    if into is None:
        out_specs = [pl.BlockSpec((tm, tn), lambda i, j, k: (i, j)) for _ in out_dtypes]
        out_shape = [jax.ShapeDtypeStruct((M, N), d) for d in out_dtypes]
    assert M % tm == 0 and N % tn == 0 and K % tk == 0, (M, N, K, tm, tn, tk)
    nk = K // tk
    extras = [e(tm, tn) for e in extras]
    a_spec = pl.BlockSpec((tk, tm), lambda i, j, k: (k, i)) if dims == "tn" else pl.BlockSpec((tm, tk), lambda i, j, k: (i, k))
    contract = {"nn": ((1,), (0,)), "nt": ((1,), (1,)), "tn": ((0,), (0,))}[dims]
    n_ex, n_out = len(extras), len(out_dtypes)
    chain = [into.buf] if into is not None and not isinstance(into.buf, jax.ShapeDtypeStruct) else []
    n_in = n_ex + len(chain)

    def finish(r, ex, outs):
        res = epilogue(r, *[e[...] for e in ex]) if epilogue is not None else (r,)
        for o, v in zip(outs, res):
            o[...] = v.astype(o.dtype)

    def product(a_ref, b_ref):
        return lax.dot_general(a_ref[...].astype(BF16), b_ref[...].astype(BF16), (contract, ((), ())), preferred_element_type=F32)

    def body_one(a_ref, b_ref, *rest):
        finish(product(a_ref, b_ref), rest[:n_ex], rest[n_in:])

    def body_acc(a_ref, b_ref, *rest):
        ex, outs, acc = rest[:n_ex], rest[n_in:n_in + n_out], rest[-1]
        k = pl.program_id(2)

        @pl.when(k == 0)
        def _():
            acc[...] = product(a_ref, b_ref)

        @pl.when(jnp.logical_and(k > 0, k < nk - 1))
        def _():
            acc[...] += product(a_ref, b_ref)

        @pl.when(k == nk - 1)
        def _():
            finish(acc[...] + product(a_ref, b_ref), ex, outs)

    out = pl.pallas_call(
        body_one if nk == 1 else body_acc, name=name, grid=(M // tm, N // tn, nk),
        in_specs=[a_spec, b_spec] + [pl.BlockSpec(blk, im) for (_, blk, im) in extras] + [ANY for _ in chain],
        out_specs=out_specs, out_shape=out_shape,
        input_output_aliases={2 + n_ex: 0} if chain else {},
        scratch_shapes=[] if nk == 1 else [pltpu.VMEM((tm, tn), F32)],
        compiler_params=_cp(("parallel", "parallel", "arbitrary")),
    )(a, b_arr, *[e[0] for e in extras], *chain)
    return out if n_out > 1 else out[0]


def _mn_extra(arr):
    return lambda tm, tn: (arr, (tm, tn), lambda i, j, k: (i, j))


def _vec_extra(vec, S):
    return lambda tm, tn: (vec, (None, 1, tn), lambda i, j, k: ((i * tm) // S, 0, j))


def _rowwise(fn, rows, vecs=(), consts=(), out_rows=(), out_sums=(), *, n_ex, name, tr=512):
    rows = [r if len(r) == 4 else (*r, 0) for r in rows]
    S = min(r[0].shape[0] for r in rows if r[3] == 0) // n_ex
    tr = math.gcd(tr, S)
    assert S % tr == 0
    nb = S // tr
    in_specs = []
    for (arr, w, cb, roff) in rows:
        assert roff % tr == 0
        in_specs.append(pl.BlockSpec((tr, w), functools.partial(lambda e, i, cb, ro: (e * nb + i + ro, cb), cb=cb, ro=roff // tr)))
    for v in vecs:
        in_specs.append(pl.BlockSpec((None, 1, v.shape[-1]), lambda e, i: (e, 0, 0)))
    for c in consts:
        in_specs.append(pl.BlockSpec((1, c.shape[-1]), lambda e, i: (0, 0)))
    n_in, n_or, n_os = len(in_specs), len(out_rows), len(out_sums)
    flipped = [len(o) == 3 and o[2] for o in out_rows]
    out_specs = [pl.BlockSpec((o[0], tr), lambda e, i: (0, e * nb + i)) if f else pl.BlockSpec((tr, o[0]), lambda e, i: (e * nb + i, 0))
                 for o, f in zip(out_rows, flipped)]
    out_specs += [pl.BlockSpec((None, 1, w), lambda e, i: (e, 0, 0)) for w in out_sums]
    out_shape = [jax.ShapeDtypeStruct((o[0], n_ex * S) if f else (n_ex * S, o[0]), o[1]) for o, f in zip(out_rows, flipped)]
    out_shape += [jax.ShapeDtypeStruct((n_ex, 1, w), F32) for w in out_sums]

    def body(*refs):
        ins, o_r, o_s = refs[:n_in], refs[n_in:n_in + n_or], refs[n_in + n_or:]
        ro, so = fn(*[r[...] for r in ins])
        for o, v, f in zip(o_r, ro, flipped):
            o[...] = (v.T if f else v).astype(o.dtype)
        i = pl.program_id(1)
        for o, v in zip(o_s, so):
            @pl.when(i == 0)
            def _(o=o, v=v):
                o[...] = v

            @pl.when(i > 0)
            def _(o=o, v=v):
                o[...] += v

    outs = pl.pallas_call(
        body, name=name, grid=(n_ex, nb), in_specs=in_specs, out_specs=out_specs, out_shape=out_shape,
        compiler_params=_cp(("parallel", "arbitrary")),
    )(*[r[0] for r in rows], *vecs, *consts)
    return outs[:n_or], outs[n_or:]


def _csum(x):
    return jnp.sum(x, axis=0, keepdims=True)


def _norm_mod_fwd(x, g, sh, sc, *, n_ex, out_dtype, name, with_transpose=False):
    def fn(xt, sht, sct, gt):
        r = lax.rsqrt(jnp.mean(xt * xt, axis=-1, keepdims=True) + EPS)
        h = (xt * r * gt) * (1.0 + sct) + sht
        return [h, h] if with_transpose else [h], []
    D = x.shape[1]
    outs = [(D, out_dtype), (D, out_dtype, True)] if with_transpose else [(D, out_dtype)]
    res = _rowwise(fn, [(x, D, 0)], [sh, sc], [g], outs, [], n_ex=n_ex, name=name)[0]
    return res if with_transpose else res[0]


def _norm_mod_fwd_pair(x, first, second, *, n_ex, name):
    def fn(xt, sh1, sc1, sh2, sc2, g1, g2):
        n = xt * lax.rsqrt(jnp.mean(xt * xt, axis=-1, keepdims=True) + EPS)
        h1 = (n * g1) * (1.0 + sc1) + sh1
        h2 = (n * g2) * (1.0 + sc2) + sh2
        return [h1, h1, h2, h2], []
    D = x.shape[1]
    outs = [(D, BF16), (D, BF16, True)] * 2
    return _rowwise(fn, [(x, D, 0)], [first[1], first[2], second[1], second[2]], [first[0], second[0]], outs, [], n_ex=n_ex, name=name)[0]


def _gated(dx, branch, gate):
    return (gate * dx).astype(BF16), _csum(dx * branch)


def _norm_mod_bwd(x, dh, dres, g, sc, *, n_ex, name, gated=None):
    n_rows = len(gated[0]) if gated is not None else 0

    def fn(xt, dht, drt, *rest):
        sct, gt = rest[-2], rest[-1]
        dht = dht.astype(F32)
        r = lax.rsqrt(jnp.mean(xt * xt, axis=-1, keepdims=True) + EPS)
        n = xt * r
        y = n * gt
        dy = dht * (1.0 + sct)
        dn = dy * gt
        dx = drt + r * (dn - n * jnp.mean(dn * n, axis=-1, keepdims=True))
        rows, sums = [dx], [_csum(dht), _csum(dht * y), _csum(dy * n)]
        if gated is not None:
            dbranch, dgate = gated[2](dx, *rest[:n_rows + 1])
            rows, sums = rows + [dbranch], sums + [dgate]
        return rows, sums
    D = x.shape[1]
    extra_rows, extra_vecs = (list(gated[0]), [gated[1]]) if gated is not None else ([], [])
    return _rowwise(fn, [(x, D, 0), (dh, D, 0), (dres, D, 0)] + extra_rows, extra_vecs + [sc], [g],
                    [(D, F32)] + ([(gated[3], BF16)] if gated is not None else []), [D, D, D] + ([D] if gated is not None else []),
                    n_ex=n_ex, name=name)


def _norm_mod_bwd_pair(x, dh1, dh2, dres, first, second, gated, *, n_ex, name):
    n_rows = len(gated[0])

    def fn(xt, d1, d2, drt, *rest):
        sc1, sc2, g1, g2 = rest[-4:]
        r = lax.rsqrt(jnp.mean(xt * xt, axis=-1, keepdims=True) + EPS)
        n = xt * r
        dx, sums = drt, []
        for dht, sct, gt in ((d1.astype(F32), sc1, g1), (d2.astype(F32), sc2, g2)):
            dy = dht * (1.0 + sct)
            dn = dy * gt
            dx = dx + r * (dn - n * jnp.mean(dn * n, axis=-1, keepdims=True))
            sums += [_csum(dht), _csum(dht * (n * gt)), _csum(dy * n)]
        dbranch, dgate = gated[2](dx, *rest[:n_rows + 1])
        return [dx, dbranch], sums + [dgate]
    D = x.shape[1]
    return _rowwise(fn, [(x, D, 0), (dh1, D, 0), (dh2, D, 0), (dres, D, 0)] + list(gated[0]), [gated[1], first[1], second[1]],
                    [first[0], second[0]], [(D, F32), (gated[3], BF16)], [D] * 7, n_ex=n_ex, name=name)


def _sigmoid(x):
    return 1.0 / (1.0 + jnp.exp(-x))


def _gelu(y):
    return 0.5 * y * (1.0 + jnp.tanh(0.7978845608028654 * (y + 0.044715 * y * y * y)))


def _gelu_grad(y):
    t = jnp.tanh(0.7978845608028654 * (y + 0.044715 * y * y * y))
    return 0.5 * (1.0 + t) + 0.5 * y * (1.0 - t * t) * 0.7978845608028654 * (1.0 + 3 * 0.044715 * y * y)


def _adamw_fn(w, g, m, v):
    m2 = ADAM_B1 * m + (1.0 - ADAM_B1) * g
    v2 = ADAM_B2 * v + (1.0 - ADAM_B2) * (g * g)
    m_hat = m2 / (1.0 - ADAM_B1 ** ADAM_STEP)
    v_hat = v2 / (1.0 - ADAM_B2 ** ADAM_STEP)
    delta = -ADAM_LR * (m_hat / (jnp.sqrt(v_hat) + ADAM_EPS) + ADAM_WD * w)
    return delta, m2, v2


def _adamw2d(w, g, m, v, *, name, g_roff=0, g_cb=0):
    R, W = w.shape

    def fn(wt, gt, mt, vt):
        d, m2, v2 = _adamw_fn(wt, gt, mt, vt)
        return [d, m2, v2, gt], []
    return _rowwise(fn, [(w, W, 0), (g, W, g_cb, g_roff), (m, W, 0), (v, W, 0)], [], [],
                    [(W, F32)] * 4, [], n_ex=1, name=name, tr=512 if W <= 1024 else 256)[0]


def _scan_tiles(re_ref, im_ref, cf, lane0, n_chunks, reverse, extra=None):
    L = SCAN_LANES
    lanes = pl.ds(lane0, L)
    A = [cf[i, :, lanes] for i in range(8)]
    shifts = (7, 6, 4) if reverse else (1, 2, 4)
    edge = 0 if reverse else 7

    U = SCAN_UNROLL
    n_groups = n_chunks // U

    def body(c, carry):
        first = ((n_groups - 1 - c) if reverse else c) * U
        rows = pl.ds(pl.multiple_of(first * 8, 8 * U), 8 * U)
        big_r, big_i = re_ref[rows, lanes], im_ref[rows, lanes]
        tiles = []
        for u in range(U):
            xr, xi = big_r[8 * u:8 * u + 8, :], big_i[8 * u:8 * u + 8, :]
            for idx, sft in enumerate(shifts):
                ar, ai = A[2 * idx], A[2 * idx + 1]
                rr, ri = pltpu.roll(xr, sft, 0), pltpu.roll(xi, sft, 0)
                xr, xi = xr + ar * rr - ai * ri, xi + ar * ri + ai * rr
            tiles.append((xr, xi))
        pr, pi = A[6], A[7]
        cr, ci = carry[0], carry[1]
        for u in (range(U - 1, -1, -1) if reverse else range(U)):
            xr, xi = tiles[u]
            xr, xi = xr + pr * cr - pi * ci, xi + pr * ci + pi * cr
            tiles[u] = (xr, xi)
            cr, ci = jnp.broadcast_to(xr[edge:edge + 1, :], (8, L)), jnp.broadcast_to(xi[edge:edge + 1, :], (8, L))
        re_ref[rows, lanes] = jnp.concatenate([t[0] for t in tiles], axis=0)
        im_ref[rows, lanes] = jnp.concatenate([t[1] for t in tiles], axis=0)
        return (cr, ci) if extra is None else (cr, ci) + extra(first, tiles, carry[2:])

    assert n_chunks % U == 0
    z = jnp.zeros((8, L), F32)
    init = (z, z) if extra is None else (z, z, z, z)
    return lax.fori_loop(0, n_groups, body, init)


def _s5_consts(ab_re, ab_im):
    ng = ab_re.shape[0] // GROUPS_PER_STEP
    ar, ai = ab_re.reshape(ng, 1, ST_LANES), ab_im.reshape(ng, 1, ST_LANES)

    def cmul(xr, xi, yr, yi):
        return xr * yr - xi * yi, xr * yi + xi * yr

    def build(ar, ai, reverse):
        pw = [(ar, ai)]
        for _ in range(7):
            pw.append(cmul(*pw[-1], ar, ai))
        row = jnp.arange(8).reshape(1, 8, 1)
        tiles = []
        for k in (1, 2, 4):
            keep = (row <= 7 - k) if reverse else (row >= k)
            tiles += [jnp.where(keep, pw[k - 1][0], 0.0), jnp.where(keep, pw[k - 1][1], 0.0)]
        order = [7 - r for r in range(8)] if reverse else list(range(8))
        tiles += [jnp.concatenate([pw[o][0] for o in order], axis=1), jnp.concatenate([pw[o][1] for o in order], axis=1)]
        return jnp.stack([jnp.broadcast_to(t, (ng, 8, ST_LANES)) for t in tiles], axis=1)

    return build(ar, ai, False), build(ar, -ai, True)


def _s5_blockdiag(bb_re, bb_im, c_re, c_im):
    G = bb_re.shape[0]
    ng = G // GROUPS_PER_STEP
    eye = jnp.eye(GROUPS_PER_STEP, dtype=F32)

    def wb(bb):
        return jnp.einsum("bgph,gk->bghkp", bb.reshape(ng, GROUPS_PER_STEP, S5_STATE, S5_GROUP), eye).reshape(ng, U_LANES, ST_LANES)

    def wc(cc):
        return jnp.einsum("bghp,gk->bkpgh", cc.reshape(ng, GROUPS_PER_STEP, S5_GROUP, S5_STATE), eye).reshape(ng, ST_LANES, U_LANES)

    Wb = jnp.concatenate([wb(bb_re), wb(bb_im)], axis=2).astype(BF16)
    Wc = jnp.concatenate([wc(c_re), -wc(c_im)], axis=1).astype(BF16)
    return Wb, Wc


def _s5_unblock(dWb, dWc):
    ng = dWb.shape[0]
    eye = jnp.eye(GROUPS_PER_STEP, dtype=F32)

    def ub(w):
        return jnp.einsum("bghkp,gk->bgph", w.reshape(ng, GROUPS_PER_STEP, S5_GROUP, GROUPS_PER_STEP, S5_STATE), eye).reshape(-1, S5_STATE, S5_GROUP)

    def uc(w):
        return jnp.einsum("bkpgh,gk->bghp", w.reshape(ng, GROUPS_PER_STEP, S5_STATE, GROUPS_PER_STEP, S5_GROUP), eye).reshape(-1, S5_GROUP, S5_STATE)

    return ub(dWb[:, :, :ST_LANES]), ub(dWb[:, :, ST_LANES:]), uc(dWc[:, :ST_LANES, :]), -uc(dWc[:, ST_LANES:, :])


def _s5_disc(a_re, a_im, log_dt, b_re, b_im):
    dt = jnp.exp(log_dt)[:, None]
    mag = jnp.exp(a_re * dt)
    ab_re = mag * jnp.cos(a_im * dt)
    ab_im = mag * jnp.sin(a_im * dt)
    den = a_re * a_re + a_im * a_im
    nr, ni = ab_re - 1, ab_im
    f_re = (nr * a_re + ni * a_im) / den
    f_im = (ni * a_re - nr * a_im) / den
    bb_re = f_re[..., None] * b_re - f_im[..., None] * b_im
    bb_im = f_re[..., None] * b_im + f_im[..., None] * b_re
    return ab_re, ab_im, bb_re, bb_im


ROW_CHUNK = 512


def _s5_fwd(u, Wb, Wc, cf, d, xsrc, *, n_ex, name):
    T, D = u.shape
    S = T // n_ex
    ng = D // U_LANES
    rc = min(ROW_CHUNK, S)

    def body(u_ref, wb_ref, wc_ref, cf_ref, d_ref, xsrc_ref, y_ref, gy_ref, gyt_ref, st_ref, xout_ref, re_s, im_s, *sems):
        step = pl.program_id(0) * ng + pl.program_id(1)
        exch = _ChipExchange(xsrc_ref, xout_ref, *sems, scatter=False)

        @pl.when(step == 0)
        def _():
            exch.start()

        for r in range(S // rc):
            rows = pl.ds(r * rc, rc)
            bu = jnp.dot(u_ref[rows, :].astype(BF16), wb_ref[...], preferred_element_type=F32)
            re_s[rows, :] = bu[:, :ST_LANES]
            im_s[rows, :] = bu[:, ST_LANES:]
        for l0 in range(0, ST_LANES, SCAN_LANES):
            _scan_tiles(re_s, im_s, cf_ref, l0, S // 8, False)
        for r in range(S // rc):
            rows = pl.ds(r * rc, rc)
            st = jnp.concatenate([re_s[rows, :], im_s[rows, :]], axis=1).astype(BF16)
            st_ref[rows, :] = st
            y = jnp.dot(st, wc_ref[...], preferred_element_type=F32) + d_ref[...] * u_ref[rows, :]
            y_ref[rows, :] = y
            gy = _gelu(y)
            gy_ref[rows, :] = gy.astype(BF16)
            gyt_ref[:, rows] = gy.T.astype(BF16)

        @pl.when(step == n_ex * ng - 1)
        def _():
            exch.wait()

    return pl.pallas_call(
        body, name=name, grid=(n_ex, ng),
        in_specs=[pl.BlockSpec((S, U_LANES), lambda e, g: (e, g)),
                  pl.BlockSpec((None, U_LANES, 2 * ST_LANES), lambda e, g: (g, 0, 0)),
                  pl.BlockSpec((None, 2 * ST_LANES, U_LANES), lambda e, g: (g, 0, 0)),
                  pl.BlockSpec((None, 8, 8, ST_LANES), lambda e, g: (g, 0, 0, 0)),
                  pl.BlockSpec((1, U_LANES), lambda e, g: (0, g)), ANY],
        out_specs=[pl.BlockSpec((S, U_LANES), lambda e, g: (e, g))] * 2 + [pl.BlockSpec((U_LANES, S), lambda e, g: (g, e)),
                   pl.BlockSpec((S, 2 * ST_LANES), lambda e, g: (e, g)), ANY],
        out_shape=[jax.ShapeDtypeStruct((T, D), F32), jax.ShapeDtypeStruct((T, D), BF16), jax.ShapeDtypeStruct((D, T), BF16),
                   jax.ShapeDtypeStruct((T, ng * 2 * ST_LANES), BF16), _ChipExchange.out_shape(xsrc, False)],
        scratch_shapes=[pltpu.VMEM((S, ST_LANES), F32)] * 2 + _ChipExchange.SCRATCH,
        compiler_params=_cp(("arbitrary", "arbitrary")),
    )(u, Wb, Wc, cf, d, xsrc)


def _s5_bwd(u, y, dgy, st, Wb, Wc, cr, d, xsrc, *, n_ex, name):
    T, D = u.shape
    S = T // n_ex
    ng = D // U_LANES
    rc = min(ROW_CHUNK, S)
    nch = S // 8
    grp = 8 * SCAN_UNROLL
    assert grp % 16 == 0

    def body(u_ref, y_ref, dgy_ref, st_ref, wb_ref, wc_ref, cr_ref, d_ref, xsrc_ref,
             du_ref, dwb_ref, dwc_ref, dab_ref, dd_ref, xout_ref, gr_s, gi_s, dy_s, *sems):
        e = pl.program_id(1)
        step = pl.program_id(0) * n_ex + e
        exch = _ChipExchange(xsrc_ref, xout_ref, *sems, scatter=True)

        @pl.when(step == 0)
        def _():
            exch.start()

        @pl.when(e == 0)
        def _():
            dwb_ref[...] = jnp.zeros_like(dwb_ref)
            dwc_ref[...] = jnp.zeros_like(dwc_ref)
            dab_ref[...] = jnp.zeros_like(dab_ref)
            dd_ref[...] = jnp.zeros_like(dd_ref)

        dd = jnp.zeros((1, U_LANES), F32)
        for r in range(S // rc):
            rows = pl.ds(r * rc, rc)
            ut = u_ref[rows, :]
            dy = dgy_ref[rows, :].astype(F32) * _gelu_grad(y_ref[rows, :])
            dy_s[rows, :] = dy
            dd = dd + _csum(dy * ut)
            go = lax.dot_general(dy.astype(BF16), wc_ref[...], (((1,), (1,)), ((), ())), preferred_element_type=F32)
            gr_s[rows, :] = go[:, :ST_LANES]
            gi_s[rows, :] = go[:, ST_LANES:]
        dd_ref[0:1, :] += dd
        row0 = lax.broadcasted_iota(jnp.int32, (8, SCAN_LANES), 0) == 0
        for l0 in range(0, ST_LANES, SCAN_LANES):
            lanes = pl.ds(l0, SCAN_LANES)

            def dab_group(first, tiles, acc, l0=l0):
                def states(r0, n, lane0):
                    return st_ref[pl.ds(pl.multiple_of(r0, 16), n), pl.ds(lane0, SCAN_LANES)].astype(F32)
                r0 = first * 8
                cur = states(r0, grp, l0), states(r0, grp, ST_LANES + l0)
                live = (first > 0).astype(F32)
                p0 = jnp.maximum(r0 - 16, 0)
                before = [states(p0, 16, l0)[8:16, :] * live, states(p0, 16, ST_LANES + l0)[8:16, :] * live]
                a_re, a_im = acc
                for t, (gr, gi) in enumerate(tiles):
                    here = [c[8 * t:8 * t + 8, :] for c in cur]
                    sr, si = [jnp.where(row0, pltpu.roll(b, 1, 0), pltpu.roll(h, 1, 0)) for b, h in zip(before, here)]
                    a_re, a_im = a_re + gr * sr + gi * si, a_im + gi * sr - gr * si
                    before = here
                return a_re, a_im

            res = _scan_tiles(gr_s, gi_s, cr_ref, l0, nch, True, extra=dab_group)
            dab_ref[0:1, lanes] += _csum(res[2])
            dab_ref[1:2, lanes] += _csum(res[3])
        for r in range(S // rc):
            rows = pl.ds(r * rc, rc)
            st = st_ref[rows, :]
            g = jnp.concatenate([gr_s[rows, :], gi_s[rows, :]], axis=1).astype(BF16)
            dyb = dy_s[rows, :].astype(BF16)
            dwc_ref[...] += lax.dot_general(st, dyb, (((0,), (0,)), ((), ())), preferred_element_type=F32)
            dwb_ref[...] += lax.dot_general(u_ref[rows, :].astype(BF16), g, (((0,), (0,)), ((), ())), preferred_element_type=F32)
            du = lax.dot_general(g, wb_ref[...], (((1,), (1,)), ((), ())), preferred_element_type=F32)
            du_ref[rows, :] = du + d_ref[...] * dy_s[rows, :]

        @pl.when(step == ng * n_ex - 1)
        def _():
            exch.wait()

    return pl.pallas_call(
        body, name=name, grid=(ng, n_ex),
        in_specs=[pl.BlockSpec((S, U_LANES), lambda g, e: (e, g))] * 3 + [
            pl.BlockSpec((S, 2 * ST_LANES), lambda g, e: (e, g)),
            pl.BlockSpec((None, U_LANES, 2 * ST_LANES), lambda g, e: (g, 0, 0)),
            pl.BlockSpec((None, 2 * ST_LANES, U_LANES), lambda g, e: (g, 0, 0)),
            pl.BlockSpec((None, 8, 8, ST_LANES), lambda g, e: (g, 0, 0, 0)),
            pl.BlockSpec((1, U_LANES), lambda g, e: (0, g)), ANY],
        out_specs=[pl.BlockSpec((S, U_LANES), lambda g, e: (e, g)),
                   pl.BlockSpec((None, U_LANES, 2 * ST_LANES), lambda g, e: (g, 0, 0)),
                   pl.BlockSpec((None, 2 * ST_LANES, U_LANES), lambda g, e: (g, 0, 0)),
                   pl.BlockSpec((None, 8, ST_LANES), lambda g, e: (g, 0, 0)),
                   pl.BlockSpec((None, 8, U_LANES), lambda g, e: (g, 0, 0)), ANY],
        out_shape=[jax.ShapeDtypeStruct((T, D), F32),
                   jax.ShapeDtypeStruct((ng, U_LANES, 2 * ST_LANES), F32),
                   jax.ShapeDtypeStruct((ng, 2 * ST_LANES, U_LANES), F32),
                   jax.ShapeDtypeStruct((ng, 8, ST_LANES), F32),
                   jax.ShapeDtypeStruct((ng, 8, U_LANES), F32), _ChipExchange.out_shape(xsrc, True)],
        scratch_shapes=[pltpu.VMEM((S, ST_LANES), F32)] * 2 + [pltpu.VMEM((S, U_LANES), F32)] + _ChipExchange.SCRATCH,
        compiler_params=_cp(("arbitrary", "arbitrary")),
    )(u, y, dgy, st, Wb, Wc, cr, d, xsrc)


TQ = 256
KW = 512
SUB = 128


def _head_masks():
    lane = lax.broadcasted_iota(jnp.int32, (1, 2 * HEAD_DIM), 1)
    m0 = (lane < HEAD_DIM).astype(F32)
    return m0, 1.0 - m0


def _head_norm(x, g, m0, m1):
    sq = x * x
    r0 = lax.rsqrt(jnp.sum(sq * m0, axis=-1, keepdims=True) / HEAD_DIM + EPS)
    r1 = lax.rsqrt(jnp.sum(sq * m1, axis=-1, keepdims=True) / HEAD_DIM + EPS)
    r = m0 * r0 + m1 * r1
    return x * r, r


def _head_norm_bwd(dy, n, r, g, m0, m1):
    dn = dy * g
    p = dn * n
    mean = (m0 * jnp.sum(p * m0, axis=-1, keepdims=True) + m1 * jnp.sum(p * m1, axis=-1, keepdims=True)) / HEAD_DIM
    return r * (dn - n * mean), _csum(dy * n)


def _pair_matrix(kind):
    r = lax.broadcasted_iota(jnp.int32, (2 * SUB, 2 * SUB), 0)
    c = lax.broadcasted_iota(jnp.int32, (2 * SUB, 2 * SUB), 1)
    same = (r < SUB) == (c < SUB)
    rel = {"after": r > c, "upto": r <= c, "before": r < c}[kind]
    return jnp.logical_and(same, rel).astype(BF16)


def _block_sums(x, mat, carry, reverse, terms=2):
    hi = x.astype(BF16)
    lo = (x - hi.astype(F32)).astype(BF16) if terms == 2 else None
    npair = x.shape[1] // (2 * SUB)
    parts = [None] * (2 * npair)
    for p in (range(npair - 1, -1, -1) if reverse else range(npair)):
        sl = slice(2 * SUB * p, 2 * SUB * (p + 1))
        loc = jnp.dot(hi[:, sl], mat, preferred_element_type=F32)
        if terms == 2:
            loc = loc + jnp.dot(lo[:, sl], mat, preferred_element_type=F32)
        for b in ((1, 0) if reverse else (0, 1)):
            k = 2 * p + b
            parts[k] = loc[:, SUB * b:SUB * (b + 1)] + carry
            carry = carry + jnp.sum(x[:, SUB * k:SUB * (k + 1)], axis=-1, keepdims=True)
    return jnp.concatenate(parts, axis=1), carry


def _sb_logits(z, mask):
    lp = jnp.minimum(z, 0.0) - jnp.log(1.0 + jnp.exp(-jnp.abs(z)))
    lf = lp - z
    if mask is not None:
        lf = jnp.where(mask, lf, 0.0)
    return lp, lf


def _causal_mask(row0, col0, kw):
    r = row0 + lax.broadcasted_iota(jnp.int32, (TQ, kw), 0)
    c = col0 + lax.broadcasted_iota(jnp.int32, (TQ, kw), 1)
    return c < r


def _transposed_windows(x, ref):
    for w in range(x.shape[0] // KW):
        ref[w] = x[w * KW:(w + 1) * KW, :].T.astype(BF16)


def _attn_fwd(q, kv, qg, kg, xsrc, *, n_ex, name):
    T, D = q.shape
    S = T // n_ex
    nhp = D // (2 * HEAD_DIM)
    nq = S // TQ
    scale = 1.0 / math.sqrt(HEAD_DIM)

    def body(q_ref, k_ref, v_ref, qg_ref, kg_ref, xsrc_ref, o_ref, tot_ref, ot_ref, xout_ref, kT_s, qm_s, vm_s, *sems):
        step = pl.program_id(0) * nhp + pl.program_id(1)
        exch = _ChipExchange(xsrc_ref, xout_ref, *sems, scatter=False)

        @pl.when(step == 0)
        def _():
            exch.start()

        m0, m1 = _head_masks()
        qn, _ = _head_norm(q_ref[...], None, m0, m1)
        qn = qn * (qg_ref[...] * scale)
        kn, _ = _head_norm(k_ref[...], None, m0, m1)
        _transposed_windows(kn * kg_ref[...], kT_s)
        v = v_ref[...]
        for h, m in enumerate((m0, m1)):
            qm_s[h] = (qn * m).astype(BF16)
            vm_s[h] = (v * m).astype(BF16)
        u_after = _pair_matrix("after")

        def window(rows, win, st, mask, kw):
            keys = pl.ds(pl.multiple_of(win * KW, KW), kw)
            zs = [jnp.dot(qm_s[h, rows, :], kT_s[win, :, :kw], preferred_element_type=F32) for h in range(2)]
            lg = [_sb_logits(zs[h], mask) for h in range(2)]
            sums = [_block_sums(lg[h][1], u_after, st[2 * h], True) for h in range(2)]
            out = ()
            for h in range(2):
                w = jnp.exp(lg[h][0] + sums[h][0])
                if mask is not None:
                    w = jnp.where(mask, w, 0.0)
                out += (sums[h][1], st[2 * h + 1] + jnp.dot(w.astype(BF16), vm_s[h, keys, :], preferred_element_type=F32))
            return out

        def qtile(iq, last, kw):
            rows = pl.ds(pl.multiple_of(iq * TQ, TQ), TQ)
            mask = _causal_mask(iq * TQ, last * KW, kw)
            z1, zq = jnp.zeros((TQ, 1), F32), jnp.zeros((TQ, 2 * HEAD_DIM), F32)
            st = window(rows, last, (z1, zq, z1, zq), mask, kw)
            st = lax.fori_loop(0, last, lambda jj, st: window(rows, last - 1 - jj, st, None, KW), st)
            o_ref[rows, :] = st[1] + st[3]
            tot_ref[rows, :] = st[0] * m0 + st[2] * m1

        def qtiles_of_window(a, _):
            for sub in range(KW // TQ):
                qtile(a * (KW // TQ) + sub, a, (sub + 1) * TQ)
            return 0

        lax.fori_loop(0, S // KW, qtiles_of_window, 0)
        ot_ref[...] = o_ref[...].T.astype(BF16)

        @pl.when(step == n_ex * nhp - 1)
        def _():
            exch.wait()

    assert S % KW == 0 and KW % TQ == 0
    nwin = S // KW
    blk = (S, 2 * HEAD_DIM)
    return pl.pallas_call(
        body, name=name, grid=(n_ex, nhp),
        in_specs=[pl.BlockSpec(blk, lambda e, h: (e, h)), pl.BlockSpec(blk, lambda e, h: (e, h)),
                  pl.BlockSpec(blk, lambda e, h: (e, h + nhp)),
                  pl.BlockSpec((1, 2 * HEAD_DIM), lambda e, h: (0, 0)), pl.BlockSpec((1, 2 * HEAD_DIM), lambda e, h: (0, 0)), ANY],
        out_specs=[pl.BlockSpec(blk, lambda e, h: (e, h))] * 2 + [pl.BlockSpec((2 * HEAD_DIM, S), lambda e, h: (h, e)), ANY],
        out_shape=[jax.ShapeDtypeStruct((T, D), F32)] * 2 + [jax.ShapeDtypeStruct((D, T), BF16), _ChipExchange.out_shape(xsrc, False)],
        scratch_shapes=[pltpu.VMEM((nwin, 2 * HEAD_DIM, KW), BF16), pltpu.VMEM((2,) + blk, BF16), pltpu.VMEM((2,) + blk, BF16)]
        + _ChipExchange.SCRATCH,
        compiler_params=_cp(("arbitrary", "arbitrary")),
    )(q, kv, kv, qg, kg, xsrc)


def _attn_bwd(q, kv, tot, do, qg, kg, *, n_ex, name):
    T, D = q.shape
    S = T // n_ex
    nhp = D // (2 * HEAD_DIM)
    nq = S // TQ
    scale = 1.0 / math.sqrt(HEAD_DIM)

    def body(q_ref, k_ref, v_ref, tot_ref, do_ref, qg_ref, kg_ref, dq_ref, dk_ref, dv_ref, dqg_ref, dkg_ref,
             kT_s, vT_s, km_s, qm_s, dom_s, dqn_s, dkT_s, dvT_s):
        m0, m1 = _head_masks()
        qn, qr = _head_norm(q_ref[...], None, m0, m1)
        kn, kr = _head_norm(k_ref[...], None, m0, m1)
        qs = qn * (qg_ref[...] * scale)
        kk = kn * kg_ref[...]
        _transposed_windows(kk, kT_s)
        _transposed_windows(v_ref[...], vT_s)
        do = do_ref[...]
        for h, m in enumerate((m0, m1)):
            qm_s[h] = (qs * m).astype(BF16)
            km_s[h] = (kk * m).astype(BF16)
            dom_s[h] = (do * m).astype(BF16)
        dkT_s[...] = jnp.zeros_like(dkT_s)
        dvT_s[...] = jnp.zeros_like(dvT_s)
        u_upto, u_before = _pair_matrix("upto"), _pair_matrix("before")

        def both(inv, win, st, mask, kw):
            keys = pl.ds(pl.multiple_of(win * KW, KW), kw)
            lg = [_sb_logits(jnp.dot(inv[h][0], kT_s[win, :, :kw], preferred_element_type=F32), mask) for h in range(2)]
            s_lf = [_block_sums(lg[h][1], u_upto, st[3 * h], False) for h in range(2)]
            ws, ews = [], []
            for h in range(2):
                w = jnp.exp(lg[h][0] - s_lf[h][0])
                if mask is not None:
                    w = jnp.where(mask, w, 0.0)
                ws.append(w)
                ews.append(jnp.dot(inv[h][2], vT_s[win, :, :kw], preferred_element_type=F32) * w)
            s_e = [_block_sums(ews[h], u_before, st[3 * h + 1], False, terms=1) for h in range(2)]
            out, dk, dv = (), None, None
            for h in range(2):
                sig = jnp.exp(lg[h][0])
                dz = ews[h] - sig * (ews[h] + s_e[h][0])
                if mask is not None:
                    dz = jnp.where(mask, dz, 0.0)
                dzb = dz.astype(BF16)
                out += (s_lf[h][1], s_e[h][1], st[3 * h + 2] + jnp.dot(dzb, km_s[h, keys, :], preferred_element_type=F32))
                dkh = jnp.dot(inv[h][1], dzb, preferred_element_type=F32)
                dvh = jnp.dot(inv[h][3], ws[h].astype(BF16), preferred_element_type=F32)
                dk, dv = (dkh, dvh) if h == 0 else (dk + dkh, dv + dvh)
            dkT_s[win, :, :kw] += dk
            dvT_s[win, :, :kw] += dv
            return out

        def qtile(iq, last, kw):
            rows = pl.ds(pl.multiple_of(iq * TQ, TQ), TQ)
            mask = _causal_mask(iq * TQ, last * KW, kw)
            tt = tot_ref[rows, :]
            inv, neg_total = [], []
            for h, m in enumerate((m0, m1)):
                qh, doh = qm_s[h, rows, :], dom_s[h, rows, :]
                neg_total.append(jnp.sum(tt * m, axis=-1, keepdims=True) * (-1.0 / HEAD_DIM))
                inv.append((qh, qh.astype(F32).T.astype(BF16), doh, doh.astype(F32).T.astype(BF16)))

            z1, zq = jnp.zeros((TQ, 1), F32), jnp.zeros((TQ, 2 * HEAD_DIM), F32)
            st = lax.fori_loop(0, last, lambda win, st: both(inv, win, st, None, KW), (neg_total[0], z1, zq, neg_total[1], z1, zq))
            st = both(inv, last, st, mask, kw)
            dqn_s[rows, :] = st[2] + st[5]

        def qtiles_of_window(a, _):
            for sub in range(KW // TQ):
                qtile(a * (KW // TQ) + sub, a, (sub + 1) * TQ)
            return 0

        lax.fori_loop(0, S // KW, qtiles_of_window, 0)
        dkn = jnp.concatenate([dkT_s[w].T for w in range(nwin)], axis=0)
        dq, dqg = _head_norm_bwd(dqn_s[...] * scale, qn, qr, qg_ref[...], m0, m1)
        dk, dkg = _head_norm_bwd(dkn, kn, kr, kg_ref[...], m0, m1)
        dq_ref[...] = dq
        dk_ref[...] = dk
        dv_ref[...] = jnp.concatenate([dvT_s[w].T for w in range(nwin)], axis=0)
        dqg_ref[...] = dqg
        dkg_ref[...] = dkg

    assert S % KW == 0 and KW % TQ == 0
    nwin = S // KW
    blk = (S, 2 * HEAD_DIM)
    tblk = (nwin, 2 * HEAD_DIM, KW)
    gblk = (None, None, 1, 2 * HEAD_DIM)
    dq, dk, dv, dqg, dkg = pl.pallas_call(
        body, name=name, grid=(n_ex, nhp),
        in_specs=[pl.BlockSpec(blk, lambda e, h: (e, h)), pl.BlockSpec(blk, lambda e, h: (e, h)),
                  pl.BlockSpec(blk, lambda e, h: (e, h + nhp)),
                  pl.BlockSpec(blk, lambda e, h: (e, h)), pl.BlockSpec(blk, lambda e, h: (e, h)),
                  pl.BlockSpec((1, 2 * HEAD_DIM), lambda e, h: (0, 0)), pl.BlockSpec((1, 2 * HEAD_DIM), lambda e, h: (0, 0))],
        out_specs=[pl.BlockSpec(blk, lambda e, h: (e, h))] * 3 + [pl.BlockSpec(gblk, lambda e, h: (e, h, 0, 0))] * 2,
        out_shape=[jax.ShapeDtypeStruct((T, D), F32)] * 3 + [jax.ShapeDtypeStruct((n_ex, nhp, 1, 2 * HEAD_DIM), F32)] * 2,
        scratch_shapes=[pltpu.VMEM(tblk, BF16), pltpu.VMEM(tblk, BF16),
                        pltpu.VMEM((2,) + blk, BF16), pltpu.VMEM((2,) + blk, BF16), pltpu.VMEM((2,) + blk, BF16),
                        pltpu.VMEM(blk, F32), pltpu.VMEM(tblk, F32), pltpu.VMEM(tblk, F32)],
        compiler_params=_cp(("parallel", "parallel")),
    )(q, kv, kv, tot, do, qg, kg)
    return dq, dk, dv, dqg, dkg


def _place():
    return lax.axis_index("x"), lax.axis_index("y"), lax.axis_index("c")


def _all_gather8(x_shard, *, name):
    m_per, n = x_shard.shape

    def body(x_ref, out_ref, send_sems, recv_sems, local_sem):
        x, y, c = _place()
        me, sibling = (x, y, c), (x, y, 1 - c)
        chips = [(1 - x, y), (x, 1 - y), (1 - x, 1 - y)]

        def rows(px, py, pc):
            return out_ref.at[pl.ds((4 * px + 2 * py + pc) * m_per, m_per), :]

        def copy(k, block, to, src=None):
            return pltpu.make_async_remote_copy(
                src_ref=rows(*block) if src is None else src, dst_ref=rows(*block),
                send_sem=send_sems.at[k], recv_sem=recv_sems.at[k], device_id=to, device_id_type=MESH)

        mine = pltpu.make_async_copy(x_ref, rows(*me), local_sem)
        mine.start()
        first = [copy(0, me, sibling, src=x_ref)]
        first += [copy(1 + j, me, (*chip, c), src=x_ref) for j, chip in enumerate(chips)]
        for cp in first:
            cp.start()
        passed = [copy(4 + j, (*chip, c), sibling) for j, chip in enumerate(chips)]
        for j, chip in enumerate(chips):
            copy(1 + j, (*chip, c), me).wait_recv()
            passed[j].start()
        copy(0, sibling, me).wait_recv()
        for j, chip in enumerate(chips):
            copy(4 + j, (*chip, 1 - c), me).wait_recv()
        for cp in first + passed:
            cp.wait_send()
        mine.wait()

    return pl.pallas_call(
        body, name=name, out_shape=jax.ShapeDtypeStruct((8 * m_per, n), x_shard.dtype),
        in_specs=[pl.BlockSpec(memory_space=pltpu.VMEM)], out_specs=pl.BlockSpec(memory_space=pltpu.VMEM),
        scratch_shapes=[pltpu.SemaphoreType.DMA((7,)), pltpu.SemaphoreType.DMA((7,)), pltpu.SemaphoreType.DMA],
        compiler_params=pltpu.CompilerParams(vmem_limit_bytes=VMEM_LIMIT),
    )(x_shard)


def _sibling_sum_half(x, *, name):
    R, C = x.shape
    half = R // 2
    assert half % 16 == 0

    def body(x_ref, o_ref, theirs, send_sem, recv_sem):
        px, py, pc = _place()
        cp = pltpu.make_async_remote_copy(src_ref=x_ref, dst_ref=theirs, send_sem=send_sem, recv_sem=recv_sem,
                                          device_id=(px, py, 1 - pc), device_id_type=MESH)
        cp.start()
        cp.wait()
        rows = pl.ds(pl.multiple_of(pc * half, 8), half)
        o_ref[...] = (x_ref[rows, :] + theirs[rows, :]).astype(BF16)

    return pl.pallas_call(
        body, name=name, out_shape=jax.ShapeDtypeStruct((half, C), BF16),
        in_specs=[pl.BlockSpec(memory_space=pltpu.VMEM)], out_specs=pl.BlockSpec(memory_space=pltpu.VMEM),
        scratch_shapes=[pltpu.VMEM((R, C), x.dtype), pltpu.SemaphoreType.DMA, pltpu.SemaphoreType.DMA],
        compiler_params=pltpu.CompilerParams(vmem_limit_bytes=VMEM_LIMIT),
    )(x)


def _sum_blocks(x, n, *, name):
    R = x.shape[0] // n

    def body(x_ref, o_ref):
        acc = x_ref[pl.ds(0, R), :].astype(F32)
        for k in range(1, n):
            acc = acc + x_ref[pl.ds(k * R, R), :].astype(F32)
        o_ref[...] = acc

    return pl.pallas_call(body, name=name, out_shape=jax.ShapeDtypeStruct((R, x.shape[1]), F32),
                          compiler_params=pltpu.CompilerParams(vmem_limit_bytes=VMEM_LIMIT))(x)


def _colsum(x, *, name):
    def body(x_ref, o_ref):
        o_ref[...] = jnp.sum(x_ref[...], axis=0, keepdims=True)
    return pl.pallas_call(body, name=name, out_shape=jax.ShapeDtypeStruct((1, x.shape[1]), x.dtype))(x)


ANY = pl.BlockSpec(memory_space=pl.ANY)


class _ChipExchange:
    SCRATCH = [pltpu.SemaphoreType.DMA((3,)), pltpu.SemaphoreType.DMA((3,)), pltpu.SemaphoreType.DMA]

    @staticmethod
    def out_shape(src, scatter):
        return jax.ShapeDtypeStruct(((4,) + tuple(src.shape[1:])) if scatter else ((4, 2) + tuple(src.shape[1:])), src.dtype)

    def __init__(self, src_ref, out_ref, send_sems, recv_sems, local_sem, scatter):
        x, y, c = _place()
        myj = 2 * x + y
        chips = [(1 - x, y), (x, 1 - y), (1 - x, 1 - y)]

        def slot(j):
            return out_ref.at[j] if scatter else out_ref.at[j, c]

        def piece(j):
            return src_ref.at[j] if scatter else src_ref.at[c]

        self.mine = pltpu.make_async_copy(piece(myj), slot(myj), local_sem)
        self.sends = [pltpu.make_async_remote_copy(
            src_ref=piece(2 * cx + cy), dst_ref=slot(myj), send_sem=send_sems.at[k], recv_sem=recv_sems.at[k],
            device_id=(cx, cy, c), device_id_type=MESH) for k, (cx, cy) in enumerate(chips)]
        self.recvs = [pltpu.make_async_remote_copy(
            src_ref=slot(2 * cx + cy), dst_ref=slot(2 * cx + cy), send_sem=send_sems.at[k], recv_sem=recv_sems.at[k],
            device_id=(cx, cy, c), device_id_type=MESH) for k, (cx, cy) in enumerate(chips)]

    def start(self):
        self.mine.start()
        for cp in self.sends:
            cp.start()

    def wait(self):
        for cp in self.recvs:
            cp.wait_recv()
        for cp in self.sends:
            cp.wait_send()
        self.mine.wait()


def _sibling_fill(buf, *, axis, name):
    def half(ref, h):
        return ref.at[h] if axis == 0 else ref.at[:, h]

    def body(in_ref, out_ref, send_sem, recv_sem):
        x, y, c = _place()
        cp = pltpu.make_async_remote_copy(src_ref=half(out_ref, c), dst_ref=half(out_ref, c), send_sem=send_sem, recv_sem=recv_sem,
                                          device_id=(x, y, 1 - c), device_id_type=MESH)
        cp.start()
        pltpu.make_async_remote_copy(src_ref=half(out_ref, 1 - c), dst_ref=half(out_ref, 1 - c), send_sem=send_sem, recv_sem=recv_sem,
                                     device_id=(x, y, 1 - c), device_id_type=MESH).wait_recv()
        cp.wait_send()

    return pl.pallas_call(
        body, name=name, out_shape=jax.ShapeDtypeStruct(buf.shape, buf.dtype), in_specs=[ANY], out_specs=ANY,
        input_output_aliases={0: 0}, scratch_shapes=[pltpu.SemaphoreType.DMA, pltpu.SemaphoreType.DMA],
    )(buf)


def _sibling_swap_half(g, *, name):
    def body(g_ref, out_ref, send_sem, recv_sem):
        x, y, c = _place()
        cp = pltpu.make_async_remote_copy(src_ref=g_ref.at[:, 1 - c], dst_ref=out_ref, send_sem=send_sem, recv_sem=recv_sem,
                                          device_id=(x, y, 1 - c), device_id_type=MESH)
        cp.start()
        cp.wait()

    return pl.pallas_call(
        body, name=name, out_shape=jax.ShapeDtypeStruct((g.shape[0],) + g.shape[2:], g.dtype), in_specs=[ANY], out_specs=ANY,
        scratch_shapes=[pltpu.SemaphoreType.DMA, pltpu.SemaphoreType.DMA],
    )(g)


def _add_my_half(g, b, cidx, *, name, tr=1024):
    n, _, R, C = g.shape
    tr = max(t for t in range(16, tr + 1, 16) if R % t == 0)

    def body(c_ref, g_ref, b_ref, o_ref):
        o_ref[...] = (g_ref[...] + b_ref[...]).astype(o_ref.dtype)

    return pl.pallas_call(
        body, name=name, out_shape=jax.ShapeDtypeStruct((n, R, C), BF16),
        grid_spec=pltpu.PrefetchScalarGridSpec(
            num_scalar_prefetch=1, grid=(n, R // tr),
            in_specs=[pl.BlockSpec((None, None, tr, C), lambda j, i, c: (j, c[0], i, 0)),
                      pl.BlockSpec((None, tr, C), lambda j, i, c: (j, i, 0))],
            out_specs=pl.BlockSpec((None, tr, C), lambda j, i, c: (j, i, 0))),
        compiler_params=_cp(("parallel", "parallel")),
    )(cidx, g, b)


def _sum4_into_half(q, cidx, *, name, tr=1024):
    _, R, C = q.shape
    tr = max(t for t in range(16, tr + 1, 16) if R % t == 0)

    def body(c_ref, q_ref, o_ref):
        o_ref[...] = ((q_ref[0].astype(F32) + q_ref[1].astype(F32)) + q_ref[2].astype(F32)) + q_ref[3].astype(F32)

    return pl.pallas_call(
        body, name=name, out_shape=jax.ShapeDtypeStruct((2, R, C), F32),
        grid_spec=pltpu.PrefetchScalarGridSpec(
            num_scalar_prefetch=1, grid=(R // tr,),
            in_specs=[pl.BlockSpec((4, tr, C), lambda i, c: (0, i, 0))],
            out_specs=pl.BlockSpec((None, tr, C), lambda i, c: (c[0], i, 0))),
        compiler_params=_cp(("parallel",)),
    )(cidx, q)


def _pack_rows(parts, width=1024, row_multiple=8):
    rows, spans, r0 = [], [], 0
    for p in parts:
        n = p.size
        nr = 8 * (-(-n // (8 * width)))
        flat = p.reshape(-1)
        if nr * width != n:
            flat = jnp.pad(flat, (0, nr * width - n))
        rows.append(flat.reshape(nr, width))
        spans.append((r0, nr, n, p.shape))
        r0 += nr
    if r0 % row_multiple:
        rows.append(jnp.zeros((row_multiple - r0 % row_multiple, width), parts[0].dtype))
    return jnp.concatenate(rows, axis=0), spans


def _unpack_rows(buf, spans):
    return [buf[r0:r0 + nr].reshape(-1)[:n].reshape(shape) for (r0, nr, n, shape) in spans]


def kernel(x, c, ada_w, ada_b, mix_norm_g, mlp_norm_g, mlp_w1, mlp_w2, s5_a_re, s5_a_im, s5_log_dt, s5_b_re, s5_b_im, s5_c_re, s5_c_im, s5_d, s5_w_glu, kv_ada_w, kv_ada_b, kv_norm_g, w_kv, k_norm_g, sb_w_q, q_norm_g, sb_w_o, loss_target, m_ada_w, m_ada_b, m_mix_norm_g, m_mlp_norm_g, m_mlp_w1, m_mlp_w2, m_s5_a_re, m_s5_a_im, m_s5_log_dt, m_s5_b_re, m_s5_b_im, m_s5_c_re, m_s5_c_im, m_s5_d, m_s5_w_glu, m_kv_ada_w, m_kv_ada_b, m_kv_norm_g, m_w_kv, m_k_norm_g, m_sb_w_q, m_q_norm_g, m_sb_w_o, v_ada_w, v_ada_b, v_mix_norm_g, v_mlp_norm_g, v_mlp_w1, v_mlp_w2, v_s5_a_re, v_s5_a_im, v_s5_log_dt, v_s5_b_re, v_s5_b_im, v_s5_c_re, v_s5_c_im, v_s5_d, v_s5_w_glu, v_kv_ada_w, v_kv_ada_b, v_kv_norm_g, v_w_kv, v_k_norm_g, v_sb_w_q, v_q_norm_g, v_sb_w_o):
    E, S, D = x.shape
    T = E * S
    FF = 4 * D
    NB = 8 * E
    px, py, pc = _place()
    chip = 2 * px + py
    dev = 4 * px + 2 * py + pc
    cidx = jnp.reshape(pc, (1,)).astype(jnp.int32)
    x0 = x.reshape(T, D)
    tgt = loss_target.reshape(T, D)

    nc_rows, nd = c.size // 128, s5_d.size // 128
    cd = jnp.concatenate([c.reshape(nc_rows, 128), jnp.pad(s5_d.reshape(nd, 128), ((0, 8 - nd), (0, 0)))], axis=0)
    cd_all = _all_gather8(cd, name="ag_c_d").reshape(8, nc_rows + 8, 128)
    c_all = cd_all[:, :nc_rows].reshape(NB, D)
    d_full = cd_all.reshape(4, 2, nc_rows + 8, 128)[:, 0, nc_rows:nc_rows + nd].reshape(1, D)
    sc_all = (c_all * _sigmoid(c_all)).astype(BF16)
    wa = ada_w.shape[2]
    wk = kv_ada_w.shape[1]
    m_sh = jnp.concatenate([_mm(sc_all, _Layer(ada_w, 0), "nn", name="ada0", tn=256),
                            _mm(sc_all, _Layer(ada_w, 1), "nn", name="ada1", tn=256),
                            _mm(sc_all, kv_ada_w, "nn", name="ada_kv", tn=256)], axis=1)
    m_half = lax.dynamic_slice_in_dim(m_sh, pc * (NB // 2), NB // 2, axis=0)
    m_all = _all_gather8(m_half, name="ag_m").reshape(4, NB, 2 * wa + wk)
    mods = []
    for l in range(2):
        full = jnp.transpose(m_all[:, :, l * wa:(l + 1) * wa], (1, 0, 2)).reshape(NB, 6 * D) + ada_b[l]
        mine = lax.dynamic_slice_in_dim(full, E * dev, E, axis=0)
        mods.append([mine[:, i * D:(i + 1) * D].reshape(E, 1, D) for i in range(6)])
    full = jnp.transpose(m_all[:, :, 2 * wa:], (1, 0, 2)).reshape(NB, 2 * D) + kv_ada_b
    mine = lax.dynamic_slice_in_dim(full, E * dev, E, axis=0)
    kv_sh, kv_sc = [mine[:, i * D:(i + 1) * D].reshape(E, 1, D) for i in range(2)]

    wpack_a = jnp.concatenate([mlp_w1[0], mlp_w2[0], jnp.concatenate([s5_w_glu[0], w_kv], axis=1), sb_w_q[0]], axis=0).astype(BF16)
    wpack_b = jnp.concatenate([mlp_w1[1], mlp_w2[1], sb_w_o[0]], axis=0).astype(BF16)
    RA, RB = wpack_a.shape[0], wpack_b.shape[0]
    RW = RA + RB

    tm = min(2048, S)
    tm_res = min(1024, S)
    gbuf = [jax.ShapeDtypeStruct((4, RW, D), F32)]

    def grad_mm(act, dout, kind, roff, nr, c0, nc, name, transposed=False):
        gbuf[0] = _mm(act, dout, "nn" if transposed else "tn", name=name, tm=1024, tk=2048,
                      into=_Sharded(gbuf[0], kind, roff, nr, c0, nc))

    def mlp_fwd(xa, l, mod, normed=None):
        sh_m, sc_m, g_m = mod[3], mod[4], mod[5]
        h, h_t = normed if normed is not None else _norm_mod_fwd(
            xa, mlp_norm_g[l:l + 1], sh_m, sc_m, n_ex=E, out_dtype=BF16, name=f"mlp_norm{l}", with_transpose=True)

        def relu_sq(acc):
            ra = jnp.maximum(acc, 0.0)
            return ra * ra, ra
        r, ra = _mm(h, W1[l], "nn", name=f"mlp_up{l}", out_dtypes=(BF16, BF16), tm=tm, epilogue=relu_sq)
        xb, ff = _mm(r, W2[l], "nn", name=f"mlp_down{l}", out_dtypes=(F32, F32), tm=tm_res,
                     extras=[_mn_extra(xa), _vec_extra(g_m, S)],
                     epilogue=lambda acc, xat, gt: (xat + gt * acc, acc))
        return xb, (h_t, r, ra, ff)

    def mlp_bwd(dxb, dff, xa, l, mod, saved, gated=None):
        sc_m = mod[4]
        h_t, r, ra, _ = saved
        da = _mm(dff, W2[l], "nt", name=f"mlp_down_dx{l}", out_dtypes=(BF16,), tm=tm, extras=[_mn_extra(ra)],
                 epilogue=lambda acc, rat: (acc * (2.0 * rat.astype(F32)),))
        grad_mm(r, dff, "rows", (2 + l) * D, D, 0, D, f"mlp_down_dw{l}")
        dh = _mm(da, W1[l], "nt", name=f"mlp_up_dx{l}", tm=tm)
        grad_mm(h_t, da, "cols", l * D, D, 0, D, f"mlp_up_dw{l}", transposed=True)
        return _norm_mod_bwd(xa, dh, dxb, mlp_norm_g[l:l + 1], sc_m, n_ex=E, name=f"mlp_norm_bwd{l}", gated=gated)

    ab_re, ab_im, bb_re, bb_im = _s5_disc(s5_a_re[0], s5_a_im[0], s5_log_dt[0], s5_b_re[0], s5_b_im[0])
    cf, cr = _s5_consts(ab_re, ab_im)
    Wb, Wc = _s5_blockdiag(bb_re, bb_im, s5_c_re[0], s5_c_im[0])
    ng = D // U_LANES

    mod0, mod1 = mods
    h0 = _norm_mod_fwd(x0, mix_norm_g[0:1], mod0[0], mod0[1], n_ex=E, out_dtype=F32, name="mix_norm0")
    y, gy, gy_t, s5_states, wfull_a = _s5_fwd(h0, Wb, Wc, cf, d_full, wpack_a.reshape(2, RA // 2, D), n_ex=E, name="s5_fwd")
    wfull_a = _sibling_fill(wfull_a, axis=1, name="wgather_a_d2d").reshape(4, RA, D)

    W1 = [_Sharded(wfull_a, "cols", 0, D, 0, D), None]
    W2 = [_Sharded(wfull_a, "rows", D, D, 0, D), None]
    Wglu = _Sharded(wfull_a, "cols", 2 * D, D, 0, D // 2)
    Wkv = _Sharded(wfull_a, "cols", 2 * D, D, D // 2, D // 2)
    Wq = _Sharded(wfull_a, "rows", 3 * D, D // 4, 0, D)
    vg = _mm(gy, Wglu, "nn", name="glu_up", tm=tm)
    def glu_gate_and_norm(v, g, xt, ga, sh, sc, gain):
        x1t = xt + ga * (v * _sigmoid(g))
        n = x1t * lax.rsqrt(jnp.mean(x1t * x1t, axis=-1, keepdims=True) + EPS)
        h = (n * gain) * (1.0 + sc) + sh
        return [x1t, h, h], []
    (x1, hm0, hm0_t), _ = _rowwise(glu_gate_and_norm, [(vg, D, 0), (vg, D, 1), (x0, D, 0)], [mod0[2], mod0[3], mod0[4]],
                                   [mlp_norm_g[0:1]], [(D, F32), (D, BF16), (D, BF16, True)], [], n_ex=E, name="glu_gate_norm")
    x2, saved_mlp0 = mlp_fwd(x1, 0, mod0, normed=(hm0, hm0_t))

    hkv, hkv_t, h1, h1_t = _norm_mod_fwd_pair(x2, (kv_norm_g.reshape(1, D), kv_sh, kv_sc), (mix_norm_g[1:2], mod1[0], mod1[1]),
                                              n_ex=E, name="kv_mix_norm")
    kvf = _mm(hkv, Wkv, "nn", name="kv_proj", tm=tm)
    qf = _mm(h1, Wq, "nn", name="q_proj", tm=tm)
    qg2 = jnp.tile(q_norm_g.reshape(1, HEAD_DIM), (1, 2))
    kg2 = jnp.tile(k_norm_g.reshape(1, HEAD_DIM), (1, 2))
    o, lf_tot, o_t, wfull_b = _attn_fwd(qf, kvf, qg2, kg2, wpack_b.reshape(2, RB // 2, D), n_ex=E, name="attn_fwd")
    wfull_b = _sibling_fill(wfull_b, axis=1, name="wgather_b_d2d").reshape(4, RB, D)
    W1[1] = _Sharded(wfull_b, "cols", 0, D, 0, D)
    W2[1] = _Sharded(wfull_b, "rows", D, D, 0, D)
    Wo = _Sharded(wfull_b, "rows", 2 * D, D // 4, 0, D)
    x3, mix1 = _mm(o, Wo, "nn", name="o_proj", out_dtypes=(F32, F32), tm=tm_res,
                   extras=[_mn_extra(x2), _vec_extra(mod1[2], S)],
                   epilogue=lambda acc, xat, gt: (xat + gt * acc, acc))
    x4, saved_mlp1 = mlp_fwd(x3, 1, mod1)

    def loss_fn(xt, tt, fft, gmt):
        dx = (xt - tt) * (1.0 / D)
        dff, dgm = _gated(dx, fft, gmt)
        return [dx, dff], [_csum(jnp.square(xt - tt)) * (0.5 / D), dgm]
    (dx4, dff1), (lsum, dgm1) = _rowwise(loss_fn, [(x4, D, 0), (tgt, D, 0), (saved_mlp1[3], D, 0)], [mod1[5]], [],
                                         [(D, F32), (D, BF16)], [D, D], n_ex=E, name="loss")
    loss = lax.psum(jnp.sum(lsum), ("x", "y", "c"))

    (dx3, dmix1), (dsh_m1, dsc_m1, dg_mlp1, dga1) = mlp_bwd(dx4, dff1, x3, 1, mod1, saved_mlp1, gated=([(mix1, D, 0)], mod1[2], _gated, D))
    do = _mm(dmix1, Wo, "nt", name="o_proj_dx", tm=tm)
    grad_mm(o_t, dmix1, "rows", 5 * D + D // 4, D // 4, 0, D, "o_proj_dw", transposed=True)
    dq, dk, dv, dqg, dkg = _attn_bwd(qf, kvf, lf_tot, do, qg2, kg2, n_ex=E, name="attn_bwd")
    dh1 = _mm(dq, Wq, "nt", name="q_proj_dx", tm=tm)
    grad_mm(h1_t, dq, "rows", 5 * D, D // 4, 0, D, "q_proj_dw", transposed=True)
    dkv = jnp.concatenate([dk, dv], axis=1)
    dhkv = _mm(dkv, Wkv, "nt", name="kv_proj_dx", tm=tm)
    grad_mm(hkv_t, dkv, "cols", 4 * D, D, D // 2, D // 2, "kv_proj_dw", transposed=True)
    (dx2, dff0), (dsh_a1, dsc_a1, dg_mix1, dkv_sh, dkv_sc, dg_kv, dgm0) = _norm_mod_bwd_pair(
        x2, dh1, dhkv, dx3, (mix_norm_g[1:2], mod1[1]), (kv_norm_g.reshape(1, D), kv_sc),
        ([(saved_mlp0[3], D, 0)], mod0[5], _gated, D), n_ex=E, name="kv_mix_norm_bwd")

    def glu_bwd(d, v, g, ga):
        sg = _sigmoid(g)
        dm = ga * d
        return jnp.concatenate([dm * sg, dm * v * sg * (1.0 - sg)], axis=1).astype(BF16), _csum(d * (v * sg))
    (dx1, dvg), (dsh_m0, dsc_m0, dg_mlp0, dga0) = mlp_bwd(dx2, dff0, x1, 0, mod0, saved_mlp0,
                                                          gated=([(vg, D, 0), (vg, D, 1)], mod0[2], glu_bwd, 2 * D))
    dgy = _mm(dvg, Wglu, "nt", name="glu_up_dx", tm=tm)
    grad_mm(gy_t, dvg, "cols", 4 * D, D, 0, D // 2, "glu_up_dw", transposed=True)

    gpack = gbuf[0].reshape(4, 2, RW // 2, D)
    theirs = _sibling_swap_half(gpack, name="gscatter_d2d")
    chip_sum = _add_my_half(gpack, theirs, cidx, name="gscatter_add")
    dh0, dWb, dWc, dab, dd, from_chips = _s5_bwd(h0, y, dgy, s5_states, Wb, Wc, cr, d_full, chip_sum, n_ex=E, name="s5_bwd")
    ghalf = _sum4_into_half(from_chips, cidx, name="gscatter_sum")
    gsh = _sibling_fill(ghalf, axis=0, name="gscatter_fill").reshape(RW, D)
    (gx,), (dsh_a0, dsc_a0, dg_mix0) = _norm_mod_bwd(x0, dh0, dx1, mix_norm_g[0:1], mod0[1], n_ex=E, name="mix_norm_bwd0")
    grad_x = gx.reshape(E, S, D)

    dm_mine = jnp.concatenate([t.reshape(E, D) for t in
                               (dsh_a0, dsc_a0, dga0, dsh_m0, dsc_m0, dgm0, dsh_a1, dsc_a1, dga1, dsh_m1, dsc_m1, dgm1, dkv_sh, dkv_sc)], axis=1)
    dm_all = _all_gather8(dm_mine.reshape(8, -1), name="ag_dm").reshape(NB, 14 * D)
    sc_f32 = c_all * _sigmoid(c_all)
    g_ada_w = jax.ShapeDtypeStruct(ada_w.shape, F32)
    for l in range(2):
        g_ada_w = _mm(sc_f32, lax.dynamic_slice_in_dim(dm_all, l * 6 * D + chip * wa, wa, axis=1), "tn", name=f"ada_dw{l}", tn=256,
                      into=_Layer(g_ada_w, l))
    g_kv_ada_w = _mm(sc_f32, lax.dynamic_slice_in_dim(dm_all, 12 * D + chip * wk, wk, axis=1), "tn", name="ada_kv_dw", tn=256)
    db_all = _colsum(dm_all, name="ada_db")
    g_ada_b = db_all[0, :12 * D].reshape(2, 6 * D)
    g_kv_ada_b = db_all[0, 12 * D:]

    dWb_re, dWb_im, dC_re, dC_im = _s5_unblock(dWb, dWc)
    small_parts = [dg_mix0.sum(0), dg_mix1.sum(0), dg_mlp0.sum(0), dg_mlp1.sum(0), dg_kv.sum(0),
                   dqg.sum((0, 1, 2)).reshape(2, HEAD_DIM).sum(0), dkg.sum((0, 1, 2)).reshape(2, HEAD_DIM).sum(0),
                   dd[:, 0, :], dab[:, 0, :], dab[:, 1, :], dWb_re, dWb_im, dC_re, dC_im]
    spack, spans = _pack_rows(small_parts, row_multiple=32)
    chip_half = _sibling_sum_half(spack, name="small_d2d")
    ssum = _sum_blocks(_all_gather8(chip_half, name="ag_small"), 4, name="sum_small")
    (g_mix0, g_mix1, g_mlp0, g_mlp1, g_kvn, g_qn, g_kn, g_d, g_abr, g_abi, g_bbr, g_bbi, g_cre, g_cim) = _unpack_rows(ssum, spans)
    _, disc_vjp = jax.vjp(_s5_disc, s5_a_re[0], s5_a_im[0], s5_log_dt[0], s5_b_re[0], s5_b_im[0])
    g_are, g_aim, g_ldt, g_bre, g_bim = disc_vjp((g_abr.reshape(ab_re.shape), g_abi.reshape(ab_im.shape), g_bbr, g_bbi))
    g_s5d = lax.dynamic_slice_in_dim(g_d.reshape(1, D), chip * s5_d.shape[1], s5_d.shape[1], axis=1)

    def upd_big(w, m, v, roff, cb, name):
        shape = w.shape
        W = shape[-1]
        d_, m_, v_, g_ = _adamw2d(w.reshape(-1, W), gsh, m.reshape(-1, W), v.reshape(-1, W), name=name, g_roff=roff, g_cb=cb)
        return [t.reshape(shape) for t in (g_, d_, m_, v_)]

    def upd_own(w, g, m, v, name):
        shape = w.shape
        W = shape[-1]
        d_, m_, v_, g_ = _adamw2d(w.reshape(-1, W), g.reshape(-1, W), m.reshape(-1, W), v.reshape(-1, W), name=name)
        return [t.reshape(shape) for t in (g_, d_, m_, v_)]

    res = {}
    res["ada_w"] = upd_own(ada_w, g_ada_w, m_ada_w, v_ada_w, "adam_ada_w")
    res["kv_ada_w"] = upd_own(kv_ada_w, g_kv_ada_w, m_kv_ada_w, v_kv_ada_w, "adam_kv_ada_w")
    res["mlp_w1"] = upd_big(mlp_w1, m_mlp_w1, v_mlp_w1, 0, 0, "adam_w1")
    res["mlp_w2"] = upd_big(mlp_w2, m_mlp_w2, v_mlp_w2, 2 * D, 0, "adam_w2")
    res["s5_w_glu"] = upd_big(s5_w_glu, m_s5_w_glu, v_s5_w_glu, 4 * D, 0, "adam_glu")
    res["w_kv"] = upd_big(w_kv, m_w_kv, v_w_kv, 4 * D, 1, "adam_wkv")
    res["sb_w_q"] = upd_big(sb_w_q, m_sb_w_q, v_sb_w_q, 5 * D, 0, "adam_wq")
    res["sb_w_o"] = upd_big(sb_w_o, m_sb_w_o, v_sb_w_o, 5 * D + D // 4, 0, "adam_wo")

    small = {
        "ada_b": (ada_b, g_ada_b, m_ada_b, v_ada_b),
        "mix_norm_g": (mix_norm_g, jnp.stack([g_mix0, g_mix1]), m_mix_norm_g, v_mix_norm_g),
        "mlp_norm_g": (mlp_norm_g, jnp.stack([g_mlp0, g_mlp1]), m_mlp_norm_g, v_mlp_norm_g),
        "s5_a_re": (s5_a_re, g_are[None], m_s5_a_re, v_s5_a_re),
        "s5_a_im": (s5_a_im, g_aim[None], m_s5_a_im, v_s5_a_im),
        "s5_log_dt": (s5_log_dt, g_ldt[None], m_s5_log_dt, v_s5_log_dt),
        "s5_b_re": (s5_b_re, g_bre[None], m_s5_b_re, v_s5_b_re),
        "s5_b_im": (s5_b_im, g_bim[None], m_s5_b_im, v_s5_b_im),
        "s5_c_re": (s5_c_re, g_cre[None], m_s5_c_re, v_s5_c_re),
        "s5_c_im": (s5_c_im, g_cim[None], m_s5_c_im, v_s5_c_im),
        "s5_d": (s5_d, g_s5d, m_s5_d, v_s5_d),
        "kv_ada_b": (kv_ada_b, g_kv_ada_b, m_kv_ada_b, v_kv_ada_b),
        "kv_norm_g": (kv_norm_g, g_kvn, m_kv_norm_g, v_kv_norm_g),
        "k_norm_g": (k_norm_g, g_kn, m_k_norm_g, v_k_norm_g),
        "q_norm_g": (q_norm_g, g_qn.reshape(q_norm_g.shape), m_q_norm_g, v_q_norm_g),
    }
    names = list(small)
    packs = [_pack_rows([small[n][i].reshape(small[n][0].shape) for n in names]) for i in range(4)]
    sp = packs[0][1]
    d_, m_, v_, g_ = _adamw2d(packs[0][0], packs[1][0], packs[2][0], packs[3][0], name="adam_small")
    for n, gg, dd_, mm_, vv_ in zip(names, _unpack_rows(g_, sp), _unpack_rows(d_, sp), _unpack_rows(m_, sp), _unpack_rows(v_, sp)):
        res[n] = [gg, dd_, mm_, vv_]

    order = ["ada_w", "ada_b", "mix_norm_g", "mlp_norm_g", "mlp_w1", "mlp_w2", "s5_a_re", "s5_a_im", "s5_log_dt", "s5_b_re", "s5_b_im",
             "s5_c_re", "s5_c_im", "s5_d", "s5_w_glu", "kv_ada_w", "kv_ada_b", "kv_norm_g", "w_kv", "k_norm_g", "sb_w_q", "q_norm_g", "sb_w_o"]
    return (loss, grad_x, *[res[n][0] for n in order], *[res[n][1] for n in order], *[res[n][2] for n in order], *[res[n][3] for n in order])
```

```python
import functools
import math

import jax
import jax.numpy as jnp
from jax import lax
from jax.experimental import pallas as pl
from jax.experimental.pallas import tpu as pltpu

F32 = jnp.float32
BF16 = jnp.bfloat16
EPS = 1e-6
HEAD_DIM = 64
S5_GROUP = 16
S5_STATE = 64
GROUPS_PER_STEP = 8
U_LANES = GROUPS_PER_STEP * S5_GROUP
ST_LANES = GROUPS_PER_STEP * S5_STATE
SCAN_LANES = 256
SCAN_UNROLL = 4
VMEM_LIMIT = 56 * 1024 * 1024
ADAM_LR, ADAM_B1, ADAM_B2, ADAM_EPS, ADAM_WD, ADAM_STEP = 0.001, 0.9, 0.999, 1e-08, 0.01, 10
MESH = pl.DeviceIdType.MESH


def _cp(sem):
    return pltpu.CompilerParams(dimension_semantics=sem, vmem_limit_bytes=VMEM_LIMIT)


class _Sharded:
    def __init__(self, buf, kind, roff, nr, c0, nc):
        self.buf, self.kind, self.roff, self.nr, self.c0, self.nc = buf, kind, roff, nr, c0, nc
        self.shape = (nr, 4 * nc) if kind == "cols" else (4 * nr, nc)

    def operand(self, dims, tn, tk):
        roff, nr, c0, nc = self.roff, self.nr, self.c0, self.nc
        if self.kind == "cols" and dims == "nn":
            tk = min(tk, nr)
            assert roff % tk == 0
            return nc, tk, (None, tk, nc), lambda i, j, k: (j, roff // tk + k, c0 // nc)
        if self.kind == "cols":
            tn = min(tn, nr)
            assert roff % tn == 0
            return tn, nc, (None, tn, nc), lambda i, j, k: (k, roff // tn + j, c0 // nc)
        if dims == "nn":
            tn = min(tn, nc)
            assert roff % nr == 0 and c0 % tn == 0
            return tn, nr, (None, nr, tn), lambda i, j, k: (k, roff // nr, c0 // tn + j)
        tk = min(tk, nc)
        assert roff % nr == 0 and c0 % tk == 0
        return nr, tk, (None, nr, tk), lambda i, j, k: (j, roff // nr, c0 // tk + k)

    def result(self, tm, tn):
        roff, nr, c0, nc = self.roff, self.nr, self.c0, self.nc
        if self.kind == "cols":
            tm = min(tm, nr)
            assert roff % tm == 0
            return tm, nc, (None, tm, nc), lambda i, j, k: (j, roff // tm + i, c0 // nc)
        tm, tn = min(tm, nr), min(tn, nc)
        assert roff % tm == 0 and c0 % tn == 0
        per = nr // tm
        return tm, tn, (None, tm, tn), lambda i, j, k: (i // per, roff // tm + i % per, c0 // tn + j)


class _Layer:
    def __init__(self, buf, layer):
        self.buf, self.layer, self.shape = buf, layer, tuple(buf.shape[1:])

    def operand(self, dims, tn, tk):
        assert dims == "nn"
        layer = self.layer
        return tn, tk, (None, tk, tn), lambda i, j, k: (layer, k, j)

    def result(self, tm, tn):
        layer = self.layer
        return tm, tn, (None, tm, tn), lambda i, j, k: (layer, i, j)


def _mm(a, b, dims, *, name, out_dtypes=(F32,), epilogue=None, extras=(), tm=512, tn=1024, tk=1024, into=None):
    bshape = b.shape
    if dims == "nn":
        (M, K), (_, N) = a.shape, bshape
    elif dims == "nt":
        (M, K), (N, _) = a.shape, bshape
    else:
        (K, M), (_, N) = a.shape, bshape
    tm, tn, tk = min(tm, M), min(tn, N), min(tk, K)
    b_arr = b
    if into is not None:
        assert (M, N) == into.shape and len(out_dtypes) == 1 and not isinstance(b, _Sharded)
        tm, tn, o_blk, o_map = into.result(tm, tn)
        out_specs, out_shape = [pl.BlockSpec(o_blk, o_map)], [jax.ShapeDtypeStruct(into.buf.shape, into.buf.dtype)]
    if isinstance(b, (_Sharded, _Layer)):
        tn, tk, b_blk, b_map = b.operand(dims, tn, tk)
        b_spec, b_arr = pl.BlockSpec(b_blk, b_map), b.buf
    else:
        b_spec = pl.BlockSpec((tn, tk), lambda i, j, k: (j, k)) if dims == "nt" else pl.BlockSpec((tk, tn), lambda i, j, k: (k, j))
    if into is None:
        out_specs = [pl.BlockSpec((tm, tn), lambda i, j, k: (i, j)) for _ in out_dtypes]
        out_shape = [jax.ShapeDtypeStruct((M, N), d) for d in out_dtypes]
    assert M % tm == 0 and N % tn == 0 and K % tk == 0, (M, N, K, tm, tn, tk)
    nk = K // tk
    extras = [e(tm, tn) for e in extras]
    a_spec = pl.BlockSpec((tk, tm), lambda i, j, k: (k, i)) if dims == "tn" else pl.BlockSpec((tm, tk), lambda i, j, k: (i, k))
    contract = {"nn": ((1,), (0,)), "nt": ((1,), (1,)), "tn": ((0,), (0,))}[dims]
    n_ex, n_out = len(extras), len(out_dtypes)
    chain = [into.buf] if into is not None and not isinstance(into.buf, jax.ShapeDtypeStruct) else []
    n_in = n_ex + len(chain)

    def finish(r, ex, outs):
        res = epilogue(r, *[e[...] for e in ex]) if epilogue is not None else (r,)
        for o, v in zip(outs, res):
            o[...] = v.astype(o.dtype)

    def product(a_ref, b_ref):
        return lax.dot_general(a_ref[...].astype(BF16), b_ref[...].astype(BF16), (contract, ((), ())), preferred_element_type=F32)

    def body_one(a_ref, b_ref, *rest):
        finish(product(a_ref, b_ref), rest[:n_ex], rest[n_in:])

    def body_acc(a_ref, b_ref, *rest):
        ex, outs, acc = rest[:n_ex], rest[n_in:n_in + n_out], rest[-1]
        k = pl.program_id(2)

        @pl.when(k == 0)
        def _():
            acc[...] = product(a_ref, b_ref)

        @pl.when(jnp.logical_and(k > 0, k < nk - 1))
        def _():
            acc[...] += product(a_ref, b_ref)

        @pl.when(k == nk - 1)
        def _():
            finish(acc[...] + product(a_ref, b_ref), ex, outs)

    out = pl.pallas_call(
        body_one if nk == 1 else body_acc, name=name, grid=(M // tm, N // tn, nk),
        in_specs=[a_spec, b_spec] + [pl.BlockSpec(blk, im) for (_, blk, im) in extras] + [ANY for _ in chain],
        out_specs=out_specs, out_shape=out_shape,
        input_output_aliases={2 + n_ex: 0} if chain else {},
        scratch_shapes=[] if nk == 1 else [pltpu.VMEM((tm, tn), F32)],
        compiler_params=_cp(("parallel", "parallel", "arbitrary")),
    )(a, b_arr, *[e[0] for e in extras], *chain)
    return out if n_out > 1 else out[0]


def _mn_extra(arr):
    return lambda tm, tn: (arr, (tm, tn), lambda i, j, k: (i, j))


def _vec_extra(vec, S):
    return lambda tm, tn: (vec, (None, 1, tn), lambda i, j, k: ((i * tm) // S, 0, j))


def _rowwise(fn, rows, vecs=(), consts=(), out_rows=(), out_sums=(), *, n_ex, name, tr=512):
    rows = [r if len(r) == 4 else (*r, 0) for r in rows]
    S = min(r[0].shape[0] for r in rows if r[3] == 0) // n_ex
    tr = math.gcd(tr, S)
    assert S % tr == 0
    nb = S // tr
    in_specs = []
    for (arr, w, cb, roff) in rows:
        assert roff % tr == 0
        in_specs.append(pl.BlockSpec((tr, w), functools.partial(lambda e, i, cb, ro: (e * nb + i + ro, cb), cb=cb, ro=roff // tr)))
    for v in vecs:
        in_specs.append(pl.BlockSpec((None, 1, v.shape[-1]), lambda e, i: (e, 0, 0)))
    for c in consts:
        in_specs.append(pl.BlockSpec((1, c.shape[-1]), lambda e, i: (0, 0)))
    n_in, n_or, n_os = len(in_specs), len(out_rows), len(out_sums)
    flipped = [len(o) == 3 and o[2] for o in out_rows]
    out_specs = [pl.BlockSpec((o[0], tr), lambda e, i: (0, e * nb + i)) if f else pl.BlockSpec((tr, o[0]), lambda e, i: (e * nb + i, 0))
                 for o, f in zip(out_rows, flipped)]
    out_specs += [pl.BlockSpec((None, 1, w), lambda e, i: (e, 0, 0)) for w in out_sums]
    out_shape = [jax.ShapeDtypeStruct((o[0], n_ex * S) if f else (n_ex * S, o[0]), o[1]) for o, f in zip(out_rows, flipped)]
    out_shape += [jax.ShapeDtypeStruct((n_ex, 1, w), F32) for w in out_sums]

    def body(*refs):
        ins, o_r, o_s = refs[:n_in], refs[n_in:n_in + n_or], refs[n_in + n_or:]
        ro, so = fn(*[r[...] for r in ins])
        for o, v, f in zip(o_r, ro, flipped):
            o[...] = (v.T if f else v).astype(o.dtype)
        i = pl.program_id(1)
        for o, v in zip(o_s, so):
            @pl.when(i == 0)
            def _(o=o, v=v):
                o[...] = v

            @pl.when(i > 0)
            def _(o=o, v=v):
                o[...] += v

    outs = pl.pallas_call(
        body, name=name, grid=(n_ex, nb), in_specs=in_specs, out_specs=out_specs, out_shape=out_shape,
        compiler_params=_cp(("parallel", "arbitrary")),
    )(*[r[0] for r in rows], *vecs, *consts)
    return outs[:n_or], outs[n_or:]


def _csum(x):
    return jnp.sum(x, axis=0, keepdims=True)


def _norm_mod_fwd(x, g, sh, sc, *, n_ex, out_dtype, name, with_transpose=False):
    def fn(xt, sht, sct, gt):
        r = lax.rsqrt(jnp.mean(xt * xt, axis=-1, keepdims=True) + EPS)
        h = (xt * r * gt) * (1.0 + sct) + sht
        return [h, h] if with_transpose else [h], []
    D = x.shape[1]
    outs = [(D, out_dtype), (D, out_dtype, True)] if with_transpose else [(D, out_dtype)]
    res = _rowwise(fn, [(x, D, 0)], [sh, sc], [g], outs, [], n_ex=n_ex, name=name)[0]
    return res if with_transpose else res[0]


def _norm_mod_fwd_pair(x, first, second, *, n_ex, name):
    def fn(xt, sh1, sc1, sh2, sc2, g1, g2):
        n = xt * lax.rsqrt(jnp.mean(xt * xt, axis=-1, keepdims=True) + EPS)
        h1 = (n * g1) * (1.0 + sc1) + sh1
        h2 = (n * g2) * (1.0 + sc2) + sh2
        return [h1, h1, h2, h2], []
    D = x.shape[1]
    outs = [(D, BF16), (D, BF16, True)] * 2
    return _rowwise(fn, [(x, D, 0)], [first[1], first[2], second[1], second[2]], [first[0], second[0]], outs, [], n_ex=n_ex, name=name)[0]


def _gated(dx, branch, gate):
    return (gate * dx).astype(BF16), _csum(dx * branch)


def _norm_mod_bwd(x, dh, dres, g, sc, *, n_ex, name, gated=None):
    n_rows = len(gated[0]) if gated is not None else 0

    def fn(xt, dht, drt, *rest):
        sct, gt = rest[-2], rest[-1]
        dht = dht.astype(F32)
        r = lax.rsqrt(jnp.mean(xt * xt, axis=-1, keepdims=True) + EPS)
        n = xt * r
        y = n * gt
        dy = dht * (1.0 + sct)
        dn = dy * gt
        dx = drt + r * (dn - n * jnp.mean(dn * n, axis=-1, keepdims=True))
        rows, sums = [dx], [_csum(dht), _csum(dht * y), _csum(dy * n)]
        if gated is not None:
            dbranch, dgate = gated[2](dx, *rest[:n_rows + 1])
            rows, sums = rows + [dbranch], sums + [dgate]
        return rows, sums
    D = x.shape[1]
    extra_rows, extra_vecs = (list(gated[0]), [gated[1]]) if gated is not None else ([], [])
    return _rowwise(fn, [(x, D, 0), (dh, D, 0), (dres, D, 0)] + extra_rows, extra_vecs + [sc], [g],
                    [(D, F32)] + ([(gated[3], BF16)] if gated is not None else []), [D, D, D] + ([D] if gated is not None else []),
                    n_ex=n_ex, name=name)


def _norm_mod_bwd_pair(x, dh1, dh2, dres, first, second, gated, *, n_ex, name):
    n_rows = len(gated[0])

    def fn(xt, d1, d2, drt, *rest):
        sc1, sc2, g1, g2 = rest[-4:]
        r = lax.rsqrt(jnp.mean(xt * xt, axis=-1, keepdims=True) + EPS)
        n = xt * r
        dx, sums = drt, []
        for dht, sct, gt in ((d1.astype(F32), sc1, g1), (d2.astype(F32), sc2, g2)):
            dy = dht * (1.0 + sct)
            dn = dy * gt
            dx = dx + r * (dn - n * jnp.mean(dn * n, axis=-1, keepdims=True))
            sums += [_csum(dht), _csum(dht * (n * gt)), _csum(dy * n)]
        dbranch, dgate = gated[2](dx, *rest[:n_rows + 1])
        return [dx, dbranch], sums + [dgate]
    D = x.shape[1]
    return _rowwise(fn, [(x, D, 0), (dh1, D, 0), (dh2, D, 0), (dres, D, 0)] + list(gated[0]), [gated[1], first[1], second[1]],
                    [first[0], second[0]], [(D, F32), (gated[3], BF16)], [D] * 7, n_ex=n_ex, name=name)


def _sigmoid(x):
    return 1.0 / (1.0 + jnp.exp(-x))


def _gelu(y):
    return 0.5 * y * (1.0 + jnp.tanh(0.7978845608028654 * (y + 0.044715 * y * y * y)))


def _gelu_grad(y):
    t = jnp.tanh(0.7978845608028654 * (y + 0.044715 * y * y * y))
    return 0.5 * (1.0 + t) + 0.5 * y * (1.0 - t * t) * 0.7978845608028654 * (1.0 + 3 * 0.044715 * y * y)


def _adamw_fn(w, g, m, v):
    m2 = ADAM_B1 * m + (1.0 - ADAM_B1) * g
    v2 = ADAM_B2 * v + (1.0 - ADAM_B2) * (g * g)
    m_hat = m2 / (1.0 - ADAM_B1 ** ADAM_STEP)
    v_hat = v2 / (1.0 - ADAM_B2 ** ADAM_STEP)
    delta = -ADAM_LR * (m_hat / (jnp.sqrt(v_hat) + ADAM_EPS) + ADAM_WD * w)
    return delta, m2, v2


def _adamw2d(w, g, m, v, *, name, g_roff=0, g_cb=0):
    R, W = w.shape

    def fn(wt, gt, mt, vt):
        d, m2, v2 = _adamw_fn(wt, gt, mt, vt)
        return [d, m2, v2, gt], []
    return _rowwise(fn, [(w, W, 0), (g, W, g_cb, g_roff), (m, W, 0), (v, W, 0)], [], [],
                    [(W, F32)] * 4, [], n_ex=1, name=name, tr=512 if W <= 1024 else 256)[0]


def _scan_tiles(re_ref, im_ref, cf, lane0, n_chunks, reverse, extra=None):
    L = SCAN_LANES
    lanes = pl.ds(lane0, L)
    A = [cf[i, :, lanes] for i in range(8)]
    shifts = (7, 6, 4) if reverse else (1, 2, 4)
    edge = 0 if reverse else 7

    U = SCAN_UNROLL
    n_groups = n_chunks // U

    def body(c, carry):
        first = ((n_groups - 1 - c) if reverse else c) * U
        rows = pl.ds(pl.multiple_of(first * 8, 8 * U), 8 * U)
        big_r, big_i = re_ref[rows, lanes], im_ref[rows, lanes]
        tiles = []
        for u in range(U):
            xr, xi = big_r[8 * u:8 * u + 8, :], big_i[8 * u:8 * u + 8, :]
            for idx, sft in enumerate(shifts):
                ar, ai = A[2 * idx], A[2 * idx + 1]
                rr, ri = pltpu.roll(xr, sft, 0), pltpu.roll(xi, sft, 0)
                xr, xi = xr + ar * rr - ai * ri, xi + ar * ri + ai * rr
            tiles.append((xr, xi))
        pr, pi = A[6], A[7]
        cr, ci = carry[0], carry[1]
        for u in (range(U - 1, -1, -1) if reverse else range(U)):
            xr, xi = tiles[u]
            xr, xi = xr + pr * cr - pi * ci, xi + pr * ci + pi * cr
            tiles[u] = (xr, xi)
            cr, ci = jnp.broadcast_to(xr[edge:edge + 1, :], (8, L)), jnp.broadcast_to(xi[edge:edge + 1, :], (8, L))
        re_ref[rows, lanes] = jnp.concatenate([t[0] for t in tiles], axis=0)
        im_ref[rows, lanes] = jnp.concatenate([t[1] for t in tiles], axis=0)
        return (cr, ci) if extra is None else (cr, ci) + extra(first, tiles, carry[2:])

    assert n_chunks % U == 0
    z = jnp.zeros((8, L), F32)
    init = (z, z) if extra is None else (z, z, z, z)
    return lax.fori_loop(0, n_groups, body, init)


def _s5_consts(ab_re, ab_im):
    ng = ab_re.shape[0] // GROUPS_PER_STEP
    ar, ai = ab_re.reshape(ng, 1, ST_LANES), ab_im.reshape(ng, 1, ST_LANES)

    def cmul(xr, xi, yr, yi):
        return xr * yr - xi * yi, xr * yi + xi * yr

    def build(ar, ai, reverse):
        pw = [(ar, ai)]
        for _ in range(7):
            pw.append(cmul(*pw[-1], ar, ai))
        row = jnp.arange(8).reshape(1, 8, 1)
        tiles = []
        for k in (1, 2, 4):
            keep = (row <= 7 - k) if reverse else (row >= k)
            tiles += [jnp.where(keep, pw[k - 1][0], 0.0), jnp.where(keep, pw[k - 1][1], 0.0)]
        order = [7 - r for r in range(8)] if reverse else list(range(8))
        tiles += [jnp.concatenate([pw[o][0] for o in order], axis=1), jnp.concatenate([pw[o][1] for o in order], axis=1)]
        return jnp.stack([jnp.broadcast_to(t, (ng, 8, ST_LANES)) for t in tiles], axis=1)

    return build(ar, ai, False), build(ar, -ai, True)


def _s5_blockdiag(bb_re, bb_im, c_re, c_im):
    G = bb_re.shape[0]
    ng = G // GROUPS_PER_STEP
    eye = jnp.eye(GROUPS_PER_STEP, dtype=F32)

    def wb(bb):
        return jnp.einsum("bgph,gk->bghkp", bb.reshape(ng, GROUPS_PER_STEP, S5_STATE, S5_GROUP), eye).reshape(ng, U_LANES, ST_LANES)

    def wc(cc):
        return jnp.einsum("bghp,gk->bkpgh", cc.reshape(ng, GROUPS_PER_STEP, S5_GROUP, S5_STATE), eye).reshape(ng, ST_LANES, U_LANES)

    Wb = jnp.concatenate([wb(bb_re), wb(bb_im)], axis=2).astype(BF16)
    Wc = jnp.concatenate([wc(c_re), -wc(c_im)], axis=1).astype(BF16)
    return Wb, Wc


def _s5_unblock(dWb, dWc):
    ng = dWb.shape[0]
    eye = jnp.eye(GROUPS_PER_STEP, dtype=F32)

    def ub(w):
        return jnp.einsum("bghkp,gk->bgph", w.reshape(ng, GROUPS_PER_STEP, S5_GROUP, GROUPS_PER_STEP, S5_STATE), eye).reshape(-1, S5_STATE, S5_GROUP)

    def uc(w):
        return jnp.einsum("bkpgh,gk->bghp", w.reshape(ng, GROUPS_PER_STEP, S5_STATE, GROUPS_PER_STEP, S5_GROUP), eye).reshape(-1, S5_GROUP, S5_STATE)

    return ub(dWb[:, :, :ST_LANES]), ub(dWb[:, :, ST_LANES:]), uc(dWc[:, :ST_LANES, :]), -uc(dWc[:, ST_LANES:, :])


def _s5_disc(a_re, a_im, log_dt, b_re, b_im):
    dt = jnp.exp(log_dt)[:, None]
    mag = jnp.exp(a_re * dt)
    ab_re = mag * jnp.cos(a_im * dt)
    ab_im = mag * jnp.sin(a_im * dt)
    den = a_re * a_re + a_im * a_im
    nr, ni = ab_re - 1, ab_im
    f_re = (nr * a_re + ni * a_im) / den
    f_im = (ni * a_re - nr * a_im) / den
    bb_re = f_re[..., None] * b_re - f_im[..., None] * b_im
    bb_im = f_re[..., None] * b_im + f_im[..., None] * b_re
    return ab_re, ab_im, bb_re, bb_im


ROW_CHUNK = 512


def _s5_fwd(u, Wb, Wc, cf, d, xsrc, *, n_ex, name):
    T, D = u.shape
    S = T // n_ex
    ng = D // U_LANES
    rc = min(ROW_CHUNK, S)

    def body(u_ref, wb_ref, wc_ref, cf_ref, d_ref, xsrc_ref, y_ref, gy_ref, gyt_ref, st_ref, xout_ref, re_s, im_s, *sems):
        step = pl.program_id(0) * ng + pl.program_id(1)
        exch = _ChipExchange(xsrc_ref, xout_ref, *sems, scatter=False)

        @pl.when(step == 0)
        def _():
            exch.start()

        for r in range(S // rc):
            rows = pl.ds(r * rc, rc)
            bu = jnp.dot(u_ref[rows, :].astype(BF16), wb_ref[...], preferred_element_type=F32)
            re_s[rows, :] = bu[:, :ST_LANES]
            im_s[rows, :] = bu[:, ST_LANES:]
        for l0 in range(0, ST_LANES, SCAN_LANES):
            _scan_tiles(re_s, im_s, cf_ref, l0, S // 8, False)
        for r in range(S // rc):
            rows = pl.ds(r * rc, rc)
            st = jnp.concatenate([re_s[rows, :], im_s[rows, :]], axis=1).astype(BF16)
            st_ref[rows, :] = st
            y = jnp.dot(st, wc_ref[...], preferred_element_type=F32) + d_ref[...] * u_ref[rows, :]
            y_ref[rows, :] = y
            gy = _gelu(y)
            gy_ref[rows, :] = gy.astype(BF16)
            gyt_ref[:, rows] = gy.T.astype(BF16)

        @pl.when(step == n_ex * ng - 1)
        def _():
            exch.wait()

    return pl.pallas_call(
        body, name=name, grid=(n_ex, ng),
        in_specs=[pl.BlockSpec((S, U_LANES), lambda e, g: (e, g)),
                  pl.BlockSpec((None, U_LANES, 2 * ST_LANES), lambda e, g: (g, 0, 0)),
                  pl.BlockSpec((None, 2 * ST_LANES, U_LANES), lambda e, g: (g, 0, 0)),
                  pl.BlockSpec((None, 8, 8, ST_LANES), lambda e, g: (g, 0, 0, 0)),
                  pl.BlockSpec((1, U_LANES), lambda e, g: (0, g)), ANY],
        out_specs=[pl.BlockSpec((S, U_LANES), lambda e, g: (e, g))] * 2 + [pl.BlockSpec((U_LANES, S), lambda e, g: (g, e)),
                   pl.BlockSpec((S, 2 * ST_LANES), lambda e, g: (e, g)), ANY],
        out_shape=[jax.ShapeDtypeStruct((T, D), F32), jax.ShapeDtypeStruct((T, D), BF16), jax.ShapeDtypeStruct((D, T), BF16),
                   jax.ShapeDtypeStruct((T, ng * 2 * ST_LANES), BF16), _ChipExchange.out_shape(xsrc, False)],
        scratch_shapes=[pltpu.VMEM((S, ST_LANES), F32)] * 2 + _ChipExchange.SCRATCH,
        compiler_params=_cp(("arbitrary", "arbitrary")),
    )(u, Wb, Wc, cf, d, xsrc)


def _s5_bwd(u, y, dgy, st, Wb, Wc, cr, d, xsrc, *, n_ex, name):
    T, D = u.shape
    S = T // n_ex
    ng = D // U_LANES
    rc = min(ROW_CHUNK, S)
    nch = S // 8
    grp = 8 * SCAN_UNROLL
    assert grp % 16 == 0

    def body(u_ref, y_ref, dgy_ref, st_ref, wb_ref, wc_ref, cr_ref, d_ref, xsrc_ref,
             du_ref, dwb_ref, dwc_ref, dab_ref, dd_ref, xout_ref, gr_s, gi_s, dy_s, *sems):
        e = pl.program_id(1)
        step = pl.program_id(0) * n_ex + e
        exch = _ChipExchange(xsrc_ref, xout_ref, *sems, scatter=True)

        @pl.when(step == 0)
        def _():
            exch.start()

        @pl.when(e == 0)
        def _():
            dwb_ref[...] = jnp.zeros_like(dwb_ref)
            dwc_ref[...] = jnp.zeros_like(dwc_ref)
            dab_ref[...] = jnp.zeros_like(dab_ref)
            dd_ref[...] = jnp.zeros_like(dd_ref)

        dd = jnp.zeros((1, U_LANES), F32)
        for r in range(S // rc):
            rows = pl.ds(r * rc, rc)
            ut = u_ref[rows, :]
            dy = dgy_ref[rows, :].astype(F32) * _gelu_grad(y_ref[rows, :])
            dy_s[rows, :] = dy
            dd = dd + _csum(dy * ut)
            go = lax.dot_general(dy.astype(BF16), wc_ref[...], (((1,), (1,)), ((), ())), preferred_element_type=F32)
            gr_s[rows, :] = go[:, :ST_LANES]
            gi_s[rows, :] = go[:, ST_LANES:]
        dd_ref[0:1, :] += dd
        row0 = lax.broadcasted_iota(jnp.int32, (8, SCAN_LANES), 0) == 0
        for l0 in range(0, ST_LANES, SCAN_LANES):
            lanes = pl.ds(l0, SCAN_LANES)

            def dab_group(first, tiles, acc, l0=l0):
                def states(r0, n, lane0):
                    return st_ref[pl.ds(pl.multiple_of(r0, 16), n), pl.ds(lane0, SCAN_LANES)].astype(F32)
                r0 = first * 8
                cur = states(r0, grp, l0), states(r0, grp, ST_LANES + l0)
                live = (first > 0).astype(F32)
                p0 = jnp.maximum(r0 - 16, 0)
                before = [states(p0, 16, l0)[8:16, :] * live, states(p0, 16, ST_LANES + l0)[8:16, :] * live]
                a_re, a_im = acc
                for t, (gr, gi) in enumerate(tiles):
                    here = [c[8 * t:8 * t + 8, :] for c in cur]
                    sr, si = [jnp.where(row0, pltpu.roll(b, 1, 0), pltpu.roll(h, 1, 0)) for b, h in zip(before, here)]
                    a_re, a_im = a_re + gr * sr + gi * si, a_im + gi * sr - gr * si
                    before = here
                return a_re, a_im

            res = _scan_tiles(gr_s, gi_s, cr_ref, l0, nch, True, extra=dab_group)
            dab_ref[0:1, lanes] += _csum(res[2])
            dab_ref[1:2, lanes] += _csum(res[3])
        for r in range(S // rc):
            rows = pl.ds(r * rc, rc)
            st = st_ref[rows, :]
            g = jnp.concatenate([gr_s[rows, :], gi_s[rows, :]], axis=1).astype(BF16)
            dyb = dy_s[rows, :].astype(BF16)
            dwc_ref[...] += lax.dot_general(st, dyb, (((0,), (0,)), ((), ())), preferred_element_type=F32)
            dwb_ref[...] += lax.dot_general(u_ref[rows, :].astype(BF16), g, (((0,), (0,)), ((), ())), preferred_element_type=F32)
            du = lax.dot_general(g, wb_ref[...], (((1,), (1,)), ((), ())), preferred_element_type=F32)
            du_ref[rows, :] = du + d_ref[...] * dy_s[rows, :]

        @pl.when(step == ng * n_ex - 1)
        def _():
            exch.wait()

    return pl.pallas_call(
        body, name=name, grid=(ng, n_ex),
        in_specs=[pl.BlockSpec((S, U_LANES), lambda g, e: (e, g))] * 3 + [
            pl.BlockSpec((S, 2 * ST_LANES), lambda g, e: (e, g)),
            pl.BlockSpec((None, U_LANES, 2 * ST_LANES), lambda g, e: (g, 0, 0)),
            pl.BlockSpec((None, 2 * ST_LANES, U_LANES), lambda g, e: (g, 0, 0)),
            pl.BlockSpec((None, 8, 8, ST_LANES), lambda g, e: (g, 0, 0, 0)),
            pl.BlockSpec((1, U_LANES), lambda g, e: (0, g)), ANY],
        out_specs=[pl.BlockSpec((S, U_LANES), lambda g, e: (e, g)),
                   pl.BlockSpec((None, U_LANES, 2 * ST_LANES), lambda g, e: (g, 0, 0)),
                   pl.BlockSpec((None, 2 * ST_LANES, U_LANES), lambda g, e: (g, 0, 0)),
                   pl.BlockSpec((None, 8, ST_LANES), lambda g, e: (g, 0, 0)),
                   pl.BlockSpec((None, 8, U_LANES), lambda g, e: (g, 0, 0)), ANY],
        out_shape=[jax.ShapeDtypeStruct((T, D), F32),
                   jax.ShapeDtypeStruct((ng, U_LANES, 2 * ST_LANES), F32),
                   jax.ShapeDtypeStruct((ng, 2 * ST_LANES, U_LANES), F32),
                   jax.ShapeDtypeStruct((ng, 8, ST_LANES), F32),
                   jax.ShapeDtypeStruct((ng, 8, U_LANES), F32), _ChipExchange.out_shape(xsrc, True)],
        scratch_shapes=[pltpu.VMEM((S, ST_LANES), F32)] * 2 + [pltpu.VMEM((S, U_LANES), F32)] + _ChipExchange.SCRATCH,
        compiler_params=_cp(("arbitrary", "arbitrary")),
    )(u, y, dgy, st, Wb, Wc, cr, d, xsrc)


TQ = 256
KW = 512
SUB = 128


def _head_masks():
    lane = lax.broadcasted_iota(jnp.int32, (1, 2 * HEAD_DIM), 1)
    m0 = (lane < HEAD_DIM).astype(F32)
    return m0, 1.0 - m0


def _head_norm(x, g, m0, m1):
    sq = x * x
    r0 = lax.rsqrt(jnp.sum(sq * m0, axis=-1, keepdims=True) / HEAD_DIM + EPS)
    r1 = lax.rsqrt(jnp.sum(sq * m1, axis=-1, keepdims=True) / HEAD_DIM + EPS)
    r = m0 * r0 + m1 * r1
    return x * r, r


def _head_norm_bwd(dy, n, r, g, m0, m1):
    dn = dy * g
    p = dn * n
    mean = (m0 * jnp.sum(p * m0, axis=-1, keepdims=True) + m1 * jnp.sum(p * m1, axis=-1, keepdims=True)) / HEAD_DIM
    return r * (dn - n * mean), _csum(dy * n)


def _pair_matrix(kind):
    r = lax.broadcasted_iota(jnp.int32, (2 * SUB, 2 * SUB), 0)
    c = lax.broadcasted_iota(jnp.int32, (2 * SUB, 2 * SUB), 1)
    same = (r < SUB) == (c < SUB)
    rel = {"after": r > c, "upto": r <= c, "before": r < c}[kind]
    return jnp.logical_and(same, rel).astype(BF16)


def _block_sums(x, mat, carry, reverse, terms=2):
    hi = x.astype(BF16)
    lo = (x - hi.astype(F32)).astype(BF16) if terms == 2 else None
    npair = x.shape[1] // (2 * SUB)
    parts = [None] * (2 * npair)
    for p in (range(npair - 1, -1, -1) if reverse else range(npair)):
        sl = slice(2 * SUB * p, 2 * SUB * (p + 1))
        loc = jnp.dot(hi[:, sl], mat, preferred_element_type=F32)
        if terms == 2:
            loc = loc + jnp.dot(lo[:, sl], mat, preferred_element_type=F32)
        for b in ((1, 0) if reverse else (0, 1)):
            k = 2 * p + b
            parts[k] = loc[:, SUB * b:SUB * (b + 1)] + carry
            carry = carry + jnp.sum(x[:, SUB * k:SUB * (k + 1)], axis=-1, keepdims=True)
    return jnp.concatenate(parts, axis=1), carry


def _sb_logits(z, mask):
    lp = jnp.minimum(z, 0.0) - jnp.log(1.0 + jnp.exp(-jnp.abs(z)))
    lf = lp - z
    if mask is not None:
        lf = jnp.where(mask, lf, 0.0)
    return lp, lf


def _causal_mask(row0, col0, kw):
    r = row0 + lax.broadcasted_iota(jnp.int32, (TQ, kw), 0)
    c = col0 + lax.broadcasted_iota(jnp.int32, (TQ, kw), 1)
    return c < r


def _transposed_windows(x, ref):
    for w in range(x.shape[0] // KW):
        ref[w] = x[w * KW:(w + 1) * KW, :].T.astype(BF16)


def _attn_fwd(q, kv, qg, kg, xsrc, *, n_ex, name):
    T, D = q.shape
    S = T // n_ex
    nhp = D // (2 * HEAD_DIM)
    nq = S // TQ
    scale = 1.0 / math.sqrt(HEAD_DIM)

    def body(q_ref, k_ref, v_ref, qg_ref, kg_ref, xsrc_ref, o_ref, tot_ref, ot_ref, xout_ref, kT_s, qm_s, vm_s, *sems):
        step = pl.program_id(0) * nhp + pl.program_id(1)
        exch = _ChipExchange(xsrc_ref, xout_ref, *sems, scatter=False)

        @pl.when(step == 0)
        def _():
            exch.start()

        m0, m1 = _head_masks()
        qn, _ = _head_norm(q_ref[...], None, m0, m1)
        qn = qn * (qg_ref[...] * scale)
        kn, _ = _head_norm(k_ref[...], None, m0, m1)
        _transposed_windows(kn * kg_ref[...], kT_s)
        v = v_ref[...]
        for h, m in enumerate((m0, m1)):
            qm_s[h] = (qn * m).astype(BF16)
            vm_s[h] = (v * m).astype(BF16)
        u_after = _pair_matrix("after")

        def window(rows, win, st, mask, kw):
            keys = pl.ds(pl.multiple_of(win * KW, KW), kw)
            zs = [jnp.dot(qm_s[h, rows, :], kT_s[win, :, :kw], preferred_element_type=F32) for h in range(2)]
            lg = [_sb_logits(zs[h], mask) for h in range(2)]
            sums = [_block_sums(lg[h][1], u_after, st[2 * h], True) for h in range(2)]
            out = ()
            for h in range(2):
                w = jnp.exp(lg[h][0] + sums[h][0])
                if mask is not None:
                    w = jnp.where(mask, w, 0.0)
                out += (sums[h][1], st[2 * h + 1] + jnp.dot(w.astype(BF16), vm_s[h, keys, :], preferred_element_type=F32))
            return out

        def qtile(iq, last, kw):
            rows = pl.ds(pl.multiple_of(iq * TQ, TQ), TQ)
            mask = _causal_mask(iq * TQ, last * KW, kw)
            z1, zq = jnp.zeros((TQ, 1), F32), jnp.zeros((TQ, 2 * HEAD_DIM), F32)
            st = window(rows, last, (z1, zq, z1, zq), mask, kw)
            st = lax.fori_loop(0, last, lambda jj, st: window(rows, last - 1 - jj, st, None, KW), st)
            o_ref[rows, :] = (st[1] + st[3]).astype(o_ref.dtype)
            tot_ref[rows, :] = st[0] * m0 + st[2] * m1

        def qtiles_of_window(a, _):
            for sub in range(KW // TQ):
                qtile(a * (KW // TQ) + sub, a, (sub + 1) * TQ)
            return 0

        lax.fori_loop(0, S // KW, qtiles_of_window, 0)
        ot_ref[...] = o_ref[...].astype(F32).T.astype(BF16)

        @pl.when(step == n_ex * nhp - 1)
        def _():
            exch.wait()

    assert S % KW == 0 and KW % TQ == 0
    nwin = S // KW
    blk = (S, 2 * HEAD_DIM)
    return pl.pallas_call(
        body, name=name, grid=(n_ex, nhp),
        in_specs=[pl.BlockSpec(blk, lambda e, h: (e, h)), pl.BlockSpec(blk, lambda e, h: (e, h)),
                  pl.BlockSpec(blk, lambda e, h: (e, h + nhp)),
                  pl.BlockSpec((1, 2 * HEAD_DIM), lambda e, h: (0, 0)), pl.BlockSpec((1, 2 * HEAD_DIM), lambda e, h: (0, 0)), ANY],
        out_specs=[pl.BlockSpec(blk, lambda e, h: (e, h))] * 2 + [pl.BlockSpec((2 * HEAD_DIM, S), lambda e, h: (h, e)), ANY],
        out_shape=[jax.ShapeDtypeStruct((T, D), BF16), jax.ShapeDtypeStruct((T, D), F32), jax.ShapeDtypeStruct((D, T), BF16),
                   _ChipExchange.out_shape(xsrc, False)],
        scratch_shapes=[pltpu.VMEM((nwin, 2 * HEAD_DIM, KW), BF16), pltpu.VMEM((2,) + blk, BF16), pltpu.VMEM((2,) + blk, BF16)]
        + _ChipExchange.SCRATCH,
        compiler_params=_cp(("arbitrary", "arbitrary")),
    )(q, kv, kv, qg, kg, xsrc)


def _attn_bwd(q, kv, tot, do, qg, kg, *, n_ex, name):
    T, D = q.shape
    S = T // n_ex
    nhp = D // (2 * HEAD_DIM)
    nq = S // TQ
    scale = 1.0 / math.sqrt(HEAD_DIM)

    def body(q_ref, k_ref, v_ref, tot_ref, do_ref, qg_ref, kg_ref, dq_ref, dk_ref, dv_ref, dqg_ref, dkg_ref,
             kT_s, vT_s, km_s, qm_s, dom_s, dqn_s, dkT_s, dvT_s):
        m0, m1 = _head_masks()
        qn, qr = _head_norm(q_ref[...], None, m0, m1)
        kn, kr = _head_norm(k_ref[...], None, m0, m1)
        qs = qn * (qg_ref[...] * scale)
        kk = kn * kg_ref[...]
        _transposed_windows(kk, kT_s)
        _transposed_windows(v_ref[...], vT_s)
        do = do_ref[...]
        for h, m in enumerate((m0, m1)):
            qm_s[h] = (qs * m).astype(BF16)
            km_s[h] = (kk * m).astype(BF16)
            dom_s[h] = (do * m).astype(BF16)
        dkT_s[...] = jnp.zeros_like(dkT_s)
        dvT_s[...] = jnp.zeros_like(dvT_s)
        u_upto, u_before = _pair_matrix("upto"), _pair_matrix("before")

        def both(inv, win, st, mask, kw):
            keys = pl.ds(pl.multiple_of(win * KW, KW), kw)
            lg = [_sb_logits(jnp.dot(inv[h][0], kT_s[win, :, :kw], preferred_element_type=F32), mask) for h in range(2)]
            s_lf = [_block_sums(lg[h][1], u_upto, st[3 * h], False) for h in range(2)]
            ws, ews = [], []
            for h in range(2):
                w = jnp.exp(lg[h][0] - s_lf[h][0])
                if mask is not None:
                    w = jnp.where(mask, w, 0.0)
                ws.append(w)
                ews.append(jnp.dot(inv[h][2], vT_s[win, :, :kw], preferred_element_type=F32) * w)
            s_e = [_block_sums(ews[h], u_before, st[3 * h + 1], False, terms=1) for h in range(2)]
            out, dk, dv = (), None, None
            for h in range(2):
                sig = jnp.exp(lg[h][0])
                dz = ews[h] - sig * (ews[h] + s_e[h][0])
                if mask is not None:
                    dz = jnp.where(mask, dz, 0.0)
                dzb = dz.astype(BF16)
                out += (s_lf[h][1], s_e[h][1], st[3 * h + 2] + jnp.dot(dzb, km_s[h, keys, :], preferred_element_type=F32))
                dkh = jnp.dot(inv[h][1], dzb, preferred_element_type=F32)
                dvh = jnp.dot(inv[h][3], ws[h].astype(BF16), preferred_element_type=F32)
                dk, dv = (dkh, dvh) if h == 0 else (dk + dkh, dv + dvh)
            dkT_s[win, :, :kw] += dk
            dvT_s[win, :, :kw] += dv
            return out

        def qtile(iq, last, kw):
            rows = pl.ds(pl.multiple_of(iq * TQ, TQ), TQ)
            mask = _causal_mask(iq * TQ, last * KW, kw)
            tt = tot_ref[rows, :]
            inv, neg_total = [], []
            for h, m in enumerate((m0, m1)):
                qh, doh = qm_s[h, rows, :], dom_s[h, rows, :]
                neg_total.append(jnp.sum(tt * m, axis=-1, keepdims=True) * (-1.0 / HEAD_DIM))
                inv.append((qh, qh.astype(F32).T.astype(BF16), doh, doh.astype(F32).T.astype(BF16)))

            z1, zq = jnp.zeros((TQ, 1), F32), jnp.zeros((TQ, 2 * HEAD_DIM), F32)
            st = lax.fori_loop(0, last, lambda win, st: both(inv, win, st, None, KW), (neg_total[0], z1, zq, neg_total[1], z1, zq))
            st = both(inv, last, st, mask, kw)
            dqn_s[rows, :] = st[2] + st[5]

        def qtiles_of_window(a, _):
            for sub in range(KW // TQ):
                qtile(a * (KW // TQ) + sub, a, (sub + 1) * TQ)
            return 0

        lax.fori_loop(0, S // KW, qtiles_of_window, 0)
        dkn = jnp.concatenate([dkT_s[w].T for w in range(nwin)], axis=0)
        dq, dqg = _head_norm_bwd(dqn_s[...] * scale, qn, qr, qg_ref[...], m0, m1)
        dk, dkg = _head_norm_bwd(dkn, kn, kr, kg_ref[...], m0, m1)
        dq_ref[...] = dq.astype(dq_ref.dtype)
        dk_ref[...] = dk.astype(dk_ref.dtype)
        dv_ref[...] = jnp.concatenate([dvT_s[w].T for w in range(nwin)], axis=0).astype(dv_ref.dtype)
        dqg_ref[...] = dqg
        dkg_ref[...] = dkg

    assert S % KW == 0 and KW % TQ == 0
    nwin = S // KW
    blk = (S, 2 * HEAD_DIM)
    tblk = (nwin, 2 * HEAD_DIM, KW)
    gblk = (None, None, 1, 2 * HEAD_DIM)
    dq, dk, dv, dqg, dkg = pl.pallas_call(
        body, name=name, grid=(n_ex, nhp),
        in_specs=[pl.BlockSpec(blk, lambda e, h: (e, h)), pl.BlockSpec(blk, lambda e, h: (e, h)),
                  pl.BlockSpec(blk, lambda e, h: (e, h + nhp)),
                  pl.BlockSpec(blk, lambda e, h: (e, h)), pl.BlockSpec(blk, lambda e, h: (e, h)),
                  pl.BlockSpec((1, 2 * HEAD_DIM), lambda e, h: (0, 0)), pl.BlockSpec((1, 2 * HEAD_DIM), lambda e, h: (0, 0))],
        out_specs=[pl.BlockSpec(blk, lambda e, h: (e, h))] * 3 + [pl.BlockSpec(gblk, lambda e, h: (e, h, 0, 0))] * 2,
        out_shape=[jax.ShapeDtypeStruct((T, D), BF16)] * 3 + [jax.ShapeDtypeStruct((n_ex, nhp, 1, 2 * HEAD_DIM), F32)] * 2,
        scratch_shapes=[pltpu.VMEM(tblk, BF16), pltpu.VMEM(tblk, BF16),
                        pltpu.VMEM((2,) + blk, BF16), pltpu.VMEM((2,) + blk, BF16), pltpu.VMEM((2,) + blk, BF16),
                        pltpu.VMEM(blk, F32), pltpu.VMEM(tblk, F32), pltpu.VMEM(tblk, F32)],
        compiler_params=_cp(("parallel", "parallel")),
    )(q, kv, kv, tot, do, qg, kg)
    return dq, dk, dv, dqg, dkg


def _place():
    return lax.axis_index("x"), lax.axis_index("y"), lax.axis_index("c")


def _all_gather8(x_shard, *, name):
    m_per, n = x_shard.shape

    def body(x_ref, out_ref, send_sems, recv_sems, local_sem):
        x, y, c = _place()
        me, sibling = (x, y, c), (x, y, 1 - c)
        chips = [(1 - x, y), (x, 1 - y), (1 - x, 1 - y)]

        def rows(px, py, pc):
            return out_ref.at[pl.ds((4 * px + 2 * py + pc) * m_per, m_per), :]

        def copy(k, block, to, src=None):
            return pltpu.make_async_remote_copy(
                src_ref=rows(*block) if src is None else src, dst_ref=rows(*block),
                send_sem=send_sems.at[k], recv_sem=recv_sems.at[k], device_id=to, device_id_type=MESH)

        mine = pltpu.make_async_copy(x_ref, rows(*me), local_sem)
        mine.start()
        first = [copy(0, me, sibling, src=x_ref)]
        first += [copy(1 + j, me, (*chip, c), src=x_ref) for j, chip in enumerate(chips)]
        for cp in first:
            cp.start()
        passed = [copy(4 + j, (*chip, c), sibling) for j, chip in enumerate(chips)]
        for j, chip in enumerate(chips):
            copy(1 + j, (*chip, c), me).wait_recv()
            passed[j].start()
        copy(0, sibling, me).wait_recv()
        for j, chip in enumerate(chips):
            copy(4 + j, (*chip, 1 - c), me).wait_recv()
        for cp in first + passed:
            cp.wait_send()
        mine.wait()

    return pl.pallas_call(
        body, name=name, out_shape=jax.ShapeDtypeStruct((8 * m_per, n), x_shard.dtype),
        in_specs=[pl.BlockSpec(memory_space=pltpu.VMEM)], out_specs=pl.BlockSpec(memory_space=pltpu.VMEM),
        scratch_shapes=[pltpu.SemaphoreType.DMA((7,)), pltpu.SemaphoreType.DMA((7,)), pltpu.SemaphoreType.DMA],
        compiler_params=pltpu.CompilerParams(vmem_limit_bytes=VMEM_LIMIT),
    )(x_shard)


def _sibling_sum_half(x, *, name):
    R, C = x.shape
    half = R // 2
    assert half % 16 == 0

    def body(x_ref, o_ref, theirs, send_sem, recv_sem):
        px, py, pc = _place()
        cp = pltpu.make_async_remote_copy(src_ref=x_ref, dst_ref=theirs, send_sem=send_sem, recv_sem=recv_sem,
                                          device_id=(px, py, 1 - pc), device_id_type=MESH)
        cp.start()
        cp.wait()
        rows = pl.ds(pl.multiple_of(pc * half, 8), half)
        o_ref[...] = (x_ref[rows, :] + theirs[rows, :]).astype(BF16)

    return pl.pallas_call(
        body, name=name, out_shape=jax.ShapeDtypeStruct((half, C), BF16),
        in_specs=[pl.BlockSpec(memory_space=pltpu.VMEM)], out_specs=pl.BlockSpec(memory_space=pltpu.VMEM),
        scratch_shapes=[pltpu.VMEM((R, C), x.dtype), pltpu.SemaphoreType.DMA, pltpu.SemaphoreType.DMA],
        compiler_params=pltpu.CompilerParams(vmem_limit_bytes=VMEM_LIMIT),
    )(x)


def _sum_blocks(x, n, *, name):
    R = x.shape[0] // n

    def body(x_ref, o_ref):
        acc = x_ref[pl.ds(0, R), :].astype(F32)
        for k in range(1, n):
            acc = acc + x_ref[pl.ds(k * R, R), :].astype(F32)
        o_ref[...] = acc

    return pl.pallas_call(body, name=name, out_shape=jax.ShapeDtypeStruct((R, x.shape[1]), F32),
                          compiler_params=pltpu.CompilerParams(vmem_limit_bytes=VMEM_LIMIT))(x)


def _colsum(x, *, name):
    def body(x_ref, o_ref):
        o_ref[...] = jnp.sum(x_ref[...], axis=0, keepdims=True)
    return pl.pallas_call(body, name=name, out_shape=jax.ShapeDtypeStruct((1, x.shape[1]), x.dtype))(x)


ANY = pl.BlockSpec(memory_space=pl.ANY)


class _ChipExchange:
    SCRATCH = [pltpu.SemaphoreType.DMA((3,)), pltpu.SemaphoreType.DMA((3,)), pltpu.SemaphoreType.DMA]

    @staticmethod
    def out_shape(src, scatter):
        return jax.ShapeDtypeStruct(((4,) + tuple(src.shape[1:])) if scatter else ((4, 2) + tuple(src.shape[1:])), src.dtype)

    def __init__(self, src_ref, out_ref, send_sems, recv_sems, local_sem, scatter):
        x, y, c = _place()
        myj = 2 * x + y
        chips = [(1 - x, y), (x, 1 - y), (1 - x, 1 - y)]

        def slot(j):
            return out_ref.at[j] if scatter else out_ref.at[j, c]

        def piece(j):
            return src_ref.at[j] if scatter else src_ref.at[c]

        self.mine = pltpu.make_async_copy(piece(myj), slot(myj), local_sem)
        self.sends = [pltpu.make_async_remote_copy(
            src_ref=piece(2 * cx + cy), dst_ref=slot(myj), send_sem=send_sems.at[k], recv_sem=recv_sems.at[k],
            device_id=(cx, cy, c), device_id_type=MESH) for k, (cx, cy) in enumerate(chips)]
        self.recvs = [pltpu.make_async_remote_copy(
            src_ref=slot(2 * cx + cy), dst_ref=slot(2 * cx + cy), send_sem=send_sems.at[k], recv_sem=recv_sems.at[k],
            device_id=(cx, cy, c), device_id_type=MESH) for k, (cx, cy) in enumerate(chips)]

    def start(self):
        self.mine.start()
        for cp in self.sends:
            cp.start()

    def wait(self):
        for cp in self.recvs:
            cp.wait_recv()
        for cp in self.sends:
            cp.wait_send()
        self.mine.wait()


def _sibling_fill(buf, *, axis, name):
    def half(ref, h):
        return ref.at[h] if axis == 0 else ref.at[:, h]

    def body(in_ref, out_ref, send_sem, recv_sem):
        x, y, c = _place()
        cp = pltpu.make_async_remote_copy(src_ref=half(out_ref, c), dst_ref=half(out_ref, c), send_sem=send_sem, recv_sem=recv_sem,
                                          device_id=(x, y, 1 - c), device_id_type=MESH)
        cp.start()
        pltpu.make_async_remote_copy(src_ref=half(out_ref, 1 - c), dst_ref=half(out_ref, 1 - c), send_sem=send_sem, recv_sem=recv_sem,
                                     device_id=(x, y, 1 - c), device_id_type=MESH).wait_recv()
        cp.wait_send()

    return pl.pallas_call(
        body, name=name, out_shape=jax.ShapeDtypeStruct(buf.shape, buf.dtype), in_specs=[ANY], out_specs=ANY,
        input_output_aliases={0: 0}, scratch_shapes=[pltpu.SemaphoreType.DMA, pltpu.SemaphoreType.DMA],
    )(buf)


def _sibling_swap_half(g, *, name):
    def body(g_ref, out_ref, send_sem, recv_sem):
        x, y, c = _place()
        cp = pltpu.make_async_remote_copy(src_ref=g_ref.at[:, 1 - c], dst_ref=out_ref, send_sem=send_sem, recv_sem=recv_sem,
                                          device_id=(x, y, 1 - c), device_id_type=MESH)
        cp.start()
        cp.wait()

    return pl.pallas_call(
        body, name=name, out_shape=jax.ShapeDtypeStruct((g.shape[0],) + g.shape[2:], g.dtype), in_specs=[ANY], out_specs=ANY,
        scratch_shapes=[pltpu.SemaphoreType.DMA, pltpu.SemaphoreType.DMA],
    )(g)


def _add_my_half(g, b, cidx, *, name, tr=1024):
    n, _, R, C = g.shape
    tr = max(t for t in range(16, tr + 1, 16) if R % t == 0)

    def body(c_ref, g_ref, b_ref, o_ref):
        o_ref[...] = (g_ref[...] + b_ref[...]).astype(o_ref.dtype)

    return pl.pallas_call(
        body, name=name, out_shape=jax.ShapeDtypeStruct((n, R, C), BF16),
        grid_spec=pltpu.PrefetchScalarGridSpec(
            num_scalar_prefetch=1, grid=(n, R // tr),
            in_specs=[pl.BlockSpec((None, None, tr, C), lambda j, i, c: (j, c[0], i, 0)),
                      pl.BlockSpec((None, tr, C), lambda j, i, c: (j, i, 0))],
            out_specs=pl.BlockSpec((None, tr, C), lambda j, i, c: (j, i, 0))),
        compiler_params=_cp(("parallel", "parallel")),
    )(cidx, g, b)


def _sum4_into_half(q, cidx, *, name, tr=1024):
    _, R, C = q.shape
    tr = max(t for t in range(16, tr + 1, 16) if R % t == 0)

    def body(c_ref, q_ref, o_ref):
        o_ref[...] = ((q_ref[0].astype(F32) + q_ref[1].astype(F32)) + q_ref[2].astype(F32)) + q_ref[3].astype(F32)

    return pl.pallas_call(
        body, name=name, out_shape=jax.ShapeDtypeStruct((2, R, C), F32),
        grid_spec=pltpu.PrefetchScalarGridSpec(
            num_scalar_prefetch=1, grid=(R // tr,),
            in_specs=[pl.BlockSpec((4, tr, C), lambda i, c: (0, i, 0))],
            out_specs=pl.BlockSpec((None, tr, C), lambda i, c: (c[0], i, 0))),
        compiler_params=_cp(("parallel",)),
    )(cidx, q)


def _pack_rows(parts, width=1024, row_multiple=8):
    rows, spans, r0 = [], [], 0
    for p in parts:
        n = p.size
        nr = 8 * (-(-n // (8 * width)))
        flat = p.reshape(-1)
        if nr * width != n:
            flat = jnp.pad(flat, (0, nr * width - n))
        rows.append(flat.reshape(nr, width))
        spans.append((r0, nr, n, p.shape))
        r0 += nr
    if r0 % row_multiple:
        rows.append(jnp.zeros((row_multiple - r0 % row_multiple, width), parts[0].dtype))
    return jnp.concatenate(rows, axis=0), spans


def _unpack_rows(buf, spans):
    return [buf[r0:r0 + nr].reshape(-1)[:n].reshape(shape) for (r0, nr, n, shape) in spans]


def kernel(x, c, ada_w, ada_b, mix_norm_g, mlp_norm_g, mlp_w1, mlp_w2, s5_a_re, s5_a_im, s5_log_dt, s5_b_re, s5_b_im, s5_c_re, s5_c_im, s5_d, s5_w_glu, kv_ada_w, kv_ada_b, kv_norm_g, w_kv, k_norm_g, sb_w_q, q_norm_g, sb_w_o, loss_target, m_ada_w, m_ada_b, m_mix_norm_g, m_mlp_norm_g, m_mlp_w1, m_mlp_w2, m_s5_a_re, m_s5_a_im, m_s5_log_dt, m_s5_b_re, m_s5_b_im, m_s5_c_re, m_s5_c_im, m_s5_d, m_s5_w_glu, m_kv_ada_w, m_kv_ada_b, m_kv_norm_g, m_w_kv, m_k_norm_g, m_sb_w_q, m_q_norm_g, m_sb_w_o, v_ada_w, v_ada_b, v_mix_norm_g, v_mlp_norm_g, v_mlp_w1, v_mlp_w2, v_s5_a_re, v_s5_a_im, v_s5_log_dt, v_s5_b_re, v_s5_b_im, v_s5_c_re, v_s5_c_im, v_s5_d, v_s5_w_glu, v_kv_ada_w, v_kv_ada_b, v_kv_norm_g, v_w_kv, v_k_norm_g, v_sb_w_q, v_q_norm_g, v_sb_w_o):
    E, S, D = x.shape
    T = E * S
    FF = 4 * D
    NB = 8 * E
    px, py, pc = _place()
    chip = 2 * px + py
    dev = 4 * px + 2 * py + pc
    cidx = jnp.reshape(pc, (1,)).astype(jnp.int32)
    x0 = x.reshape(T, D)
    tgt = loss_target.reshape(T, D)

    nc_rows, nd = c.size // 128, s5_d.size // 128
    cd = jnp.concatenate([c.reshape(nc_rows, 128), jnp.pad(s5_d.reshape(nd, 128), ((0, 8 - nd), (0, 0)))], axis=0)
    cd_all = _all_gather8(cd, name="ag_c_d").reshape(8, nc_rows + 8, 128)
    c_all = cd_all[:, :nc_rows].reshape(NB, D)
    d_full = cd_all.reshape(4, 2, nc_rows + 8, 128)[:, 0, nc_rows:nc_rows + nd].reshape(1, D)
    sc_all = (c_all * _sigmoid(c_all)).astype(BF16)
    wa = ada_w.shape[2]
    wk = kv_ada_w.shape[1]
    m_sh = jnp.concatenate([_mm(sc_all, _Layer(ada_w, 0), "nn", name="ada0", tn=256),
                            _mm(sc_all, _Layer(ada_w, 1), "nn", name="ada1", tn=256),
                            _mm(sc_all, kv_ada_w, "nn", name="ada_kv", tn=256)], axis=1)
    m_half = lax.dynamic_slice_in_dim(m_sh, pc * (NB // 2), NB // 2, axis=0)
    m_all = _all_gather8(m_half, name="ag_m").reshape(4, NB, 2 * wa + wk)
    mods = []
    for l in range(2):
        full = jnp.transpose(m_all[:, :, l * wa:(l + 1) * wa], (1, 0, 2)).reshape(NB, 6 * D) + ada_b[l]
        mine = lax.dynamic_slice_in_dim(full, E * dev, E, axis=0)
        mods.append([mine[:, i * D:(i + 1) * D].reshape(E, 1, D) for i in range(6)])
    full = jnp.transpose(m_all[:, :, 2 * wa:], (1, 0, 2)).reshape(NB, 2 * D) + kv_ada_b
    mine = lax.dynamic_slice_in_dim(full, E * dev, E, axis=0)
    kv_sh, kv_sc = [mine[:, i * D:(i + 1) * D].reshape(E, 1, D) for i in range(2)]

    wpack_a = jnp.concatenate([mlp_w1[0], mlp_w2[0], jnp.concatenate([s5_w_glu[0], w_kv], axis=1), sb_w_q[0]], axis=0).astype(BF16)
    wpack_b = jnp.concatenate([mlp_w1[1], mlp_w2[1], sb_w_o[0]], axis=0).astype(BF16)
    RA, RB = wpack_a.shape[0], wpack_b.shape[0]
    RW = RA + RB

    tm = min(2048, S)
    tm_res = min(1024, S)
    gbuf = [jax.ShapeDtypeStruct((4, RW, D), F32)]

    def grad_mm(act, dout, kind, roff, nr, c0, nc, name, transposed=False):
        gbuf[0] = _mm(act, dout, "nn" if transposed else "tn", name=name, tm=1024, tk=2048,
                      into=_Sharded(gbuf[0], kind, roff, nr, c0, nc))

    def mlp_fwd(xa, l, mod, normed=None):
        sh_m, sc_m, g_m = mod[3], mod[4], mod[5]
        h, h_t = normed if normed is not None else _norm_mod_fwd(
            xa, mlp_norm_g[l:l + 1], sh_m, sc_m, n_ex=E, out_dtype=BF16, name=f"mlp_norm{l}", with_transpose=True)

        def relu_sq(acc):
            ra = jnp.maximum(acc, 0.0)
            return ra * ra, ra
        r, ra = _mm(h, W1[l], "nn", name=f"mlp_up{l}", out_dtypes=(BF16, BF16), tm=tm, epilogue=relu_sq)
        xb, ff = _mm(r, W2[l], "nn", name=f"mlp_down{l}", out_dtypes=(F32, F32), tm=tm_res,
                     extras=[_mn_extra(xa), _vec_extra(g_m, S)],
                     epilogue=lambda acc, xat, gt: (xat + gt * acc, acc))
        return xb, (h_t, r, ra, ff)

    def mlp_bwd(dxb, dff, xa, l, mod, saved, gated=None):
        sc_m = mod[4]
        h_t, r, ra, _ = saved
        da = _mm(dff, W2[l], "nt", name=f"mlp_down_dx{l}", out_dtypes=(BF16,), tm=tm, extras=[_mn_extra(ra)],
                 epilogue=lambda acc, rat: (acc * (2.0 * rat.astype(F32)),))
        grad_mm(r, dff, "rows", (2 + l) * D, D, 0, D, f"mlp_down_dw{l}")
        dh = _mm(da, W1[l], "nt", name=f"mlp_up_dx{l}", tm=tm)
        grad_mm(h_t, da, "cols", l * D, D, 0, D, f"mlp_up_dw{l}", transposed=True)
        return _norm_mod_bwd(xa, dh, dxb, mlp_norm_g[l:l + 1], sc_m, n_ex=E, name=f"mlp_norm_bwd{l}", gated=gated)

    ab_re, ab_im, bb_re, bb_im = _s5_disc(s5_a_re[0], s5_a_im[0], s5_log_dt[0], s5_b_re[0], s5_b_im[0])
    cf, cr = _s5_consts(ab_re, ab_im)
    Wb, Wc = _s5_blockdiag(bb_re, bb_im, s5_c_re[0], s5_c_im[0])
    ng = D // U_LANES

    mod0, mod1 = mods
    h0 = _norm_mod_fwd(x0, mix_norm_g[0:1], mod0[0], mod0[1], n_ex=E, out_dtype=F32, name="mix_norm0")
    y, gy, gy_t, s5_states, wfull_a = _s5_fwd(h0, Wb, Wc, cf, d_full, wpack_a.reshape(2, RA // 2, D), n_ex=E, name="s5_fwd")
    wfull_a = _sibling_fill(wfull_a, axis=1, name="wgather_a_d2d").reshape(4, RA, D)

    W1 = [_Sharded(wfull_a, "cols", 0, D, 0, D), None]
    W2 = [_Sharded(wfull_a, "rows", D, D, 0, D), None]
    Wglu = _Sharded(wfull_a, "cols", 2 * D, D, 0, D // 2)
    Wkv = _Sharded(wfull_a, "cols", 2 * D, D, D // 2, D // 2)
    Wq = _Sharded(wfull_a, "rows", 3 * D, D // 4, 0, D)
    vg = _mm(gy, Wglu, "nn", name="glu_up", tm=tm)
    def glu_gate_and_norm(v, g, xt, ga, sh, sc, gain):
        x1t = xt + ga * (v * _sigmoid(g))
        n = x1t * lax.rsqrt(jnp.mean(x1t * x1t, axis=-1, keepdims=True) + EPS)
        h = (n * gain) * (1.0 + sc) + sh
        return [x1t, h, h], []
    (x1, hm0, hm0_t), _ = _rowwise(glu_gate_and_norm, [(vg, D, 0), (vg, D, 1), (x0, D, 0)], [mod0[2], mod0[3], mod0[4]],
                                   [mlp_norm_g[0:1]], [(D, F32), (D, BF16), (D, BF16, True)], [], n_ex=E, name="glu_gate_norm")
    x2, saved_mlp0 = mlp_fwd(x1, 0, mod0, normed=(hm0, hm0_t))

    hkv, hkv_t, h1, h1_t = _norm_mod_fwd_pair(x2, (kv_norm_g.reshape(1, D), kv_sh, kv_sc), (mix_norm_g[1:2], mod1[0], mod1[1]),
                                              n_ex=E, name="kv_mix_norm")
    kvf = _mm(hkv, Wkv, "nn", name="kv_proj", tm=tm)
    qf = _mm(h1, Wq, "nn", name="q_proj", tm=tm)
    qg2 = jnp.tile(q_norm_g.reshape(1, HEAD_DIM), (1, 2))
    kg2 = jnp.tile(k_norm_g.reshape(1, HEAD_DIM), (1, 2))
    o, lf_tot, o_t, wfull_b = _attn_fwd(qf, kvf, qg2, kg2, wpack_b.reshape(2, RB // 2, D), n_ex=E, name="attn_fwd")
    wfull_b = _sibling_fill(wfull_b, axis=1, name="wgather_b_d2d").reshape(4, RB, D)
    W1[1] = _Sharded(wfull_b, "cols", 0, D, 0, D)
    W2[1] = _Sharded(wfull_b, "rows", D, D, 0, D)
    Wo = _Sharded(wfull_b, "rows", 2 * D, D // 4, 0, D)
    x3, mix1 = _mm(o, Wo, "nn", name="o_proj", out_dtypes=(F32, F32), tm=tm_res,
                   extras=[_mn_extra(x2), _vec_extra(mod1[2], S)],
                   epilogue=lambda acc, xat, gt: (xat + gt * acc, acc))
    x4, saved_mlp1 = mlp_fwd(x3, 1, mod1)

    def loss_fn(xt, tt, fft, gmt):
        dx = (xt - tt) * (1.0 / D)
        dff, dgm = _gated(dx, fft, gmt)
        return [dx, dff], [_csum(jnp.square(xt - tt)) * (0.5 / D), dgm]
    (dx4, dff1), (lsum, dgm1) = _rowwise(loss_fn, [(x4, D, 0), (tgt, D, 0), (saved_mlp1[3], D, 0)], [mod1[5]], [],
                                         [(D, F32), (D, BF16)], [D, D], n_ex=E, name="loss")
    loss = lax.psum(jnp.sum(lsum), ("x", "y", "c"))

    (dx3, dmix1), (dsh_m1, dsc_m1, dg_mlp1, dga1) = mlp_bwd(dx4, dff1, x3, 1, mod1, saved_mlp1, gated=([(mix1, D, 0)], mod1[2], _gated, D))
    do = _mm(dmix1, Wo, "nt", name="o_proj_dx", tm=tm)
    grad_mm(o_t, dmix1, "rows", 5 * D + D // 4, D // 4, 0, D, "o_proj_dw", transposed=True)
    dq, dk, dv, dqg, dkg = _attn_bwd(qf, kvf, lf_tot, do, qg2, kg2, n_ex=E, name="attn_bwd")
    dh1 = _mm(dq, Wq, "nt", name="q_proj_dx", tm=tm)
    grad_mm(h1_t, dq, "rows", 5 * D, D // 4, 0, D, "q_proj_dw", transposed=True)
    dkv = jnp.concatenate([dk, dv], axis=1)
    dhkv = _mm(dkv, Wkv, "nt", name="kv_proj_dx", tm=tm)
    grad_mm(hkv_t, dkv, "cols", 4 * D, D, D // 2, D // 2, "kv_proj_dw", transposed=True)
    (dx2, dff0), (dsh_a1, dsc_a1, dg_mix1, dkv_sh, dkv_sc, dg_kv, dgm0) = _norm_mod_bwd_pair(
        x2, dh1, dhkv, dx3, (mix_norm_g[1:2], mod1[1]), (kv_norm_g.reshape(1, D), kv_sc),
        ([(saved_mlp0[3], D, 0)], mod0[5], _gated, D), n_ex=E, name="kv_mix_norm_bwd")

    def glu_bwd(d, v, g, ga):
        sg = _sigmoid(g)
        dm = ga * d
        return jnp.concatenate([dm * sg, dm * v * sg * (1.0 - sg)], axis=1).astype(BF16), _csum(d * (v * sg))
    (dx1, dvg), (dsh_m0, dsc_m0, dg_mlp0, dga0) = mlp_bwd(dx2, dff0, x1, 0, mod0, saved_mlp0,
                                                          gated=([(vg, D, 0), (vg, D, 1)], mod0[2], glu_bwd, 2 * D))
    dgy = _mm(dvg, Wglu, "nt", name="glu_up_dx", tm=tm)
    grad_mm(gy_t, dvg, "cols", 4 * D, D, 0, D // 2, "glu_up_dw", transposed=True)

    gpack = gbuf[0].reshape(4, 2, RW // 2, D)
    theirs = _sibling_swap_half(gpack, name="gscatter_d2d")
    chip_sum = _add_my_half(gpack, theirs, cidx, name="gscatter_add")
    dh0, dWb, dWc, dab, dd, from_chips = _s5_bwd(h0, y, dgy, s5_states, Wb, Wc, cr, d_full, chip_sum, n_ex=E, name="s5_bwd")
    ghalf = _sum4_into_half(from_chips, cidx, name="gscatter_sum")
    gsh = _sibling_fill(ghalf, axis=0, name="gscatter_fill").reshape(RW, D)
    (gx,), (dsh_a0, dsc_a0, dg_mix0) = _norm_mod_bwd(x0, dh0, dx1, mix_norm_g[0:1], mod0[1], n_ex=E, name="mix_norm_bwd0")
    grad_x = gx.reshape(E, S, D)

    dm_mine = jnp.concatenate([t.reshape(E, D) for t in
                               (dsh_a0, dsc_a0, dga0, dsh_m0, dsc_m0, dgm0, dsh_a1, dsc_a1, dga1, dsh_m1, dsc_m1, dgm1, dkv_sh, dkv_sc)], axis=1)
    dm_all = _all_gather8(dm_mine.reshape(8, -1), name="ag_dm").reshape(NB, 14 * D)
    sc_f32 = c_all * _sigmoid(c_all)
    g_ada_w = jax.ShapeDtypeStruct(ada_w.shape, F32)
    for l in range(2):
        g_ada_w = _mm(sc_f32, lax.dynamic_slice_in_dim(dm_all, l * 6 * D + chip * wa, wa, axis=1), "tn", name=f"ada_dw{l}", tn=256,
                      into=_Layer(g_ada_w, l))
    g_kv_ada_w = _mm(sc_f32, lax.dynamic_slice_in_dim(dm_all, 12 * D + chip * wk, wk, axis=1), "tn", name="ada_kv_dw", tn=256)
    db_all = _colsum(dm_all, name="ada_db")
    g_ada_b = db_all[0, :12 * D].reshape(2, 6 * D)
    g_kv_ada_b = db_all[0, 12 * D:]

    dWb_re, dWb_im, dC_re, dC_im = _s5_unblock(dWb, dWc)
    small_parts = [dg_mix0.sum(0), dg_mix1.sum(0), dg_mlp0.sum(0), dg_mlp1.sum(0), dg_kv.sum(0),
                   dqg.sum((0, 1, 2)).reshape(2, HEAD_DIM).sum(0), dkg.sum((0, 1, 2)).reshape(2, HEAD_DIM).sum(0),
                   dd[:, 0, :], dab[:, 0, :], dab[:, 1, :], dWb_re, dWb_im, dC_re, dC_im]
    spack, spans = _pack_rows(small_parts, row_multiple=32)
    chip_half = _sibling_sum_half(spack, name="small_d2d")
    ssum = _sum_blocks(_all_gather8(chip_half, name="ag_small"), 4, name="sum_small")
    (g_mix0, g_mix1, g_mlp0, g_mlp1, g_kvn, g_qn, g_kn, g_d, g_abr, g_abi, g_bbr, g_bbi, g_cre, g_cim) = _unpack_rows(ssum, spans)
    _, disc_vjp = jax.vjp(_s5_disc, s5_a_re[0], s5_a_im[0], s5_log_dt[0], s5_b_re[0], s5_b_im[0])
    g_are, g_aim, g_ldt, g_bre, g_bim = disc_vjp((g_abr.reshape(ab_re.shape), g_abi.reshape(ab_im.shape), g_bbr, g_bbi))
    g_s5d = lax.dynamic_slice_in_dim(g_d.reshape(1, D), chip * s5_d.shape[1], s5_d.shape[1], axis=1)

    def upd_big(w, m, v, roff, cb, name):
        shape = w.shape
        W = shape[-1]
        d_, m_, v_, g_ = _adamw2d(w.reshape(-1, W), gsh, m.reshape(-1, W), v.reshape(-1, W), name=name, g_roff=roff, g_cb=cb)
        return [t.reshape(shape) for t in (g_, d_, m_, v_)]

    def upd_own(w, g, m, v, name):
        shape = w.shape
        W = shape[-1]
        d_, m_, v_, g_ = _adamw2d(w.reshape(-1, W), g.reshape(-1, W), m.reshape(-1, W), v.reshape(-1, W), name=name)
        return [t.reshape(shape) for t in (g_, d_, m_, v_)]

    res = {}
    res["ada_w"] = upd_own(ada_w, g_ada_w, m_ada_w, v_ada_w, "adam_ada_w")
    res["kv_ada_w"] = upd_own(kv_ada_w, g_kv_ada_w, m_kv_ada_w, v_kv_ada_w, "adam_kv_ada_w")
    res["mlp_w1"] = upd_big(mlp_w1, m_mlp_w1, v_mlp_w1, 0, 0, "adam_w1")
    res["mlp_w2"] = upd_big(mlp_w2, m_mlp_w2, v_mlp_w2, 2 * D, 0, "adam_w2")
    res["s5_w_glu"] = upd_big(s5_w_glu, m_s5_w_glu, v_s5_w_glu, 4 * D, 0, "adam_glu")
    res["w_kv"] = upd_big(w_kv, m_w_kv, v_w_kv, 4 * D, 1, "adam_wkv")
    res["sb_w_q"] = upd_big(sb_w_q, m_sb_w_q, v_sb_w_q, 5 * D, 0, "adam_wq")
    res["sb_w_o"] = upd_big(sb_w_o, m_sb_w_o, v_sb_w_o, 5 * D + D // 4, 0, "adam_wo")

    small = {
        "ada_b": (ada_b, g_ada_b, m_ada_b, v_ada_b),
        "mix_norm_g": (mix_norm_g, jnp.stack([g_mix0, g_mix1]), m_mix_norm_g, v_mix_norm_g),
        "mlp_norm_g": (mlp_norm_g, jnp.stack([g_mlp0, g_mlp1]), m_mlp_norm_g, v_mlp_norm_g),
        "s5_a_re": (s5_a_re, g_are[None], m_s5_a_re, v_s5_a_re),
        "s5_a_im": (s5_a_im, g_aim[None], m_s5_a_im, v_s5_a_im),
        "s5_log_dt": (s5_log_dt, g_ldt[None], m_s5_log_dt, v_s5_log_dt),
        "s5_b_re": (s5_b_re, g_bre[None], m_s5_b_re, v_s5_b_re),
        "s5_b_im": (s5_b_im, g_bim[None], m_s5_b_im, v_s5_b_im),
        "s5_c_re": (s5_c_re, g_cre[None], m_s5_c_re, v_s5_c_re),
        "s5_c_im": (s5_c_im, g_cim[None], m_s5_c_im, v_s5_c_im),
        "s5_d": (s5_d, g_s5d, m_s5_d, v_s5_d),
        "kv_ada_b": (kv_ada_b, g_kv_ada_b, m_kv_ada_b, v_kv_ada_b),
        "kv_norm_g": (kv_norm_g, g_kvn, m_kv_norm_g, v_kv_norm_g),
        "k_norm_g": (k_norm_g, g_kn, m_k_norm_g, v_k_norm_g),
        "q_norm_g": (q_norm_g, g_qn.reshape(q_norm_g.shape), m_q_norm_g, v_q_norm_g),
    }
    names = list(small)
    packs = [_pack_rows([small[n][i].reshape(small[n][0].shape) for n in names]) for i in range(4)]
    sp = packs[0][1]
    d_, m_, v_, g_ = _adamw2d(packs[0][0], packs[1][0], packs[2][0], packs[3][0], name="adam_small")
    for n, gg, dd_, mm_, vv_ in zip(names, _unpack_rows(g_, sp), _unpack_rows(d_, sp), _unpack_rows(m_, sp), _unpack_rows(v_, sp)):
        res[n] = [gg, dd_, mm_, vv_]

    order = ["ada_w", "ada_b", "mix_norm_g", "mlp_norm_g", "mlp_w1", "mlp_w2", "s5_a_re", "s5_a_im", "s5_log_dt", "s5_b_re", "s5_b_im",
             "s5_c_re", "s5_c_im", "s5_d", "s5_w_glu", "kv_ada_w", "kv_ada_b", "kv_norm_g", "w_kv", "k_norm_g", "sb_w_q", "q_norm_g", "sb_w_o"]
    return (loss, grad_x, *[res[n][0] for n in order], *[res[n][1] for n in order], *[res[n][2] for n in order], *[res[n][3] for n in order])
```
